```python
import jax
import jax.numpy as jnp
from jax import lax
import numpy as np

D_MODEL = 1024
BATCH = 8
SEQ = 8192
DEPTH = 1

ATT_GROUPS = ((128, 1), (512, 4), (2048, 16))
ATT_HEADS_PER_GROUP = 4
ATT_HEADS = ATT_HEADS_PER_GROUP * len(ATT_GROUPS)
ATT_HEAD_DIM = 128
ATT_BLOCK = 128
ATT_W = ATT_HEADS * ATT_HEAD_DIM
ATT_OUT_W = ATT_HEADS_PER_GROUP * ATT_HEAD_DIM
RET_HEADS = 4
RET_QK_DIM = D_MODEL // RET_HEADS
RET_V_DIM = 2 * D_MODEL // RET_HEADS
RET_QK_W = RET_HEADS * RET_QK_DIM
RET_V_W = RET_HEADS * RET_V_DIM
RET_CHUNK = 128
D_FF = 4 * D_MODEL
IN_SPLITS = (ATT_W, ATT_W, ATT_W, RET_QK_W, RET_QK_W, RET_V_W, RET_V_W, D_MODEL, D_MODEL)
IN_W = sum(IN_SPLITS)
EPS = 1e-6

kernel_name = 'hybrid_dilated_attention_retention_block'


def rmsnorm(x, g):
    xf = x.astype(jnp.float32)
    y = xf * lax.rsqrt(jnp.mean(xf * xf, axis=-1, keepdims=True) + EPS)
    return (y * g.astype(jnp.float32)).astype(x.dtype)


def alibi_slopes(n_heads):
    return jnp.asarray(2.0 ** (-8.0 * np.arange(1, n_heads + 1, dtype=np.float32) / n_heads), dtype=jnp.float32)


def dilated_group(q, k, v, slopes, window, dilation):
    B, S, H, Dh = q.shape
    span = dilation * ATT_BLOCK
    s_pad = -(-S // span) * span
    L = s_pad // dilation
    nb = L // ATT_BLOCK
    reach = window // dilation

    def to_blocks(t):
        t = jnp.pad(t, ((0, 0), (0, s_pad - S), (0, 0), (0, 0)))
        t = t.reshape(B, L, dilation, H, Dh).transpose(0, 2, 3, 1, 4)
        return t.reshape(B, dilation, H, nb, ATT_BLOCK, Dh)

    def with_prev(t):
        prev = jnp.pad(t, ((0, 0), (0, 0), (0, 0), (1, 0), (0, 0), (0, 0)))[:, :, :, :-1]
        return jnp.concatenate([prev, t], axis=4)

    qb = to_blocks(q)
    kb = with_prev(to_blocks(k))
    vb = with_prev(to_blocks(v))
    s = jnp.einsum('brhnqd,brhnkd->brhnqk', qb, kb, preferred_element_type=jnp.float32) * (Dh ** -0.5)
    qi = jnp.arange(ATT_BLOCK)[:, None]
    kj = jnp.arange(2 * ATT_BLOCK)[None, :]
    dist = ATT_BLOCK + qi - kj
    blk = jnp.arange(nb)[:, None, None]
    valid = (dist >= 0) & (dist <= reach) & ((blk - 1) * ATT_BLOCK + kj >= 0)
    bias = -slopes[:, None, None] * (dist * dilation).astype(jnp.float32)
    s = jnp.where(valid[None, None, None], s + bias[None, None, :, None], -jnp.inf)
    m = jnp.max(s, axis=-1)
    p = jnp.exp(s - m[..., None])
    den = jnp.sum(p, axis=-1)
    o = jnp.einsum('brhnqk,brhnkd->brhnqd', p, vb.astype(jnp.float32)) / den[..., None]
    lse = m + jnp.log(den)
    o = o.reshape(B, dilation, H, L, Dh).transpose(0, 3, 1, 2, 4).reshape(B, s_pad, H, Dh)[:, :S]
    lse = lse.reshape(B, dilation, H, L).transpose(0, 3, 1, 2).reshape(B, s_pad, H)[:, :S]
    return o, lse


def retention(q, k, v):
    B, S, H, dk = q.shape
    dv = v.shape[-1]
    C = RET_CHUNK
    N = S // C
    log_g = jnp.log(1.0 - 2.0 ** (-5.0 - jnp.arange(H, dtype=jnp.float32)))
    idx = jnp.arange(C, dtype=jnp.float32)
    diff = idx[:, None] - idx[None, :]
    decay = jnp.where(diff >= 0, jnp.exp(log_g[:, None, None] * jnp.maximum(diff, 0.0)), 0.0)
    xi = jnp.exp(log_g[None, :] * (idx[:, None] + 1.0))
    zeta = jnp.exp(log_g[None, :] * (C - 1.0 - idx[:, None]))
    g_chunk = jnp.exp(log_g * C)
    qc = q.astype(jnp.float32).reshape(B, N, C, H, dk)
    kc = (k.astype(jnp.float32) * (dk ** -0.5)).reshape(B, N, C, H, dk)
    vc = v.astype(jnp.float32).reshape(B, N, C, H, dv)
    s = jnp.einsum('bnqhd,bnkhd->bnhqk', qc, kc) * decay[None, None]
    inner = jnp.einsum('bnhqk,bnkhv->bnqhv', s, vc)
    kz = kc * zeta[None, None, :, :, None]

    def step(state, xs):
        q_i, kz_i, v_i = xs
        cross = jnp.einsum('bqhd,bhdv->bqhv', q_i, state)
        state = state * g_chunk[None, :, None, None] + jnp.einsum('bkhd,bkhv->bhdv', kz_i, v_i)
        return state, cross

    state0 = jnp.zeros((B, H, dk, dv), jnp.float32)
    _, cross = lax.scan(step, state0, (qc.transpose(1, 0, 2, 3, 4), kz.transpose(1, 0, 2, 3, 4), vc.transpose(1, 0, 2, 3, 4)))
    cross = cross.transpose(1, 0, 2, 3, 4) * xi[None, None, :, :, None]
    return (inner + cross).reshape(B, S, H, dv)


def head_groupnorm(o, g, b):
    B, S, H, dv = o.shape
    mu = jnp.mean(o, axis=-1, keepdims=True)
    var = jnp.mean(jnp.square(o - mu), axis=-1, keepdims=True)
    y = ((o - mu) * lax.rsqrt(var + EPS)).reshape(B, S, H * dv)
    return y * g.astype(jnp.float32) + b.astype(jnp.float32)


def _fwd_setup_inputs(seed: int = 0) -> dict:
    key = jax.random.key(seed)
    ks = jax.random.split(key, 14)

    def nrm(k, shape, scale):
        return scale * jax.random.normal(k, shape, jnp.float32)

    return {
        'x': nrm(ks[0], (BATCH, SEQ, D_MODEL), 1.0),
        'norm1_g': 1.0 + nrm(ks[1], (DEPTH, D_MODEL), 0.02),
        'w_in': nrm(ks[2], (DEPTH, D_MODEL, IN_W), D_MODEL ** -0.5),
        'q_norm_g': 1.0 + nrm(ks[3], (DEPTH, ATT_HEADS, ATT_HEAD_DIM), 0.02),
        'k_norm_g': 1.0 + nrm(ks[4], (DEPTH, ATT_HEADS, ATT_HEAD_DIM), 0.02),
        'ret_gn_g': 1.0 + nrm(ks[5], (DEPTH, RET_V_W), 0.02),
        'ret_gn_b': nrm(ks[6], (DEPTH, RET_V_W), 0.02),
        'w_proj_a': nrm(ks[7], (DEPTH, ATT_OUT_W, D_MODEL), ATT_OUT_W ** -0.5),
        'w_proj_b': nrm(ks[8], (DEPTH, RET_V_W, D_MODEL), RET_V_W ** -0.5),
        'w_out': nrm(ks[9], (DEPTH, D_MODEL, D_MODEL), D_MODEL ** -0.5),
        'norm2_g': 1.0 + nrm(ks[10], (DEPTH, D_MODEL), 0.02),
        'w_up': nrm(ks[11], (DEPTH, D_MODEL, D_FF), D_MODEL ** -0.5),
        'w_down': nrm(ks[12], (DEPTH, D_FF, D_MODEL), D_FF ** -0.5),
    }


def _fwd_reference(x, norm1_g, w_in, q_norm_g, k_norm_g, ret_gn_g, ret_gn_b, w_proj_a, w_proj_b, w_out, norm2_g, w_up, w_down):
    B, S, _ = x.shape
    slopes = alibi_slopes(ATT_HEADS)
    bounds = np.cumsum((0,) + IN_SPLITS).tolist()
    for l in range(DEPTH):
        xn = rmsnorm(x, norm1_g[l])
        wl = w_in[l]
        qa, ka, va, qr, kr, vr, gr, gate_a, gate_b = [xn @ wl[:, bounds[i]:bounds[i + 1]] for i in range(len(IN_SPLITS))]
        qa = rmsnorm(qa.reshape(B, S, ATT_HEADS, ATT_HEAD_DIM), q_norm_g[l])
        ka = rmsnorm(ka.reshape(B, S, ATT_HEADS, ATT_HEAD_DIM), k_norm_g[l])
        va = va.reshape(B, S, ATT_HEADS, ATT_HEAD_DIM)
        outs, lses = [], []
        for gi, (window, dilation) in enumerate(ATT_GROUPS):
            hs = slice(gi * ATT_HEADS_PER_GROUP, (gi + 1) * ATT_HEADS_PER_GROUP)
            o, lse = dilated_group(qa[:, :, hs], ka[:, :, hs], va[:, :, hs], slopes[hs], window, dilation)
            outs.append(o)
            lses.append(lse)
        alpha = jax.nn.softmax(jnp.stack(lses, axis=0), axis=0)
        o_a = jnp.sum(alpha[..., None] * jnp.stack(outs, axis=0), axis=0).reshape(B, S, ATT_OUT_W).astype(x.dtype)
        o_r = retention(qr.reshape(B, S, RET_HEADS, RET_QK_DIM), kr.reshape(B, S, RET_HEADS, RET_QK_DIM),
                        vr.reshape(B, S, RET_HEADS, RET_V_DIM))
        o_r = (head_groupnorm(o_r, ret_gn_g[l], ret_gn_b[l]) * jax.nn.silu(gr.astype(jnp.float32))).astype(x.dtype)
        y = jax.nn.sigmoid(gate_a) * (o_a @ w_proj_a[l]) + jax.nn.sigmoid(gate_b) * (o_r @ w_proj_b[l])
        x = x + y @ w_out[l]
        xn2 = rmsnorm(x, norm2_g[l])
        x = x + jnp.square(jax.nn.relu(xn2 @ w_up[l])) @ w_down[l]
    return x


import jax as _jax
import jax.numpy as _jnp

TWIN_FORMAT = 'train_step'
FWD_PARAMS = ['x', 'norm1_g', 'w_in', 'q_norm_g', 'k_norm_g', 'ret_gn_g', 'ret_gn_b', 'w_proj_a', 'w_proj_b', 'w_out', 'norm2_g', 'w_up', 'w_down']
TWIN_WEIGHTS = ['norm1_g', 'w_in', 'q_norm_g', 'k_norm_g', 'ret_gn_g', 'ret_gn_b', 'w_proj_a', 'w_proj_b', 'w_out', 'norm2_g', 'w_up', 'w_down']
TWIN_DIFF_INPUT = 'x'
TWIN_INPUTS = ['x', 'norm1_g', 'w_in', 'q_norm_g', 'k_norm_g', 'ret_gn_g', 'ret_gn_b', 'w_proj_a', 'w_proj_b', 'w_out', 'norm2_g', 'w_up', 'w_down', 'loss_target', 'm_norm1_g', 'm_w_in', 'm_q_norm_g', 'm_k_norm_g', 'm_ret_gn_g', 'm_ret_gn_b', 'm_w_proj_a', 'm_w_proj_b', 'm_w_out', 'm_norm2_g', 'm_w_up', 'm_w_down', 'v_norm1_g', 'v_w_in', 'v_q_norm_g', 'v_k_norm_g', 'v_ret_gn_g', 'v_ret_gn_b', 'v_w_proj_a', 'v_w_proj_b', 'v_w_out', 'v_norm2_g', 'v_w_up', 'v_w_down']
TWIN_OUTPUTS = ['loss', 'grad_x', 'grad_norm1_g', 'grad_w_in', 'grad_q_norm_g', 'grad_k_norm_g', 'grad_ret_gn_g', 'grad_ret_gn_b', 'grad_w_proj_a', 'grad_w_proj_b', 'grad_w_out', 'grad_norm2_g', 'grad_w_up', 'grad_w_down', 'delta_norm1_g', 'delta_w_in', 'delta_q_norm_g', 'delta_k_norm_g', 'delta_ret_gn_g', 'delta_ret_gn_b', 'delta_w_proj_a', 'delta_w_proj_b', 'delta_w_out', 'delta_norm2_g', 'delta_w_up', 'delta_w_down', 'new_m_norm1_g', 'new_m_w_in', 'new_m_q_norm_g', 'new_m_k_norm_g', 'new_m_ret_gn_g', 'new_m_ret_gn_b', 'new_m_w_proj_a', 'new_m_w_proj_b', 'new_m_w_out', 'new_m_norm2_g', 'new_m_w_up', 'new_m_w_down', 'new_v_norm1_g', 'new_v_w_in', 'new_v_q_norm_g', 'new_v_k_norm_g', 'new_v_ret_gn_g', 'new_v_ret_gn_b', 'new_v_w_proj_a', 'new_v_w_proj_b', 'new_v_w_out', 'new_v_norm2_g', 'new_v_w_up', 'new_v_w_down']
TWIN_LEAF_KINDS = {'loss': 'loss', 'grad_x': 'grad_x', 'grad_norm1_g': 'grad_w', 'grad_w_in': 'grad_w', 'grad_q_norm_g': 'grad_w', 'grad_k_norm_g': 'grad_w', 'grad_ret_gn_g': 'grad_w', 'grad_ret_gn_b': 'grad_w', 'grad_w_proj_a': 'grad_w', 'grad_w_proj_b': 'grad_w', 'grad_w_out': 'grad_w', 'grad_norm2_g': 'grad_w', 'grad_w_up': 'grad_w', 'grad_w_down': 'grad_w', 'delta_norm1_g': 'delta_w', 'delta_w_in': 'delta_w', 'delta_q_norm_g': 'delta_w', 'delta_k_norm_g': 'delta_w', 'delta_ret_gn_g': 'delta_w', 'delta_ret_gn_b': 'delta_w', 'delta_w_proj_a': 'delta_w', 'delta_w_proj_b': 'delta_w', 'delta_w_out': 'delta_w', 'delta_norm2_g': 'delta_w', 'delta_w_up': 'delta_w', 'delta_w_down': 'delta_w', 'new_m_norm1_g': 'new_m', 'new_m_w_in': 'new_m', 'new_m_q_norm_g': 'new_m', 'new_m_k_norm_g': 'new_m', 'new_m_ret_gn_g': 'new_m', 'new_m_ret_gn_b': 'new_m', 'new_m_w_proj_a': 'new_m', 'new_m_w_proj_b': 'new_m', 'new_m_w_out': 'new_m', 'new_m_norm2_g': 'new_m', 'new_m_w_up': 'new_m', 'new_m_w_down': 'new_m', 'new_v_norm1_g': 'new_v', 'new_v_w_in': 'new_v', 'new_v_q_norm_g': 'new_v', 'new_v_k_norm_g': 'new_v', 'new_v_ret_gn_g': 'new_v', 'new_v_ret_gn_b': 'new_v', 'new_v_w_proj_a': 'new_v', 'new_v_w_proj_b': 'new_v', 'new_v_w_out': 'new_v', 'new_v_norm2_g': 'new_v', 'new_v_w_up': 'new_v', 'new_v_w_down': 'new_v'}


def _forward(args):
    return _fwd_reference(*[args[k] for k in FWD_PARAMS])


def _output_shape():
    def fwd():
        inp = _fwd_setup_inputs(0)
        return _fwd_reference(*[inp[k] for k in FWD_PARAMS])
    out = _jax.eval_shape(fwd)
    return out.shape, out.dtype

N_MICROBATCH = 1
ADAM_LR = 0.001
ADAM_B1 = 0.9
ADAM_B2 = 0.999
ADAM_EPS = 1e-08
ADAM_WD = 0.01
ADAM_STEP = 10
PER_EXAMPLE_BATCH_AXIS = {'x': 0, 'loss_target': 0}
SHARED_INPUTS = []
_WEIGHT_DTYPES = {'norm1_g': _jnp.float32, 'w_in': _jnp.float32, 'q_norm_g': _jnp.float32, 'k_norm_g': _jnp.float32, 'ret_gn_g': _jnp.float32, 'ret_gn_b': _jnp.float32, 'w_proj_a': _jnp.float32, 'w_proj_b': _jnp.float32, 'w_out': _jnp.float32, 'norm2_g': _jnp.float32, 'w_up': _jnp.float32, 'w_down': _jnp.float32}
MOMENT_SCALE = {'norm1_g': 8.948100e+00, 'w_in': 2.247789e-01, 'q_norm_g': 3.978857e-01, 'k_norm_g': 3.967346e-01, 'ret_gn_g': 3.094473e+00, 'ret_gn_b': 4.423788e+00, 'w_proj_a': 2.081769e-01, 'w_proj_b': 4.046254e-01, 'w_out': 5.404239e-01, 'norm2_g': 1.926925e+02, 'w_up': 1.421478e+00, 'w_down': 1.622033e+01}


def _to_microbatches(a, axis):
    t = _jnp.moveaxis(a, axis, 0)
    t = t.reshape((N_MICROBATCH, t.shape[0] // N_MICROBATCH) + t.shape[1:])
    return _jnp.moveaxis(t, 1, axis + 1)


def setup_inputs(seed: int = 0) -> dict:
    inp = _fwd_setup_inputs(seed)
    key = _jax.random.fold_in(_jax.random.key(seed), 7919)
    shape, _ = _output_shape()
    out = dict(inp)
    out["loss_target"] = _jax.random.normal(_jax.random.fold_in(key, 0), shape, _jnp.float32)
    for i, name in enumerate(TWIN_WEIGHTS):
        w = inp[name].astype(_jnp.float32)
        if MOMENT_SCALE is None:
            s = _jnp.sqrt(_jnp.mean(_jnp.square(w)) + 1e-30)
        else:
            s = MOMENT_SCALE[name]
        km, kv = _jax.random.split(_jax.random.fold_in(key, i + 1))
        out[name] = w
        out["m_" + name] = s * _jax.random.normal(km, w.shape, _jnp.float32)
        out["v_" + name] = (s * s) * _jax.random.uniform(kv, w.shape, _jnp.float32, 0.5, 1.5)
    if N_MICROBATCH > 1:
        for name, axis in PER_EXAMPLE_BATCH_AXIS.items():
            out[name] = _to_microbatches(out[name], axis)
    return {'x': out['x'], 'norm1_g': out['norm1_g'], 'w_in': out['w_in'], 'q_norm_g': out['q_norm_g'], 'k_norm_g': out['k_norm_g'], 'ret_gn_g': out['ret_gn_g'], 'ret_gn_b': out['ret_gn_b'], 'w_proj_a': out['w_proj_a'], 'w_proj_b': out['w_proj_b'], 'w_out': out['w_out'], 'norm2_g': out['norm2_g'], 'w_up': out['w_up'], 'w_down': out['w_down'], 'loss_target': out['loss_target'], 'm_norm1_g': out['m_norm1_g'], 'm_w_in': out['m_w_in'], 'm_q_norm_g': out['m_q_norm_g'], 'm_k_norm_g': out['m_k_norm_g'], 'm_ret_gn_g': out['m_ret_gn_g'], 'm_ret_gn_b': out['m_ret_gn_b'], 'm_w_proj_a': out['m_w_proj_a'], 'm_w_proj_b': out['m_w_proj_b'], 'm_w_out': out['m_w_out'], 'm_norm2_g': out['m_norm2_g'], 'm_w_up': out['m_w_up'], 'm_w_down': out['m_w_down'], 'v_norm1_g': out['v_norm1_g'], 'v_w_in': out['v_w_in'], 'v_q_norm_g': out['v_q_norm_g'], 'v_k_norm_g': out['v_k_norm_g'], 'v_ret_gn_g': out['v_ret_gn_g'], 'v_ret_gn_b': out['v_ret_gn_b'], 'v_w_proj_a': out['v_w_proj_a'], 'v_w_proj_b': out['v_w_proj_b'], 'v_w_out': out['v_w_out'], 'v_norm2_g': out['v_norm2_g'], 'v_w_up': out['v_w_up'], 'v_w_down': out['v_w_down']}


def _loss(weights, diff, rest, loss_target):
    with _jax.named_scope("forward"):
        args = {**rest, TWIN_DIFF_INPUT: diff, **{k: w.astype(_WEIGHT_DTYPES[k]) for k, w in weights.items()}}
        y = _forward(args)
    with _jax.named_scope("loss_head"):
        err = _jnp.square(y.astype(_jnp.float32) - loss_target)
        return 0.5 * _jnp.sum(_jnp.mean(err, axis=-1)) if err.ndim else 0.5 * err


def _adamw(w, g, m, v):
    m = ADAM_B1 * m + (1.0 - ADAM_B1) * g
    v = ADAM_B2 * v + (1.0 - ADAM_B2) * _jnp.square(g)
    m_hat = m / (1.0 - ADAM_B1 ** ADAM_STEP)
    v_hat = v / (1.0 - ADAM_B2 ** ADAM_STEP)
    delta = -ADAM_LR * (m_hat / (_jnp.sqrt(v_hat) + ADAM_EPS) + ADAM_WD * w)
    return delta, m, v


def reference(x, norm1_g, w_in, q_norm_g, k_norm_g, ret_gn_g, ret_gn_b, w_proj_a, w_proj_b, w_out, norm2_g, w_up, w_down, loss_target, m_norm1_g, m_w_in, m_q_norm_g, m_k_norm_g, m_ret_gn_g, m_ret_gn_b, m_w_proj_a, m_w_proj_b, m_w_out, m_norm2_g, m_w_up, m_w_down, v_norm1_g, v_w_in, v_q_norm_g, v_k_norm_g, v_ret_gn_g, v_ret_gn_b, v_w_proj_a, v_w_proj_b, v_w_out, v_norm2_g, v_w_up, v_w_down):
    given = dict(x=x, norm1_g=norm1_g, w_in=w_in, q_norm_g=q_norm_g, k_norm_g=k_norm_g, ret_gn_g=ret_gn_g, ret_gn_b=ret_gn_b, w_proj_a=w_proj_a, w_proj_b=w_proj_b, w_out=w_out, norm2_g=norm2_g, w_up=w_up, w_down=w_down, loss_target=loss_target, m_norm1_g=m_norm1_g, m_w_in=m_w_in, m_q_norm_g=m_q_norm_g, m_k_norm_g=m_k_norm_g, m_ret_gn_g=m_ret_gn_g, m_ret_gn_b=m_ret_gn_b, m_w_proj_a=m_w_proj_a, m_w_proj_b=m_w_proj_b, m_w_out=m_w_out, m_norm2_g=m_norm2_g, m_w_up=m_w_up, m_w_down=m_w_down, v_norm1_g=v_norm1_g, v_w_in=v_w_in, v_q_norm_g=v_q_norm_g, v_k_norm_g=v_k_norm_g, v_ret_gn_g=v_ret_gn_g, v_ret_gn_b=v_ret_gn_b, v_w_proj_a=v_w_proj_a, v_w_proj_b=v_w_proj_b, v_w_out=v_w_out, v_norm2_g=v_norm2_g, v_w_up=v_w_up, v_w_down=v_w_down)
    weights = {n: given[n] for n in TWIN_WEIGHTS}
    shared = {n: given[n] for n in SHARED_INPUTS}
    per_example = {n: given[n] for n in ['x']}
    grad_fn = _jax.value_and_grad(_loss, argnums=(0, 1))

    def one_microbatch(ex, loss_target):
        ex = dict(ex)
        diff = ex.pop(TWIN_DIFF_INPUT)
        return grad_fn(weights, diff, {**shared, **ex}, loss_target)

    if N_MICROBATCH == 1:
        loss, (grad_w, grad_x) = one_microbatch(per_example, given["loss_target"])
    else:
        def body(carry, xs):
            loss_sum, grad_sum = carry
            l_k, (gw_k, gx_k) = one_microbatch(xs[0], xs[1])
            with _jax.named_scope("update"):
                return (loss_sum + l_k, _jax.tree.map(_jnp.add, grad_sum, gw_k)), gx_k

        init = (_jnp.zeros((), _jnp.float32), _jax.tree.map(_jnp.zeros_like, weights))
        (loss, grad_w), grad_x = _jax.lax.scan(body, init, (per_example, given["loss_target"]))
    with _jax.named_scope("update"):
        delta_w, new_m, new_v = {}, {}, {}
        for n in TWIN_WEIGHTS:
            delta_w[n], new_m[n], new_v[n] = _adamw(weights[n], grad_w[n], given["m_" + n], given["v_" + n])
    return (loss, grad_x, *[grad_w[n] for n in TWIN_WEIGHTS], *[delta_w[n] for n in TWIN_WEIGHTS],
            *[new_m[n] for n in TWIN_WEIGHTS], *[new_v[n] for n in TWIN_WEIGHTS])
```

```python
import math

import numpy as np
import jax
import jax.numpy as jnp
from jax import lax
from jax.experimental import pallas as pl
from jax.experimental.pallas import tpu as pltpu

F32 = jnp.float32
BF16 = jnp.bfloat16

D_MODEL = 1024
ATT_GROUPS = ((128, 1), (512, 4), (2048, 16))
HPG = 4
ATT_HEADS = 12
DH = 128
BLK = 128
ATT_W = ATT_HEADS * DH
ATT_OUT_W = HPG * DH
RET_HEADS = 4
RET_QK = 256
RET_V = 512
RET_QK_W = RET_HEADS * RET_QK
RET_V_W = RET_HEADS * RET_V
CHUNK = 128
D_FF = 4096
IN_W = 12800
REST_W = IN_W - 3 * ATT_W
RET_COLS = 6144
EPS = 1e-6
ADAM_LR, ADAM_B1, ADAM_B2, ADAM_EPS, ADAM_WD, ADAM_STEP = 0.001, 0.9, 0.999, 1e-08, 0.01, 10
N_DEV = 8
N_CHIP = 4
MESH_AXES = ("x", "y", "c")
MESH = pl.DeviceIdType.MESH
VMEM_LIMIT = 56 * 1024 * 1024
LANES = 128
NEG = -1e30

_NN = (((1,), (0,)), ((), ()))
_NT = (((1,), (1,)), ((), ()))
_TN = (((0,), (0,)), ((), ()))

R_Q, R_K, R_V, R_G, R_GA, R_GB = 0, 1024, 2048, 4096, 6144, 7168

LOG_GAMMA = [float(v) for v in np.log(1.0 - 2.0 ** (-5.0 - np.arange(RET_HEADS, dtype=np.float32))).astype(np.float32)]
ALIBI = np.asarray(2.0 ** (-8.0 * np.arange(1, ATT_HEADS + 1, dtype=np.float32) / ATT_HEADS), np.float32)

BIG_SHAPES = ((1024, 1600), (512, 128), (256, 1024), (128, 1024), (1024, 512), (512, 1024))
SMALL_SHAPES = ((1, 1024), (1, 12, 128), (1, 12, 128), (1, 2048), (1, 2048), (1, 1024))
BIG_ROWS = sum(a * b for a, b in BIG_SHAPES) // LANES
SMALL_ROWS = sum(int(np.prod(s)) for s in SMALL_SHAPES) // LANES
FLAT_ROWS = BIG_ROWS + SMALL_ROWS
FLAT_TILE = 1896


def _dot(a, b, dims=_NN):
    return lax.dot_general(a, b, dims, preferred_element_type=F32)


def _cparams(sem):
    return pltpu.CompilerParams(dimension_semantics=sem, vmem_limit_bytes=VMEM_LIMIT)


def _sds(shape, dtype):
    return jax.ShapeDtypeStruct(shape, dtype)


def _mm(name, a, b, mode, tm, tn, tk, outs, epi, extras=(), a_pro=None, b_pro=None,
        sem=("parallel", "parallel", "arbitrary")):
    if mode == "nn":
        (M, K), (_, N) = a.shape, b.shape
        a_spec = pl.BlockSpec((tm, tk), lambda i, j, k: (i, k))
        b_spec = pl.BlockSpec((tk, tn), lambda i, j, k: (k, j))
        dims = _NN
    elif mode == "nt":
        (M, K), (N, _) = a.shape, b.shape
        a_spec = pl.BlockSpec((tm, tk), lambda i, j, k: (i, k))
        b_spec = pl.BlockSpec((tn, tk), lambda i, j, k: (j, k))
        dims = _NT
    else:
        (K, M), (_, N) = a.shape, b.shape
        a_spec = pl.BlockSpec((tk, tm), lambda i, j, k: (k, i))
        b_spec = pl.BlockSpec((tk, tn), lambda i, j, k: (k, j))
        dims = _TN
    assert M % tm == 0 and N % tn == 0 and K % tk == 0, (name, M, N, K, tm, tn, tk)
    nk = K // tk
    n_ex, n_out = len(extras), len(outs)

    def body(a_ref, b_ref, *rest):
        ex, out = rest[:n_ex], rest[n_ex:n_ex + n_out]
        av, bv = a_ref[...], b_ref[...]
        if a_pro is not None:
            av = a_pro(av)
        if b_pro is not None:
            bv = b_pro(bv)
        part = _dot(av.astype(BF16), bv.astype(BF16), dims)
        if nk == 1:
            epi(part, ex, out)
        else:
            acc_ref = rest[n_ex + n_out]
            k = pl.program_id(2)

            @pl.when(k == 0)
            def _():
                acc_ref[...] = part

            @pl.when(k > 0)
            def _():
                acc_ref[...] += part

            @pl.when(k == nk - 1)
            def _():
                epi(acc_ref[...], ex, out)

    return pl.pallas_call(
        body,
        name=name,
        grid=(M // tm, N // tn, nk),
        in_specs=[a_spec, b_spec] + [s for _, s in extras],
        out_specs=[s for _, s in outs],
        out_shape=[o for o, _ in outs],
        scratch_shapes=[pltpu.VMEM((tm, tn), F32)] if nk > 1 else [],
        compiler_params=_cparams(sem),
    )(a, b, *[e for e, _ in extras])


def _tile_ij(tm, tn):
    return pl.BlockSpec((tm, tn), lambda i, j, k: (i, j))


def _epi_store(dtype):
    def epi(acc, ex, out):
        out[0][...] = acc.astype(dtype)
    return epi


def _rms_rows(x):
    return lax.rsqrt(jnp.mean(x * x, axis=-1, keepdims=True) + EPS)


def _acc_rows(ref, part, first):
    @pl.when(first)
    def _():
        ref[...] = part

    @pl.when(jnp.logical_not(first))
    def _():
        ref[...] += part


def _rmsnorm_fwd(x, g, tm):
    S, Dm = x.shape

    def body(x_ref, g_ref, o_ref):
        xv = x_ref[...]
        o_ref[...] = (xv * _rms_rows(xv) * g_ref[...]).astype(BF16)

    return pl.pallas_call(
        body, name="rmsnorm1_fwd", grid=(S // tm,),
        in_specs=[pl.BlockSpec((tm, Dm), lambda i: (i, 0)), pl.BlockSpec((1, Dm), lambda i: (0, 0))],
        out_specs=pl.BlockSpec((tm, Dm), lambda i: (i, 0)),
        out_shape=_sds((S, Dm), BF16),
        compiler_params=_cparams(("parallel",)),
    )(x, g)


def _rows(ref, r, d):
    if d == 1:
        return ref[...]
    return ref[pl.ds(r, BLK, stride=d), :]


def _put_rows(ref, r, d, val):
    if d == 1:
        ref[...] = val
    else:
        ref[pl.ds(r, BLK, stride=d), :] = val


def _head_norm(x, g):
    r = _rms_rows(x)
    xh = x * r
    return (xh * g).astype(BF16), xh, r


def _head_norm_bwd(dyn, xh, r, g):
    dxh = dyn * g
    dx = r * (dxh - xh * jnp.mean(dxh * xh, axis=-1, keepdims=True))
    return dx, jnp.sum(dyn * xh, axis=0, keepdims=True)


def _att_mask_bias(slope, d, first):
    qi = lax.broadcasted_iota(jnp.int32, (BLK, 2 * BLK), 0)
    kj = lax.broadcasted_iota(jnp.int32, (BLK, 2 * BLK), 1)
    dist = BLK + qi - kj
    valid = (dist >= 0) & (dist <= BLK) & (jnp.logical_not(first) | (kj >= BLK))
    bias = -slope * (dist * d).astype(F32)
    return valid, bias


def _att_specs(gi, d, nsb):
    blk = (BLK * d, DH)

    def at(kind, shift):
        def imap(j, n):
            return (jnp.clip(n + shift, 0, nsb - 1), 3 * (gi * HPG + j) + kind)
        return pl.BlockSpec(blk, imap)

    head = pl.BlockSpec((None, 1, DH), lambda j, n: (gi * HPG + j, 0, 0))
    slot = lambda shift: pl.BlockSpec(blk, lambda j, n: (jnp.clip(n + shift, 0, nsb - 1), j))
    return at, head, slot


def _att_fwd_group(z_att, gq3, gk3, slopes3, gi, d):
    S = z_att.shape[0]
    nsb = S // (BLK * d)
    scale = DH ** -0.5

    def body(q_ref, kp_ref, kc_ref, vp_ref, vc_ref, gq_ref, gk_ref, sl_ref, o_ref, l_ref):
        valid, bias = _att_mask_bias(sl_ref[...][:, :1], d, pl.program_id(1) == 0)
        for r in range(d):
            q, _, _ = _head_norm(_rows(q_ref, r, d), gq_ref[...])
            kp, _, _ = _head_norm(_rows(kp_ref, r, d), gk_ref[...])
            kc, _, _ = _head_norm(_rows(kc_ref, r, d), gk_ref[...])
            k = jnp.concatenate([kp, kc], axis=0)
            v = jnp.concatenate([_rows(vp_ref, r, d), _rows(vc_ref, r, d)], axis=0).astype(BF16)
            s = jnp.where(valid, _dot(q, k, _NT) * scale + bias, NEG)
            m = jnp.max(s, axis=-1, keepdims=True)
            p = jnp.exp(s - m)
            den = jnp.sum(p, axis=-1, keepdims=True)
            _put_rows(o_ref, r, d, _dot(p.astype(BF16), v) / den)
            _put_rows(l_ref, r, d, jnp.broadcast_to(m + jnp.log(den), (BLK, DH)))

    at, head, slot = _att_specs(gi, d, nsb)
    return pl.pallas_call(
        body, name=f"att_fwd_g{gi}", grid=(HPG, nsb),
        in_specs=[at(0, 0), at(1, -1), at(1, 0), at(2, -1), at(2, 0), head, head, head],
        out_specs=[slot(0), slot(0)],
        out_shape=[_sds((S, ATT_OUT_W), F32), _sds((S, ATT_OUT_W), F32)],
        compiler_params=_cparams(("parallel", "arbitrary")),
    )(z_att, z_att, z_att, z_att, z_att, gq3, gk3, slopes3)


def _att_combine(os_, ls_, tm):
    S = os_[0].shape[0]

    def body(o0, o1, o2, l0, l1, l2, oa_ref, lm_ref):
        a, b, c = l0[...], l1[...], l2[...]
        m = jnp.maximum(jnp.maximum(a, b), c)
        ea, eb, ec = jnp.exp(a - m), jnp.exp(b - m), jnp.exp(c - m)
        tot = ea + eb + ec
        oa_ref[...] = (ea * o0[...] + eb * o1[...] + ec * o2[...]) / tot
        lm_ref[...] = m + jnp.log(tot)

    spec = pl.BlockSpec((tm, ATT_OUT_W), lambda i: (i, 0))
    return pl.pallas_call(
        body, name="att_combine", grid=(S // tm,),
        in_specs=[spec] * 6, out_specs=[spec, spec],
        out_shape=[_sds((S, ATT_OUT_W), F32), _sds((S, ATT_OUT_W), F32)],
        compiler_params=_cparams(("parallel",)),
    )(*os_, *ls_)


def _ret_tables(lg):
    ri = lax.broadcasted_iota(jnp.int32, (CHUNK, CHUNK), 0)
    ci = lax.broadcasted_iota(jnp.int32, (CHUNK, CHUNK), 1)
    diff = (ri - ci).astype(F32)
    decay = jnp.where(diff >= 0, jnp.exp(lg * jnp.maximum(diff, 0.0)), 0.0)
    idx = lax.broadcasted_iota(jnp.int32, (CHUNK, 1), 0).astype(F32)
    xi = jnp.exp(lg * (idx + 1.0))
    zeta = jnp.exp(lg * (CHUNK - 1.0 - idx))
    return decay, xi, zeta, math.exp(lg * CHUNK)


def _ret_specs(nch, rev):
    idx = (lambda n: nch - 1 - n) if rev else (lambda n: n)
    qk = lambda off: pl.BlockSpec((CHUNK, RET_QK_W), lambda n: (idx(n), off // RET_QK_W))
    vv = lambda off: pl.BlockSpec((CHUNK, RET_V_W), lambda n: (idx(n), off // RET_V_W))
    par = pl.BlockSpec((1, RET_V_W), lambda n: (0, 0))
    wide = pl.BlockSpec((CHUNK, RET_V_W), lambda n: (idx(n), 0))
    st = pl.BlockSpec((RET_HEADS, None, RET_QK, RET_V), lambda n: (0, idx(n), 0, 0))
    return qk, vv, par, wide, st


def _ret_fwd(z_rest, gn_g, gn_b):
    S = z_rest.shape[0]
    nch = S // CHUNK

    def body(q_ref, k_ref, v_ref, gr_ref, g_ref, b_ref, or_ref, o_ref, st_ref, state):
        @pl.when(pl.program_id(0) == 0)
        def _():
            state[...] = jnp.zeros_like(state)

        for h in range(RET_HEADS):
            decay, xi, zeta, gch = _ret_tables(LOG_GAMMA[h])
            cq = slice(h * RET_QK, (h + 1) * RET_QK)
            cv = slice(h * RET_V, (h + 1) * RET_V)
            q = q_ref[:, cq]
            kc32 = k_ref[:, cq].astype(F32) * (RET_QK ** -0.5)
            kc = kc32.astype(BF16)
            v = v_ref[:, cv]
            st = state[h]
            stb = st.astype(BF16)
            st_ref[h] = stb
            s = _dot(q, kc, _NT) * decay
            o = _dot(s.astype(BF16), v) + _dot(q, stb) * xi
            state[h] = st * gch + _dot((kc32 * zeta).astype(BF16), v, _TN)
            mu = jnp.mean(o, axis=-1, keepdims=True)
            cen = o - mu
            yh = cen * lax.rsqrt(jnp.mean(cen * cen, axis=-1, keepdims=True) + EPS)
            gr = gr_ref[:, cv].astype(F32)
            or_ref[:, cv] = ((yh * g_ref[:, cv] + b_ref[:, cv]) * (gr * jax.nn.sigmoid(gr))).astype(BF16)
            o_ref[:, cv] = o

    qk, vv, par, wide, st = _ret_specs(nch, False)
    return pl.pallas_call(
        body, name="ret_fwd", grid=(nch,),
        in_specs=[qk(R_Q), qk(R_K), vv(R_V), vv(R_G), par, par],
        out_specs=[wide, wide, st],
        out_shape=[_sds((S, RET_V_W), BF16), _sds((S, RET_V_W), F32), _sds((RET_HEADS, nch, RET_QK, RET_V), BF16)],
        scratch_shapes=[pltpu.VMEM((RET_HEADS, RET_QK, RET_V), F32)],
        compiler_params=_cparams(("arbitrary",)),
    )(z_rest, z_rest, z_rest, z_rest, gn_g, gn_b)


def _merge_fwd(o_a, o_r, z_rest, x, wpa, wpb, wout, g2, tm):
    S = x.shape[0]

    def body(oa_ref, or_ref, ga_ref, gb_ref, x_ref, wpa_ref, wpb_ref, wo_ref, g2_ref,
             x1_ref, y_ref, pa_ref, pb_ref, xn2_ref):
        pa = _dot(oa_ref[...].astype(BF16), wpa_ref[...])
        pb = _dot(or_ref[...], wpb_ref[...])
        y = jax.nn.sigmoid(ga_ref[...].astype(F32)) * pa + jax.nn.sigmoid(gb_ref[...].astype(F32)) * pb
        yb = y.astype(BF16)
        x1 = x_ref[...] + _dot(yb, wo_ref[...])
        x1_ref[...] = x1
        y_ref[...] = yb
        pa_ref[...] = pa.astype(BF16)
        pb_ref[...] = pb.astype(BF16)
        xn2_ref[...] = (x1 * _rms_rows(x1) * g2_ref[...]).astype(BF16)

    row = lambda w: pl.BlockSpec((tm, w), lambda i: (i, 0))
    full = lambda a: pl.BlockSpec(a.shape, lambda i: (0, 0))
    return pl.pallas_call(
        body, name="merge_fwd", grid=(S // tm,),
        in_specs=[row(ATT_OUT_W), row(RET_V_W),
                  pl.BlockSpec((tm, D_MODEL), lambda i: (i, R_GA // D_MODEL)),
                  pl.BlockSpec((tm, D_MODEL), lambda i: (i, R_GB // D_MODEL)),
                  row(D_MODEL), full(wpa), full(wpb), full(wout), full(g2)],
        out_specs=[row(D_MODEL)] * 5,
        out_shape=[_sds((S, D_MODEL), F32)] + [_sds((S, D_MODEL), BF16)] * 4,
        compiler_params=_cparams(("parallel",)),
    )(o_a, o_r, z_rest, z_rest, x, wpa, wpb, wout, g2)


def _relu_sq(u):
    r = jnp.maximum(u, 0.0)
    return r * r


def _rms_bwd(dy, xv, g):
    r = _rms_rows(xv)
    xh = xv * r
    dg = dy * g
    dx = r * (dg - xh * jnp.mean(dg * xh, axis=-1, keepdims=True))
    return dx, jnp.sum(dy * xh, axis=0, keepdims=True)


def _ret_bwd(do_r, o_pre, states, z_rest, dz_rest, gn_g, gn_b):
    S = z_rest.shape[0]
    nch = S // CHUNK

    def body(do_ref, o_ref, st_ref, q_ref, k_ref, v_ref, gr_ref, g_ref, b_ref, dz_in,
             dz_ref, dg_ref, db_ref, gst):
        del dz_in
        first = pl.program_id(0) == 0

        @pl.when(first)
        def _():
            gst[...] = jnp.zeros_like(gst)

        dgs, dbs = [], []
        for h in range(RET_HEADS):
            decay, xi, zeta, gch = _ret_tables(LOG_GAMMA[h])
            cq = slice(h * RET_QK, (h + 1) * RET_QK)
            cv = slice(h * RET_V, (h + 1) * RET_V)
            o = o_ref[:, cv]
            mu = jnp.mean(o, axis=-1, keepdims=True)
            cen = o - mu
            rstd = lax.rsqrt(jnp.mean(cen * cen, axis=-1, keepdims=True) + EPS)
            yh = cen * rstd
            gam = g_ref[:, cv]
            y = yh * gam + b_ref[:, cv]
            gr = gr_ref[:, cv].astype(F32)
            sg = jax.nn.sigmoid(gr)
            dout = do_ref[:, cv]
            dy = dout * (gr * sg)
            dz_ref[:, R_G + h * RET_V:R_G + (h + 1) * RET_V] = (dout * y * (sg * (1.0 + gr * (1.0 - sg)))).astype(BF16)
            dgs.append(jnp.sum(dy * yh, axis=0, keepdims=True))
            dbs.append(jnp.sum(dy, axis=0, keepdims=True))
            dyh = dy * gam
            do = rstd * (dyh - jnp.mean(dyh, axis=-1, keepdims=True)
                         - yh * jnp.mean(dyh * yh, axis=-1, keepdims=True))

            q = q_ref[:, cq]
            kc32 = k_ref[:, cq].astype(F32) * (RET_QK ** -0.5)
            kc = kc32.astype(BF16)
            v = v_ref[:, cv]
            dob = do.astype(BF16)
            a = (_dot(q, kc, _NT) * decay).astype(BF16)
            da = (_dot(dob, v, _NT) * decay).astype(BF16)
            dcross = (do * xi).astype(BF16)
            g_next = gst[h]
            gb = g_next.astype(BF16)
            dq = _dot(da, kc) + _dot(dcross, st_ref[h], _NT)
            dkc = _dot(da, q, _TN)
            dkz = _dot(v, gb, _NT)
            dv = _dot(a, dob, _TN) + _dot((kc32 * zeta).astype(BF16), gb)
            gst[h] = g_next * gch + _dot(q, dcross, _TN)
            dz_ref[:, R_Q + h * RET_QK:R_Q + (h + 1) * RET_QK] = dq.astype(BF16)
            dz_ref[:, R_K + h * RET_QK:R_K + (h + 1) * RET_QK] = ((dkc + dkz * zeta) * (RET_QK ** -0.5)).astype(BF16)
            dz_ref[:, R_V + h * RET_V:R_V + (h + 1) * RET_V] = dv.astype(BF16)
        _acc_rows(dg_ref, jnp.concatenate(dgs, axis=1), first)
        _acc_rows(db_ref, jnp.concatenate(dbs, axis=1), first)

    qk, vv, par, wide, st = _ret_specs(nch, True)
    return pl.pallas_call(
        body, name="ret_bwd", grid=(nch,),
        in_specs=[wide, wide, st, qk(R_Q), qk(R_K), vv(R_V), vv(R_G), par, par, pl.BlockSpec(memory_space=pl.ANY)],
        out_specs=[pl.BlockSpec((CHUNK, RET_COLS), lambda n: (nch - 1 - n, 0)), par, par],
        out_shape=[_sds(dz_rest.shape, BF16), _sds((1, RET_V_W), F32), _sds((1, RET_V_W), F32)],
        input_output_aliases={9: 0},
        scratch_shapes=[pltpu.VMEM((RET_HEADS, RET_QK, RET_V), F32)],
        compiler_params=_cparams(("arbitrary",)),
    )(do_r, o_pre, states, z_rest, z_rest, z_rest, z_rest, gn_g, gn_b, dz_rest)


def _att_probs(q, k, lmix, valid, bias):
    s = _dot(q, k, _NT) * (DH ** -0.5) + bias
    return jnp.where(valid, jnp.exp(jnp.where(valid, s, NEG) - lmix), 0.0)


def _att_bwd_group(z_att, dz_att, do_a, delta, lmix, gq3, gk3, slopes3, gi, d):
    S = z_att.shape[0]
    nsb = S // (BLK * d)
    scale = DH ** -0.5
    aliased = dz_att is not None

    def body(q_ref, k_ref, v_ref, kp_ref, vp_ref, qn_ref, do_ref, don_ref, dl_ref, dln_ref, lm_ref, lmn_ref,
             gq_ref, gk_ref, sl_ref, *rest):
        dz_ref, dgq_ref, dgk_ref, stage = rest[-4:]
        n = pl.program_id(1)
        slope = sl_ref[...][:, :1]
        valid, bias = _att_mask_bias(slope, d, n == 0)
        qi = lax.broadcasted_iota(jnp.int32, (BLK, BLK), 0)
        kj = lax.broadcasted_iota(jnp.int32, (BLK, BLK), 1)
        dist_n = BLK + qi - kj
        valid_n = (dist_n <= BLK) & (n < nsb - 1)
        bias_n = -slope * (dist_n * d).astype(F32)
        gq, gk = gq_ref[...], gk_ref[...]
        dgq = jnp.zeros((1, DH), F32)
        dgk = jnp.zeros((1, DH), F32)
        for r in range(d):
            q, qh, qr = _head_norm(_rows(q_ref, r, d), gq)
            kc, kh, kr = _head_norm(_rows(k_ref, r, d), gk)
            kp, _, _ = _head_norm(_rows(kp_ref, r, d), gk)
            qn, _, _ = _head_norm(_rows(qn_ref, r, d), gq)
            vc = _rows(v_ref, r, d).astype(BF16)
            vp = _rows(vp_ref, r, d).astype(BF16)
            do_c = _rows(do_ref, r, d).astype(BF16)
            do_n = _rows(don_ref, r, d).astype(BF16)
            k2 = jnp.concatenate([kp, kc], axis=0)
            v2 = jnp.concatenate([vp, vc], axis=0)
            p = _att_probs(q, k2, _rows(lm_ref, r, d)[:, :1], valid, bias)
            ds = (p * (_dot(do_c, v2, _NT) - _rows(dl_ref, r, d)[:, :1]) * scale).astype(BF16)
            dx, dg = _head_norm_bwd(_dot(ds, k2), qh, qr, gq)
            _put_rows(stage.at[0], r, d, dx)
            dgq = dgq + dg
            p_n = _att_probs(qn, kc, _rows(lmn_ref, r, d)[:, :1], valid_n, bias_n)
            ds_n = (p_n * (_dot(do_n, vc, _NT) - _rows(dln_ref, r, d)[:, :1]) * scale).astype(BF16)
            dk = _dot(ds[:, BLK:], q, _TN) + _dot(ds_n, qn, _TN)
            dv = _dot(p[:, BLK:].astype(BF16), do_c, _TN) + _dot(p_n.astype(BF16), do_n, _TN)
            dx, dg = _head_norm_bwd(dk, kh, kr, gk)
            _put_rows(stage.at[1], r, d, dx)
            _put_rows(stage.at[2], r, d, dv)
            dgk = dgk + dg
        for kind in range(3):
            dz_ref[:, kind * DH:(kind + 1) * DH] = stage[kind].astype(BF16)
        _acc_rows(dgq_ref, dgq, n == 0)
        _acc_rows(dgk_ref, dgk, n == 0)

    at, head, slot = _att_specs(gi, d, nsb)
    gain = pl.BlockSpec((None, 1, DH), lambda j, n: (j, 0, 0))
    in_specs = [at(0, 0), at(1, 0), at(2, 0), at(1, -1), at(2, -1), at(0, 1),
                slot(0), slot(1), slot(0), slot(1), slot(0), slot(1), head, head, head]
    args = [z_att] * 6 + [do_a, do_a, delta, delta, lmix, lmix, gq3, gk3, slopes3]
    if aliased:
        in_specs.append(pl.BlockSpec(memory_space=pl.ANY))
        args.append(dz_att)
    return pl.pallas_call(
        body, name=f"att_bwd_g{gi}", grid=(HPG, nsb),
        in_specs=in_specs,
        out_specs=[pl.BlockSpec((BLK * d, 3 * DH), lambda j, n: (n, gi * HPG + j)), gain, gain],
        out_shape=[_sds(z_att.shape, BF16), _sds((HPG, 1, DH), F32), _sds((HPG, 1, DH), F32)],
        input_output_aliases={len(args) - 1: 0} if aliased else {},
        scratch_shapes=[pltpu.VMEM((3, BLK * d, DH), F32)],
        compiler_params=_cparams(("parallel", "arbitrary")),
    )(*args)


def _att_col_blocks():
    return [kind * ATT_HEADS + h for h in range(ATT_HEADS) for kind in range(3)]


def _local_step(x, target, norm1_g, q_norm_g, k_norm_g, gn_g, gn_b, norm2_g, w_in, w_pa, w_pb, w_out, w_up, w_down):
    S = x.shape[0]
    tm = min(512, S)
    tk = min(512, S)
    blocks = _att_col_blocks()
    w_att = jnp.concatenate([w_in[:, p * DH:(p + 1) * DH] for p in blocks], axis=1)
    w_rest = w_in[:, 3 * ATT_W:]
    gq3 = q_norm_g.reshape(ATT_HEADS, 1, DH)
    gk3 = k_norm_g.reshape(ATT_HEADS, 1, DH)
    slopes3 = jnp.asarray(np.broadcast_to(ALIBI[:, None, None], (ATT_HEADS, 1, DH)).copy())

    xn = _rmsnorm_fwd(x, norm1_g, tm)
    z_att = _mm("in_proj_att", xn, w_att, "nn", tm, 512, D_MODEL,
                [(_sds((S, 3 * ATT_W), F32), _tile_ij(tm, 512))], _epi_store(F32))[0]
    z_rest = _mm("in_proj_rest", xn, w_rest, "nn", tm, 1024, D_MODEL,
                 [(_sds((S, REST_W), BF16), _tile_ij(tm, 1024))], _epi_store(BF16))[0]
    os_, ls_ = [], []
    for gi, (_, d) in enumerate(ATT_GROUPS):
        o, l = _att_fwd_group(z_att, gq3, gk3, slopes3, gi, d)
        os_.append(o)
        ls_.append(l)
    o_a, lmix = _att_combine(os_, ls_, tm)
    o_r, o_pre, states = _ret_fwd(z_rest, gn_g, gn_b)
    x1, y, pa, pb, xn2 = _merge_fwd(o_a, o_r, z_rest, x, w_pa, w_pb, w_out, norm2_g, min(256, S))
    u = _mm("mlp_up", xn2, w_up, "nn", tm, 1024, D_MODEL,
            [(_sds((S, D_FF), F32), _tile_ij(tm, 1024))], _epi_store(F32))[0]

    def epi_down(acc, ex, out):
        diff = ex[0][...] + acc - ex[1][...]
        out[0][...] = diff * (1.0 / D_MODEL)
        out[1][...] = jnp.broadcast_to(jnp.sum(diff * diff) * (1.0 / (8 * LANES)), (8, LANES))

    row_tile = _tile_ij(tm, D_MODEL)
    dx2, loss_parts = _mm(
        "mlp_down_loss", u, w_down, "nn", tm, D_MODEL, 1024,
        [(_sds((S, D_MODEL), F32), row_tile),
         (_sds((S // tm * 8, LANES), F32), pl.BlockSpec((8, LANES), lambda i, j, k: (i, 0)))],
        epi_down, extras=[(x1, row_tile), (target, row_tile)], a_pro=_relu_sq)
    loss_local = jnp.sum(loss_parts) * (0.5 / D_MODEL)

    def epi_du(acc, ex, out):
        out[0][...] = (acc * (2.0 * jnp.maximum(ex[0][...], 0.0))).astype(BF16)

    du = _mm("mlp_down_bwd", dx2, w_down, "nt", tm, 1024, D_MODEL,
             [(_sds((S, D_FF), BF16), _tile_ij(tm, 1024))], epi_du, extras=[(u, _tile_ij(tm, 1024))])[0]
    gw_down = _mm("gw_down", u, dx2, "tn", 1024, D_MODEL, tk,
                  [(_sds((D_FF, D_MODEL), F32), _tile_ij(1024, D_MODEL))], _epi_store(F32), a_pro=_relu_sq)[0]
    gw_up = _mm("gw_up", xn2, du, "tn", D_MODEL, 1024, tk,
                [(_sds((D_MODEL, D_FF), F32), _tile_ij(D_MODEL, 1024))], _epi_store(F32))[0]

    vec = pl.BlockSpec((1, D_MODEL), lambda i, j, k: (0, 0))
    seq_sem = ("arbitrary", "arbitrary", "arbitrary")

    def epi_norm2(acc, ex, out):
        dx, dg = _rms_bwd(acc, ex[0][...], ex[2][...])
        out[0][...] = ex[1][...] + dx
        _acc_rows(out[1], dg, pl.program_id(0) == 0)

    dx1, g_norm2 = _mm(
        "mlp_up_bwd", du, w_up, "nt", tm, D_MODEL, 1024,
        [(_sds((S, D_MODEL), F32), row_tile), (_sds((1, D_MODEL), F32), vec)],
        epi_norm2, extras=[(x1, row_tile), (dx2, row_tile), (norm2_g, vec)], sem=seq_sem)

    def epi_dy(acc, ex, out):
        sa = jax.nn.sigmoid(ex[0][...].astype(F32))
        sb = jax.nn.sigmoid(ex[1][...].astype(F32))
        out[0][...] = (acc * sa).astype(BF16)
        out[1][...] = (acc * sb).astype(BF16)
        out[2][:, :D_MODEL] = (acc * ex[2][...].astype(F32) * (sa * (1.0 - sa))).astype(BF16)
        out[2][:, D_MODEL:] = (acc * ex[3][...].astype(F32) * (sb * (1.0 - sb))).astype(BF16)

    ga_spec = pl.BlockSpec((tm, D_MODEL), lambda i, j, k: (i, R_GA // D_MODEL))
    gb_spec = pl.BlockSpec((tm, D_MODEL), lambda i, j, k: (i, R_GB // D_MODEL))
    gates_spec = pl.BlockSpec((tm, 2 * D_MODEL), lambda i, j, k: (i, R_GA // (2 * D_MODEL)))
    dpa, dpb, dz_rest = _mm(
        "out_proj_bwd", dx1, w_out, "nt", tm, D_MODEL, D_MODEL,
        [(_sds((S, D_MODEL), BF16), row_tile), (_sds((S, D_MODEL), BF16), row_tile), (_sds((S, REST_W), BF16), gates_spec)],
        epi_dy, extras=[(z_rest, ga_spec), (z_rest, gb_spec), (pa, row_tile), (pb, row_tile)])
    gw_out = _mm("gw_out", y, dx1, "tn", D_MODEL, D_MODEL, tk,
                 [(_sds((D_MODEL, D_MODEL), F32), _tile_ij(D_MODEL, D_MODEL))], _epi_store(F32))[0]
    gw_pa = _mm("gw_proj_a", o_a, dpa, "tn", ATT_OUT_W, D_MODEL, tk,
                [(_sds((ATT_OUT_W, D_MODEL), F32), _tile_ij(ATT_OUT_W, D_MODEL))], _epi_store(F32))[0]
    gw_pb = _mm("gw_proj_b", o_r, dpb, "tn", 1024, D_MODEL, tk,
                [(_sds((RET_V_W, D_MODEL), F32), _tile_ij(1024, D_MODEL))], _epi_store(F32))[0]

    def epi_doa(acc, ex, out):
        out[0][...] = acc
        prod = acc * ex[0][...]
        out[1][...] = jnp.concatenate(
            [jnp.broadcast_to(jnp.sum(prod[:, j * DH:(j + 1) * DH], axis=-1, keepdims=True), (prod.shape[0], DH))
             for j in range(HPG)], axis=1)

    slot_tile = _tile_ij(tm, ATT_OUT_W)
    do_a, delta = _mm("proj_a_bwd", dpa, w_pa, "nt", tm, ATT_OUT_W, D_MODEL,
                      [(_sds((S, ATT_OUT_W), F32), slot_tile), (_sds((S, ATT_OUT_W), F32), slot_tile)],
                      epi_doa, extras=[(o_a, slot_tile)])
    do_r = _mm("proj_b_bwd", dpb, w_pb, "nt", tm, 1024, D_MODEL,
               [(_sds((S, RET_V_W), F32), _tile_ij(tm, 1024))], _epi_store(F32))[0]

    dz_rest, g_gn_g, g_gn_b = _ret_bwd(do_r, o_pre, states, z_rest, dz_rest, gn_g, gn_b)
    dz_att, gq_parts, gk_parts = None, [], []
    for gi, (_, d) in enumerate(ATT_GROUPS):
        dz_att, gq_p, gk_p = _att_bwd_group(z_att, dz_att, do_a, delta, lmix, gq3, gk3, slopes3, gi, d)
        gq_parts.append(gq_p)
        gk_parts.append(gk_p)
    g_qn = jnp.concatenate(gq_parts, axis=0).reshape(1, ATT_HEADS, DH)
    g_kn = jnp.concatenate(gk_parts, axis=0).reshape(1, ATT_HEADS, DH)

    gw_att = _mm("gw_in_att", xn, dz_att, "tn", D_MODEL, 512, tk,
                 [(_sds((D_MODEL, 3 * ATT_W), F32), _tile_ij(D_MODEL, 512))], _epi_store(F32))[0]
    gw_rest = _mm("gw_in_rest", xn, dz_rest, "tn", D_MODEL, 1024, tk,
                  [(_sds((D_MODEL, REST_W), F32), _tile_ij(D_MODEL, 1024))], _epi_store(F32))[0]
    dxn_att = _mm("in_proj_att_bwd", dz_att, w_att, "nt", tm, D_MODEL, 512,
                  [(_sds((S, D_MODEL), F32), row_tile)], _epi_store(F32))[0]

    def epi_norm1(acc, ex, out):
        dx, dg = _rms_bwd(acc + ex[0][...], ex[1][...], ex[3][...])
        out[0][...] = ex[2][...] + dx
        _acc_rows(out[1], dg, pl.program_id(0) == 0)

    grad_x, g_norm1 = _mm(
        "in_proj_rest_bwd", dz_rest, w_rest, "nt", tm, D_MODEL, 1024,
        [(_sds((S, D_MODEL), F32), row_tile), (_sds((1, D_MODEL), F32), vec)],
        epi_norm1, extras=[(dxn_att, row_tile), (x, row_tile), (dx1, row_tile), (norm1_g, vec)], sem=seq_sem)

    inv = np.argsort(np.asarray(blocks))
    gw_in = jnp.concatenate([gw_att[:, p * DH:(p + 1) * DH] for p in inv] + [gw_rest], axis=1)
    big = (gw_in, gw_pa, gw_pb, gw_out, gw_up, gw_down)
    small = (g_norm1, g_qn, g_kn, g_gn_g, g_gn_b, g_norm2)
    return loss_local, grad_x, big, small


def _position():
    return lax.axis_index("x"), lax.axis_index("y"), lax.axis_index("c")


def _all_gather(shard):
    R = shard.shape[0]

    def body(x_ref, out_ref, send_sems, recv_sems, local_sem):
        x, y, c = _position()
        me, sibling = (x, y, c), (x, y, 1 - c)
        chips = [(1 - x, y), (x, 1 - y), (1 - x, 1 - y)]

        def rows(px, py, pc):
            return out_ref.at[4 * px + 2 * py + pc]

        def copy(k, block, to, src=None):
            return pltpu.make_async_remote_copy(
                src_ref=rows(*block) if src is None else src, dst_ref=rows(*block),
                send_sem=send_sems.at[k], recv_sem=recv_sems.at[k], device_id=to, device_id_type=MESH)

        mine = pltpu.make_async_copy(x_ref, rows(*me), local_sem)
        mine.start()
        first = [copy(0, me, sibling, src=x_ref)]
        first += [copy(1 + j, me, (*chip, c), src=x_ref) for j, chip in enumerate(chips)]
        for cp in first:
            cp.start()
        passed = [copy(4 + j, (*chip, c), sibling) for j, chip in enumerate(chips)]
        for j, chip in enumerate(chips):
            copy(1 + j, (*chip, c), me).wait_recv()
            passed[j].start()
        copy(0, sibling, me).wait_recv()
        for j, chip in enumerate(chips):
            copy(4 + j, (*chip, 1 - c), me).wait_recv()
        for cp in first + passed:
            cp.wait_send()
        mine.wait()

    return pl.pallas_call(
        body, name="all_gather_weights",
        out_shape=_sds((N_DEV, R, LANES), shard.dtype),
        in_specs=[pl.BlockSpec(memory_space=pl.ANY)],
        out_specs=pl.BlockSpec(memory_space=pl.ANY),
        scratch_shapes=[pltpu.SemaphoreType.DMA((7,)), pltpu.SemaphoreType.DMA((7,)), pltpu.SemaphoreType.DMA],
    )(shard)


def _pair_exchange(g):
    def body(g_ref, out_ref, send_sem, recv_sem):
        x, y, c = _position()
        cp = pltpu.make_async_remote_copy(
            src_ref=g_ref.at[1 - c], dst_ref=out_ref, send_sem=send_sem, recv_sem=recv_sem,
            device_id=(x, y, 1 - c), device_id_type=MESH)
        cp.start()
        cp.wait()

    return pl.pallas_call(
        body, name="grad_pair_exchange",
        out_shape=_sds(g.shape[1:], g.dtype),
        in_specs=[pl.BlockSpec(memory_space=pl.ANY)],
        out_specs=pl.BlockSpec(memory_space=pl.ANY),
        scratch_shapes=[pltpu.SemaphoreType.DMA, pltpu.SemaphoreType.DMA],
    )(g)


def _pair_sum(g, got, core):
    _, n_chip, R, _ = g.shape

    def body(c_ref, a_ref, b_ref, o_ref):
        del c_ref
        o_ref[...] = a_ref[...] + b_ref[...]

    return pl.pallas_call(
        body, name="grad_pair_sum",
        grid_spec=pltpu.PrefetchScalarGridSpec(
            num_scalar_prefetch=1, grid=(n_chip, R // FLAT_TILE),
            in_specs=[pl.BlockSpec((None, None, FLAT_TILE, LANES), lambda ch, i, c_ref: (c_ref[0], ch, i, 0)),
                      pl.BlockSpec((None, FLAT_TILE, LANES), lambda ch, i, c_ref: (ch, i, 0))],
            out_specs=pl.BlockSpec((None, FLAT_TILE, LANES), lambda ch, i, c_ref: (ch, i, 0))),
        out_shape=_sds(g.shape[1:], g.dtype),
        compiler_params=_cparams(("parallel", "parallel")),
    )(core, g, got)


def _chip_exchange(p):
    def body(p_ref, out_ref, send_sems, recv_sems, local_sem):
        x, y, c = _position()
        my_chip = 2 * x + y
        chips = [(1 - x, y), (x, 1 - y), (1 - x, 1 - y)]
        local = pltpu.make_async_copy(p_ref.at[my_chip], out_ref.at[my_chip], local_sem)
        local.start()
        cps = [pltpu.make_async_remote_copy(
            src_ref=p_ref.at[2 * cx + cy], dst_ref=out_ref.at[my_chip],
            send_sem=send_sems.at[k], recv_sem=recv_sems.at[k], device_id=(cx, cy, c), device_id_type=MESH)
            for k, (cx, cy) in enumerate(chips)]
        for cp in cps:
            cp.start()
        for cp in cps:
            cp.wait()
        local.wait()

    return pl.pallas_call(
        body, name="grad_chip_exchange",
        out_shape=_sds(p.shape, p.dtype),
        in_specs=[pl.BlockSpec(memory_space=pl.ANY)],
        out_specs=pl.BlockSpec(memory_space=pl.ANY),
        scratch_shapes=[pltpu.SemaphoreType.DMA((3,)), pltpu.SemaphoreType.DMA((3,)), pltpu.SemaphoreType.DMA],
    )(p)


def _adamw(parts, w, m, v):
    R = w.shape[0]

    def body(p_ref, w_ref, m_ref, v_ref, g_ref, d_ref, mo_ref, vo_ref):
        g = ((p_ref[0] + p_ref[1]) + p_ref[2]) + p_ref[3]
        m_new = ADAM_B1 * m_ref[...] + (1.0 - ADAM_B1) * g
        v_new = ADAM_B2 * v_ref[...] + (1.0 - ADAM_B2) * (g * g)
        m_hat = m_new / (1.0 - ADAM_B1 ** ADAM_STEP)
        v_hat = v_new / (1.0 - ADAM_B2 ** ADAM_STEP)
        g_ref[...] = g
        d_ref[...] = -ADAM_LR * (m_hat / (jnp.sqrt(v_hat) + ADAM_EPS) + ADAM_WD * w_ref[...])
        mo_ref[...] = m_new
        vo_ref[...] = v_new

    tile = pl.BlockSpec((FLAT_TILE, LANES), lambda i: (i, 0))
    return pl.pallas_call(
        body, name="adamw", grid=(R // FLAT_TILE,),
        in_specs=[pl.BlockSpec((N_CHIP, FLAT_TILE, LANES), lambda i: (0, i, 0)), tile, tile, tile],
        out_specs=[tile] * 4,
        out_shape=[_sds((R, LANES), F32)] * 4,
        compiler_params=_cparams(("parallel",)),
    )(parts, w, m, v)


def _flat(arrs):
    return jnp.concatenate([a.reshape(-1) for a in arrs]).reshape(-1, LANES)


def _unflat(flat, shapes):
    out, off = [], 0
    v = flat.reshape(-1)
    for s in shapes:
        n = int(np.prod(s))
        out.append(v[off:off + n].reshape(s))
        off += n
    return out


def _assemble_weights(gathered):
    per_dev = [_unflat(gathered[k], BIG_SHAPES) for k in range(N_DEV)]
    col = lambda i: jnp.concatenate([per_dev[k][i] for k in range(N_DEV)], axis=1)
    row = lambda i: jnp.concatenate([per_dev[k][i] for k in range(N_DEV)], axis=0)
    return col(0), col(1), row(2), row(3), col(4), row(5)


def _owed_slices(big, small):
    gw_in, gw_pa, gw_pb, gw_out, gw_up, gw_down = big
    small_flat = [s.reshape(-1) for s in small]
    per_core = []
    for core in range(2):
        per_chip = []
        for chip in range(N_CHIP):
            k = 2 * chip + core
            parts = [gw_in[:, k * 1600:(k + 1) * 1600], gw_pa[:, k * 128:(k + 1) * 128], gw_pb[k * 256:(k + 1) * 256],
                     gw_out[k * 128:(k + 1) * 128], gw_up[:, k * 512:(k + 1) * 512], gw_down[k * 512:(k + 1) * 512]]
            per_chip.append(jnp.concatenate([p.reshape(-1) for p in parts] + small_flat).reshape(FLAT_ROWS, LANES))
        per_core.append(jnp.stack(per_chip))
    return jnp.stack(per_core)


def kernel(x, norm1_g, w_in, q_norm_g, k_norm_g, ret_gn_g, ret_gn_b, w_proj_a, w_proj_b, w_out, norm2_g, w_up, w_down, loss_target, m_norm1_g, m_w_in, m_q_norm_g, m_k_norm_g, m_ret_gn_g, m_ret_gn_b, m_w_proj_a, m_w_proj_b, m_w_out, m_norm2_g, m_w_up, m_w_down, v_norm1_g, v_w_in, v_q_norm_g, v_k_norm_g, v_ret_gn_g, v_ret_gn_b, v_w_proj_a, v_w_proj_b, v_w_out, v_norm2_g, v_w_up, v_w_down):
    big_w = (w_in, w_proj_a, w_proj_b, w_out, w_up, w_down)
    small_w = (norm1_g, q_norm_g, k_norm_g, ret_gn_g, ret_gn_b, norm2_g)
    big_m = (m_w_in, m_w_proj_a, m_w_proj_b, m_w_out, m_w_up, m_w_down)
    small_m = (m_norm1_g, m_q_norm_g, m_k_norm_g, m_ret_gn_g, m_ret_gn_b, m_norm2_g)
    big_v = (v_w_in, v_w_proj_a, v_w_proj_b, v_w_out, v_w_up, v_w_down)
    small_v = (v_norm1_g, v_q_norm_g, v_k_norm_g, v_ret_gn_g, v_ret_gn_b, v_norm2_g)

    gathered = _all_gather(_flat([w.astype(BF16) for w in big_w]))
    full_w = _assemble_weights(gathered)
    loss_local, grad_x, big_g, small_g = _local_step(
        x[0], loss_target[0], norm1_g, q_norm_g[0], k_norm_g[0], ret_gn_g, ret_gn_b, norm2_g, *full_w)

    owed = _owed_slices(big_g, small_g)
    core = lax.axis_index("c").astype(jnp.int32).reshape(1)
    chip_sums = _pair_sum(owed, _pair_exchange(owed), core)
    parts = _chip_exchange(chip_sums)
    g, d, m_new, v_new = _adamw(parts, _flat(big_w + small_w), _flat(big_m + small_m), _flat(big_v + small_v))

    big_shapes = [w.shape for w in big_w]
    small_shapes = [w.shape for w in small_w]

    def split(flat):
        b = _unflat(flat[:BIG_ROWS], big_shapes)
        s = _unflat(flat[BIG_ROWS:], small_shapes)
        return [s[0], b[0], s[1], s[2], s[3], s[4], b[1], b[2], b[3], s[5], b[4], b[5]]

    loss = lax.psum(loss_local, MESH_AXES)
    return (loss, grad_x[None], *split(g), *split(d), *split(m_new), *split(v_new))
```

```python
import math

import numpy as np
import jax
import jax.numpy as jnp
from jax import lax
from jax.experimental import pallas as pl
from jax.experimental.pallas import tpu as pltpu

F32 = jnp.float32
BF16 = jnp.bfloat16

D_MODEL = 1024
ATT_GROUPS = ((128, 1), (512, 4), (2048, 16))
ATT_BLOCKS_PER_STEP = (8, 1, 1)
HPG = 4
ATT_HEADS = 12
DH = 128
BLK = 128
ATT_W = ATT_HEADS * DH
ATT_OUT_W = HPG * DH
RET_HEADS = 4
RET_QK = 256
RET_V = 512
RET_QK_W = RET_HEADS * RET_QK
RET_V_W = RET_HEADS * RET_V
CHUNK = 128
D_FF = 4096
IN_W = 12800
REST_W = IN_W - 3 * ATT_W
RET_COLS = 6144
EPS = 1e-6
ADAM_LR, ADAM_B1, ADAM_B2, ADAM_EPS, ADAM_WD, ADAM_STEP = 0.001, 0.9, 0.999, 1e-08, 0.01, 10
N_DEV = 8
N_CHIP = 4
MESH_AXES = ("x", "y", "c")
MESH = pl.DeviceIdType.MESH
VMEM_LIMIT = 56 * 1024 * 1024
LANES = 128
NEG = -1e30

_NN = (((1,), (0,)), ((), ()))
_NT = (((1,), (1,)), ((), ()))
_TN = (((0,), (0,)), ((), ()))

R_Q, R_K, R_V, R_G, R_GA, R_GB = 0, 1024, 2048, 4096, 6144, 7168

LOG_GAMMA = [float(v) for v in np.log(1.0 - 2.0 ** (-5.0 - np.arange(RET_HEADS, dtype=np.float32))).astype(np.float32)]
ALIBI = np.asarray(2.0 ** (-8.0 * np.arange(1, ATT_HEADS + 1, dtype=np.float32) / ATT_HEADS), np.float32)

SMALL_ROWS = (1024 + 1536 + 1536 + 2048 + 2048 + 1024) // LANES


def _dot(a, b, dims=_NN):
    return lax.dot_general(a, b, dims, preferred_element_type=F32)


def _cparams(sem):
    return pltpu.CompilerParams(dimension_semantics=sem, vmem_limit_bytes=VMEM_LIMIT)


def _sds(shape, dtype):
    return jax.ShapeDtypeStruct(shape, dtype)


def _mm(name, a, b, mode, tm, tn, tk, outs, epi, extras=(), b_pro=None,
        sem=("parallel", "parallel", "arbitrary")):
    if mode == "nn":
        (M, K), (_, N) = a.shape, b.shape
        a_spec = pl.BlockSpec((tm, tk), lambda i, j, k: (i, k))
        dims = _NN
    else:
        (K, M), (_, N) = a.shape, b.shape
        a_spec = pl.BlockSpec((tk, tm), lambda i, j, k: (k, i))
        dims = _TN
    b_spec = pl.BlockSpec((tk, tn), lambda i, j, k: (k, j))
    assert M % tm == 0 and N % tn == 0 and K % tk == 0, (name, M, N, K, tm, tn, tk)
    nk = K // tk
    n_ex, n_out = len(extras), len(outs)

    def body(a_ref, b_ref, *rest):
        ex, out = rest[:n_ex], rest[n_ex:n_ex + n_out]
        bv = b_ref[...]
        if b_pro is not None:
            bv = b_pro(bv)
        part = _dot(a_ref[...].astype(BF16), bv.astype(BF16), dims)
        if nk == 1:
            epi(part, ex, out)
        else:
            acc_ref = rest[n_ex + n_out]
            k = pl.program_id(2)

            @pl.when(k == 0)
            def _():
                acc_ref[...] = part

            @pl.when(k > 0)
            def _():
                acc_ref[...] += part

            @pl.when(k == nk - 1)
            def _():
                epi(acc_ref[...], ex, out)

    return pl.pallas_call(
        body,
        name=name,
        grid=(M // tm, N // tn, nk),
        in_specs=[a_spec, b_spec] + [s for _, s in extras],
        out_specs=[s for _, s in outs],
        out_shape=[o for o, _ in outs],
        scratch_shapes=[pltpu.VMEM((tm, tn), F32)] if nk > 1 else [],
        compiler_params=_cparams(sem),
    )(a, b, *[e for e, _ in extras])


def _tile_ij(tm, tn):
    return pl.BlockSpec((tm, tn), lambda i, j, k: (i, j))


def _epi_store(dtype):
    def epi(acc, ex, out):
        out[0][...] = acc.astype(dtype)
    return epi


def _rms_rows(x):
    return lax.rsqrt(jnp.mean(x * x, axis=-1, keepdims=True) + EPS)


def _acc_rows(ref, part, first):
    @pl.when(first)
    def _():
        ref[...] = part

    @pl.when(jnp.logical_not(first))
    def _():
        ref[...] += part


def _rmsnorm_fwd(x, g, tm):
    S, Dm = x.shape

    def body(x_ref, g_ref, o_ref):
        xv = x_ref[...]
        o_ref[...] = (xv * _rms_rows(xv) * g_ref[...]).astype(BF16)

    return pl.pallas_call(
        body, name="rmsnorm1_fwd", grid=(S // tm,),
        in_specs=[pl.BlockSpec((tm, Dm), lambda i: (i, 0)), pl.BlockSpec((1, Dm), lambda i: (0, 0))],
        out_specs=pl.BlockSpec((tm, Dm), lambda i: (i, 0)),
        out_shape=_sds((S, Dm), BF16),
        compiler_params=_cparams(("parallel",)),
    )(x, g)


def _rows(ref, r, b, d):
    if d == 1:
        return ref[b * BLK:(b + 1) * BLK, :]
    return ref[pl.ds(b * BLK * d + r, BLK, stride=d), :]


def _put_rows(ref, r, b, d, val):
    if d == 1:
        ref[b * BLK:(b + 1) * BLK, :] = val
    else:
        ref[pl.ds(b * BLK * d + r, BLK, stride=d), :] = val


def _head_norm(x, g):
    r = _rms_rows(x)
    xh = x * r
    return (xh * g).astype(BF16), xh, r


def _head_norm_bwd(dyn, xh, r, g):
    dxh = dyn * g
    dx = r * (dxh - xh * jnp.mean(dxh * xh, axis=-1, keepdims=True))
    return dx, jnp.sum(dyn * xh, axis=0, keepdims=True)


def _att_mask_bias(slope, d, first):
    qi = lax.broadcasted_iota(jnp.int32, (BLK, 2 * BLK), 0)
    kj = lax.broadcasted_iota(jnp.int32, (BLK, 2 * BLK), 1)
    dist = BLK + qi - kj
    valid = (dist >= 0) & (dist <= BLK)
    if first is not None:
        valid = valid & (jnp.logical_not(first) | (kj >= BLK))
    bias = -slope * (dist * d).astype(F32)
    return valid, bias


def _att_specs(gi, d, nb, S):
    span = BLK * d
    sb = span * nb
    nspan = S // span
    before = lambda n: jnp.maximum(n * nb - 1, 0)
    after = lambda n: jnp.minimum((n + 1) * nb, nspan - 1)
    zcol = lambda j, kind: 3 * (gi * HPG + j) + kind
    cur = lambda kind: pl.BlockSpec((sb, DH), lambda j, n: (n, zcol(j, kind)))
    prev = lambda kind: pl.BlockSpec((span, DH), lambda j, n: (before(n), zcol(j, kind)))
    nxt = lambda kind: pl.BlockSpec((span, DH), lambda j, n: (after(n), zcol(j, kind)))
    slot = pl.BlockSpec((sb, DH), lambda j, n: (n, j))
    slot_next = pl.BlockSpec((span, DH), lambda j, n: (after(n), j))
    head = pl.BlockSpec((None, 1, DH), lambda j, n: (gi * HPG + j, 0, 0))
    return cur, prev, nxt, slot, slot_next, head


def _att_fwd_group(z_att, gq3, gk3, slopes3, gi, d, nb):
    S = z_att.shape[0]
    nsb = S // (BLK * d * nb)
    scale = DH ** -0.5

    def body(q_ref, k_ref, v_ref, kp_ref, vp_ref, gq_ref, gk_ref, sl_ref, o_ref, l_ref):
        slope = sl_ref[...][:, :1]
        valid0, bias = _att_mask_bias(slope, d, pl.program_id(1) == 0)
        valid_in, _ = _att_mask_bias(slope, d, None)
        gq, gk = gq_ref[...], gk_ref[...]
        for r in range(d):
            kp = _head_norm(_rows(kp_ref, r, 0, d), gk)[0]
            vp = _rows(vp_ref, r, 0, d).astype(BF16)
            for b in range(nb):
                q = _head_norm(_rows(q_ref, r, b, d), gq)[0]
                kc = _head_norm(_rows(k_ref, r, b, d), gk)[0]
                vc = _rows(v_ref, r, b, d).astype(BF16)
                k2 = jnp.concatenate([kp, kc], axis=0)
                v2 = jnp.concatenate([vp, vc], axis=0)
                s = jnp.where(valid0 if b == 0 else valid_in, _dot(q, k2, _NT) * scale + bias, NEG)
                m = jnp.max(s, axis=-1, keepdims=True)
                p = jnp.exp(s - m)
                den = jnp.sum(p, axis=-1, keepdims=True)
                _put_rows(o_ref, r, b, d, _dot(p.astype(BF16), v2) / den)
                _put_rows(l_ref, r, b, d, jnp.broadcast_to(m + jnp.log(den), (BLK, DH)))
                kp, vp = kc, vc

    cur, prev, _, slot, _, head = _att_specs(gi, d, nb, S)
    return pl.pallas_call(
        body, name=f"att_fwd_g{gi}", grid=(HPG, nsb),
        in_specs=[cur(0), cur(1), cur(2), prev(1), prev(2), head, head, head],
        out_specs=[slot, slot],
        out_shape=[_sds((S, ATT_OUT_W), F32), _sds((S, ATT_OUT_W), F32)],
        compiler_params=_cparams(("parallel", "arbitrary")),
    )(z_att, z_att, z_att, z_att, z_att, gq3, gk3, slopes3)


def _att_combine(os_, ls_, tm):
    S = os_[0].shape[0]

    def body(o0, o1, o2, l0, l1, l2, oa_ref, lm_ref):
        a, b, c = l0[...], l1[...], l2[...]
        m = jnp.maximum(jnp.maximum(a, b), c)
        ea, eb, ec = jnp.exp(a - m), jnp.exp(b - m), jnp.exp(c - m)
        tot = ea + eb + ec
        oa_ref[...] = (ea * o0[...] + eb * o1[...] + ec * o2[...]) / tot
        lm_ref[...] = m + jnp.log(tot)

    spec = pl.BlockSpec((tm, ATT_OUT_W), lambda i: (i, 0))
    return pl.pallas_call(
        body, name="att_combine", grid=(S // tm,),
        in_specs=[spec] * 6, out_specs=[spec, spec],
        out_shape=[_sds((S, ATT_OUT_W), F32), _sds((S, ATT_OUT_W), F32)],
        compiler_params=_cparams(("parallel",)),
    )(*os_, *ls_)


def _ret_tables(lg):
    ri = lax.broadcasted_iota(jnp.int32, (CHUNK, CHUNK), 0)
    ci = lax.broadcasted_iota(jnp.int32, (CHUNK, CHUNK), 1)
    diff = (ri - ci).astype(F32)
    decay = jnp.where(diff >= 0, jnp.exp(lg * jnp.maximum(diff, 0.0)), 0.0)
    idx = lax.broadcasted_iota(jnp.int32, (CHUNK, 1), 0).astype(F32)
    xi = jnp.exp(lg * (idx + 1.0))
    zeta = jnp.exp(lg * (CHUNK - 1.0 - idx))
    return decay, xi, zeta, math.exp(lg * CHUNK)


def _ret_specs(nch, rev):
    idx = (lambda n: nch - 1 - n) if rev else (lambda n: n)
    qk = lambda off: pl.BlockSpec((CHUNK, RET_QK_W), lambda n: (idx(n), off // RET_QK_W))
    vv = lambda off: pl.BlockSpec((CHUNK, RET_V_W), lambda n: (idx(n), off // RET_V_W))
    par = pl.BlockSpec((1, RET_V_W), lambda n: (0, 0))
    wide = pl.BlockSpec((CHUNK, RET_V_W), lambda n: (idx(n), 0))
    st = pl.BlockSpec((RET_HEADS, None, RET_QK, RET_V), lambda n: (0, idx(n), 0, 0))
    return qk, vv, par, wide, st


def _ret_fwd(z_rest, gn_g, gn_b):
    S = z_rest.shape[0]
    nch = S // CHUNK

    def body(q_ref, k_ref, v_ref, gr_ref, g_ref, b_ref, or_ref, o_ref, st_ref, state):
        @pl.when(pl.program_id(0) == 0)
        def _():
            state[...] = jnp.zeros_like(state)

        for h in range(RET_HEADS):
            decay, xi, zeta, gch = _ret_tables(LOG_GAMMA[h])
            cq = slice(h * RET_QK, (h + 1) * RET_QK)
            cv = slice(h * RET_V, (h + 1) * RET_V)
            q = q_ref[:, cq]
            kc32 = k_ref[:, cq].astype(F32) * (RET_QK ** -0.5)
            kc = kc32.astype(BF16)
            v = v_ref[:, cv]
            st = state[h]
            stb = st.astype(BF16)
            st_ref[h] = stb
            s = _dot(q, kc, _NT) * decay
            o = _dot(s.astype(BF16), v) + _dot(q, stb) * xi
            state[h] = st * gch + _dot((kc32 * zeta).astype(BF16), v, _TN)
            mu = jnp.mean(o, axis=-1, keepdims=True)
            cen = o - mu
            yh = cen * lax.rsqrt(jnp.mean(cen * cen, axis=-1, keepdims=True) + EPS)
            gr = gr_ref[:, cv].astype(F32)
            or_ref[:, cv] = ((yh * g_ref[:, cv] + b_ref[:, cv]) * (gr * jax.nn.sigmoid(gr))).astype(BF16)
            o_ref[:, cv] = o

    qk, vv, par, wide, st = _ret_specs(nch, False)
    return pl.pallas_call(
        body, name="ret_fwd", grid=(nch,),
        in_specs=[qk(R_Q), qk(R_K), vv(R_V), vv(R_G), par, par],
        out_specs=[wide, wide, st],
        out_shape=[_sds((S, RET_V_W), BF16), _sds((S, RET_V_W), F32), _sds((RET_HEADS, nch, RET_QK, RET_V), BF16)],
        scratch_shapes=[pltpu.VMEM((RET_HEADS, RET_QK, RET_V), F32)],
        compiler_params=_cparams(("arbitrary",)),
    )(z_rest, z_rest, z_rest, z_rest, gn_g, gn_b)


def _merge_fwd(o_a, o_r, z_rest, x, wpa, wpb, wout, g2, tm):
    S = x.shape[0]

    def body(oa_ref, or_ref, ga_ref, gb_ref, x_ref, wpa_ref, wpb_ref, wo_ref, g2_ref,
             x1_ref, y_ref, pa_ref, pb_ref, xn2_ref):
        pa = _dot(oa_ref[...].astype(BF16), wpa_ref[...])
        pb = _dot(or_ref[...], wpb_ref[...])
        y = jax.nn.sigmoid(ga_ref[...].astype(F32)) * pa + jax.nn.sigmoid(gb_ref[...].astype(F32)) * pb
        yb = y.astype(BF16)
        x1 = x_ref[...] + _dot(yb, wo_ref[...])
        x1_ref[...] = x1
        y_ref[...] = yb
        pa_ref[...] = pa.astype(BF16)
        pb_ref[...] = pb.astype(BF16)
        xn2_ref[...] = (x1 * _rms_rows(x1) * g2_ref[...]).astype(BF16)

    row = lambda w: pl.BlockSpec((tm, w), lambda i: (i, 0))
    full = lambda a: pl.BlockSpec(a.shape, lambda i: (0, 0))
    return pl.pallas_call(
        body, name="merge_fwd", grid=(S // tm,),
        in_specs=[row(ATT_OUT_W), row(RET_V_W),
                  pl.BlockSpec((tm, D_MODEL), lambda i: (i, R_GA // D_MODEL)),
                  pl.BlockSpec((tm, D_MODEL), lambda i: (i, R_GB // D_MODEL)),
                  row(D_MODEL), full(wpa), full(wpb), full(wout), full(g2)],
        out_specs=[row(D_MODEL)] * 5,
        out_shape=[_sds((S, D_MODEL), F32)] + [_sds((S, D_MODEL), BF16)] * 4,
        compiler_params=_cparams(("parallel",)),
    )(o_a, o_r, z_rest, z_rest, x, wpa, wpb, wout, g2)


def _rms_bwd(dy, xv, g):
    r = _rms_rows(xv)
    xh = xv * r
    dg = dy * g
    dx = r * (dg - xh * jnp.mean(dg * xh, axis=-1, keepdims=True))
    return dx, jnp.sum(dy * xh, axis=0, keepdims=True)


def _ret_bwd(do_r, o_pre, states, z_rest, dz_rest, gn_g, gn_b):
    S = z_rest.shape[0]
    nch = S // CHUNK

    def body(do_ref, o_ref, st_ref, q_ref, k_ref, v_ref, gr_ref, g_ref, b_ref, dz_in,
             dz_ref, dg_ref, db_ref, gst):
        del dz_in
        first = pl.program_id(0) == 0

        @pl.when(first)
        def _():
            gst[...] = jnp.zeros_like(gst)

        dgs, dbs = [], []
        for h in range(RET_HEADS):
            decay, xi, zeta, gch = _ret_tables(LOG_GAMMA[h])
            cq = slice(h * RET_QK, (h + 1) * RET_QK)
            cv = slice(h * RET_V, (h + 1) * RET_V)
            o = o_ref[:, cv]
            mu = jnp.mean(o, axis=-1, keepdims=True)
            cen = o - mu
            rstd = lax.rsqrt(jnp.mean(cen * cen, axis=-1, keepdims=True) + EPS)
            yh = cen * rstd
            gam = g_ref[:, cv]
            y = yh * gam + b_ref[:, cv]
            gr = gr_ref[:, cv].astype(F32)
            sg = jax.nn.sigmoid(gr)
            dout = do_ref[:, cv]
            dy = dout * (gr * sg)
            dz_ref[:, R_G + h * RET_V:R_G + (h + 1) * RET_V] = (dout * y * (sg * (1.0 + gr * (1.0 - sg)))).astype(BF16)
            dgs.append(jnp.sum(dy * yh, axis=0, keepdims=True))
            dbs.append(jnp.sum(dy, axis=0, keepdims=True))
            dyh = dy * gam
            do = rstd * (dyh - jnp.mean(dyh, axis=-1, keepdims=True)
                         - yh * jnp.mean(dyh * yh, axis=-1, keepdims=True))

            q = q_ref[:, cq]
            kc32 = k_ref[:, cq].astype(F32) * (RET_QK ** -0.5)
            kc = kc32.astype(BF16)
            v = v_ref[:, cv]
            dob = do.astype(BF16)
            a = (_dot(q, kc, _NT) * decay).astype(BF16)
            da = (_dot(dob, v, _NT) * decay).astype(BF16)
            dcross = (do * xi).astype(BF16)
            g_next = gst[h]
            gb = g_next.astype(BF16)
            dq = _dot(da, kc) + _dot(dcross, st_ref[h], _NT)
            dkc = _dot(da, q, _TN)
            dkz = _dot(v, gb, _NT)
            dv = _dot(a, dob, _TN) + _dot((kc32 * zeta).astype(BF16), gb)
            gst[h] = g_next * gch + _dot(q, dcross, _TN)
            dz_ref[:, R_Q + h * RET_QK:R_Q + (h + 1) * RET_QK] = dq.astype(BF16)
            dz_ref[:, R_K + h * RET_QK:R_K + (h + 1) * RET_QK] = ((dkc + dkz * zeta) * (RET_QK ** -0.5)).astype(BF16)
            dz_ref[:, R_V + h * RET_V:R_V + (h + 1) * RET_V] = dv.astype(BF16)
        _acc_rows(dg_ref, jnp.concatenate(dgs, axis=1), first)
        _acc_rows(db_ref, jnp.concatenate(dbs, axis=1), first)

    qk, vv, par, wide, st = _ret_specs(nch, True)
    return pl.pallas_call(
        body, name="ret_bwd", grid=(nch,),
        in_specs=[wide, wide, st, qk(R_Q), qk(R_K), vv(R_V), vv(R_G), par, par, pl.BlockSpec(memory_space=pl.ANY)],
        out_specs=[pl.BlockSpec((CHUNK, RET_COLS), lambda n: (nch - 1 - n, 0)), par, par],
        out_shape=[_sds(dz_rest.shape, BF16), _sds((1, RET_V_W), F32), _sds((1, RET_V_W), F32)],
        input_output_aliases={9: 0},
        scratch_shapes=[pltpu.VMEM((RET_HEADS, RET_QK, RET_V), F32)],
        compiler_params=_cparams(("arbitrary",)),
    )(do_r, o_pre, states, z_rest, z_rest, z_rest, z_rest, gn_g, gn_b, dz_rest)


def _att_probs(q, k, lmix, valid, bias):
    s = _dot(q, k, _NT) * (DH ** -0.5) + bias
    return jnp.where(valid, jnp.exp(jnp.where(valid, s, NEG) - lmix), 0.0)


def _att_bwd_group(z_att, dz_att, do_a, delta, lmix, gq3, gk3, slopes3, gi, d, nb):
    S = z_att.shape[0]
    sb = BLK * d * nb
    nsb = S // sb
    scale = DH ** -0.5
    aliased = dz_att is not None

    def body(q_ref, k_ref, v_ref, kp_ref, vp_ref, qn_ref, do_ref, don_ref, dl_ref, dln_ref, lm_ref, lmn_ref,
             gq_ref, gk_ref, sl_ref, *rest):
        dz_ref, dgq_ref, dgk_ref, stage = rest[-4:]
        n = pl.program_id(1)
        slope = sl_ref[...][:, :1]
        valid0, bias = _att_mask_bias(slope, d, n == 0)
        valid_in, _ = _att_mask_bias(slope, d, None)
        qi = lax.broadcasted_iota(jnp.int32, (BLK, BLK), 0)
        kj = lax.broadcasted_iota(jnp.int32, (BLK, BLK), 1)
        dist_n = BLK + qi - kj
        valid_n_in = dist_n <= BLK
        valid_n_last = valid_n_in & (n < nsb - 1)
        bias_n = -slope * (dist_n * d).astype(F32)
        gq, gk = gq_ref[...], gk_ref[...]
        dgq = jnp.zeros((1, DH), F32)
        dgk = jnp.zeros((1, DH), F32)
        for r in range(d):
            memo = {}

            def get(kind, b):
                if (kind, b) not in memo:
                    inner = 0 <= b < nb
                    bb = b if inner else 0
                    if kind == "q":
                        val = _head_norm(_rows(q_ref if inner else qn_ref, r, bb, d), gq)
                    elif kind == "k":
                        val = _head_norm(_rows(k_ref if inner else kp_ref, r, bb, d), gk)
                    elif kind == "v":
                        val = _rows(v_ref if inner else vp_ref, r, bb, d).astype(BF16)
                    elif kind == "do":
                        val = _rows(do_ref if inner else don_ref, r, bb, d).astype(BF16)
                    elif kind == "dl":
                        val = _rows(dl_ref if inner else dln_ref, r, bb, d)[:, :1]
                    else:
                        val = _rows(lm_ref if inner else lmn_ref, r, bb, d)[:, :1]
                    memo[(kind, b)] = val
                return memo[(kind, b)]

            for b in range(nb):
                q, qh, qr = get("q", b)
                kc, kh, kr = get("k", b)
                qn = get("q", b + 1)[0]
                vc, do_c, do_n = get("v", b), get("do", b), get("do", b + 1)
                k2 = jnp.concatenate([get("k", b - 1)[0], kc], axis=0)
                v2 = jnp.concatenate([get("v", b - 1), vc], axis=0)
                p = _att_probs(q, k2, get("lm", b), valid0 if b == 0 else valid_in, bias)
                ds = (p * (_dot(do_c, v2, _NT) - get("dl", b)) * scale).astype(BF16)
                dx, dg = _head_norm_bwd(_dot(ds, k2), qh, qr, gq)
                _put_rows(stage.at[0], r, b, d, dx)
                dgq = dgq + dg
                p_n = _att_probs(qn, kc, get("lm", b + 1), valid_n_last if b == nb - 1 else valid_n_in, bias_n)
                ds_n = (p_n * (_dot(do_n, vc, _NT) - get("dl", b + 1)) * scale).astype(BF16)
                dk = _dot(ds[:, BLK:], q, _TN) + _dot(ds_n, qn, _TN)
                dv = _dot(p[:, BLK:].astype(BF16), do_c, _TN) + _dot(p_n.astype(BF16), do_n, _TN)
                dx, dg = _head_norm_bwd(dk, kh, kr, gk)
                _put_rows(stage.at[1], r, b, d, dx)
                _put_rows(stage.at[2], r, b, d, dv)
                dgk = dgk + dg
        for kind in range(3):
            dz_ref[:, kind * DH:(kind + 1) * DH] = stage[kind].astype(BF16)
        _acc_rows(dgq_ref, dgq, n == 0)
        _acc_rows(dgk_ref, dgk, n == 0)

    cur, prev, nxt, slot, slot_next, head = _att_specs(gi, d, nb, S)
    gain = pl.BlockSpec((None, 1, DH), lambda j, n: (j, 0, 0))
    in_specs = [cur(0), cur(1), cur(2), prev(1), prev(2), nxt(0),
                slot, slot_next, slot, slot_next, slot, slot_next, head, head, head]
    args = [z_att] * 6 + [do_a, do_a, delta, delta, lmix, lmix, gq3, gk3, slopes3]
    if aliased:
        in_specs.append(pl.BlockSpec(memory_space=pl.ANY))
        args.append(dz_att)
    return pl.pallas_call(
        body, name=f"att_bwd_g{gi}", grid=(HPG, nsb),
        in_specs=in_specs,
        out_specs=[pl.BlockSpec((sb, 3 * DH), lambda j, n: (n, gi * HPG + j)), gain, gain],
        out_shape=[_sds(z_att.shape, BF16), _sds((HPG, 1, DH), F32), _sds((HPG, 1, DH), F32)],
        input_output_aliases={len(args) - 1: 0} if aliased else {},
        scratch_shapes=[pltpu.VMEM((3, sb, DH), F32)],
        compiler_params=_cparams(("parallel", "arbitrary")),
    )(*args)


def _att_col_blocks():
    return [kind * ATT_HEADS + h for h in range(ATT_HEADS) for kind in range(3)]


def _local_step(x, target, norm1_g, q_norm_g, k_norm_g, gn_g, gn_b, norm2_g, w_in, w_pa, w_pb, w_out, w_up, w_down):
    S = x.shape[0]
    tm = min(512, S)
    tl = min(1024, S)
    tk = min(2048, S)
    blocks = _att_col_blocks()
    w_att = jnp.concatenate([w_in[:, p * DH:(p + 1) * DH] for p in blocks], axis=1)
    w_rest = w_in[:, 3 * ATT_W:]
    w_att_t, w_rest_t, w_up_t, w_down_t = w_att.T, w_rest.T, w_up.T, w_down.T
    w_out_t, w_pa_t, w_pb_t = w_out.T, w_pa.T, w_pb.T
    gq3 = q_norm_g.reshape(ATT_HEADS, 1, DH)
    gk3 = k_norm_g.reshape(ATT_HEADS, 1, DH)
    slopes3 = jnp.asarray(np.broadcast_to(ALIBI[:, None, None], (ATT_HEADS, 1, DH)).copy())

    xn = _rmsnorm_fwd(x, norm1_g, tm)
    z_att = _mm("in_proj_att", xn, w_att, "nn", tl, ATT_W, D_MODEL,
                [(_sds((S, 3 * ATT_W), F32), _tile_ij(tl, ATT_W))], _epi_store(F32))[0]
    z_rest = _mm("in_proj_rest", xn, w_rest, "nn", tl, 1024, D_MODEL,
                 [(_sds((S, REST_W), BF16), _tile_ij(tl, 1024))], _epi_store(BF16))[0]
    os_, ls_ = [], []
    for gi, ((_, d), nb) in enumerate(zip(ATT_GROUPS, ATT_BLOCKS_PER_STEP)):
        o, l = _att_fwd_group(z_att, gq3, gk3, slopes3, gi, d, nb)
        os_.append(o)
        ls_.append(l)
    o_a, lmix = _att_combine(os_, ls_, tm)
    o_r, o_pre, states = _ret_fwd(z_rest, gn_g, gn_b)
    x1, y, pa, pb, xn2 = _merge_fwd(o_a, o_r, z_rest, x, w_pa, w_pb, w_out, norm2_g, min(256, S))

    def epi_up(acc, ex, out):
        r = jnp.maximum(acc, 0.0)
        out[0][...] = (r * r).astype(BF16)
        out[1][...] = r.astype(BF16)

    h, relu_u = _mm("mlp_up", xn2, w_up, "nn", tl, 1024, D_MODEL,
                    [(_sds((S, D_FF), BF16), _tile_ij(tl, 1024)), (_sds((S, D_FF), BF16), _tile_ij(tl, 1024))], epi_up)

    def epi_down(acc, ex, out):
        diff = ex[0][...] + acc - ex[1][...]
        out[0][...] = diff * (1.0 / D_MODEL)
        out[1][...] = jnp.broadcast_to(jnp.sum(diff * diff) * (1.0 / (8 * LANES)), (8, LANES))

    row_tile = _tile_ij(tm, D_MODEL)
    dx2, loss_parts = _mm(
        "mlp_down_loss", h, w_down, "nn", tm, D_MODEL, 2048,
        [(_sds((S, D_MODEL), F32), row_tile),
         (_sds((S // tm * 8, LANES), F32), pl.BlockSpec((8, LANES), lambda i, j, k: (i, 0)))],
        epi_down, extras=[(x1, row_tile), (target, row_tile)])
    loss_local = jnp.sum(loss_parts) * (0.5 / D_MODEL)

    def epi_du(acc, ex, out):
        out[0][...] = (acc * (2.0 * ex[0][...].astype(F32))).astype(BF16)

    du = _mm("mlp_down_bwd", dx2, w_down_t, "nn", tm, 1024, D_MODEL,
             [(_sds((S, D_FF), BF16), _tile_ij(tm, 1024))], epi_du, extras=[(relu_u, _tile_ij(tm, 1024))])[0]
    gw_down = _mm("gw_down", h, dx2, "tn", 1024, D_MODEL, tk,
                  [(_sds((D_FF, D_MODEL), BF16), _tile_ij(1024, D_MODEL))], _epi_store(BF16))[0]
    gw_up = _mm("gw_up", xn2, du, "tn", D_MODEL, 512, tk,
                [(_sds((N_DEV, D_MODEL, 512), BF16), pl.BlockSpec((None, D_MODEL, 512), lambda i, j, k: (j, 0, 0)))],
                _epi_store(BF16))[0]

    vec = pl.BlockSpec((1, D_MODEL), lambda i, j, k: (0, 0))
    seq_sem = ("arbitrary", "arbitrary", "arbitrary")

    def epi_norm2(acc, ex, out):
        dx, dg = _rms_bwd(acc, ex[0][...], ex[2][...])
        out[0][...] = ex[1][...] + dx
        _acc_rows(out[1], dg, pl.program_id(0) == 0)

    dx1, g_norm2 = _mm(
        "mlp_up_bwd", du, w_up_t, "nn", tm, D_MODEL, 2048,
        [(_sds((S, D_MODEL), F32), row_tile), (_sds((1, D_MODEL), F32), vec)],
        epi_norm2, extras=[(x1, row_tile), (dx2, row_tile), (norm2_g, vec)], sem=seq_sem)

    def epi_dy(acc, ex, out):
        sa = jax.nn.sigmoid(ex[0][...].astype(F32))
        sb = jax.nn.sigmoid(ex[1][...].astype(F32))
        out[0][...] = (acc * sa).astype(BF16)
        out[1][...] = (acc * sb).astype(BF16)
        out[2][:, :D_MODEL] = (acc * ex[2][...].astype(F32) * (sa * (1.0 - sa))).astype(BF16)
        out[2][:, D_MODEL:] = (acc * ex[3][...].astype(F32) * (sb * (1.0 - sb))).astype(BF16)

    ga_spec = pl.BlockSpec((tm, D_MODEL), lambda i, j, k: (i, R_GA // D_MODEL))
    gb_spec = pl.BlockSpec((tm, D_MODEL), lambda i, j, k: (i, R_GB // D_MODEL))
    gates_spec = pl.BlockSpec((tm, 2 * D_MODEL), lambda i, j, k: (i, R_GA // (2 * D_MODEL)))
    dpa, dpb, dz_rest = _mm(
        "out_proj_bwd", dx1, w_out_t, "nn", tm, D_MODEL, D_MODEL,
        [(_sds((S, D_MODEL), BF16), row_tile), (_sds((S, D_MODEL), BF16), row_tile), (_sds((S, REST_W), BF16), gates_spec)],
        epi_dy, extras=[(z_rest, ga_spec), (z_rest, gb_spec), (pa, row_tile), (pb, row_tile)])
    gw_out = _mm("gw_out", y, dx1, "tn", D_MODEL, D_MODEL, tk,
                 [(_sds((D_MODEL, D_MODEL), BF16), _tile_ij(D_MODEL, D_MODEL))], _epi_store(BF16))[0]
    gw_pa = _mm("gw_proj_a", o_a, dpa, "tn", ATT_OUT_W, D_MODEL, tk,
                [(_sds((ATT_OUT_W, D_MODEL), BF16), _tile_ij(ATT_OUT_W, D_MODEL))], _epi_store(BF16))[0]
    gw_pb = _mm("gw_proj_b", o_r, dpb, "tn", 1024, D_MODEL, tk,
                [(_sds((RET_V_W, D_MODEL), BF16), _tile_ij(1024, D_MODEL))], _epi_store(BF16))[0]

    def epi_doa(acc, ex, out):
        out[0][...] = acc
        prod = acc * ex[0][...]
        out[1][...] = jnp.concatenate(
            [jnp.broadcast_to(jnp.sum(prod[:, j * DH:(j + 1) * DH], axis=-1, keepdims=True), (prod.shape[0], DH))
             for j in range(HPG)], axis=1)

    slot_tile = _tile_ij(tm, ATT_OUT_W)
    do_a, delta = _mm("proj_a_bwd", dpa, w_pa_t, "nn", tm, ATT_OUT_W, D_MODEL,
                      [(_sds((S, ATT_OUT_W), F32), slot_tile), (_sds((S, ATT_OUT_W), F32), slot_tile)],
                      epi_doa, extras=[(o_a, slot_tile)])
    do_r = _mm("proj_b_bwd", dpb, w_pb_t, "nn", tm, 1024, D_MODEL,
               [(_sds((S, RET_V_W), F32), _tile_ij(tm, 1024))], _epi_store(F32))[0]

    dz_rest, g_gn_g, g_gn_b = _ret_bwd(do_r, o_pre, states, z_rest, dz_rest, gn_g, gn_b)
    dz_att, gq_parts, gk_parts = None, [], []
    for gi, ((_, d), nb) in enumerate(zip(ATT_GROUPS, ATT_BLOCKS_PER_STEP)):
        dz_att, gq_p, gk_p = _att_bwd_group(z_att, dz_att, do_a, delta, lmix, gq3, gk3, slopes3, gi, d, nb)
        gq_parts.append(gq_p)
        gk_parts.append(gk_p)
    g_qn = jnp.concatenate(gq_parts, axis=0).reshape(1, ATT_HEADS, DH)
    g_kn = jnp.concatenate(gk_parts, axis=0).reshape(1, ATT_HEADS, DH)

    gw_att = _mm("gw_in_att", xn, dz_att, "tn", D_MODEL, ATT_W, tk,
                 [(_sds((D_MODEL, 3 * ATT_W), BF16), _tile_ij(D_MODEL, ATT_W))], _epi_store(BF16))[0]
    gw_rest = _mm("gw_in_rest", xn, dz_rest, "tn", D_MODEL, 1024, tk,
                  [(_sds((D_MODEL, REST_W), BF16), _tile_ij(D_MODEL, 1024))], _epi_store(BF16))[0]
    dxn_att = _mm("in_proj_att_bwd", dz_att, w_att_t, "nn", tm, D_MODEL, ATT_W,
                  [(_sds((S, D_MODEL), F32), row_tile)], _epi_store(F32))[0]

    def epi_norm1(acc, ex, out):
        dx, dg = _rms_bwd(acc + ex[0][...], ex[1][...], ex[3][...])
        out[0][...] = ex[2][...] + dx
        _acc_rows(out[1], dg, pl.program_id(0) == 0)

    grad_x, g_norm1 = _mm(
        "in_proj_rest_bwd", dz_rest, w_rest_t, "nn", tm, D_MODEL, 2048,
        [(_sds((S, D_MODEL), F32), row_tile), (_sds((1, D_MODEL), F32), vec)],
        epi_norm1, extras=[(dxn_att, row_tile), (x, row_tile), (dx1, row_tile), (norm1_g, vec)], sem=seq_sem)

    big = (gw_att, gw_rest, gw_pa, gw_pb, gw_out, gw_up, gw_down)
    small = (g_norm1, g_qn, g_kn, g_gn_g, g_gn_b, g_norm2)
    return loss_local, grad_x, big, small


def _position():
    return lax.axis_index("x"), lax.axis_index("y"), lax.axis_index("c")


def _other_chips(x, y):
    return [(1 - x, y), (x, 1 - y), (1 - x, 1 - y)]


_ANY = pl.BlockSpec(memory_space=pl.ANY)


def _all_gather(shards):
    nw = len(shards)

    def body(*refs):
        x_refs, out_refs = refs[:nw], refs[nw:2 * nw]
        send_sems, recv_sems, local_sems = refs[2 * nw:]
        x, y, c = _position()
        me, sibling = (x, y, c), (x, y, 1 - c)
        chips = _other_chips(x, y)

        def copy(w, k, block, to, src=None):
            px, py, pc = block
            rows = out_refs[w].at[4 * px + 2 * py + pc]
            return pltpu.make_async_remote_copy(
                src_ref=rows if src is None else src, dst_ref=rows,
                send_sem=send_sems.at[7 * w + k], recv_sem=recv_sems.at[7 * w + k], device_id=to, device_id_type=MESH)

        mine = [pltpu.make_async_copy(x_refs[w], out_refs[w].at[4 * x + 2 * y + c], local_sems.at[w]) for w in range(nw)]
        first = []
        for w in range(nw):
            mine[w].start()
            first.append(copy(w, 0, me, sibling, src=x_refs[w]))
            first += [copy(w, 1 + j, me, (*chip, c), src=x_refs[w]) for j, chip in enumerate(chips)]
        for cp in first:
            cp.start()
        passed = []
        for j, chip in enumerate(chips):
            for w in range(nw):
                copy(w, 1 + j, (*chip, c), me).wait_recv()
                fwd = copy(w, 4 + j, (*chip, c), sibling)
                fwd.start()
                passed.append(fwd)
        for w in range(nw):
            copy(w, 0, sibling, me).wait_recv()
            for j, chip in enumerate(chips):
                copy(w, 4 + j, (*chip, 1 - c), me).wait_recv()
        for cp in first + passed:
            cp.wait_send()
        for cp in mine:
            cp.wait()

    return pl.pallas_call(
        body, name="all_gather_weights",
        out_shape=[_sds((N_DEV,) + s.shape, s.dtype) for s in shards],
        in_specs=[_ANY] * nw, out_specs=[_ANY] * nw,
        scratch_shapes=[pltpu.SemaphoreType.DMA((7 * nw,)), pltpu.SemaphoreType.DMA((7 * nw,)),
                        pltpu.SemaphoreType.DMA((nw,))],
    )(*shards)


def _pair_exchange(grads, small):
    ng = len(grads)

    def body(*refs):
        g_refs, s_ref = refs[:ng], refs[ng]
        out_refs, s_out = refs[ng + 1:2 * ng + 1], refs[2 * ng + 1]
        send_sems, recv_sems, local_sem = refs[2 * ng + 2:]
        x, y, c = _position()
        me_id = 4 * x + 2 * y + c
        cps = [pltpu.make_async_remote_copy(
            src_ref=g_refs[w].at[:, 1 - c], dst_ref=out_refs[w], send_sem=send_sems.at[w], recv_sem=recv_sems.at[w],
            device_id=(x, y, 1 - c), device_id_type=MESH) for w in range(ng)]
        for cp in cps:
            cp.start()
        local = pltpu.make_async_copy(s_ref, s_out.at[me_id], local_sem)
        local.start()
        flips = [(a, b, e) for a in (0, 1) for b in (0, 1) for e in (0, 1)][1:]
        peers = [(x ^ a, y ^ b, c ^ e) for a, b, e in flips]
        s_cps = [pltpu.make_async_remote_copy(
            src_ref=s_ref, dst_ref=s_out.at[me_id], send_sem=send_sems.at[ng + k], recv_sem=recv_sems.at[ng + k],
            device_id=p, device_id_type=MESH) for k, p in enumerate(peers)]
        for cp in s_cps:
            cp.start()
        for cp in cps:
            cp.wait()
        for k, (px, py, pc) in enumerate(peers):
            pltpu.make_async_remote_copy(
                src_ref=s_ref, dst_ref=s_out.at[4 * px + 2 * py + pc], send_sem=send_sems.at[ng + k],
                recv_sem=recv_sems.at[ng + k], device_id=(px, py, pc), device_id_type=MESH).wait()
        local.wait()

    return pl.pallas_call(
        body, name="grad_pair_exchange",
        out_shape=[_sds((N_CHIP,) + g.shape[2:], g.dtype) for g in grads] + [_sds((N_DEV,) + small.shape, small.dtype)],
        in_specs=[_ANY] * (ng + 1), out_specs=[_ANY] * (ng + 1),
        scratch_shapes=[pltpu.SemaphoreType.DMA((ng + 7,)), pltpu.SemaphoreType.DMA((ng + 7,)), pltpu.SemaphoreType.DMA],
    )(*grads, small)


def _pair_sum(name, g, got, core, tr):
    n_chip, _, R, C = g.shape

    def body(c_ref, a_ref, b_ref, o_ref):
        del c_ref
        o_ref[...] = (a_ref[...].astype(F32) + b_ref[...].astype(F32)).astype(o_ref.dtype)

    return pl.pallas_call(
        body, name=name,
        grid_spec=pltpu.PrefetchScalarGridSpec(
            num_scalar_prefetch=1, grid=(n_chip, R // tr),
            in_specs=[pl.BlockSpec((None, None, tr, C), lambda ch, i, c_ref: (ch, c_ref[0], i, 0)),
                      pl.BlockSpec((None, tr, C), lambda ch, i, c_ref: (ch, i, 0))],
            out_specs=pl.BlockSpec((None, tr, C), lambda ch, i, c_ref: (ch, i, 0))),
        out_shape=_sds(got.shape, got.dtype),
        compiler_params=_cparams(("parallel", "parallel")),
    )(core, g, got)


def _chip_exchange(parts):
    ng = len(parts)

    def body(*refs):
        p_refs, out_refs = refs[:ng], refs[ng:2 * ng]
        send_sems, recv_sems, local_sems = refs[2 * ng:]
        x, y, c = _position()
        my_chip = 2 * x + y
        cps, locals_ = [], []
        for w in range(ng):
            lc = pltpu.make_async_copy(p_refs[w].at[my_chip], out_refs[w].at[my_chip], local_sems.at[w])
            lc.start()
            locals_.append(lc)
            for k, (cx, cy) in enumerate(_other_chips(x, y)):
                cp = pltpu.make_async_remote_copy(
                    src_ref=p_refs[w].at[2 * cx + cy], dst_ref=out_refs[w].at[my_chip],
                    send_sem=send_sems.at[3 * w + k], recv_sem=recv_sems.at[3 * w + k],
                    device_id=(cx, cy, c), device_id_type=MESH)
                cp.start()
                cps.append(cp)
        for cp in cps:
            cp.wait()
        for lc in locals_:
            lc.wait()

    return pl.pallas_call(
        body, name="grad_chip_exchange",
        out_shape=[_sds(p.shape, p.dtype) for p in parts],
        in_specs=[_ANY] * ng, out_specs=[_ANY] * ng,
        scratch_shapes=[pltpu.SemaphoreType.DMA((3 * ng,)), pltpu.SemaphoreType.DMA((3 * ng,)),
                        pltpu.SemaphoreType.DMA((ng,))],
    )(*parts)


def _adamw(name, parts, w, m, v, tr):
    n_parts = parts.shape[0]
    R, C = w.shape

    def body(p_ref, w_ref, m_ref, v_ref, g_ref, d_ref, mo_ref, vo_ref):
        g = p_ref[0].astype(F32)
        for i in range(1, n_parts):
            g = g + p_ref[i].astype(F32)
        m_new = ADAM_B1 * m_ref[...] + (1.0 - ADAM_B1) * g
        v_new = ADAM_B2 * v_ref[...] + (1.0 - ADAM_B2) * (g * g)
        m_hat = m_new / (1.0 - ADAM_B1 ** ADAM_STEP)
        v_hat = v_new / (1.0 - ADAM_B2 ** ADAM_STEP)
        g_ref[...] = g
        d_ref[...] = -ADAM_LR * (m_hat / (jnp.sqrt(v_hat) + ADAM_EPS) + ADAM_WD * w_ref[...])
        mo_ref[...] = m_new
        vo_ref[...] = v_new

    tile = pl.BlockSpec((tr, C), lambda i: (i, 0))
    return pl.pallas_call(
        body, name=name, grid=(R // tr,),
        in_specs=[pl.BlockSpec((n_parts, tr, C), lambda i: (0, i, 0)), tile, tile, tile],
        out_specs=[tile] * 4,
        out_shape=[_sds((R, C), F32)] * 4,
        compiler_params=_cparams(("parallel",)),
    )(parts, w, m, v)


def _flat_small(arrs):
    return jnp.concatenate([a.reshape(-1) for a in arrs]).reshape(SMALL_ROWS, LANES)


def _by_owner_cols(g):
    rows, cols = g.shape
    return g.reshape(rows, N_DEV, cols // N_DEV).transpose(1, 0, 2)


def _chip_core(g):
    return g.reshape((N_CHIP, 2) + g.shape[1:])


def kernel(x, norm1_g, w_in, q_norm_g, k_norm_g, ret_gn_g, ret_gn_b, w_proj_a, w_proj_b, w_out, norm2_g, w_up, w_down, loss_target, m_norm1_g, m_w_in, m_q_norm_g, m_k_norm_g, m_ret_gn_g, m_ret_gn_b, m_w_proj_a, m_w_proj_b, m_w_out, m_norm2_g, m_w_up, m_w_down, v_norm1_g, v_w_in, v_q_norm_g, v_k_norm_g, v_ret_gn_g, v_ret_gn_b, v_w_proj_a, v_w_proj_b, v_w_out, v_norm2_g, v_w_up, v_w_down):
    big_w = (w_in, w_proj_a, w_proj_b, w_out, w_up, w_down)
    big_m = (m_w_in, m_w_proj_a, m_w_proj_b, m_w_out, m_w_up, m_w_down)
    big_v = (v_w_in, v_w_proj_a, v_w_proj_b, v_w_out, v_w_up, v_w_down)
    small_w = (norm1_g, q_norm_g, k_norm_g, ret_gn_g, ret_gn_b, norm2_g)
    small_m = (m_norm1_g, m_q_norm_g, m_k_norm_g, m_ret_gn_g, m_ret_gn_b, m_norm2_g)
    small_v = (v_norm1_g, v_q_norm_g, v_k_norm_g, v_ret_gn_g, v_ret_gn_b, v_norm2_g)

    g_in, g_pa, g_pb, g_out, g_up, g_down = _all_gather([w[0].astype(BF16) for w in big_w])
    cols = lambda g: g.transpose(1, 0, 2).reshape(g.shape[1], -1)
    rows = lambda g: g.reshape(-1, g.shape[2])
    full_w = (cols(g_in), cols(g_pa), rows(g_pb), rows(g_out), cols(g_up), rows(g_down))

    loss_local, grad_x, big_g, small_g = _local_step(
        x[0], loss_target[0], norm1_g, q_norm_g[0], k_norm_g[0], ret_gn_g, ret_gn_b, norm2_g, *full_w)
    gw_att, gw_rest, gw_pa, gw_pb, gw_out, gw_up, gw_down = big_g

    inv = np.argsort(np.asarray(_att_col_blocks()))
    gw_in = jnp.concatenate([gw_att[:, p * DH:(p + 1) * DH] for p in inv] + [gw_rest], axis=1)
    owed = [_chip_core(_by_owner_cols(gw_in)), _chip_core(_by_owner_cols(gw_pa)),
            _chip_core(gw_pb.reshape(N_DEV, -1, D_MODEL)), _chip_core(gw_out.reshape(N_DEV, -1, D_MODEL)),
            _chip_core(gw_up), _chip_core(gw_down.reshape(N_DEV, -1, D_MODEL))]
    names = ("w_in", "w_proj_a", "w_proj_b", "w_out", "w_up", "w_down")
    *got, small_all = _pair_exchange(owed, _flat_small(small_g))
    core = lax.axis_index("c").astype(jnp.int32).reshape(1)
    chip_sums = [_pair_sum(f"pair_sum_{n}", g, r, core, min(256, g.shape[2])) for n, g, r in zip(names, owed, got)]
    parts = _chip_exchange(chip_sums)

    res = {}
    for n, p, w, m, v in zip(names, parts, big_w, big_m, big_v):
        outs = _adamw(f"adamw_{n}", p, w[0], m[0], v[0], min(128, w.shape[1]))
        res[n] = [o[None] for o in outs]
    s_outs = _adamw("adamw_small", small_all, _flat_small(small_w), _flat_small(small_m), _flat_small(small_v), SMALL_ROWS)
    small_names = ("norm1_g", "q_norm_g", "k_norm_g", "ret_gn_g", "ret_gn_b", "norm2_g")
    for n in small_names:
        res[n] = []
    for o in s_outs:
        flat, off = o.reshape(-1), 0
        for n, w in zip(small_names, small_w):
            res[n].append(flat[off:off + w.size].reshape(w.shape))
            off += w.size

    order = ("norm1_g", "w_in", "q_norm_g", "k_norm_g", "ret_gn_g", "ret_gn_b", "w_proj_a", "w_proj_b", "w_out",
             "norm2_g", "w_up", "w_down")
    loss = lax.psum(loss_local, MESH_AXES)
    return (loss, grad_x[None], *[res[n][0] for n in order], *[res[n][1] for n in order],
            *[res[n][2] for n in order], *[res[n][3] for n in order])
```

```python
import math

import numpy as np
import jax
import jax.numpy as jnp
from jax import lax
from jax.experimental import pallas as pl
from jax.experimental.pallas import tpu as pltpu

F32 = jnp.float32
BF16 = jnp.bfloat16

D_MODEL = 1024
ATT_GROUPS = ((128, 1), (512, 4), (2048, 16))
ATT_BLOCKS_PER_STEP = (8, 1, 1)
HPG = 4
ATT_HEADS = 12
DH = 128
BLK = 128
ATT_W = ATT_HEADS * DH
ATT_OUT_W = HPG * DH
RET_HEADS = 4
RET_QK = 256
RET_V = 512
RET_QK_W = RET_HEADS * RET_QK
RET_V_W = RET_HEADS * RET_V
CHUNK = 128
D_FF = 4096
IN_W = 12800
REST_W = IN_W - 3 * ATT_W
RET_COLS = 6144
EPS = 1e-6
ADAM_LR, ADAM_B1, ADAM_B2, ADAM_EPS, ADAM_WD, ADAM_STEP = 0.001, 0.9, 0.999, 1e-08, 0.01, 10
N_DEV = 8
N_CHIP = 4
MESH_AXES = ("x", "y", "c")
MESH = pl.DeviceIdType.MESH
VMEM_LIMIT = 56 * 1024 * 1024
LANES = 128
NEG = -1e30

_NN = (((1,), (0,)), ((), ()))
_NT = (((1,), (1,)), ((), ()))
_TN = (((0,), (0,)), ((), ()))

R_Q, R_K, R_V, R_G, R_GA, R_GB = 0, 1024, 2048, 4096, 6144, 7168

LOG_GAMMA = [float(v) for v in np.log(1.0 - 2.0 ** (-5.0 - np.arange(RET_HEADS, dtype=np.float32))).astype(np.float32)]
ALIBI = np.asarray(2.0 ** (-8.0 * np.arange(1, ATT_HEADS + 1, dtype=np.float32) / ATT_HEADS), np.float32)

SMALL_ROWS = (1024 + 1536 + 1536 + 2048 + 2048 + 1024) // LANES


def _dot(a, b, dims=_NN):
    return lax.dot_general(a, b, dims, preferred_element_type=F32)


def _cparams(sem):
    return pltpu.CompilerParams(dimension_semantics=sem, vmem_limit_bytes=VMEM_LIMIT)


def _sds(shape, dtype):
    return jax.ShapeDtypeStruct(shape, dtype)


def _mm(name, a, b, mode, tm, tn, tk, outs, epi, extras=(), b_pro=None,
        sem=("parallel", "parallel", "arbitrary")):
    if mode == "nn":
        (M, K), (_, N) = a.shape, b.shape
        a_spec = pl.BlockSpec((tm, tk), lambda i, j, k: (i, k))
        dims = _NN
    else:
        (K, M), (_, N) = a.shape, b.shape
        a_spec = pl.BlockSpec((tk, tm), lambda i, j, k: (k, i))
        dims = _TN
    assert M % tm == 0 and N % tn == 0 and K % tk == 0, (name, M, N, K, tm, tn, tk)
    nk = K // tk
    whole_b = dict(pipeline_mode=pl.Buffered(1)) if (nk == 1 and N == tn) else {}
    b_spec = pl.BlockSpec((tk, tn), lambda i, j, k: (k, j), **whole_b)
    n_ex, n_out = len(extras), len(outs)

    def body(a_ref, b_ref, *rest):
        ex, out = rest[:n_ex], rest[n_ex:n_ex + n_out]
        bv = b_ref[...]
        if b_pro is not None:
            bv = b_pro(bv)
        part = _dot(a_ref[...].astype(BF16), bv.astype(BF16), dims)
        if nk == 1:
            epi(part, ex, out)
        else:
            acc_ref = rest[n_ex + n_out]
            k = pl.program_id(2)

            @pl.when(k == 0)
            def _():
                acc_ref[...] = part

            @pl.when(k > 0)
            def _():
                acc_ref[...] += part

            @pl.when(k == nk - 1)
            def _():
                epi(acc_ref[...], ex, out)

    return pl.pallas_call(
        body,
        name=name,
        grid=(M // tm, N // tn, nk),
        in_specs=[a_spec, b_spec] + [s for _, s in extras],
        out_specs=[s for _, s in outs],
        out_shape=[o for o, _ in outs],
        scratch_shapes=[pltpu.VMEM((tm, tn), F32)] if nk > 1 else [],
        compiler_params=_cparams(sem),
    )(a, b, *[e for e, _ in extras])


def _tile_ij(tm, tn):
    return pl.BlockSpec((tm, tn), lambda i, j, k: (i, j))


def _epi_store(dtype):
    def epi(acc, ex, out):
        out[0][...] = acc.astype(dtype)
    return epi


def _rms_rows(x):
    return lax.rsqrt(jnp.mean(x * x, axis=-1, keepdims=True) + EPS)


def _acc_rows(ref, part, first):
    @pl.when(first)
    def _():
        ref[...] = part

    @pl.when(jnp.logical_not(first))
    def _():
        ref[...] += part


def _rmsnorm_fwd(x, g, tm):
    S, Dm = x.shape

    def body(x_ref, g_ref, o_ref):
        xv = x_ref[...]
        o_ref[...] = (xv * _rms_rows(xv) * g_ref[...]).astype(BF16)

    return pl.pallas_call(
        body, name="rmsnorm1_fwd", grid=(S // tm,),
        in_specs=[pl.BlockSpec((tm, Dm), lambda i: (i, 0)), pl.BlockSpec((1, Dm), lambda i: (0, 0))],
        out_specs=pl.BlockSpec((tm, Dm), lambda i: (i, 0)),
        out_shape=_sds((S, Dm), BF16),
        compiler_params=_cparams(("parallel",)),
    )(x, g)


def _rows(ref, r, b, d):
    if d == 1:
        return ref[b * BLK:(b + 1) * BLK, :]
    return ref[pl.ds(b * BLK * d + r, BLK, stride=d), :]


def _put_rows(ref, r, b, d, val):
    if d == 1:
        ref[b * BLK:(b + 1) * BLK, :] = val
    else:
        ref[pl.ds(b * BLK * d + r, BLK, stride=d), :] = val


def _head_norm(x, g):
    r = _rms_rows(x)
    xh = x * r
    return (xh * g).astype(BF16), xh, r


def _head_norm_bwd(dyn, xh, r, g):
    dxh = dyn * g
    dx = r * (dxh - xh * jnp.mean(dxh * xh, axis=-1, keepdims=True))
    return dx, jnp.sum(dyn * xh, axis=0, keepdims=True)


def _att_mask_bias(slope, d, first):
    qi = lax.broadcasted_iota(jnp.int32, (BLK, 2 * BLK), 0)
    kj = lax.broadcasted_iota(jnp.int32, (BLK, 2 * BLK), 1)
    dist = BLK + qi - kj
    valid = (dist >= 0) & (dist <= BLK)
    if first is not None:
        valid = valid & (jnp.logical_not(first) | (kj >= BLK))
    bias = -slope * (dist * d).astype(F32)
    return valid, bias


def _att_specs(gi, d, nb, S):
    span = BLK * d
    sb = span * nb
    nspan = S // span
    before = lambda n: jnp.maximum(n * nb - 1, 0)
    after = lambda n: jnp.minimum((n + 1) * nb, nspan - 1)
    zcol = lambda j, kind: 3 * (gi * HPG + j) + kind
    cur = lambda kind: pl.BlockSpec((sb, DH), lambda j, n: (n, zcol(j, kind)))
    prev = lambda kind: pl.BlockSpec((span, DH), lambda j, n: (before(n), zcol(j, kind)))
    nxt = lambda kind: pl.BlockSpec((span, DH), lambda j, n: (after(n), zcol(j, kind)))
    slot = pl.BlockSpec((sb, DH), lambda j, n: (n, j))
    slot_next = pl.BlockSpec((span, DH), lambda j, n: (after(n), j))
    head = pl.BlockSpec((None, 1, DH), lambda j, n: (gi * HPG + j, 0, 0))
    return cur, prev, nxt, slot, slot_next, head


def _att_fwd_group(z_att, gq3, gk3, slopes3, gi, d, nb):
    S = z_att.shape[0]
    nsb = S // (BLK * d * nb)
    scale = DH ** -0.5

    def body(q_ref, k_ref, v_ref, kp_ref, vp_ref, gq_ref, gk_ref, sl_ref, o_ref, l_ref):
        slope = sl_ref[...][:, :1]
        valid0, bias = _att_mask_bias(slope, d, pl.program_id(1) == 0)
        valid_in, _ = _att_mask_bias(slope, d, None)
        gq, gk = gq_ref[...], gk_ref[...]
        for r in range(d):
            kp = _head_norm(_rows(kp_ref, r, 0, d), gk)[0]
            vp = _rows(vp_ref, r, 0, d).astype(BF16)
            for b in range(nb):
                q = _head_norm(_rows(q_ref, r, b, d), gq)[0]
                kc = _head_norm(_rows(k_ref, r, b, d), gk)[0]
                vc = _rows(v_ref, r, b, d).astype(BF16)
                k2 = jnp.concatenate([kp, kc], axis=0)
                v2 = jnp.concatenate([vp, vc], axis=0)
                s = jnp.where(valid0 if b == 0 else valid_in, _dot(q, k2, _NT) * scale + bias, NEG)
                m = jnp.max(s, axis=-1, keepdims=True)
                p = jnp.exp(s - m)
                den = jnp.sum(p, axis=-1, keepdims=True)
                _put_rows(o_ref, r, b, d, _dot(p.astype(BF16), v2) / den)
                _put_rows(l_ref, r, b, d, jnp.broadcast_to(m + jnp.log(den), (BLK, DH)))
                kp, vp = kc, vc

    cur, prev, _, slot, _, head = _att_specs(gi, d, nb, S)
    return pl.pallas_call(
        body, name=f"att_fwd_g{gi}", grid=(HPG, nsb),
        in_specs=[cur(0), cur(1), cur(2), prev(1), prev(2), head, head, head],
        out_specs=[slot, slot],
        out_shape=[_sds((S, ATT_OUT_W), F32), _sds((S, ATT_OUT_W), F32)],
        compiler_params=_cparams(("parallel", "arbitrary")),
    )(z_att, z_att, z_att, z_att, z_att, gq3, gk3, slopes3)


def _att_combine(os_, ls_, tm):
    S = os_[0].shape[0]

    def body(o0, o1, o2, l0, l1, l2, oa_ref, lm_ref):
        a, b, c = l0[...], l1[...], l2[...]
        m = jnp.maximum(jnp.maximum(a, b), c)
        ea, eb, ec = jnp.exp(a - m), jnp.exp(b - m), jnp.exp(c - m)
        tot = ea + eb + ec
        oa_ref[...] = (ea * o0[...] + eb * o1[...] + ec * o2[...]) / tot
        lm_ref[...] = m + jnp.log(tot)

    spec = pl.BlockSpec((tm, ATT_OUT_W), lambda i: (i, 0))
    return pl.pallas_call(
        body, name="att_combine", grid=(S // tm,),
        in_specs=[spec] * 6, out_specs=[spec, spec],
        out_shape=[_sds((S, ATT_OUT_W), F32), _sds((S, ATT_OUT_W), F32)],
        compiler_params=_cparams(("parallel",)),
    )(*os_, *ls_)


def _ret_tables(lg):
    ri = lax.broadcasted_iota(jnp.int32, (CHUNK, CHUNK), 0)
    ci = lax.broadcasted_iota(jnp.int32, (CHUNK, CHUNK), 1)
    diff = (ri - ci).astype(F32)
    decay = jnp.where(diff >= 0, jnp.exp(lg * jnp.maximum(diff, 0.0)), 0.0)
    idx = lax.broadcasted_iota(jnp.int32, (CHUNK, 1), 0).astype(F32)
    xi = jnp.exp(lg * (idx + 1.0))
    zeta = jnp.exp(lg * (CHUNK - 1.0 - idx))
    return decay, xi, zeta, math.exp(lg * CHUNK)


def _ret_specs(nch, rev):
    idx = (lambda n: nch - 1 - n) if rev else (lambda n: n)
    qk = lambda off: pl.BlockSpec((CHUNK, RET_QK_W), lambda n: (idx(n), off // RET_QK_W))
    vv = lambda off: pl.BlockSpec((CHUNK, RET_V_W), lambda n: (idx(n), off // RET_V_W))
    par = pl.BlockSpec((1, RET_V_W), lambda n: (0, 0))
    wide = pl.BlockSpec((CHUNK, RET_V_W), lambda n: (idx(n), 0))
    st = pl.BlockSpec((RET_HEADS, None, RET_QK, RET_V), lambda n: (0, idx(n), 0, 0))
    return qk, vv, par, wide, st


def _ret_fwd(z_rest, gn_g, gn_b):
    S = z_rest.shape[0]
    nch = S // CHUNK

    def body(q_ref, k_ref, v_ref, gr_ref, g_ref, b_ref, or_ref, o_ref, st_ref, state):
        @pl.when(pl.program_id(0) == 0)
        def _():
            state[...] = jnp.zeros_like(state)

        for h in range(RET_HEADS):
            decay, xi, zeta, gch = _ret_tables(LOG_GAMMA[h])
            cq = slice(h * RET_QK, (h + 1) * RET_QK)
            cv = slice(h * RET_V, (h + 1) * RET_V)
            q = q_ref[:, cq]
            kc32 = k_ref[:, cq].astype(F32) * (RET_QK ** -0.5)
            kc = kc32.astype(BF16)
            v = v_ref[:, cv]
            st = state[h]
            stb = st.astype(BF16)
            st_ref[h] = stb
            s = _dot(q, kc, _NT) * decay
            o = _dot(s.astype(BF16), v) + _dot(q, stb) * xi
            state[h] = st * gch + _dot((kc32 * zeta).astype(BF16), v, _TN)
            mu = jnp.mean(o, axis=-1, keepdims=True)
            cen = o - mu
            yh = cen * lax.rsqrt(jnp.mean(cen * cen, axis=-1, keepdims=True) + EPS)
            gr = gr_ref[:, cv].astype(F32)
            or_ref[:, cv] = ((yh * g_ref[:, cv] + b_ref[:, cv]) * (gr * jax.nn.sigmoid(gr))).astype(BF16)
            o_ref[:, cv] = o

    qk, vv, par, wide, st = _ret_specs(nch, False)
    return pl.pallas_call(
        body, name="ret_fwd", grid=(nch,),
        in_specs=[qk(R_Q), qk(R_K), vv(R_V), vv(R_G), par, par],
        out_specs=[wide, wide, st],
        out_shape=[_sds((S, RET_V_W), BF16), _sds((S, RET_V_W), F32), _sds((RET_HEADS, nch, RET_QK, RET_V), BF16)],
        scratch_shapes=[pltpu.VMEM((RET_HEADS, RET_QK, RET_V), F32)],
        compiler_params=_cparams(("arbitrary",)),
    )(z_rest, z_rest, z_rest, z_rest, gn_g, gn_b)


def _merge_fwd(o_a, o_r, z_rest, x, wpa, wpb, wout, g2, tm):
    S = x.shape[0]

    def body(oa_ref, or_ref, ga_ref, gb_ref, x_ref, wpa_ref, wpb_ref, wo_ref, g2_ref,
             x1_ref, y_ref, pa_ref, pb_ref, xn2_ref):
        pa = _dot(oa_ref[...].astype(BF16), wpa_ref[...])
        pb = _dot(or_ref[...], wpb_ref[...])
        y = jax.nn.sigmoid(ga_ref[...].astype(F32)) * pa + jax.nn.sigmoid(gb_ref[...].astype(F32)) * pb
        yb = y.astype(BF16)
        x1 = x_ref[...] + _dot(yb, wo_ref[...])
        x1_ref[...] = x1
        y_ref[...] = yb
        pa_ref[...] = pa.astype(BF16)
        pb_ref[...] = pb.astype(BF16)
        xn2_ref[...] = (x1 * _rms_rows(x1) * g2_ref[...]).astype(BF16)

    row = lambda w: pl.BlockSpec((tm, w), lambda i: (i, 0))
    full = lambda a: pl.BlockSpec(a.shape, lambda i: (0, 0))
    return pl.pallas_call(
        body, name="merge_fwd", grid=(S // tm,),
        in_specs=[row(ATT_OUT_W), row(RET_V_W),
                  pl.BlockSpec((tm, D_MODEL), lambda i: (i, R_GA // D_MODEL)),
                  pl.BlockSpec((tm, D_MODEL), lambda i: (i, R_GB // D_MODEL)),
                  row(D_MODEL), full(wpa), full(wpb), full(wout), full(g2)],
        out_specs=[row(D_MODEL)] * 5,
        out_shape=[_sds((S, D_MODEL), F32)] + [_sds((S, D_MODEL), BF16)] * 4,
        compiler_params=_cparams(("parallel",)),
    )(o_a, o_r, z_rest, z_rest, x, wpa, wpb, wout, g2)


def _rms_bwd(dy, xv, g):
    r = _rms_rows(xv)
    xh = xv * r
    dg = dy * g
    dx = r * (dg - xh * jnp.mean(dg * xh, axis=-1, keepdims=True))
    return dx, jnp.sum(dy * xh, axis=0, keepdims=True)


def _ret_bwd(do_r, o_pre, states, z_rest, dz_rest, gn_g, gn_b):
    S = z_rest.shape[0]
    nch = S // CHUNK

    def body(do_ref, o_ref, st_ref, q_ref, k_ref, v_ref, gr_ref, g_ref, b_ref, dz_in,
             dz_ref, dg_ref, db_ref, gst):
        del dz_in
        first = pl.program_id(0) == 0

        @pl.when(first)
        def _():
            gst[...] = jnp.zeros_like(gst)

        dgs, dbs = [], []
        for h in range(RET_HEADS):
            decay, xi, zeta, gch = _ret_tables(LOG_GAMMA[h])
            cq = slice(h * RET_QK, (h + 1) * RET_QK)
            cv = slice(h * RET_V, (h + 1) * RET_V)
            o = o_ref[:, cv]
            mu = jnp.mean(o, axis=-1, keepdims=True)
            cen = o - mu
            rstd = lax.rsqrt(jnp.mean(cen * cen, axis=-1, keepdims=True) + EPS)
            yh = cen * rstd
            gam = g_ref[:, cv]
            y = yh * gam + b_ref[:, cv]
            gr = gr_ref[:, cv].astype(F32)
            sg = jax.nn.sigmoid(gr)
            dout = do_ref[:, cv]
            dy = dout * (gr * sg)
            dz_ref[:, R_G + h * RET_V:R_G + (h + 1) * RET_V] = (dout * y * (sg * (1.0 + gr * (1.0 - sg)))).astype(BF16)
            dgs.append(jnp.sum(dy * yh, axis=0, keepdims=True))
            dbs.append(jnp.sum(dy, axis=0, keepdims=True))
            dyh = dy * gam
            do = rstd * (dyh - jnp.mean(dyh, axis=-1, keepdims=True)
                         - yh * jnp.mean(dyh * yh, axis=-1, keepdims=True))

            q = q_ref[:, cq]
            kc32 = k_ref[:, cq].astype(F32) * (RET_QK ** -0.5)
            kc = kc32.astype(BF16)
            v = v_ref[:, cv]
            dob = do.astype(BF16)
            a = (_dot(q, kc, _NT) * decay).astype(BF16)
            da = (_dot(dob, v, _NT) * decay).astype(BF16)
            dcross = (do * xi).astype(BF16)
            g_next = gst[h]
            gb = g_next.astype(BF16)
            dq = _dot(da, kc) + _dot(dcross, st_ref[h], _NT)
            dkc = _dot(da, q, _TN)
            dkz = _dot(v, gb, _NT)
            dv = _dot(a, dob, _TN) + _dot((kc32 * zeta).astype(BF16), gb)
            gst[h] = g_next * gch + _dot(q, dcross, _TN)
            dz_ref[:, R_Q + h * RET_QK:R_Q + (h + 1) * RET_QK] = dq.astype(BF16)
            dz_ref[:, R_K + h * RET_QK:R_K + (h + 1) * RET_QK] = ((dkc + dkz * zeta) * (RET_QK ** -0.5)).astype(BF16)
            dz_ref[:, R_V + h * RET_V:R_V + (h + 1) * RET_V] = dv.astype(BF16)
        _acc_rows(dg_ref, jnp.concatenate(dgs, axis=1), first)
        _acc_rows(db_ref, jnp.concatenate(dbs, axis=1), first)

    qk, vv, par, wide, st = _ret_specs(nch, True)
    return pl.pallas_call(
        body, name="ret_bwd", grid=(nch,),
        in_specs=[wide, wide, st, qk(R_Q), qk(R_K), vv(R_V), vv(R_G), par, par, pl.BlockSpec(memory_space=pl.ANY)],
        out_specs=[pl.BlockSpec((CHUNK, RET_COLS), lambda n: (nch - 1 - n, 0)), par, par],
        out_shape=[_sds(dz_rest.shape, BF16), _sds((1, RET_V_W), F32), _sds((1, RET_V_W), F32)],
        input_output_aliases={9: 0},
        scratch_shapes=[pltpu.VMEM((RET_HEADS, RET_QK, RET_V), F32)],
        compiler_params=_cparams(("arbitrary",)),
    )(do_r, o_pre, states, z_rest, z_rest, z_rest, z_rest, gn_g, gn_b, dz_rest)


def _att_probs(q, k, lmix, valid, bias):
    s = _dot(q, k, _NT) * (DH ** -0.5) + bias
    return jnp.where(valid, jnp.exp(jnp.where(valid, s, NEG) - lmix), 0.0)


def _att_bwd_group(z_att, dz_att, do_a, delta, lmix, gq3, gk3, slopes3, gi, d, nb):
    S = z_att.shape[0]
    sb = BLK * d * nb
    nsb = S // sb
    scale = DH ** -0.5
    aliased = dz_att is not None

    def body(q_ref, k_ref, v_ref, kp_ref, vp_ref, qn_ref, do_ref, don_ref, dl_ref, dln_ref, lm_ref, lmn_ref,
             gq_ref, gk_ref, sl_ref, *rest):
        dz_ref, dgq_ref, dgk_ref, stage = rest[-4:]
        n = pl.program_id(1)
        slope = sl_ref[...][:, :1]
        valid0, bias = _att_mask_bias(slope, d, n == 0)
        valid_in, _ = _att_mask_bias(slope, d, None)
        qi = lax.broadcasted_iota(jnp.int32, (BLK, BLK), 0)
        kj = lax.broadcasted_iota(jnp.int32, (BLK, BLK), 1)
        dist_n = BLK + qi - kj
        valid_n_in = dist_n <= BLK
        valid_n_last = valid_n_in & (n < nsb - 1)
        bias_n = -slope * (dist_n * d).astype(F32)
        gq, gk = gq_ref[...], gk_ref[...]
        dgq = jnp.zeros((1, DH), F32)
        dgk = jnp.zeros((1, DH), F32)
        for r in range(d):
            memo = {}

            def get(kind, b):
                if (kind, b) not in memo:
                    inner = 0 <= b < nb
                    bb = b if inner else 0
                    if kind == "q":
                        val = _head_norm(_rows(q_ref if inner else qn_ref, r, bb, d), gq)
                    elif kind == "k":
                        val = _head_norm(_rows(k_ref if inner else kp_ref, r, bb, d), gk)
                    elif kind == "v":
                        val = _rows(v_ref if inner else vp_ref, r, bb, d).astype(BF16)
                    elif kind == "do":
                        val = _rows(do_ref if inner else don_ref, r, bb, d).astype(BF16)
                    elif kind == "dl":
                        val = _rows(dl_ref if inner else dln_ref, r, bb, d)[:, :1]
                    else:
                        val = _rows(lm_ref if inner else lmn_ref, r, bb, d)[:, :1]
                    memo[(kind, b)] = val
                return memo[(kind, b)]

            for b in range(nb):
                q, qh, qr = get("q", b)
                kc, kh, kr = get("k", b)
                qn = get("q", b + 1)[0]
                vc, do_c, do_n = get("v", b), get("do", b), get("do", b + 1)
                k2 = jnp.concatenate([get("k", b - 1)[0], kc], axis=0)
                v2 = jnp.concatenate([get("v", b - 1), vc], axis=0)
                p = _att_probs(q, k2, get("lm", b), valid0 if b == 0 else valid_in, bias)
                ds = (p * (_dot(do_c, v2, _NT) - get("dl", b)) * scale).astype(BF16)
                dx, dg = _head_norm_bwd(_dot(ds, k2), qh, qr, gq)
                _put_rows(stage.at[0], r, b, d, dx)
                dgq = dgq + dg
                p_n = _att_probs(qn, kc, get("lm", b + 1), valid_n_last if b == nb - 1 else valid_n_in, bias_n)
                ds_n = (p_n * (_dot(do_n, vc, _NT) - get("dl", b + 1)) * scale).astype(BF16)
                dk = _dot(ds[:, BLK:], q, _TN) + _dot(ds_n, qn, _TN)
                dv = _dot(p[:, BLK:].astype(BF16), do_c, _TN) + _dot(p_n.astype(BF16), do_n, _TN)
                dx, dg = _head_norm_bwd(dk, kh, kr, gk)
                _put_rows(stage.at[1], r, b, d, dx)
                _put_rows(stage.at[2], r, b, d, dv)
                dgk = dgk + dg
        for kind in range(3):
            dz_ref[:, kind * DH:(kind + 1) * DH] = stage[kind].astype(BF16)
        _acc_rows(dgq_ref, dgq, n == 0)
        _acc_rows(dgk_ref, dgk, n == 0)

    cur, prev, nxt, slot, slot_next, head = _att_specs(gi, d, nb, S)
    gain = pl.BlockSpec((None, 1, DH), lambda j, n: (j, 0, 0))
    in_specs = [cur(0), cur(1), cur(2), prev(1), prev(2), nxt(0),
                slot, slot_next, slot, slot_next, slot, slot_next, head, head, head]
    args = [z_att] * 6 + [do_a, do_a, delta, delta, lmix, lmix, gq3, gk3, slopes3]
    if aliased:
        in_specs.append(pl.BlockSpec(memory_space=pl.ANY))
        args.append(dz_att)
    return pl.pallas_call(
        body, name=f"att_bwd_g{gi}", grid=(HPG, nsb),
        in_specs=in_specs,
        out_specs=[pl.BlockSpec((sb, 3 * DH), lambda j, n: (n, gi * HPG + j)), gain, gain],
        out_shape=[_sds(z_att.shape, BF16), _sds((HPG, 1, DH), F32), _sds((HPG, 1, DH), F32)],
        input_output_aliases={len(args) - 1: 0} if aliased else {},
        scratch_shapes=[pltpu.VMEM((3, sb, DH), F32)],
        compiler_params=_cparams(("parallel", "arbitrary")),
    )(*args)


def _att_col_blocks():
    return [kind * ATT_HEADS + h for h in range(ATT_HEADS) for kind in range(3)]


def _local_step(x, target, norm1_g, q_norm_g, k_norm_g, gn_g, gn_b, norm2_g, w_in, w_pa, w_pb, w_out, w_up, w_down):
    S = x.shape[0]
    tm = min(512, S)
    ts = min(256, S)
    tk = min(2048, S)
    blocks = _att_col_blocks()
    w_att = jnp.concatenate([w_in[:, p * DH:(p + 1) * DH] for p in blocks], axis=1)
    w_rest = w_in[:, 3 * ATT_W:]
    w_att_t, w_rest_t, w_up_t, w_down_t = w_att.T, w_rest.T, w_up.T, w_down.T
    w_out_t, w_pa_t, w_pb_t = w_out.T, w_pa.T, w_pb.T
    gq3 = q_norm_g.reshape(ATT_HEADS, 1, DH)
    gk3 = k_norm_g.reshape(ATT_HEADS, 1, DH)
    slopes3 = jnp.asarray(np.broadcast_to(ALIBI[:, None, None], (ATT_HEADS, 1, DH)).copy())

    xn = _rmsnorm_fwd(x, norm1_g, tm)
    z_att = _mm("in_proj_att", xn, w_att, "nn", ts, 3 * ATT_W, D_MODEL,
                [(_sds((S, 3 * ATT_W), F32), _tile_ij(ts, 3 * ATT_W))], _epi_store(F32))[0]
    z_rest = _mm("in_proj_rest", xn, w_rest, "nn", ts, REST_W, D_MODEL,
                 [(_sds((S, REST_W), BF16), _tile_ij(ts, REST_W))], _epi_store(BF16))[0]
    os_, ls_ = [], []
    for gi, ((_, d), nb) in enumerate(zip(ATT_GROUPS, ATT_BLOCKS_PER_STEP)):
        o, l = _att_fwd_group(z_att, gq3, gk3, slopes3, gi, d, nb)
        os_.append(o)
        ls_.append(l)
    o_a, lmix = _att_combine(os_, ls_, tm)
    o_r, o_pre, states = _ret_fwd(z_rest, gn_g, gn_b)
    x1, y, pa, pb, xn2 = _merge_fwd(o_a, o_r, z_rest, x, w_pa, w_pb, w_out, norm2_g, min(256, S))

    def epi_up(acc, ex, out):
        r = jnp.maximum(acc, 0.0)
        out[0][...] = (r * r).astype(BF16)
        out[1][...] = r.astype(BF16)

    h, relu_u = _mm("mlp_up", xn2, w_up, "nn", ts, D_FF, D_MODEL,
                    [(_sds((S, D_FF), BF16), _tile_ij(ts, D_FF)), (_sds((S, D_FF), BF16), _tile_ij(ts, D_FF))], epi_up)

    def epi_down(acc, ex, out):
        diff = ex[0][...] + acc - ex[1][...]
        out[0][...] = diff * (1.0 / D_MODEL)
        out[1][...] = jnp.broadcast_to(jnp.sum(diff * diff) * (1.0 / (8 * LANES)), (8, LANES))

    row_tile = _tile_ij(tm, D_MODEL)
    dx2, loss_parts = _mm(
        "mlp_down_loss", h, w_down, "nn", tm, D_MODEL, D_FF,
        [(_sds((S, D_MODEL), F32), row_tile),
         (_sds((S // tm * 8, LANES), F32), pl.BlockSpec((8, LANES), lambda i, j, k: (i, 0)))],
        epi_down, extras=[(x1, row_tile), (target, row_tile)])
    loss_local = jnp.sum(loss_parts) * (0.5 / D_MODEL)

    def epi_du(acc, ex, out):
        out[0][...] = (acc * (2.0 * ex[0][...].astype(F32))).astype(BF16)

    du = _mm("mlp_down_bwd", dx2, w_down_t, "nn", ts, D_FF, D_MODEL,
             [(_sds((S, D_FF), BF16), _tile_ij(ts, D_FF))], epi_du, extras=[(relu_u, _tile_ij(ts, D_FF))])[0]
    gw_down = _mm("gw_down", h, dx2, "tn", 1024, D_MODEL, tk,
                  [(_sds((D_FF, D_MODEL), BF16), _tile_ij(1024, D_MODEL))], _epi_store(BF16))[0]
    gw_up = _mm("gw_up", xn2, du, "tn", D_MODEL, 512, tk,
                [(_sds((N_DEV, D_MODEL, 512), BF16), pl.BlockSpec((None, D_MODEL, 512), lambda i, j, k: (j, 0, 0)))],
                _epi_store(BF16))[0]

    vec = pl.BlockSpec((1, D_MODEL), lambda i, j, k: (0, 0))
    seq_sem = ("arbitrary", "arbitrary", "arbitrary")

    def epi_norm2(acc, ex, out):
        dx, dg = _rms_bwd(acc, ex[0][...], ex[2][...])
        out[0][...] = ex[1][...] + dx
        _acc_rows(out[1], dg, pl.program_id(0) == 0)

    dx1, g_norm2 = _mm(
        "mlp_up_bwd", du, w_up_t, "nn", tm, D_MODEL, D_FF,
        [(_sds((S, D_MODEL), F32), row_tile), (_sds((1, D_MODEL), F32), vec)],
        epi_norm2, extras=[(x1, row_tile), (dx2, row_tile), (norm2_g, vec)], sem=seq_sem)

    def epi_dy(acc, ex, out):
        sa = jax.nn.sigmoid(ex[0][...].astype(F32))
        sb = jax.nn.sigmoid(ex[1][...].astype(F32))
        out[0][...] = (acc * sa).astype(BF16)
        out[1][...] = (acc * sb).astype(BF16)
        out[2][:, :D_MODEL] = (acc * ex[2][...].astype(F32) * (sa * (1.0 - sa))).astype(BF16)
        out[2][:, D_MODEL:] = (acc * ex[3][...].astype(F32) * (sb * (1.0 - sb))).astype(BF16)

    ga_spec = pl.BlockSpec((tm, D_MODEL), lambda i, j, k: (i, R_GA // D_MODEL))
    gb_spec = pl.BlockSpec((tm, D_MODEL), lambda i, j, k: (i, R_GB // D_MODEL))
    gates_spec = pl.BlockSpec((tm, 2 * D_MODEL), lambda i, j, k: (i, R_GA // (2 * D_MODEL)))
    dpa, dpb, dz_rest = _mm(
        "out_proj_bwd", dx1, w_out_t, "nn", tm, D_MODEL, D_MODEL,
        [(_sds((S, D_MODEL), BF16), row_tile), (_sds((S, D_MODEL), BF16), row_tile), (_sds((S, REST_W), BF16), gates_spec)],
        epi_dy, extras=[(z_rest, ga_spec), (z_rest, gb_spec), (pa, row_tile), (pb, row_tile)])
    gw_out = _mm("gw_out", y, dx1, "tn", D_MODEL, D_MODEL, tk,
                 [(_sds((D_MODEL, D_MODEL), BF16), _tile_ij(D_MODEL, D_MODEL))], _epi_store(BF16))[0]
    gw_pa = _mm("gw_proj_a", o_a, dpa, "tn", ATT_OUT_W, D_MODEL, tk,
                [(_sds((ATT_OUT_W, D_MODEL), BF16), _tile_ij(ATT_OUT_W, D_MODEL))], _epi_store(BF16))[0]
    gw_pb = _mm("gw_proj_b", o_r, dpb, "tn", 1024, D_MODEL, tk,
                [(_sds((RET_V_W, D_MODEL), BF16), _tile_ij(1024, D_MODEL))], _epi_store(BF16))[0]

    def epi_doa(acc, ex, out):
        out[0][...] = acc
        prod = acc * ex[0][...]
        out[1][...] = jnp.concatenate(
            [jnp.broadcast_to(jnp.sum(prod[:, j * DH:(j + 1) * DH], axis=-1, keepdims=True), (prod.shape[0], DH))
             for j in range(HPG)], axis=1)

    slot_tile = _tile_ij(tm, ATT_OUT_W)
    do_a, delta = _mm("proj_a_bwd", dpa, w_pa_t, "nn", tm, ATT_OUT_W, D_MODEL,
                      [(_sds((S, ATT_OUT_W), F32), slot_tile), (_sds((S, ATT_OUT_W), F32), slot_tile)],
                      epi_doa, extras=[(o_a, slot_tile)])
    do_r = _mm("proj_b_bwd", dpb, w_pb_t, "nn", tm, RET_V_W, D_MODEL,
               [(_sds((S, RET_V_W), F32), _tile_ij(tm, RET_V_W))], _epi_store(F32))[0]

    dz_rest, g_gn_g, g_gn_b = _ret_bwd(do_r, o_pre, states, z_rest, dz_rest, gn_g, gn_b)
    dz_att, gq_parts, gk_parts = None, [], []
    for gi, ((_, d), nb) in enumerate(zip(ATT_GROUPS, ATT_BLOCKS_PER_STEP)):
        dz_att, gq_p, gk_p = _att_bwd_group(z_att, dz_att, do_a, delta, lmix, gq3, gk3, slopes3, gi, d, nb)
        gq_parts.append(gq_p)
        gk_parts.append(gk_p)
    g_qn = jnp.concatenate(gq_parts, axis=0).reshape(1, ATT_HEADS, DH)
    g_kn = jnp.concatenate(gk_parts, axis=0).reshape(1, ATT_HEADS, DH)

    gw_att = _mm("gw_in_att", xn, dz_att, "tn", D_MODEL, ATT_W, tk,
                 [(_sds((D_MODEL, 3 * ATT_W), BF16), _tile_ij(D_MODEL, ATT_W))], _epi_store(BF16))[0]
    gw_rest = _mm("gw_in_rest", xn, dz_rest, "tn", D_MODEL, 1024, tk,
                  [(_sds((D_MODEL, REST_W), BF16), _tile_ij(D_MODEL, 1024))], _epi_store(BF16))[0]
    dxn_att = _mm("in_proj_att_bwd", dz_att, w_att_t, "nn", tm, D_MODEL, 3 * ATT_W,
                  [(_sds((S, D_MODEL), F32), row_tile)], _epi_store(F32))[0]

    def epi_norm1(acc, ex, out):
        dx, dg = _rms_bwd(acc + ex[0][...], ex[1][...], ex[3][...])
        out[0][...] = ex[2][...] + dx
        _acc_rows(out[1], dg, pl.program_id(0) == 0)

    short_tile = _tile_ij(ts, D_MODEL)
    grad_x, g_norm1 = _mm(
        "in_proj_rest_bwd", dz_rest, w_rest_t, "nn", ts, D_MODEL, REST_W,
        [(_sds((S, D_MODEL), F32), short_tile), (_sds((1, D_MODEL), F32), vec)],
        epi_norm1, extras=[(dxn_att, short_tile), (x, short_tile), (dx1, short_tile), (norm1_g, vec)], sem=seq_sem)

    big = (gw_att, gw_rest, gw_pa, gw_pb, gw_out, gw_up, gw_down)
    small = (g_norm1, g_qn, g_kn, g_gn_g, g_gn_b, g_norm2)
    return loss_local, grad_x, big, small


def _position():
    return lax.axis_index("x"), lax.axis_index("y"), lax.axis_index("c")


def _other_chips(x, y):
    return [(1 - x, y), (x, 1 - y), (1 - x, 1 - y)]


_ANY = pl.BlockSpec(memory_space=pl.ANY)


def _all_gather(shards):
    nw = len(shards)

    def body(*refs):
        x_refs, out_refs = refs[:nw], refs[nw:2 * nw]
        send_sems, recv_sems, local_sems = refs[2 * nw:]
        x, y, c = _position()
        me, sibling = (x, y, c), (x, y, 1 - c)
        chips = _other_chips(x, y)

        def copy(w, k, block, to, src=None):
            px, py, pc = block
            rows = out_refs[w].at[4 * px + 2 * py + pc]
            return pltpu.make_async_remote_copy(
                src_ref=rows if src is None else src, dst_ref=rows,
                send_sem=send_sems.at[7 * w + k], recv_sem=recv_sems.at[7 * w + k], device_id=to, device_id_type=MESH)

        mine = [pltpu.make_async_copy(x_refs[w], out_refs[w].at[4 * x + 2 * y + c], local_sems.at[w]) for w in range(nw)]
        first = []
        for w in range(nw):
            mine[w].start()
            first.append(copy(w, 0, me, sibling, src=x_refs[w]))
            first += [copy(w, 1 + j, me, (*chip, c), src=x_refs[w]) for j, chip in enumerate(chips)]
        for cp in first:
            cp.start()
        passed = []
        for j, chip in enumerate(chips):
            for w in range(nw):
                copy(w, 1 + j, (*chip, c), me).wait_recv()
                fwd = copy(w, 4 + j, (*chip, c), sibling)
                fwd.start()
                passed.append(fwd)
        for w in range(nw):
            copy(w, 0, sibling, me).wait_recv()
            for j, chip in enumerate(chips):
                copy(w, 4 + j, (*chip, 1 - c), me).wait_recv()
        for cp in first + passed:
            cp.wait_send()
        for cp in mine:
            cp.wait()

    return pl.pallas_call(
        body, name="all_gather_weights",
        out_shape=[_sds((N_DEV,) + s.shape, s.dtype) for s in shards],
        in_specs=[_ANY] * nw, out_specs=[_ANY] * nw,
        scratch_shapes=[pltpu.SemaphoreType.DMA((7 * nw,)), pltpu.SemaphoreType.DMA((7 * nw,)),
                        pltpu.SemaphoreType.DMA((nw,))],
    )(*shards)


def _pair_exchange(grads, small):
    ng = len(grads)

    def body(*refs):
        g_refs, s_ref = refs[:ng], refs[ng]
        out_refs, s_out = refs[ng + 1:2 * ng + 1], refs[2 * ng + 1]
        send_sems, recv_sems, local_sem = refs[2 * ng + 2:]
        x, y, c = _position()
        me_id = 4 * x + 2 * y + c
        cps = [pltpu.make_async_remote_copy(
            src_ref=g_refs[w].at[:, 1 - c], dst_ref=out_refs[w], send_sem=send_sems.at[w], recv_sem=recv_sems.at[w],
            device_id=(x, y, 1 - c), device_id_type=MESH) for w in range(ng)]
        for cp in cps:
            cp.start()
        local = pltpu.make_async_copy(s_ref, s_out.at[me_id], local_sem)
        local.start()
        flips = [(a, b, e) for a in (0, 1) for b in (0, 1) for e in (0, 1)][1:]
        peers = [(x ^ a, y ^ b, c ^ e) for a, b, e in flips]
        s_cps = [pltpu.make_async_remote_copy(
            src_ref=s_ref, dst_ref=s_out.at[me_id], send_sem=send_sems.at[ng + k], recv_sem=recv_sems.at[ng + k],
            device_id=p, device_id_type=MESH) for k, p in enumerate(peers)]
        for cp in s_cps:
            cp.start()
        for cp in cps:
            cp.wait()
        for k, (px, py, pc) in enumerate(peers):
            pltpu.make_async_remote_copy(
                src_ref=s_ref, dst_ref=s_out.at[4 * px + 2 * py + pc], send_sem=send_sems.at[ng + k],
                recv_sem=recv_sems.at[ng + k], device_id=(px, py, pc), device_id_type=MESH).wait()
        local.wait()

    return pl.pallas_call(
        body, name="grad_pair_exchange",
        out_shape=[_sds((N_CHIP,) + g.shape[2:], g.dtype) for g in grads] + [_sds((N_DEV,) + small.shape, small.dtype)],
        in_specs=[_ANY] * (ng + 1), out_specs=[_ANY] * (ng + 1),
        scratch_shapes=[pltpu.SemaphoreType.DMA((ng + 7,)), pltpu.SemaphoreType.DMA((ng + 7,)), pltpu.SemaphoreType.DMA],
    )(*grads, small)


def _pair_sum(name, g, got, core, tr):
    n_chip, _, R, C = g.shape

    def body(c_ref, a_ref, b_ref, o_ref):
        del c_ref
        o_ref[...] = (a_ref[...].astype(F32) + b_ref[...].astype(F32)).astype(o_ref.dtype)

    return pl.pallas_call(
        body, name=name,
        grid_spec=pltpu.PrefetchScalarGridSpec(
            num_scalar_prefetch=1, grid=(n_chip, R // tr),
            in_specs=[pl.BlockSpec((None, None, tr, C), lambda ch, i, c_ref: (ch, c_ref[0], i, 0)),
                      pl.BlockSpec((None, tr, C), lambda ch, i, c_ref: (ch, i, 0))],
            out_specs=pl.BlockSpec((None, tr, C), lambda ch, i, c_ref: (ch, i, 0))),
        out_shape=_sds(got.shape, got.dtype),
        compiler_params=_cparams(("parallel", "parallel")),
    )(core, g, got)


def _chip_exchange(parts):
    ng = len(parts)

    def body(*refs):
        p_refs, out_refs = refs[:ng], refs[ng:2 * ng]
        send_sems, recv_sems, local_sems = refs[2 * ng:]
        x, y, c = _position()
        my_chip = 2 * x + y
        cps, locals_ = [], []
        for w in range(ng):
            lc = pltpu.make_async_copy(p_refs[w].at[my_chip], out_refs[w].at[my_chip], local_sems.at[w])
            lc.start()
            locals_.append(lc)
            for k, (cx, cy) in enumerate(_other_chips(x, y)):
                cp = pltpu.make_async_remote_copy(
                    src_ref=p_refs[w].at[2 * cx + cy], dst_ref=out_refs[w].at[my_chip],
                    send_sem=send_sems.at[3 * w + k], recv_sem=recv_sems.at[3 * w + k],
                    device_id=(cx, cy, c), device_id_type=MESH)
                cp.start()
                cps.append(cp)
        for cp in cps:
            cp.wait()
        for lc in locals_:
            lc.wait()

    return pl.pallas_call(
        body, name="grad_chip_exchange",
        out_shape=[_sds(p.shape, p.dtype) for p in parts],
        in_specs=[_ANY] * ng, out_specs=[_ANY] * ng,
        scratch_shapes=[pltpu.SemaphoreType.DMA((3 * ng,)), pltpu.SemaphoreType.DMA((3 * ng,)),
                        pltpu.SemaphoreType.DMA((ng,))],
    )(*parts)


def _adamw(name, parts, w, m, v, tr):
    n_parts = parts.shape[0]
    R, C = w.shape

    def body(p_ref, w_ref, m_ref, v_ref, g_ref, d_ref, mo_ref, vo_ref):
        g = p_ref[0].astype(F32)
        for i in range(1, n_parts):
            g = g + p_ref[i].astype(F32)
        m_new = ADAM_B1 * m_ref[...] + (1.0 - ADAM_B1) * g
        v_new = ADAM_B2 * v_ref[...] + (1.0 - ADAM_B2) * (g * g)
        m_hat = m_new / (1.0 - ADAM_B1 ** ADAM_STEP)
        v_hat = v_new / (1.0 - ADAM_B2 ** ADAM_STEP)
        g_ref[...] = g
        d_ref[...] = -ADAM_LR * (m_hat / (jnp.sqrt(v_hat) + ADAM_EPS) + ADAM_WD * w_ref[...])
        mo_ref[...] = m_new
        vo_ref[...] = v_new

    tile = pl.BlockSpec((tr, C), lambda i: (i, 0))
    return pl.pallas_call(
        body, name=name, grid=(R // tr,),
        in_specs=[pl.BlockSpec((n_parts, tr, C), lambda i: (0, i, 0)), tile, tile, tile],
        out_specs=[tile] * 4,
        out_shape=[_sds((R, C), F32)] * 4,
        compiler_params=_cparams(("parallel",)),
    )(parts, w, m, v)


def _flat_small(arrs):
    return jnp.concatenate([a.reshape(-1) for a in arrs]).reshape(SMALL_ROWS, LANES)


def _by_owner_cols(g):
    rows, cols = g.shape
    return g.reshape(rows, N_DEV, cols // N_DEV).transpose(1, 0, 2)


def _chip_core(g):
    return g.reshape((N_CHIP, 2) + g.shape[1:])


def kernel(x, norm1_g, w_in, q_norm_g, k_norm_g, ret_gn_g, ret_gn_b, w_proj_a, w_proj_b, w_out, norm2_g, w_up, w_down, loss_target, m_norm1_g, m_w_in, m_q_norm_g, m_k_norm_g, m_ret_gn_g, m_ret_gn_b, m_w_proj_a, m_w_proj_b, m_w_out, m_norm2_g, m_w_up, m_w_down, v_norm1_g, v_w_in, v_q_norm_g, v_k_norm_g, v_ret_gn_g, v_ret_gn_b, v_w_proj_a, v_w_proj_b, v_w_out, v_norm2_g, v_w_up, v_w_down):
    big_w = (w_in, w_proj_a, w_proj_b, w_out, w_up, w_down)
    big_m = (m_w_in, m_w_proj_a, m_w_proj_b, m_w_out, m_w_up, m_w_down)
    big_v = (v_w_in, v_w_proj_a, v_w_proj_b, v_w_out, v_w_up, v_w_down)
    small_w = (norm1_g, q_norm_g, k_norm_g, ret_gn_g, ret_gn_b, norm2_g)
    small_m = (m_norm1_g, m_q_norm_g, m_k_norm_g, m_ret_gn_g, m_ret_gn_b, m_norm2_g)
    small_v = (v_norm1_g, v_q_norm_g, v_k_norm_g, v_ret_gn_g, v_ret_gn_b, v_norm2_g)

    g_in, g_pa, g_pb, g_out, g_up, g_down = _all_gather([w[0].astype(BF16) for w in big_w])
    cols = lambda g: g.transpose(1, 0, 2).reshape(g.shape[1], -1)
    rows = lambda g: g.reshape(-1, g.shape[2])
    full_w = (cols(g_in), cols(g_pa), rows(g_pb), rows(g_out), cols(g_up), rows(g_down))

    loss_local, grad_x, big_g, small_g = _local_step(
        x[0], loss_target[0], norm1_g, q_norm_g[0], k_norm_g[0], ret_gn_g, ret_gn_b, norm2_g, *full_w)
    gw_att, gw_rest, gw_pa, gw_pb, gw_out, gw_up, gw_down = big_g

    inv = np.argsort(np.asarray(_att_col_blocks()))
    gw_in = jnp.concatenate([gw_att[:, p * DH:(p + 1) * DH] for p in inv] + [gw_rest], axis=1)
    owed = [_chip_core(_by_owner_cols(gw_in)), _chip_core(_by_owner_cols(gw_pa)),
            _chip_core(gw_pb.reshape(N_DEV, -1, D_MODEL)), _chip_core(gw_out.reshape(N_DEV, -1, D_MODEL)),
            _chip_core(gw_up), _chip_core(gw_down.reshape(N_DEV, -1, D_MODEL))]
    names = ("w_in", "w_proj_a", "w_proj_b", "w_out", "w_up", "w_down")
    *got, small_all = _pair_exchange(owed, _flat_small(small_g))
    core = lax.axis_index("c").astype(jnp.int32).reshape(1)
    chip_sums = [_pair_sum(f"pair_sum_{n}", g, r, core, min(256, g.shape[2])) for n, g, r in zip(names, owed, got)]
    parts = _chip_exchange(chip_sums)

    res = {}
    for n, p, w, m, v in zip(names, parts, big_w, big_m, big_v):
        outs = _adamw(f"adamw_{n}", p, w[0], m[0], v[0], min(128, w.shape[1]))
        res[n] = [o[None] for o in outs]
    s_outs = _adamw("adamw_small", small_all, _flat_small(small_w), _flat_small(small_m), _flat_small(small_v), SMALL_ROWS)
    small_names = ("norm1_g", "q_norm_g", "k_norm_g", "ret_gn_g", "ret_gn_b", "norm2_g")
    for n in small_names:
        res[n] = []
    for o in s_outs:
        flat, off = o.reshape(-1), 0
        for n, w in zip(small_names, small_w):
            res[n].append(flat[off:off + w.size].reshape(w.shape))
            off += w.size

    order = ("norm1_g", "w_in", "q_norm_g", "k_norm_g", "ret_gn_g", "ret_gn_b", "w_proj_a", "w_proj_b", "w_out",
             "norm2_g", "w_up", "w_down")
    loss = lax.psum(loss_local, MESH_AXES)
    return (loss, grad_x[None], *[res[n][0] for n in order], *[res[n][1] for n in order],
            *[res[n][2] for n in order], *[res[n][3] for n in order])
```

```python
import math

import numpy as np
import jax
import jax.numpy as jnp
from jax import lax
from jax.experimental import pallas as pl
from jax.experimental.pallas import tpu as pltpu

F32 = jnp.float32
BF16 = jnp.bfloat16

D_MODEL = 1024
ATT_GROUPS = ((128, 1), (512, 4), (2048, 16))
ATT_BLOCKS_PER_STEP = (8, 1, 1)
HPG = 4
ATT_HEADS = 12
DH = 128
BLK = 128
ATT_W = ATT_HEADS * DH
ATT_OUT_W = HPG * DH
RET_HEADS = 4
RET_QK = 256
RET_V = 512
RET_QK_W = RET_HEADS * RET_QK
RET_V_W = RET_HEADS * RET_V
CHUNK = 128
D_FF = 4096
IN_W = 12800
REST_W = IN_W - 3 * ATT_W
RET_COLS = 6144
EPS = 1e-6
ADAM_LR, ADAM_B1, ADAM_B2, ADAM_EPS, ADAM_WD, ADAM_STEP = 0.001, 0.9, 0.999, 1e-08, 0.01, 10
N_DEV = 8
N_CHIP = 4
MESH_AXES = ("x", "y", "c")
MESH = pl.DeviceIdType.MESH
VMEM_LIMIT = 56 * 1024 * 1024
LANES = 128
NEG = -1e30

_NN = (((1,), (0,)), ((), ()))
_NT = (((1,), (1,)), ((), ()))
_TN = (((0,), (0,)), ((), ()))

R_Q, R_K, R_V, R_G, R_GA, R_GB = 0, 1024, 2048, 4096, 6144, 7168

LOG_GAMMA = [float(v) for v in np.log(1.0 - 2.0 ** (-5.0 - np.arange(RET_HEADS, dtype=np.float32))).astype(np.float32)]
ALIBI = np.asarray(2.0 ** (-8.0 * np.arange(1, ATT_HEADS + 1, dtype=np.float32) / ATT_HEADS), np.float32)

SMALL_ROWS = (1024 + 1536 + 1536 + 2048 + 2048 + 1024) // LANES


def _dot(a, b, dims=_NN):
    return lax.dot_general(a, b, dims, preferred_element_type=F32)


def _cparams(sem):
    return pltpu.CompilerParams(dimension_semantics=sem, vmem_limit_bytes=VMEM_LIMIT)


def _sds(shape, dtype):
    return jax.ShapeDtypeStruct(shape, dtype)


class _Exchange:
    def __init__(self, ins, out_shapes, sems, plan):
        self.ins, self.out_shapes, self.sems, self.plan = list(ins), list(out_shapes), list(sems), plan

    def split(self, in_refs, out_refs, sem_refs):
        return self.plan(in_refs, out_refs, sem_refs)


def _mm(name, a, b, mode, tm, tn, tk, outs, epi, extras=(), b_pro=None,
        sem=("parallel", "parallel", "arbitrary"), exchange=None):
    if mode == "nn":
        (M, K), (_, N) = a.shape, b.shape
        a_spec = pl.BlockSpec((tm, tk), lambda i, j, k: (i, k))
        dims = _NN
    else:
        (K, M), (_, N) = a.shape, b.shape
        a_spec = pl.BlockSpec((tk, tm), lambda i, j, k: (k, i))
        dims = _TN
    assert M % tm == 0 and N % tn == 0 and K % tk == 0, (name, M, N, K, tm, tn, tk)
    nk = K // tk
    whole_b = dict(pipeline_mode=pl.Buffered(1)) if (nk == 1 and N == tn) else {}
    b_spec = pl.BlockSpec((tk, tn), lambda i, j, k: (k, j), **whole_b)
    n_ex, n_out = len(extras), len(outs)
    grid = (M // tm, N // tn, nk)
    n_xi = len(exchange.ins) if exchange else 0
    n_xo = len(exchange.out_shapes) if exchange else 0
    n_acc = 1 if nk > 1 else 0

    def body(a_ref, b_ref, *rest):
        ex, rest = rest[:n_ex], rest[n_ex:]
        x_in, rest = rest[:n_xi], rest[n_xi:]
        out, rest = rest[:n_out], rest[n_out:]
        x_out, rest = rest[:n_xo], rest[n_xo:]
        step = (pl.program_id(0) * grid[1] + pl.program_id(1)) * grid[2] + pl.program_id(2)
        n_steps = grid[0] * grid[1] * grid[2]
        if exchange:
            start, relay, finish = exchange.split(x_in, x_out, rest[n_acc:])
            pl.when(step == 0)(start)
        bv = b_ref[...]
        if b_pro is not None:
            bv = b_pro(bv)
        part = _dot(a_ref[...].astype(BF16), bv.astype(BF16), dims)
        if nk == 1:
            epi(part, ex, out)
        else:
            acc_ref = rest[0]
            k = pl.program_id(2)

            @pl.when(k == 0)
            def _():
                acc_ref[...] = part

            @pl.when(k > 0)
            def _():
                acc_ref[...] += part

            @pl.when(k == nk - 1)
            def _():
                epi(acc_ref[...], ex, out)
        if exchange:
            if relay is not None:
                pl.when(step == n_steps // 2)(relay)
            pl.when(step == n_steps - 1)(finish)

    res = pl.pallas_call(
        body,
        name=name,
        grid=grid,
        in_specs=[a_spec, b_spec] + [s for _, s in extras] + [_ANY] * n_xi,
        out_specs=[s for _, s in outs] + [_ANY] * n_xo,
        out_shape=[o for o, _ in outs] + (exchange.out_shapes if exchange else []),
        scratch_shapes=([pltpu.VMEM((tm, tn), F32)] if nk > 1 else []) + (exchange.sems if exchange else []),
        compiler_params=_cparams(("arbitrary",) * 3 if exchange else sem),
    )(a, b, *[e for e, _ in extras], *(exchange.ins if exchange else []))
    return (res[:n_out], res[n_out:]) if exchange else res


def _tile_ij(tm, tn):
    return pl.BlockSpec((tm, tn), lambda i, j, k: (i, j))


def _epi_store(dtype):
    def epi(acc, ex, out):
        out[0][...] = acc.astype(dtype)
    return epi


def _rms_rows(x):
    return lax.rsqrt(jnp.mean(x * x, axis=-1, keepdims=True) + EPS)


def _acc_rows(ref, part, first):
    @pl.when(first)
    def _():
        ref[...] = part

    @pl.when(jnp.logical_not(first))
    def _():
        ref[...] += part


def _rmsnorm_fwd(x, g, tm):
    S, Dm = x.shape

    def body(x_ref, g_ref, o_ref):
        xv = x_ref[...]
        o_ref[...] = (xv * _rms_rows(xv) * g_ref[...]).astype(BF16)

    return pl.pallas_call(
        body, name="rmsnorm1_fwd", grid=(S // tm,),
        in_specs=[pl.BlockSpec((tm, Dm), lambda i: (i, 0)), pl.BlockSpec((1, Dm), lambda i: (0, 0))],
        out_specs=pl.BlockSpec((tm, Dm), lambda i: (i, 0)),
        out_shape=_sds((S, Dm), BF16),
        compiler_params=_cparams(("parallel",)),
    )(x, g)


def _rows(ref, r, b, d):
    if d == 1:
        return ref[b * BLK:(b + 1) * BLK, :]
    return ref[pl.ds(b * BLK * d + r, BLK, stride=d), :]


def _put_rows(ref, r, b, d, val):
    if d == 1:
        ref[b * BLK:(b + 1) * BLK, :] = val
    else:
        ref[pl.ds(b * BLK * d + r, BLK, stride=d), :] = val


def _head_norm(x, g):
    r = _rms_rows(x)
    xh = x * r
    return (xh * g).astype(BF16), xh, r


def _head_norm_bwd(dyn, xh, r, g):
    dxh = dyn * g
    dx = r * (dxh - xh * jnp.mean(dxh * xh, axis=-1, keepdims=True))
    return dx, jnp.sum(dyn * xh, axis=0, keepdims=True)


def _att_mask_bias(slope, d, first):
    qi = lax.broadcasted_iota(jnp.int32, (BLK, 2 * BLK), 0)
    kj = lax.broadcasted_iota(jnp.int32, (BLK, 2 * BLK), 1)
    dist = BLK + qi - kj
    valid = (dist >= 0) & (dist <= BLK)
    if first is not None:
        valid = valid & (jnp.logical_not(first) | (kj >= BLK))
    bias = -slope * (dist * d).astype(F32)
    return valid, bias


def _att_specs(gi, d, nb, S):
    span = BLK * d
    sb = span * nb
    nspan = S // span
    before = lambda n: jnp.maximum(n * nb - 1, 0)
    after = lambda n: jnp.minimum((n + 1) * nb, nspan - 1)
    zcol = lambda j, kind: 3 * (gi * HPG + j) + kind
    cur = lambda kind: pl.BlockSpec((sb, DH), lambda j, n: (n, zcol(j, kind)))
    prev = lambda kind: pl.BlockSpec((span, DH), lambda j, n: (before(n), zcol(j, kind)))
    nxt = lambda kind: pl.BlockSpec((span, DH), lambda j, n: (after(n), zcol(j, kind)))
    slot = pl.BlockSpec((sb, DH), lambda j, n: (n, j))
    slot_next = pl.BlockSpec((span, DH), lambda j, n: (after(n), j))
    head = pl.BlockSpec((None, 1, DH), lambda j, n: (gi * HPG + j, 0, 0))
    return cur, prev, nxt, slot, slot_next, head


def _att_fwd_group(z_att, gq3, gk3, slopes3, gi, d, nb):
    S = z_att.shape[0]
    nsb = S // (BLK * d * nb)
    scale = DH ** -0.5

    def body(q_ref, k_ref, v_ref, kp_ref, vp_ref, gq_ref, gk_ref, sl_ref, o_ref, l_ref):
        slope = sl_ref[...][:, :1]
        valid0, bias = _att_mask_bias(slope, d, pl.program_id(1) == 0)
        valid_in, _ = _att_mask_bias(slope, d, None)
        gq, gk = gq_ref[...], gk_ref[...]
        for r in range(d):
            kp = _head_norm(_rows(kp_ref, r, 0, d), gk)[0]
            vp = _rows(vp_ref, r, 0, d).astype(BF16)
            for b in range(nb):
                q = _head_norm(_rows(q_ref, r, b, d), gq)[0]
                kc = _head_norm(_rows(k_ref, r, b, d), gk)[0]
                vc = _rows(v_ref, r, b, d).astype(BF16)
                k2 = jnp.concatenate([kp, kc], axis=0)
                v2 = jnp.concatenate([vp, vc], axis=0)
                s = jnp.where(valid0 if b == 0 else valid_in, _dot(q, k2, _NT) * scale + bias, NEG)
                m = jnp.max(s, axis=-1, keepdims=True)
                p = jnp.exp(s - m)
                den = jnp.sum(p, axis=-1, keepdims=True)
                _put_rows(o_ref, r, b, d, _dot(p.astype(BF16), v2) / den)
                _put_rows(l_ref, r, b, d, jnp.broadcast_to(m + jnp.log(den), (BLK, DH)))
                kp, vp = kc, vc

    cur, prev, _, slot, _, head = _att_specs(gi, d, nb, S)
    return pl.pallas_call(
        body, name=f"att_fwd_g{gi}", grid=(HPG, nsb),
        in_specs=[cur(0), cur(1), cur(2), prev(1), prev(2), head, head, head],
        out_specs=[slot, slot],
        out_shape=[_sds((S, ATT_OUT_W), F32), _sds((S, ATT_OUT_W), F32)],
        compiler_params=_cparams(("parallel", "arbitrary")),
    )(z_att, z_att, z_att, z_att, z_att, gq3, gk3, slopes3)


def _att_combine(os_, ls_, tm):
    S = os_[0].shape[0]

    def body(o0, o1, o2, l0, l1, l2, oa_ref, lm_ref):
        a, b, c = l0[...], l1[...], l2[...]
        m = jnp.maximum(jnp.maximum(a, b), c)
        ea, eb, ec = jnp.exp(a - m), jnp.exp(b - m), jnp.exp(c - m)
        tot = ea + eb + ec
        oa_ref[...] = (ea * o0[...] + eb * o1[...] + ec * o2[...]) / tot
        lm_ref[...] = m + jnp.log(tot)

    spec = pl.BlockSpec((tm, ATT_OUT_W), lambda i: (i, 0))
    return pl.pallas_call(
        body, name="att_combine", grid=(S // tm,),
        in_specs=[spec] * 6, out_specs=[spec, spec],
        out_shape=[_sds((S, ATT_OUT_W), F32), _sds((S, ATT_OUT_W), F32)],
        compiler_params=_cparams(("parallel",)),
    )(*os_, *ls_)


def _ret_tables(lg):
    ri = lax.broadcasted_iota(jnp.int32, (CHUNK, CHUNK), 0)
    ci = lax.broadcasted_iota(jnp.int32, (CHUNK, CHUNK), 1)
    diff = (ri - ci).astype(F32)
    decay = jnp.where(diff >= 0, jnp.exp(lg * jnp.maximum(diff, 0.0)), 0.0)
    idx = lax.broadcasted_iota(jnp.int32, (CHUNK, 1), 0).astype(F32)
    xi = jnp.exp(lg * (idx + 1.0))
    zeta = jnp.exp(lg * (CHUNK - 1.0 - idx))
    return decay, xi, zeta, math.exp(lg * CHUNK)


def _ret_specs(nch, rev):
    idx = (lambda n: nch - 1 - n) if rev else (lambda n: n)
    qk = lambda off: pl.BlockSpec((CHUNK, RET_QK_W), lambda n: (idx(n), off // RET_QK_W))
    vv = lambda off: pl.BlockSpec((CHUNK, RET_V_W), lambda n: (idx(n), off // RET_V_W))
    par = pl.BlockSpec((1, RET_V_W), lambda n: (0, 0))
    wide = pl.BlockSpec((CHUNK, RET_V_W), lambda n: (idx(n), 0))
    st = pl.BlockSpec((RET_HEADS, None, RET_QK, RET_V), lambda n: (0, idx(n), 0, 0))
    return qk, vv, par, wide, st


def _ret_fwd(z_rest, gn_g, gn_b):
    S = z_rest.shape[0]
    nch = S // CHUNK

    def body(q_ref, k_ref, v_ref, gr_ref, g_ref, b_ref, or_ref, o_ref, st_ref, state):
        @pl.when(pl.program_id(0) == 0)
        def _():
            state[...] = jnp.zeros_like(state)

        for h in range(RET_HEADS):
            decay, xi, zeta, gch = _ret_tables(LOG_GAMMA[h])
            cq = slice(h * RET_QK, (h + 1) * RET_QK)
            cv = slice(h * RET_V, (h + 1) * RET_V)
            q = q_ref[:, cq]
            kc32 = k_ref[:, cq].astype(F32) * (RET_QK ** -0.5)
            kc = kc32.astype(BF16)
            v = v_ref[:, cv]
            st = state[h]
            stb = st.astype(BF16)
            st_ref[h] = stb
            s = _dot(q, kc, _NT) * decay
            o = _dot(s.astype(BF16), v) + _dot(q, stb) * xi
            state[h] = st * gch + _dot((kc32 * zeta).astype(BF16), v, _TN)
            mu = jnp.mean(o, axis=-1, keepdims=True)
            cen = o - mu
            yh = cen * lax.rsqrt(jnp.mean(cen * cen, axis=-1, keepdims=True) + EPS)
            gr = gr_ref[:, cv].astype(F32)
            or_ref[:, cv] = ((yh * g_ref[:, cv] + b_ref[:, cv]) * (gr * jax.nn.sigmoid(gr))).astype(BF16)
            o_ref[:, cv] = o

    qk, vv, par, wide, st = _ret_specs(nch, False)
    return pl.pallas_call(
        body, name="ret_fwd", grid=(nch,),
        in_specs=[qk(R_Q), qk(R_K), vv(R_V), vv(R_G), par, par],
        out_specs=[wide, wide, st],
        out_shape=[_sds((S, RET_V_W), BF16), _sds((S, RET_V_W), F32), _sds((RET_HEADS, nch, RET_QK, RET_V), BF16)],
        scratch_shapes=[pltpu.VMEM((RET_HEADS, RET_QK, RET_V), F32)],
        compiler_params=_cparams(("arbitrary",)),
    )(z_rest, z_rest, z_rest, z_rest, gn_g, gn_b)


def _merge_fwd(o_a, o_r, z_rest, x, wpa, wpb, wout, g2, tm):
    S = x.shape[0]

    def body(oa_ref, or_ref, ga_ref, gb_ref, x_ref, wpa_ref, wpb_ref, wo_ref, g2_ref,
             x1_ref, y_ref, pa_ref, pb_ref, xn2_ref):
        pa = _dot(oa_ref[...].astype(BF16), wpa_ref[...])
        pb = _dot(or_ref[...], wpb_ref[...])
        y = jax.nn.sigmoid(ga_ref[...].astype(F32)) * pa + jax.nn.sigmoid(gb_ref[...].astype(F32)) * pb
        yb = y.astype(BF16)
        x1 = x_ref[...] + _dot(yb, wo_ref[...])
        x1_ref[...] = x1
        y_ref[...] = yb
        pa_ref[...] = pa.astype(BF16)
        pb_ref[...] = pb.astype(BF16)
        xn2_ref[...] = (x1 * _rms_rows(x1) * g2_ref[...]).astype(BF16)

    row = lambda w: pl.BlockSpec((tm, w), lambda i: (i, 0))
    full = lambda a: pl.BlockSpec(a.shape, lambda i: (0, 0))
    return pl.pallas_call(
        body, name="merge_fwd", grid=(S // tm,),
        in_specs=[row(ATT_OUT_W), row(RET_V_W),
                  pl.BlockSpec((tm, D_MODEL), lambda i: (i, R_GA // D_MODEL)),
                  pl.BlockSpec((tm, D_MODEL), lambda i: (i, R_GB // D_MODEL)),
                  row(D_MODEL), full(wpa), full(wpb), full(wout), full(g2)],
        out_specs=[row(D_MODEL)] * 5,
        out_shape=[_sds((S, D_MODEL), F32)] + [_sds((S, D_MODEL), BF16)] * 4,
        compiler_params=_cparams(("parallel",)),
    )(o_a, o_r, z_rest, z_rest, x, wpa, wpb, wout, g2)


def _rms_bwd(dy, xv, g):
    r = _rms_rows(xv)
    xh = xv * r
    dg = dy * g
    dx = r * (dg - xh * jnp.mean(dg * xh, axis=-1, keepdims=True))
    return dx, jnp.sum(dy * xh, axis=0, keepdims=True)


def _ret_bwd(do_r, o_pre, states, z_rest, dz_rest, gn_g, gn_b, exchange):
    S = z_rest.shape[0]
    nch = S // CHUNK
    n_xi, n_xo = len(exchange.ins), len(exchange.out_shapes)

    def body(do_ref, o_ref, st_ref, q_ref, k_ref, v_ref, gr_ref, g_ref, b_ref, dz_in, *rest):
        del dz_in
        x_in, (dz_ref, dg_ref, db_ref), rest = rest[:n_xi], rest[n_xi:n_xi + 3], rest[n_xi + 3:]
        x_out, gst, x_sems = rest[:n_xo], rest[n_xo], rest[n_xo + 1:]
        start, _, finish = exchange.split(x_in, x_out, x_sems)
        first = pl.program_id(0) == 0
        pl.when(first)(start)

        @pl.when(first)
        def _():
            gst[...] = jnp.zeros_like(gst)

        dgs, dbs = [], []
        for h in range(RET_HEADS):
            decay, xi, zeta, gch = _ret_tables(LOG_GAMMA[h])
            cq = slice(h * RET_QK, (h + 1) * RET_QK)
            cv = slice(h * RET_V, (h + 1) * RET_V)
            o = o_ref[:, cv]
            mu = jnp.mean(o, axis=-1, keepdims=True)
            cen = o - mu
            rstd = lax.rsqrt(jnp.mean(cen * cen, axis=-1, keepdims=True) + EPS)
            yh = cen * rstd
            gam = g_ref[:, cv]
            y = yh * gam + b_ref[:, cv]
            gr = gr_ref[:, cv].astype(F32)
            sg = jax.nn.sigmoid(gr)
            dout = do_ref[:, cv]
            dy = dout * (gr * sg)
            dz_ref[:, R_G + h * RET_V:R_G + (h + 1) * RET_V] = (dout * y * (sg * (1.0 + gr * (1.0 - sg)))).astype(BF16)
            dgs.append(jnp.sum(dy * yh, axis=0, keepdims=True))
            dbs.append(jnp.sum(dy, axis=0, keepdims=True))
            dyh = dy * gam
            do = rstd * (dyh - jnp.mean(dyh, axis=-1, keepdims=True)
                         - yh * jnp.mean(dyh * yh, axis=-1, keepdims=True))

            q = q_ref[:, cq]
            kc32 = k_ref[:, cq].astype(F32) * (RET_QK ** -0.5)
            kc = kc32.astype(BF16)
            v = v_ref[:, cv]
            dob = do.astype(BF16)
            a = (_dot(q, kc, _NT) * decay).astype(BF16)
            da = (_dot(dob, v, _NT) * decay).astype(BF16)
            dcross = (do * xi).astype(BF16)
            g_next = gst[h]
            gb = g_next.astype(BF16)
            dq = _dot(da, kc) + _dot(dcross, st_ref[h], _NT)
            dkc = _dot(da, q, _TN)
            dkz = _dot(v, gb, _NT)
            dv = _dot(a, dob, _TN) + _dot((kc32 * zeta).astype(BF16), gb)
            gst[h] = g_next * gch + _dot(q, dcross, _TN)
            dz_ref[:, R_Q + h * RET_QK:R_Q + (h + 1) * RET_QK] = dq.astype(BF16)
            dz_ref[:, R_K + h * RET_QK:R_K + (h + 1) * RET_QK] = ((dkc + dkz * zeta) * (RET_QK ** -0.5)).astype(BF16)
            dz_ref[:, R_V + h * RET_V:R_V + (h + 1) * RET_V] = dv.astype(BF16)
        _acc_rows(dg_ref, jnp.concatenate(dgs, axis=1), first)
        _acc_rows(db_ref, jnp.concatenate(dbs, axis=1), first)
        pl.when(pl.program_id(0) == nch - 1)(finish)

    qk, vv, par, wide, st = _ret_specs(nch, True)
    res = pl.pallas_call(
        body, name="ret_bwd", grid=(nch,),
        in_specs=[wide, wide, st, qk(R_Q), qk(R_K), vv(R_V), vv(R_G), par, par, _ANY] + [_ANY] * n_xi,
        out_specs=[pl.BlockSpec((CHUNK, RET_COLS), lambda n: (nch - 1 - n, 0)), par, par] + [_ANY] * n_xo,
        out_shape=[_sds(dz_rest.shape, BF16), _sds((1, RET_V_W), F32), _sds((1, RET_V_W), F32)] + exchange.out_shapes,
        input_output_aliases={9: 0},
        scratch_shapes=[pltpu.VMEM((RET_HEADS, RET_QK, RET_V), F32)] + exchange.sems,
        compiler_params=_cparams(("arbitrary",)),
    )(do_r, o_pre, states, z_rest, z_rest, z_rest, z_rest, gn_g, gn_b, dz_rest, *exchange.ins)
    return res[:3], res[3:]


def _att_probs(q, k, lmix, valid, bias):
    s = _dot(q, k, _NT) * (DH ** -0.5) + bias
    return jnp.where(valid, jnp.exp(jnp.where(valid, s, NEG) - lmix), 0.0)


def _att_bwd_group(z_att, dz_att, do_a, delta, lmix, gq3, gk3, slopes3, gi, d, nb):
    S = z_att.shape[0]
    sb = BLK * d * nb
    nsb = S // sb
    scale = DH ** -0.5
    aliased = dz_att is not None

    def body(q_ref, k_ref, v_ref, kp_ref, vp_ref, qn_ref, do_ref, don_ref, dl_ref, dln_ref, lm_ref, lmn_ref,
             gq_ref, gk_ref, sl_ref, *rest):
        dz_ref, dgq_ref, dgk_ref, stage = rest[-4:]
        n = pl.program_id(1)
        slope = sl_ref[...][:, :1]
        valid0, bias = _att_mask_bias(slope, d, n == 0)
        valid_in, _ = _att_mask_bias(slope, d, None)
        qi = lax.broadcasted_iota(jnp.int32, (BLK, BLK), 0)
        kj = lax.broadcasted_iota(jnp.int32, (BLK, BLK), 1)
        dist_n = BLK + qi - kj
        valid_n_in = dist_n <= BLK
        valid_n_last = valid_n_in & (n < nsb - 1)
        bias_n = -slope * (dist_n * d).astype(F32)
        gq, gk = gq_ref[...], gk_ref[...]
        dgq = jnp.zeros((1, DH), F32)
        dgk = jnp.zeros((1, DH), F32)
        for r in range(d):
            memo = {}

            def get(kind, b):
                if (kind, b) not in memo:
                    inner = 0 <= b < nb
                    bb = b if inner else 0
                    if kind == "q":
                        val = _head_norm(_rows(q_ref if inner else qn_ref, r, bb, d), gq)
                    elif kind == "k":
                        val = _head_norm(_rows(k_ref if inner else kp_ref, r, bb, d), gk)
                    elif kind == "v":
                        val = _rows(v_ref if inner else vp_ref, r, bb, d).astype(BF16)
                    elif kind == "do":
                        val = _rows(do_ref if inner else don_ref, r, bb, d).astype(BF16)
                    elif kind == "dl":
                        val = _rows(dl_ref if inner else dln_ref, r, bb, d)[:, :1]
                    else:
                        val = _rows(lm_ref if inner else lmn_ref, r, bb, d)[:, :1]
                    memo[(kind, b)] = val
                return memo[(kind, b)]

            for b in range(nb):
                q, qh, qr = get("q", b)
                kc, kh, kr = get("k", b)
                qn = get("q", b + 1)[0]
                vc, do_c, do_n = get("v", b), get("do", b), get("do", b + 1)
                k2 = jnp.concatenate([get("k", b - 1)[0], kc], axis=0)
                v2 = jnp.concatenate([get("v", b - 1), vc], axis=0)
                p = _att_probs(q, k2, get("lm", b), valid0 if b == 0 else valid_in, bias)
                ds = (p * (_dot(do_c, v2, _NT) - get("dl", b)) * scale).astype(BF16)
                dx, dg = _head_norm_bwd(_dot(ds, k2), qh, qr, gq)
                _put_rows(stage.at[0], r, b, d, dx)
                dgq = dgq + dg
                p_n = _att_probs(qn, kc, get("lm", b + 1), valid_n_last if b == nb - 1 else valid_n_in, bias_n)
                ds_n = (p_n * (_dot(do_n, vc, _NT) - get("dl", b + 1)) * scale).astype(BF16)
                dk = _dot(ds[:, BLK:], q, _TN) + _dot(ds_n, qn, _TN)
                dv = _dot(p[:, BLK:].astype(BF16), do_c, _TN) + _dot(p_n.astype(BF16), do_n, _TN)
                dx, dg = _head_norm_bwd(dk, kh, kr, gk)
                _put_rows(stage.at[1], r, b, d, dx)
                _put_rows(stage.at[2], r, b, d, dv)
                dgk = dgk + dg
        for kind in range(3):
            dz_ref[:, kind * DH:(kind + 1) * DH] = stage[kind].astype(BF16)
        _acc_rows(dgq_ref, dgq, n == 0)
        _acc_rows(dgk_ref, dgk, n == 0)

    cur, prev, nxt, slot, slot_next, head = _att_specs(gi, d, nb, S)
    gain = pl.BlockSpec((None, 1, DH), lambda j, n: (j, 0, 0))
    in_specs = [cur(0), cur(1), cur(2), prev(1), prev(2), nxt(0),
                slot, slot_next, slot, slot_next, slot, slot_next, head, head, head]
    args = [z_att] * 6 + [do_a, do_a, delta, delta, lmix, lmix, gq3, gk3, slopes3]
    if aliased:
        in_specs.append(pl.BlockSpec(memory_space=pl.ANY))
        args.append(dz_att)
    return pl.pallas_call(
        body, name=f"att_bwd_g{gi}", grid=(HPG, nsb),
        in_specs=in_specs,
        out_specs=[pl.BlockSpec((sb, 3 * DH), lambda j, n: (n, gi * HPG + j)), gain, gain],
        out_shape=[_sds(z_att.shape, BF16), _sds((HPG, 1, DH), F32), _sds((HPG, 1, DH), F32)],
        input_output_aliases={len(args) - 1: 0} if aliased else {},
        scratch_shapes=[pltpu.VMEM((3, sb, DH), F32)],
        compiler_params=_cparams(("parallel", "arbitrary")),
    )(*args)


def _att_col_blocks():
    return [kind * ATT_HEADS + h for h in range(ATT_HEADS) for kind in range(3)]


def _step(x, target, norm1_g, q_norm_g, k_norm_g, gn_g, gn_b, norm2_g, w_in, later_shards, core):
    S = x.shape[0]
    tm = min(512, S)
    ts = min(256, S)
    tk = min(2048, S)
    blocks = _att_col_blocks()
    w_att = jnp.concatenate([w_in[:, p * DH:(p + 1) * DH] for p in blocks], axis=1)
    w_rest = w_in[:, 3 * ATT_W:]
    w_att_t, w_rest_t = w_att.T, w_rest.T
    gq3 = q_norm_g.reshape(ATT_HEADS, 1, DH)
    gk3 = k_norm_g.reshape(ATT_HEADS, 1, DH)
    slopes3 = jnp.asarray(np.broadcast_to(ALIBI[:, None, None], (ATT_HEADS, 1, DH)).copy())

    xn = _rmsnorm_fwd(x, norm1_g, tm)
    z_att = _mm("in_proj_att", xn, w_att, "nn", ts, 3 * ATT_W, D_MODEL,
                [(_sds((S, 3 * ATT_W), F32), _tile_ij(ts, 3 * ATT_W))], _epi_store(F32))[0]
    (z_rest,), gathered = _mm("in_proj_rest", xn, w_rest, "nn", ts, REST_W, D_MODEL,
                              [(_sds((S, REST_W), BF16), _tile_ij(ts, REST_W))], _epi_store(BF16),
                              exchange=_gather_exchange(later_shards))
    w_pa, w_pb, w_out, w_up, w_down = [f(g) for f, g in zip((_cols, _rows_of, _rows_of, _cols, _rows_of), gathered)]
    w_up_t, w_down_t, w_out_t, w_pa_t, w_pb_t = w_up.T, w_down.T, w_out.T, w_pa.T, w_pb.T
    os_, ls_ = [], []
    for gi, ((_, d), nb) in enumerate(zip(ATT_GROUPS, ATT_BLOCKS_PER_STEP)):
        o, l = _att_fwd_group(z_att, gq3, gk3, slopes3, gi, d, nb)
        os_.append(o)
        ls_.append(l)
    o_a, lmix = _att_combine(os_, ls_, tm)
    o_r, o_pre, states = _ret_fwd(z_rest, gn_g, gn_b)
    x1, y, pa, pb, xn2 = _merge_fwd(o_a, o_r, z_rest, x, w_pa, w_pb, w_out, norm2_g, min(256, S))

    def epi_up(acc, ex, out):
        r = jnp.maximum(acc, 0.0)
        out[0][...] = (r * r).astype(BF16)
        out[1][...] = r.astype(BF16)

    h, relu_u = _mm("mlp_up", xn2, w_up, "nn", ts, D_FF, D_MODEL,
                    [(_sds((S, D_FF), BF16), _tile_ij(ts, D_FF)), (_sds((S, D_FF), BF16), _tile_ij(ts, D_FF))], epi_up)

    def epi_down(acc, ex, out):
        diff = ex[0][...] + acc - ex[1][...]
        out[0][...] = diff * (1.0 / D_MODEL)
        out[1][...] = jnp.broadcast_to(jnp.sum(diff * diff) * (1.0 / (8 * LANES)), (8, LANES))

    row_tile = _tile_ij(tm, D_MODEL)
    dx2, loss_parts = _mm(
        "mlp_down_loss", h, w_down, "nn", tm, D_MODEL, D_FF,
        [(_sds((S, D_MODEL), F32), row_tile),
         (_sds((S // tm * 8, LANES), F32), pl.BlockSpec((8, LANES), lambda i, j, k: (i, 0)))],
        epi_down, extras=[(x1, row_tile), (target, row_tile)])
    loss_local = jnp.sum(loss_parts) * (0.5 / D_MODEL)

    def epi_du(acc, ex, out):
        out[0][...] = (acc * (2.0 * ex[0][...].astype(F32))).astype(BF16)

    du = _mm("mlp_down_bwd", dx2, w_down_t, "nn", ts, D_FF, D_MODEL,
             [(_sds((S, D_FF), BF16), _tile_ij(ts, D_FF))], epi_du, extras=[(relu_u, _tile_ij(ts, D_FF))])[0]
    gw_down = _mm("gw_down", h, dx2, "tn", 1024, D_MODEL, tk,
                  [(_sds((D_FF, D_MODEL), BF16), _tile_ij(1024, D_MODEL))], _epi_store(BF16))[0]
    gw_up = _mm("gw_up", xn2, du, "tn", D_MODEL, 512, tk,
                [(_sds((N_DEV, D_MODEL, 512), BF16), pl.BlockSpec((None, D_MODEL, 512), lambda i, j, k: (j, 0, 0)))],
                _epi_store(BF16))[0]

    vec = pl.BlockSpec((1, D_MODEL), lambda i, j, k: (0, 0))
    seq_sem = ("arbitrary", "arbitrary", "arbitrary")

    def epi_norm2(acc, ex, out):
        dx, dg = _rms_bwd(acc, ex[0][...], ex[2][...])
        out[0][...] = ex[1][...] + dx
        _acc_rows(out[1], dg, pl.program_id(0) == 0)

    dx1, g_norm2 = _mm(
        "mlp_up_bwd", du, w_up_t, "nn", tm, D_MODEL, D_FF,
        [(_sds((S, D_MODEL), F32), row_tile), (_sds((1, D_MODEL), F32), vec)],
        epi_norm2, extras=[(x1, row_tile), (dx2, row_tile), (norm2_g, vec)], sem=seq_sem)

    def epi_dy(acc, ex, out):
        sa = jax.nn.sigmoid(ex[0][...].astype(F32))
        sb = jax.nn.sigmoid(ex[1][...].astype(F32))
        out[0][...] = (acc * sa).astype(BF16)
        out[1][...] = (acc * sb).astype(BF16)
        out[2][:, :D_MODEL] = (acc * ex[2][...].astype(F32) * (sa * (1.0 - sa))).astype(BF16)
        out[2][:, D_MODEL:] = (acc * ex[3][...].astype(F32) * (sb * (1.0 - sb))).astype(BF16)

    ga_spec = pl.BlockSpec((tm, D_MODEL), lambda i, j, k: (i, R_GA // D_MODEL))
    gb_spec = pl.BlockSpec((tm, D_MODEL), lambda i, j, k: (i, R_GB // D_MODEL))
    gates_spec = pl.BlockSpec((tm, 2 * D_MODEL), lambda i, j, k: (i, R_GA // (2 * D_MODEL)))
    dpa, dpb, dz_rest = _mm(
        "out_proj_bwd", dx1, w_out_t, "nn", tm, D_MODEL, D_MODEL,
        [(_sds((S, D_MODEL), BF16), row_tile), (_sds((S, D_MODEL), BF16), row_tile), (_sds((S, REST_W), BF16), gates_spec)],
        epi_dy, extras=[(z_rest, ga_spec), (z_rest, gb_spec), (pa, row_tile), (pb, row_tile)])
    gw_out = _mm("gw_out", y, dx1, "tn", D_MODEL, D_MODEL, tk,
                 [(_sds((D_MODEL, D_MODEL), BF16), _tile_ij(D_MODEL, D_MODEL))], _epi_store(BF16))[0]
    gw_pa = _mm("gw_proj_a", o_a, dpa, "tn", ATT_OUT_W, D_MODEL, tk,
                [(_sds((ATT_OUT_W, D_MODEL), BF16), _tile_ij(ATT_OUT_W, D_MODEL))], _epi_store(BF16))[0]
    gw_pb = _mm("gw_proj_b", o_r, dpb, "tn", 1024, D_MODEL, tk,
                [(_sds((RET_V_W, D_MODEL), BF16), _tile_ij(1024, D_MODEL))], _epi_store(BF16))[0]

    def epi_doa(acc, ex, out):
        out[0][...] = acc
        prod = acc * ex[0][...]
        out[1][...] = jnp.concatenate(
            [jnp.broadcast_to(jnp.sum(prod[:, j * DH:(j + 1) * DH], axis=-1, keepdims=True), (prod.shape[0], DH))
             for j in range(HPG)], axis=1)

    slot_tile = _tile_ij(tm, ATT_OUT_W)
    do_a, delta = _mm("proj_a_bwd", dpa, w_pa_t, "nn", tm, ATT_OUT_W, D_MODEL,
                      [(_sds((S, ATT_OUT_W), F32), slot_tile), (_sds((S, ATT_OUT_W), F32), slot_tile)],
                      epi_doa, extras=[(o_a, slot_tile)])
    do_r = _mm("proj_b_bwd", dpb, w_pb_t, "nn", tm, RET_V_W, D_MODEL,
               [(_sds((S, RET_V_W), F32), _tile_ij(tm, RET_V_W))], _epi_store(F32))[0]

    owed = [_chip_core(_by_owner_cols(gw_pa)), _chip_core(gw_pb.reshape(N_DEV, -1, D_MODEL)),
            _chip_core(gw_out.reshape(N_DEV, -1, D_MODEL)), _chip_core(gw_up), _chip_core(gw_down.reshape(N_DEV, -1, D_MODEL))]
    names = ("w_proj_a", "w_proj_b", "w_out", "w_up", "w_down")
    got = _pair_exchange("grad_pair_exchange_late", owed)
    chip_sums = [_pair_sum(f"pair_sum_{n}", g, r, core, min(256, g.shape[2])) for n, g, r in zip(names, owed, got)]

    (dz_rest, g_gn_g, g_gn_b), parts_late = _ret_bwd(do_r, o_pre, states, z_rest, dz_rest, gn_g, gn_b,
                                                     _chip_exchange(chip_sums))
    dz_att, gq_parts, gk_parts = None, [], []
    for gi, ((_, d), nb) in enumerate(zip(ATT_GROUPS, ATT_BLOCKS_PER_STEP)):
        dz_att, gq_p, gk_p = _att_bwd_group(z_att, dz_att, do_a, delta, lmix, gq3, gk3, slopes3, gi, d, nb)
        gq_parts.append(gq_p)
        gk_parts.append(gk_p)
    g_qn = jnp.concatenate(gq_parts, axis=0).reshape(1, ATT_HEADS, DH)
    g_kn = jnp.concatenate(gk_parts, axis=0).reshape(1, ATT_HEADS, DH)

    gw_att = _mm("gw_in_att", xn, dz_att, "tn", D_MODEL, ATT_W, tk,
                 [(_sds((D_MODEL, 3 * ATT_W), BF16), _tile_ij(D_MODEL, ATT_W))], _epi_store(BF16))[0]
    gw_rest = _mm("gw_in_rest", xn, dz_rest, "tn", D_MODEL, 1024, tk,
                  [(_sds((D_MODEL, REST_W), BF16), _tile_ij(D_MODEL, 1024))], _epi_store(BF16))[0]
    inv = np.argsort(np.asarray(blocks))
    gw_in = jnp.concatenate([gw_att[:, p * DH:(p + 1) * DH] for p in inv] + [gw_rest], axis=1)
    owed_in = _chip_core(_by_owner_cols(gw_in))
    (got_in,) = _pair_exchange("grad_pair_exchange_w_in", [owed_in])
    chip_sum_in = _pair_sum("pair_sum_w_in", owed_in, got_in, core, min(256, owed_in.shape[2]))

    dxn_att = _mm("in_proj_att_bwd", dz_att, w_att_t, "nn", tm, D_MODEL, 3 * ATT_W,
                  [(_sds((S, D_MODEL), F32), row_tile)], _epi_store(F32))[0]

    def epi_norm1(acc, ex, out):
        dx, dg = _rms_bwd(acc + ex[0][...], ex[1][...], ex[3][...])
        out[0][...] = ex[2][...] + dx
        _acc_rows(out[1], dg, pl.program_id(0) == 0)

    short_tile = _tile_ij(ts, D_MODEL)
    (grad_x, g_norm1), parts_in = _mm(
        "in_proj_rest_bwd", dz_rest, w_rest_t, "nn", ts, D_MODEL, REST_W,
        [(_sds((S, D_MODEL), F32), short_tile), (_sds((1, D_MODEL), F32), vec)],
        epi_norm1, extras=[(dxn_att, short_tile), (x, short_tile), (dx1, short_tile), (norm1_g, vec)],
        exchange=_chip_exchange([chip_sum_in]))

    small = (g_norm1, g_qn, g_kn, g_gn_g, g_gn_b, g_norm2)
    return loss_local, grad_x, list(parts_in) + list(parts_late), small


def _position():
    return lax.axis_index("x"), lax.axis_index("y"), lax.axis_index("c")


def _other_chips(x, y):
    return [(1 - x, y), (x, 1 - y), (1 - x, 1 - y)]


_ANY = pl.BlockSpec(memory_space=pl.ANY)


def _gather_exchange(shards):
    nw = len(shards)

    def plan(x_refs, out_refs, sems):
        send_sems, recv_sems, local_sems = sems
        x, y, c = _position()
        me, sibling = (x, y, c), (x, y, 1 - c)
        chips = _other_chips(x, y)

        def copy(w, k, block, to, own=False):
            px, py, pc = block
            rows = out_refs[w].at[4 * px + 2 * py + pc]
            return pltpu.make_async_remote_copy(
                src_ref=x_refs[w] if own else rows, dst_ref=rows,
                send_sem=send_sems.at[7 * w + k], recv_sem=recv_sems.at[7 * w + k], device_id=to, device_id_type=MESH)

        def mine(w):
            return pltpu.make_async_copy(x_refs[w], out_refs[w].at[4 * x + 2 * y + c], local_sems.at[w])

        def own_sends(w):
            return [copy(w, 0, me, sibling, own=True)] + [copy(w, 1 + j, me, (*chip, c), own=True)
                                                          for j, chip in enumerate(chips)]

        def start():
            for w in range(nw):
                mine(w).start()
                for cp in own_sends(w):
                    cp.start()

        def relay():
            for j, chip in enumerate(chips):
                for w in range(nw):
                    copy(w, 1 + j, (*chip, c), me).wait_recv()
                    copy(w, 4 + j, (*chip, c), sibling).start()

        def finish():
            for w in range(nw):
                copy(w, 0, sibling, me).wait_recv()
                for j, chip in enumerate(chips):
                    copy(w, 4 + j, (*chip, 1 - c), me).wait_recv()
            for w in range(nw):
                for cp in own_sends(w):
                    cp.wait_send()
                for j, chip in enumerate(chips):
                    copy(w, 4 + j, (*chip, c), sibling).wait_send()
                mine(w).wait()

        return start, relay, finish

    return _Exchange(shards, [_sds((N_DEV,) + s.shape, s.dtype) for s in shards],
                     [pltpu.SemaphoreType.DMA((7 * nw,)), pltpu.SemaphoreType.DMA((7 * nw,)),
                      pltpu.SemaphoreType.DMA((nw,))], plan)


def _run_exchange(name, exchange):
    n_in, n_out = len(exchange.ins), len(exchange.out_shapes)

    def body(*refs):
        start, relay, finish = exchange.split(refs[:n_in], refs[n_in:n_in + n_out], refs[n_in + n_out:])
        start()
        if relay is not None:
            relay()
        finish()

    return pl.pallas_call(
        body, name=name, out_shape=exchange.out_shapes,
        in_specs=[_ANY] * n_in, out_specs=[_ANY] * n_out, scratch_shapes=exchange.sems,
    )(*exchange.ins)


def _pair_exchange(name, grads):
    ng = len(grads)

    def plan(g_refs, out_refs, sems):
        send_sems, recv_sems = sems
        x, y, c = _position()

        def copies():
            return [pltpu.make_async_remote_copy(
                src_ref=g_refs[w].at[:, 1 - c], dst_ref=out_refs[w], send_sem=send_sems.at[w],
                recv_sem=recv_sems.at[w], device_id=(x, y, 1 - c), device_id_type=MESH) for w in range(ng)]

        def start():
            for cp in copies():
                cp.start()

        def finish():
            for cp in copies():
                cp.wait()

        return start, None, finish

    return _run_exchange(name, _Exchange(
        grads, [_sds((N_CHIP,) + g.shape[2:], g.dtype) for g in grads],
        [pltpu.SemaphoreType.DMA((ng,)), pltpu.SemaphoreType.DMA((ng,))], plan))


def _small_all_gather(small):
    def plan(in_refs, out_refs, sems):
        (s_ref,), (s_out,) = in_refs, out_refs
        send_sems, recv_sems, local_sem = sems
        x, y, c = _position()
        me_id = 4 * x + 2 * y + c
        flips = [(a, b, e) for a in (0, 1) for b in (0, 1) for e in (0, 1)][1:]
        peers = [(x ^ a, y ^ b, c ^ e) for a, b, e in flips]

        def start():
            pltpu.make_async_copy(s_ref, s_out.at[me_id], local_sem).start()
            for k, p in enumerate(peers):
                pltpu.make_async_remote_copy(
                    src_ref=s_ref, dst_ref=s_out.at[me_id], send_sem=send_sems.at[k], recv_sem=recv_sems.at[k],
                    device_id=p, device_id_type=MESH).start()

        def finish():
            for k, (px, py, pc) in enumerate(peers):
                pltpu.make_async_remote_copy(
                    src_ref=s_ref, dst_ref=s_out.at[4 * px + 2 * py + pc], send_sem=send_sems.at[k],
                    recv_sem=recv_sems.at[k], device_id=(px, py, pc), device_id_type=MESH).wait()
            pltpu.make_async_copy(s_ref, s_out.at[me_id], local_sem).wait()

        return start, None, finish

    return _run_exchange("small_grad_all_gather", _Exchange(
        [small], [_sds((N_DEV,) + small.shape, small.dtype)],
        [pltpu.SemaphoreType.DMA((7,)), pltpu.SemaphoreType.DMA((7,)), pltpu.SemaphoreType.DMA], plan))[0]


def _pair_sum(name, g, got, core, tr):
    n_chip, _, R, C = g.shape

    def body(c_ref, a_ref, b_ref, o_ref):
        del c_ref
        o_ref[...] = (a_ref[...].astype(F32) + b_ref[...].astype(F32)).astype(o_ref.dtype)

    return pl.pallas_call(
        body, name=name,
        grid_spec=pltpu.PrefetchScalarGridSpec(
            num_scalar_prefetch=1, grid=(n_chip, R // tr),
            in_specs=[pl.BlockSpec((None, None, tr, C), lambda ch, i, c_ref: (ch, c_ref[0], i, 0)),
                      pl.BlockSpec((None, tr, C), lambda ch, i, c_ref: (ch, i, 0))],
            out_specs=pl.BlockSpec((None, tr, C), lambda ch, i, c_ref: (ch, i, 0))),
        out_shape=_sds(got.shape, got.dtype),
        compiler_params=_cparams(("parallel", "parallel")),
    )(core, g, got)


def _chip_exchange(parts):
    ng = len(parts)

    def plan(p_refs, out_refs, sems):
        send_sems, recv_sems, local_sems = sems
        x, y, c = _position()
        my_chip = 2 * x + y

        def copies():
            local = [pltpu.make_async_copy(p_refs[w].at[my_chip], out_refs[w].at[my_chip], local_sems.at[w])
                     for w in range(ng)]
            remote = [pltpu.make_async_remote_copy(
                src_ref=p_refs[w].at[2 * cx + cy], dst_ref=out_refs[w].at[my_chip],
                send_sem=send_sems.at[3 * w + k], recv_sem=recv_sems.at[3 * w + k],
                device_id=(cx, cy, c), device_id_type=MESH)
                for w in range(ng) for k, (cx, cy) in enumerate(_other_chips(x, y))]
            return local + remote

        def start():
            for cp in copies():
                cp.start()

        def finish():
            for cp in copies():
                cp.wait()

        return start, None, finish

    return _Exchange(parts, [_sds(p.shape, p.dtype) for p in parts],
                     [pltpu.SemaphoreType.DMA((3 * ng,)), pltpu.SemaphoreType.DMA((3 * ng,)),
                      pltpu.SemaphoreType.DMA((ng,))], plan)


def _adamw(name, parts, w, m, v, tr):
    n_parts = parts.shape[0]
    R, C = w.shape

    def body(p_ref, w_ref, m_ref, v_ref, g_ref, d_ref, mo_ref, vo_ref):
        g = p_ref[0].astype(F32)
        for i in range(1, n_parts):
            g = g + p_ref[i].astype(F32)
        m_new = ADAM_B1 * m_ref[...] + (1.0 - ADAM_B1) * g
        v_new = ADAM_B2 * v_ref[...] + (1.0 - ADAM_B2) * (g * g)
        m_hat = m_new / (1.0 - ADAM_B1 ** ADAM_STEP)
        v_hat = v_new / (1.0 - ADAM_B2 ** ADAM_STEP)
        g_ref[...] = g
        d_ref[...] = -ADAM_LR * (m_hat / (jnp.sqrt(v_hat) + ADAM_EPS) + ADAM_WD * w_ref[...])
        mo_ref[...] = m_new
        vo_ref[...] = v_new

    tile = pl.BlockSpec((tr, C), lambda i: (i, 0))
    return pl.pallas_call(
        body, name=name, grid=(R // tr,),
        in_specs=[pl.BlockSpec((n_parts, tr, C), lambda i: (0, i, 0)), tile, tile, tile],
        out_specs=[tile] * 4,
        out_shape=[_sds((R, C), F32)] * 4,
        compiler_params=_cparams(("parallel",)),
    )(parts, w, m, v)


def _flat_small(arrs):
    return jnp.concatenate([a.reshape(-1) for a in arrs]).reshape(SMALL_ROWS, LANES)


def _cols(g):
    return g.transpose(1, 0, 2).reshape(g.shape[1], -1)


def _rows_of(g):
    return g.reshape(-1, g.shape[2])


def _by_owner_cols(g):
    rows, cols = g.shape
    return g.reshape(rows, N_DEV, cols // N_DEV).transpose(1, 0, 2)


def _chip_core(g):
    return g.reshape((N_CHIP, 2) + g.shape[1:])


def kernel(x, norm1_g, w_in, q_norm_g, k_norm_g, ret_gn_g, ret_gn_b, w_proj_a, w_proj_b, w_out, norm2_g, w_up, w_down, loss_target, m_norm1_g, m_w_in, m_q_norm_g, m_k_norm_g, m_ret_gn_g, m_ret_gn_b, m_w_proj_a, m_w_proj_b, m_w_out, m_norm2_g, m_w_up, m_w_down, v_norm1_g, v_w_in, v_q_norm_g, v_k_norm_g, v_ret_gn_g, v_ret_gn_b, v_w_proj_a, v_w_proj_b, v_w_out, v_norm2_g, v_w_up, v_w_down):
    big_w = (w_in, w_proj_a, w_proj_b, w_out, w_up, w_down)
    big_m = (m_w_in, m_w_proj_a, m_w_proj_b, m_w_out, m_w_up, m_w_down)
    big_v = (v_w_in, v_w_proj_a, v_w_proj_b, v_w_out, v_w_up, v_w_down)
    small_w = (norm1_g, q_norm_g, k_norm_g, ret_gn_g, ret_gn_b, norm2_g)
    small_m = (m_norm1_g, m_q_norm_g, m_k_norm_g, m_ret_gn_g, m_ret_gn_b, m_norm2_g)
    small_v = (v_norm1_g, v_q_norm_g, v_k_norm_g, v_ret_gn_g, v_ret_gn_b, v_norm2_g)

    shards = [w[0].astype(BF16) for w in big_w]
    (g_in,) = _run_exchange("all_gather_w_in", _gather_exchange(shards[:1]))
    core = lax.axis_index("c").astype(jnp.int32).reshape(1)
    loss_local, grad_x, parts, small_g = _step(
        x[0], loss_target[0], norm1_g, q_norm_g[0], k_norm_g[0], ret_gn_g, ret_gn_b, norm2_g, _cols(g_in), shards[1:], core)
    small_all = _small_all_gather(_flat_small(small_g))
    names = ("w_in", "w_proj_a", "w_proj_b", "w_out", "w_up", "w_down")

    res = {}
    for n, p, w, m, v in zip(names, parts, big_w, big_m, big_v):
        outs = _adamw(f"adamw_{n}", p, w[0], m[0], v[0], min(128, w.shape[1]))
        res[n] = [o[None] for o in outs]
    s_outs = _adamw("adamw_small", small_all, _flat_small(small_w), _flat_small(small_m), _flat_small(small_v), SMALL_ROWS)
    small_names = ("norm1_g", "q_norm_g", "k_norm_g", "ret_gn_g", "ret_gn_b", "norm2_g")
    for n in small_names:
        res[n] = []
    for o in s_outs:
        flat, off = o.reshape(-1), 0
        for n, w in zip(small_names, small_w):
            res[n].append(flat[off:off + w.size].reshape(w.shape))
            off += w.size

    order = ("norm1_g", "w_in", "q_norm_g", "k_norm_g", "ret_gn_g", "ret_gn_b", "w_proj_a", "w_proj_b", "w_out",
             "norm2_g", "w_up", "w_down")
    loss = lax.psum(loss_local, MESH_AXES)
    return (loss, grad_x[None], *[res[n][0] for n in order], *[res[n][1] for n in order],
            *[res[n][2] for n in order], *[res[n][3] for n in order])
```

```python
import math

import numpy as np
import jax
import jax.numpy as jnp
from jax import lax
from jax.experimental import pallas as pl
from jax.experimental.pallas import tpu as pltpu

F32 = jnp.float32
BF16 = jnp.bfloat16

D_MODEL = 1024
ATT_GROUPS = ((128, 1), (512, 4), (2048, 16))
ATT_BLOCKS_PER_STEP = (8, 1, 1)
HPG = 4
ATT_HEADS = 12
DH = 128
BLK = 128
ATT_W = ATT_HEADS * DH
ATT_OUT_W = HPG * DH
RET_HEADS = 4
RET_QK = 256
RET_V = 512
RET_QK_W = RET_HEADS * RET_QK
RET_V_W = RET_HEADS * RET_V
CHUNK = 128
D_FF = 4096
IN_W = 12800
REST_W = IN_W - 3 * ATT_W
RET_COLS = 6144
EPS = 1e-6
ADAM_LR, ADAM_B1, ADAM_B2, ADAM_EPS, ADAM_WD, ADAM_STEP = 0.001, 0.9, 0.999, 1e-08, 0.01, 10
N_DEV = 8
N_CHIP = 4
MESH_AXES = ("x", "y", "c")
MESH = pl.DeviceIdType.MESH
VMEM_LIMIT = 56 * 1024 * 1024
LANES = 128
NEG = -1e30

_NN = (((1,), (0,)), ((), ()))
_NT = (((1,), (1,)), ((), ()))
_TN = (((0,), (0,)), ((), ()))

R_Q, R_K, R_V, R_G, R_GA, R_GB = 0, 1024, 2048, 4096, 6144, 7168

LOG_GAMMA = [float(v) for v in np.log(1.0 - 2.0 ** (-5.0 - np.arange(RET_HEADS, dtype=np.float32))).astype(np.float32)]
ALIBI = np.asarray(2.0 ** (-8.0 * np.arange(1, ATT_HEADS + 1, dtype=np.float32) / ATT_HEADS), np.float32)

SMALL_ROWS = (1024 + 1536 + 1536 + 2048 + 2048 + 1024) // LANES


def _dot(a, b, dims=_NN):
    return lax.dot_general(a, b, dims, preferred_element_type=F32)


def _cparams(sem):
    return pltpu.CompilerParams(dimension_semantics=sem, vmem_limit_bytes=VMEM_LIMIT)


def _sds(shape, dtype):
    return jax.ShapeDtypeStruct(shape, dtype)


class _Exchange:
    def __init__(self, ins, out_shapes, sems, plan):
        self.ins, self.out_shapes, self.sems, self.plan = list(ins), list(out_shapes), list(sems), plan

    def split(self, in_refs, out_refs, sem_refs):
        return self.plan(in_refs, out_refs, sem_refs)


def _mm(name, a, b, mode, tm, tn, tk, outs, epi, extras=(), b_pro=None,
        sem=("parallel", "parallel", "arbitrary"), exchange=None):
    if mode == "nn":
        (M, K), (_, N) = a.shape, b.shape
        a_spec = pl.BlockSpec((tm, tk), lambda i, j, k: (i, k))
        dims = _NN
    else:
        (K, M), (_, N) = a.shape, b.shape
        a_spec = pl.BlockSpec((tk, tm), lambda i, j, k: (k, i))
        dims = _TN
    assert M % tm == 0 and N % tn == 0 and K % tk == 0, (name, M, N, K, tm, tn, tk)
    nk = K // tk
    whole_b = dict(pipeline_mode=pl.Buffered(1)) if (nk == 1 and N == tn) else {}
    b_spec = pl.BlockSpec((tk, tn), lambda i, j, k: (k, j), **whole_b)
    n_ex, n_out = len(extras), len(outs)
    grid = (M // tm, N // tn, nk)
    n_xi = len(exchange.ins) if exchange else 0
    n_xo = len(exchange.out_shapes) if exchange else 0
    n_acc = 1 if nk > 1 else 0

    def body(a_ref, b_ref, *rest):
        ex, rest = rest[:n_ex], rest[n_ex:]
        x_in, rest = rest[:n_xi], rest[n_xi:]
        out, rest = rest[:n_out], rest[n_out:]
        x_out, rest = rest[:n_xo], rest[n_xo:]
        step = (pl.program_id(0) * grid[1] + pl.program_id(1)) * grid[2] + pl.program_id(2)
        n_steps = grid[0] * grid[1] * grid[2]
        if exchange:
            start, relay, finish = exchange.split(x_in, x_out, rest[n_acc:])
            pl.when(step == 0)(start)
        bv = b_ref[...]
        if b_pro is not None:
            bv = b_pro(bv)
        part = _dot(a_ref[...].astype(BF16), bv.astype(BF16), dims)
        if nk == 1:
            epi(part, ex, out)
        else:
            acc_ref = rest[0]
            k = pl.program_id(2)

            @pl.when(k == 0)
            def _():
                acc_ref[...] = part

            @pl.when(k > 0)
            def _():
                acc_ref[...] += part

            @pl.when(k == nk - 1)
            def _():
                epi(acc_ref[...], ex, out)
        if exchange:
            if relay is not None:
                pl.when(step == (7 * n_steps) // 8)(relay)
            pl.when(step == n_steps - 1)(finish)

    res = pl.pallas_call(
        body,
        name=name,
        grid=grid,
        in_specs=[a_spec, b_spec] + [s for _, s in extras] + [_ANY] * n_xi,
        out_specs=[s for _, s in outs] + [_ANY] * n_xo,
        out_shape=[o for o, _ in outs] + (exchange.out_shapes if exchange else []),
        scratch_shapes=([pltpu.VMEM((tm, tn), F32)] if nk > 1 else []) + (exchange.sems if exchange else []),
        compiler_params=_cparams(("arbitrary",) * 3 if exchange else sem),
    )(a, b, *[e for e, _ in extras], *(exchange.ins if exchange else []))
    return (res[:n_out], res[n_out:]) if exchange else res


def _tile_ij(tm, tn):
    return pl.BlockSpec((tm, tn), lambda i, j, k: (i, j))


def _epi_store(dtype):
    def epi(acc, ex, out):
        out[0][...] = acc.astype(dtype)
    return epi


def _rms_rows(x):
    return lax.rsqrt(jnp.mean(x * x, axis=-1, keepdims=True) + EPS)


def _acc_rows(ref, part, first):
    @pl.when(first)
    def _():
        ref[...] = part

    @pl.when(jnp.logical_not(first))
    def _():
        ref[...] += part


def _rmsnorm_fwd(x, g, tm):
    S, Dm = x.shape

    def body(x_ref, g_ref, o_ref):
        xv = x_ref[...]
        o_ref[...] = (xv * _rms_rows(xv) * g_ref[...]).astype(BF16)

    return pl.pallas_call(
        body, name="rmsnorm1_fwd", grid=(S // tm,),
        in_specs=[pl.BlockSpec((tm, Dm), lambda i: (i, 0)), pl.BlockSpec((1, Dm), lambda i: (0, 0))],
        out_specs=pl.BlockSpec((tm, Dm), lambda i: (i, 0)),
        out_shape=_sds((S, Dm), BF16),
        compiler_params=_cparams(("parallel",)),
    )(x, g)


def _rows(ref, r, b, d):
    if d == 1:
        return ref[b * BLK:(b + 1) * BLK, :]
    return ref[pl.ds(b * BLK * d + r, BLK, stride=d), :]


def _put_rows(ref, r, b, d, val):
    if d == 1:
        ref[b * BLK:(b + 1) * BLK, :] = val
    else:
        ref[pl.ds(b * BLK * d + r, BLK, stride=d), :] = val


def _head_norm(x, g):
    r = _rms_rows(x)
    xh = x * r
    return (xh * g).astype(BF16), xh, r


def _head_norm_bwd(dyn, xh, r, g):
    dxh = dyn * g
    dx = r * (dxh - xh * jnp.mean(dxh * xh, axis=-1, keepdims=True))
    return dx, jnp.sum(dyn * xh, axis=0, keepdims=True)


def _att_mask_bias(slope, d, first):
    qi = lax.broadcasted_iota(jnp.int32, (BLK, 2 * BLK), 0)
    kj = lax.broadcasted_iota(jnp.int32, (BLK, 2 * BLK), 1)
    dist = BLK + qi - kj
    valid = (dist >= 0) & (dist <= BLK)
    if first is not None:
        valid = valid & (jnp.logical_not(first) | (kj >= BLK))
    bias = -slope * (dist * d).astype(F32)
    return valid, bias


def _att_specs(gi, d, nb, S):
    span = BLK * d
    sb = span * nb
    nspan = S // span
    before = lambda n: jnp.maximum(n * nb - 1, 0)
    after = lambda n: jnp.minimum((n + 1) * nb, nspan - 1)
    zcol = lambda j, kind: 3 * (gi * HPG + j) + kind
    cur = lambda kind: pl.BlockSpec((sb, DH), lambda j, n: (n, zcol(j, kind)))
    prev = lambda kind: pl.BlockSpec((span, DH), lambda j, n: (before(n), zcol(j, kind)))
    nxt = lambda kind: pl.BlockSpec((span, DH), lambda j, n: (after(n), zcol(j, kind)))
    slot = pl.BlockSpec((sb, DH), lambda j, n: (n, j))
    slot_next = pl.BlockSpec((span, DH), lambda j, n: (after(n), j))
    head = pl.BlockSpec((None, 1, DH), lambda j, n: (gi * HPG + j, 0, 0))
    return cur, prev, nxt, slot, slot_next, head


def _att_fwd_group(z_att, gq3, gk3, slopes3, gi, d, nb, others=()):
    S = z_att.shape[0]
    nsb = S // (BLK * d * nb)
    scale = DH ** -0.5
    n_other = len(others)

    def body(q_ref, k_ref, v_ref, kp_ref, vp_ref, gq_ref, gk_ref, sl_ref, *rest):
        other_refs, (o_ref, l_ref) = rest[:2 * n_other], rest[2 * n_other:]
        slope = sl_ref[...][:, :1]
        valid0, bias = _att_mask_bias(slope, d, pl.program_id(1) == 0)
        valid_in, _ = _att_mask_bias(slope, d, None)
        gq, gk = gq_ref[...], gk_ref[...]
        for r in range(d):
            kp = _head_norm(_rows(kp_ref, r, 0, d), gk)[0]
            vp = _rows(vp_ref, r, 0, d).astype(BF16)
            for b in range(nb):
                q = _head_norm(_rows(q_ref, r, b, d), gq)[0]
                kc = _head_norm(_rows(k_ref, r, b, d), gk)[0]
                vc = _rows(v_ref, r, b, d).astype(BF16)
                k2 = jnp.concatenate([kp, kc], axis=0)
                v2 = jnp.concatenate([vp, vc], axis=0)
                s = jnp.where(valid0 if b == 0 else valid_in, _dot(q, k2, _NT) * scale + bias, NEG)
                m = jnp.max(s, axis=-1, keepdims=True)
                p = jnp.exp(s - m)
                den = jnp.sum(p, axis=-1, keepdims=True)
                _put_rows(o_ref, r, b, d, _dot(p.astype(BF16), v2) / den)
                _put_rows(l_ref, r, b, d, jnp.broadcast_to(m + jnp.log(den), (BLK, DH)))
                kp, vp = kc, vc
        if n_other:
            os_ = [ref[...] for ref in other_refs[:n_other]] + [o_ref[...]]
            ls_ = [ref[...] for ref in other_refs[n_other:]] + [l_ref[...]]
            m = ls_[0]
            for l in ls_[1:]:
                m = jnp.maximum(m, l)
            es = [jnp.exp(l - m) for l in ls_]
            tot, mix = es[0], es[0] * os_[0]
            for e, o in zip(es[1:], os_[1:]):
                tot, mix = tot + e, mix + e * o
            o_ref[...] = mix / tot
            l_ref[...] = m + jnp.log(tot)

    cur, prev, _, slot, _, head = _att_specs(gi, d, nb, S)
    return pl.pallas_call(
        body, name=f"att_fwd_g{gi}", grid=(HPG, nsb),
        in_specs=[cur(0), cur(1), cur(2), prev(1), prev(2), head, head, head] + [slot] * (2 * n_other),
        out_specs=[slot, slot],
        out_shape=[_sds((S, ATT_OUT_W), F32), _sds((S, ATT_OUT_W), F32)],
        compiler_params=_cparams(("parallel", "arbitrary")),
    )(z_att, z_att, z_att, z_att, z_att, gq3, gk3, slopes3, *[o for o, _ in others], *[l for _, l in others])


def _ret_tables(lg):
    ri = lax.broadcasted_iota(jnp.int32, (CHUNK, CHUNK), 0)
    ci = lax.broadcasted_iota(jnp.int32, (CHUNK, CHUNK), 1)
    diff = (ri - ci).astype(F32)
    decay = jnp.where(diff >= 0, jnp.exp(lg * jnp.maximum(diff, 0.0)), 0.0)
    idx = lax.broadcasted_iota(jnp.int32, (CHUNK, 1), 0).astype(F32)
    xi = jnp.exp(lg * (idx + 1.0))
    zeta = jnp.exp(lg * (CHUNK - 1.0 - idx))
    return decay, xi, zeta, math.exp(lg * CHUNK)


def _ret_specs(nch, rev):
    idx = (lambda n: nch - 1 - n) if rev else (lambda n: n)
    qk = lambda off: pl.BlockSpec((CHUNK, RET_QK_W), lambda n: (idx(n), off // RET_QK_W))
    vv = lambda off: pl.BlockSpec((CHUNK, RET_V_W), lambda n: (idx(n), off // RET_V_W))
    par = pl.BlockSpec((1, RET_V_W), lambda n: (0, 0))
    wide = pl.BlockSpec((CHUNK, RET_V_W), lambda n: (idx(n), 0))
    st = pl.BlockSpec((RET_HEADS, None, RET_QK, RET_V), lambda n: (0, idx(n), 0, 0))
    return qk, vv, par, wide, st


def _ret_fwd(z_rest, gn_g, gn_b):
    S = z_rest.shape[0]
    nch = S // CHUNK

    def body(q_ref, k_ref, v_ref, gr_ref, g_ref, b_ref, or_ref, o_ref, st_ref, state):
        @pl.when(pl.program_id(0) == 0)
        def _():
            state[...] = jnp.zeros_like(state)

        for h in range(RET_HEADS):
            decay, xi, zeta, gch = _ret_tables(LOG_GAMMA[h])
            cq = slice(h * RET_QK, (h + 1) * RET_QK)
            cv = slice(h * RET_V, (h + 1) * RET_V)
            q = q_ref[:, cq]
            kc32 = k_ref[:, cq].astype(F32) * (RET_QK ** -0.5)
            kc = kc32.astype(BF16)
            v = v_ref[:, cv]
            st = state[h]
            stb = st.astype(BF16)
            st_ref[h] = stb
            s = _dot(q, kc, _NT) * decay
            o = _dot(s.astype(BF16), v) + _dot(q, stb) * xi
            state[h] = st * gch + _dot((kc32 * zeta).astype(BF16), v, _TN)
            mu = jnp.mean(o, axis=-1, keepdims=True)
            cen = o - mu
            yh = cen * lax.rsqrt(jnp.mean(cen * cen, axis=-1, keepdims=True) + EPS)
            gr = gr_ref[:, cv].astype(F32)
            or_ref[:, cv] = ((yh * g_ref[:, cv] + b_ref[:, cv]) * (gr * jax.nn.sigmoid(gr))).astype(BF16)
            o_ref[:, cv] = o

    qk, vv, par, wide, st = _ret_specs(nch, False)
    return pl.pallas_call(
        body, name="ret_fwd", grid=(nch,),
        in_specs=[qk(R_Q), qk(R_K), vv(R_V), vv(R_G), par, par],
        out_specs=[wide, wide, st],
        out_shape=[_sds((S, RET_V_W), BF16), _sds((S, RET_V_W), F32), _sds((RET_HEADS, nch, RET_QK, RET_V), BF16)],
        scratch_shapes=[pltpu.VMEM((RET_HEADS, RET_QK, RET_V), F32)],
        compiler_params=_cparams(("arbitrary",)),
    )(z_rest, z_rest, z_rest, z_rest, gn_g, gn_b)


def _merge_fwd(o_a, o_r, z_rest, x, wpa, wpb, wout, g2, tm):
    S = x.shape[0]

    def body(oa_ref, or_ref, ga_ref, gb_ref, x_ref, wpa_ref, wpb_ref, wo_ref, g2_ref,
             x1_ref, y_ref, pa_ref, pb_ref, xn2_ref):
        pa = _dot(oa_ref[...].astype(BF16), wpa_ref[...])
        pb = _dot(or_ref[...], wpb_ref[...])
        y = jax.nn.sigmoid(ga_ref[...].astype(F32)) * pa + jax.nn.sigmoid(gb_ref[...].astype(F32)) * pb
        yb = y.astype(BF16)
        x1 = x_ref[...] + _dot(yb, wo_ref[...])
        x1_ref[...] = x1
        y_ref[...] = yb
        pa_ref[...] = pa.astype(BF16)
        pb_ref[...] = pb.astype(BF16)
        xn2_ref[...] = (x1 * _rms_rows(x1) * g2_ref[...]).astype(BF16)

    row = lambda w: pl.BlockSpec((tm, w), lambda i: (i, 0))
    full = lambda a: pl.BlockSpec(a.shape, lambda i: (0, 0))
    return pl.pallas_call(
        body, name="merge_fwd", grid=(S // tm,),
        in_specs=[row(ATT_OUT_W), row(RET_V_W),
                  pl.BlockSpec((tm, D_MODEL), lambda i: (i, R_GA // D_MODEL)),
                  pl.BlockSpec((tm, D_MODEL), lambda i: (i, R_GB // D_MODEL)),
                  row(D_MODEL), full(wpa), full(wpb), full(wout), full(g2)],
        out_specs=[row(D_MODEL)] * 5,
        out_shape=[_sds((S, D_MODEL), F32)] + [_sds((S, D_MODEL), BF16)] * 4,
        compiler_params=_cparams(("parallel",)),
    )(o_a, o_r, z_rest, z_rest, x, wpa, wpb, wout, g2)


def _rms_bwd(dy, xv, g):
    r = _rms_rows(xv)
    xh = xv * r
    dg = dy * g
    dx = r * (dg - xh * jnp.mean(dg * xh, axis=-1, keepdims=True))
    return dx, jnp.sum(dy * xh, axis=0, keepdims=True)


def _ret_bwd(do_r, o_pre, states, z_rest, dz_rest, gn_g, gn_b, exchange):
    S = z_rest.shape[0]
    nch = S // CHUNK
    n_xi, n_xo = len(exchange.ins), len(exchange.out_shapes)

    def body(do_ref, o_ref, st_ref, q_ref, k_ref, v_ref, gr_ref, g_ref, b_ref, dz_in, *rest):
        del dz_in
        x_in, (dz_ref, dg_ref, db_ref), rest = rest[:n_xi], rest[n_xi:n_xi + 3], rest[n_xi + 3:]
        x_out, gst, x_sems = rest[:n_xo], rest[n_xo], rest[n_xo + 1:]
        start, _, finish = exchange.split(x_in, x_out, x_sems)
        first = pl.program_id(0) == 0
        pl.when(first)(start)

        @pl.when(first)
        def _():
            gst[...] = jnp.zeros_like(gst)

        dgs, dbs = [], []
        for h in range(RET_HEADS):
            decay, xi, zeta, gch = _ret_tables(LOG_GAMMA[h])
            cq = slice(h * RET_QK, (h + 1) * RET_QK)
            cv = slice(h * RET_V, (h + 1) * RET_V)
            o = o_ref[:, cv]
            mu = jnp.mean(o, axis=-1, keepdims=True)
            cen = o - mu
            rstd = lax.rsqrt(jnp.mean(cen * cen, axis=-1, keepdims=True) + EPS)
            yh = cen * rstd
            gam = g_ref[:, cv]
            y = yh * gam + b_ref[:, cv]
            gr = gr_ref[:, cv].astype(F32)
            sg = jax.nn.sigmoid(gr)
            dout = do_ref[:, cv]
            dy = dout * (gr * sg)
            dz_ref[:, R_G + h * RET_V:R_G + (h + 1) * RET_V] = (dout * y * (sg * (1.0 + gr * (1.0 - sg)))).astype(BF16)
            dgs.append(jnp.sum(dy * yh, axis=0, keepdims=True))
            dbs.append(jnp.sum(dy, axis=0, keepdims=True))
            dyh = dy * gam
            do = rstd * (dyh - jnp.mean(dyh, axis=-1, keepdims=True)
                         - yh * jnp.mean(dyh * yh, axis=-1, keepdims=True))

            q = q_ref[:, cq]
            kc32 = k_ref[:, cq].astype(F32) * (RET_QK ** -0.5)
            kc = kc32.astype(BF16)
            v = v_ref[:, cv]
            dob = do.astype(BF16)
            a = (_dot(q, kc, _NT) * decay).astype(BF16)
            da = (_dot(dob, v, _NT) * decay).astype(BF16)
            dcross = (do * xi).astype(BF16)
            g_next = gst[h]
            gb = g_next.astype(BF16)
            dq = _dot(da, kc) + _dot(dcross, st_ref[h], _NT)
            dkc = _dot(da, q, _TN)
            dkz = _dot(v, gb, _NT)
            dv = _dot(a, dob, _TN) + _dot((kc32 * zeta).astype(BF16), gb)
            gst[h] = g_next * gch + _dot(q, dcross, _TN)
            dz_ref[:, R_Q + h * RET_QK:R_Q + (h + 1) * RET_QK] = dq.astype(BF16)
            dz_ref[:, R_K + h * RET_QK:R_K + (h + 1) * RET_QK] = ((dkc + dkz * zeta) * (RET_QK ** -0.5)).astype(BF16)
            dz_ref[:, R_V + h * RET_V:R_V + (h + 1) * RET_V] = dv.astype(BF16)
        _acc_rows(dg_ref, jnp.concatenate(dgs, axis=1), first)
        _acc_rows(db_ref, jnp.concatenate(dbs, axis=1), first)
        pl.when(pl.program_id(0) == nch - 1)(finish)

    qk, vv, par, wide, st = _ret_specs(nch, True)
    res = pl.pallas_call(
        body, name="ret_bwd", grid=(nch,),
        in_specs=[wide, wide, st, qk(R_Q), qk(R_K), vv(R_V), vv(R_G), par, par, _ANY] + [_ANY] * n_xi,
        out_specs=[pl.BlockSpec((CHUNK, RET_COLS), lambda n: (nch - 1 - n, 0)), par, par] + [_ANY] * n_xo,
        out_shape=[_sds(dz_rest.shape, BF16), _sds((1, RET_V_W), F32), _sds((1, RET_V_W), F32)] + exchange.out_shapes,
        input_output_aliases={9: 0},
        scratch_shapes=[pltpu.VMEM((RET_HEADS, RET_QK, RET_V), F32)] + exchange.sems,
        compiler_params=_cparams(("arbitrary",)),
    )(do_r, o_pre, states, z_rest, z_rest, z_rest, z_rest, gn_g, gn_b, dz_rest, *exchange.ins)
    return res[:3], res[3:]


def _att_probs(q, k, lmix, valid, bias):
    s = _dot(q, k, _NT) * (DH ** -0.5) + bias
    return jnp.where(valid, jnp.exp(jnp.where(valid, s, NEG) - lmix), 0.0)


def _att_bwd_group(z_att, dz_att, do_a, delta, lmix, gq3, gk3, slopes3, gi, d, nb):
    S = z_att.shape[0]
    sb = BLK * d * nb
    nsb = S // sb
    scale = DH ** -0.5
    aliased = dz_att is not None

    def body(q_ref, k_ref, v_ref, kp_ref, vp_ref, qn_ref, do_ref, don_ref, dl_ref, dln_ref, lm_ref, lmn_ref,
             gq_ref, gk_ref, sl_ref, *rest):
        dz_ref, dgq_ref, dgk_ref, stage = rest[-4:]
        n = pl.program_id(1)
        slope = sl_ref[...][:, :1]
        valid0, bias = _att_mask_bias(slope, d, n == 0)
        valid_in, _ = _att_mask_bias(slope, d, None)
        qi = lax.broadcasted_iota(jnp.int32, (BLK, BLK), 0)
        kj = lax.broadcasted_iota(jnp.int32, (BLK, BLK), 1)
        dist_n = BLK + qi - kj
        valid_n_in = dist_n <= BLK
        valid_n_last = valid_n_in & (n < nsb - 1)
        bias_n = -slope * (dist_n * d).astype(F32)
        gq, gk = gq_ref[...], gk_ref[...]
        dgq = jnp.zeros((1, DH), F32)
        dgk = jnp.zeros((1, DH), F32)
        for r in range(d):
            memo = {}

            def get(kind, b):
                if (kind, b) not in memo:
                    inner = 0 <= b < nb
                    bb = b if inner else 0
                    if kind == "q":
                        val = _head_norm(_rows(q_ref if inner else qn_ref, r, bb, d), gq)
                    elif kind == "k":
                        val = _head_norm(_rows(k_ref if inner else kp_ref, r, bb, d), gk)
                    elif kind == "v":
                        val = _rows(v_ref if inner else vp_ref, r, bb, d).astype(BF16)
                    elif kind == "do":
                        val = _rows(do_ref if inner else don_ref, r, bb, d).astype(BF16)
                    elif kind == "dl":
                        val = _rows(dl_ref if inner else dln_ref, r, bb, d)[:, :1]
                    else:
                        val = _rows(lm_ref if inner else lmn_ref, r, bb, d)[:, :1]
                    memo[(kind, b)] = val
                return memo[(kind, b)]

            for b in range(nb):
                q, qh, qr = get("q", b)
                kc, kh, kr = get("k", b)
                qn = get("q", b + 1)[0]
                vc, do_c, do_n = get("v", b), get("do", b), get("do", b + 1)
                k2 = jnp.concatenate([get("k", b - 1)[0], kc], axis=0)
                v2 = jnp.concatenate([get("v", b - 1), vc], axis=0)
                p = _att_probs(q, k2, get("lm", b), valid0 if b == 0 else valid_in, bias)
                ds = (p * (_dot(do_c, v2, _NT) - get("dl", b)) * scale).astype(BF16)
                dx, dg = _head_norm_bwd(_dot(ds, k2), qh, qr, gq)
                _put_rows(stage.at[0], r, b, d, dx)
                dgq = dgq + dg
                p_n = _att_probs(qn, kc, get("lm", b + 1), valid_n_last if b == nb - 1 else valid_n_in, bias_n)
                ds_n = (p_n * (_dot(do_n, vc, _NT) - get("dl", b + 1)) * scale).astype(BF16)
                dk = _dot(ds[:, BLK:], q, _TN) + _dot(ds_n, qn, _TN)
                dv = _dot(p[:, BLK:].astype(BF16), do_c, _TN) + _dot(p_n.astype(BF16), do_n, _TN)
                dx, dg = _head_norm_bwd(dk, kh, kr, gk)
                _put_rows(stage.at[1], r, b, d, dx)
                _put_rows(stage.at[2], r, b, d, dv)
                dgk = dgk + dg
        for kind in range(3):
            dz_ref[:, kind * DH:(kind + 1) * DH] = stage[kind].astype(BF16)
        _acc_rows(dgq_ref, dgq, n == 0)
        _acc_rows(dgk_ref, dgk, n == 0)

    cur, prev, nxt, slot, slot_next, head = _att_specs(gi, d, nb, S)
    gain = pl.BlockSpec((None, 1, DH), lambda j, n: (j, 0, 0))
    in_specs = [cur(0), cur(1), cur(2), prev(1), prev(2), nxt(0),
                slot, slot_next, slot, slot_next, slot, slot_next, head, head, head]
    args = [z_att] * 6 + [do_a, do_a, delta, delta, lmix, lmix, gq3, gk3, slopes3]
    if aliased:
        in_specs.append(pl.BlockSpec(memory_space=pl.ANY))
        args.append(dz_att)
    return pl.pallas_call(
        body, name=f"att_bwd_g{gi}", grid=(HPG, nsb),
        in_specs=in_specs,
        out_specs=[pl.BlockSpec((sb, 3 * DH), lambda j, n: (n, gi * HPG + j)), gain, gain],
        out_shape=[_sds(z_att.shape, BF16), _sds((HPG, 1, DH), F32), _sds((HPG, 1, DH), F32)],
        input_output_aliases={len(args) - 1: 0} if aliased else {},
        scratch_shapes=[pltpu.VMEM((3, sb, DH), F32)],
        compiler_params=_cparams(("parallel", "arbitrary")),
    )(*args)


def _step(x, target, norm1_g, q_norm_g, k_norm_g, gn_g, gn_b, norm2_g, w_in, later_shards, core):
    S = x.shape[0]
    tm = min(512, S)
    ts = min(256, S)
    tk = min(2048, S)
    w_att = w_in[:, :3 * ATT_W].reshape(D_MODEL, 3, ATT_HEADS, DH).transpose(0, 2, 1, 3).reshape(D_MODEL, 3 * ATT_W)
    w_rest = w_in[:, 3 * ATT_W:]
    w_att_t, w_rest_t = w_att.T, w_rest.T
    gq3 = q_norm_g.reshape(ATT_HEADS, 1, DH)
    gk3 = k_norm_g.reshape(ATT_HEADS, 1, DH)
    slopes3 = jnp.asarray(np.broadcast_to(ALIBI[:, None, None], (ATT_HEADS, 1, DH)).copy())

    xn = _rmsnorm_fwd(x, norm1_g, tm)
    z_att = _mm("in_proj_att", xn, w_att, "nn", ts, 3 * ATT_W, D_MODEL,
                [(_sds((S, 3 * ATT_W), F32), _tile_ij(ts, 3 * ATT_W))], _epi_store(F32))[0]
    (z_rest,), gathered = _mm("in_proj_rest", xn, w_rest, "nn", ts, REST_W, D_MODEL,
                              [(_sds((S, REST_W), BF16), _tile_ij(ts, REST_W))], _epi_store(BF16),
                              exchange=_gather_exchange(later_shards))
    w_pa, w_pb, w_out, w_up, w_down = [f(g) for f, g in zip((_cols, _rows_of, _rows_of, _cols, _rows_of), gathered)]
    w_up_t, w_down_t, w_out_t, w_pa_t, w_pb_t = w_up.T, w_down.T, w_out.T, w_pa.T, w_pb.T
    done = []
    for gi, ((_, d), nb) in enumerate(zip(ATT_GROUPS, ATT_BLOCKS_PER_STEP)):
        last = gi == len(ATT_GROUPS) - 1
        done.append(_att_fwd_group(z_att, gq3, gk3, slopes3, gi, d, nb, others=tuple(done) if last else ()))
    o_a, lmix = done[-1]
    o_r, o_pre, states = _ret_fwd(z_rest, gn_g, gn_b)
    x1, y, pa, pb, xn2 = _merge_fwd(o_a, o_r, z_rest, x, w_pa, w_pb, w_out, norm2_g, min(256, S))

    def epi_up(acc, ex, out):
        r = jnp.maximum(acc, 0.0)
        out[0][...] = (r * r).astype(BF16)
        out[1][...] = r.astype(BF16)

    h, relu_u = _mm("mlp_up", xn2, w_up, "nn", ts, D_FF, D_MODEL,
                    [(_sds((S, D_FF), BF16), _tile_ij(ts, D_FF)), (_sds((S, D_FF), BF16), _tile_ij(ts, D_FF))], epi_up)

    def epi_down(acc, ex, out):
        diff = ex[0][...] + acc - ex[1][...]
        out[0][...] = diff * (1.0 / D_MODEL)
        out[1][...] = jnp.broadcast_to(jnp.sum(diff * diff) * (1.0 / (8 * LANES)), (8, LANES))

    row_tile = _tile_ij(tm, D_MODEL)
    dx2, loss_parts = _mm(
        "mlp_down_loss", h, w_down, "nn", tm, D_MODEL, D_FF,
        [(_sds((S, D_MODEL), F32), row_tile),
         (_sds((S // tm * 8, LANES), F32), pl.BlockSpec((8, LANES), lambda i, j, k: (i, 0)))],
        epi_down, extras=[(x1, row_tile), (target, row_tile)])
    loss_local = jnp.sum(loss_parts) * (0.5 / D_MODEL)

    def epi_du(acc, ex, out):
        out[0][...] = (acc * (2.0 * ex[0][...].astype(F32))).astype(BF16)

    du = _mm("mlp_down_bwd", dx2, w_down_t, "nn", ts, D_FF, D_MODEL,
             [(_sds((S, D_FF), BF16), _tile_ij(ts, D_FF))], epi_du, extras=[(relu_u, _tile_ij(ts, D_FF))])[0]
    gw_down = _mm("gw_down", h, dx2, "tn", 1024, D_MODEL, tk,
                  [(_sds((D_FF, D_MODEL), BF16), _tile_ij(1024, D_MODEL))], _epi_store(BF16))[0]
    gw_up = _mm("gw_up", xn2, du, "tn", D_MODEL, 512, tk,
                [(_sds((N_DEV, D_MODEL, 512), BF16), pl.BlockSpec((None, D_MODEL, 512), lambda i, j, k: (j, 0, 0)))],
                _epi_store(BF16))[0]

    vec = pl.BlockSpec((1, D_MODEL), lambda i, j, k: (0, 0))
    seq_sem = ("arbitrary", "arbitrary", "arbitrary")

    def epi_norm2(acc, ex, out):
        dx, dg = _rms_bwd(acc, ex[0][...], ex[2][...])
        out[0][...] = ex[1][...] + dx
        _acc_rows(out[1], dg, pl.program_id(0) == 0)

    dx1, g_norm2 = _mm(
        "mlp_up_bwd", du, w_up_t, "nn", tm, D_MODEL, D_FF,
        [(_sds((S, D_MODEL), F32), row_tile), (_sds((1, D_MODEL), F32), vec)],
        epi_norm2, extras=[(x1, row_tile), (dx2, row_tile), (norm2_g, vec)], sem=seq_sem)

    def epi_dy(acc, ex, out):
        sa = jax.nn.sigmoid(ex[0][...].astype(F32))
        sb = jax.nn.sigmoid(ex[1][...].astype(F32))
        out[0][...] = (acc * sa).astype(BF16)
        out[1][...] = (acc * sb).astype(BF16)
        out[2][:, :D_MODEL] = (acc * ex[2][...].astype(F32) * (sa * (1.0 - sa))).astype(BF16)
        out[2][:, D_MODEL:] = (acc * ex[3][...].astype(F32) * (sb * (1.0 - sb))).astype(BF16)

    ga_spec = pl.BlockSpec((tm, D_MODEL), lambda i, j, k: (i, R_GA // D_MODEL))
    gb_spec = pl.BlockSpec((tm, D_MODEL), lambda i, j, k: (i, R_GB // D_MODEL))
    gates_spec = pl.BlockSpec((tm, 2 * D_MODEL), lambda i, j, k: (i, R_GA // (2 * D_MODEL)))
    dpa, dpb, dz_rest = _mm(
        "out_proj_bwd", dx1, w_out_t, "nn", tm, D_MODEL, D_MODEL,
        [(_sds((S, D_MODEL), BF16), row_tile), (_sds((S, D_MODEL), BF16), row_tile), (_sds((S, REST_W), BF16), gates_spec)],
        epi_dy, extras=[(z_rest, ga_spec), (z_rest, gb_spec), (pa, row_tile), (pb, row_tile)])
    gw_out = _mm("gw_out", y, dx1, "tn", D_MODEL, D_MODEL, tk,
                 [(_sds((D_MODEL, D_MODEL), BF16), _tile_ij(D_MODEL, D_MODEL))], _epi_store(BF16))[0]
    gw_pa = _mm("gw_proj_a", o_a, dpa, "tn", ATT_OUT_W, D_MODEL, tk,
                [(_sds((ATT_OUT_W, D_MODEL), BF16), _tile_ij(ATT_OUT_W, D_MODEL))], _epi_store(BF16))[0]
    gw_pb = _mm("gw_proj_b", o_r, dpb, "tn", 1024, D_MODEL, tk,
                [(_sds((RET_V_W, D_MODEL), BF16), _tile_ij(1024, D_MODEL))], _epi_store(BF16))[0]

    def epi_doa(acc, ex, out):
        out[0][...] = acc
        prod = acc * ex[0][...]
        out[1][...] = jnp.concatenate(
            [jnp.broadcast_to(jnp.sum(prod[:, j * DH:(j + 1) * DH], axis=-1, keepdims=True), (prod.shape[0], DH))
             for j in range(HPG)], axis=1)

    slot_tile = _tile_ij(tm, ATT_OUT_W)
    do_a, delta = _mm("proj_a_bwd", dpa, w_pa_t, "nn", tm, ATT_OUT_W, D_MODEL,
                      [(_sds((S, ATT_OUT_W), F32), slot_tile), (_sds((S, ATT_OUT_W), F32), slot_tile)],
                      epi_doa, extras=[(o_a, slot_tile)])
    owed = [_chip_core(_by_owner_cols(gw_pa)), _chip_core(gw_pb.reshape(N_DEV, -1, D_MODEL)),
            _chip_core(gw_out.reshape(N_DEV, -1, D_MODEL)), _chip_core(gw_up), _chip_core(gw_down.reshape(N_DEV, -1, D_MODEL))]
    names = ("w_proj_a", "w_proj_b", "w_out", "w_up", "w_down")
    (do_r,), got = _mm("proj_b_bwd", dpb, w_pb_t, "nn", tm, RET_V_W, D_MODEL,
                       [(_sds((S, RET_V_W), F32), _tile_ij(tm, RET_V_W))], _epi_store(F32), exchange=_pair_exchange(owed))
    chip_sums = [_pair_sum(f"pair_sum_{n}", g, r, core, min(256, g.shape[2])) for n, g, r in zip(names, owed, got)]

    (dz_rest, g_gn_g, g_gn_b), parts_late = _ret_bwd(do_r, o_pre, states, z_rest, dz_rest, gn_g, gn_b,
                                                     _chip_exchange(chip_sums))
    dz_att, gq_parts, gk_parts = None, [], []
    for gi, ((_, d), nb) in enumerate(zip(ATT_GROUPS, ATT_BLOCKS_PER_STEP)):
        dz_att, gq_p, gk_p = _att_bwd_group(z_att, dz_att, do_a, delta, lmix, gq3, gk3, slopes3, gi, d, nb)
        gq_parts.append(gq_p)
        gk_parts.append(gk_p)
    g_qn = jnp.concatenate(gq_parts, axis=0).reshape(1, ATT_HEADS, DH)
    g_kn = jnp.concatenate(gk_parts, axis=0).reshape(1, ATT_HEADS, DH)

    gw_att = _mm("gw_in_att", xn, dz_att, "tn", D_MODEL, ATT_W, tk,
                 [(_sds((D_MODEL, 3 * ATT_W), BF16), _tile_ij(D_MODEL, ATT_W))], _epi_store(BF16))[0]
    gw_rest = _mm("gw_in_rest", xn, dz_rest, "tn", D_MODEL, 1024, tk,
                  [(_sds((D_MODEL, REST_W), BF16), _tile_ij(D_MODEL, 1024))], _epi_store(BF16))[0]
    gw_att = gw_att.reshape(D_MODEL, ATT_HEADS, 3, DH).transpose(0, 2, 1, 3).reshape(D_MODEL, 3 * ATT_W)
    gw_in = jnp.concatenate([gw_att, gw_rest], axis=1)
    owed_in = _chip_core(_by_owner_cols(gw_in))
    (dxn_att,), (got_in,) = _mm("in_proj_att_bwd", dz_att, w_att_t, "nn", tm, D_MODEL, 3 * ATT_W,
                                [(_sds((S, D_MODEL), F32), row_tile)], _epi_store(F32), exchange=_pair_exchange([owed_in]))
    chip_sum_in = _pair_sum("pair_sum_w_in", owed_in, got_in, core, min(256, owed_in.shape[2]))

    def epi_norm1(acc, ex, out):
        dx, dg = _rms_bwd(acc + ex[0][...], ex[1][...], ex[3][...])
        out[0][...] = ex[2][...] + dx
        _acc_rows(out[1], dg, pl.program_id(0) == 0)

    short_tile = _tile_ij(ts, D_MODEL)
    (grad_x, g_norm1), parts_in = _mm(
        "in_proj_rest_bwd", dz_rest, w_rest_t, "nn", ts, D_MODEL, REST_W,
        [(_sds((S, D_MODEL), F32), short_tile), (_sds((1, D_MODEL), F32), vec)],
        epi_norm1, extras=[(dxn_att, short_tile), (x, short_tile), (dx1, short_tile), (norm1_g, vec)],
        exchange=_chip_exchange([chip_sum_in]))

    small = (g_norm1, g_qn, g_kn, g_gn_g, g_gn_b, g_norm2)
    return loss_local, grad_x, list(parts_in) + list(parts_late), small


def _position():
    return lax.axis_index("x"), lax.axis_index("y"), lax.axis_index("c")


def _other_chips(x, y):
    return [(1 - x, y), (x, 1 - y), (1 - x, 1 - y)]


_ANY = pl.BlockSpec(memory_space=pl.ANY)


def _gather_exchange(shards):
    nw = len(shards)

    def plan(x_refs, out_refs, sems):
        send_sems, recv_sems, local_sems = sems
        x, y, c = _position()
        me, sibling = (x, y, c), (x, y, 1 - c)
        chips = _other_chips(x, y)

        def copy(w, k, block, to, own=False):
            px, py, pc = block
            rows = out_refs[w].at[4 * px + 2 * py + pc]
            return pltpu.make_async_remote_copy(
                src_ref=x_refs[w] if own else rows, dst_ref=rows,
                send_sem=send_sems.at[7 * w + k], recv_sem=recv_sems.at[7 * w + k], device_id=to, device_id_type=MESH)

        def mine(w):
            return pltpu.make_async_copy(x_refs[w], out_refs[w].at[4 * x + 2 * y + c], local_sems.at[w])

        def own_sends(w):
            return [copy(w, 0, me, sibling, own=True)] + [copy(w, 1 + j, me, (*chip, c), own=True)
                                                          for j, chip in enumerate(chips)]

        def start():
            for w in range(nw):
                mine(w).start()
                for cp in own_sends(w):
                    cp.start()

        def relay():
            for j, chip in enumerate(chips):
                for w in range(nw):
                    copy(w, 1 + j, (*chip, c), me).wait_recv()
                    copy(w, 4 + j, (*chip, c), sibling).start()

        def finish():
            for w in range(nw):
                copy(w, 0, sibling, me).wait_recv()
                for j, chip in enumerate(chips):
                    copy(w, 4 + j, (*chip, 1 - c), me).wait_recv()
            for w in range(nw):
                for cp in own_sends(w):
                    cp.wait_send()
                for j, chip in enumerate(chips):
                    copy(w, 4 + j, (*chip, c), sibling).wait_send()
                mine(w).wait()

        return start, relay, finish

    return _Exchange(shards, [_sds((N_DEV,) + s.shape, s.dtype) for s in shards],
                     [pltpu.SemaphoreType.DMA((7 * nw,)), pltpu.SemaphoreType.DMA((7 * nw,)),
                      pltpu.SemaphoreType.DMA((nw,))], plan)


def _run_exchange(name, exchange):
    n_in, n_out = len(exchange.ins), len(exchange.out_shapes)

    def body(*refs):
        start, relay, finish = exchange.split(refs[:n_in], refs[n_in:n_in + n_out], refs[n_in + n_out:])
        start()
        if relay is not None:
            relay()
        finish()

    return pl.pallas_call(
        body, name=name, out_shape=exchange.out_shapes,
        in_specs=[_ANY] * n_in, out_specs=[_ANY] * n_out, scratch_shapes=exchange.sems,
    )(*exchange.ins)


def _pair_exchange(grads):
    ng = len(grads)

    def plan(g_refs, out_refs, sems):
        send_sems, recv_sems = sems
        x, y, c = _position()

        def copies():
            return [pltpu.make_async_remote_copy(
                src_ref=g_refs[w].at[:, 1 - c], dst_ref=out_refs[w], send_sem=send_sems.at[w],
                recv_sem=recv_sems.at[w], device_id=(x, y, 1 - c), device_id_type=MESH) for w in range(ng)]

        def start():
            for cp in copies():
                cp.start()

        def finish():
            for cp in copies():
                cp.wait()

        return start, None, finish

    return _Exchange(grads, [_sds((N_CHIP,) + g.shape[2:], g.dtype) for g in grads],
                     [pltpu.SemaphoreType.DMA((ng,)), pltpu.SemaphoreType.DMA((ng,))], plan)


def _small_all_gather(small):
    def plan(in_refs, out_refs, sems):
        (s_ref,), (s_out,) = in_refs, out_refs
        send_sems, recv_sems, local_sem = sems
        x, y, c = _position()
        me_id = 4 * x + 2 * y + c
        flips = [(a, b, e) for a in (0, 1) for b in (0, 1) for e in (0, 1)][1:]
        peers = [(x ^ a, y ^ b, c ^ e) for a, b, e in flips]

        def start():
            pltpu.make_async_copy(s_ref, s_out.at[me_id], local_sem).start()
            for k, p in enumerate(peers):
                pltpu.make_async_remote_copy(
                    src_ref=s_ref, dst_ref=s_out.at[me_id], send_sem=send_sems.at[k], recv_sem=recv_sems.at[k],
                    device_id=p, device_id_type=MESH).start()

        def finish():
            for k, (px, py, pc) in enumerate(peers):
                pltpu.make_async_remote_copy(
                    src_ref=s_ref, dst_ref=s_out.at[4 * px + 2 * py + pc], send_sem=send_sems.at[k],
                    recv_sem=recv_sems.at[k], device_id=(px, py, pc), device_id_type=MESH).wait()
            pltpu.make_async_copy(s_ref, s_out.at[me_id], local_sem).wait()

        return start, None, finish

    return _run_exchange("small_grad_all_gather", _Exchange(
        [small], [_sds((N_DEV,) + small.shape, small.dtype)],
        [pltpu.SemaphoreType.DMA((7,)), pltpu.SemaphoreType.DMA((7,)), pltpu.SemaphoreType.DMA], plan))[0]


def _pair_sum(name, g, got, core, tr):
    n_chip, _, R, C = g.shape

    def body(c_ref, a_ref, b_ref, o_ref):
        del c_ref
        o_ref[...] = (a_ref[...].astype(F32) + b_ref[...].astype(F32)).astype(o_ref.dtype)

    return pl.pallas_call(
        body, name=name,
        grid_spec=pltpu.PrefetchScalarGridSpec(
            num_scalar_prefetch=1, grid=(n_chip, R // tr),
            in_specs=[pl.BlockSpec((None, None, tr, C), lambda ch, i, c_ref: (ch, c_ref[0], i, 0)),
                      pl.BlockSpec((None, tr, C), lambda ch, i, c_ref: (ch, i, 0))],
            out_specs=pl.BlockSpec((None, tr, C), lambda ch, i, c_ref: (ch, i, 0))),
        out_shape=_sds(got.shape, got.dtype),
        compiler_params=_cparams(("parallel", "parallel")),
    )(core, g, got)


def _chip_exchange(parts):
    ng = len(parts)

    def plan(p_refs, out_refs, sems):
        send_sems, recv_sems, local_sems = sems
        x, y, c = _position()
        my_chip = 2 * x + y

        def copies():
            local = [pltpu.make_async_copy(p_refs[w].at[my_chip], out_refs[w].at[my_chip], local_sems.at[w])
                     for w in range(ng)]
            remote = [pltpu.make_async_remote_copy(
                src_ref=p_refs[w].at[2 * cx + cy], dst_ref=out_refs[w].at[my_chip],
                send_sem=send_sems.at[3 * w + k], recv_sem=recv_sems.at[3 * w + k],
                device_id=(cx, cy, c), device_id_type=MESH)
                for w in range(ng) for k, (cx, cy) in enumerate(_other_chips(x, y))]
            return local + remote

        def start():
            for cp in copies():
                cp.start()

        def finish():
            for cp in copies():
                cp.wait()

        return start, None, finish

    return _Exchange(parts, [_sds(p.shape, p.dtype) for p in parts],
                     [pltpu.SemaphoreType.DMA((3 * ng,)), pltpu.SemaphoreType.DMA((3 * ng,)),
                      pltpu.SemaphoreType.DMA((ng,))], plan)


def _adamw(name, parts, w, m, v, tr):
    n_parts = parts.shape[0]
    R, C = w.shape

    def body(p_ref, w_ref, m_ref, v_ref, g_ref, d_ref, mo_ref, vo_ref):
        g = p_ref[0].astype(F32)
        for i in range(1, n_parts):
            g = g + p_ref[i].astype(F32)
        m_new = ADAM_B1 * m_ref[...] + (1.0 - ADAM_B1) * g
        v_new = ADAM_B2 * v_ref[...] + (1.0 - ADAM_B2) * (g * g)
        m_hat = m_new / (1.0 - ADAM_B1 ** ADAM_STEP)
        v_hat = v_new / (1.0 - ADAM_B2 ** ADAM_STEP)
        g_ref[...] = g
        d_ref[...] = -ADAM_LR * (m_hat / (jnp.sqrt(v_hat) + ADAM_EPS) + ADAM_WD * w_ref[...])
        mo_ref[...] = m_new
        vo_ref[...] = v_new

    tile = pl.BlockSpec((tr, C), lambda i: (i, 0))
    return pl.pallas_call(
        body, name=name, grid=(R // tr,),
        in_specs=[pl.BlockSpec((n_parts, tr, C), lambda i: (0, i, 0)), tile, tile, tile],
        out_specs=[tile] * 4,
        out_shape=[_sds((R, C), F32)] * 4,
        compiler_params=_cparams(("parallel",)),
    )(parts, w, m, v)


def _flat_small(arrs):
    return jnp.concatenate([a.reshape(-1) for a in arrs]).reshape(SMALL_ROWS, LANES)


def _cols(g):
    return g.transpose(1, 0, 2).reshape(g.shape[1], -1)


def _rows_of(g):
    return g.reshape(-1, g.shape[2])


def _by_owner_cols(g):
    rows, cols = g.shape
    return g.reshape(rows, N_DEV, cols // N_DEV).transpose(1, 0, 2)


def _chip_core(g):
    return g.reshape((N_CHIP, 2) + g.shape[1:])


def kernel(x, norm1_g, w_in, q_norm_g, k_norm_g, ret_gn_g, ret_gn_b, w_proj_a, w_proj_b, w_out, norm2_g, w_up, w_down, loss_target, m_norm1_g, m_w_in, m_q_norm_g, m_k_norm_g, m_ret_gn_g, m_ret_gn_b, m_w_proj_a, m_w_proj_b, m_w_out, m_norm2_g, m_w_up, m_w_down, v_norm1_g, v_w_in, v_q_norm_g, v_k_norm_g, v_ret_gn_g, v_ret_gn_b, v_w_proj_a, v_w_proj_b, v_w_out, v_norm2_g, v_w_up, v_w_down):
    big_w = (w_in, w_proj_a, w_proj_b, w_out, w_up, w_down)
    big_m = (m_w_in, m_w_proj_a, m_w_proj_b, m_w_out, m_w_up, m_w_down)
    big_v = (v_w_in, v_w_proj_a, v_w_proj_b, v_w_out, v_w_up, v_w_down)
    small_w = (norm1_g, q_norm_g, k_norm_g, ret_gn_g, ret_gn_b, norm2_g)
    small_m = (m_norm1_g, m_q_norm_g, m_k_norm_g, m_ret_gn_g, m_ret_gn_b, m_norm2_g)
    small_v = (v_norm1_g, v_q_norm_g, v_k_norm_g, v_ret_gn_g, v_ret_gn_b, v_norm2_g)

    shards = [w[0].astype(BF16) for w in big_w]
    (g_in,) = _run_exchange("all_gather_w_in", _gather_exchange(shards[:1]))
    core = lax.axis_index("c").astype(jnp.int32).reshape(1)
    loss_local, grad_x, parts, small_g = _step(
        x[0], loss_target[0], norm1_g, q_norm_g[0], k_norm_g[0], ret_gn_g, ret_gn_b, norm2_g, _cols(g_in), shards[1:], core)
    small_all = _small_all_gather(_flat_small(small_g))
    names = ("w_in", "w_proj_a", "w_proj_b", "w_out", "w_up", "w_down")

    res = {}
    for n, p, w, m, v in zip(names, parts, big_w, big_m, big_v):
        outs = _adamw(f"adamw_{n}", p, w[0], m[0], v[0], min(128, w.shape[1]))
        res[n] = [o[None] for o in outs]
    s_outs = _adamw("adamw_small", small_all, _flat_small(small_w), _flat_small(small_m), _flat_small(small_v), SMALL_ROWS)
    small_names = ("norm1_g", "q_norm_g", "k_norm_g", "ret_gn_g", "ret_gn_b", "norm2_g")
    for n in small_names:
        res[n] = []
    for o in s_outs:
        flat, off = o.reshape(-1), 0
        for n, w in zip(small_names, small_w):
            res[n].append(flat[off:off + w.size].reshape(w.shape))
            off += w.size

    order = ("norm1_g", "w_in", "q_norm_g", "k_norm_g", "ret_gn_g", "ret_gn_b", "w_proj_a", "w_proj_b", "w_out",
             "norm2_g", "w_up", "w_down")
    loss = lax.psum(loss_local, MESH_AXES)
    return (loss, grad_x[None], *[res[n][0] for n in order], *[res[n][1] for n in order],
            *[res[n][2] for n in order], *[res[n][3] for n in order])
```

```python
import math

import numpy as np
import jax
import jax.numpy as jnp
from jax import lax
from jax.experimental import pallas as pl
from jax.experimental.pallas import tpu as pltpu

F32 = jnp.float32
BF16 = jnp.bfloat16

D_MODEL = 1024
ATT_GROUPS = ((128, 1), (512, 4), (2048, 16))
ATT_BLOCKS_PER_STEP = (8, 1, 1)
ATT_TOGETHER = 4
HPG = 4
ATT_HEADS = 12
DH = 128
BLK = 128
ATT_W = ATT_HEADS * DH
ATT_OUT_W = HPG * DH
RET_HEADS = 4
RET_QK = 256
RET_V = 512
RET_QK_W = RET_HEADS * RET_QK
RET_V_W = RET_HEADS * RET_V
CHUNK = 128
D_FF = 4096
IN_W = 12800
REST_W = IN_W - 3 * ATT_W
RET_COLS = 6144
EPS = 1e-6
ADAM_LR, ADAM_B1, ADAM_B2, ADAM_EPS, ADAM_WD, ADAM_STEP = 0.001, 0.9, 0.999, 1e-08, 0.01, 10
N_DEV = 8
N_CHIP = 4
MESH_AXES = ("x", "y", "c")
MESH = pl.DeviceIdType.MESH
VMEM_LIMIT = 56 * 1024 * 1024
LANES = 128
NEG = -1e30

_NN = (((1,), (0,)), ((), ()))
_NT = (((1,), (1,)), ((), ()))
_TN = (((0,), (0,)), ((), ()))

R_Q, R_K, R_V, R_G, R_GA, R_GB = 0, 1024, 2048, 4096, 6144, 7168

LOG_GAMMA = [float(v) for v in np.log(1.0 - 2.0 ** (-5.0 - np.arange(RET_HEADS, dtype=np.float32))).astype(np.float32)]
ALIBI = np.asarray(2.0 ** (-8.0 * np.arange(1, ATT_HEADS + 1, dtype=np.float32) / ATT_HEADS), np.float32)

SMALL_ROWS = (1024 + 1536 + 1536 + 2048 + 2048 + 1024) // LANES


def _dot(a, b, dims=_NN):
    return lax.dot_general(a, b, dims, preferred_element_type=F32)


def _cparams(sem):
    return pltpu.CompilerParams(dimension_semantics=sem, vmem_limit_bytes=VMEM_LIMIT)


def _sds(shape, dtype):
    return jax.ShapeDtypeStruct(shape, dtype)


class _Exchange:
    def __init__(self, ins, out_shapes, sems, plan):
        self.ins, self.out_shapes, self.sems, self.plan = list(ins), list(out_shapes), list(sems), plan

    def split(self, in_refs, out_refs, sem_refs):
        return self.plan(in_refs, out_refs, sem_refs)


def _mm(name, a, b, mode, tm, tn, tk, outs, epi, extras=(), b_pro=None,
        sem=("parallel", "parallel", "arbitrary"), exchange=None):
    if mode == "nn":
        (M, K), (_, N) = a.shape, b.shape
        a_spec = pl.BlockSpec((tm, tk), lambda i, j, k: (i, k))
        dims = _NN
    else:
        (K, M), (_, N) = a.shape, b.shape
        a_spec = pl.BlockSpec((tk, tm), lambda i, j, k: (k, i))
        dims = _TN
    assert M % tm == 0 and N % tn == 0 and K % tk == 0, (name, M, N, K, tm, tn, tk)
    nk = K // tk
    whole_b = dict(pipeline_mode=pl.Buffered(1)) if (nk == 1 and N == tn) else {}
    b_spec = pl.BlockSpec((tk, tn), lambda i, j, k: (k, j), **whole_b)
    n_ex, n_out = len(extras), len(outs)
    grid = (M // tm, N // tn, nk)
    n_xi = len(exchange.ins) if exchange else 0
    n_xo = len(exchange.out_shapes) if exchange else 0
    n_acc = 1 if nk > 1 else 0

    def body(a_ref, b_ref, *rest):
        ex, rest = rest[:n_ex], rest[n_ex:]
        x_in, rest = rest[:n_xi], rest[n_xi:]
        out, rest = rest[:n_out], rest[n_out:]
        x_out, rest = rest[:n_xo], rest[n_xo:]
        step = (pl.program_id(0) * grid[1] + pl.program_id(1)) * grid[2] + pl.program_id(2)
        n_steps = grid[0] * grid[1] * grid[2]
        if exchange:
            start, relay, finish = exchange.split(x_in, x_out, rest[n_acc:])
            pl.when(step == 0)(start)
        bv = b_ref[...]
        if b_pro is not None:
            bv = b_pro(bv)
        part = _dot(a_ref[...].astype(BF16), bv.astype(BF16), dims)
        if nk == 1:
            epi(part, ex, out)
        else:
            acc_ref = rest[0]
            k = pl.program_id(2)

            @pl.when(k == 0)
            def _():
                acc_ref[...] = part

            @pl.when(k > 0)
            def _():
                acc_ref[...] += part

            @pl.when(k == nk - 1)
            def _():
                epi(acc_ref[...], ex, out)
        if exchange:
            if relay is not None:
                pl.when(step == (7 * n_steps) // 8)(relay)
            pl.when(step == n_steps - 1)(finish)

    res = pl.pallas_call(
        body,
        name=name,
        grid=grid,
        in_specs=[a_spec, b_spec] + [s for _, s in extras] + [_ANY] * n_xi,
        out_specs=[s for _, s in outs] + [_ANY] * n_xo,
        out_shape=[o for o, _ in outs] + (exchange.out_shapes if exchange else []),
        scratch_shapes=([pltpu.VMEM((tm, tn), F32)] if nk > 1 else []) + (exchange.sems if exchange else []),
        compiler_params=_cparams(("arbitrary",) * 3 if exchange else sem),
    )(a, b, *[e for e, _ in extras], *(exchange.ins if exchange else []))
    return (res[:n_out], res[n_out:]) if exchange else res


def _tile_ij(tm, tn):
    return pl.BlockSpec((tm, tn), lambda i, j, k: (i, j))


def _epi_store(dtype):
    def epi(acc, ex, out):
        out[0][...] = acc.astype(dtype)
    return epi


def _rms_rows(x):
    return lax.rsqrt(jnp.mean(x * x, axis=-1, keepdims=True) + EPS)


def _acc_rows(ref, part, first):
    @pl.when(first)
    def _():
        ref[...] = part

    @pl.when(jnp.logical_not(first))
    def _():
        ref[...] += part


def _rmsnorm_fwd(x, g, tm):
    S, Dm = x.shape

    def body(x_ref, g_ref, o_ref):
        xv = x_ref[...]
        o_ref[...] = (xv * _rms_rows(xv) * g_ref[...]).astype(BF16)

    return pl.pallas_call(
        body, name="rmsnorm1_fwd", grid=(S // tm,),
        in_specs=[pl.BlockSpec((tm, Dm), lambda i: (i, 0)), pl.BlockSpec((1, Dm), lambda i: (0, 0))],
        out_specs=pl.BlockSpec((tm, Dm), lambda i: (i, 0)),
        out_shape=_sds((S, Dm), BF16),
        compiler_params=_cparams(("parallel",)),
    )(x, g)


def _rows(ref, r, b, d):
    if d == 1:
        return ref[b * BLK:(b + 1) * BLK, :]
    return ref[pl.ds(b * BLK * d + r, BLK, stride=d), :]


def _put_rows(ref, r, b, d, val):
    if d == 1:
        ref[b * BLK:(b + 1) * BLK, :] = val
    else:
        ref[pl.ds(b * BLK * d + r, BLK, stride=d), :] = val


def _head_norm(x, g):
    r = _rms_rows(x)
    xh = x * r
    return (xh * g).astype(BF16), xh, r


def _head_norm_bwd(dyn, xh, r, g):
    dxh = dyn * g
    dx = r * (dxh - xh * jnp.mean(dxh * xh, axis=-1, keepdims=True))
    return dx, jnp.sum(dyn * xh, axis=0, keepdims=True)


def _att_mask_bias(slope, d, first):
    qi = lax.broadcasted_iota(jnp.int32, (BLK, 2 * BLK), 0)
    kj = lax.broadcasted_iota(jnp.int32, (BLK, 2 * BLK), 1)
    dist = BLK + qi - kj
    valid = (dist >= 0) & (dist <= BLK)
    if first is not None:
        valid = valid & (jnp.logical_not(first) | (kj >= BLK))
    bias = -slope * (dist * d).astype(F32)
    return valid, bias


def _att_specs(gi, d, nb, S):
    span = BLK * d
    sb = span * nb
    nspan = S // span
    before = lambda n: jnp.maximum(n * nb - 1, 0)
    after = lambda n: jnp.minimum((n + 1) * nb, nspan - 1)
    zcol = lambda j, kind: 3 * (gi * HPG + j) + kind
    cur = lambda kind: pl.BlockSpec((sb, DH), lambda j, n: (n, zcol(j, kind)))
    prev = lambda kind: pl.BlockSpec((span, DH), lambda j, n: (before(n), zcol(j, kind)))
    nxt = lambda kind: pl.BlockSpec((span, DH), lambda j, n: (after(n), zcol(j, kind)))
    slot = pl.BlockSpec((sb, DH), lambda j, n: (n, j))
    slot_next = pl.BlockSpec((span, DH), lambda j, n: (after(n), j))
    head = pl.BlockSpec((None, 1, DH), lambda j, n: (gi * HPG + j, 0, 0))
    return cur, prev, nxt, slot, slot_next, head


def _att_fwd_group(z_att, gq3, gk3, slopes3, gi, d, nb, others=()):
    S = z_att.shape[0]
    nsb = S // (BLK * d * nb)
    scale = DH ** -0.5
    n_other = len(others)

    def body(q_ref, k_ref, v_ref, kp_ref, vp_ref, gq_ref, gk_ref, sl_ref, *rest):
        other_refs, (o_ref, l_ref) = rest[:2 * n_other], rest[2 * n_other:]
        slope = sl_ref[...][:, :1]
        valid0, bias = _att_mask_bias(slope, d, pl.program_id(1) == 0)
        valid_in, _ = _att_mask_bias(slope, d, None)
        gq, gk = gq_ref[...], gk_ref[...]
        memo = {}

        def get(kind, r, b):
            if (kind, r, b) not in memo:
                if kind == "k":
                    val = _head_norm(_rows(k_ref if b >= 0 else kp_ref, r, max(b, 0), d), gk)[0]
                else:
                    val = _rows(v_ref if b >= 0 else vp_ref, r, max(b, 0), d).astype(BF16)
                memo[(kind, r, b)] = val
            return memo[(kind, r, b)]

        units = [(r, b) for r in range(d) for b in range(nb)]
        for c0 in range(0, len(units), ATT_TOGETHER):
            us = units[c0:c0 + ATT_TOGETHER]
            q = [_head_norm(_rows(q_ref, r, b, d), gq)[0] for r, b in us]
            k2 = [jnp.concatenate([get("k", r, b - 1), get("k", r, b)], axis=0) for r, b in us]
            v2 = [jnp.concatenate([get("v", r, b - 1), get("v", r, b)], axis=0) for r, b in us]
            s = [jnp.where(valid0 if b == 0 else valid_in, _dot(q[i], k2[i], _NT) * scale + bias, NEG)
                 for i, (r, b) in enumerate(us)]
            m = [jnp.max(si, axis=-1, keepdims=True) for si in s]
            p = [jnp.exp(si - mi) for si, mi in zip(s, m)]
            den = [jnp.sum(pi, axis=-1, keepdims=True) for pi in p]
            o = [_dot(pi.astype(BF16), vi) / di for pi, vi, di in zip(p, v2, den)]
            for i, (r, b) in enumerate(us):
                _put_rows(o_ref, r, b, d, o[i])
                _put_rows(l_ref, r, b, d, jnp.broadcast_to(m[i] + jnp.log(den[i]), (BLK, DH)))
        if n_other:
            os_ = [ref[...] for ref in other_refs[:n_other]] + [o_ref[...]]
            ls_ = [ref[...] for ref in other_refs[n_other:]] + [l_ref[...]]
            m = ls_[0]
            for l in ls_[1:]:
                m = jnp.maximum(m, l)
            es = [jnp.exp(l - m) for l in ls_]
            tot, mix = es[0], es[0] * os_[0]
            for e, o in zip(es[1:], os_[1:]):
                tot, mix = tot + e, mix + e * o
            o_ref[...] = mix / tot
            l_ref[...] = m + jnp.log(tot)

    cur, prev, _, slot, _, head = _att_specs(gi, d, nb, S)
    return pl.pallas_call(
        body, name=f"att_fwd_g{gi}", grid=(HPG, nsb),
        in_specs=[cur(0), cur(1), cur(2), prev(1), prev(2), head, head, head] + [slot] * (2 * n_other),
        out_specs=[slot, slot],
        out_shape=[_sds((S, ATT_OUT_W), F32), _sds((S, ATT_OUT_W), F32)],
        compiler_params=_cparams(("parallel", "arbitrary")),
    )(z_att, z_att, z_att, z_att, z_att, gq3, gk3, slopes3, *[o for o, _ in others], *[l for _, l in others])


def _ret_tables(lg):
    ri = lax.broadcasted_iota(jnp.int32, (CHUNK, CHUNK), 0)
    ci = lax.broadcasted_iota(jnp.int32, (CHUNK, CHUNK), 1)
    diff = (ri - ci).astype(F32)
    decay = jnp.where(diff >= 0, jnp.exp(lg * jnp.maximum(diff, 0.0)), 0.0)
    idx = lax.broadcasted_iota(jnp.int32, (CHUNK, 1), 0).astype(F32)
    xi = jnp.exp(lg * (idx + 1.0))
    zeta = jnp.exp(lg * (CHUNK - 1.0 - idx))
    return decay, xi, zeta, math.exp(lg * CHUNK)


def _ret_specs(nch, rev):
    idx = (lambda n: nch - 1 - n) if rev else (lambda n: n)
    qk = lambda off: pl.BlockSpec((CHUNK, RET_QK_W), lambda n: (idx(n), off // RET_QK_W))
    vv = lambda off: pl.BlockSpec((CHUNK, RET_V_W), lambda n: (idx(n), off // RET_V_W))
    par = pl.BlockSpec((1, RET_V_W), lambda n: (0, 0))
    wide = pl.BlockSpec((CHUNK, RET_V_W), lambda n: (idx(n), 0))
    st = pl.BlockSpec((RET_HEADS, None, RET_QK, RET_V), lambda n: (0, idx(n), 0, 0))
    return qk, vv, par, wide, st


def _ret_fwd(z_rest, gn_g, gn_b):
    S = z_rest.shape[0]
    nch = S // CHUNK

    def body(q_ref, k_ref, v_ref, gr_ref, g_ref, b_ref, or_ref, o_ref, st_ref, state):
        @pl.when(pl.program_id(0) == 0)
        def _():
            state[...] = jnp.zeros_like(state)

        for h in range(RET_HEADS):
            decay, xi, zeta, gch = _ret_tables(LOG_GAMMA[h])
            cq = slice(h * RET_QK, (h + 1) * RET_QK)
            cv = slice(h * RET_V, (h + 1) * RET_V)
            q = q_ref[:, cq]
            kc32 = k_ref[:, cq].astype(F32) * (RET_QK ** -0.5)
            kc = kc32.astype(BF16)
            v = v_ref[:, cv]
            st = state[h]
            stb = st.astype(BF16)
            st_ref[h] = stb
            s = _dot(q, kc, _NT) * decay
            o = _dot(s.astype(BF16), v) + _dot(q, stb) * xi
            state[h] = st * gch + _dot((kc32 * zeta).astype(BF16), v, _TN)
            mu = jnp.mean(o, axis=-1, keepdims=True)
            cen = o - mu
            yh = cen * lax.rsqrt(jnp.mean(cen * cen, axis=-1, keepdims=True) + EPS)
            gr = gr_ref[:, cv].astype(F32)
            or_ref[:, cv] = ((yh * g_ref[:, cv] + b_ref[:, cv]) * (gr * jax.nn.sigmoid(gr))).astype(BF16)
            o_ref[:, cv] = o

    qk, vv, par, wide, st = _ret_specs(nch, False)
    return pl.pallas_call(
        body, name="ret_fwd", grid=(nch,),
        in_specs=[qk(R_Q), qk(R_K), vv(R_V), vv(R_G), par, par],
        out_specs=[wide, wide, st],
        out_shape=[_sds((S, RET_V_W), BF16), _sds((S, RET_V_W), F32), _sds((RET_HEADS, nch, RET_QK, RET_V), BF16)],
        scratch_shapes=[pltpu.VMEM((RET_HEADS, RET_QK, RET_V), F32)],
        compiler_params=_cparams(("arbitrary",)),
    )(z_rest, z_rest, z_rest, z_rest, gn_g, gn_b)


def _merge_fwd(o_a, o_r, z_rest, x, wpa, wpb, wout, g2, tm):
    S = x.shape[0]

    def body(oa_ref, or_ref, ga_ref, gb_ref, x_ref, wpa_ref, wpb_ref, wo_ref, g2_ref,
             x1_ref, y_ref, pa_ref, pb_ref, xn2_ref):
        pa = _dot(oa_ref[...].astype(BF16), wpa_ref[...])
        pb = _dot(or_ref[...], wpb_ref[...])
        y = jax.nn.sigmoid(ga_ref[...].astype(F32)) * pa + jax.nn.sigmoid(gb_ref[...].astype(F32)) * pb
        yb = y.astype(BF16)
        x1 = x_ref[...] + _dot(yb, wo_ref[...])
        x1_ref[...] = x1
        y_ref[...] = yb
        pa_ref[...] = pa.astype(BF16)
        pb_ref[...] = pb.astype(BF16)
        xn2_ref[...] = (x1 * _rms_rows(x1) * g2_ref[...]).astype(BF16)

    row = lambda w: pl.BlockSpec((tm, w), lambda i: (i, 0))
    full = lambda a: pl.BlockSpec(a.shape, lambda i: (0, 0))
    return pl.pallas_call(
        body, name="merge_fwd", grid=(S // tm,),
        in_specs=[row(ATT_OUT_W), row(RET_V_W),
                  pl.BlockSpec((tm, D_MODEL), lambda i: (i, R_GA // D_MODEL)),
                  pl.BlockSpec((tm, D_MODEL), lambda i: (i, R_GB // D_MODEL)),
                  row(D_MODEL), full(wpa), full(wpb), full(wout), full(g2)],
        out_specs=[row(D_MODEL)] * 5,
        out_shape=[_sds((S, D_MODEL), F32)] + [_sds((S, D_MODEL), BF16)] * 4,
        compiler_params=_cparams(("parallel",)),
    )(o_a, o_r, z_rest, z_rest, x, wpa, wpb, wout, g2)


def _rms_bwd(dy, xv, g):
    r = _rms_rows(xv)
    xh = xv * r
    dg = dy * g
    dx = r * (dg - xh * jnp.mean(dg * xh, axis=-1, keepdims=True))
    return dx, jnp.sum(dy * xh, axis=0, keepdims=True)


def _ret_bwd(do_r, o_pre, states, z_rest, dz_rest, gn_g, gn_b, exchange):
    S = z_rest.shape[0]
    nch = S // CHUNK
    n_xi, n_xo = len(exchange.ins), len(exchange.out_shapes)

    def body(do_ref, o_ref, st_ref, q_ref, k_ref, v_ref, gr_ref, g_ref, b_ref, dz_in, *rest):
        del dz_in
        x_in, (dz_ref, dg_ref, db_ref), rest = rest[:n_xi], rest[n_xi:n_xi + 3], rest[n_xi + 3:]
        x_out, gst, x_sems = rest[:n_xo], rest[n_xo], rest[n_xo + 1:]
        start, _, finish = exchange.split(x_in, x_out, x_sems)
        first = pl.program_id(0) == 0
        pl.when(first)(start)

        @pl.when(first)
        def _():
            gst[...] = jnp.zeros_like(gst)

        dgs, dbs = [], []
        for h in range(RET_HEADS):
            decay, xi, zeta, gch = _ret_tables(LOG_GAMMA[h])
            cq = slice(h * RET_QK, (h + 1) * RET_QK)
            cv = slice(h * RET_V, (h + 1) * RET_V)
            o = o_ref[:, cv]
            mu = jnp.mean(o, axis=-1, keepdims=True)
            cen = o - mu
            rstd = lax.rsqrt(jnp.mean(cen * cen, axis=-1, keepdims=True) + EPS)
            yh = cen * rstd
            gam = g_ref[:, cv]
            y = yh * gam + b_ref[:, cv]
            gr = gr_ref[:, cv].astype(F32)
            sg = jax.nn.sigmoid(gr)
            dout = do_ref[:, cv]
            dy = dout * (gr * sg)
            dz_ref[:, R_G + h * RET_V:R_G + (h + 1) * RET_V] = (dout * y * (sg * (1.0 + gr * (1.0 - sg)))).astype(BF16)
            dgs.append(jnp.sum(dy * yh, axis=0, keepdims=True))
            dbs.append(jnp.sum(dy, axis=0, keepdims=True))
            dyh = dy * gam
            do = rstd * (dyh - jnp.mean(dyh, axis=-1, keepdims=True)
                         - yh * jnp.mean(dyh * yh, axis=-1, keepdims=True))

            q = q_ref[:, cq]
            kc32 = k_ref[:, cq].astype(F32) * (RET_QK ** -0.5)
            kc = kc32.astype(BF16)
            v = v_ref[:, cv]
            dob = do.astype(BF16)
            a = (_dot(q, kc, _NT) * decay).astype(BF16)
            da = (_dot(dob, v, _NT) * decay).astype(BF16)
            dcross = (do * xi).astype(BF16)
            g_next = gst[h]
            gb = g_next.astype(BF16)
            dq = _dot(da, kc) + _dot(dcross, st_ref[h], _NT)
            dkc = _dot(da, q, _TN)
            dkz = _dot(v, gb, _NT)
            dv = _dot(a, dob, _TN) + _dot((kc32 * zeta).astype(BF16), gb)
            gst[h] = g_next * gch + _dot(q, dcross, _TN)
            dz_ref[:, R_Q + h * RET_QK:R_Q + (h + 1) * RET_QK] = dq.astype(BF16)
            dz_ref[:, R_K + h * RET_QK:R_K + (h + 1) * RET_QK] = ((dkc + dkz * zeta) * (RET_QK ** -0.5)).astype(BF16)
            dz_ref[:, R_V + h * RET_V:R_V + (h + 1) * RET_V] = dv.astype(BF16)
        _acc_rows(dg_ref, jnp.concatenate(dgs, axis=1), first)
        _acc_rows(db_ref, jnp.concatenate(dbs, axis=1), first)
        pl.when(pl.program_id(0) == nch - 1)(finish)

    qk, vv, par, wide, st = _ret_specs(nch, True)
    res = pl.pallas_call(
        body, name="ret_bwd", grid=(nch,),
        in_specs=[wide, wide, st, qk(R_Q), qk(R_K), vv(R_V), vv(R_G), par, par, _ANY] + [_ANY] * n_xi,
        out_specs=[pl.BlockSpec((CHUNK, RET_COLS), lambda n: (nch - 1 - n, 0)), par, par] + [_ANY] * n_xo,
        out_shape=[_sds(dz_rest.shape, BF16), _sds((1, RET_V_W), F32), _sds((1, RET_V_W), F32)] + exchange.out_shapes,
        input_output_aliases={9: 0},
        scratch_shapes=[pltpu.VMEM((RET_HEADS, RET_QK, RET_V), F32)] + exchange.sems,
        compiler_params=_cparams(("arbitrary",)),
    )(do_r, o_pre, states, z_rest, z_rest, z_rest, z_rest, gn_g, gn_b, dz_rest, *exchange.ins)
    return res[:3], res[3:]


def _att_probs(q, k, lmix, valid, bias):
    s = _dot(q, k, _NT) * (DH ** -0.5) + bias
    return jnp.where(valid, jnp.exp(jnp.where(valid, s, NEG) - lmix), 0.0)


def _att_bwd_group(z_att, dz_att, do_a, delta, lmix, gq3, gk3, slopes3, gi, d, nb):
    S = z_att.shape[0]
    sb = BLK * d * nb
    nsb = S // sb
    scale = DH ** -0.5
    aliased = dz_att is not None

    def body(q_ref, k_ref, v_ref, kp_ref, vp_ref, qn_ref, do_ref, don_ref, dl_ref, dln_ref, lm_ref, lmn_ref,
             gq_ref, gk_ref, sl_ref, *rest):
        dz_ref, dgq_ref, dgk_ref, stage = rest[-4:]
        n = pl.program_id(1)
        slope = sl_ref[...][:, :1]
        valid0, bias = _att_mask_bias(slope, d, n == 0)
        valid_in, _ = _att_mask_bias(slope, d, None)
        qi = lax.broadcasted_iota(jnp.int32, (BLK, BLK), 0)
        kj = lax.broadcasted_iota(jnp.int32, (BLK, BLK), 1)
        dist_n = BLK + qi - kj
        valid_n_in = dist_n <= BLK
        valid_n_last = valid_n_in & (n < nsb - 1)
        bias_n = -slope * (dist_n * d).astype(F32)
        gq, gk = gq_ref[...], gk_ref[...]
        dgq = jnp.zeros((1, DH), F32)
        dgk = jnp.zeros((1, DH), F32)
        memo = {}

        def get(kind, r, b):
            if (kind, r, b) not in memo:
                inner = 0 <= b < nb
                bb = b if inner else 0
                if kind == "q":
                    val = _head_norm(_rows(q_ref if inner else qn_ref, r, bb, d), gq)
                elif kind == "k":
                    val = _head_norm(_rows(k_ref if inner else kp_ref, r, bb, d), gk)
                elif kind == "v":
                    val = _rows(v_ref if inner else vp_ref, r, bb, d).astype(BF16)
                elif kind == "do":
                    val = _rows(do_ref if inner else don_ref, r, bb, d).astype(BF16)
                elif kind == "dl":
                    val = _rows(dl_ref if inner else dln_ref, r, bb, d)[:, :1]
                else:
                    val = _rows(lm_ref if inner else lmn_ref, r, bb, d)[:, :1]
                memo[(kind, r, b)] = val
            return memo[(kind, r, b)]

        units = [(r, b) for r in range(d) for b in range(nb)]
        for c0 in range(0, len(units), ATT_TOGETHER):
            us = units[c0:c0 + ATT_TOGETHER]
            k2 = [jnp.concatenate([get("k", r, b - 1)[0], get("k", r, b)[0]], axis=0) for r, b in us]
            v2 = [jnp.concatenate([get("v", r, b - 1), get("v", r, b)], axis=0) for r, b in us]
            p = [_att_probs(get("q", r, b)[0], k2[i], get("lm", r, b), valid0 if b == 0 else valid_in, bias)
                 for i, (r, b) in enumerate(us)]
            dp = [_dot(get("do", r, b), v2[i], _NT) for i, (r, b) in enumerate(us)]
            ds = [(p[i] * (dp[i] - get("dl", r, b)) * scale).astype(BF16) for i, (r, b) in enumerate(us)]
            dq = [_dot(ds[i], k2[i]) for i in range(len(us))]
            p_n = [_att_probs(get("q", r, b + 1)[0], get("k", r, b)[0], get("lm", r, b + 1),
                              valid_n_last if b == nb - 1 else valid_n_in, bias_n) for r, b in us]
            dp_n = [_dot(get("do", r, b + 1), get("v", r, b), _NT) for r, b in us]
            ds_n = [(p_n[i] * (dp_n[i] - get("dl", r, b + 1)) * scale).astype(BF16) for i, (r, b) in enumerate(us)]
            dk = [_dot(ds[i][:, BLK:], get("q", r, b)[0], _TN) + _dot(ds_n[i], get("q", r, b + 1)[0], _TN)
                  for i, (r, b) in enumerate(us)]
            dv = [_dot(p[i][:, BLK:].astype(BF16), get("do", r, b), _TN)
                  + _dot(p_n[i].astype(BF16), get("do", r, b + 1), _TN) for i, (r, b) in enumerate(us)]
            for i, (r, b) in enumerate(us):
                _, qh, qr = get("q", r, b)
                _, kh, kr = get("k", r, b)
                dxq, dg_q = _head_norm_bwd(dq[i], qh, qr, gq)
                dxk, dg_k = _head_norm_bwd(dk[i], kh, kr, gk)
                _put_rows(stage.at[0], r, b, d, dxq)
                _put_rows(stage.at[1], r, b, d, dxk)
                _put_rows(stage.at[2], r, b, d, dv[i])
                dgq, dgk = dgq + dg_q, dgk + dg_k
        for kind in range(3):
            dz_ref[:, kind * DH:(kind + 1) * DH] = stage[kind].astype(BF16)
        _acc_rows(dgq_ref, dgq, n == 0)
        _acc_rows(dgk_ref, dgk, n == 0)

    cur, prev, nxt, slot, slot_next, head = _att_specs(gi, d, nb, S)
    gain = pl.BlockSpec((None, 1, DH), lambda j, n: (j, 0, 0))
    in_specs = [cur(0), cur(1), cur(2), prev(1), prev(2), nxt(0),
                slot, slot_next, slot, slot_next, slot, slot_next, head, head, head]
    args = [z_att] * 6 + [do_a, do_a, delta, delta, lmix, lmix, gq3, gk3, slopes3]
    if aliased:
        in_specs.append(pl.BlockSpec(memory_space=pl.ANY))
        args.append(dz_att)
    return pl.pallas_call(
        body, name=f"att_bwd_g{gi}", grid=(HPG, nsb),
        in_specs=in_specs,
        out_specs=[pl.BlockSpec((sb, 3 * DH), lambda j, n: (n, gi * HPG + j)), gain, gain],
        out_shape=[_sds(z_att.shape, BF16), _sds((HPG, 1, DH), F32), _sds((HPG, 1, DH), F32)],
        input_output_aliases={len(args) - 1: 0} if aliased else {},
        scratch_shapes=[pltpu.VMEM((3, sb, DH), F32)],
        compiler_params=_cparams(("parallel", "arbitrary")),
    )(*args)


def _step(x, target, norm1_g, q_norm_g, k_norm_g, gn_g, gn_b, norm2_g, w_in, later_shards, core):
    S = x.shape[0]
    tm = min(512, S)
    ts = min(256, S)
    tk = min(2048, S)
    w_att = w_in[:, :3 * ATT_W].reshape(D_MODEL, 3, ATT_HEADS, DH).transpose(0, 2, 1, 3).reshape(D_MODEL, 3 * ATT_W)
    w_rest = w_in[:, 3 * ATT_W:]
    w_att_t, w_rest_t = w_att.T, w_rest.T
    gq3 = q_norm_g.reshape(ATT_HEADS, 1, DH)
    gk3 = k_norm_g.reshape(ATT_HEADS, 1, DH)
    slopes3 = jnp.asarray(np.broadcast_to(ALIBI[:, None, None], (ATT_HEADS, 1, DH)).copy())

    xn = _rmsnorm_fwd(x, norm1_g, tm)
    z_att = _mm("in_proj_att", xn, w_att, "nn", ts, 3 * ATT_W, D_MODEL,
                [(_sds((S, 3 * ATT_W), F32), _tile_ij(ts, 3 * ATT_W))], _epi_store(F32))[0]
    (z_rest,), gathered = _mm("in_proj_rest", xn, w_rest, "nn", ts, REST_W, D_MODEL,
                              [(_sds((S, REST_W), BF16), _tile_ij(ts, REST_W))], _epi_store(BF16),
                              exchange=_gather_exchange(later_shards))
    w_pa, w_pb, w_out, w_up, w_down = [f(g) for f, g in zip((_cols, _rows_of, _rows_of, _cols, _rows_of), gathered)]
    w_up_t, w_down_t, w_out_t, w_pa_t, w_pb_t = w_up.T, w_down.T, w_out.T, w_pa.T, w_pb.T
    done = []
    for gi, ((_, d), nb) in enumerate(zip(ATT_GROUPS, ATT_BLOCKS_PER_STEP)):
        last = gi == len(ATT_GROUPS) - 1
        done.append(_att_fwd_group(z_att, gq3, gk3, slopes3, gi, d, nb, others=tuple(done) if last else ()))
    o_a, lmix = done[-1]
    o_r, o_pre, states = _ret_fwd(z_rest, gn_g, gn_b)
    x1, y, pa, pb, xn2 = _merge_fwd(o_a, o_r, z_rest, x, w_pa, w_pb, w_out, norm2_g, min(256, S))

    def epi_up(acc, ex, out):
        r = jnp.maximum(acc, 0.0)
        out[0][...] = (r * r).astype(BF16)
        out[1][...] = r.astype(BF16)

    h, relu_u = _mm("mlp_up", xn2, w_up, "nn", ts, D_FF, D_MODEL,
                    [(_sds((S, D_FF), BF16), _tile_ij(ts, D_FF)), (_sds((S, D_FF), BF16), _tile_ij(ts, D_FF))], epi_up)

    def epi_down(acc, ex, out):
        diff = ex[0][...] + acc - ex[1][...]
        out[0][...] = diff * (1.0 / D_MODEL)
        out[1][...] = jnp.broadcast_to(jnp.sum(diff * diff) * (1.0 / (8 * LANES)), (8, LANES))

    row_tile = _tile_ij(tm, D_MODEL)
    dx2, loss_parts = _mm(
        "mlp_down_loss", h, w_down, "nn", tm, D_MODEL, D_FF,
        [(_sds((S, D_MODEL), F32), row_tile),
         (_sds((S // tm * 8, LANES), F32), pl.BlockSpec((8, LANES), lambda i, j, k: (i, 0)))],
        epi_down, extras=[(x1, row_tile), (target, row_tile)])
    loss_local = jnp.sum(loss_parts) * (0.5 / D_MODEL)

    def epi_du(acc, ex, out):
        out[0][...] = (acc * (2.0 * ex[0][...].astype(F32))).astype(BF16)

    du = _mm("mlp_down_bwd", dx2, w_down_t, "nn", ts, D_FF, D_MODEL,
             [(_sds((S, D_FF), BF16), _tile_ij(ts, D_FF))], epi_du, extras=[(relu_u, _tile_ij(ts, D_FF))])[0]
    gw_down = _mm("gw_down", h, dx2, "tn", 1024, D_MODEL, tk,
                  [(_sds((D_FF, D_MODEL), BF16), _tile_ij(1024, D_MODEL))], _epi_store(BF16))[0]
    gw_up = _mm("gw_up", xn2, du, "tn", D_MODEL, 512, tk,
                [(_sds((N_DEV, D_MODEL, 512), BF16), pl.BlockSpec((None, D_MODEL, 512), lambda i, j, k: (j, 0, 0)))],
                _epi_store(BF16))[0]

    vec = pl.BlockSpec((1, D_MODEL), lambda i, j, k: (0, 0))
    seq_sem = ("arbitrary", "arbitrary", "arbitrary")

    def epi_norm2(acc, ex, out):
        dx, dg = _rms_bwd(acc, ex[0][...], ex[2][...])
        out[0][...] = ex[1][...] + dx
        _acc_rows(out[1], dg, pl.program_id(0) == 0)

    dx1, g_norm2 = _mm(
        "mlp_up_bwd", du, w_up_t, "nn", tm, D_MODEL, D_FF,
        [(_sds((S, D_MODEL), F32), row_tile), (_sds((1, D_MODEL), F32), vec)],
        epi_norm2, extras=[(x1, row_tile), (dx2, row_tile), (norm2_g, vec)], sem=seq_sem)

    def epi_dy(acc, ex, out):
        sa = jax.nn.sigmoid(ex[0][...].astype(F32))
        sb = jax.nn.sigmoid(ex[1][...].astype(F32))
        out[0][...] = (acc * sa).astype(BF16)
        out[1][...] = (acc * sb).astype(BF16)
        out[2][:, :D_MODEL] = (acc * ex[2][...].astype(F32) * (sa * (1.0 - sa))).astype(BF16)
        out[2][:, D_MODEL:] = (acc * ex[3][...].astype(F32) * (sb * (1.0 - sb))).astype(BF16)

    ga_spec = pl.BlockSpec((tm, D_MODEL), lambda i, j, k: (i, R_GA // D_MODEL))
    gb_spec = pl.BlockSpec((tm, D_MODEL), lambda i, j, k: (i, R_GB // D_MODEL))
    gates_spec = pl.BlockSpec((tm, 2 * D_MODEL), lambda i, j, k: (i, R_GA // (2 * D_MODEL)))
    dpa, dpb, dz_rest = _mm(
        "out_proj_bwd", dx1, w_out_t, "nn", tm, D_MODEL, D_MODEL,
        [(_sds((S, D_MODEL), BF16), row_tile), (_sds((S, D_MODEL), BF16), row_tile), (_sds((S, REST_W), BF16), gates_spec)],
        epi_dy, extras=[(z_rest, ga_spec), (z_rest, gb_spec), (pa, row_tile), (pb, row_tile)])
    gw_out = _mm("gw_out", y, dx1, "tn", D_MODEL, D_MODEL, tk,
                 [(_sds((D_MODEL, D_MODEL), BF16), _tile_ij(D_MODEL, D_MODEL))], _epi_store(BF16))[0]
    gw_pa = _mm("gw_proj_a", o_a, dpa, "tn", ATT_OUT_W, D_MODEL, tk,
                [(_sds((ATT_OUT_W, D_MODEL), BF16), _tile_ij(ATT_OUT_W, D_MODEL))], _epi_store(BF16))[0]
    gw_pb = _mm("gw_proj_b", o_r, dpb, "tn", 1024, D_MODEL, tk,
                [(_sds((RET_V_W, D_MODEL), BF16), _tile_ij(1024, D_MODEL))], _epi_store(BF16))[0]

    def epi_doa(acc, ex, out):
        out[0][...] = acc
        prod = acc * ex[0][...]
        out[1][...] = jnp.concatenate(
            [jnp.broadcast_to(jnp.sum(prod[:, j * DH:(j + 1) * DH], axis=-1, keepdims=True), (prod.shape[0], DH))
             for j in range(HPG)], axis=1)

    slot_tile = _tile_ij(tm, ATT_OUT_W)
    do_a, delta = _mm("proj_a_bwd", dpa, w_pa_t, "nn", tm, ATT_OUT_W, D_MODEL,
                      [(_sds((S, ATT_OUT_W), F32), slot_tile), (_sds((S, ATT_OUT_W), F32), slot_tile)],
                      epi_doa, extras=[(o_a, slot_tile)])
    owed = [_chip_core(_by_owner_cols(gw_pa)), _chip_core(gw_pb.reshape(N_DEV, -1, D_MODEL)),
            _chip_core(gw_out.reshape(N_DEV, -1, D_MODEL)), _chip_core(gw_up), _chip_core(gw_down.reshape(N_DEV, -1, D_MODEL))]
    names = ("w_proj_a", "w_proj_b", "w_out", "w_up", "w_down")
    (do_r,), got = _mm("proj_b_bwd", dpb, w_pb_t, "nn", tm, RET_V_W, D_MODEL,
                       [(_sds((S, RET_V_W), F32), _tile_ij(tm, RET_V_W))], _epi_store(F32), exchange=_pair_exchange(owed))
    chip_sums = [_pair_sum(f"pair_sum_{n}", g, r, core, min(256, g.shape[2])) for n, g, r in zip(names, owed, got)]

    (dz_rest, g_gn_g, g_gn_b), parts_late = _ret_bwd(do_r, o_pre, states, z_rest, dz_rest, gn_g, gn_b,
                                                     _chip_exchange(chip_sums))
    dz_att, gq_parts, gk_parts = None, [], []
    for gi, ((_, d), nb) in enumerate(zip(ATT_GROUPS, ATT_BLOCKS_PER_STEP)):
        dz_att, gq_p, gk_p = _att_bwd_group(z_att, dz_att, do_a, delta, lmix, gq3, gk3, slopes3, gi, d, nb)
        gq_parts.append(gq_p)
        gk_parts.append(gk_p)
    g_qn = jnp.concatenate(gq_parts, axis=0).reshape(1, ATT_HEADS, DH)
    g_kn = jnp.concatenate(gk_parts, axis=0).reshape(1, ATT_HEADS, DH)

    gw_att = _mm("gw_in_att", xn, dz_att, "tn", D_MODEL, ATT_W, tk,
                 [(_sds((D_MODEL, 3 * ATT_W), BF16), _tile_ij(D_MODEL, ATT_W))], _epi_store(BF16))[0]
    gw_rest = _mm("gw_in_rest", xn, dz_rest, "tn", D_MODEL, 1024, tk,
                  [(_sds((D_MODEL, REST_W), BF16), _tile_ij(D_MODEL, 1024))], _epi_store(BF16))[0]
    gw_att = gw_att.reshape(D_MODEL, ATT_HEADS, 3, DH).transpose(0, 2, 1, 3).reshape(D_MODEL, 3 * ATT_W)
    gw_in = jnp.concatenate([gw_att, gw_rest], axis=1)
    owed_in = _chip_core(_by_owner_cols(gw_in))
    (dxn_att,), (got_in,) = _mm("in_proj_att_bwd", dz_att, w_att_t, "nn", tm, D_MODEL, 3 * ATT_W,
                                [(_sds((S, D_MODEL), F32), row_tile)], _epi_store(F32), exchange=_pair_exchange([owed_in]))
    chip_sum_in = _pair_sum("pair_sum_w_in", owed_in, got_in, core, min(256, owed_in.shape[2]))

    def epi_norm1(acc, ex, out):
        dx, dg = _rms_bwd(acc + ex[0][...], ex[1][...], ex[3][...])
        out[0][...] = ex[2][...] + dx
        _acc_rows(out[1], dg, pl.program_id(0) == 0)

    short_tile = _tile_ij(ts, D_MODEL)
    (grad_x, g_norm1), parts_in = _mm(
        "in_proj_rest_bwd", dz_rest, w_rest_t, "nn", ts, D_MODEL, REST_W,
        [(_sds((S, D_MODEL), F32), short_tile), (_sds((1, D_MODEL), F32), vec)],
        epi_norm1, extras=[(dxn_att, short_tile), (x, short_tile), (dx1, short_tile), (norm1_g, vec)],
        exchange=_chip_exchange([chip_sum_in]))

    small = (g_norm1, g_qn, g_kn, g_gn_g, g_gn_b, g_norm2)
    return loss_local, grad_x, list(parts_in) + list(parts_late), small


def _position():
    return lax.axis_index("x"), lax.axis_index("y"), lax.axis_index("c")


def _other_chips(x, y):
    return [(1 - x, y), (x, 1 - y), (1 - x, 1 - y)]


_ANY = pl.BlockSpec(memory_space=pl.ANY)


def _gather_exchange(shards):
    nw = len(shards)

    def plan(x_refs, out_refs, sems):
        send_sems, recv_sems, local_sems = sems
        x, y, c = _position()
        me, sibling = (x, y, c), (x, y, 1 - c)
        chips = _other_chips(x, y)

        def copy(w, k, block, to, own=False):
            px, py, pc = block
            rows = out_refs[w].at[4 * px + 2 * py + pc]
            return pltpu.make_async_remote_copy(
                src_ref=x_refs[w] if own else rows, dst_ref=rows,
                send_sem=send_sems.at[7 * w + k], recv_sem=recv_sems.at[7 * w + k], device_id=to, device_id_type=MESH)

        def mine(w):
            return pltpu.make_async_copy(x_refs[w], out_refs[w].at[4 * x + 2 * y + c], local_sems.at[w])

        def own_sends(w):
            return [copy(w, 0, me, sibling, own=True)] + [copy(w, 1 + j, me, (*chip, c), own=True)
                                                          for j, chip in enumerate(chips)]

        def start():
            for w in range(nw):
                mine(w).start()
                for cp in own_sends(w):
                    cp.start()

        def relay():
            for j, chip in enumerate(chips):
                for w in range(nw):
                    copy(w, 1 + j, (*chip, c), me).wait_recv()
                    copy(w, 4 + j, (*chip, c), sibling).start()

        def finish():
            for w in range(nw):
                copy(w, 0, sibling, me).wait_recv()
                for j, chip in enumerate(chips):
                    copy(w, 4 + j, (*chip, 1 - c), me).wait_recv()
            for w in range(nw):
                for cp in own_sends(w):
                    cp.wait_send()
                for j, chip in enumerate(chips):
                    copy(w, 4 + j, (*chip, c), sibling).wait_send()
                mine(w).wait()

        return start, relay, finish

    return _Exchange(shards, [_sds((N_DEV,) + s.shape, s.dtype) for s in shards],
                     [pltpu.SemaphoreType.DMA((7 * nw,)), pltpu.SemaphoreType.DMA((7 * nw,)),
                      pltpu.SemaphoreType.DMA((nw,))], plan)


def _run_exchange(name, exchange):
    n_in, n_out = len(exchange.ins), len(exchange.out_shapes)

    def body(*refs):
        start, relay, finish = exchange.split(refs[:n_in], refs[n_in:n_in + n_out], refs[n_in + n_out:])
        start()
        if relay is not None:
            relay()
        finish()

    return pl.pallas_call(
        body, name=name, out_shape=exchange.out_shapes,
        in_specs=[_ANY] * n_in, out_specs=[_ANY] * n_out, scratch_shapes=exchange.sems,
    )(*exchange.ins)


def _pair_exchange(grads):
    ng = len(grads)

    def plan(g_refs, out_refs, sems):
        send_sems, recv_sems = sems
        x, y, c = _position()

        def copies():
            return [pltpu.make_async_remote_copy(
                src_ref=g_refs[w].at[:, 1 - c], dst_ref=out_refs[w], send_sem=send_sems.at[w],
                recv_sem=recv_sems.at[w], device_id=(x, y, 1 - c), device_id_type=MESH) for w in range(ng)]

        def start():
            for cp in copies():
                cp.start()

        def finish():
            for cp in copies():
                cp.wait()

        return start, None, finish

    return _Exchange(grads, [_sds((N_CHIP,) + g.shape[2:], g.dtype) for g in grads],
                     [pltpu.SemaphoreType.DMA((ng,)), pltpu.SemaphoreType.DMA((ng,))], plan)


def _small_all_gather(small):
    def plan(in_refs, out_refs, sems):
        (s_ref,), (s_out,) = in_refs, out_refs
        send_sems, recv_sems, local_sem = sems
        x, y, c = _position()
        me_id = 4 * x + 2 * y + c
        flips = [(a, b, e) for a in (0, 1) for b in (0, 1) for e in (0, 1)][1:]
        peers = [(x ^ a, y ^ b, c ^ e) for a, b, e in flips]

        def start():
            pltpu.make_async_copy(s_ref, s_out.at[me_id], local_sem).start()
            for k, p in enumerate(peers):
                pltpu.make_async_remote_copy(
                    src_ref=s_ref, dst_ref=s_out.at[me_id], send_sem=send_sems.at[k], recv_sem=recv_sems.at[k],
                    device_id=p, device_id_type=MESH).start()

        def finish():
            for k, (px, py, pc) in enumerate(peers):
                pltpu.make_async_remote_copy(
                    src_ref=s_ref, dst_ref=s_out.at[4 * px + 2 * py + pc], send_sem=send_sems.at[k],
                    recv_sem=recv_sems.at[k], device_id=(px, py, pc), device_id_type=MESH).wait()
            pltpu.make_async_copy(s_ref, s_out.at[me_id], local_sem).wait()

        return start, None, finish

    return _run_exchange("small_grad_all_gather", _Exchange(
        [small], [_sds((N_DEV,) + small.shape, small.dtype)],
        [pltpu.SemaphoreType.DMA((7,)), pltpu.SemaphoreType.DMA((7,)), pltpu.SemaphoreType.DMA], plan))[0]


def _pair_sum(name, g, got, core, tr):
    n_chip, _, R, C = g.shape

    def body(c_ref, a_ref, b_ref, o_ref):
        del c_ref
        o_ref[...] = (a_ref[...].astype(F32) + b_ref[...].astype(F32)).astype(o_ref.dtype)

    return pl.pallas_call(
        body, name=name,
        grid_spec=pltpu.PrefetchScalarGridSpec(
            num_scalar_prefetch=1, grid=(n_chip, R // tr),
            in_specs=[pl.BlockSpec((None, None, tr, C), lambda ch, i, c_ref: (ch, c_ref[0], i, 0)),
                      pl.BlockSpec((None, tr, C), lambda ch, i, c_ref: (ch, i, 0))],
            out_specs=pl.BlockSpec((None, tr, C), lambda ch, i, c_ref: (ch, i, 0))),
        out_shape=_sds(got.shape, got.dtype),
        compiler_params=_cparams(("parallel", "parallel")),
    )(core, g, got)


def _chip_exchange(parts):
    ng = len(parts)

    def plan(p_refs, out_refs, sems):
        send_sems, recv_sems, local_sems = sems
        x, y, c = _position()
        my_chip = 2 * x + y

        def copies():
            local = [pltpu.make_async_copy(p_refs[w].at[my_chip], out_refs[w].at[my_chip], local_sems.at[w])
                     for w in range(ng)]
            remote = [pltpu.make_async_remote_copy(
                src_ref=p_refs[w].at[2 * cx + cy], dst_ref=out_refs[w].at[my_chip],
                send_sem=send_sems.at[3 * w + k], recv_sem=recv_sems.at[3 * w + k],
                device_id=(cx, cy, c), device_id_type=MESH)
                for w in range(ng) for k, (cx, cy) in enumerate(_other_chips(x, y))]
            return local + remote

        def start():
            for cp in copies():
                cp.start()

        def finish():
            for cp in copies():
                cp.wait()

        return start, None, finish

    return _Exchange(parts, [_sds(p.shape, p.dtype) for p in parts],
                     [pltpu.SemaphoreType.DMA((3 * ng,)), pltpu.SemaphoreType.DMA((3 * ng,)),
                      pltpu.SemaphoreType.DMA((ng,))], plan)


def _adamw(name, parts, w, m, v, tr):
    n_parts = parts.shape[0]
    R, C = w.shape

    def body(p_ref, w_ref, m_ref, v_ref, g_ref, d_ref, mo_ref, vo_ref):
        g = p_ref[0].astype(F32)
        for i in range(1, n_parts):
            g = g + p_ref[i].astype(F32)
        m_new = ADAM_B1 * m_ref[...] + (1.0 - ADAM_B1) * g
        v_new = ADAM_B2 * v_ref[...] + (1.0 - ADAM_B2) * (g * g)
        m_hat = m_new / (1.0 - ADAM_B1 ** ADAM_STEP)
        v_hat = v_new / (1.0 - ADAM_B2 ** ADAM_STEP)
        g_ref[...] = g
        d_ref[...] = -ADAM_LR * (m_hat / (jnp.sqrt(v_hat) + ADAM_EPS) + ADAM_WD * w_ref[...])
        mo_ref[...] = m_new
        vo_ref[...] = v_new

    tile = pl.BlockSpec((tr, C), lambda i: (i, 0))
    return pl.pallas_call(
        body, name=name, grid=(R // tr,),
        in_specs=[pl.BlockSpec((n_parts, tr, C), lambda i: (0, i, 0)), tile, tile, tile],
        out_specs=[tile] * 4,
        out_shape=[_sds((R, C), F32)] * 4,
        compiler_params=_cparams(("parallel",)),
    )(parts, w, m, v)


def _flat_small(arrs):
    return jnp.concatenate([a.reshape(-1) for a in arrs]).reshape(SMALL_ROWS, LANES)


def _cols(g):
    return g.transpose(1, 0, 2).reshape(g.shape[1], -1)


def _rows_of(g):
    return g.reshape(-1, g.shape[2])


def _by_owner_cols(g):
    rows, cols = g.shape
    return g.reshape(rows, N_DEV, cols // N_DEV).transpose(1, 0, 2)


def _chip_core(g):
    return g.reshape((N_CHIP, 2) + g.shape[1:])


def kernel(x, norm1_g, w_in, q_norm_g, k_norm_g, ret_gn_g, ret_gn_b, w_proj_a, w_proj_b, w_out, norm2_g, w_up, w_down, loss_target, m_norm1_g, m_w_in, m_q_norm_g, m_k_norm_g, m_ret_gn_g, m_ret_gn_b, m_w_proj_a, m_w_proj_b, m_w_out, m_norm2_g, m_w_up, m_w_down, v_norm1_g, v_w_in, v_q_norm_g, v_k_norm_g, v_ret_gn_g, v_ret_gn_b, v_w_proj_a, v_w_proj_b, v_w_out, v_norm2_g, v_w_up, v_w_down):
    big_w = (w_in, w_proj_a, w_proj_b, w_out, w_up, w_down)
    big_m = (m_w_in, m_w_proj_a, m_w_proj_b, m_w_out, m_w_up, m_w_down)
    big_v = (v_w_in, v_w_proj_a, v_w_proj_b, v_w_out, v_w_up, v_w_down)
    small_w = (norm1_g, q_norm_g, k_norm_g, ret_gn_g, ret_gn_b, norm2_g)
    small_m = (m_norm1_g, m_q_norm_g, m_k_norm_g, m_ret_gn_g, m_ret_gn_b, m_norm2_g)
    small_v = (v_norm1_g, v_q_norm_g, v_k_norm_g, v_ret_gn_g, v_ret_gn_b, v_norm2_g)

    shards = [w[0].astype(BF16) for w in big_w]
    (g_in,) = _run_exchange("all_gather_w_in", _gather_exchange(shards[:1]))
    core = lax.axis_index("c").astype(jnp.int32).reshape(1)
    loss_local, grad_x, parts, small_g = _step(
        x[0], loss_target[0], norm1_g, q_norm_g[0], k_norm_g[0], ret_gn_g, ret_gn_b, norm2_g, _cols(g_in), shards[1:], core)
    small_all = _small_all_gather(_flat_small(small_g))
    names = ("w_in", "w_proj_a", "w_proj_b", "w_out", "w_up", "w_down")

    res = {}
    for n, p, w, m, v in zip(names, parts, big_w, big_m, big_v):
        outs = _adamw(f"adamw_{n}", p, w[0], m[0], v[0], min(128, w.shape[1]))
        res[n] = [o[None] for o in outs]
    s_outs = _adamw("adamw_small", small_all, _flat_small(small_w), _flat_small(small_m), _flat_small(small_v), SMALL_ROWS)
    small_names = ("norm1_g", "q_norm_g", "k_norm_g", "ret_gn_g", "ret_gn_b", "norm2_g")
    for n in small_names:
        res[n] = []
    for o in s_outs:
        flat, off = o.reshape(-1), 0
        for n, w in zip(small_names, small_w):
            res[n].append(flat[off:off + w.size].reshape(w.shape))
            off += w.size

    order = ("norm1_g", "w_in", "q_norm_g", "k_norm_g", "ret_gn_g", "ret_gn_b", "w_proj_a", "w_proj_b", "w_out",
             "norm2_g", "w_up", "w_down")
    loss = lax.psum(loss_local, MESH_AXES)
    return (loss, grad_x[None], *[res[n][0] for n in order], *[res[n][1] for n in order],
            *[res[n][2] for n in order], *[res[n][3] for n in order])
```

```python
import math

import numpy as np
import jax
import jax.numpy as jnp
from jax import lax
from jax.experimental import pallas as pl
from jax.experimental.pallas import tpu as pltpu

F32 = jnp.float32
BF16 = jnp.bfloat16

D_MODEL = 1024
ATT_GROUPS = ((128, 1), (512, 4), (2048, 16))
ATT_BLOCKS_PER_STEP = (8, 1, 1)
ATT_TOGETHER = 4
HPG = 4
ATT_HEADS = 12
DH = 128
BLK = 128
ATT_W = ATT_HEADS * DH
ATT_OUT_W = HPG * DH
RET_HEADS = 4
RET_QK = 256
RET_V = 512
RET_QK_W = RET_HEADS * RET_QK
RET_V_W = RET_HEADS * RET_V
CHUNK = 128
D_FF = 4096
IN_W = 12800
REST_W = IN_W - 3 * ATT_W
EPS = 1e-6
ADAM_LR, ADAM_B1, ADAM_B2, ADAM_EPS, ADAM_WD, ADAM_STEP = 0.001, 0.9, 0.999, 1e-08, 0.01, 10
N_DEV = 8
N_CHIP = 4
MESH_AXES = ("x", "y", "c")
MESH = pl.DeviceIdType.MESH
VMEM_LIMIT = 56 * 1024 * 1024
LANES = 128
NEG = -1e30

_NN = (((1,), (0,)), ((), ()))
_NT = (((1,), (1,)), ((), ()))
_TN = (((0,), (0,)), ((), ()))

R_Q, R_K, R_V, R_G, R_GA, R_GB = 0, 1024, 2048, 4096, 6144, 7168

LOG_GAMMA = [float(v) for v in np.log(1.0 - 2.0 ** (-5.0 - np.arange(RET_HEADS, dtype=np.float32))).astype(np.float32)]
ALIBI = np.asarray(2.0 ** (-8.0 * np.arange(1, ATT_HEADS + 1, dtype=np.float32) / ATT_HEADS), np.float32)

SMALL_ROWS = (1024 + 1536 + 1536 + 2048 + 2048 + 1024) // LANES


def _dot(a, b, dims=_NN):
    return lax.dot_general(a, b, dims, preferred_element_type=F32)


def _cparams(sem):
    return pltpu.CompilerParams(dimension_semantics=sem, vmem_limit_bytes=VMEM_LIMIT)


def _sds(shape, dtype):
    return jax.ShapeDtypeStruct(shape, dtype)


class _Exchange:
    def __init__(self, ins, out_shapes, sems, plan):
        self.ins, self.out_shapes, self.sems, self.plan = list(ins), list(out_shapes), list(sems), plan

    def split(self, in_refs, out_refs, sem_refs):
        return self.plan(in_refs, out_refs, sem_refs)


def _mm(name, a, b, mode, tm, tn, tk, outs, epi, extras=(), b_pro=None,
        sem=("parallel", "parallel", "arbitrary"), exchange=None, in_place=None):
    if mode == "nn":
        (M, K), (_, N) = a.shape, b.shape
        a_spec = pl.BlockSpec((tm, tk), lambda i, j, k: (i, k))
        dims = _NN
    else:
        (K, M), (_, N) = a.shape, b.shape
        a_spec = pl.BlockSpec((tk, tm), lambda i, j, k: (k, i))
        dims = _TN
    assert M % tm == 0 and N % tn == 0 and K % tk == 0, (name, M, N, K, tm, tn, tk)
    nk = K // tk
    whole_b = dict(pipeline_mode=pl.Buffered(1)) if (nk == 1 and N == tn) else {}
    b_spec = pl.BlockSpec((tk, tn), lambda i, j, k: (k, j), **whole_b)
    n_ex, n_out = len(extras), len(outs)
    grid = (M // tm, N // tn, nk)
    n_xi = len(exchange.ins) if exchange else 0
    n_xo = len(exchange.out_shapes) if exchange else 0
    n_acc = 1 if nk > 1 else 0

    n_ip = 1 if in_place else 0

    def body(a_ref, b_ref, *rest):
        ex, rest = rest[:n_ex], rest[n_ex:]
        x_in, rest = rest[:n_xi], rest[n_xi + n_ip:]
        out, rest = rest[:n_out], rest[n_out:]
        x_out, rest = rest[:n_xo], rest[n_xo:]
        step = (pl.program_id(0) * grid[1] + pl.program_id(1)) * grid[2] + pl.program_id(2)
        n_steps = grid[0] * grid[1] * grid[2]
        if exchange:
            start, relay, finish = exchange.split(x_in, x_out, rest[n_acc:])
            pl.when(step == 0)(start)
        bv = b_ref[...]
        if b_pro is not None:
            bv = b_pro(bv)
        part = _dot(a_ref[...].astype(BF16), bv.astype(BF16), dims)
        if nk == 1:
            epi(part, ex, out)
        else:
            acc_ref = rest[0]
            k = pl.program_id(2)

            @pl.when(k == 0)
            def _():
                acc_ref[...] = part

            @pl.when(k > 0)
            def _():
                acc_ref[...] += part

            @pl.when(k == nk - 1)
            def _():
                epi(acc_ref[...], ex, out)
        if exchange:
            if relay is not None:
                pl.when(step == (7 * n_steps) // 8)(relay)
            pl.when(step == n_steps - 1)(finish)

    res = pl.pallas_call(
        body,
        name=name,
        grid=grid,
        in_specs=[a_spec, b_spec] + [s for _, s in extras] + [_ANY] * (n_xi + n_ip),
        out_specs=[s for _, s in outs] + [_ANY] * n_xo,
        out_shape=[o for o, _ in outs] + (exchange.out_shapes if exchange else []),
        scratch_shapes=([pltpu.VMEM((tm, tn), F32)] if nk > 1 else []) + (exchange.sems if exchange else []),
        input_output_aliases={2 + n_ex + n_xi: in_place[1]} if in_place else {},
        compiler_params=_cparams(("arbitrary",) * 3 if exchange else sem),
    )(a, b, *[e for e, _ in extras], *(exchange.ins if exchange else []), *([in_place[0]] if in_place else []))
    return (res[:n_out], res[n_out:]) if exchange else res


def _tile_ij(tm, tn):
    return pl.BlockSpec((tm, tn), lambda i, j, k: (i, j))


def _epi_store(dtype):
    def epi(acc, ex, out):
        out[0][...] = acc.astype(dtype)
    return epi


def _rms_rows(x):
    return lax.rsqrt(jnp.mean(x * x, axis=-1, keepdims=True) + EPS)


def _acc_rows(ref, part, first):
    @pl.when(first)
    def _():
        ref[...] = part

    @pl.when(jnp.logical_not(first))
    def _():
        ref[...] += part


def _rmsnorm_fwd(x, g, tm):
    S, Dm = x.shape

    def body(x_ref, g_ref, o_ref):
        xv = x_ref[...]
        o_ref[...] = (xv * _rms_rows(xv) * g_ref[...]).astype(BF16)

    return pl.pallas_call(
        body, name="rmsnorm1_fwd", grid=(S // tm,),
        in_specs=[pl.BlockSpec((tm, Dm), lambda i: (i, 0)), pl.BlockSpec((1, Dm), lambda i: (0, 0))],
        out_specs=pl.BlockSpec((tm, Dm), lambda i: (i, 0)),
        out_shape=_sds((S, Dm), BF16),
        compiler_params=_cparams(("parallel",)),
    )(x, g)


def _rows(ref, r, b, d):
    if d == 1:
        return ref[b * BLK:(b + 1) * BLK, :]
    return ref[pl.ds(b * BLK * d + r, BLK, stride=d), :]


def _put_rows(ref, r, b, d, val):
    if d == 1:
        ref[b * BLK:(b + 1) * BLK, :] = val
    else:
        ref[pl.ds(b * BLK * d + r, BLK, stride=d), :] = val


def _head_norm(x, g):
    r = _rms_rows(x)
    xh = x * r
    return (xh * g).astype(BF16), xh, r


def _head_norm_bwd(dyn, xh, r, g):
    dxh = dyn * g
    dx = r * (dxh - xh * jnp.mean(dxh * xh, axis=-1, keepdims=True))
    return dx, jnp.sum(dyn * xh, axis=0, keepdims=True)


def _att_mask_bias(slope, d, first):
    qi = lax.broadcasted_iota(jnp.int32, (BLK, 2 * BLK), 0)
    kj = lax.broadcasted_iota(jnp.int32, (BLK, 2 * BLK), 1)
    dist = BLK + qi - kj
    valid = (dist >= 0) & (dist <= BLK)
    if first is not None:
        valid = valid & (jnp.logical_not(first) | (kj >= BLK))
    bias = -slope * (dist * d).astype(F32)
    return valid, bias


def _att_specs(gi, d, nb, S):
    span = BLK * d
    sb = span * nb
    nspan = S // span
    before = lambda n: jnp.maximum(n * nb - 1, 0)
    after = lambda n: jnp.minimum((n + 1) * nb, nspan - 1)
    zcol = lambda j, kind: 3 * (gi * HPG + j) + kind
    cur = lambda kind: pl.BlockSpec((sb, DH), lambda j, n: (n, zcol(j, kind)))
    prev = lambda kind: pl.BlockSpec((span, DH), lambda j, n: (before(n), zcol(j, kind)))
    nxt = lambda kind: pl.BlockSpec((span, DH), lambda j, n: (after(n), zcol(j, kind)))
    slot = pl.BlockSpec((sb, DH), lambda j, n: (n, j))
    slot_next = pl.BlockSpec((span, DH), lambda j, n: (after(n), j))
    head = pl.BlockSpec((None, 1, DH), lambda j, n: (gi * HPG + j, 0, 0))
    return cur, prev, nxt, slot, slot_next, head


def _att_fwd_group(z_att, gq3, gk3, slopes3, gi, d, nb, others=()):
    S = z_att.shape[0]
    nsb = S // (BLK * d * nb)
    scale = DH ** -0.5
    n_other = len(others)

    def body(q_ref, k_ref, v_ref, kp_ref, vp_ref, gq_ref, gk_ref, sl_ref, *rest):
        other_refs, (o_ref, l_ref) = rest[:2 * n_other], rest[2 * n_other:]
        slope = sl_ref[...][:, :1]
        valid0, bias = _att_mask_bias(slope, d, pl.program_id(1) == 0)
        valid_in, _ = _att_mask_bias(slope, d, None)
        gq, gk = gq_ref[...], gk_ref[...]
        memo = {}

        def get(kind, r, b):
            if (kind, r, b) not in memo:
                if kind == "k":
                    val = _head_norm(_rows(k_ref if b >= 0 else kp_ref, r, max(b, 0), d), gk)[0]
                else:
                    val = _rows(v_ref if b >= 0 else vp_ref, r, max(b, 0), d).astype(BF16)
                memo[(kind, r, b)] = val
            return memo[(kind, r, b)]

        units = [(r, b) for r in range(d) for b in range(nb)]
        for c0 in range(0, len(units), ATT_TOGETHER):
            us = units[c0:c0 + ATT_TOGETHER]
            q = [_head_norm(_rows(q_ref, r, b, d), gq)[0] for r, b in us]
            k2 = [jnp.concatenate([get("k", r, b - 1), get("k", r, b)], axis=0) for r, b in us]
            v2 = [jnp.concatenate([get("v", r, b - 1), get("v", r, b)], axis=0) for r, b in us]
            s = [jnp.where(valid0 if b == 0 else valid_in, _dot(q[i], k2[i], _NT) * scale + bias, NEG)
                 for i, (r, b) in enumerate(us)]
            m = [jnp.max(si, axis=-1, keepdims=True) for si in s]
            p = [jnp.exp(si - mi) for si, mi in zip(s, m)]
            den = [jnp.sum(pi, axis=-1, keepdims=True) for pi in p]
            o = [_dot(pi.astype(BF16), vi) / di for pi, vi, di in zip(p, v2, den)]
            for i, (r, b) in enumerate(us):
                _put_rows(o_ref, r, b, d, o[i])
                _put_rows(l_ref, r, b, d, jnp.broadcast_to(m[i] + jnp.log(den[i]), (BLK, DH)))
        if n_other:
            os_ = [ref[...] for ref in other_refs[:n_other]] + [o_ref[...]]
            ls_ = [ref[...] for ref in other_refs[n_other:]] + [l_ref[...]]
            m = ls_[0]
            for l in ls_[1:]:
                m = jnp.maximum(m, l)
            es = [jnp.exp(l - m) for l in ls_]
            tot, mix = es[0], es[0] * os_[0]
            for e, o in zip(es[1:], os_[1:]):
                tot, mix = tot + e, mix + e * o
            o_ref[...] = mix / tot
            l_ref[...] = m + jnp.log(tot)

    cur, prev, _, slot, _, head = _att_specs(gi, d, nb, S)
    return pl.pallas_call(
        body, name=f"att_fwd_g{gi}", grid=(HPG, nsb),
        in_specs=[cur(0), cur(1), cur(2), prev(1), prev(2), head, head, head] + [slot] * (2 * n_other),
        out_specs=[slot, slot],
        out_shape=[_sds((S, ATT_OUT_W), F32), _sds((S, ATT_OUT_W), F32)],
        compiler_params=_cparams(("parallel", "arbitrary")),
    )(z_att, z_att, z_att, z_att, z_att, gq3, gk3, slopes3, *[o for o, _ in others], *[l for _, l in others])


def _ret_tables(lg):
    ri = lax.broadcasted_iota(jnp.int32, (CHUNK, CHUNK), 0)
    ci = lax.broadcasted_iota(jnp.int32, (CHUNK, CHUNK), 1)
    diff = (ri - ci).astype(F32)
    decay = jnp.where(diff >= 0, jnp.exp(lg * jnp.maximum(diff, 0.0)), 0.0)
    idx = lax.broadcasted_iota(jnp.int32, (CHUNK, 1), 0).astype(F32)
    xi = jnp.exp(lg * (idx + 1.0))
    zeta = jnp.exp(lg * (CHUNK - 1.0 - idx))
    return decay, xi, zeta, math.exp(lg * CHUNK)


def _ret_specs(nch, rev):
    idx = (lambda n: nch - 1 - n) if rev else (lambda n: n)
    qk = lambda off: pl.BlockSpec((CHUNK, RET_QK_W), lambda n: (idx(n), off // RET_QK_W))
    vv = lambda off: pl.BlockSpec((CHUNK, RET_V_W), lambda n: (idx(n), off // RET_V_W))
    par = pl.BlockSpec((1, RET_V_W), lambda n: (0, 0))
    wide = pl.BlockSpec((CHUNK, RET_V_W), lambda n: (idx(n), 0))
    st = pl.BlockSpec((RET_HEADS, None, RET_QK, RET_V), lambda n: (0, idx(n), 0, 0))
    return qk, vv, par, wide, st


def _ret_fwd(z_rest, gn_g, gn_b):
    S = z_rest.shape[0]
    nch = S // CHUNK

    def body(q_ref, k_ref, v_ref, gr_ref, g_ref, b_ref, or_ref, o_ref, st_ref, state):
        @pl.when(pl.program_id(0) == 0)
        def _():
            state[...] = jnp.zeros_like(state)

        for h in range(RET_HEADS):
            decay, xi, zeta, gch = _ret_tables(LOG_GAMMA[h])
            cq = slice(h * RET_QK, (h + 1) * RET_QK)
            cv = slice(h * RET_V, (h + 1) * RET_V)
            q = q_ref[:, cq]
            kc32 = k_ref[:, cq].astype(F32) * (RET_QK ** -0.5)
            kc = kc32.astype(BF16)
            v = v_ref[:, cv]
            st = state[h]
            stb = st.astype(BF16)
            st_ref[h] = stb
            s = _dot(q, kc, _NT) * decay
            o = _dot(s.astype(BF16), v) + _dot(q, stb) * xi
            state[h] = st * gch + _dot((kc32 * zeta).astype(BF16), v, _TN)
            mu = jnp.mean(o, axis=-1, keepdims=True)
            cen = o - mu
            yh = cen * lax.rsqrt(jnp.mean(cen * cen, axis=-1, keepdims=True) + EPS)
            gr = gr_ref[:, cv].astype(F32)
            or_ref[:, cv] = ((yh * g_ref[:, cv] + b_ref[:, cv]) * (gr * jax.nn.sigmoid(gr))).astype(BF16)
            o_ref[:, cv] = o

    qk, vv, par, wide, st = _ret_specs(nch, False)
    return pl.pallas_call(
        body, name="ret_fwd", grid=(nch,),
        in_specs=[qk(R_Q), qk(R_K), vv(R_V), vv(R_G), par, par],
        out_specs=[wide, wide, st],
        out_shape=[_sds((S, RET_V_W), BF16), _sds((S, RET_V_W), F32), _sds((RET_HEADS, nch, RET_QK, RET_V), BF16)],
        scratch_shapes=[pltpu.VMEM((RET_HEADS, RET_QK, RET_V), F32)],
        compiler_params=_cparams(("arbitrary",)),
    )(z_rest, z_rest, z_rest, z_rest, gn_g, gn_b)


def _merge_fwd(o_a, o_r, z_rest, x, wpa, wpb, wout, g2, tm):
    S = x.shape[0]

    def body(oa_ref, or_ref, ga_ref, gb_ref, x_ref, wpa_ref, wpb_ref, wo_ref, g2_ref,
             x1_ref, y_ref, pa_ref, pb_ref, xn2_ref):
        pa = _dot(oa_ref[...].astype(BF16), wpa_ref[...])
        pb = _dot(or_ref[...], wpb_ref[...])
        y = jax.nn.sigmoid(ga_ref[...].astype(F32)) * pa + jax.nn.sigmoid(gb_ref[...].astype(F32)) * pb
        yb = y.astype(BF16)
        x1 = x_ref[...] + _dot(yb, wo_ref[...])
        x1_ref[...] = x1
        y_ref[...] = yb
        pa_ref[...] = pa.astype(BF16)
        pb_ref[...] = pb.astype(BF16)
        xn2_ref[...] = (x1 * _rms_rows(x1) * g2_ref[...]).astype(BF16)

    row = lambda w: pl.BlockSpec((tm, w), lambda i: (i, 0))
    full = lambda a: pl.BlockSpec(a.shape, lambda i: (0, 0))
    return pl.pallas_call(
        body, name="merge_fwd", grid=(S // tm,),
        in_specs=[row(ATT_OUT_W), row(RET_V_W),
                  pl.BlockSpec((tm, D_MODEL), lambda i: (i, R_GA // D_MODEL)),
                  pl.BlockSpec((tm, D_MODEL), lambda i: (i, R_GB // D_MODEL)),
                  row(D_MODEL), full(wpa), full(wpb), full(wout), full(g2)],
        out_specs=[row(D_MODEL)] * 5,
        out_shape=[_sds((S, D_MODEL), F32)] + [_sds((S, D_MODEL), BF16)] * 4,
        compiler_params=_cparams(("parallel",)),
    )(o_a, o_r, z_rest, z_rest, x, wpa, wpb, wout, g2)


def _rms_bwd(dy, xv, g):
    r = _rms_rows(xv)
    xh = xv * r
    dg = dy * g
    dx = r * (dg - xh * jnp.mean(dg * xh, axis=-1, keepdims=True))
    return dx, jnp.sum(dy * xh, axis=0, keepdims=True)


def _norm_gate_bwd(dout, o, gr, gam, bet):
    cen = o - jnp.mean(o, axis=-1, keepdims=True)
    rstd = lax.rsqrt(jnp.mean(cen * cen, axis=-1, keepdims=True) + EPS)
    yh = cen * rstd
    y = yh * gam + bet
    sg = jax.nn.sigmoid(gr)
    dy = dout * (gr * sg)
    dgr = dout * y * (sg * (1.0 + gr * (1.0 - sg)))
    dyh = dy * gam
    do = rstd * (dyh - jnp.mean(dyh, axis=-1, keepdims=True) - yh * jnp.mean(dyh * yh, axis=-1, keepdims=True))
    return do, dgr, jnp.sum(dy * yh, axis=0, keepdims=True), jnp.sum(dy, axis=0, keepdims=True)


def _ret_bwd(do, states, z_rest, dz_rest):
    S = z_rest.shape[0]
    nch = S // CHUNK

    def body(do_ref, st_ref, q_ref, k_ref, v_ref, dz_in, dz_ref, gst):
        del dz_in

        @pl.when(pl.program_id(0) == 0)
        def _():
            gst[...] = jnp.zeros_like(gst)

        for h in range(RET_HEADS):
            decay, xi, zeta, gch = _ret_tables(LOG_GAMMA[h])
            cq = slice(h * RET_QK, (h + 1) * RET_QK)
            cv = slice(h * RET_V, (h + 1) * RET_V)
            q = q_ref[:, cq]
            kc32 = k_ref[:, cq].astype(F32) * (RET_QK ** -0.5)
            kc = kc32.astype(BF16)
            v = v_ref[:, cv]
            dob = do_ref[:, cv]
            a = (_dot(q, kc, _NT) * decay).astype(BF16)
            da = (_dot(dob, v, _NT) * decay).astype(BF16)
            dcross = (dob.astype(F32) * xi).astype(BF16)
            g_next = gst[h]
            gb = g_next.astype(BF16)
            dq = _dot(da, kc) + _dot(dcross, st_ref[h], _NT)
            dkc = _dot(da, q, _TN)
            dkz = _dot(v, gb, _NT)
            dv = _dot(a, dob, _TN) + _dot((kc32 * zeta).astype(BF16), gb)
            gst[h] = g_next * gch + _dot(q, dcross, _TN)
            dz_ref[:, R_Q + h * RET_QK:R_Q + (h + 1) * RET_QK] = dq.astype(BF16)
            dz_ref[:, R_K + h * RET_QK:R_K + (h + 1) * RET_QK] = ((dkc + dkz * zeta) * (RET_QK ** -0.5)).astype(BF16)
            dz_ref[:, R_V + h * RET_V:R_V + (h + 1) * RET_V] = dv.astype(BF16)

    qk, vv, _, wide, st = _ret_specs(nch, True)
    return pl.pallas_call(
        body, name="ret_bwd", grid=(nch,),
        in_specs=[wide, st, qk(R_Q), qk(R_K), vv(R_V), _ANY],
        out_specs=pl.BlockSpec((CHUNK, R_G), lambda n: (nch - 1 - n, 0)),
        out_shape=_sds(dz_rest.shape, BF16),
        input_output_aliases={5: 0},
        scratch_shapes=[pltpu.VMEM((RET_HEADS, RET_QK, RET_V), F32)],
        compiler_params=_cparams(("arbitrary",)),
    )(do, states, z_rest, z_rest, z_rest, dz_rest)


def _att_probs(q, k, lmix, valid, bias):
    s = _dot(q, k, _NT) * (DH ** -0.5) + bias
    return jnp.where(valid, jnp.exp(jnp.where(valid, s, NEG) - lmix), 0.0)


def _att_bwd_group(z_att, dz_att, do_a, delta, lmix, gq3, gk3, slopes3, gi, d, nb, exchange=None):
    S = z_att.shape[0]
    sb = BLK * d * nb
    nsb = S // sb
    scale = DH ** -0.5
    aliased = dz_att is not None
    n_xi = len(exchange.ins) if exchange else 0
    n_xo = len(exchange.out_shapes) if exchange else 0

    def body(q_ref, k_ref, v_ref, kp_ref, vp_ref, qn_ref, do_ref, don_ref, dl_ref, dln_ref, lm_ref, lmn_ref,
             gq_ref, gk_ref, sl_ref, *rest):
        rest = rest[1:] if aliased else rest
        x_in, (dz_ref, dgq_ref, dgk_ref), rest = rest[:n_xi], rest[n_xi:n_xi + 3], rest[n_xi + 3:]
        x_out, stage, x_sems = rest[:n_xo], rest[n_xo], rest[n_xo + 1:]
        n = pl.program_id(1)
        step = pl.program_id(0) * nsb + n
        if exchange:
            start, _, finish = exchange.split(x_in, x_out, x_sems)
            pl.when(step == 0)(start)
        slope = sl_ref[...][:, :1]
        valid0, bias = _att_mask_bias(slope, d, n == 0)
        valid_in, _ = _att_mask_bias(slope, d, None)
        qi = lax.broadcasted_iota(jnp.int32, (BLK, BLK), 0)
        kj = lax.broadcasted_iota(jnp.int32, (BLK, BLK), 1)
        dist_n = BLK + qi - kj
        valid_n_in = dist_n <= BLK
        valid_n_last = valid_n_in & (n < nsb - 1)
        bias_n = -slope * (dist_n * d).astype(F32)
        gq, gk = gq_ref[...], gk_ref[...]
        dgq = jnp.zeros((1, DH), F32)
        dgk = jnp.zeros((1, DH), F32)
        memo = {}

        def get(kind, r, b):
            if (kind, r, b) not in memo:
                inner = 0 <= b < nb
                bb = b if inner else 0
                if kind == "q":
                    val = _head_norm(_rows(q_ref if inner else qn_ref, r, bb, d), gq)
                elif kind == "k":
                    val = _head_norm(_rows(k_ref if inner else kp_ref, r, bb, d), gk)
                elif kind == "v":
                    val = _rows(v_ref if inner else vp_ref, r, bb, d).astype(BF16)
                elif kind == "do":
                    val = _rows(do_ref if inner else don_ref, r, bb, d).astype(BF16)
                elif kind == "dl":
                    val = _rows(dl_ref if inner else dln_ref, r, bb, d)[:, :1]
                else:
                    val = _rows(lm_ref if inner else lmn_ref, r, bb, d)[:, :1]
                memo[(kind, r, b)] = val
            return memo[(kind, r, b)]

        units = [(r, b) for r in range(d) for b in range(nb)]
        for c0 in range(0, len(units), ATT_TOGETHER):
            us = units[c0:c0 + ATT_TOGETHER]
            k2 = [jnp.concatenate([get("k", r, b - 1)[0], get("k", r, b)[0]], axis=0) for r, b in us]
            v2 = [jnp.concatenate([get("v", r, b - 1), get("v", r, b)], axis=0) for r, b in us]
            p = [_att_probs(get("q", r, b)[0], k2[i], get("lm", r, b), valid0 if b == 0 else valid_in, bias)
                 for i, (r, b) in enumerate(us)]
            dp = [_dot(get("do", r, b), v2[i], _NT) for i, (r, b) in enumerate(us)]
            ds = [(p[i] * (dp[i] - get("dl", r, b)) * scale).astype(BF16) for i, (r, b) in enumerate(us)]
            dq = [_dot(ds[i], k2[i]) for i in range(len(us))]
            p_n = [_att_probs(get("q", r, b + 1)[0], get("k", r, b)[0], get("lm", r, b + 1),
                              valid_n_last if b == nb - 1 else valid_n_in, bias_n) for r, b in us]
            dp_n = [_dot(get("do", r, b + 1), get("v", r, b), _NT) for r, b in us]
            ds_n = [(p_n[i] * (dp_n[i] - get("dl", r, b + 1)) * scale).astype(BF16) for i, (r, b) in enumerate(us)]
            dk = [_dot(ds[i][:, BLK:], get("q", r, b)[0], _TN) + _dot(ds_n[i], get("q", r, b + 1)[0], _TN)
                  for i, (r, b) in enumerate(us)]
            dv = [_dot(p[i][:, BLK:].astype(BF16), get("do", r, b), _TN)
                  + _dot(p_n[i].astype(BF16), get("do", r, b + 1), _TN) for i, (r, b) in enumerate(us)]
            for i, (r, b) in enumerate(us):
                _, qh, qr = get("q", r, b)
                _, kh, kr = get("k", r, b)
                dxq, dg_q = _head_norm_bwd(dq[i], qh, qr, gq)
                dxk, dg_k = _head_norm_bwd(dk[i], kh, kr, gk)
                _put_rows(stage.at[0], r, b, d, dxq)
                _put_rows(stage.at[1], r, b, d, dxk)
                _put_rows(stage.at[2], r, b, d, dv[i])
                dgq, dgk = dgq + dg_q, dgk + dg_k
        for kind in range(3):
            dz_ref[:, kind * DH:(kind + 1) * DH] = stage[kind].astype(BF16)
        _acc_rows(dgq_ref, dgq, n == 0)
        _acc_rows(dgk_ref, dgk, n == 0)
        if exchange:
            pl.when(step == HPG * nsb - 1)(finish)

    cur, prev, nxt, slot, slot_next, head = _att_specs(gi, d, nb, S)
    gain = pl.BlockSpec((None, 1, DH), lambda j, n: (j, 0, 0))
    in_specs = [cur(0), cur(1), cur(2), prev(1), prev(2), nxt(0),
                slot, slot_next, slot, slot_next, slot, slot_next, head, head, head]
    args = [z_att] * 6 + [do_a, do_a, delta, delta, lmix, lmix, gq3, gk3, slopes3]
    if aliased:
        in_specs.append(_ANY)
        args.append(dz_att)
    res = pl.pallas_call(
        body, name=f"att_bwd_g{gi}", grid=(HPG, nsb),
        in_specs=in_specs + [_ANY] * n_xi,
        out_specs=[pl.BlockSpec((sb, 3 * DH), lambda j, n: (n, gi * HPG + j)), gain, gain] + [_ANY] * n_xo,
        out_shape=[_sds(z_att.shape, BF16), _sds((HPG, 1, DH), F32), _sds((HPG, 1, DH), F32)]
        + (exchange.out_shapes if exchange else []),
        input_output_aliases={len(args) - 1: 0} if aliased else {},
        scratch_shapes=[pltpu.VMEM((3, sb, DH), F32)] + (exchange.sems if exchange else []),
        compiler_params=_cparams(("arbitrary", "arbitrary") if exchange else ("parallel", "arbitrary")),
    )(*args, *(exchange.ins if exchange else []))
    return res[:3], res[3:]


def _step(x, target, norm1_g, q_norm_g, k_norm_g, gn_g, gn_b, norm2_g, w_in, later_shards, core):
    S = x.shape[0]
    tm = min(512, S)
    ts = min(256, S)
    tk = min(2048, S)
    w_att = w_in[:, :3 * ATT_W].reshape(D_MODEL, 3, ATT_HEADS, DH).transpose(0, 2, 1, 3).reshape(D_MODEL, 3 * ATT_W)
    w_rest = w_in[:, 3 * ATT_W:]
    w_att_t, w_rest_t = w_att.T, w_rest.T
    gq3 = q_norm_g.reshape(ATT_HEADS, 1, DH)
    gk3 = k_norm_g.reshape(ATT_HEADS, 1, DH)
    slopes3 = jnp.asarray(np.broadcast_to(ALIBI[:, None, None], (ATT_HEADS, 1, DH)).copy())

    xn = _rmsnorm_fwd(x, norm1_g, tm)
    z_att = _mm("in_proj_att", xn, w_att, "nn", ts, 3 * ATT_W, D_MODEL,
                [(_sds((S, 3 * ATT_W), F32), _tile_ij(ts, 3 * ATT_W))], _epi_store(F32))[0]
    (z_rest,), gathered = _mm("in_proj_rest", xn, w_rest, "nn", ts, REST_W, D_MODEL,
                              [(_sds((S, REST_W), BF16), _tile_ij(ts, REST_W))], _epi_store(BF16),
                              exchange=_gather_exchange(later_shards))
    w_pa, w_pb, w_out, w_up, w_down = [f(g) for f, g in zip((_cols, _rows_of, _rows_of, _cols, _rows_of), gathered)]
    w_up_t, w_down_t, w_out_t, w_pa_t, w_pb_t = w_up.T, w_down.T, w_out.T, w_pa.T, w_pb.T
    done = []
    for gi, ((_, d), nb) in enumerate(zip(ATT_GROUPS, ATT_BLOCKS_PER_STEP)):
        last = gi == len(ATT_GROUPS) - 1
        done.append(_att_fwd_group(z_att, gq3, gk3, slopes3, gi, d, nb, others=tuple(done) if last else ()))
    o_a, lmix = done[-1]
    o_r, o_pre, states = _ret_fwd(z_rest, gn_g, gn_b)
    x1, y, pa, pb, xn2 = _merge_fwd(o_a, o_r, z_rest, x, w_pa, w_pb, w_out, norm2_g, min(256, S))

    def epi_up(acc, ex, out):
        r = jnp.maximum(acc, 0.0)
        out[0][...] = (r * r).astype(BF16)
        out[1][...] = r.astype(BF16)

    h, relu_u = _mm("mlp_up", xn2, w_up, "nn", ts, D_FF, D_MODEL,
                    [(_sds((S, D_FF), BF16), _tile_ij(ts, D_FF)), (_sds((S, D_FF), BF16), _tile_ij(ts, D_FF))], epi_up)

    def epi_down(acc, ex, out):
        diff = ex[0][...] + acc - ex[1][...]
        out[0][...] = diff * (1.0 / D_MODEL)
        out[1][...] = jnp.broadcast_to(jnp.sum(diff * diff) * (1.0 / (8 * LANES)), (8, LANES))

    row_tile = _tile_ij(tm, D_MODEL)
    dx2, loss_parts = _mm(
        "mlp_down_loss", h, w_down, "nn", tm, D_MODEL, D_FF,
        [(_sds((S, D_MODEL), F32), row_tile),
         (_sds((S // tm * 8, LANES), F32), pl.BlockSpec((8, LANES), lambda i, j, k: (i, 0)))],
        epi_down, extras=[(x1, row_tile), (target, row_tile)])
    loss_local = jnp.sum(loss_parts) * (0.5 / D_MODEL)

    def epi_du(acc, ex, out):
        out[0][...] = (acc * (2.0 * ex[0][...].astype(F32))).astype(BF16)

    du = _mm("mlp_down_bwd", dx2, w_down_t, "nn", ts, D_FF, D_MODEL,
             [(_sds((S, D_FF), BF16), _tile_ij(ts, D_FF))], epi_du, extras=[(relu_u, _tile_ij(ts, D_FF))])[0]
    gw_down = _mm("gw_down", h, dx2, "tn", 1024, D_MODEL, tk,
                  [(_sds((D_FF, D_MODEL), BF16), _tile_ij(1024, D_MODEL))], _epi_store(BF16))[0]
    gw_up = _mm("gw_up", xn2, du, "tn", D_MODEL, 512, tk,
                [(_sds((N_DEV, D_MODEL, 512), BF16), pl.BlockSpec((None, D_MODEL, 512), lambda i, j, k: (j, 0, 0)))],
                _epi_store(BF16))[0]

    vec = pl.BlockSpec((1, D_MODEL), lambda i, j, k: (0, 0))
    seq_sem = ("arbitrary", "arbitrary", "arbitrary")

    def epi_norm2(acc, ex, out):
        dx, dg = _rms_bwd(acc, ex[0][...], ex[2][...])
        out[0][...] = ex[1][...] + dx
        _acc_rows(out[1], dg, pl.program_id(0) == 0)

    dx1, g_norm2 = _mm(
        "mlp_up_bwd", du, w_up_t, "nn", tm, D_MODEL, D_FF,
        [(_sds((S, D_MODEL), F32), row_tile), (_sds((1, D_MODEL), F32), vec)],
        epi_norm2, extras=[(x1, row_tile), (dx2, row_tile), (norm2_g, vec)], sem=seq_sem)

    def epi_dy(acc, ex, out):
        sa = jax.nn.sigmoid(ex[0][...].astype(F32))
        sb = jax.nn.sigmoid(ex[1][...].astype(F32))
        out[0][...] = (acc * sa).astype(BF16)
        out[1][...] = (acc * sb).astype(BF16)
        out[2][:, :D_MODEL] = (acc * ex[2][...].astype(F32) * (sa * (1.0 - sa))).astype(BF16)
        out[2][:, D_MODEL:] = (acc * ex[3][...].astype(F32) * (sb * (1.0 - sb))).astype(BF16)

    ga_spec = pl.BlockSpec((tm, D_MODEL), lambda i, j, k: (i, R_GA // D_MODEL))
    gb_spec = pl.BlockSpec((tm, D_MODEL), lambda i, j, k: (i, R_GB // D_MODEL))
    gates_spec = pl.BlockSpec((tm, 2 * D_MODEL), lambda i, j, k: (i, R_GA // (2 * D_MODEL)))
    dpa, dpb, dz_rest = _mm(
        "out_proj_bwd", dx1, w_out_t, "nn", tm, D_MODEL, D_MODEL,
        [(_sds((S, D_MODEL), BF16), row_tile), (_sds((S, D_MODEL), BF16), row_tile), (_sds((S, REST_W), BF16), gates_spec)],
        epi_dy, extras=[(z_rest, ga_spec), (z_rest, gb_spec), (pa, row_tile), (pb, row_tile)])
    gw_out = _mm("gw_out", y, dx1, "tn", D_MODEL, D_MODEL, tk,
                 [(_sds((D_MODEL, D_MODEL), BF16), _tile_ij(D_MODEL, D_MODEL))], _epi_store(BF16))[0]
    gw_pa = _mm("gw_proj_a", o_a, dpa, "tn", ATT_OUT_W, D_MODEL, tk,
                [(_sds((ATT_OUT_W, D_MODEL), BF16), _tile_ij(ATT_OUT_W, D_MODEL))], _epi_store(BF16))[0]
    gw_pb = _mm("gw_proj_b", o_r, dpb, "tn", 1024, D_MODEL, tk,
                [(_sds((RET_V_W, D_MODEL), BF16), _tile_ij(1024, D_MODEL))], _epi_store(BF16))[0]

    def epi_doa(acc, ex, out):
        out[0][...] = acc
        prod = acc * ex[0][...]
        out[1][...] = jnp.concatenate(
            [jnp.broadcast_to(jnp.sum(prod[:, j * DH:(j + 1) * DH], axis=-1, keepdims=True), (prod.shape[0], DH))
             for j in range(HPG)], axis=1)

    slot_tile = _tile_ij(tm, ATT_OUT_W)
    do_a, delta = _mm("proj_a_bwd", dpa, w_pa_t, "nn", tm, ATT_OUT_W, D_MODEL,
                      [(_sds((S, ATT_OUT_W), F32), slot_tile), (_sds((S, ATT_OUT_W), F32), slot_tile)],
                      epi_doa, extras=[(o_a, slot_tile)])
    owed = [_chip_core(_by_owner_cols(gw_pa)), _chip_core(gw_pb.reshape(N_DEV, -1, D_MODEL)),
            _chip_core(gw_out.reshape(N_DEV, -1, D_MODEL)), _chip_core(gw_up), _chip_core(gw_down.reshape(N_DEV, -1, D_MODEL))]
    names = ("w_proj_a", "w_proj_b", "w_out", "w_up", "w_down")

    def epi_dor(acc, ex, out):
        dgs, dbs = [], []
        for h in range(RET_HEADS):
            cv = slice(h * RET_V, (h + 1) * RET_V)
            do, dgr, dg, db = _norm_gate_bwd(acc[:, cv], ex[0][:, cv], ex[1][:, cv].astype(F32), ex[2][:, cv], ex[3][:, cv])
            out[0][:, cv] = do.astype(BF16)
            out[1][:, cv] = dgr.astype(BF16)
            dgs.append(dg)
            dbs.append(db)
        first = pl.program_id(0) == 0
        _acc_rows(out[2], jnp.concatenate(dgs, axis=1), first)
        _acc_rows(out[3], jnp.concatenate(dbs, axis=1), first)

    wide_tile = _tile_ij(ts, RET_V_W)
    gate_tile = pl.BlockSpec((ts, RET_V_W), lambda i, j, k: (i, R_G // RET_V_W))
    wide_vec = pl.BlockSpec((1, RET_V_W), lambda i, j, k: (0, 0))
    (do_ret, dz_rest, g_gn_g, g_gn_b), got = _mm(
        "proj_b_bwd", dpb, w_pb_t, "nn", ts, RET_V_W, D_MODEL,
        [(_sds((S, RET_V_W), BF16), wide_tile), (_sds((S, REST_W), BF16), gate_tile),
         (_sds((1, RET_V_W), F32), wide_vec), (_sds((1, RET_V_W), F32), wide_vec)],
        epi_dor, extras=[(o_pre, wide_tile), (z_rest, gate_tile), (gn_g, wide_vec), (gn_b, wide_vec)],
        exchange=_pair_exchange(owed), in_place=(dz_rest, 1))
    chip_sums = [_pair_sum(f"pair_sum_{n}", g, r, core, min(256, g.shape[2])) for n, g, r in zip(names, owed, got)]

    dz_rest = _ret_bwd(do_ret, states, z_rest, dz_rest)
    dz_att, gq_parts, gk_parts, parts_late = None, [], [], None
    for gi, ((_, d), nb) in enumerate(zip(ATT_GROUPS, ATT_BLOCKS_PER_STEP)):
        last = gi == len(ATT_GROUPS) - 1
        (dz_att, gq_p, gk_p), parts = _att_bwd_group(z_att, dz_att, do_a, delta, lmix, gq3, gk3, slopes3, gi, d, nb,
                                                     exchange=_chip_exchange(chip_sums) if last else None)
        parts_late = parts if last else parts_late
        gq_parts.append(gq_p)
        gk_parts.append(gk_p)
    g_qn = jnp.concatenate(gq_parts, axis=0).reshape(1, ATT_HEADS, DH)
    g_kn = jnp.concatenate(gk_parts, axis=0).reshape(1, ATT_HEADS, DH)

    gw_att = _mm("gw_in_att", xn, dz_att, "tn", D_MODEL, ATT_W, tk,
                 [(_sds((D_MODEL, 3 * ATT_W), BF16), _tile_ij(D_MODEL, ATT_W))], _epi_store(BF16))[0]
    gw_rest = _mm("gw_in_rest", xn, dz_rest, "tn", D_MODEL, 1024, tk,
                  [(_sds((D_MODEL, REST_W), BF16), _tile_ij(D_MODEL, 1024))], _epi_store(BF16))[0]
    gw_att = gw_att.reshape(D_MODEL, ATT_HEADS, 3, DH).transpose(0, 2, 1, 3).reshape(D_MODEL, 3 * ATT_W)
    gw_in = jnp.concatenate([gw_att, gw_rest], axis=1)
    owed_in = _chip_core(_by_owner_cols(gw_in))
    (dxn_att,), (got_in,) = _mm("in_proj_att_bwd", dz_att, w_att_t, "nn", tm, D_MODEL, 3 * ATT_W,
                                [(_sds((S, D_MODEL), F32), row_tile)], _epi_store(F32), exchange=_pair_exchange([owed_in]))
    chip_sum_in = _pair_sum("pair_sum_w_in", owed_in, got_in, core, min(256, owed_in.shape[2]))

    def epi_norm1(acc, ex, out):
        dx, dg = _rms_bwd(acc + ex[0][...], ex[1][...], ex[3][...])
        out[0][...] = ex[2][...] + dx
        _acc_rows(out[1], dg, pl.program_id(0) == 0)

    short_tile = _tile_ij(ts, D_MODEL)
    (grad_x, g_norm1), parts_in = _mm(
        "in_proj_rest_bwd", dz_rest, w_rest_t, "nn", ts, D_MODEL, REST_W,
        [(_sds((S, D_MODEL), F32), short_tile), (_sds((1, D_MODEL), F32), vec)],
        epi_norm1, extras=[(dxn_att, short_tile), (x, short_tile), (dx1, short_tile), (norm1_g, vec)],
        exchange=_chip_exchange([chip_sum_in]))

    small = (g_norm1, g_qn, g_kn, g_gn_g, g_gn_b, g_norm2)
    return loss_local, grad_x, list(parts_in) + list(parts_late), small


def _position():
    return lax.axis_index("x"), lax.axis_index("y"), lax.axis_index("c")


def _other_chips(x, y):
    return [(1 - x, y), (x, 1 - y), (1 - x, 1 - y)]


_ANY = pl.BlockSpec(memory_space=pl.ANY)


def _gather_exchange(shards):
    nw = len(shards)

    def plan(x_refs, out_refs, sems):
        send_sems, recv_sems, local_sems = sems
        x, y, c = _position()
        me, sibling = (x, y, c), (x, y, 1 - c)
        chips = _other_chips(x, y)

        def copy(w, k, block, to, own=False):
            px, py, pc = block
            rows = out_refs[w].at[4 * px + 2 * py + pc]
            return pltpu.make_async_remote_copy(
                src_ref=x_refs[w] if own else rows, dst_ref=rows,
                send_sem=send_sems.at[7 * w + k], recv_sem=recv_sems.at[7 * w + k], device_id=to, device_id_type=MESH)

        def mine(w):
            return pltpu.make_async_copy(x_refs[w], out_refs[w].at[4 * x + 2 * y + c], local_sems.at[w])

        def own_sends(w):
            return [copy(w, 0, me, sibling, own=True)] + [copy(w, 1 + j, me, (*chip, c), own=True)
                                                          for j, chip in enumerate(chips)]

        def start():
            for w in range(nw):
                mine(w).start()
                for cp in own_sends(w):
                    cp.start()

        def relay():
            for j, chip in enumerate(chips):
                for w in range(nw):
                    copy(w, 1 + j, (*chip, c), me).wait_recv()
                    copy(w, 4 + j, (*chip, c), sibling).start()

        def finish():
            for w in range(nw):
                copy(w, 0, sibling, me).wait_recv()
                for j, chip in enumerate(chips):
                    copy(w, 4 + j, (*chip, 1 - c), me).wait_recv()
            for w in range(nw):
                for cp in own_sends(w):
                    cp.wait_send()
                for j, chip in enumerate(chips):
                    copy(w, 4 + j, (*chip, c), sibling).wait_send()
                mine(w).wait()

        return start, relay, finish

    return _Exchange(shards, [_sds((N_DEV,) + s.shape, s.dtype) for s in shards],
                     [pltpu.SemaphoreType.DMA((7 * nw,)), pltpu.SemaphoreType.DMA((7 * nw,)),
                      pltpu.SemaphoreType.DMA((nw,))], plan)


def _run_exchange(name, exchange):
    n_in, n_out = len(exchange.ins), len(exchange.out_shapes)

    def body(*refs):
        start, relay, finish = exchange.split(refs[:n_in], refs[n_in:n_in + n_out], refs[n_in + n_out:])
        start()
        if relay is not None:
            relay()
        finish()

    return pl.pallas_call(
        body, name=name, out_shape=exchange.out_shapes,
        in_specs=[_ANY] * n_in, out_specs=[_ANY] * n_out, scratch_shapes=exchange.sems,
    )(*exchange.ins)


def _pair_exchange(grads):
    ng = len(grads)

    def plan(g_refs, out_refs, sems):
        send_sems, recv_sems = sems
        x, y, c = _position()

        def copies():
            return [pltpu.make_async_remote_copy(
                src_ref=g_refs[w].at[:, 1 - c], dst_ref=out_refs[w], send_sem=send_sems.at[w],
                recv_sem=recv_sems.at[w], device_id=(x, y, 1 - c), device_id_type=MESH) for w in range(ng)]

        def start():
            for cp in copies():
                cp.start()

        def finish():
            for cp in copies():
                cp.wait()

        return start, None, finish

    return _Exchange(grads, [_sds((N_CHIP,) + g.shape[2:], g.dtype) for g in grads],
                     [pltpu.SemaphoreType.DMA((ng,)), pltpu.SemaphoreType.DMA((ng,))], plan)


def _small_all_gather(small):
    def plan(in_refs, out_refs, sems):
        (s_ref,), (s_out,) = in_refs, out_refs
        send_sems, recv_sems, local_sem = sems
        x, y, c = _position()
        me_id = 4 * x + 2 * y + c
        flips = [(a, b, e) for a in (0, 1) for b in (0, 1) for e in (0, 1)][1:]
        peers = [(x ^ a, y ^ b, c ^ e) for a, b, e in flips]

        def start():
            pltpu.make_async_copy(s_ref, s_out.at[me_id], local_sem).start()
            for k, p in enumerate(peers):
                pltpu.make_async_remote_copy(
                    src_ref=s_ref, dst_ref=s_out.at[me_id], send_sem=send_sems.at[k], recv_sem=recv_sems.at[k],
                    device_id=p, device_id_type=MESH).start()

        def finish():
            for k, (px, py, pc) in enumerate(peers):
                pltpu.make_async_remote_copy(
                    src_ref=s_ref, dst_ref=s_out.at[4 * px + 2 * py + pc], send_sem=send_sems.at[k],
                    recv_sem=recv_sems.at[k], device_id=(px, py, pc), device_id_type=MESH).wait()
            pltpu.make_async_copy(s_ref, s_out.at[me_id], local_sem).wait()

        return start, None, finish

    return _run_exchange("small_grad_all_gather", _Exchange(
        [small], [_sds((N_DEV,) + small.shape, small.dtype)],
        [pltpu.SemaphoreType.DMA((7,)), pltpu.SemaphoreType.DMA((7,)), pltpu.SemaphoreType.DMA], plan))[0]


def _pair_sum(name, g, got, core, tr):
    n_chip, _, R, C = g.shape

    def body(c_ref, a_ref, b_ref, o_ref):
        del c_ref
        o_ref[...] = (a_ref[...].astype(F32) + b_ref[...].astype(F32)).astype(o_ref.dtype)

    return pl.pallas_call(
        body, name=name,
        grid_spec=pltpu.PrefetchScalarGridSpec(
            num_scalar_prefetch=1, grid=(n_chip, R // tr),
            in_specs=[pl.BlockSpec((None, None, tr, C), lambda ch, i, c_ref: (ch, c_ref[0], i, 0)),
                      pl.BlockSpec((None, tr, C), lambda ch, i, c_ref: (ch, i, 0))],
            out_specs=pl.BlockSpec((None, tr, C), lambda ch, i, c_ref: (ch, i, 0))),
        out_shape=_sds(got.shape, got.dtype),
        compiler_params=_cparams(("parallel", "parallel")),
    )(core, g, got)


def _chip_exchange(parts):
    ng = len(parts)

    def plan(p_refs, out_refs, sems):
        send_sems, recv_sems, local_sems = sems
        x, y, c = _position()
        my_chip = 2 * x + y

        def copies():
            local = [pltpu.make_async_copy(p_refs[w].at[my_chip], out_refs[w].at[my_chip], local_sems.at[w])
                     for w in range(ng)]
            remote = [pltpu.make_async_remote_copy(
                src_ref=p_refs[w].at[2 * cx + cy], dst_ref=out_refs[w].at[my_chip],
                send_sem=send_sems.at[3 * w + k], recv_sem=recv_sems.at[3 * w + k],
                device_id=(cx, cy, c), device_id_type=MESH)
                for w in range(ng) for k, (cx, cy) in enumerate(_other_chips(x, y))]
            return local + remote

        def start():
            for cp in copies():
                cp.start()

        def finish():
            for cp in copies():
                cp.wait()

        return start, None, finish

    return _Exchange(parts, [_sds(p.shape, p.dtype) for p in parts],
                     [pltpu.SemaphoreType.DMA((3 * ng,)), pltpu.SemaphoreType.DMA((3 * ng,)),
                      pltpu.SemaphoreType.DMA((ng,))], plan)


def _adamw(name, parts, w, m, v, tr):
    n_parts = parts.shape[0]
    R, C = w.shape

    def body(p_ref, w_ref, m_ref, v_ref, g_ref, d_ref, mo_ref, vo_ref):
        g = p_ref[0].astype(F32)
        for i in range(1, n_parts):
            g = g + p_ref[i].astype(F32)
        m_new = ADAM_B1 * m_ref[...] + (1.0 - ADAM_B1) * g
        v_new = ADAM_B2 * v_ref[...] + (1.0 - ADAM_B2) * (g * g)
        m_hat = m_new / (1.0 - ADAM_B1 ** ADAM_STEP)
        v_hat = v_new / (1.0 - ADAM_B2 ** ADAM_STEP)
        g_ref[...] = g
        d_ref[...] = -ADAM_LR * (m_hat / (jnp.sqrt(v_hat) + ADAM_EPS) + ADAM_WD * w_ref[...])
        mo_ref[...] = m_new
        vo_ref[...] = v_new

    tile = pl.BlockSpec((tr, C), lambda i: (i, 0))
    return pl.pallas_call(
        body, name=name, grid=(R // tr,),
        in_specs=[pl.BlockSpec((n_parts, tr, C), lambda i: (0, i, 0)), tile, tile, tile],
        out_specs=[tile] * 4,
        out_shape=[_sds((R, C), F32)] * 4,
        compiler_params=_cparams(("parallel",)),
    )(parts, w, m, v)


def _flat_small(arrs):
    return jnp.concatenate([a.reshape(-1) for a in arrs]).reshape(SMALL_ROWS, LANES)


def _cols(g):
    return g.transpose(1, 0, 2).reshape(g.shape[1], -1)


def _rows_of(g):
    return g.reshape(-1, g.shape[2])


def _by_owner_cols(g):
    rows, cols = g.shape
    return g.reshape(rows, N_DEV, cols // N_DEV).transpose(1, 0, 2)


def _chip_core(g):
    return g.reshape((N_CHIP, 2) + g.shape[1:])


def kernel(x, norm1_g, w_in, q_norm_g, k_norm_g, ret_gn_g, ret_gn_b, w_proj_a, w_proj_b, w_out, norm2_g, w_up, w_down, loss_target, m_norm1_g, m_w_in, m_q_norm_g, m_k_norm_g, m_ret_gn_g, m_ret_gn_b, m_w_proj_a, m_w_proj_b, m_w_out, m_norm2_g, m_w_up, m_w_down, v_norm1_g, v_w_in, v_q_norm_g, v_k_norm_g, v_ret_gn_g, v_ret_gn_b, v_w_proj_a, v_w_proj_b, v_w_out, v_norm2_g, v_w_up, v_w_down):
    big_w = (w_in, w_proj_a, w_proj_b, w_out, w_up, w_down)
    big_m = (m_w_in, m_w_proj_a, m_w_proj_b, m_w_out, m_w_up, m_w_down)
    big_v = (v_w_in, v_w_proj_a, v_w_proj_b, v_w_out, v_w_up, v_w_down)
    small_w = (norm1_g, q_norm_g, k_norm_g, ret_gn_g, ret_gn_b, norm2_g)
    small_m = (m_norm1_g, m_q_norm_g, m_k_norm_g, m_ret_gn_g, m_ret_gn_b, m_norm2_g)
    small_v = (v_norm1_g, v_q_norm_g, v_k_norm_g, v_ret_gn_g, v_ret_gn_b, v_norm2_g)

    shards = [w[0].astype(BF16) for w in big_w]
    (g_in,) = _run_exchange("all_gather_w_in", _gather_exchange(shards[:1]))
    core = lax.axis_index("c").astype(jnp.int32).reshape(1)
    loss_local, grad_x, parts, small_g = _step(
        x[0], loss_target[0], norm1_g, q_norm_g[0], k_norm_g[0], ret_gn_g, ret_gn_b, norm2_g, _cols(g_in), shards[1:], core)
    small_all = _small_all_gather(_flat_small(small_g))
    names = ("w_in", "w_proj_a", "w_proj_b", "w_out", "w_up", "w_down")

    res = {}
    for n, p, w, m, v in zip(names, parts, big_w, big_m, big_v):
        outs = _adamw(f"adamw_{n}", p, w[0], m[0], v[0], min(128, w.shape[1]))
        res[n] = [o[None] for o in outs]
    s_outs = _adamw("adamw_small", small_all, _flat_small(small_w), _flat_small(small_m), _flat_small(small_v), SMALL_ROWS)
    small_names = ("norm1_g", "q_norm_g", "k_norm_g", "ret_gn_g", "ret_gn_b", "norm2_g")
    for n in small_names:
        res[n] = []
    for o in s_outs:
        flat, off = o.reshape(-1), 0
        for n, w in zip(small_names, small_w):
            res[n].append(flat[off:off + w.size].reshape(w.shape))
            off += w.size

    order = ("norm1_g", "w_in", "q_norm_g", "k_norm_g", "ret_gn_g", "ret_gn_b", "w_proj_a", "w_proj_b", "w_out",
             "norm2_g", "w_up", "w_down")
    loss = lax.psum(loss_local, MESH_AXES)
    return (loss, grad_x[None], *[res[n][0] for n in order], *[res[n][1] for n in order],
            *[res[n][2] for n in order], *[res[n][3] for n in order])
```

```python
import math

import numpy as np
import jax
import jax.numpy as jnp
from jax import lax
from jax.experimental import pallas as pl
from jax.experimental.pallas import tpu as pltpu

F32 = jnp.float32
BF16 = jnp.bfloat16

D_MODEL = 1024
ATT_GROUPS = ((128, 1), (512, 4), (2048, 16))
ATT_BLOCKS_PER_STEP = (8, 1, 1)
ATT_TOGETHER = 4
HPG = 4
ATT_HEADS = 12
DH = 128
BLK = 128
ATT_W = ATT_HEADS * DH
ATT_OUT_W = HPG * DH
RET_HEADS = 4
RET_QK = 256
RET_V = 512
RET_QK_W = RET_HEADS * RET_QK
RET_V_W = RET_HEADS * RET_V
CHUNK = 128
D_FF = 4096
IN_W = 12800
REST_W = IN_W - 3 * ATT_W
EPS = 1e-6
ADAM_LR, ADAM_B1, ADAM_B2, ADAM_EPS, ADAM_WD, ADAM_STEP = 0.001, 0.9, 0.999, 1e-08, 0.01, 10
N_DEV = 8
N_CHIP = 4
MESH_AXES = ("x", "y", "c")
MESH = pl.DeviceIdType.MESH
VMEM_LIMIT = 56 * 1024 * 1024
LANES = 128
NEG = -1e30

_NN = (((1,), (0,)), ((), ()))
_NT = (((1,), (1,)), ((), ()))
_TN = (((0,), (0,)), ((), ()))

R_Q, R_K, R_V, R_G, R_GA, R_GB = 0, 1024, 2048, 4096, 6144, 7168

LOG_GAMMA = [float(v) for v in np.log(1.0 - 2.0 ** (-5.0 - np.arange(RET_HEADS, dtype=np.float32))).astype(np.float32)]
ALIBI = np.asarray(2.0 ** (-8.0 * np.arange(1, ATT_HEADS + 1, dtype=np.float32) / ATT_HEADS), np.float32)

SMALL_ROWS = (1024 + 1536 + 1536 + 2048 + 2048 + 1024) // LANES


def _dot(a, b, dims=_NN):
    return lax.dot_general(a, b, dims, preferred_element_type=F32)


def _cparams(sem):
    return pltpu.CompilerParams(dimension_semantics=sem, vmem_limit_bytes=VMEM_LIMIT)


def _sds(shape, dtype):
    return jax.ShapeDtypeStruct(shape, dtype)


class _Exchange:
    def __init__(self, ins, out_shapes, sems, plan):
        self.ins, self.out_shapes, self.sems, self.plan = list(ins), list(out_shapes), list(sems), plan

    def split(self, in_refs, out_refs, sem_refs):
        return self.plan(in_refs, out_refs, sem_refs)


def _mm(name, a, b, mode, tm, tn, tk, outs, epi, extras=(), b_pro=None,
        sem=("parallel", "parallel", "arbitrary"), exchange=None, in_place=None):
    if mode == "nn":
        (M, K), (_, N) = a.shape, b.shape
        a_spec = pl.BlockSpec((tm, tk), lambda i, j, k: (i, k))
        dims = _NN
    else:
        (K, M), (_, N) = a.shape, b.shape
        a_spec = pl.BlockSpec((tk, tm), lambda i, j, k: (k, i))
        dims = _TN
    assert M % tm == 0 and N % tn == 0 and K % tk == 0, (name, M, N, K, tm, tn, tk)
    nk = K // tk
    whole_b = dict(pipeline_mode=pl.Buffered(1)) if (nk == 1 and N == tn) else {}
    b_spec = pl.BlockSpec((tk, tn), lambda i, j, k: (k, j), **whole_b)
    n_ex, n_out = len(extras), len(outs)
    grid = (M // tm, N // tn, nk)
    n_xi = len(exchange.ins) if exchange else 0
    n_xo = len(exchange.out_shapes) if exchange else 0
    n_acc = 1 if nk > 1 else 0

    n_ip = 1 if in_place else 0

    def body(a_ref, b_ref, *rest):
        ex, rest = rest[:n_ex], rest[n_ex:]
        x_in, rest = rest[:n_xi], rest[n_xi + n_ip:]
        out, rest = rest[:n_out], rest[n_out:]
        x_out, rest = rest[:n_xo], rest[n_xo:]
        step =(pl.program_id(0) * grid[1] + pl.program_id(1)) * grid[2] + pl.program_id(2)
        n_steps = grid[0] * grid[1] * grid[2]
        if exchange:
            start, relay, finish = exchange.split(x_in, x_out, rest[n_acc:])
            pl.when(step == 0)(start)
        bv = b_ref[...]
        if b_pro is not None:
            bv = b_pro(bv)
        part = _dot(a_ref[...].astype(BF16), bv.astype(BF16), dims)
        if nk == 1:
            epi(part, ex, out)
        else:
            acc_ref = rest[0]
            k = pl.program_id(2)

            @pl.when(k == 0)
            def _():
                acc_ref[...] = part

            @pl.when(k > 0)
            def _():
                acc_ref[...] += part

            @pl.when(k == nk - 1)
            def _():
                epi(acc_ref[...], ex, out)
        if exchange:
            if relay is not None:
                pl.when(step == (7 * n_steps) // 8)(relay)
            pl.when(step == n_steps - 1)(finish)

    res = pl.pallas_call(
        body,
        name=name,
        grid=grid,
        in_specs=[a_spec, b_spec] + [s for _, s in extras] + [_ANY] * (n_xi + n_ip),
        out_specs=[s for _, s in outs] + [_ANY] * n_xo,
        out_shape=[o for o, _ in outs] + (exchange.out_shapes if exchange else []),
        scratch_shapes=([pltpu.VMEM((tm, tn), F32)] if nk > 1 else []) + (exchange.sems if exchange else []),
        input_output_aliases={2 + n_ex + n_xi: in_place[1]} if in_place else {},
        compiler_params=_cparams(("arbitrary",) * 3 if exchange else sem),
    )(a, b, *[e for e, _ in extras], *(exchange.ins if exchange else []), *([in_place[0]] if in_place else []))
    return (res[:n_out], res[n_out:]) if exchange else res


def _tile_ij(tm, tn):
    return pl.BlockSpec((tm, tn), lambda i, j, k: (i, j))


def _epi_store(dtype):
    def epi(acc, ex, out):
        out[0][...] = acc.astype(dtype)
    return epi


def _rms_rows(x):
    return lax.rsqrt(jnp.mean(x * x, axis=-1, keepdims=True) + EPS)


def _acc_rows(ref, part, first):
    @pl.when(first)
    def _():
        ref[...] = part

    @pl.when(jnp.logical_not(first))
    def _():
        ref[...] += part


def _rmsnorm_fwd(x, g, tm, exchange):
    S, Dm = x.shape
    n_steps = S // tm
    n_xi, n_xo = len(exchange.ins), len(exchange.out_shapes)

    def body(x_ref, g_ref, *rest):
        x_in, o_ref, x_out, sems = rest[:n_xi], rest[n_xi], rest[n_xi + 1:n_xi + 1 + n_xo], rest[n_xi + 1 + n_xo:]
        start, relay, finish = exchange.split(x_in, x_out, sems)
        step = pl.program_id(0)
        pl.when(step == 0)(start)
        xv = x_ref[...]
        o_ref[...] = (xv * _rms_rows(xv) * g_ref[...]).astype(BF16)

        @pl.when(step == n_steps - 1)
        def _():
            relay()
            finish()

    res = pl.pallas_call(
        body, name="rmsnorm1_fwd", grid=(n_steps,),
        in_specs=[pl.BlockSpec((tm, Dm), lambda i: (i, 0)), pl.BlockSpec((1, Dm), lambda i: (0, 0))] + [_ANY] * n_xi,
        out_specs=[pl.BlockSpec((tm, Dm), lambda i: (i, 0))] + [_ANY] * n_xo,
        out_shape=[_sds((S, Dm), BF16)] + exchange.out_shapes,
        scratch_shapes=exchange.sems,
        compiler_params=_cparams(("arbitrary",)),
    )(x, g, *exchange.ins)
    return res[0], res[1:]


def _rows(ref, r, b, d):
    if d == 1:
        return ref[b * BLK:(b + 1) * BLK, :]
    return ref[pl.ds(b * BLK * d + r, BLK, stride=d), :]


def _put_rows(ref, r, b, d, val):
    if d == 1:
        ref[b * BLK:(b + 1) * BLK, :] = val
    else:
        ref[pl.ds(b * BLK * d + r, BLK, stride=d), :] = val


def _head_norm(x, g):
    r = _rms_rows(x)
    xh = x * r
    return (xh * g).astype(BF16), xh, r


def _head_norm_bwd(dyn, xh, r, g):
    dxh = dyn * g
    dx = r * (dxh - xh * jnp.mean(dxh * xh, axis=-1, keepdims=True))
    return dx, jnp.sum(dyn * xh, axis=0, keepdims=True)


def _att_mask_bias(slope, d, first):
    qi = lax.broadcasted_iota(jnp.int32, (BLK, 2 * BLK), 0)
    kj = lax.broadcasted_iota(jnp.int32, (BLK, 2 * BLK), 1)
    dist = BLK + qi - kj
    valid = (dist >= 0) & (dist <= BLK)
    if first is not None:
        valid = valid & (jnp.logical_not(first) | (kj >= BLK))
    bias = -slope * (dist * d).astype(F32)
    return valid, bias


def _att_specs(gi, d, nb, S):
    span = BLK * d
    sb = span * nb
    nspan = S // span
    before = lambda n: jnp.maximum(n * nb - 1, 0)
    after = lambda n: jnp.minimum((n + 1) * nb, nspan - 1)
    zcol = lambda j, kind: 3 * (gi * HPG + j) + kind
    cur = lambda kind: pl.BlockSpec((sb, DH), lambda j, n: (n, zcol(j, kind)))
    prev = lambda kind: pl.BlockSpec((span, DH), lambda j, n: (before(n), zcol(j, kind)))
    nxt = lambda kind: pl.BlockSpec((span, DH), lambda j, n: (after(n), zcol(j, kind)))
    slot = pl.BlockSpec((sb, DH), lambda j, n: (n, j))
    slot_next = pl.BlockSpec((span, DH), lambda j, n: (after(n), j))
    head = pl.BlockSpec((None, 1, DH), lambda j, n: (gi * HPG + j, 0, 0))
    return cur, prev, nxt, slot, slot_next, head


def _att_fwd_group(z_att, gq3, gk3, slopes3, gi, d, nb, others=()):
    S = z_att.shape[0]
    nsb = S // (BLK * d * nb)
    scale = DH ** -0.5
    n_other = len(others)

    def body(q_ref, k_ref, v_ref, kp_ref, vp_ref, gq_ref, gk_ref, sl_ref, *rest):
        other_refs, (o_ref, l_ref) = rest[:2 * n_other], rest[2 * n_other:]
        slope = sl_ref[...][:, :1]
        valid0, bias = _att_mask_bias(slope, d, pl.program_id(1) == 0)
        valid_in, _ = _att_mask_bias(slope, d, None)
        gq, gk = gq_ref[...], gk_ref[...]
        memo = {}

        def get(kind, r, b):
            if (kind, r, b) not in memo:
                if kind == "k":
                    val = _head_norm(_rows(k_ref if b >= 0 else kp_ref, r, max(b, 0), d), gk)[0]
                else:
                    val = _rows(v_ref if b >= 0 else vp_ref, r, max(b, 0), d).astype(BF16)
                memo[(kind, r, b)] = val
            return memo[(kind, r, b)]

        units = [(r, b) for r in range(d) for b in range(nb)]
        for c0 in range(0, len(units), ATT_TOGETHER):
            us = units[c0:c0 + ATT_TOGETHER]
            q = [_head_norm(_rows(q_ref, r, b, d), gq)[0] for r, b in us]
            k2 = [jnp.concatenate([get("k", r, b - 1), get("k", r, b)], axis=0) for r, b in us]
            v2 = [jnp.concatenate([get("v", r, b - 1), get("v", r, b)], axis=0) for r, b in us]
            s = [jnp.where(valid0 if b == 0 else valid_in, _dot(q[i], k2[i], _NT) * scale + bias, NEG)
                 for i, (r, b) in enumerate(us)]
            m = [jnp.max(si, axis=-1, keepdims=True) for si in s]
            p = [jnp.exp(si - mi) for si, mi in zip(s, m)]
            den = [jnp.sum(pi, axis=-1, keepdims=True) for pi in p]
            o = [_dot(pi.astype(BF16), vi) / di for pi, vi, di in zip(p, v2, den)]
            for i, (r, b) in enumerate(us):
                _put_rows(o_ref, r, b, d, o[i])
                _put_rows(l_ref, r, b, d, jnp.broadcast_to(m[i] + jnp.log(den[i]), (BLK, DH)))
        if n_other:
            os_ = [ref[...] for ref in other_refs[:n_other]] + [o_ref[...]]
            ls_ = [ref[...] for ref in other_refs[n_other:]] + [l_ref[...]]
            m = ls_[0]
            for l in ls_[1:]:
                m = jnp.maximum(m, l)
            es = [jnp.exp(l - m) for l in ls_]
            tot, mix = es[0], es[0] * os_[0]
            for e, o in zip(es[1:], os_[1:]):
                tot, mix = tot + e, mix + e * o
            o_ref[...] = mix / tot
            l_ref[...] = m + jnp.log(tot)

    cur, prev, _, slot, _, head = _att_specs(gi, d, nb, S)
    return pl.pallas_call(
        body, name=f"att_fwd_g{gi}", grid=(HPG, nsb),
        in_specs=[cur(0), cur(1), cur(2), prev(1), prev(2), head, head, head] + [slot] * (2 * n_other),
        out_specs=[slot, slot],
        out_shape=[_sds((S, ATT_OUT_W), F32), _sds((S, ATT_OUT_W), F32)],
        compiler_params=_cparams(("parallel", "arbitrary")),
    )(z_att, z_att, z_att, z_att, z_att, gq3, gk3, slopes3, *[o for o, _ in others], *[l for _, l in others])


def _ret_tables(lg):
    ri = lax.broadcasted_iota(jnp.int32, (CHUNK, CHUNK), 0)
    ci = lax.broadcasted_iota(jnp.int32, (CHUNK, CHUNK), 1)
    diff = (ri - ci).astype(F32)
    decay = jnp.where(diff >= 0, jnp.exp(lg * jnp.maximum(diff, 0.0)), 0.0)
    idx = lax.broadcasted_iota(jnp.int32, (CHUNK, 1), 0).astype(F32)
    xi = jnp.exp(lg * (idx + 1.0))
    zeta = jnp.exp(lg * (CHUNK - 1.0 - idx))
    return decay, xi, zeta, math.exp(lg * CHUNK)


def _ret_specs(nch, rev):
    idx = (lambda n: nch - 1 - n) if rev else (lambda n: n)
    qk = lambda off: pl.BlockSpec((CHUNK, RET_QK_W), lambda n: (idx(n), off // RET_QK_W))
    vv = lambda off: pl.BlockSpec((CHUNK, RET_V_W), lambda n: (idx(n), off // RET_V_W))
    par = pl.BlockSpec((1, RET_V_W), lambda n: (0, 0))
    wide = pl.BlockSpec((CHUNK, RET_V_W), lambda n: (idx(n), 0))
    st = pl.BlockSpec((RET_HEADS, None, RET_QK, RET_V), lambda n: (0, idx(n), 0, 0))
    return qk, vv, par, wide, st


def _ret_fwd(z_rest, gn_g, gn_b):
    S = z_rest.shape[0]
    nch = S // CHUNK

    def body(q_ref, k_ref, v_ref, gr_ref, g_ref, b_ref, or_ref, o_ref, st_ref, state):
        @pl.when(pl.program_id(0) == 0)
        def _():
            state[...] = jnp.zeros_like(state)

        for h in range(RET_HEADS):
            decay, xi, zeta, gch = _ret_tables(LOG_GAMMA[h])
            cq = slice(h * RET_QK, (h + 1) * RET_QK)
            cv = slice(h * RET_V, (h + 1) * RET_V)
            q = q_ref[:, cq]
            kc32 = k_ref[:, cq].astype(F32) * (RET_QK ** -0.5)
            kc = kc32.astype(BF16)
            v = v_ref[:, cv]
            st = state[h]
            stb = st.astype(BF16)
            st_ref[h] = stb
            s = _dot(q, kc, _NT) * decay
            o = _dot(s.astype(BF16), v) + _dot(q, stb) * xi
            state[h] = st * gch + _dot((kc32 * zeta).astype(BF16), v, _TN)
            mu = jnp.mean(o, axis=-1, keepdims=True)
            cen = o - mu
            yh = cen * lax.rsqrt(jnp.mean(cen * cen, axis=-1, keepdims=True) + EPS)
            gr = gr_ref[:, cv].astype(F32)
            or_ref[:, cv] = ((yh * g_ref[:, cv] + b_ref[:, cv]) * (gr * jax.nn.sigmoid(gr))).astype(BF16)
            o_ref[:, cv] = o

    qk, vv, par, wide, st = _ret_specs(nch, False)
    return pl.pallas_call(
        body, name="ret_fwd", grid=(nch,),
        in_specs=[qk(R_Q), qk(R_K), vv(R_V), vv(R_G), par, par],
        out_specs=[wide, wide, st],
        out_shape=[_sds((S, RET_V_W), BF16), _sds((S, RET_V_W), F32), _sds((RET_HEADS, nch, RET_QK, RET_V), BF16)],
        scratch_shapes=[pltpu.VMEM((RET_HEADS, RET_QK, RET_V), F32)],
        compiler_params=_cparams(("arbitrary",)),
    )(z_rest, z_rest, z_rest, z_rest, gn_g, gn_b)


def _merge_fwd(o_a, o_r, z_rest, x, wpa, wpb, wout, g2, tm):
    S = x.shape[0]

    def body(oa_ref, or_ref, ga_ref, gb_ref, x_ref, wpa_ref, wpb_ref, wo_ref, g2_ref,
             x1_ref, y_ref, pa_ref, pb_ref, xn2_ref):
        pa = _dot(oa_ref[...].astype(BF16), wpa_ref[...])
        pb = _dot(or_ref[...], wpb_ref[...])
        y = jax.nn.sigmoid(ga_ref[...].astype(F32)) * pa + jax.nn.sigmoid(gb_ref[...].astype(F32)) * pb
        yb = y.astype(BF16)
        x1 = x_ref[...] + _dot(yb, wo_ref[...])
        x1_ref[...] = x1
        y_ref[...] = yb
        pa_ref[...] = pa.astype(BF16)
        pb_ref[...] = pb.astype(BF16)
        xn2_ref[...] = (x1 * _rms_rows(x1) * g2_ref[...]).astype(BF16)

    row = lambda w: pl.BlockSpec((tm, w), lambda i: (i, 0))
    full = lambda a: pl.BlockSpec(a.shape, lambda i: (0, 0))
    return pl.pallas_call(
        body, name="merge_fwd", grid=(S // tm,),
        in_specs=[row(ATT_OUT_W), row(RET_V_W),
                  pl.BlockSpec((tm, D_MODEL), lambda i: (i, R_GA // D_MODEL)),
                  pl.BlockSpec((tm, D_MODEL), lambda i: (i, R_GB // D_MODEL)),
                  row(D_MODEL), full(wpa), full(wpb), full(wout), full(g2)],
        out_specs=[row(D_MODEL)] * 5,
        out_shape=[_sds((S, D_MODEL), F32)] + [_sds((S, D_MODEL), BF16)] * 4,
        compiler_params=_cparams(("parallel",)),
    )(o_a, o_r, z_rest, z_rest, x, wpa, wpb, wout, g2)


def _rms_bwd(dy, xv, g):
    r = _rms_rows(xv)
    xh = xv * r
    dg = dy * g
    dx = r * (dg - xh * jnp.mean(dg * xh, axis=-1, keepdims=True))
    return dx, jnp.sum(dy * xh, axis=0, keepdims=True)


def _norm_gate_bwd(dout, o, gr, gam, bet):
    mean = lambda xs: [jnp.mean(x, axis=-1, keepdims=True) for x in xs]
    cen = [x - m for x, m in zip(o, mean(o))]
    rstd = [lax.rsqrt(v + EPS) for v in mean([c * c for c in cen])]
    yh = [c * r for c, r in zip(cen, rstd)]
    y = [a * g + b for a, g, b in zip(yh, gam, bet)]
    sg = [jax.nn.sigmoid(g) for g in gr]
    dy = [d * (g * s) for d, g, s in zip(dout, gr, sg)]
    dgr = [d * a * (s * (1.0 + g * (1.0 - s))) for d, a, s, g in zip(dout, y, sg, gr)]
    dyh = [d * g for d, g in zip(dy, gam)]
    m1, m2 = mean(dyh), mean([a * b for a, b in zip(dyh, yh)])
    do = [r * (d - a - h * b) for r, d, a, h, b in zip(rstd, dyh, m1, yh, m2)]
    dg = [jnp.sum(d * h, axis=0, keepdims=True) for d, h in zip(dy, yh)]
    db = [jnp.sum(d, axis=0, keepdims=True) for d in dy]
    return do, dgr, dg, db


def _ret_bwd(do, states, z_rest, dz_rest):
    S = z_rest.shape[0]
    nch = S // CHUNK

    def body(do_ref, st_ref, q_ref, k_ref, v_ref, dz_in, dz_ref, gst):
        del dz_in

        @pl.when(pl.program_id(0) == 0)
        def _():
            gst[...] = jnp.zeros_like(gst)

        for h in range(RET_HEADS):
            decay, xi, zeta, gch = _ret_tables(LOG_GAMMA[h])
            cq = slice(h * RET_QK, (h + 1) * RET_QK)
            cv = slice(h * RET_V, (h + 1) * RET_V)
            q = q_ref[:, cq]
            kc32 = k_ref[:, cq].astype(F32) * (RET_QK ** -0.5)
            kc = kc32.astype(BF16)
            v = v_ref[:, cv]
            dob = do_ref[:, cv]
            a = (_dot(q, kc, _NT) * decay).astype(BF16)
            da = (_dot(dob, v, _NT) * decay).astype(BF16)
            dcross = (dob.astype(F32) * xi).astype(BF16)
            g_next = gst[h]
            gb = g_next.astype(BF16)
            dq = _dot(da, kc) + _dot(dcross, st_ref[h], _NT)
            dkc = _dot(da, q, _TN)
            dkz = _dot(v, gb, _NT)
            dv = _dot(a, dob, _TN) + _dot((kc32 * zeta).astype(BF16), gb)
            gst[h] = g_next * gch + _dot(q, dcross, _TN)
            dz_ref[:, R_Q + h * RET_QK:R_Q + (h + 1) * RET_QK] = dq.astype(BF16)
            dz_ref[:, R_K + h * RET_QK:R_K + (h + 1) * RET_QK] = ((dkc + dkz * zeta) * (RET_QK ** -0.5)).astype(BF16)
            dz_ref[:, R_V + h * RET_V:R_V + (h + 1) * RET_V] = dv.astype(BF16)

    qk, vv, _, wide, st = _ret_specs(nch, True)
    return pl.pallas_call(
        body, name="ret_bwd", grid=(nch,),
        in_specs=[wide, st, qk(R_Q), qk(R_K), vv(R_V), _ANY],
        out_specs=pl.BlockSpec((CHUNK, R_G), lambda n: (nch - 1 - n, 0)),
        out_shape=_sds(dz_rest.shape, BF16),
        input_output_aliases={5: 0},
        scratch_shapes=[pltpu.VMEM((RET_HEADS, RET_QK, RET_V), F32)],
        compiler_params=_cparams(("arbitrary",)),
    )(do, states, z_rest, z_rest, z_rest, dz_rest)


def _att_probs(q, k, lmix, valid, bias):
    s = _dot(q, k, _NT) * (DH ** -0.5) + bias
    return jnp.where(valid, jnp.exp(jnp.where(valid, s, NEG) - lmix), 0.0)


def _att_bwd_group(z_att, dz_att, do_a, delta, lmix, gq3, gk3, slopes3, gi, d, nb, exchange=None):
    S = z_att.shape[0]
    sb = BLK * d * nb
    nsb = S // sb
    scale = DH ** -0.5
    aliased = dz_att is not None
    n_xi = len(exchange.ins) if exchange else 0
    n_xo = len(exchange.out_shapes) if exchange else 0

    def body(q_ref, k_ref, v_ref, kp_ref, vp_ref, qn_ref, do_ref, don_ref, dl_ref, dln_ref, lm_ref, lmn_ref,
             gq_ref, gk_ref, sl_ref, *rest):
        rest = rest[1:] if aliased else rest
        x_in, (dz_ref, dgq_ref, dgk_ref), rest = rest[:n_xi], rest[n_xi:n_xi + 3], rest[n_xi + 3:]
        x_out, stage, x_sems = rest[:n_xo], rest[n_xo], rest[n_xo + 1:]
        n = pl.program_id(1)
        step = pl.program_id(0) * nsb + n
        if exchange:
            start, _, finish = exchange.split(x_in, x_out, x_sems)
            pl.when(step == 0)(start)
        slope = sl_ref[...][:, :1]
        valid0, bias = _att_mask_bias(slope, d, n == 0)
        valid_in, _ = _att_mask_bias(slope, d, None)
        qi = lax.broadcasted_iota(jnp.int32, (BLK, BLK), 0)
        kj = lax.broadcasted_iota(jnp.int32, (BLK, BLK), 1)
        dist_n = BLK + qi - kj
        valid_n_in = dist_n <= BLK
        valid_n_last = valid_n_in & (n < nsb - 1)
        bias_n = -slope * (dist_n * d).astype(F32)
        gq, gk = gq_ref[...], gk_ref[...]
        dgq = jnp.zeros((1, DH), F32)
        dgk = jnp.zeros((1, DH), F32)
        memo = {}

        def get(kind, r, b):
            if (kind, r, b) not in memo:
                inner = 0 <= b < nb
                bb = b if inner else 0
                if kind == "q":
                    val = _head_norm(_rows(q_ref if inner else qn_ref, r, bb, d), gq)
                elif kind == "k":
                    val = _head_norm(_rows(k_ref if inner else kp_ref, r, bb, d), gk)
                elif kind == "v":
                    val = _rows(v_ref if inner else vp_ref, r, bb, d).astype(BF16)
                elif kind == "do":
                    val = _rows(do_ref if inner else don_ref, r, bb, d).astype(BF16)
                elif kind == "dl":
                    val = _rows(dl_ref if inner else dln_ref, r, bb, d)[:, :1]
                else:
                    val = _rows(lm_ref if inner else lmn_ref, r, bb, d)[:, :1]
                memo[(kind, r, b)] = val
            return memo[(kind, r, b)]

        units = [(r, b) for r in range(d) for b in range(nb)]
        for c0 in range(0, len(units), ATT_TOGETHER):
            us = units[c0:c0 + ATT_TOGETHER]
            k2 = [jnp.concatenate([get("k", r, b - 1)[0], get("k", r, b)[0]], axis=0) for r, b in us]
            v2 = [jnp.concatenate([get("v", r, b - 1), get("v", r, b)], axis=0) for r, b in us]
            p = [_att_probs(get("q", r, b)[0], k2[i], get("lm", r, b), valid0 if b == 0 else valid_in, bias)
                 for i, (r, b) in enumerate(us)]
            dp = [_dot(get("do", r, b), v2[i], _NT) for i, (r, b) in enumerate(us)]
            ds = [(p[i] * (dp[i] - get("dl", r, b)) * scale).astype(BF16) for i, (r, b) in enumerate(us)]
            dq = [_dot(ds[i], k2[i]) for i in range(len(us))]
            p_n = [_att_probs(get("q", r, b + 1)[0], get("k", r, b)[0], get("lm", r, b + 1),
                              valid_n_last if b == nb - 1 else valid_n_in, bias_n) for r, b in us]
            dp_n = [_dot(get("do", r, b + 1), get("v", r, b), _NT) for r, b in us]
            ds_n = [(p_n[i] * (dp_n[i] - get("dl", r, b + 1)) * scale).astype(BF16) for i, (r, b) in enumerate(us)]
            dk = [_dot(ds[i][:, BLK:], get("q", r, b)[0], _TN) + _dot(ds_n[i], get("q", r, b + 1)[0], _TN)
                  for i, (r, b) in enumerate(us)]
            dv = [_dot(p[i][:, BLK:].astype(BF16), get("do", r, b), _TN)
                  + _dot(p_n[i].astype(BF16), get("do", r, b + 1), _TN) for i, (r, b) in enumerate(us)]
            for i, (r, b) in enumerate(us):
                _, qh, qr = get("q", r, b)
                _, kh, kr = get("k", r, b)
                dxq, dg_q = _head_norm_bwd(dq[i], qh, qr, gq)
                dxk, dg_k = _head_norm_bwd(dk[i], kh, kr, gk)
                _put_rows(stage.at[0], r, b, d, dxq)
                _put_rows(stage.at[1], r, b, d, dxk)
                _put_rows(stage.at[2], r, b, d, dv[i])
                dgq, dgk = dgq + dg_q, dgk + dg_k
        for kind in range(3):
            dz_ref[:, kind * DH:(kind + 1) * DH] = stage[kind].astype(BF16)
        _acc_rows(dgq_ref, dgq, n == 0)
        _acc_rows(dgk_ref, dgk, n == 0)
        if exchange:
            pl.when(step == HPG * nsb - 1)(finish)

    cur, prev, nxt, slot, slot_next, head = _att_specs(gi, d, nb, S)
    gain = pl.BlockSpec((None, 1, DH), lambda j, n: (j, 0, 0))
    in_specs = [cur(0), cur(1), cur(2), prev(1), prev(2), nxt(0),
                slot, slot_next, slot, slot_next, slot, slot_next, head, head, head]
    args = [z_att] * 6 + [do_a, do_a, delta, delta, lmix, lmix, gq3, gk3, slopes3]
    if aliased:
        in_specs.append(_ANY)
        args.append(dz_att)
    res = pl.pallas_call(
        body, name=f"att_bwd_g{gi}", grid=(HPG, nsb),
        in_specs=in_specs + [_ANY] * n_xi,
        out_specs=[pl.BlockSpec((sb, 3 * DH), lambda j, n: (n, gi * HPG + j)), gain, gain] + [_ANY] * n_xo,
        out_shape=[_sds(z_att.shape, BF16), _sds((HPG, 1, DH), F32), _sds((HPG, 1, DH), F32)]
        + (exchange.out_shapes if exchange else []),
        input_output_aliases={len(args) - 1: 0} if aliased else {},
        scratch_shapes=[pltpu.VMEM((3, sb, DH), F32)] + (exchange.sems if exchange else []),
        compiler_params=_cparams(("arbitrary", "arbitrary") if exchange else ("parallel", "arbitrary")),
    )(*args, *(exchange.ins if exchange else []))
    return res[:3], res[3:]


def _step(x, target, norm1_g, q_norm_g, k_norm_g, gn_g, gn_b, norm2_g, shards, core):
    S = x.shape[0]
    tm = min(512, S)
    ts = min(256, S)
    tk = min(2048, S)
    later_shards = shards[1:]
    gq3 = q_norm_g.reshape(ATT_HEADS, 1, DH)
    gk3 = k_norm_g.reshape(ATT_HEADS, 1, DH)
    slopes3 = jnp.asarray(np.broadcast_to(ALIBI[:, None, None], (ATT_HEADS, 1, DH)).copy())

    xn, (g_in,) = _rmsnorm_fwd(x, norm1_g, tm, _gather_exchange(shards[:1]))
    w_in = _cols(g_in)
    w_att = w_in[:, :3 * ATT_W].reshape(D_MODEL, 3, ATT_HEADS, DH).transpose(0, 2, 1, 3).reshape(D_MODEL, 3 * ATT_W)
    w_rest = w_in[:, 3 * ATT_W:]
    w_att_t, w_rest_t = w_att.T, w_rest.T
    z_att = _mm("in_proj_att", xn, w_att, "nn", ts, 3 * ATT_W, D_MODEL,
                [(_sds((S, 3 * ATT_W), F32), _tile_ij(ts, 3 * ATT_W))], _epi_store(F32))[0]
    (z_rest,), gathered = _mm("in_proj_rest", xn, w_rest, "nn", ts, REST_W, D_MODEL,
                              [(_sds((S, REST_W), BF16), _tile_ij(ts, REST_W))], _epi_store(BF16),
                              exchange=_gather_exchange(later_shards))
    w_pa, w_pb, w_out, w_up, w_down = [f(g) for f, g in zip((_cols, _rows_of, _rows_of, _cols, _rows_of), gathered)]
    w_up_t, w_down_t, w_out_t, w_pa_t, w_pb_t = w_up.T, w_down.T, w_out.T, w_pa.T, w_pb.T
    done = []
    for gi, ((_, d), nb) in enumerate(zip(ATT_GROUPS, ATT_BLOCKS_PER_STEP)):
        last = gi == len(ATT_GROUPS) - 1
        done.append(_att_fwd_group(z_att, gq3, gk3, slopes3, gi, d, nb, others=tuple(done) if last else ()))
    o_a, lmix = done[-1]
    o_r, o_pre, states = _ret_fwd(z_rest, gn_g, gn_b)
    x1, y, pa, pb, xn2 = _merge_fwd(o_a, o_r, z_rest, x, w_pa, w_pb, w_out, norm2_g, min(256, S))

    def epi_up(acc, ex, out):
        r = jnp.maximum(acc, 0.0)
        out[0][...] = (r * r).astype(BF16)
        out[1][...] = r.astype(BF16)

    h, relu_u = _mm("mlp_up", xn2, w_up, "nn", ts, D_FF, D_MODEL,
                    [(_sds((S, D_FF), BF16), _tile_ij(ts, D_FF)), (_sds((S, D_FF), BF16), _tile_ij(ts, D_FF))], epi_up)

    def epi_down(acc, ex, out):
        diff = ex[0][...] + acc - ex[1][...]
        out[0][...] = diff * (1.0 / D_MODEL)
        out[1][...] = jnp.broadcast_to(jnp.sum(diff * diff) * (1.0 / (8 * LANES)), (8, LANES))

    row_tile = _tile_ij(tm, D_MODEL)
    dx2, loss_parts = _mm(
        "mlp_down_loss", h, w_down, "nn", tm, D_MODEL, D_FF,
        [(_sds((S, D_MODEL), F32), row_tile),
         (_sds((S // tm * 8, LANES), F32), pl.BlockSpec((8, LANES), lambda i, j, k: (i, 0)))],
        epi_down, extras=[(x1, row_tile), (target, row_tile)])
    loss_local = jnp.sum(loss_parts) * (0.5 / D_MODEL)

    def epi_du(acc, ex, out):
        out[0][...] = (acc * (2.0 * ex[0][...].astype(F32))).astype(BF16)

    du = _mm("mlp_down_bwd", dx2, w_down_t, "nn", ts, D_FF, D_MODEL,
             [(_sds((S, D_FF), BF16), _tile_ij(ts, D_FF))], epi_du, extras=[(relu_u, _tile_ij(ts, D_FF))])[0]
    gw_down = _mm("gw_down", h, dx2, "tn", 1024, D_MODEL, tk,
                  [(_sds((D_FF, D_MODEL), BF16), _tile_ij(1024, D_MODEL))], _epi_store(BF16))[0]
    gw_up = _mm("gw_up", xn2, du, "tn", D_MODEL, 512, tk,
                [(_sds((N_DEV, D_MODEL, 512), BF16), pl.BlockSpec((None, D_MODEL, 512), lambda i, j, k: (j, 0, 0)))],
                _epi_store(BF16))[0]

    vec = pl.BlockSpec((1, D_MODEL), lambda i, j, k: (0, 0))
    seq_sem = ("arbitrary", "arbitrary", "arbitrary")

    def epi_norm2(acc, ex, out):
        dx, dg = _rms_bwd(acc, ex[0][...], ex[2][...])
        out[0][...] = ex[1][...] + dx
        _acc_rows(out[1], dg, pl.program_id(0) == 0)

    dx1, g_norm2 = _mm(
        "mlp_up_bwd", du, w_up_t, "nn", tm, D_MODEL, D_FF,
        [(_sds((S, D_MODEL), F32), row_tile), (_sds((1, D_MODEL), F32), vec)],
        epi_norm2, extras=[(x1, row_tile), (dx2, row_tile), (norm2_g, vec)], sem=seq_sem)

    def epi_dy(acc, ex, out):
        sa = jax.nn.sigmoid(ex[0][...].astype(F32))
        sb = jax.nn.sigmoid(ex[1][...].astype(F32))
        out[0][...] = (acc * sa).astype(BF16)
        out[1][...] = (acc * sb).astype(BF16)
        out[2][:, :D_MODEL] = (acc * ex[2][...].astype(F32) * (sa * (1.0 - sa))).astype(BF16)
        out[2][:, D_MODEL:] = (acc * ex[3][...].astype(F32) * (sb * (1.0 - sb))).astype(BF16)

    ga_spec = pl.BlockSpec((tm, D_MODEL), lambda i, j, k: (i, R_GA // D_MODEL))
    gb_spec = pl.BlockSpec((tm, D_MODEL), lambda i, j, k: (i, R_GB // D_MODEL))
    gates_spec = pl.BlockSpec((tm, 2 * D_MODEL), lambda i, j, k: (i, R_GA // (2 * D_MODEL)))
    dpa, dpb, dz_rest = _mm(
        "out_proj_bwd", dx1, w_out_t, "nn", tm, D_MODEL, D_MODEL,
        [(_sds((S, D_MODEL), BF16), row_tile), (_sds((S, D_MODEL), BF16), row_tile), (_sds((S, REST_W), BF16), gates_spec)],
        epi_dy, extras=[(z_rest, ga_spec), (z_rest, gb_spec), (pa, row_tile), (pb, row_tile)])
    gw_out = _mm("gw_out", y, dx1, "tn", D_MODEL, D_MODEL, tk,
                 [(_sds((D_MODEL, D_MODEL), BF16), _tile_ij(D_MODEL, D_MODEL))], _epi_store(BF16))[0]
    gw_pa = _mm("gw_proj_a", o_a, dpa, "tn", ATT_OUT_W, D_MODEL, tk,
                [(_sds((ATT_OUT_W, D_MODEL), BF16), _tile_ij(ATT_OUT_W, D_MODEL))], _epi_store(BF16))[0]
    gw_pb = _mm("gw_proj_b", o_r, dpb, "tn", 1024, D_MODEL, tk,
                [(_sds((RET_V_W, D_MODEL), BF16), _tile_ij(1024, D_MODEL))], _epi_store(BF16))[0]

    def epi_doa(acc, ex, out):
        out[0][...] = acc
        prod = acc * ex[0][...]
        out[1][...] = jnp.concatenate(
            [jnp.broadcast_to(jnp.sum(prod[:, j * DH:(j + 1) * DH], axis=-1, keepdims=True), (prod.shape[0], DH))
             for j in range(HPG)], axis=1)

    slot_tile = _tile_ij(tm, ATT_OUT_W)
    do_a, delta = _mm("proj_a_bwd", dpa, w_pa_t, "nn", tm, ATT_OUT_W, D_MODEL,
                      [(_sds((S, ATT_OUT_W), F32), slot_tile), (_sds((S, ATT_OUT_W), F32), slot_tile)],
                      epi_doa, extras=[(o_a, slot_tile)])
    owed = [_chip_core(_by_owner_cols(gw_pa)), _chip_core(gw_pb.reshape(N_DEV, -1, D_MODEL)),
            _chip_core(gw_out.reshape(N_DEV, -1, D_MODEL)), _chip_core(gw_up), _chip_core(gw_down.reshape(N_DEV, -1, D_MODEL))]
    names = ("w_proj_a", "w_proj_b", "w_out", "w_up", "w_down")

    def epi_dor(acc, ex, out):
        cvs = [slice(h * RET_V, (h + 1) * RET_V) for h in range(RET_HEADS)]
        do, dgr, dg, db = _norm_gate_bwd([acc[:, cv] for cv in cvs], [ex[0][:, cv] for cv in cvs],
                                         [ex[1][:, cv].astype(F32) for cv in cvs],
                                         [ex[2][:, cv] for cv in cvs], [ex[3][:, cv] for cv in cvs])
        for h, cv in enumerate(cvs):
            out[0][:, cv] = do[h].astype(BF16)
            out[1][:, cv] = dgr[h].astype(BF16)
        first = pl.program_id(0) == 0
        _acc_rows(out[2], jnp.concatenate(dg, axis=1), first)
        _acc_rows(out[3], jnp.concatenate(db, axis=1), first)

    wide_tile = _tile_ij(ts, RET_V_W)
    gate_tile = pl.BlockSpec((ts, RET_V_W), lambda i, j, k: (i, R_G // RET_V_W))
    wide_vec = pl.BlockSpec((1, RET_V_W), lambda i, j, k: (0, 0))
    (do_ret, dz_rest, g_gn_g, g_gn_b), got = _mm(
        "proj_b_bwd", dpb, w_pb_t, "nn", ts, RET_V_W, D_MODEL,
        [(_sds((S, RET_V_W), BF16), wide_tile), (_sds((S, REST_W), BF16), gate_tile),
         (_sds((1, RET_V_W), F32), wide_vec), (_sds((1, RET_V_W), F32), wide_vec)],
        epi_dor, extras=[(o_pre, wide_tile), (z_rest, gate_tile), (gn_g, wide_vec), (gn_b, wide_vec)],
        exchange=_pair_exchange(owed), in_place=(dz_rest, 1))
    chip_sums = [_pair_sum(f"pair_sum_{n}", g, r, core, min(256, g.shape[2])) for n, g, r in zip(names, owed, got)]

    dz_rest = _ret_bwd(do_ret, states, z_rest, dz_rest)
    dz_att, gq_parts, gk_parts, parts_late = None, [], [], None
    for gi, ((_, d), nb) in enumerate(zip(ATT_GROUPS, ATT_BLOCKS_PER_STEP)):
        last = gi == len(ATT_GROUPS) - 1
        (dz_att, gq_p, gk_p), parts = _att_bwd_group(z_att, dz_att, do_a, delta, lmix, gq3, gk3, slopes3, gi, d, nb,
                                                     exchange=_chip_exchange(chip_sums) if last else None)
        parts_late = parts if last else parts_late
        gq_parts.append(gq_p)
        gk_parts.append(gk_p)
    g_qn = jnp.concatenate(gq_parts, axis=0).reshape(1, ATT_HEADS, DH)
    g_kn = jnp.concatenate(gk_parts, axis=0).reshape(1, ATT_HEADS, DH)

    gw_att = _mm("gw_in_att", xn, dz_att, "tn", D_MODEL, ATT_W, tk,
                 [(_sds((D_MODEL, 3 * ATT_W), BF16), _tile_ij(D_MODEL, ATT_W))], _epi_store(BF16))[0]
    gw_rest = _mm("gw_in_rest", xn, dz_rest, "tn", D_MODEL, 1024, tk,
                  [(_sds((D_MODEL, REST_W), BF16), _tile_ij(D_MODEL, 1024))], _epi_store(BF16))[0]
    gw_att = gw_att.reshape(D_MODEL, ATT_HEADS, 3, DH).transpose(0, 2, 1, 3).reshape(D_MODEL, 3 * ATT_W)
    gw_in = jnp.concatenate([gw_att, gw_rest], axis=1)
    owed_in = _chip_core(_by_owner_cols(gw_in))
    (dxn_att,), (got_in,) = _mm("in_proj_att_bwd", dz_att, w_att_t, "nn", tm, D_MODEL, 3 * ATT_W,
                                [(_sds((S, D_MODEL), F32), row_tile)], _epi_store(F32), exchange=_pair_exchange([owed_in]))
    chip_sum_in = _pair_sum("pair_sum_w_in", owed_in, got_in, core, min(256, owed_in.shape[2]))

    def epi_norm1(acc, ex, out):
        dx, dg = _rms_bwd(acc + ex[0][...], ex[1][...], ex[3][...])
        out[0][...] = ex[2][...] + dx
        _acc_rows(out[1], dg, pl.program_id(0) == 0)

    short_tile = _tile_ij(ts, D_MODEL)
    (grad_x, g_norm1), parts_in = _mm(
        "in_proj_rest_bwd", dz_rest, w_rest_t, "nn", ts, D_MODEL, REST_W,
        [(_sds((S, D_MODEL), F32), short_tile), (_sds((1, D_MODEL), F32), vec)],
        epi_norm1, extras=[(dxn_att, short_tile), (x, short_tile), (dx1, short_tile), (norm1_g, vec)],
        exchange=_chip_exchange([chip_sum_in]))

    small = (g_norm1, g_qn, g_kn, g_gn_g, g_gn_b, g_norm2)
    return loss_local, grad_x, list(parts_in) + list(parts_late), small


def _position():
    return lax.axis_index("x"), lax.axis_index("y"), lax.axis_index("c")


def _other_chips(x, y):
    return [(1 - x, y), (x, 1 - y), (1 - x, 1 - y)]


_ANY = pl.BlockSpec(memory_space=pl.ANY)


def _gather_exchange(shards):
    nw = len(shards)

    def plan(x_refs, out_refs, sems):
        send_sems, recv_sems, local_sems = sems
        x, y, c = _position()
        me, sibling = (x, y, c), (x, y, 1 - c)
        chips = _other_chips(x, y)

        def copy(w, k, block, to, own=False):
            px, py, pc = block
            rows = out_refs[w].at[4 * px + 2 * py + pc]
            return pltpu.make_async_remote_copy(
                src_ref=x_refs[w] if own else rows, dst_ref=rows,
                send_sem=send_sems.at[7 * w + k], recv_sem=recv_sems.at[7 * w + k], device_id=to, device_id_type=MESH)

        def mine(w):
            return pltpu.make_async_copy(x_refs[w], out_refs[w].at[4 * x + 2 * y + c], local_sems.at[w])

        def own_sends(w):
            return [copy(w, 0, me, sibling, own=True)] + [copy(w, 1 + j, me, (*chip, c), own=True)
                                                          for j, chip in enumerate(chips)]

        def start():
            for w in range(nw):
                mine(w).start()
                for cp in own_sends(w):
                    cp.start()

        def relay():
            for j, chip in enumerate(chips):
                for w in range(nw):
                    copy(w, 1 + j, (*chip, c), me).wait_recv()
                    copy(w, 4 + j, (*chip, c), sibling).start()

        def finish():
            for w in range(nw):
                copy(w, 0, sibling, me).wait_recv()
                for j, chip in enumerate(chips):
                    copy(w, 4 + j, (*chip, 1 - c), me).wait_recv()
            for w in range(nw):
                for cp in own_sends(w):
                    cp.wait_send()
                for j, chip in enumerate(chips):
                    copy(w, 4 + j, (*chip, c), sibling).wait_send()
                mine(w).wait()

        return start, relay, finish

    return _Exchange(shards, [_sds((N_DEV,) + s.shape, s.dtype) for s in shards],
                     [pltpu.SemaphoreType.DMA((7 * nw,)), pltpu.SemaphoreType.DMA((7 * nw,)),
                      pltpu.SemaphoreType.DMA((nw,))], plan)


def _run_exchange(name, exchange):
    n_in, n_out = len(exchange.ins), len(exchange.out_shapes)

    def body(*refs):
        start, relay, finish = exchange.split(refs[:n_in], refs[n_in:n_in + n_out], refs[n_in + n_out:])
        start()
        if relay is not None:
            relay()
        finish()

    return pl.pallas_call(
        body, name=name, out_shape=exchange.out_shapes,
        in_specs=[_ANY] * n_in, out_specs=[_ANY] * n_out, scratch_shapes=exchange.sems,
    )(*exchange.ins)


def _pair_exchange(grads):
    ng = len(grads)

    def plan(g_refs, out_refs, sems):
        send_sems, recv_sems = sems
        x, y, c = _position()

        def copies():
            return [pltpu.make_async_remote_copy(
                src_ref=g_refs[w].at[:, 1 - c], dst_ref=out_refs[w], send_sem=send_sems.at[w],
                recv_sem=recv_sems.at[w], device_id=(x, y, 1 - c), device_id_type=MESH) for w in range(ng)]

        def start():
            for cp in copies():
                cp.start()

        def finish():
            for cp in copies():
                cp.wait()

        return start, None, finish

    return _Exchange(grads, [_sds((N_CHIP,) + g.shape[2:], g.dtype) for g in grads],
                     [pltpu.SemaphoreType.DMA((ng,)), pltpu.SemaphoreType.DMA((ng,))], plan)


def _small_all_gather(small):
    def plan(in_refs, out_refs, sems):
        (s_ref,), (s_out,) = in_refs, out_refs
        send_sems, recv_sems, local_sem = sems
        x, y, c = _position()
        me_id = 4 * x + 2 * y + c
        flips = [(a, b, e) for a in (0, 1) for b in (0, 1) for e in (0, 1)][1:]
        peers = [(x ^ a, y ^ b, c ^ e) for a, b, e in flips]

        def start():
            pltpu.make_async_copy(s_ref, s_out.at[me_id], local_sem).start()
            for k, p in enumerate(peers):
                pltpu.make_async_remote_copy(
                    src_ref=s_ref, dst_ref=s_out.at[me_id], send_sem=send_sems.at[k], recv_sem=recv_sems.at[k],
                    device_id=p, device_id_type=MESH).start()

        def finish():
            for k, (px, py, pc) in enumerate(peers):
                pltpu.make_async_remote_copy(
                    src_ref=s_ref, dst_ref=s_out.at[4 * px + 2 * py + pc], send_sem=send_sems.at[k],
                    recv_sem=recv_sems.at[k], device_id=(px, py, pc), device_id_type=MESH).wait()
            pltpu.make_async_copy(s_ref, s_out.at[me_id], local_sem).wait()

        return start, None, finish

    return _run_exchange("small_grad_all_gather", _Exchange(
        [small], [_sds((N_DEV,) + small.shape, small.dtype)],
        [pltpu.SemaphoreType.DMA((7,)), pltpu.SemaphoreType.DMA((7,)), pltpu.SemaphoreType.DMA], plan))[0]


def _pair_sum(name, g, got, core, tr):
    n_chip, _, R, C = g.shape

    def body(c_ref, a_ref, b_ref, o_ref):
        del c_ref
        o_ref[...] = (a_ref[...].astype(F32) + b_ref[...].astype(F32)).astype(o_ref.dtype)

    return pl.pallas_call(
        body, name=name,
        grid_spec=pltpu.PrefetchScalarGridSpec(
            num_scalar_prefetch=1, grid=(n_chip, R // tr),
            in_specs=[pl.BlockSpec((None, None, tr, C), lambda ch, i, c_ref: (ch, c_ref[0], i, 0)),
                      pl.BlockSpec((None, tr, C), lambda ch, i, c_ref: (ch, i, 0))],
            out_specs=pl.BlockSpec((None, tr, C), lambda ch, i, c_ref: (ch, i, 0))),
        out_shape=_sds(got.shape, got.dtype),
        compiler_params=_cparams(("parallel", "parallel")),
    )(core, g, got)


def _chip_exchange(parts):
    ng = len(parts)

    def plan(p_refs, out_refs, sems):
        send_sems, recv_sems, local_sems = sems
        x, y, c = _position()
        my_chip = 2 * x + y

        def copies():
            local = [pltpu.make_async_copy(p_refs[w].at[my_chip], out_refs[w].at[my_chip], local_sems.at[w])
                     for w in range(ng)]
            remote = [pltpu.make_async_remote_copy(
                src_ref=p_refs[w].at[2 * cx + cy], dst_ref=out_refs[w].at[my_chip],
                send_sem=send_sems.at[3 * w + k], recv_sem=recv_sems.at[3 * w + k],
                device_id=(cx, cy, c), device_id_type=MESH)
                for w in range(ng) for k, (cx, cy) in enumerate(_other_chips(x, y))]
            return local + remote

        def start():
            for cp in copies():
                cp.start()

        def finish():
            for cp in copies():
                cp.wait()

        return start, None, finish

    return _Exchange(parts, [_sds(p.shape, p.dtype) for p in parts],
                     [pltpu.SemaphoreType.DMA((3 * ng,)), pltpu.SemaphoreType.DMA((3 * ng,)),
                      pltpu.SemaphoreType.DMA((ng,))], plan)


def _adamw(name, parts, w, m, v, tr):
    n_parts = parts.shape[0]
    R, C = w.shape

    def body(p_ref, w_ref, m_ref, v_ref, g_ref, d_ref, mo_ref, vo_ref):
        g = p_ref[0].astype(F32)
        for i in range(1, n_parts):
            g = g + p_ref[i].astype(F32)
        m_new = ADAM_B1 * m_ref[...] + (1.0 - ADAM_B1) * g
        v_new = ADAM_B2 * v_ref[...] + (1.0 - ADAM_B2) * (g * g)
        m_hat = m_new / (1.0 - ADAM_B1 ** ADAM_STEP)
        v_hat = v_new / (1.0 - ADAM_B2 ** ADAM_STEP)
        g_ref[...] = g
        d_ref[...] = -ADAM_LR * (m_hat / (jnp.sqrt(v_hat) + ADAM_EPS) + ADAM_WD * w_ref[...])
        mo_ref[...] = m_new
        vo_ref[...] = v_new

    tile = pl.BlockSpec((tr, C), lambda i: (i, 0))
    return pl.pallas_call(
        body, name=name, grid=(R // tr,),
        in_specs=[pl.BlockSpec((n_parts, tr, C), lambda i: (0, i, 0)), tile, tile, tile],
        out_specs=[tile] * 4,
        out_shape=[_sds((R, C), F32)] * 4,
        compiler_params=_cparams(("parallel",)),
    )(parts, w, m, v)


def _flat_small(arrs):
    return jnp.concatenate([a.reshape(-1) for a in arrs]).reshape(SMALL_ROWS, LANES)


def _cols(g):
    return g.transpose(1, 0, 2).reshape(g.shape[1], -1)


def _rows_of(g):
    return g.reshape(-1, g.shape[2])


def _by_owner_cols(g):
    rows, cols = g.shape
    return g.reshape(rows, N_DEV, cols // N_DEV).transpose(1, 0, 2)


def _chip_core(g):
    return g.reshape((N_CHIP, 2) + g.shape[1:])


def kernel(x, norm1_g, w_in, q_norm_g, k_norm_g, ret_gn_g, ret_gn_b, w_proj_a, w_proj_b, w_out, norm2_g, w_up, w_down, loss_target, m_norm1_g, m_w_in, m_q_norm_g, m_k_norm_g, m_ret_gn_g, m_ret_gn_b, m_w_proj_a, m_w_proj_b, m_w_out, m_norm2_g, m_w_up, m_w_down, v_norm1_g, v_w_in, v_q_norm_g, v_k_norm_g, v_ret_gn_g, v_ret_gn_b, v_w_proj_a, v_w_proj_b, v_w_out, v_norm2_g, v_w_up, v_w_down):
    big_w = (w_in, w_proj_a, w_proj_b, w_out, w_up, w_down)
    big_m = (m_w_in, m_w_proj_a, m_w_proj_b, m_w_out, m_w_up, m_w_down)
    big_v = (v_w_in, v_w_proj_a, v_w_proj_b, v_w_out, v_w_up, v_w_down)
    small_w = (norm1_g, q_norm_g, k_norm_g, ret_gn_g, ret_gn_b, norm2_g)
    small_m = (m_norm1_g, m_q_norm_g, m_k_norm_g, m_ret_gn_g, m_ret_gn_b, m_norm2_g)
    small_v = (v_norm1_g, v_q_norm_g, v_k_norm_g, v_ret_gn_g, v_ret_gn_b, v_norm2_g)

    shards = [w[0].astype(BF16) for w in big_w]
    core = lax.axis_index("c").astype(jnp.int32).reshape(1)
    loss_local, grad_x, parts, small_g = _step(
        x[0], loss_target[0], norm1_g, q_norm_g[0], k_norm_g[0], ret_gn_g, ret_gn_b, norm2_g, shards, core)
    small_all = _small_all_gather(_flat_small(small_g))
    names = ("w_in", "w_proj_a", "w_proj_b", "w_out", "w_up", "w_down")

    res = {}
    for n, p, w, m, v in zip(names, parts, big_w, big_m, big_v):
        outs = _adamw(f"adamw_{n}", p, w[0], m[0], v[0], min(128, w.shape[1]))
        res[n] = [o[None] for o in outs]
    s_outs = _adamw("adamw_small", small_all, _flat_small(small_w), _flat_small(small_m), _flat_small(small_v), SMALL_ROWS)
    small_names = ("norm1_g", "q_norm_g", "k_norm_g", "ret_gn_g", "ret_gn_b", "norm2_g")
    for n in small_names:
        res[n] = []
    for o in s_outs:
        flat, off = o.reshape(-1), 0
        for n, w in zip(small_names, small_w):
            res[n].append(flat[off:off + w.size].reshape(w.shape))
            off += w.size

    order = ("norm1_g", "w_in", "q_norm_g", "k_norm_g", "ret_gn_g", "ret_gn_b", "w_proj_a", "w_proj_b", "w_out",
             "norm2_g", "w_up", "w_down")
    loss = lax.psum(loss_local, MESH_AXES)
    return (loss, grad_x[None], *[res[n][0] for n in order], *[res[n][1] for n in order],
            *[res[n][2] for n in order], *[res[n][3] for n in order])
```

```python
import math

import numpy as np
import jax
import jax.numpy as jnp
from jax import lax
from jax.experimental import pallas as pl
from jax.experimental.pallas import tpu as pltpu

F32 = jnp.float32
BF16 = jnp.bfloat16

D_MODEL = 1024
ATT_GROUPS = ((128, 1), (512, 4), (2048, 16))
ATT_BLOCKS_PER_STEP = (8, 1, 1)
ATT_TOGETHER = 4
HPG = 4
ATT_HEADS = 12
DH = 128
BLK = 128
ATT_W = ATT_HEADS * DH
ATT_OUT_W = HPG * DH
RET_HEADS = 4
RET_QK = 256
RET_V = 512
RET_QK_W = RET_HEADS * RET_QK
RET_V_W = RET_HEADS * RET_V
CHUNK = 128
D_FF = 4096
IN_W = 12800
REST_W = IN_W - 3 * ATT_W
EPS = 1e-6
ADAM_LR, ADAM_B1, ADAM_B2, ADAM_EPS, ADAM_WD, ADAM_STEP = 0.001, 0.9, 0.999, 1e-08, 0.01, 10
N_DEV = 8
N_CHIP = 4
MESH_AXES = ("x", "y", "c")
MESH = pl.DeviceIdType.MESH
VMEM_LIMIT = 56 * 1024 * 1024
LANES = 128
NEG = -1e30

_NN = (((1,), (0,)), ((), ()))
_NT = (((1,), (1,)), ((), ()))
_TN = (((0,), (0,)), ((), ()))

R_Q, R_K, R_V, R_G, R_GA, R_GB = 0, 1024, 2048, 4096, 6144, 7168

LOG_GAMMA = [float(v) for v in np.log(1.0 - 2.0 ** (-5.0 - np.arange(RET_HEADS, dtype=np.float32))).astype(np.float32)]
ALIBI = np.asarray(2.0 ** (-8.0 * np.arange(1, ATT_HEADS + 1, dtype=np.float32) / ATT_HEADS), np.float32)

SMALL_ROWS = (1024 + 1536 + 1536 + 2048 + 2048 + 1024) // LANES


def _dot(a, b, dims=_NN):
    return lax.dot_general(a, b, dims, preferred_element_type=F32)


def _cparams(sem):
    return pltpu.CompilerParams(dimension_semantics=sem, vmem_limit_bytes=VMEM_LIMIT)


def _sds(shape, dtype):
    return jax.ShapeDtypeStruct(shape, dtype)


class _Exchange:
    def __init__(self, ins, out_shapes, sems, plan):
        self.ins, self.out_shapes, self.sems, self.plan = list(ins), list(out_shapes), list(sems), plan

    def split(self, in_refs, out_refs, sem_refs):
        return self.plan(in_refs, out_refs, sem_refs)


def _mm(name, a, b, mode, tm, tn, tk, outs, epi, extras=(), b_pro=None,
        sem=("parallel", "parallel", "arbitrary"), exchange=None, in_place=None):
    if mode == "nn":
        (M, K), (_, N) = a.shape, b.shape
        a_spec = pl.BlockSpec((tm, tk), lambda i, j, k: (i, k))
        dims = _NN
    else:
        (K, M), (_, N) = a.shape, b.shape
        a_spec = pl.BlockSpec((tk, tm), lambda i, j, k: (k, i))
        dims = _TN
    assert M % tm == 0 and N % tn == 0 and K % tk == 0, (name, M, N, K, tm, tn, tk)
    nk = K // tk
    whole_b = dict(pipeline_mode=pl.Buffered(1)) if (nk == 1 and N == tn) else {}
    b_spec = pl.BlockSpec((tk, tn), lambda i, j, k: (k, j), **whole_b)
    n_ex, n_out = len(extras), len(outs)
    grid = (M // tm, N // tn, nk)
    n_xi = len(exchange.ins) if exchange else 0
    n_xo = len(exchange.out_shapes) if exchange else 0
    n_acc = 1 if nk > 1 else 0

    n_ip = 1 if in_place else 0

    def body(a_ref, b_ref, *rest):
        ex, rest = rest[:n_ex], rest[n_ex:]
        x_in, rest = rest[:n_xi], rest[n_xi + n_ip:]
        out, rest = rest[:n_out], rest[n_out:]
        x_out, rest = rest[:n_xo], rest[n_xo:]
        step =(pl.program_id(0) * grid[1] + pl.program_id(1)) * grid[2] + pl.program_id(2)
        n_steps = grid[0] * grid[1] * grid[2]
        if exchange:
            start, relay, finish = exchange.split(x_in, x_out, rest[n_acc:])
            pl.when(step == 0)(start)
        bv = b_ref[...]
        if b_pro is not None:
            bv = b_pro(bv)
        part = _dot(a_ref[...].astype(BF16), bv.astype(BF16), dims)
        if nk == 1:
            epi(part, ex, out)
        else:
            acc_ref = rest[0]
            k = pl.program_id(2)

            @pl.when(k == 0)
            def _():
                acc_ref[...] = part

            @pl.when(k > 0)
            def _():
                acc_ref[...] += part

            @pl.when(k == nk - 1)
            def _():
                epi(acc_ref[...], ex, out)
        if exchange:
            if relay is not None:
                pl.when(step == (7 * n_steps) // 8)(relay)
            pl.when(step == n_steps - 1)(finish)

    res = pl.pallas_call(
        body,
        name=name,
        grid=grid,
        in_specs=[a_spec, b_spec] + [s for _, s in extras] + [_ANY] * (n_xi + n_ip),
        out_specs=[s for _, s in outs] + [_ANY] * n_xo,
        out_shape=[o for o, _ in outs] + (exchange.out_shapes if exchange else []),
        scratch_shapes=([pltpu.VMEM((tm, tn), F32)] if nk > 1 else []) + (exchange.sems if exchange else []),
        input_output_aliases={2 + n_ex + n_xi: in_place[1]} if in_place else {},
        compiler_params=_cparams(("arbitrary",) * 3 if exchange else sem),
    )(a, b, *[e for e, _ in extras], *(exchange.ins if exchange else []), *([in_place[0]] if in_place else []))
    return (res[:n_out], res[n_out:]) if exchange else res


def _tile_ij(tm, tn):
    return pl.BlockSpec((tm, tn), lambda i, j, k: (i, j))


def _epi_store(dtype):
    def epi(acc, ex, out):
        out[0][...] = acc.astype(dtype)
    return epi


def _rms_rows(x):
    return lax.rsqrt(jnp.mean(x * x, axis=-1, keepdims=True) + EPS)


def _acc_rows(ref, part, first):
    @pl.when(first)
    def _():
        ref[...] = part

    @pl.when(jnp.logical_not(first))
    def _():
        ref[...] += part


def _rmsnorm_fwd(x, g, tm, exchange):
    S, Dm = x.shape
    n_steps = S // tm
    n_xi, n_xo = len(exchange.ins), len(exchange.out_shapes)

    def body(x_ref, g_ref, *rest):
        x_in, o_ref, x_out, sems = rest[:n_xi], rest[n_xi], rest[n_xi + 1:n_xi + 1 + n_xo], rest[n_xi + 1 + n_xo:]
        start, relay, finish = exchange.split(x_in, x_out, sems)
        step = pl.program_id(0)
        pl.when(step == 0)(start)
        xv = x_ref[...]
        o_ref[...] = (xv * _rms_rows(xv) * g_ref[...]).astype(BF16)

        @pl.when(step == n_steps - 1)
        def _():
            relay()
            finish()

    res = pl.pallas_call(
        body, name="rmsnorm1_fwd", grid=(n_steps,),
        in_specs=[pl.BlockSpec((tm, Dm), lambda i: (i, 0)), pl.BlockSpec((1, Dm), lambda i: (0, 0))] + [_ANY] * n_xi,
        out_specs=[pl.BlockSpec((tm, Dm), lambda i: (i, 0))] + [_ANY] * n_xo,
        out_shape=[_sds((S, Dm), BF16)] + exchange.out_shapes,
        scratch_shapes=exchange.sems,
        compiler_params=_cparams(("arbitrary",)),
    )(x, g, *exchange.ins)
    return res[0], res[1:]


def _rows(ref, r, b, d):
    if d == 1:
        return ref[b * BLK:(b + 1) * BLK, :]
    return ref[pl.ds(b * BLK * d + r, BLK, stride=d), :]


def _put_rows(ref, r, b, d, val):
    if d == 1:
        ref[b * BLK:(b + 1) * BLK, :] = val
    else:
        ref[pl.ds(b * BLK * d + r, BLK, stride=d), :] = val


def _head_norm(x, g):
    r = _rms_rows(x)
    xh = x * r
    return (xh * g).astype(BF16), xh, r


def _head_norm_bwd(dyn, xh, r, g):
    dxh = dyn * g
    dx = r * (dxh - xh * jnp.mean(dxh * xh, axis=-1, keepdims=True))
    return dx, jnp.sum(dyn * xh, axis=0, keepdims=True)


def _att_mask_bias(slope, d, first):
    qi = lax.broadcasted_iota(jnp.int32, (BLK, 2 * BLK), 0)
    kj = lax.broadcasted_iota(jnp.int32, (BLK, 2 * BLK), 1)
    dist = BLK + qi - kj
    valid = (dist >= 0) & (dist <= BLK)
    if first is not None:
        valid = valid & (jnp.logical_not(first) | (kj >= BLK))
    bias = -slope * (dist * d).astype(F32)
    return valid, bias


def _att_specs(gi, d, nb, S):
    span = BLK * d
    sb = span * nb
    nspan = S // span
    before = lambda n: jnp.maximum(n * nb - 1, 0)
    after = lambda n: jnp.minimum((n + 1) * nb, nspan - 1)
    zcol = lambda j, kind: 3 * (gi * HPG + j) + kind
    cur = lambda kind: pl.BlockSpec((sb, DH), lambda j, n: (n, zcol(j, kind)))
    prev = lambda kind: pl.BlockSpec((span, DH), lambda j, n: (before(n), zcol(j, kind)))
    nxt = lambda kind: pl.BlockSpec((span, DH), lambda j, n: (after(n), zcol(j, kind)))
    slot = pl.BlockSpec((sb, DH), lambda j, n: (n, j))
    slot_next = pl.BlockSpec((span, DH), lambda j, n: (after(n), j))
    head = pl.BlockSpec((None, 1, DH), lambda j, n: (gi * HPG + j, 0, 0))
    return cur, prev, nxt, slot, slot_next, head


def _att_fwd_group(z_att, gq3, gk3, slopes3, gi, d, nb, others=()):
    S = z_att.shape[0]
    nsb = S // (BLK * d * nb)
    scale = DH ** -0.5
    n_other = len(others)

    def body(q_ref, k_ref, v_ref, kp_ref, vp_ref, gq_ref, gk_ref, sl_ref, *rest):
        other_refs, (o_ref, l_ref) = rest[:2 * n_other], rest[2 * n_other:]
        slope = sl_ref[...][:, :1]
        valid0, bias = _att_mask_bias(slope, d, pl.program_id(1) == 0)
        valid_in, _ = _att_mask_bias(slope, d, None)
        gq, gk = gq_ref[...], gk_ref[...]
        memo = {}

        def get(kind, r, b):
            if (kind, r, b) not in memo:
                if kind == "k":
                    val = _head_norm(_rows(k_ref if b >= 0 else kp_ref, r, max(b, 0), d), gk)[0]
                else:
                    val = _rows(v_ref if b >= 0 else vp_ref, r, max(b, 0), d).astype(BF16)
                memo[(kind, r, b)] = val
            return memo[(kind, r, b)]

        units = [(r, b) for r in range(d) for b in range(nb)]
        for c0 in range(0, len(units), ATT_TOGETHER):
            us = units[c0:c0 + ATT_TOGETHER]
            q = [_head_norm(_rows(q_ref, r, b, d), gq)[0] for r, b in us]
            k2 = [jnp.concatenate([get("k", r, b - 1), get("k", r, b)], axis=0) for r, b in us]
            v2 = [jnp.concatenate([get("v", r, b - 1), get("v", r, b)], axis=0) for r, b in us]
            s = [jnp.where(valid0 if b == 0 else valid_in, _dot(q[i], k2[i], _NT) * scale + bias, NEG)
                 for i, (r, b) in enumerate(us)]
            m = [jnp.max(si, axis=-1, keepdims=True) for si in s]
            p = [jnp.exp(si - mi) for si, mi in zip(s, m)]
            den = [jnp.sum(pi, axis=-1, keepdims=True) for pi in p]
            o = [_dot(pi.astype(BF16), vi) / di for pi, vi, di in zip(p, v2, den)]
            for i, (r, b) in enumerate(us):
                _put_rows(o_ref, r, b, d, o[i])
                _put_rows(l_ref, r, b, d, jnp.broadcast_to(m[i] + jnp.log(den[i]), (BLK, DH)))
        if n_other:
            os_ = [ref[...] for ref in other_refs[:n_other]] + [o_ref[...]]
            ls_ = [ref[...] for ref in other_refs[n_other:]] + [l_ref[...]]
            m = ls_[0]
            for l in ls_[1:]:
                m = jnp.maximum(m, l)
            es = [jnp.exp(l - m) for l in ls_]
            tot, mix = es[0], es[0] * os_[0]
            for e, o in zip(es[1:], os_[1:]):
                tot, mix = tot + e, mix + e * o
            o_ref[...] = mix / tot
            l_ref[...] = m + jnp.log(tot)

    cur, prev, _, slot, _, head = _att_specs(gi, d, nb, S)
    return pl.pallas_call(
        body, name=f"att_fwd_g{gi}", grid=(HPG, nsb),
        in_specs=[cur(0), cur(1), cur(2), prev(1), prev(2), head, head, head] + [slot] * (2 * n_other),
        out_specs=[slot, slot],
        out_shape=[_sds((S, ATT_OUT_W), F32), _sds((S, ATT_OUT_W), F32)],
        compiler_params=_cparams(("parallel", "arbitrary")),
    )(z_att, z_att, z_att, z_att, z_att, gq3, gk3, slopes3, *[o for o, _ in others], *[l for _, l in others])


def _ret_tables(lg):
    ri = lax.broadcasted_iota(jnp.int32, (CHUNK, CHUNK), 0)
    ci = lax.broadcasted_iota(jnp.int32, (CHUNK, CHUNK), 1)
    diff = (ri - ci).astype(F32)
    decay = jnp.where(diff >= 0, jnp.exp(lg * jnp.maximum(diff, 0.0)), 0.0)
    idx = lax.broadcasted_iota(jnp.int32, (CHUNK, 1), 0).astype(F32)
    xi = jnp.exp(lg * (idx + 1.0))
    zeta = jnp.exp(lg * (CHUNK - 1.0 - idx))
    return decay, xi, zeta, math.exp(lg * CHUNK)


def _ret_specs(nch, rev):
    idx = (lambda n: nch - 1 - n) if rev else (lambda n: n)
    qk = lambda off: pl.BlockSpec((CHUNK, RET_QK_W), lambda n: (idx(n), off // RET_QK_W))
    vv = lambda off: pl.BlockSpec((CHUNK, RET_V_W), lambda n: (idx(n), off // RET_V_W))
    par = pl.BlockSpec((1, RET_V_W), lambda n: (0, 0))
    wide = pl.BlockSpec((CHUNK, RET_V_W), lambda n: (idx(n), 0))
    st = pl.BlockSpec((RET_HEADS, None, RET_QK, RET_V), lambda n: (0, idx(n), 0, 0))
    return qk, vv, par, wide, st


def _ret_fwd(z_rest):
    S = z_rest.shape[0]
    nch = S // CHUNK

    def body(q_ref, k_ref, v_ref, o_ref, st_ref, state):
        @pl.when(pl.program_id(0) == 0)
        def _():
            state[...] = jnp.zeros_like(state)

        for h in range(RET_HEADS):
            decay, xi, zeta, gch = _ret_tables(LOG_GAMMA[h])
            cq = slice(h * RET_QK, (h + 1) * RET_QK)
            cv = slice(h * RET_V, (h + 1) * RET_V)
            q = q_ref[:, cq]
            kc32 = k_ref[:, cq].astype(F32) * (RET_QK ** -0.5)
            kc = kc32.astype(BF16)
            v = v_ref[:, cv]
            st = state[h]
            stb = st.astype(BF16)
            st_ref[h] = stb
            s = _dot(q, kc, _NT) * decay
            o_ref[:, cv] = _dot(s.astype(BF16), v) + _dot(q, stb) * xi
            state[h] = st * gch + _dot((kc32 * zeta).astype(BF16), v, _TN)

    qk, vv, _, wide, st = _ret_specs(nch, False)
    return pl.pallas_call(
        body, name="ret_fwd", grid=(nch,),
        in_specs=[qk(R_Q), qk(R_K), vv(R_V)],
        out_specs=[wide, st],
        out_shape=[_sds((S, RET_V_W), F32), _sds((RET_HEADS, nch, RET_QK, RET_V), BF16)],
        scratch_shapes=[pltpu.VMEM((RET_HEADS, RET_QK, RET_V), F32)],
        compiler_params=_cparams(("arbitrary",)),
    )(z_rest, z_rest, z_rest)


def _merge_fwd(o_a, o_pre, z_rest, x, gn_g, gn_b, wpa, wpb, wout, g2, tm):
    S = x.shape[0]

    def body(oa_ref, o_ref, gr_ref, ga_ref, gb_ref, x_ref, gng_ref, gnb_ref, wpa_ref, wpb_ref, wo_ref, g2_ref,
             x1_ref, y_ref, pa_ref, pb_ref, xn2_ref, or_ref):
        for h in range(RET_HEADS):
            cv = slice(h * RET_V, (h + 1) * RET_V)
            o = o_ref[:, cv]
            cen = o - jnp.mean(o, axis=-1, keepdims=True)
            yh = cen * lax.rsqrt(jnp.mean(cen * cen, axis=-1, keepdims=True) + EPS)
            gr = gr_ref[:, cv].astype(F32)
            or_ref[:, cv] = ((yh * gng_ref[:, cv] + gnb_ref[:, cv]) * (gr * jax.nn.sigmoid(gr))).astype(BF16)
        pa = _dot(oa_ref[...].astype(BF16), wpa_ref[...])
        pb = _dot(or_ref[...], wpb_ref[...])
        y = jax.nn.sigmoid(ga_ref[...].astype(F32)) * pa + jax.nn.sigmoid(gb_ref[...].astype(F32)) * pb
        yb = y.astype(BF16)
        x1 = x_ref[...] + _dot(yb, wo_ref[...])
        x1_ref[...] = x1
        y_ref[...] = yb
        pa_ref[...] = pa.astype(BF16)
        pb_ref[...] = pb.astype(BF16)
        xn2_ref[...] = (x1 * _rms_rows(x1) * g2_ref[...]).astype(BF16)

    row = lambda w: pl.BlockSpec((tm, w), lambda i: (i, 0))
    full = lambda a: pl.BlockSpec(a.shape, lambda i: (0, 0))
    return pl.pallas_call(
        body, name="merge_fwd", grid=(S // tm,),
        in_specs=[row(ATT_OUT_W), row(RET_V_W),
                  pl.BlockSpec((tm, RET_V_W), lambda i: (i, R_G // RET_V_W)),
                  pl.BlockSpec((tm, D_MODEL), lambda i: (i, R_GA // D_MODEL)),
                  pl.BlockSpec((tm, D_MODEL), lambda i: (i, R_GB // D_MODEL)),
                  row(D_MODEL), full(gn_g), full(gn_b), full(wpa), full(wpb), full(wout), full(g2)],
        out_specs=[row(D_MODEL)] * 5 + [row(RET_V_W)],
        out_shape=[_sds((S, D_MODEL), F32)] + [_sds((S, D_MODEL), BF16)] * 4 + [_sds((S, RET_V_W), BF16)],
        compiler_params=_cparams(("parallel",)),
    )(o_a, o_pre, z_rest, z_rest, z_rest, x, gn_g, gn_b, wpa, wpb, wout, g2)


def _rms_bwd(dy, xv, g):
    r = _rms_rows(xv)
    xh = xv * r
    dg = dy * g
    dx = r * (dg - xh * jnp.mean(dg * xh, axis=-1, keepdims=True))
    return dx, jnp.sum(dy * xh, axis=0, keepdims=True)


def _norm_gate_bwd(dout, o, gr, gam, bet):
    mean = lambda xs: [jnp.mean(x, axis=-1, keepdims=True) for x in xs]
    cen = [x - m for x, m in zip(o, mean(o))]
    rstd = [lax.rsqrt(v + EPS) for v in mean([c * c for c in cen])]
    yh = [c * r for c, r in zip(cen, rstd)]
    y = [a * g + b for a, g, b in zip(yh, gam, bet)]
    sg = [jax.nn.sigmoid(g) for g in gr]
    dy = [d * (g * s) for d, g, s in zip(dout, gr, sg)]
    dgr = [d * a * (s * (1.0 + g * (1.0 - s))) for d, a, s, g in zip(dout, y, sg, gr)]
    dyh = [d * g for d, g in zip(dy, gam)]
    m1, m2 = mean(dyh), mean([a * b for a, b in zip(dyh, yh)])
    do = [r * (d - a - h * b) for r, d, a, h, b in zip(rstd, dyh, m1, yh, m2)]
    dg = [jnp.sum(d * h, axis=0, keepdims=True) for d, h in zip(dy, yh)]
    db = [jnp.sum(d, axis=0, keepdims=True) for d in dy]
    return do, dgr, dg, db


def _ret_bwd(do, states, z_rest, dz_rest):
    S = z_rest.shape[0]
    nch = S // CHUNK

    def body(do_ref, st_ref, q_ref, k_ref, v_ref, dz_in, dz_ref, gst):
        del dz_in

        @pl.when(pl.program_id(0) == 0)
        def _():
            gst[...] = jnp.zeros_like(gst)

        for h in range(RET_HEADS):
            decay, xi, zeta, gch = _ret_tables(LOG_GAMMA[h])
            cq = slice(h * RET_QK, (h + 1) * RET_QK)
            cv = slice(h * RET_V, (h + 1) * RET_V)
            q = q_ref[:, cq]
            kc32 = k_ref[:, cq].astype(F32) * (RET_QK ** -0.5)
            kc = kc32.astype(BF16)
            v = v_ref[:, cv]
            dob = do_ref[:, cv]
            a = (_dot(q, kc, _NT) * decay).astype(BF16)
            da = (_dot(dob, v, _NT) * decay).astype(BF16)
            dcross = (dob.astype(F32) * xi).astype(BF16)
            g_next = gst[h]
            gb = g_next.astype(BF16)
            dq = _dot(da, kc) + _dot(dcross, st_ref[h], _NT)
            dkc = _dot(da, q, _TN)
            dkz = _dot(v, gb, _NT)
            dv = _dot(a, dob, _TN) + _dot((kc32 * zeta).astype(BF16), gb)
            gst[h] = g_next * gch + _dot(q, dcross, _TN)
            dz_ref[:, R_Q + h * RET_QK:R_Q + (h + 1) * RET_QK] = dq.astype(BF16)
            dz_ref[:, R_K + h * RET_QK:R_K + (h + 1) * RET_QK] = ((dkc + dkz * zeta) * (RET_QK ** -0.5)).astype(BF16)
            dz_ref[:, R_V + h * RET_V:R_V + (h + 1) * RET_V] = dv.astype(BF16)

    qk, vv, _, wide, st = _ret_specs(nch, True)
    return pl.pallas_call(
        body, name="ret_bwd", grid=(nch,),
        in_specs=[wide, st, qk(R_Q), qk(R_K), vv(R_V), _ANY],
        out_specs=pl.BlockSpec((CHUNK, R_G), lambda n: (nch - 1 - n, 0)),
        out_shape=_sds(dz_rest.shape, BF16),
        input_output_aliases={5: 0},
        scratch_shapes=[pltpu.VMEM((RET_HEADS, RET_QK, RET_V), F32)],
        compiler_params=_cparams(("arbitrary",)),
    )(do, states, z_rest, z_rest, z_rest, dz_rest)


def _att_probs(q, k, lmix, valid, bias):
    s = _dot(q, k, _NT) * (DH ** -0.5) + bias
    return jnp.where(valid, jnp.exp(jnp.where(valid, s, NEG) - lmix), 0.0)


def _att_bwd_group(z_att, dz_att, do_a, delta, lmix, gq3, gk3, slopes3, gi, d, nb, exchange=None):
    S = z_att.shape[0]
    sb = BLK * d * nb
    nsb = S // sb
    scale = DH ** -0.5
    aliased = dz_att is not None
    n_xi = len(exchange.ins) if exchange else 0
    n_xo = len(exchange.out_shapes) if exchange else 0

    def body(q_ref, k_ref, v_ref, kp_ref, vp_ref, qn_ref, do_ref, don_ref, dl_ref, dln_ref, lm_ref, lmn_ref,
             gq_ref, gk_ref, sl_ref, *rest):
        rest = rest[1:] if aliased else rest
        x_in, (dz_ref, dgq_ref, dgk_ref), rest = rest[:n_xi], rest[n_xi:n_xi + 3], rest[n_xi + 3:]
        x_out, stage, x_sems = rest[:n_xo], rest[n_xo], rest[n_xo + 1:]
        n = pl.program_id(1)
        step = pl.program_id(0) * nsb + n
        if exchange:
            start, _, finish = exchange.split(x_in, x_out, x_sems)
            pl.when(step == 0)(start)
        slope = sl_ref[...][:, :1]
        valid0, bias = _att_mask_bias(slope, d, n == 0)
        valid_in, _ = _att_mask_bias(slope, d, None)
        qi = lax.broadcasted_iota(jnp.int32, (BLK, BLK), 0)
        kj = lax.broadcasted_iota(jnp.int32, (BLK, BLK), 1)
        dist_n = BLK + qi - kj
        valid_n_in = dist_n <= BLK
        valid_n_last = valid_n_in & (n < nsb - 1)
        bias_n = -slope * (dist_n * d).astype(F32)
        gq, gk = gq_ref[...], gk_ref[...]
        dgq = jnp.zeros((1, DH), F32)
        dgk = jnp.zeros((1, DH), F32)
        memo = {}

        def get(kind, r, b):
            if (kind, r, b) not in memo:
                inner = 0 <= b < nb
                bb = b if inner else 0
                if kind == "q":
                    val = _head_norm(_rows(q_ref if inner else qn_ref, r, bb, d), gq)
                elif kind == "k":
                    val = _head_norm(_rows(k_ref if inner else kp_ref, r, bb, d), gk)
                elif kind == "v":
                    val = _rows(v_ref if inner else vp_ref, r, bb, d).astype(BF16)
                elif kind == "do":
                    val = _rows(do_ref if inner else don_ref, r, bb, d).astype(BF16)
                elif kind == "dl":
                    val = _rows(dl_ref if inner else dln_ref, r, bb, d)[:, :1]
                else:
                    val = _rows(lm_ref if inner else lmn_ref, r, bb, d)[:, :1]
                memo[(kind, r, b)] = val
            return memo[(kind, r, b)]

        units = [(r, b) for r in range(d) for b in range(nb)]
        for c0 in range(0, len(units), ATT_TOGETHER):
            us = units[c0:c0 + ATT_TOGETHER]
            k2 = [jnp.concatenate([get("k", r, b - 1)[0], get("k", r, b)[0]], axis=0) for r, b in us]
            v2 = [jnp.concatenate([get("v", r, b - 1), get("v", r, b)], axis=0) for r, b in us]
            p = [_att_probs(get("q", r, b)[0], k2[i], get("lm", r, b), valid0 if b == 0 else valid_in, bias)
                 for i, (r, b) in enumerate(us)]
            dp = [_dot(get("do", r, b), v2[i], _NT) for i, (r, b) in enumerate(us)]
            ds = [(p[i] * (dp[i] - get("dl", r, b)) * scale).astype(BF16) for i, (r, b) in enumerate(us)]
            dq = [_dot(ds[i], k2[i]) for i in range(len(us))]
            p_n = [_att_probs(get("q", r, b + 1)[0], get("k", r, b)[0], get("lm", r, b + 1),
                              valid_n_last if b == nb - 1 else valid_n_in, bias_n) for r, b in us]
            dp_n = [_dot(get("do", r, b + 1), get("v", r, b), _NT) for r, b in us]
            ds_n = [(p_n[i] * (dp_n[i] - get("dl", r, b + 1)) * scale).astype(BF16) for i, (r, b) in enumerate(us)]
            dk = [_dot(ds[i][:, BLK:], get("q", r, b)[0], _TN) + _dot(ds_n[i], get("q", r, b + 1)[0], _TN)
                  for i, (r, b) in enumerate(us)]
            dv = [_dot(p[i][:, BLK:].astype(BF16), get("do", r, b), _TN)
                  + _dot(p_n[i].astype(BF16), get("do", r, b + 1), _TN) for i, (r, b) in enumerate(us)]
            for i, (r, b) in enumerate(us):
                _, qh, qr = get("q", r, b)
                _, kh, kr = get("k", r, b)
                dxq, dg_q = _head_norm_bwd(dq[i], qh, qr, gq)
                dxk, dg_k = _head_norm_bwd(dk[i], kh, kr, gk)
                _put_rows(stage.at[0], r, b, d, dxq)
                _put_rows(stage.at[1], r, b, d, dxk)
                _put_rows(stage.at[2], r, b, d, dv[i])
                dgq, dgk = dgq + dg_q, dgk + dg_k
        for kind in range(3):
            dz_ref[:, kind * DH:(kind + 1) * DH] = stage[kind].astype(BF16)
        _acc_rows(dgq_ref, dgq, n == 0)
        _acc_rows(dgk_ref, dgk, n == 0)
        if exchange:
            pl.when(step == HPG * nsb - 1)(finish)

    cur, prev, nxt, slot, slot_next, head = _att_specs(gi, d, nb, S)
    gain = pl.BlockSpec((None, 1, DH), lambda j, n: (j, 0, 0))
    in_specs = [cur(0), cur(1), cur(2), prev(1), prev(2), nxt(0),
                slot, slot_next, slot, slot_next, slot, slot_next, head, head, head]
    args = [z_att] * 6 + [do_a, do_a, delta, delta, lmix, lmix, gq3, gk3, slopes3]
    if aliased:
        in_specs.append(_ANY)
        args.append(dz_att)
    res = pl.pallas_call(
        body, name=f"att_bwd_g{gi}", grid=(HPG, nsb),
        in_specs=in_specs + [_ANY] * n_xi,
        out_specs=[pl.BlockSpec((sb, 3 * DH), lambda j, n: (n, gi * HPG + j)), gain, gain] + [_ANY] * n_xo,
        out_shape=[_sds(z_att.shape, BF16), _sds((HPG, 1, DH), F32), _sds((HPG, 1, DH), F32)]
        + (exchange.out_shapes if exchange else []),
        input_output_aliases={len(args) - 1: 0} if aliased else {},
        scratch_shapes=[pltpu.VMEM((3, sb, DH), F32)] + (exchange.sems if exchange else []),
        compiler_params=_cparams(("arbitrary", "arbitrary") if exchange else ("parallel", "arbitrary")),
    )(*args, *(exchange.ins if exchange else []))
    return res[:3], res[3:]


def _step(x, target, norm1_g, q_norm_g, k_norm_g, gn_g, gn_b, norm2_g, shards, core):
    S = x.shape[0]
    tm = min(512, S)
    ts = min(256, S)
    tk = min(2048, S)
    later_shards = shards[1:]
    gq3 = q_norm_g.reshape(ATT_HEADS, 1, DH)
    gk3 = k_norm_g.reshape(ATT_HEADS, 1, DH)
    slopes3 = jnp.asarray(np.broadcast_to(ALIBI[:, None, None], (ATT_HEADS, 1, DH)).copy())

    xn, (g_in,) = _rmsnorm_fwd(x, norm1_g, tm, _gather_exchange(shards[:1]))
    w_in = _cols(g_in)
    w_att = w_in[:, :3 * ATT_W].reshape(D_MODEL, 3, ATT_HEADS, DH).transpose(0, 2, 1, 3).reshape(D_MODEL, 3 * ATT_W)
    w_rest = w_in[:, 3 * ATT_W:]
    w_att_t, w_rest_t = w_att.T, w_rest.T
    z_att = _mm("in_proj_att", xn, w_att, "nn", ts, 3 * ATT_W, D_MODEL,
                [(_sds((S, 3 * ATT_W), F32), _tile_ij(ts, 3 * ATT_W))], _epi_store(F32))[0]
    (z_rest,), gathered = _mm("in_proj_rest", xn, w_rest, "nn", ts, REST_W, D_MODEL,
                              [(_sds((S, REST_W), BF16), _tile_ij(ts, REST_W))], _epi_store(BF16),
                              exchange=_gather_exchange(later_shards))
    w_pa, w_pb, w_out, w_up, w_down = [f(g) for f, g in zip((_cols, _rows_of, _rows_of, _cols, _rows_of), gathered)]
    w_up_t, w_down_t, w_out_t, w_pa_t, w_pb_t = w_up.T, w_down.T, w_out.T, w_pa.T, w_pb.T
    done = []
    for gi, ((_, d), nb) in enumerate(zip(ATT_GROUPS, ATT_BLOCKS_PER_STEP)):
        last = gi == len(ATT_GROUPS) - 1
        done.append(_att_fwd_group(z_att, gq3, gk3, slopes3, gi, d, nb, others=tuple(done) if last else ()))
    o_a, lmix = done[-1]
    o_pre, states = _ret_fwd(z_rest)
    x1, y, pa, pb, xn2, o_r = _merge_fwd(o_a, o_pre, z_rest, x, gn_g, gn_b, w_pa, w_pb, w_out, norm2_g, ts)

    def epi_up(acc, ex, out):
        r = jnp.maximum(acc, 0.0)
        out[0][...] = (r * r).astype(BF16)
        out[1][...] = r.astype(BF16)

    h, relu_u = _mm("mlp_up", xn2, w_up, "nn", ts, D_FF, D_MODEL,
                    [(_sds((S, D_FF), BF16), _tile_ij(ts, D_FF)), (_sds((S, D_FF), BF16), _tile_ij(ts, D_FF))], epi_up)

    def epi_down(acc, ex, out):
        diff = ex[0][...] + acc - ex[1][...]
        out[0][...] = diff * (1.0 / D_MODEL)
        out[1][...] = jnp.broadcast_to(jnp.sum(diff * diff) * (1.0 / (8 * LANES)), (8, LANES))

    row_tile = _tile_ij(tm, D_MODEL)
    dx2, loss_parts = _mm(
        "mlp_down_loss", h, w_down, "nn", tm, D_MODEL, D_FF,
        [(_sds((S, D_MODEL), F32), row_tile),
         (_sds((S // tm * 8, LANES), F32), pl.BlockSpec((8, LANES), lambda i, j, k: (i, 0)))],
        epi_down, extras=[(x1, row_tile), (target, row_tile)])
    loss_local = jnp.sum(loss_parts) * (0.5 / D_MODEL)

    def epi_du(acc, ex, out):
        out[0][...] = (acc * (2.0 * ex[0][...].astype(F32))).astype(BF16)

    du = _mm("mlp_down_bwd", dx2, w_down_t, "nn", ts, D_FF, D_MODEL,
             [(_sds((S, D_FF), BF16), _tile_ij(ts, D_FF))], epi_du, extras=[(relu_u, _tile_ij(ts, D_FF))])[0]
    gw_down = _mm("gw_down", h, dx2, "tn", 1024, D_MODEL, tk,
                  [(_sds((D_FF, D_MODEL), BF16), _tile_ij(1024, D_MODEL))], _epi_store(BF16))[0]
    gw_up = _mm("gw_up", xn2, du, "tn", D_MODEL, 512, tk,
                [(_sds((N_DEV, D_MODEL, 512), BF16), pl.BlockSpec((None, D_MODEL, 512), lambda i, j, k: (j, 0, 0)))],
                _epi_store(BF16))[0]

    vec = pl.BlockSpec((1, D_MODEL), lambda i, j, k: (0, 0))
    seq_sem = ("arbitrary", "arbitrary", "arbitrary")

    def epi_norm2(acc, ex, out):
        dx, dg = _rms_bwd(acc, ex[0][...], ex[2][...])
        out[0][...] = ex[1][...] + dx
        _acc_rows(out[1], dg, pl.program_id(0) == 0)

    dx1, g_norm2 = _mm(
        "mlp_up_bwd", du, w_up_t, "nn", tm, D_MODEL, D_FF,
        [(_sds((S, D_MODEL), F32), row_tile), (_sds((1, D_MODEL), F32), vec)],
        epi_norm2, extras=[(x1, row_tile), (dx2, row_tile), (norm2_g, vec)], sem=seq_sem)

    def epi_dy(acc, ex, out):
        sa = jax.nn.sigmoid(ex[0][...].astype(F32))
        sb = jax.nn.sigmoid(ex[1][...].astype(F32))
        out[0][...] = (acc * sa).astype(BF16)
        out[1][...] = (acc * sb).astype(BF16)
        out[2][:, :D_MODEL] = (acc * ex[2][...].astype(F32) * (sa * (1.0 - sa))).astype(BF16)
        out[2][:, D_MODEL:] = (acc * ex[3][...].astype(F32) * (sb * (1.0 - sb))).astype(BF16)

    ga_spec = pl.BlockSpec((tm, D_MODEL), lambda i, j, k: (i, R_GA // D_MODEL))
    gb_spec = pl.BlockSpec((tm, D_MODEL), lambda i, j, k: (i, R_GB // D_MODEL))
    gates_spec = pl.BlockSpec((tm, 2 * D_MODEL), lambda i, j, k: (i, R_GA // (2 * D_MODEL)))
    dpa, dpb, dz_rest = _mm(
        "out_proj_bwd", dx1, w_out_t, "nn", tm, D_MODEL, D_MODEL,
        [(_sds((S, D_MODEL), BF16), row_tile), (_sds((S, D_MODEL), BF16), row_tile), (_sds((S, REST_W), BF16), gates_spec)],
        epi_dy, extras=[(z_rest, ga_spec), (z_rest, gb_spec), (pa, row_tile), (pb, row_tile)])
    gw_out = _mm("gw_out", y, dx1, "tn", D_MODEL, D_MODEL, tk,
                 [(_sds((D_MODEL, D_MODEL), BF16), _tile_ij(D_MODEL, D_MODEL))], _epi_store(BF16))[0]
    gw_pa = _mm("gw_proj_a", o_a, dpa, "tn", ATT_OUT_W, D_MODEL, tk,
                [(_sds((ATT_OUT_W, D_MODEL), BF16), _tile_ij(ATT_OUT_W, D_MODEL))], _epi_store(BF16))[0]
    gw_pb = _mm("gw_proj_b", o_r, dpb, "tn", 1024, D_MODEL, tk,
                [(_sds((RET_V_W, D_MODEL), BF16), _tile_ij(1024, D_MODEL))], _epi_store(BF16))[0]

    def epi_doa(acc, ex, out):
        out[0][...] = acc
        prod = acc * ex[0][...]
        out[1][...] = jnp.concatenate(
            [jnp.broadcast_to(jnp.sum(prod[:, j * DH:(j + 1) * DH], axis=-1, keepdims=True), (prod.shape[0], DH))
             for j in range(HPG)], axis=1)

    slot_tile = _tile_ij(tm, ATT_OUT_W)
    do_a, delta = _mm("proj_a_bwd", dpa, w_pa_t, "nn", tm, ATT_OUT_W, D_MODEL,
                      [(_sds((S, ATT_OUT_W), F32), slot_tile), (_sds((S, ATT_OUT_W), F32), slot_tile)],
                      epi_doa, extras=[(o_a, slot_tile)])
    owed = [_chip_core(_by_owner_cols(gw_pa)), _chip_core(gw_pb.reshape(N_DEV, -1, D_MODEL)),
            _chip_core(gw_out.reshape(N_DEV, -1, D_MODEL)), _chip_core(gw_up), _chip_core(gw_down.reshape(N_DEV, -1, D_MODEL))]
    names = ("w_proj_a", "w_proj_b", "w_out", "w_up", "w_down")

    def epi_dor(acc, ex, out):
        cvs = [slice(h * RET_V, (h + 1) * RET_V) for h in range(RET_HEADS)]
        do, dgr, dg, db = _norm_gate_bwd([acc[:, cv] for cv in cvs], [ex[0][:, cv] for cv in cvs],
                                         [ex[1][:, cv].astype(F32) for cv in cvs],
                                         [ex[2][:, cv] for cv in cvs], [ex[3][:, cv] for cv in cvs])
        for h, cv in enumerate(cvs):
            out[0][:, cv] = do[h].astype(BF16)
            out[1][:, cv] = dgr[h].astype(BF16)
        first = pl.program_id(0) == 0
        _acc_rows(out[2], jnp.concatenate(dg, axis=1), first)
        _acc_rows(out[3], jnp.concatenate(db, axis=1), first)

    wide_tile = _tile_ij(ts, RET_V_W)
    gate_tile = pl.BlockSpec((ts, RET_V_W), lambda i, j, k: (i, R_G // RET_V_W))
    wide_vec = pl.BlockSpec((1, RET_V_W), lambda i, j, k: (0, 0))
    (do_ret, dz_rest, g_gn_g, g_gn_b), got = _mm(
        "proj_b_bwd", dpb, w_pb_t, "nn", ts, RET_V_W, D_MODEL,
        [(_sds((S, RET_V_W), BF16), wide_tile), (_sds((S, REST_W), BF16), gate_tile),
         (_sds((1, RET_V_W), F32), wide_vec), (_sds((1, RET_V_W), F32), wide_vec)],
        epi_dor, extras=[(o_pre, wide_tile), (z_rest, gate_tile), (gn_g, wide_vec), (gn_b, wide_vec)],
        exchange=_pair_exchange(owed), in_place=(dz_rest, 1))
    chip_sums = [_pair_sum(f"pair_sum_{n}", g, r, core, min(256, g.shape[2])) for n, g, r in zip(names, owed, got)]

    dz_rest = _ret_bwd(do_ret, states, z_rest, dz_rest)
    dz_att, gq_parts, gk_parts, parts_late = None, [], [], None
    for gi, ((_, d), nb) in enumerate(zip(ATT_GROUPS, ATT_BLOCKS_PER_STEP)):
        last = gi == len(ATT_GROUPS) - 1
        (dz_att, gq_p, gk_p), parts = _att_bwd_group(z_att, dz_att, do_a, delta, lmix, gq3, gk3, slopes3, gi, d, nb,
                                                     exchange=_chip_exchange(chip_sums) if last else None)
        parts_late = parts if last else parts_late
        gq_parts.append(gq_p)
        gk_parts.append(gk_p)
    g_qn = jnp.concatenate(gq_parts, axis=0).reshape(1, ATT_HEADS, DH)
    g_kn = jnp.concatenate(gk_parts, axis=0).reshape(1, ATT_HEADS, DH)

    gw_att = _mm("gw_in_att", xn, dz_att, "tn", D_MODEL, ATT_W, tk,
                 [(_sds((D_MODEL, 3 * ATT_W), BF16), _tile_ij(D_MODEL, ATT_W))], _epi_store(BF16))[0]
    gw_rest = _mm("gw_in_rest", xn, dz_rest, "tn", D_MODEL, 1024, tk,
                  [(_sds((D_MODEL, REST_W), BF16), _tile_ij(D_MODEL, 1024))], _epi_store(BF16))[0]
    gw_att = gw_att.reshape(D_MODEL, ATT_HEADS, 3, DH).transpose(0, 2, 1, 3).reshape(D_MODEL, 3 * ATT_W)
    gw_in = jnp.concatenate([gw_att, gw_rest], axis=1)
    owed_in = _chip_core(_by_owner_cols(gw_in))
    (dxn_att,), (got_in,) = _mm("in_proj_att_bwd", dz_att, w_att_t, "nn", tm, D_MODEL, 3 * ATT_W,
                                [(_sds((S, D_MODEL), F32), row_tile)], _epi_store(F32), exchange=_pair_exchange([owed_in]))
    chip_sum_in = _pair_sum("pair_sum_w_in", owed_in, got_in, core, min(256, owed_in.shape[2]))

    def epi_norm1(acc, ex, out):
        dx, dg = _rms_bwd(acc + ex[0][...], ex[1][...], ex[3][...])
        out[0][...] = ex[2][...] + dx
        _acc_rows(out[1], dg, pl.program_id(0) == 0)

    short_tile = _tile_ij(ts, D_MODEL)
    (grad_x, g_norm1), parts_in = _mm(
        "in_proj_rest_bwd", dz_rest, w_rest_t, "nn", ts, D_MODEL, REST_W,
        [(_sds((S, D_MODEL), F32), short_tile), (_sds((1, D_MODEL), F32), vec)],
        epi_norm1, extras=[(dxn_att, short_tile), (x, short_tile), (dx1, short_tile), (norm1_g, vec)],
        exchange=_chip_exchange([chip_sum_in]))

    small = (g_norm1, g_qn, g_kn, g_gn_g, g_gn_b, g_norm2)
    return loss_local, grad_x, list(parts_in) + list(parts_late), small


def _position():
    return lax.axis_index("x"), lax.axis_index("y"), lax.axis_index("c")


def _other_chips(x, y):
    return [(1 - x, y), (x, 1 - y), (1 - x, 1 - y)]


_ANY = pl.BlockSpec(memory_space=pl.ANY)


def _gather_exchange(shards):
    nw = len(shards)

    def plan(x_refs, out_refs, sems):
        send_sems, recv_sems, local_sems = sems
        x, y, c = _position()
        me, sibling = (x, y, c), (x, y, 1 - c)
        chips = _other_chips(x, y)

        def copy(w, k, block, to, own=False):
            px, py, pc = block
            rows = out_refs[w].at[4 * px + 2 * py + pc]
            return pltpu.make_async_remote_copy(
                src_ref=x_refs[w] if own else rows, dst_ref=rows,
                send_sem=send_sems.at[7 * w + k], recv_sem=recv_sems.at[7 * w + k], device_id=to, device_id_type=MESH)

        def mine(w):
            return pltpu.make_async_copy(x_refs[w], out_refs[w].at[4 * x + 2 * y + c], local_sems.at[w])

        def own_sends(w):
            return [copy(w, 0, me, sibling, own=True)] + [copy(w, 1 + j, me, (*chip, c), own=True)
                                                          for j, chip in enumerate(chips)]

        def start():
            for w in range(nw):
                mine(w).start()
                for cp in own_sends(w):
                    cp.start()

        def relay():
            for j, chip in enumerate(chips):
                for w in range(nw):
                    copy(w, 1 + j, (*chip, c), me).wait_recv()
                    copy(w, 4 + j, (*chip, c), sibling).start()

        def finish():
            for w in range(nw):
                copy(w, 0, sibling, me).wait_recv()
                for j, chip in enumerate(chips):
                    copy(w, 4 + j, (*chip, 1 - c), me).wait_recv()
            for w in range(nw):
                for cp in own_sends(w):
                    cp.wait_send()
                for j, chip in enumerate(chips):
                    copy(w, 4 + j, (*chip, c), sibling).wait_send()
                mine(w).wait()

        return start, relay, finish

    return _Exchange(shards, [_sds((N_DEV,) + s.shape, s.dtype) for s in shards],
                     [pltpu.SemaphoreType.DMA((7 * nw,)), pltpu.SemaphoreType.DMA((7 * nw,)),
                      pltpu.SemaphoreType.DMA((nw,))], plan)


def _run_exchange(name, exchange):
    n_in, n_out = len(exchange.ins), len(exchange.out_shapes)

    def body(*refs):
        start, relay, finish = exchange.split(refs[:n_in], refs[n_in:n_in + n_out], refs[n_in + n_out:])
        start()
        if relay is not None:
            relay()
        finish()

    return pl.pallas_call(
        body, name=name, out_shape=exchange.out_shapes,
        in_specs=[_ANY] * n_in, out_specs=[_ANY] * n_out, scratch_shapes=exchange.sems,
    )(*exchange.ins)


def _pair_exchange(grads):
    ng = len(grads)

    def plan(g_refs, out_refs, sems):
        send_sems, recv_sems = sems
        x, y, c = _position()

        def copies():
            return [pltpu.make_async_remote_copy(
                src_ref=g_refs[w].at[:, 1 - c], dst_ref=out_refs[w], send_sem=send_sems.at[w],
                recv_sem=recv_sems.at[w], device_id=(x, y, 1 - c), device_id_type=MESH) for w in range(ng)]

        def start():
            for cp in copies():
                cp.start()

        def finish():
            for cp in copies():
                cp.wait()

        return start, None, finish

    return _Exchange(grads, [_sds((N_CHIP,) + g.shape[2:], g.dtype) for g in grads],
                     [pltpu.SemaphoreType.DMA((ng,)), pltpu.SemaphoreType.DMA((ng,))], plan)


def _small_all_gather(small):
    def plan(in_refs, out_refs, sems):
        (s_ref,), (s_out,) = in_refs, out_refs
        send_sems, recv_sems, local_sem = sems
        x, y, c = _position()
        me_id = 4 * x + 2 * y + c
        flips = [(a, b, e) for a in (0, 1) for b in (0, 1) for e in (0, 1)][1:]
        peers = [(x ^ a, y ^ b, c ^ e) for a, b, e in flips]

        def start():
            pltpu.make_async_copy(s_ref, s_out.at[me_id], local_sem).start()
            for k, p in enumerate(peers):
                pltpu.make_async_remote_copy(
                    src_ref=s_ref, dst_ref=s_out.at[me_id], send_sem=send_sems.at[k], recv_sem=recv_sems.at[k],
                    device_id=p, device_id_type=MESH).start()

        def finish():
            for k, (px, py, pc) in enumerate(peers):
                pltpu.make_async_remote_copy(
                    src_ref=s_ref, dst_ref=s_out.at[4 * px + 2 * py + pc], send_sem=send_sems.at[k],
                    recv_sem=recv_sems.at[k], device_id=(px, py, pc), device_id_type=MESH).wait()
            pltpu.make_async_copy(s_ref, s_out.at[me_id], local_sem).wait()

        return start, None, finish

    return _run_exchange("small_grad_all_gather", _Exchange(
        [small], [_sds((N_DEV,) + small.shape, small.dtype)],
        [pltpu.SemaphoreType.DMA((7,)), pltpu.SemaphoreType.DMA((7,)), pltpu.SemaphoreType.DMA], plan))[0]


def _pair_sum(name, g, got, core, tr):
    n_chip, _, R, C = g.shape

    def body(c_ref, a_ref, b_ref, o_ref):
        del c_ref
        o_ref[...] = (a_ref[...].astype(F32) + b_ref[...].astype(F32)).astype(o_ref.dtype)

    return pl.pallas_call(
        body, name=name,
        grid_spec=pltpu.PrefetchScalarGridSpec(
            num_scalar_prefetch=1, grid=(n_chip, R // tr),
            in_specs=[pl.BlockSpec((None, None, tr, C), lambda ch, i, c_ref: (ch, c_ref[0], i, 0)),
                      pl.BlockSpec((None, tr, C), lambda ch, i, c_ref: (ch, i, 0))],
            out_specs=pl.BlockSpec((None, tr, C), lambda ch, i, c_ref: (ch, i, 0))),
        out_shape=_sds(got.shape, got.dtype),
        compiler_params=_cparams(("parallel", "parallel")),
    )(core, g, got)


def _chip_exchange(parts):
    ng = len(parts)

    def plan(p_refs, out_refs, sems):
        send_sems, recv_sems, local_sems = sems
        x, y, c = _position()
        my_chip = 2 * x + y

        def copies():
            local = [pltpu.make_async_copy(p_refs[w].at[my_chip], out_refs[w].at[my_chip], local_sems.at[w])
                     for w in range(ng)]
            remote = [pltpu.make_async_remote_copy(
                src_ref=p_refs[w].at[2 * cx + cy], dst_ref=out_refs[w].at[my_chip],
                send_sem=send_sems.at[3 * w + k], recv_sem=recv_sems.at[3 * w + k],
                device_id=(cx, cy, c), device_id_type=MESH)
                for w in range(ng) for k, (cx, cy) in enumerate(_other_chips(x, y))]
            return local + remote

        def start():
            for cp in copies():
                cp.start()

        def finish():
            for cp in copies():
                cp.wait()

        return start, None, finish

    return _Exchange(parts, [_sds(p.shape, p.dtype) for p in parts],
                     [pltpu.SemaphoreType.DMA((3 * ng,)), pltpu.SemaphoreType.DMA((3 * ng,)),
                      pltpu.SemaphoreType.DMA((ng,))], plan)


def _adamw(name, parts, w, m, v, tr):
    n_parts = parts.shape[0]
    _, R, C = w.shape

    def body(p_ref, w_ref, m_ref, v_ref, g_ref, d_ref, mo_ref, vo_ref):
        g = p_ref[0].astype(F32)
        for i in range(1, n_parts):
            g = g + p_ref[i].astype(F32)
        m_new = ADAM_B1 * m_ref[...] + (1.0 - ADAM_B1) * g
        v_new = ADAM_B2 * v_ref[...] + (1.0 - ADAM_B2) * (g * g)
        m_hat = m_new / (1.0 - ADAM_B1 ** ADAM_STEP)
        v_hat = v_new / (1.0 - ADAM_B2 ** ADAM_STEP)
        g_ref[...] = g
        d_ref[...] = -ADAM_LR * (m_hat / (jnp.sqrt(v_hat) + ADAM_EPS) + ADAM_WD * w_ref[...])
        mo_ref[...] = m_new
        vo_ref[...] = v_new

    tile = pl.BlockSpec((None, tr, C), lambda i: (0, i, 0))
    return pl.pallas_call(
        body, name=name, grid=(R // tr,),
        in_specs=[pl.BlockSpec((n_parts, tr, C), lambda i: (0, i, 0)), tile, tile, tile],
        out_specs=[tile] * 4,
        out_shape=[_sds((1, R, C), F32)] * 4,
        compiler_params=_cparams(("parallel",)),
    )(parts, w, m, v)


def _flat_small(arrs):
    return jnp.concatenate([a.reshape(-1) for a in arrs]).reshape(SMALL_ROWS, LANES)


def _cols(g):
    return g.transpose(1, 0, 2).reshape(g.shape[1], -1)


def _rows_of(g):
    return g.reshape(-1, g.shape[2])


def _by_owner_cols(g):
    rows, cols = g.shape
    return g.reshape(rows, N_DEV, cols // N_DEV).transpose(1, 0, 2)


def _chip_core(g):
    return g.reshape((N_CHIP, 2) + g.shape[1:])


def kernel(x, norm1_g, w_in, q_norm_g, k_norm_g, ret_gn_g, ret_gn_b, w_proj_a, w_proj_b, w_out, norm2_g, w_up, w_down, loss_target, m_norm1_g, m_w_in, m_q_norm_g, m_k_norm_g, m_ret_gn_g, m_ret_gn_b, m_w_proj_a, m_w_proj_b, m_w_out, m_norm2_g, m_w_up, m_w_down, v_norm1_g, v_w_in, v_q_norm_g, v_k_norm_g, v_ret_gn_g, v_ret_gn_b, v_w_proj_a, v_w_proj_b, v_w_out, v_norm2_g, v_w_up, v_w_down):
    big_w = (w_in, w_proj_a, w_proj_b, w_out, w_up, w_down)
    big_m = (m_w_in, m_w_proj_a, m_w_proj_b, m_w_out, m_w_up, m_w_down)
    big_v = (v_w_in, v_w_proj_a, v_w_proj_b, v_w_out, v_w_up, v_w_down)
    small_w = (norm1_g, q_norm_g, k_norm_g, ret_gn_g, ret_gn_b, norm2_g)
    small_m = (m_norm1_g, m_q_norm_g, m_k_norm_g, m_ret_gn_g, m_ret_gn_b, m_norm2_g)
    small_v = (v_norm1_g, v_q_norm_g, v_k_norm_g, v_ret_gn_g, v_ret_gn_b, v_norm2_g)

    shards = [w[0].astype(BF16) for w in big_w]
    core = lax.axis_index("c").astype(jnp.int32).reshape(1)
    loss_local, grad_x, parts, small_g = _step(
        x[0], loss_target[0], norm1_g, q_norm_g[0], k_norm_g[0], ret_gn_g, ret_gn_b, norm2_g, shards, core)
    small_all = _small_all_gather(_flat_small(small_g))
    names = ("w_in", "w_proj_a", "w_proj_b", "w_out", "w_up", "w_down")

    res = {}
    for n, p, w, m, v in zip(names, parts, big_w, big_m, big_v):
        res[n] = _adamw(f"adamw_{n}", p, w, m, v, min(128, w.shape[1]))
    s_outs = _adamw("adamw_small", small_all, _flat_small(small_w)[None], _flat_small(small_m)[None],
                    _flat_small(small_v)[None], SMALL_ROWS)
    small_names = ("norm1_g", "q_norm_g", "k_norm_g", "ret_gn_g", "ret_gn_b", "norm2_g")
    for n in small_names:
        res[n] = []
    for o in s_outs:
        flat, off = o.reshape(-1), 0
        for n, w in zip(small_names, small_w):
            res[n].append(flat[off:off + w.size].reshape(w.shape))
            off += w.size

    order = ("norm1_g", "w_in", "q_norm_g", "k_norm_g", "ret_gn_g", "ret_gn_b", "w_proj_a", "w_proj_b", "w_out",
             "norm2_g", "w_up", "w_down")
    loss = lax.psum(loss_local, MESH_AXES)
    return (loss, grad_x[None], *[res[n][0] for n in order], *[res[n][1] for n in order],
            *[res[n][2] for n in order], *[res[n][3] for n in order])
```

```python
import math

import numpy as np
import jax
import jax.numpy as jnp
from jax import lax
from jax.experimental import pallas as pl
from jax.experimental.pallas import tpu as pltpu

F32 = jnp.float32
BF16 = jnp.bfloat16

D_MODEL = 1024
ATT_GROUPS = ((128, 1), (512, 4), (2048, 16))
ATT_BLOCKS_PER_STEP = (8, 2, 1)
ATT_TOGETHER = 4
HPG = 4
ATT_HEADS = 12
DH = 128
BLK = 128
ATT_W = ATT_HEADS * DH
ATT_OUT_W = HPG * DH
RET_HEADS = 4
RET_QK = 256
RET_V = 512
RET_QK_W = RET_HEADS * RET_QK
RET_V_W = RET_HEADS * RET_V
CHUNK = 128
D_FF = 4096
IN_W = 12800
REST_W = IN_W - 3 * ATT_W
EPS = 1e-6
ADAM_LR, ADAM_B1, ADAM_B2, ADAM_EPS, ADAM_WD, ADAM_STEP = 0.001, 0.9, 0.999, 1e-08, 0.01, 10
N_DEV = 8
N_CHIP = 4
MESH_AXES = ("x", "y", "c")
MESH = pl.DeviceIdType.MESH
VMEM_LIMIT = 56 * 1024 * 1024
LANES = 128
NEG = -1e30

_NN = (((1,), (0,)), ((), ()))
_NT = (((1,), (1,)), ((), ()))
_TN = (((0,), (0,)), ((), ()))

R_Q, R_K, R_V, R_G, R_GA, R_GB = 0, 1024, 2048, 4096, 6144, 7168

LOG_GAMMA = [float(v) for v in np.log(1.0 - 2.0 ** (-5.0 - np.arange(RET_HEADS, dtype=np.float32))).astype(np.float32)]
ALIBI = np.asarray(2.0 ** (-8.0 * np.arange(1, ATT_HEADS + 1, dtype=np.float32) / ATT_HEADS), np.float32)

SMALL_ROWS = (1024 + 1536 + 1536 + 2048 + 2048 + 1024) // LANES


def _dot(a, b, dims=_NN):
    return lax.dot_general(a, b, dims, preferred_element_type=F32)


def _cparams(sem):
    return pltpu.CompilerParams(dimension_semantics=sem, vmem_limit_bytes=VMEM_LIMIT)


def _sds(shape, dtype):
    return jax.ShapeDtypeStruct(shape, dtype)


class _Exchange:
    def __init__(self, ins, out_shapes, sems, plan):
        self.ins, self.out_shapes, self.sems, self.plan = list(ins), list(out_shapes), list(sems), plan

    def split(self, in_refs, out_refs, sem_refs):
        return self.plan(in_refs, out_refs, sem_refs)


def _mm(name, a, b, mode, tm, tn, tk, outs, epi, extras=(), b_pro=None,
        sem=("parallel", "parallel", "arbitrary"), exchange=None, in_place=None):
    if mode == "nn":
        (M, K), (_, N) = a.shape, b.shape
        a_spec = pl.BlockSpec((tm, tk), lambda i, j, k: (i, k))
        dims = _NN
    else:
        (K, M), (_, N) = a.shape, b.shape
        a_spec = pl.BlockSpec((tk, tm), lambda i, j, k: (k, i))
        dims = _TN
    assert M % tm == 0 and N % tn == 0 and K % tk == 0, (name, M, N, K, tm, tn, tk)
    nk = K // tk
    whole_b = dict(pipeline_mode=pl.Buffered(1)) if (nk == 1 and N == tn) else {}
    b_spec = pl.BlockSpec((tk, tn), lambda i, j, k: (k, j), **whole_b)
    n_ex, n_out = len(extras), len(outs)
    grid = (M // tm, N // tn, nk)
    n_xi = len(exchange.ins) if exchange else 0
    n_xo = len(exchange.out_shapes) if exchange else 0
    n_acc = 1 if nk > 1 else 0

    n_ip = 1 if in_place else 0

    def body(a_ref, b_ref, *rest):
        ex, rest = rest[:n_ex], rest[n_ex:]
        x_in, rest = rest[:n_xi], rest[n_xi + n_ip:]
        out, rest = rest[:n_out], rest[n_out:]
        x_out, rest = rest[:n_xo], rest[n_xo:]
        step =(pl.program_id(0) * grid[1] + pl.program_id(1)) * grid[2] + pl.program_id(2)
        n_steps = grid[0] * grid[1] * grid[2]
        if exchange:
            start, relay, finish = exchange.split(x_in, x_out, rest[n_acc:])
            pl.when(step == 0)(start)
        bv = b_ref[...]
        if b_pro is not None:
            bv = b_pro(bv)
        part = _dot(a_ref[...].astype(BF16), bv.astype(BF16), dims)
        if nk == 1:
            epi(part, ex, out)
        else:
            acc_ref = rest[0]
            k = pl.program_id(2)

            @pl.when(k == 0)
            def _():
                acc_ref[...] = part

            @pl.when(k > 0)
            def _():
                acc_ref[...] += part

            @pl.when(k == nk - 1)
            def _():
                epi(acc_ref[...], ex, out)
        if exchange:
            if relay is not None:
                pl.when(step == (7 * n_steps) // 8)(relay)
            pl.when(step == n_steps - 1)(finish)

    res = pl.pallas_call(
        body,
        name=name,
        grid=grid,
        in_specs=[a_spec, b_spec] + [s for _, s in extras] + [_ANY] * (n_xi + n_ip),
        out_specs=[s for _, s in outs] + [_ANY] * n_xo,
        out_shape=[o for o, _ in outs] + (exchange.out_shapes if exchange else []),
        scratch_shapes=([pltpu.VMEM((tm, tn), F32)] if nk > 1 else []) + (exchange.sems if exchange else []),
        input_output_aliases={2 + n_ex + n_xi: in_place[1]} if in_place else {},
        compiler_params=_cparams(("arbitrary",) * 3 if exchange else sem),
    )(a, b, *[e for e, _ in extras], *(exchange.ins if exchange else []), *([in_place[0]] if in_place else []))
    return (res[:n_out], res[n_out:]) if exchange else res


def _tile_ij(tm, tn):
    return pl.BlockSpec((tm, tn), lambda i, j, k: (i, j))


def _epi_store(dtype):
    def epi(acc, ex, out):
        out[0][...] = acc.astype(dtype)
    return epi


def _rms_rows(x):
    return lax.rsqrt(jnp.mean(x * x, axis=-1, keepdims=True) + EPS)


def _acc_rows(ref, part, first):
    @pl.when(first)
    def _():
        ref[...] = part

    @pl.when(jnp.logical_not(first))
    def _():
        ref[...] += part


def _rmsnorm_fwd(x, g, tm, exchange):
    S, Dm = x.shape
    n_steps = S // tm
    n_xi, n_xo = len(exchange.ins), len(exchange.out_shapes)

    def body(x_ref, g_ref, *rest):
        x_in, o_ref, x_out, sems = rest[:n_xi], rest[n_xi], rest[n_xi + 1:n_xi + 1 + n_xo], rest[n_xi + 1 + n_xo:]
        start, relay, finish = exchange.split(x_in, x_out, sems)
        step = pl.program_id(0)
        pl.when(step == 0)(start)
        xv = x_ref[...]
        o_ref[...] = (xv * _rms_rows(xv) * g_ref[...]).astype(BF16)

        @pl.when(step == n_steps - 1)
        def _():
            relay()
            finish()

    res = pl.pallas_call(
        body, name="rmsnorm1_fwd", grid=(n_steps,),
        in_specs=[pl.BlockSpec((tm, Dm), lambda i: (i, 0)), pl.BlockSpec((1, Dm), lambda i: (0, 0))] + [_ANY] * n_xi,
        out_specs=[pl.BlockSpec((tm, Dm), lambda i: (i, 0))] + [_ANY] * n_xo,
        out_shape=[_sds((S, Dm), BF16)] + exchange.out_shapes,
        scratch_shapes=exchange.sems,
        compiler_params=_cparams(("arbitrary",)),
    )(x, g, *exchange.ins)
    return res[0], res[1:]


def _rows(ref, r, b, d):
    if d == 1:
        return ref[b * BLK:(b + 1) * BLK, :]
    return ref[pl.ds(b * BLK * d + r, BLK, stride=d), :]


def _put_rows(ref, r, b, d, val):
    if d == 1:
        ref[b * BLK:(b + 1) * BLK, :] = val
    else:
        ref[pl.ds(b * BLK * d + r, BLK, stride=d), :] = val


def _head_norm(x, g):
    r = _rms_rows(x)
    xh = x * r
    return (xh * g).astype(BF16), xh, r


def _head_norm_bwd(dyn, xh, r, g):
    dxh = dyn * g
    dx = r * (dxh - xh * jnp.mean(dxh * xh, axis=-1, keepdims=True))
    return dx, jnp.sum(dyn * xh, axis=0, keepdims=True)


def _att_mask_bias(slope, d, first):
    qi = lax.broadcasted_iota(jnp.int32, (BLK, 2 * BLK), 0)
    kj = lax.broadcasted_iota(jnp.int32, (BLK, 2 * BLK), 1)
    dist = BLK + qi - kj
    valid = (dist >= 0) & (dist <= BLK)
    if first is not None:
        valid = valid & (jnp.logical_not(first) | (kj >= BLK))
    bias = -slope * (dist * d).astype(F32)
    return valid, bias


def _att_specs(gi, d, nb, S):
    span = BLK * d
    sb = span * nb
    nspan = S // span
    before = lambda n: jnp.maximum(n * nb - 1, 0)
    after = lambda n: jnp.minimum((n + 1) * nb, nspan - 1)
    zcol = lambda j, kind: 3 * (gi * HPG + j) + kind
    cur = lambda kind: pl.BlockSpec((sb, DH), lambda j, n: (n, zcol(j, kind)))
    prev = lambda kind: pl.BlockSpec((span, DH), lambda j, n: (before(n), zcol(j, kind)))
    nxt = lambda kind: pl.BlockSpec((span, DH), lambda j, n: (after(n), zcol(j, kind)))
    slot = pl.BlockSpec((sb, DH), lambda j, n: (n, j))
    slot_next = pl.BlockSpec((span, DH), lambda j, n: (after(n), j))
    head = pl.BlockSpec((None, 1, DH), lambda j, n: (gi * HPG + j, 0, 0))
    return cur, prev, nxt, slot, slot_next, head


def _att_fwd_group(z_att, gq3, gk3, slopes3, gi, d, nb, others=()):
    S = z_att.shape[0]
    nsb = S // (BLK * d * nb)
    scale = DH ** -0.5
    n_other = len(others)

    def body(q_ref, k_ref, v_ref, kp_ref, vp_ref, gq_ref, gk_ref, sl_ref, *rest):
        other_refs, (o_ref, l_ref) = rest[:2 * n_other], rest[2 * n_other:]
        slope = sl_ref[...][:, :1]
        valid0, bias = _att_mask_bias(slope, d, pl.program_id(1) == 0)
        valid_in, _ = _att_mask_bias(slope, d, None)
        gq, gk = gq_ref[...], gk_ref[...]
        memo = {}

        def get(kind, r, b):
            if (kind, r, b) not in memo:
                if kind == "k":
                    val = _head_norm(_rows(k_ref if b >= 0 else kp_ref, r, max(b, 0), d), gk)[0]
                else:
                    val = _rows(v_ref if b >= 0 else vp_ref, r, max(b, 0), d).astype(BF16)
                memo[(kind, r, b)] = val
            return memo[(kind, r, b)]

        units = [(r, b) for r in range(d) for b in range(nb)]
        for c0 in range(0, len(units), ATT_TOGETHER):
            us = units[c0:c0 + ATT_TOGETHER]
            q = [_head_norm(_rows(q_ref, r, b, d), gq)[0] for r, b in us]
            k2 = [jnp.concatenate([get("k", r, b - 1), get("k", r, b)], axis=0) for r, b in us]
            v2 = [jnp.concatenate([get("v", r, b - 1), get("v", r, b)], axis=0) for r, b in us]
            s = [jnp.where(valid0 if b == 0 else valid_in, _dot(q[i], k2[i], _NT) * scale + bias, NEG)
                 for i, (r, b) in enumerate(us)]
            m = [jnp.max(si, axis=-1, keepdims=True) for si in s]
            p = [jnp.exp(si - mi) for si, mi in zip(s, m)]
            den = [jnp.sum(pi, axis=-1, keepdims=True) for pi in p]
            o = [_dot(pi.astype(BF16), vi) / di for pi, vi, di in zip(p, v2, den)]
            for i, (r, b) in enumerate(us):
                _put_rows(o_ref, r, b, d, o[i])
                _put_rows(l_ref, r, b, d, jnp.broadcast_to(m[i] + jnp.log(den[i]), (BLK, DH)))
        if n_other:
            os_ = [ref[...] for ref in other_refs[:n_other]] + [o_ref[...]]
            ls_ = [ref[...] for ref in other_refs[n_other:]] + [l_ref[...]]
            m = ls_[0]
            for l in ls_[1:]:
                m = jnp.maximum(m, l)
            es = [jnp.exp(l - m) for l in ls_]
            tot, mix = es[0], es[0] * os_[0]
            for e, o in zip(es[1:], os_[1:]):
                tot, mix = tot + e, mix + e * o
            o_ref[...] = mix / tot
            l_ref[...] = m + jnp.log(tot)

    cur, prev, _, slot, _, head = _att_specs(gi, d, nb, S)
    return pl.pallas_call(
        body, name=f"att_fwd_g{gi}", grid=(HPG, nsb),
        in_specs=[cur(0), cur(1), cur(2), prev(1), prev(2), head, head, head] + [slot] * (2 * n_other),
        out_specs=[slot, slot],
        out_shape=[_sds((S, ATT_OUT_W), F32), _sds((S, ATT_OUT_W), F32)],
        compiler_params=_cparams(("parallel", "arbitrary")),
    )(z_att, z_att, z_att, z_att, z_att, gq3, gk3, slopes3, *[o for o, _ in others], *[l for _, l in others])


def _ret_tables(lg):
    ri = lax.broadcasted_iota(jnp.int32, (CHUNK, CHUNK), 0)
    ci = lax.broadcasted_iota(jnp.int32, (CHUNK, CHUNK), 1)
    diff = (ri - ci).astype(F32)
    decay = jnp.where(diff >= 0, jnp.exp(lg * jnp.maximum(diff, 0.0)), 0.0)
    idx = lax.broadcasted_iota(jnp.int32, (CHUNK, 1), 0).astype(F32)
    xi = jnp.exp(lg * (idx + 1.0))
    zeta = jnp.exp(lg * (CHUNK - 1.0 - idx))
    return decay, xi, zeta, math.exp(lg * CHUNK)


def _ret_specs(nch, rev):
    idx = (lambda n: nch - 1 - n) if rev else (lambda n: n)
    qk = lambda off: pl.BlockSpec((CHUNK, RET_QK_W), lambda n: (idx(n), off // RET_QK_W))
    vv = lambda off: pl.BlockSpec((CHUNK, RET_V_W), lambda n: (idx(n), off // RET_V_W))
    par = pl.BlockSpec((1, RET_V_W), lambda n: (0, 0))
    wide = pl.BlockSpec((CHUNK, RET_V_W), lambda n: (idx(n), 0))
    st = pl.BlockSpec((RET_HEADS, None, RET_QK, RET_V), lambda n: (0, idx(n), 0, 0))
    return qk, vv, par, wide, st


def _ret_fwd(z_rest, gn_g, gn_b):
    S = z_rest.shape[0]
    nch = S // CHUNK

    def body(q_ref, k_ref, v_ref, gr_ref, g_ref, b_ref, or_ref, o_ref, st_ref, state):
        @pl.when(pl.program_id(0) == 0)
        def _():
            state[...] = jnp.zeros_like(state)

        for h in range(RET_HEADS):
            decay, xi, zeta, gch = _ret_tables(LOG_GAMMA[h])
            cq = slice(h * RET_QK, (h + 1) * RET_QK)
            cv = slice(h * RET_V, (h + 1) * RET_V)
            q = q_ref[:, cq]
            kc32 = k_ref[:, cq].astype(F32) * (RET_QK ** -0.5)
            kc = kc32.astype(BF16)
            v = v_ref[:, cv]
            st = state[h]
            stb = st.astype(BF16)
            st_ref[h] = stb
            s = _dot(q, kc, _NT) * decay
            o = _dot(s.astype(BF16), v) + _dot(q, stb) * xi
            state[h] = st * gch + _dot((kc32 * zeta).astype(BF16), v, _TN)
            mu = jnp.mean(o, axis=-1, keepdims=True)
            cen = o - mu
            yh = cen * lax.rsqrt(jnp.mean(cen * cen, axis=-1, keepdims=True) + EPS)
            gr = gr_ref[:, cv].astype(F32)
            or_ref[:, cv] = ((yh * g_ref[:, cv] + b_ref[:, cv]) * (gr * jax.nn.sigmoid(gr))).astype(BF16)
            o_ref[:, cv] = o

    qk, vv, par, wide, st = _ret_specs(nch, False)
    return pl.pallas_call(
        body, name="ret_fwd", grid=(nch,),
        in_specs=[qk(R_Q), qk(R_K), vv(R_V), vv(R_G), par, par],
        out_specs=[wide, wide, st],
        out_shape=[_sds((S, RET_V_W), BF16), _sds((S, RET_V_W), F32), _sds((RET_HEADS, nch, RET_QK, RET_V), BF16)],
        scratch_shapes=[pltpu.VMEM((RET_HEADS, RET_QK, RET_V), F32)],
        compiler_params=_cparams(("arbitrary",)),
    )(z_rest, z_rest, z_rest, z_rest, gn_g, gn_b)


def _merge_fwd(o_a, o_r, z_rest, x, wpa, wpb, wout, g2, tm):
    S = x.shape[0]

    def body(oa_ref, or_ref, ga_ref, gb_ref, x_ref, wpa_ref, wpb_ref, wo_ref, g2_ref,
             x1_ref, y_ref, pa_ref, pb_ref, xn2_ref):
        pa = _dot(oa_ref[...].astype(BF16), wpa_ref[...])
        pb = _dot(or_ref[...], wpb_ref[...])
        y = jax.nn.sigmoid(ga_ref[...].astype(F32)) * pa + jax.nn.sigmoid(gb_ref[...].astype(F32)) * pb
        yb = y.astype(BF16)
        x1 = x_ref[...] + _dot(yb, wo_ref[...])
        x1_ref[...] = x1
        y_ref[...] = yb
        pa_ref[...] = pa.astype(BF16)
        pb_ref[...] = pb.astype(BF16)
        xn2_ref[...] = (x1 * _rms_rows(x1) * g2_ref[...]).astype(BF16)

    row = lambda w: pl.BlockSpec((tm, w), lambda i: (i, 0))
    full = lambda a: pl.BlockSpec(a.shape, lambda i: (0, 0))
    return pl.pallas_call(
        body, name="merge_fwd", grid=(S // tm,),
        in_specs=[row(ATT_OUT_W), row(RET_V_W),
                  pl.BlockSpec((tm, D_MODEL), lambda i: (i, R_GA // D_MODEL)),
                  pl.BlockSpec((tm, D_MODEL), lambda i: (i, R_GB // D_MODEL)),
                  row(D_MODEL), full(wpa), full(wpb), full(wout), full(g2)],
        out_specs=[row(D_MODEL)] * 5,
        out_shape=[_sds((S, D_MODEL), F32)] + [_sds((S, D_MODEL), BF16)] * 4,
        compiler_params=_cparams(("parallel",)),
    )(o_a, o_r, z_rest, z_rest, x, wpa, wpb, wout, g2)


def _rms_bwd(dy, xv, g):
    r = _rms_rows(xv)
    xh = xv * r
    dg = dy * g
    dx = r * (dg - xh * jnp.mean(dg * xh, axis=-1, keepdims=True))
    return dx, jnp.sum(dy * xh, axis=0, keepdims=True)


def _norm_gate_bwd(dout, o, gr, gam, bet):
    mean = lambda xs: [jnp.mean(x, axis=-1, keepdims=True) for x in xs]
    cen = [x - m for x, m in zip(o, mean(o))]
    rstd = [lax.rsqrt(v + EPS) for v in mean([c * c for c in cen])]
    yh = [c * r for c, r in zip(cen, rstd)]
    y = [a * g + b for a, g, b in zip(yh, gam, bet)]
    sg = [jax.nn.sigmoid(g) for g in gr]
    dy = [d * (g * s) for d, g, s in zip(dout, gr, sg)]
    dgr = [d * a * (s * (1.0 + g * (1.0 - s))) for d, a, s, g in zip(dout, y, sg, gr)]
    dyh = [d * g for d, g in zip(dy, gam)]
    m1, m2 = mean(dyh), mean([a * b for a, b in zip(dyh, yh)])
    do = [r * (d - a - h * b) for r, d, a, h, b in zip(rstd, dyh, m1, yh, m2)]
    dg = [jnp.sum(d * h, axis=0, keepdims=True) for d, h in zip(dy, yh)]
    db = [jnp.sum(d, axis=0, keepdims=True) for d in dy]
    return do, dgr, dg, db


def _ret_bwd(do, states, z_rest, dz_rest):
    S = z_rest.shape[0]
    nch = S // CHUNK

    def body(do_ref, st_ref, q_ref, k_ref, v_ref, dz_in, dz_ref, gst):
        del dz_in

        @pl.when(pl.program_id(0) == 0)
        def _():
            gst[...] = jnp.zeros_like(gst)

        for h in range(RET_HEADS):
            decay, xi, zeta, gch = _ret_tables(LOG_GAMMA[h])
            cq = slice(h * RET_QK, (h + 1) * RET_QK)
            cv = slice(h * RET_V, (h + 1) * RET_V)
            q = q_ref[:, cq]
            kc32 = k_ref[:, cq].astype(F32) * (RET_QK ** -0.5)
            kc = kc32.astype(BF16)
            v = v_ref[:, cv]
            dob = do_ref[:, cv]
            a = (_dot(q, kc, _NT) * decay).astype(BF16)
            da = (_dot(dob, v, _NT) * decay).astype(BF16)
            dcross = (dob.astype(F32) * xi).astype(BF16)
            g_next = gst[h]
            gb = g_next.astype(BF16)
            dq = _dot(da, kc) + _dot(dcross, st_ref[h], _NT)
            dkc = _dot(da, q, _TN)
            dkz = _dot(v, gb, _NT)
            dv = _dot(a, dob, _TN) + _dot((kc32 * zeta).astype(BF16), gb)
            gst[h] = g_next * gch + _dot(q, dcross, _TN)
            dz_ref[:, R_Q + h * RET_QK:R_Q + (h + 1) * RET_QK] = dq.astype(BF16)
            dz_ref[:, R_K + h * RET_QK:R_K + (h + 1) * RET_QK] = ((dkc + dkz * zeta) * (RET_QK ** -0.5)).astype(BF16)
            dz_ref[:, R_V + h * RET_V:R_V + (h + 1) * RET_V] = dv.astype(BF16)

    qk, vv, _, wide, st = _ret_specs(nch, True)
    return pl.pallas_call(
        body, name="ret_bwd", grid=(nch,),
        in_specs=[wide, st, qk(R_Q), qk(R_K), vv(R_V), _ANY],
        out_specs=pl.BlockSpec((CHUNK, R_G), lambda n: (nch - 1 - n, 0)),
        out_shape=_sds(dz_rest.shape, BF16),
        input_output_aliases={5: 0},
        scratch_shapes=[pltpu.VMEM((RET_HEADS, RET_QK, RET_V), F32)],
        compiler_params=_cparams(("arbitrary",)),
    )(do, states, z_rest, z_rest, z_rest, dz_rest)


def _att_probs(q, k, lmix, valid, bias):
    s = _dot(q, k, _NT) * (DH ** -0.5) + bias
    return jnp.where(valid, jnp.exp(jnp.where(valid, s, NEG) - lmix), 0.0)


def _att_bwd_group(z_att, dz_att, do_a, delta, lmix, gq3, gk3, slopes3, gi, d, nb, exchange=None):
    S = z_att.shape[0]
    sb = BLK * d * nb
    nsb = S // sb
    scale = DH ** -0.5
    aliased = dz_att is not None
    n_xi = len(exchange.ins) if exchange else 0
    n_xo = len(exchange.out_shapes) if exchange else 0

    def body(q_ref, k_ref, v_ref, kp_ref, vp_ref, qn_ref, do_ref, don_ref, dl_ref, dln_ref, lm_ref, lmn_ref,
             gq_ref, gk_ref, sl_ref, *rest):
        rest = rest[1:] if aliased else rest
        x_in, (dz_ref, dgq_ref, dgk_ref), rest = rest[:n_xi], rest[n_xi:n_xi + 3], rest[n_xi + 3:]
        x_out, stage, x_sems = rest[:n_xo], rest[n_xo], rest[n_xo + 1:]
        n = pl.program_id(1)
        step = pl.program_id(0) * nsb + n
        if exchange:
            start, _, finish = exchange.split(x_in, x_out, x_sems)
            pl.when(step == 0)(start)
        slope = sl_ref[...][:, :1]
        valid0, bias = _att_mask_bias(slope, d, n == 0)
        valid_in, _ = _att_mask_bias(slope, d, None)
        qi = lax.broadcasted_iota(jnp.int32, (BLK, BLK), 0)
        kj = lax.broadcasted_iota(jnp.int32, (BLK, BLK), 1)
        dist_n = BLK + qi - kj
        valid_n_in = dist_n <= BLK
        valid_n_last = valid_n_in & (n < nsb - 1)
        bias_n = -slope * (dist_n * d).astype(F32)
        gq, gk = gq_ref[...], gk_ref[...]
        dgq = jnp.zeros((1, DH), F32)
        dgk = jnp.zeros((1, DH), F32)
        memo = {}

        def get(kind, r, b):
            if (kind, r, b) not in memo:
                inner = 0 <= b < nb
                bb = b if inner else 0
                if kind == "q":
                    val = _head_norm(_rows(q_ref if inner else qn_ref, r, bb, d), gq)
                elif kind == "k":
                    val = _head_norm(_rows(k_ref if inner else kp_ref, r, bb, d), gk)
                elif kind == "v":
                    val = _rows(v_ref if inner else vp_ref, r, bb, d).astype(BF16)
                elif kind == "do":
                    val = _rows(do_ref if inner else don_ref, r, bb, d).astype(BF16)
                elif kind == "dl":
                    val = _rows(dl_ref if inner else dln_ref, r, bb, d)[:, :1]
                else:
                    val = _rows(lm_ref if inner else lmn_ref, r, bb, d)[:, :1]
                memo[(kind, r, b)] = val
            return memo[(kind, r, b)]

        units = [(r, b) for r in range(d) for b in range(nb)]
        for c0 in range(0, len(units), ATT_TOGETHER):
            us = units[c0:c0 + ATT_TOGETHER]
            k2 = [jnp.concatenate([get("k", r, b - 1)[0], get("k", r, b)[0]], axis=0) for r, b in us]
            v2 = [jnp.concatenate([get("v", r, b - 1), get("v", r, b)], axis=0) for r, b in us]
            p = [_att_probs(get("q", r, b)[0], k2[i], get("lm", r, b), valid0 if b == 0 else valid_in, bias)
                 for i, (r, b) in enumerate(us)]
            dp = [_dot(get("do", r, b), v2[i], _NT) for i, (r, b) in enumerate(us)]
            ds = [(p[i] * (dp[i] - get("dl", r, b)) * scale).astype(BF16) for i, (r, b) in enumerate(us)]
            dq = [_dot(ds[i], k2[i]) for i in range(len(us))]
            p_n = [_att_probs(get("q", r, b + 1)[0], get("k", r, b)[0], get("lm", r, b + 1),
                              valid_n_last if b == nb - 1 else valid_n_in, bias_n) for r, b in us]
            dp_n = [_dot(get("do", r, b + 1), get("v", r, b), _NT) for r, b in us]
            ds_n = [(p_n[i] * (dp_n[i] - get("dl", r, b + 1)) * scale).astype(BF16) for i, (r, b) in enumerate(us)]
            dk = [_dot(ds[i][:, BLK:], get("q", r, b)[0], _TN) + _dot(ds_n[i], get("q", r, b + 1)[0], _TN)
                  for i, (r, b) in enumerate(us)]
            dv = [_dot(p[i][:, BLK:].astype(BF16), get("do", r, b), _TN)
                  + _dot(p_n[i].astype(BF16), get("do", r, b + 1), _TN) for i, (r, b) in enumerate(us)]
            for i, (r, b) in enumerate(us):
                _, qh, qr = get("q", r, b)
                _, kh, kr = get("k", r, b)
                dxq, dg_q = _head_norm_bwd(dq[i], qh, qr, gq)
                dxk, dg_k = _head_norm_bwd(dk[i], kh, kr, gk)
                _put_rows(stage.at[0], r, b, d, dxq)
                _put_rows(stage.at[1], r, b, d, dxk)
                _put_rows(stage.at[2], r, b, d, dv[i])
                dgq, dgk = dgq + dg_q, dgk + dg_k
        for kind in range(3):
            dz_ref[:, kind * DH:(kind + 1) * DH] = stage[kind].astype(BF16)
        _acc_rows(dgq_ref, dgq, n == 0)
        _acc_rows(dgk_ref, dgk, n == 0)
        if exchange:
            pl.when(step == HPG * nsb - 1)(finish)

    cur, prev, nxt, slot, slot_next, head = _att_specs(gi, d, nb, S)
    gain = pl.BlockSpec((None, 1, DH), lambda j, n: (j, 0, 0))
    in_specs = [cur(0), cur(1), cur(2), prev(1), prev(2), nxt(0),
                slot, slot_next, slot, slot_next, slot, slot_next, head, head, head]
    args = [z_att] * 6 + [do_a, do_a, delta, delta, lmix, lmix, gq3, gk3, slopes3]
    if aliased:
        in_specs.append(_ANY)
        args.append(dz_att)
    res = pl.pallas_call(
        body, name=f"att_bwd_g{gi}", grid=(HPG, nsb),
        in_specs=in_specs + [_ANY] * n_xi,
        out_specs=[pl.BlockSpec((sb, 3 * DH), lambda j, n: (n, gi * HPG + j)), gain, gain] + [_ANY] * n_xo,
        out_shape=[_sds(z_att.shape, BF16), _sds((HPG, 1, DH), F32), _sds((HPG, 1, DH), F32)]
        + (exchange.out_shapes if exchange else []),
        input_output_aliases={len(args) - 1: 0} if aliased else {},
        scratch_shapes=[pltpu.VMEM((3, sb, DH), F32)] + (exchange.sems if exchange else []),
        compiler_params=_cparams(("arbitrary", "arbitrary") if exchange else ("parallel", "arbitrary")),
    )(*args, *(exchange.ins if exchange else []))
    return res[:3], res[3:]


def _step(x, target, norm1_g, q_norm_g, k_norm_g, gn_g, gn_b, norm2_g, shards, core):
    S = x.shape[0]
    tm = min(512, S)
    ts = min(256, S)
    tk = min(2048, S)
    tk2 = min(4096, S)
    later_shards = shards[1:]
    gq3 = q_norm_g.reshape(ATT_HEADS, 1, DH)
    gk3 = k_norm_g.reshape(ATT_HEADS, 1, DH)
    slopes3 = jnp.asarray(np.broadcast_to(ALIBI[:, None, None], (ATT_HEADS, 1, DH)).copy())

    xn, (g_in,) = _rmsnorm_fwd(x, norm1_g, tm, _gather_exchange(shards[:1]))
    w_in = _cols(g_in)
    w_att = w_in[:, :3 * ATT_W].reshape(D_MODEL, 3, ATT_HEADS, DH).transpose(0, 2, 1, 3).reshape(D_MODEL, 3 * ATT_W)
    w_rest = w_in[:, 3 * ATT_W:]
    w_att_t, w_rest_t = w_att.T, w_rest.T
    z_att = _mm("in_proj_att", xn, w_att, "nn", tm, 3 * ATT_W, D_MODEL,
                [(_sds((S, 3 * ATT_W), F32), _tile_ij(tm, 3 * ATT_W))], _epi_store(F32))[0]
    (z_rest,), gathered = _mm("in_proj_rest", xn, w_rest, "nn", ts, REST_W, D_MODEL,
                              [(_sds((S, REST_W), BF16), _tile_ij(ts, REST_W))], _epi_store(BF16),
                              exchange=_gather_exchange(later_shards))
    w_pa, w_pb, w_out, w_up, w_down = [f(g) for f, g in zip((_cols, _rows_of, _rows_of, _cols, _rows_of), gathered)]
    w_up_t, w_down_t, w_out_t, w_pa_t, w_pb_t = w_up.T, w_down.T, w_out.T, w_pa.T, w_pb.T
    done = []
    for gi, ((_, d), nb) in enumerate(zip(ATT_GROUPS, ATT_BLOCKS_PER_STEP)):
        last = gi == len(ATT_GROUPS) - 1
        done.append(_att_fwd_group(z_att, gq3, gk3, slopes3, gi, d, nb, others=tuple(done) if last else ()))
    o_a, lmix = done[-1]
    o_r, o_pre, states = _ret_fwd(z_rest, gn_g, gn_b)
    x1, y, pa, pb, xn2 = _merge_fwd(o_a, o_r, z_rest, x, w_pa, w_pb, w_out, norm2_g, min(256, S))

    def epi_up(acc, ex, out):
        r = jnp.maximum(acc, 0.0)
        out[0][...] = (r * r).astype(BF16)
        out[1][...] = r.astype(BF16)

    h, relu_u = _mm("mlp_up", xn2, w_up, "nn", tm, D_FF, D_MODEL,
                    [(_sds((S, D_FF), BF16), _tile_ij(tm, D_FF)), (_sds((S, D_FF), BF16), _tile_ij(tm, D_FF))], epi_up)

    def epi_down(acc, ex, out):
        diff = ex[0][...] + acc - ex[1][...]
        out[0][...] = diff * (1.0 / D_MODEL)
        out[1][...] = jnp.broadcast_to(jnp.sum(diff * diff) * (1.0 / (8 * LANES)), (8, LANES))

    row_tile = _tile_ij(tm, D_MODEL)
    dx2, loss_parts = _mm(
        "mlp_down_loss", h, w_down, "nn", tm, D_MODEL, D_FF,
        [(_sds((S, D_MODEL), F32), row_tile),
         (_sds((S // tm * 8, LANES), F32), pl.BlockSpec((8, LANES), lambda i, j, k: (i, 0)))],
        epi_down, extras=[(x1, row_tile), (target, row_tile)])
    loss_local = jnp.sum(loss_parts) * (0.5 / D_MODEL)

    def epi_du(acc, ex, out):
        out[0][...] = (acc * (2.0 * ex[0][...].astype(F32))).astype(BF16)

    du = _mm("mlp_down_bwd", dx2, w_down_t, "nn", tm, D_FF, D_MODEL,
             [(_sds((S, D_FF), BF16), _tile_ij(tm, D_FF))], epi_du, extras=[(relu_u, _tile_ij(tm, D_FF))])[0]
    gw_down = _mm("gw_down", h, dx2, "tn", 1024, D_MODEL, tk,
                  [(_sds((D_FF, D_MODEL), BF16), _tile_ij(1024, D_MODEL))], _epi_store(BF16))[0]
    gw_up = _mm("gw_up", xn2, du, "tn", D_MODEL, 512, tk2,
                [(_sds((N_DEV, D_MODEL, 512), BF16), pl.BlockSpec((None, D_MODEL, 512), lambda i, j, k: (j, 0, 0)))],
                _epi_store(BF16))[0]

    vec = pl.BlockSpec((1, D_MODEL), lambda i, j, k: (0, 0))
    seq_sem = ("arbitrary", "arbitrary", "arbitrary")

    def epi_norm2(acc, ex, out):
        dx, dg = _rms_bwd(acc, ex[0][...], ex[2][...])
        out[0][...] = ex[1][...] + dx
        _acc_rows(out[1], dg, pl.program_id(0) == 0)

    dx1, g_norm2 = _mm(
        "mlp_up_bwd", du, w_up_t, "nn", tm, D_MODEL, D_FF,
        [(_sds((S, D_MODEL), F32), row_tile), (_sds((1, D_MODEL), F32), vec)],
        epi_norm2, extras=[(x1, row_tile), (dx2, row_tile), (norm2_g, vec)], sem=seq_sem)

    def epi_dy(acc, ex, out):
        sa = jax.nn.sigmoid(ex[0][...].astype(F32))
        sb = jax.nn.sigmoid(ex[1][...].astype(F32))
        out[0][...] = (acc * sa).astype(BF16)
        out[1][...] = (acc * sb).astype(BF16)
        out[2][:, :D_MODEL] = (acc * ex[2][...].astype(F32) * (sa * (1.0 - sa))).astype(BF16)
        out[2][:, D_MODEL:] = (acc * ex[3][...].astype(F32) * (sb * (1.0 - sb))).astype(BF16)

    ga_spec = pl.BlockSpec((tm, D_MODEL), lambda i, j, k: (i, R_GA // D_MODEL))
    gb_spec = pl.BlockSpec((tm, D_MODEL), lambda i, j, k: (i, R_GB // D_MODEL))
    gates_spec = pl.BlockSpec((tm, 2 * D_MODEL), lambda i, j, k: (i, R_GA // (2 * D_MODEL)))
    dpa, dpb, dz_rest = _mm(
        "out_proj_bwd", dx1, w_out_t, "nn", tm, D_MODEL, D_MODEL,
        [(_sds((S, D_MODEL), BF16), row_tile), (_sds((S, D_MODEL), BF16), row_tile), (_sds((S, REST_W), BF16), gates_spec)],
        epi_dy, extras=[(z_rest, ga_spec), (z_rest, gb_spec), (pa, row_tile), (pb, row_tile)])
    gw_out = _mm("gw_out", y, dx1, "tn", D_MODEL, D_MODEL, tk,
                 [(_sds((D_MODEL, D_MODEL), BF16), _tile_ij(D_MODEL, D_MODEL))], _epi_store(BF16))[0]
    gw_pa = _mm("gw_proj_a", o_a, dpa, "tn", ATT_OUT_W, D_MODEL, tk,
                [(_sds((ATT_OUT_W, D_MODEL), BF16), _tile_ij(ATT_OUT_W, D_MODEL))], _epi_store(BF16))[0]
    gw_pb = _mm("gw_proj_b", o_r, dpb, "tn", 1024, D_MODEL, tk2,
                [(_sds((RET_V_W, D_MODEL), BF16), _tile_ij(1024, D_MODEL))], _epi_store(BF16))[0]

    def epi_doa(acc, ex, out):
        out[0][...] = acc
        prod = acc * ex[0][...]
        out[1][...] = jnp.concatenate(
            [jnp.broadcast_to(jnp.sum(prod[:, j * DH:(j + 1) * DH], axis=-1, keepdims=True), (prod.shape[0], DH))
             for j in range(HPG)], axis=1)

    slot_tile = _tile_ij(tm, ATT_OUT_W)
    do_a, delta = _mm("proj_a_bwd", dpa, w_pa_t, "nn", tm, ATT_OUT_W, D_MODEL,
                      [(_sds((S, ATT_OUT_W), F32), slot_tile), (_sds((S, ATT_OUT_W), F32), slot_tile)],
                      epi_doa, extras=[(o_a, slot_tile)])
    owed = [_chip_core(_by_owner_cols(gw_pa)), _chip_core(gw_pb.reshape(N_DEV, -1, D_MODEL)),
            _chip_core(gw_out.reshape(N_DEV, -1, D_MODEL)), _chip_core(gw_up), _chip_core(gw_down.reshape(N_DEV, -1, D_MODEL))]
    names = ("w_proj_a", "w_proj_b", "w_out", "w_up", "w_down")

    def epi_dor(acc, ex, out):
        cvs = [slice(h * RET_V, (h + 1) * RET_V) for h in range(RET_HEADS)]
        do, dgr, dg, db = _norm_gate_bwd([acc[:, cv] for cv in cvs], [ex[0][:, cv] for cv in cvs],
                                         [ex[1][:, cv].astype(F32) for cv in cvs],
                                         [ex[2][:, cv] for cv in cvs], [ex[3][:, cv] for cv in cvs])
        for h, cv in enumerate(cvs):
            out[0][:, cv] = do[h].astype(BF16)
            out[1][:, cv] = dgr[h].astype(BF16)
        first = pl.program_id(0) == 0
        _acc_rows(out[2], jnp.concatenate(dg, axis=1), first)
        _acc_rows(out[3], jnp.concatenate(db, axis=1), first)

    wide_tile = _tile_ij(ts, RET_V_W)
    gate_tile = pl.BlockSpec((ts, RET_V_W), lambda i, j, k: (i, R_G // RET_V_W))
    wide_vec = pl.BlockSpec((1, RET_V_W), lambda i, j, k: (0, 0))
    (do_ret, dz_rest, g_gn_g, g_gn_b), got = _mm(
        "proj_b_bwd", dpb, w_pb_t, "nn", ts, RET_V_W, D_MODEL,
        [(_sds((S, RET_V_W), BF16), wide_tile), (_sds((S, REST_W), BF16), gate_tile),
         (_sds((1, RET_V_W), F32), wide_vec), (_sds((1, RET_V_W), F32), wide_vec)],
        epi_dor, extras=[(o_pre, wide_tile), (z_rest, gate_tile), (gn_g, wide_vec), (gn_b, wide_vec)],
        exchange=_pair_exchange(owed), in_place=(dz_rest, 1))
    chip_sums = [_pair_sum(f"pair_sum_{n}", g, r, core, min(256, g.shape[2])) for n, g, r in zip(names, owed, got)]

    dz_rest = _ret_bwd(do_ret, states, z_rest, dz_rest)
    dz_att, gq_parts, gk_parts, parts_late = None, [], [], None
    for gi, ((_, d), nb) in enumerate(zip(ATT_GROUPS, ATT_BLOCKS_PER_STEP)):
        last = gi == len(ATT_GROUPS) - 1
        (dz_att, gq_p, gk_p), parts = _att_bwd_group(z_att, dz_att, do_a, delta, lmix, gq3, gk3, slopes3, gi, d, nb,
                                                     exchange=_chip_exchange(chip_sums) if last else None)
        parts_late = parts if last else parts_late
        gq_parts.append(gq_p)
        gk_parts.append(gk_p)
    g_qn = jnp.concatenate(gq_parts, axis=0).reshape(1, ATT_HEADS, DH)
    g_kn = jnp.concatenate(gk_parts, axis=0).reshape(1, ATT_HEADS, DH)

    gw_att = _mm("gw_in_att", xn, dz_att, "tn", D_MODEL, ATT_W, tk,
                 [(_sds((D_MODEL, 3 * ATT_W), BF16), _tile_ij(D_MODEL, ATT_W))], _epi_store(BF16))[0]
    gw_rest = _mm("gw_in_rest", xn, dz_rest, "tn", D_MODEL, 1024, tk2,
                  [(_sds((D_MODEL, REST_W), BF16), _tile_ij(D_MODEL, 1024))], _epi_store(BF16))[0]
    gw_att = gw_att.reshape(D_MODEL, ATT_HEADS, 3, DH).transpose(0, 2, 1, 3).reshape(D_MODEL, 3 * ATT_W)
    gw_in = jnp.concatenate([gw_att, gw_rest], axis=1)
    owed_in = _chip_core(_by_owner_cols(gw_in))
    (dxn_att,), (got_in,) = _mm("in_proj_att_bwd", dz_att, w_att_t, "nn", tm, D_MODEL, 3 * ATT_W,
                                [(_sds((S, D_MODEL), F32), row_tile)], _epi_store(F32), exchange=_pair_exchange([owed_in]))
    chip_sum_in = _pair_sum("pair_sum_w_in", owed_in, got_in, core, min(256, owed_in.shape[2]))

    def epi_norm1(acc, ex, out):
        dx, dg = _rms_bwd(acc + ex[0][...], ex[1][...], ex[3][...])
        out[0][...] = ex[2][...] + dx
        _acc_rows(out[1], dg, pl.program_id(0) == 0)

    short_tile = _tile_ij(ts, D_MODEL)
    (grad_x, g_norm1), parts_in = _mm(
        "in_proj_rest_bwd", dz_rest, w_rest_t, "nn", ts, D_MODEL, REST_W,
        [(_sds((S, D_MODEL), F32), short_tile), (_sds((1, D_MODEL), F32), vec)],
        epi_norm1, extras=[(dxn_att, short_tile), (x, short_tile), (dx1, short_tile), (norm1_g, vec)],
        exchange=_chip_exchange([chip_sum_in]))

    small = (g_norm1, g_qn, g_kn, g_gn_g, g_gn_b, g_norm2)
    return loss_local, grad_x, list(parts_in) + list(parts_late), small


def _position():
    return lax.axis_index("x"), lax.axis_index("y"), lax.axis_index("c")


def _other_chips(x, y):
    return [(1 - x, y), (x, 1 - y), (1 - x, 1 - y)]


_ANY = pl.BlockSpec(memory_space=pl.ANY)


def _gather_exchange(shards):
    nw = len(shards)

    def plan(x_refs, out_refs, sems):
        send_sems, recv_sems, local_sems = sems
        x, y, c = _position()
        me, sibling = (x, y, c), (x, y, 1 - c)
        chips = _other_chips(x, y)

        def copy(w, k, block, to, own=False):
            px, py, pc = block
            rows = out_refs[w].at[4 * px + 2 * py + pc]
            return pltpu.make_async_remote_copy(
                src_ref=x_refs[w] if own else rows, dst_ref=rows,
                send_sem=send_sems.at[7 * w + k], recv_sem=recv_sems.at[7 * w + k], device_id=to, device_id_type=MESH)

        def mine(w):
            return pltpu.make_async_copy(x_refs[w], out_refs[w].at[4 * x + 2 * y + c], local_sems.at[w])

        def own_sends(w):
            return [copy(w, 0, me, sibling, own=True)] + [copy(w, 1 + j, me, (*chip, c), own=True)
                                                          for j, chip in enumerate(chips)]

        def start():
            for w in range(nw):
                mine(w).start()
                for cp in own_sends(w):
                    cp.start()

        def relay():
            for j, chip in enumerate(chips):
                for w in range(nw):
                    copy(w, 1 + j, (*chip, c), me).wait_recv()
                    copy(w, 4 + j, (*chip, c), sibling).start()

        def finish():
            for w in range(nw):
                copy(w, 0, sibling, me).wait_recv()
                for j, chip in enumerate(chips):
                    copy(w, 4 + j, (*chip, 1 - c), me).wait_recv()
            for w in range(nw):
                for cp in own_sends(w):
                    cp.wait_send()
                for j, chip in enumerate(chips):
                    copy(w, 4 + j, (*chip, c), sibling).wait_send()
                mine(w).wait()

        return start, relay, finish

    return _Exchange(shards, [_sds((N_DEV,) + s.shape, s.dtype) for s in shards],
                     [pltpu.SemaphoreType.DMA((7 * nw,)), pltpu.SemaphoreType.DMA((7 * nw,)),
                      pltpu.SemaphoreType.DMA((nw,))], plan)


def _run_exchange(name, exchange):
    n_in, n_out = len(exchange.ins), len(exchange.out_shapes)

    def body(*refs):
        start, relay, finish = exchange.split(refs[:n_in], refs[n_in:n_in + n_out], refs[n_in + n_out:])
        start()
        if relay is not None:
            relay()
        finish()

    return pl.pallas_call(
        body, name=name, out_shape=exchange.out_shapes,
        in_specs=[_ANY] * n_in, out_specs=[_ANY] * n_out, scratch_shapes=exchange.sems,
    )(*exchange.ins)


def _pair_exchange(grads):
    ng = len(grads)

    def plan(g_refs, out_refs, sems):
        send_sems, recv_sems = sems
        x, y, c = _position()

        def copies():
            return [pltpu.make_async_remote_copy(
                src_ref=g_refs[w].at[:, 1 - c], dst_ref=out_refs[w], send_sem=send_sems.at[w],
                recv_sem=recv_sems.at[w], device_id=(x, y, 1 - c), device_id_type=MESH) for w in range(ng)]

        def start():
            for cp in copies():
                cp.start()

        def finish():
            for cp in copies():
                cp.wait()

        return start, None, finish

    return _Exchange(grads, [_sds((N_CHIP,) + g.shape[2:], g.dtype) for g in grads],
                     [pltpu.SemaphoreType.DMA((ng,)), pltpu.SemaphoreType.DMA((ng,))], plan)


def _small_all_gather(small):
    def plan(in_refs, out_refs, sems):
        (s_ref,), (s_out,) = in_refs, out_refs
        send_sems, recv_sems, local_sem = sems
        x, y, c = _position()
        me_id = 4 * x + 2 * y + c
        flips = [(a, b, e) for a in (0, 1) for b in (0, 1) for e in (0, 1)][1:]
        peers = [(x ^ a, y ^ b, c ^ e) for a, b, e in flips]

        def start():
            pltpu.make_async_copy(s_ref, s_out.at[me_id], local_sem).start()
            for k, p in enumerate(peers):
                pltpu.make_async_remote_copy(
                    src_ref=s_ref, dst_ref=s_out.at[me_id], send_sem=send_sems.at[k], recv_sem=recv_sems.at[k],
                    device_id=p, device_id_type=MESH).start()

        def finish():
            for k, (px, py, pc) in enumerate(peers):
                pltpu.make_async_remote_copy(
                    src_ref=s_ref, dst_ref=s_out.at[4 * px + 2 * py + pc], send_sem=send_sems.at[k],
                    recv_sem=recv_sems.at[k], device_id=(px, py, pc), device_id_type=MESH).wait()
            pltpu.make_async_copy(s_ref, s_out.at[me_id], local_sem).wait()

        return start, None, finish

    return _run_exchange("small_grad_all_gather", _Exchange(
        [small], [_sds((N_DEV,) + small.shape, small.dtype)],
        [pltpu.SemaphoreType.DMA((7,)), pltpu.SemaphoreType.DMA((7,)), pltpu.SemaphoreType.DMA], plan))[0]


def _pair_sum(name, g, got, core, tr):
    n_chip, _, R, C = g.shape

    def body(c_ref, a_ref, b_ref, o_ref):
        del c_ref
        o_ref[...] = (a_ref[...].astype(F32) + b_ref[...].astype(F32)).astype(o_ref.dtype)

    return pl.pallas_call(
        body, name=name,
        grid_spec=pltpu.PrefetchScalarGridSpec(
            num_scalar_prefetch=1, grid=(n_chip, R // tr),
            in_specs=[pl.BlockSpec((None, None, tr, C), lambda ch, i, c_ref: (ch, c_ref[0], i, 0)),
                      pl.BlockSpec((None, tr, C), lambda ch, i, c_ref: (ch, i, 0))],
            out_specs=pl.BlockSpec((None, tr, C), lambda ch, i, c_ref: (ch, i, 0))),
        out_shape=_sds(got.shape, got.dtype),
        compiler_params=_cparams(("parallel", "parallel")),
    )(core, g, got)


def _chip_exchange(parts):
    ng = len(parts)

    def plan(p_refs, out_refs, sems):
        send_sems, recv_sems, local_sems = sems
        x, y, c = _position()
        my_chip = 2 * x + y

        def copies():
            local = [pltpu.make_async_copy(p_refs[w].at[my_chip], out_refs[w].at[my_chip], local_sems.at[w])
                     for w in range(ng)]
            remote = [pltpu.make_async_remote_copy(
                src_ref=p_refs[w].at[2 * cx + cy], dst_ref=out_refs[w].at[my_chip],
                send_sem=send_sems.at[3 * w + k], recv_sem=recv_sems.at[3 * w + k],
                device_id=(cx, cy, c), device_id_type=MESH)
                for w in range(ng) for k, (cx, cy) in enumerate(_other_chips(x, y))]
            return local + remote

        def start():
            for cp in copies():
                cp.start()

        def finish():
            for cp in copies():
                cp.wait()

        return start, None, finish

    return _Exchange(parts, [_sds(p.shape, p.dtype) for p in parts],
                     [pltpu.SemaphoreType.DMA((3 * ng,)), pltpu.SemaphoreType.DMA((3 * ng,)),
                      pltpu.SemaphoreType.DMA((ng,))], plan)


def _adamw(name, parts, w, m, v, tr):
    n_parts = parts.shape[0]
    R, C = w.shape

    def body(p_ref, w_ref, m_ref, v_ref, g_ref, d_ref, mo_ref, vo_ref):
        g = p_ref[0].astype(F32)
        for i in range(1, n_parts):
            g = g + p_ref[i].astype(F32)
        m_new = ADAM_B1 * m_ref[...] + (1.0 - ADAM_B1) * g
        v_new = ADAM_B2 * v_ref[...] + (1.0 - ADAM_B2) * (g * g)
        m_hat = m_new / (1.0 - ADAM_B1 ** ADAM_STEP)
        v_hat = v_new / (1.0 - ADAM_B2 ** ADAM_STEP)
        g_ref[...] = g
        d_ref[...] = -ADAM_LR * (m_hat / (jnp.sqrt(v_hat) + ADAM_EPS) + ADAM_WD * w_ref[...])
        mo_ref[...] = m_new
        vo_ref[...] = v_new

    tile = pl.BlockSpec((tr, C), lambda i: (i, 0))
    return pl.pallas_call(
        body, name=name, grid=(R // tr,),
        in_specs=[pl.BlockSpec((n_parts, tr, C), lambda i: (0, i, 0)), tile, tile, tile],
        out_specs=[tile] * 4,
        out_shape=[_sds((R, C), F32)] * 4,
        compiler_params=_cparams(("parallel",)),
    )(parts, w, m, v)


def _flat_small(arrs):
    return jnp.concatenate([a.reshape(-1) for a in arrs]).reshape(SMALL_ROWS, LANES)


def _cols(g):
    return g.transpose(1, 0, 2).reshape(g.shape[1], -1)


def _rows_of(g):
    return g.reshape(-1, g.shape[2])


def _by_owner_cols(g):
    rows, cols = g.shape
    return g.reshape(rows, N_DEV, cols // N_DEV).transpose(1, 0, 2)


def _chip_core(g):
    return g.reshape((N_CHIP, 2) + g.shape[1:])


def kernel(x, norm1_g, w_in, q_norm_g, k_norm_g, ret_gn_g, ret_gn_b, w_proj_a, w_proj_b, w_out, norm2_g, w_up, w_down, loss_target, m_norm1_g, m_w_in, m_q_norm_g, m_k_norm_g, m_ret_gn_g, m_ret_gn_b, m_w_proj_a, m_w_proj_b, m_w_out, m_norm2_g, m_w_up, m_w_down, v_norm1_g, v_w_in, v_q_norm_g, v_k_norm_g, v_ret_gn_g, v_ret_gn_b, v_w_proj_a, v_w_proj_b, v_w_out, v_norm2_g, v_w_up, v_w_down):
    big_w = (w_in, w_proj_a, w_proj_b, w_out, w_up, w_down)
    big_m = (m_w_in, m_w_proj_a, m_w_proj_b, m_w_out, m_w_up, m_w_down)
    big_v = (v_w_in, v_w_proj_a, v_w_proj_b, v_w_out, v_w_up, v_w_down)
    small_w = (norm1_g, q_norm_g, k_norm_g, ret_gn_g, ret_gn_b, norm2_g)
    small_m = (m_norm1_g, m_q_norm_g, m_k_norm_g, m_ret_gn_g, m_ret_gn_b, m_norm2_g)
    small_v = (v_norm1_g, v_q_norm_g, v_k_norm_g, v_ret_gn_g, v_ret_gn_b, v_norm2_g)

    shards = [w[0].astype(BF16) for w in big_w]
    core = lax.axis_index("c").astype(jnp.int32).reshape(1)
    loss_local, grad_x, parts, small_g = _step(
        x[0], loss_target[0], norm1_g, q_norm_g[0], k_norm_g[0], ret_gn_g, ret_gn_b, norm2_g, shards, core)
    small_all = _small_all_gather(_flat_small(small_g))
    names = ("w_in", "w_proj_a", "w_proj_b", "w_out", "w_up", "w_down")

    res = {}
    for n, p, w, m, v in zip(names, parts, big_w, big_m, big_v):
        outs = _adamw(f"adamw_{n}", p, w[0], m[0], v[0], min(128, w.shape[1]))
        res[n] = [o[None] for o in outs]
    s_outs = _adamw("adamw_small", small_all, _flat_small(small_w), _flat_small(small_m), _flat_small(small_v), SMALL_ROWS)
    small_names = ("norm1_g", "q_norm_g", "k_norm_g", "ret_gn_g", "ret_gn_b", "norm2_g")
    for n in small_names:
        res[n] = []
    for o in s_outs:
        flat, off = o.reshape(-1), 0
        for n, w in zip(small_names, small_w):
            res[n].append(flat[off:off + w.size].reshape(w.shape))
            off += w.size

    order = ("norm1_g", "w_in", "q_norm_g", "k_norm_g", "ret_gn_g", "ret_gn_b", "w_proj_a", "w_proj_b", "w_out",
             "norm2_g", "w_up", "w_down")
    loss = lax.psum(loss_local, MESH_AXES)
    return (loss, grad_x[None], *[res[n][0] for n in order], *[res[n][1] for n in order],
            *[res[n][2] for n in order], *[res[n][3] for n in order])
```

```python
import math

import numpy as np
import jax
import jax.numpy as jnp
from jax import lax
from jax.experimental import pallas as pl
from jax.experimental.pallas import tpu as pltpu

F32 = jnp.float32
BF16 = jnp.bfloat16

D_MODEL = 1024
ATT_GROUPS = ((128, 1), (512, 4), (2048, 16))
ATT_BLOCKS_PER_STEP = (16, 4, 1)
ATT_TOGETHER = 4
HPG = 4
ATT_HEADS = 12
DH = 128
BLK = 128
ATT_W = ATT_HEADS * DH
ATT_OUT_W = HPG * DH
RET_HEADS = 4
RET_QK = 256
RET_V = 512
RET_QK_W = RET_HEADS * RET_QK
RET_V_W = RET_HEADS * RET_V
CHUNK = 128
RET_PER_STEP = 2
D_FF = 4096
IN_W = 12800
REST_W = IN_W - 3 * ATT_W
EPS = 1e-6
ADAM_LR, ADAM_B1, ADAM_B2, ADAM_EPS, ADAM_WD, ADAM_STEP = 0.001, 0.9, 0.999, 1e-08, 0.01, 10
N_DEV = 8
N_CHIP = 4
MESH_AXES = ("x", "y", "c")
MESH = pl.DeviceIdType.MESH
VMEM_LIMIT = 56 * 1024 * 1024
LANES = 128
NEG = -1e30

_NN = (((1,), (0,)), ((), ()))
_NT = (((1,), (1,)), ((), ()))
_TN = (((0,), (0,)), ((), ()))

R_Q, R_K, R_V, R_G, R_GA, R_GB = 0, 1024, 2048, 4096, 6144, 7168

LOG_GAMMA = [float(v) for v in np.log(1.0 - 2.0 ** (-5.0 - np.arange(RET_HEADS, dtype=np.float32))).astype(np.float32)]
ALIBI = np.asarray(2.0 ** (-8.0 * np.arange(1, ATT_HEADS + 1, dtype=np.float32) / ATT_HEADS), np.float32)

SMALL_ROWS = (1024 + 1536 + 1536 + 2048 + 2048 + 1024) // LANES


def _dot(a, b, dims=_NN):
    return lax.dot_general(a, b, dims, preferred_element_type=F32)


def _cparams(sem):
    return pltpu.CompilerParams(dimension_semantics=sem, vmem_limit_bytes=VMEM_LIMIT)


def _sds(shape, dtype):
    return jax.ShapeDtypeStruct(shape, dtype)


class _Exchange:
    def __init__(self, ins, out_shapes, sems, plan):
        self.ins, self.out_shapes, self.sems, self.plan = list(ins), list(out_shapes), list(sems), plan

    def split(self, in_refs, out_refs, sem_refs):
        return self.plan(in_refs, out_refs, sem_refs)


def _mm(name, a, b, mode, tm, tn, tk, outs, epi, extras=(), b_pro=None,
        sem=("parallel", "parallel", "arbitrary"), exchange=None, in_place=None):
    if mode == "nn":
        (M, K), (_, N) = a.shape, b.shape
        a_spec = pl.BlockSpec((tm, tk), lambda i, j, k: (i, k))
        dims = _NN
    else:
        (K, M), (_, N) = a.shape, b.shape
        a_spec = pl.BlockSpec((tk, tm), lambda i, j, k: (k, i))
        dims = _TN
    assert M % tm == 0 and N % tn == 0 and K % tk == 0, (name, M, N, K, tm, tn, tk)
    nk = K // tk
    whole_b = dict(pipeline_mode=pl.Buffered(1)) if (nk == 1 and N == tn) else {}
    b_spec = pl.BlockSpec((tk, tn), lambda i, j, k: (k, j), **whole_b)
    n_ex, n_out = len(extras), len(outs)
    grid = (M // tm, N // tn, nk)
    n_xi = len(exchange.ins) if exchange else 0
    n_xo = len(exchange.out_shapes) if exchange else 0
    n_acc = 1 if nk > 1 else 0

    n_ip = 1 if in_place else 0

    def body(a_ref, b_ref, *rest):
        ex, rest = rest[:n_ex], rest[n_ex:]
        x_in, rest = rest[:n_xi], rest[n_xi + n_ip:]
        out, rest = rest[:n_out], rest[n_out:]
        x_out, rest = rest[:n_xo], rest[n_xo:]
        step =(pl.program_id(0) * grid[1] + pl.program_id(1)) * grid[2] + pl.program_id(2)
        n_steps = grid[0] * grid[1] * grid[2]
        if exchange:
            start, relay, finish = exchange.split(x_in, x_out, rest[n_acc:])
            pl.when(step == 0)(start)
        bv = b_ref[...]
        if b_pro is not None:
            bv = b_pro(bv)
        part = _dot(a_ref[...].astype(BF16), bv.astype(BF16), dims)
        if nk == 1:
            epi(part, ex, out)
        else:
            acc_ref = rest[0]
            k = pl.program_id(2)

            @pl.when(k == 0)
            def _():
                acc_ref[...] = part

            @pl.when(k > 0)
            def _():
                acc_ref[...] += part

            @pl.when(k == nk - 1)
            def _():
                epi(acc_ref[...], ex, out)
        if exchange:
            if relay is not None:
                pl.when(step == (7 * n_steps) // 8)(relay)
            pl.when(step == n_steps - 1)(finish)

    res = pl.pallas_call(
        body,
        name=name,
        grid=grid,
        in_specs=[a_spec, b_spec] + [s for _, s in extras] + [_ANY] * (n_xi + n_ip),
        out_specs=[s for _, s in outs] + [_ANY] * n_xo,
        out_shape=[o for o, _ in outs] + (exchange.out_shapes if exchange else []),
        scratch_shapes=([pltpu.VMEM((tm, tn), F32)] if nk > 1 else []) + (exchange.sems if exchange else []),
        input_output_aliases={2 + n_ex + n_xi: in_place[1]} if in_place else {},
        compiler_params=_cparams(("arbitrary",) * 3 if exchange else sem),
    )(a, b, *[e for e, _ in extras], *(exchange.ins if exchange else []), *([in_place[0]] if in_place else []))
    return (res[:n_out], res[n_out:]) if exchange else res


def _tile_ij(tm, tn):
    return pl.BlockSpec((tm, tn), lambda i, j, k: (i, j))


def _epi_store(dtype):
    def epi(acc, ex, out):
        out[0][...] = acc.astype(dtype)
    return epi


def _rms_rows(x):
    return lax.rsqrt(jnp.mean(x * x, axis=-1, keepdims=True) + EPS)


def _acc_rows(ref, part, first):
    @pl.when(first)
    def _():
        ref[...] = part

    @pl.when(jnp.logical_not(first))
    def _():
        ref[...] += part


def _rmsnorm_fwd(x, g, tm, exchange):
    S, Dm = x.shape
    n_steps = S // tm
    n_xi, n_xo = len(exchange.ins), len(exchange.out_shapes)

    def body(x_ref, g_ref, *rest):
        x_in, o_ref, x_out, sems = rest[:n_xi], rest[n_xi], rest[n_xi + 1:n_xi + 1 + n_xo], rest[n_xi + 1 + n_xo:]
        start, relay, finish = exchange.split(x_in, x_out, sems)
        step = pl.program_id(0)
        pl.when(step == 0)(start)
        xv = x_ref[...]
        o_ref[...] = (xv * _rms_rows(xv) * g_ref[...]).astype(BF16)

        @pl.when(step == n_steps - 1)
        def _():
            relay()
            finish()

    res = pl.pallas_call(
        body, name="rmsnorm1_fwd", grid=(n_steps,),
        in_specs=[pl.BlockSpec((tm, Dm), lambda i: (i, 0)), pl.BlockSpec((1, Dm), lambda i: (0, 0))] + [_ANY] * n_xi,
        out_specs=[pl.BlockSpec((tm, Dm), lambda i: (i, 0))] + [_ANY] * n_xo,
        out_shape=[_sds((S, Dm), BF16)] + exchange.out_shapes,
        scratch_shapes=exchange.sems,
        compiler_params=_cparams(("arbitrary",)),
    )(x, g, *exchange.ins)
    return res[0], res[1:]


def _rows(ref, r, b, d):
    if d == 1:
        return ref[b * BLK:(b + 1) * BLK, :]
    return ref[pl.ds(b * BLK * d + r, BLK, stride=d), :]


def _put_rows(ref, r, b, d, val):
    if d == 1:
        ref[b * BLK:(b + 1) * BLK, :] = val
    else:
        ref[pl.ds(b * BLK * d + r, BLK, stride=d), :] = val


def _head_norm(x, g):
    r = _rms_rows(x)
    xh = x * r
    return (xh * g).astype(BF16), xh, r


def _head_norm_bwd(dyn, xh, r, g):
    dxh = dyn * g
    dx = r * (dxh - xh * jnp.mean(dxh * xh, axis=-1, keepdims=True))
    return dx, jnp.sum(dyn * xh, axis=0, keepdims=True)


def _att_mask_bias(slope, d, first):
    qi = lax.broadcasted_iota(jnp.int32, (BLK, 2 * BLK), 0)
    kj = lax.broadcasted_iota(jnp.int32, (BLK, 2 * BLK), 1)
    dist = BLK + qi - kj
    valid = (dist >= 0) & (dist <= BLK)
    if first is not None:
        valid = valid & (jnp.logical_not(first) | (kj >= BLK))
    bias = -slope * (dist * d).astype(F32)
    return valid, bias


def _att_specs(gi, d, nb, S):
    span = BLK * d
    sb = span * nb
    nspan = S // span
    before = lambda n: jnp.maximum(n * nb - 1, 0)
    after = lambda n: jnp.minimum((n + 1) * nb, nspan - 1)
    zcol = lambda j, kind: 3 * (gi * HPG + j) + kind
    cur = lambda kind: pl.BlockSpec((sb, DH), lambda j, n: (n, zcol(j, kind)))
    prev = lambda kind: pl.BlockSpec((span, DH), lambda j, n: (before(n), zcol(j, kind)))
    nxt = lambda kind: pl.BlockSpec((span, DH), lambda j, n: (after(n), zcol(j, kind)))
    slot = pl.BlockSpec((sb, DH), lambda j, n: (n, j))
    slot_next = pl.BlockSpec((span, DH), lambda j, n: (after(n), j))
    head = pl.BlockSpec((None, 1, DH), lambda j, n: (gi * HPG + j, 0, 0))
    return cur, prev, nxt, slot, slot_next, head


def _att_fwd_group(z_att, gq3, gk3, slopes3, gi, d, nb, others=()):
    S = z_att.shape[0]
    nsb = S // (BLK * d * nb)
    scale = DH ** -0.5
    n_other = len(others)

    def body(q_ref, k_ref, v_ref, kp_ref, vp_ref, gq_ref, gk_ref, sl_ref, *rest):
        other_refs, (o_ref, l_ref) = rest[:2 * n_other], rest[2 * n_other:]
        slope = sl_ref[...][:, :1]
        valid0, bias = _att_mask_bias(slope, d, pl.program_id(1) == 0)
        valid_in, _ = _att_mask_bias(slope, d, None)
        gq, gk = gq_ref[...], gk_ref[...]
        memo = {}

        def get(kind, r, b):
            if (kind, r, b) not in memo:
                if kind == "k":
                    val = _head_norm(_rows(k_ref if b >= 0 else kp_ref, r, max(b, 0), d), gk)[0]
                else:
                    val = _rows(v_ref if b >= 0 else vp_ref, r, max(b, 0), d).astype(BF16)
                memo[(kind, r, b)] = val
            return memo[(kind, r, b)]

        units = [(r, b) for r in range(d) for b in range(nb)]
        for c0 in range(0, len(units), ATT_TOGETHER):
            us = units[c0:c0 + ATT_TOGETHER]
            q = [_head_norm(_rows(q_ref, r, b, d), gq)[0] for r, b in us]
            k2 = [jnp.concatenate([get("k", r, b - 1), get("k", r, b)], axis=0) for r, b in us]
            v2 = [jnp.concatenate([get("v", r, b - 1), get("v", r, b)], axis=0) for r, b in us]
            s = [jnp.where(valid0 if b == 0 else valid_in, _dot(q[i], k2[i], _NT) * scale + bias, NEG)
                 for i, (r, b) in enumerate(us)]
            m = [jnp.max(si, axis=-1, keepdims=True) for si in s]
            p = [jnp.exp(si - mi) for si, mi in zip(s, m)]
            den = [jnp.sum(pi, axis=-1, keepdims=True) for pi in p]
            o = [_dot(pi.astype(BF16), vi) / di for pi, vi, di in zip(p, v2, den)]
            for i, (r, b) in enumerate(us):
                _put_rows(o_ref, r, b, d, o[i])
                _put_rows(l_ref, r, b, d, jnp.broadcast_to(m[i] + jnp.log(den[i]), (BLK, DH)))
        if n_other:
            os_ = [ref[...] for ref in other_refs[:n_other]] + [o_ref[...]]
            ls_ = [ref[...] for ref in other_refs[n_other:]] + [l_ref[...]]
            m = ls_[0]
            for l in ls_[1:]:
                m = jnp.maximum(m, l)
            es = [jnp.exp(l - m) for l in ls_]
            tot, mix = es[0], es[0] * os_[0]
            for e, o in zip(es[1:], os_[1:]):
                tot, mix = tot + e, mix + e * o
            o_ref[...] = mix / tot
            l_ref[...] = m + jnp.log(tot)

    cur, prev, _, slot, _, head = _att_specs(gi, d, nb, S)
    return pl.pallas_call(
        body, name=f"att_fwd_g{gi}", grid=(HPG, nsb),
        in_specs=[cur(0), cur(1), cur(2), prev(1), prev(2), head, head, head] + [slot] * (2 * n_other),
        out_specs=[slot, slot],
        out_shape=[_sds((S, ATT_OUT_W), F32), _sds((S, ATT_OUT_W), F32)],
        compiler_params=_cparams(("parallel", "arbitrary")),
    )(z_att, z_att, z_att, z_att, z_att, gq3, gk3, slopes3, *[o for o, _ in others], *[l for _, l in others])


def _ret_tables(lg):
    ri = lax.broadcasted_iota(jnp.int32, (CHUNK, CHUNK), 0)
    ci = lax.broadcasted_iota(jnp.int32, (CHUNK, CHUNK), 1)
    diff = (ri - ci).astype(F32)
    decay = jnp.where(diff >= 0, jnp.exp(lg * jnp.maximum(diff, 0.0)), 0.0)
    idx = lax.broadcasted_iota(jnp.int32, (CHUNK, 1), 0).astype(F32)
    xi = jnp.exp(lg * (idx + 1.0))
    zeta = jnp.exp(lg * (CHUNK - 1.0 - idx))
    return decay, xi, zeta, math.exp(lg * CHUNK)


def _ret_specs(nsteps, rev):
    idx = (lambda n: nsteps - 1 - n) if rev else (lambda n: n)
    rows = CHUNK * RET_PER_STEP
    qk = lambda off: pl.BlockSpec((rows, RET_QK_W), lambda n: (idx(n), off // RET_QK_W))
    vv = lambda off: pl.BlockSpec((rows, RET_V_W), lambda n: (idx(n), off // RET_V_W))
    par = pl.BlockSpec((1, RET_V_W), lambda n: (0, 0))
    wide = pl.BlockSpec((rows, RET_V_W), lambda n: (idx(n), 0))
    st = pl.BlockSpec((RET_HEADS, RET_PER_STEP, RET_QK, RET_V), lambda n: (0, idx(n), 0, 0))
    return qk, vv, par, wide, st


def _ret_fwd(z_rest, gn_g, gn_b):
    S = z_rest.shape[0]
    nch = S // CHUNK
    nsteps = nch // RET_PER_STEP

    def body(q_ref, k_ref, v_ref, gr_ref, g_ref, b_ref, or_ref, o_ref, st_ref, state):
        @pl.when(pl.program_id(0) == 0)
        def _():
            state[...] = jnp.zeros_like(state)

        for c in range(RET_PER_STEP):
            rc = slice(c * CHUNK, (c + 1) * CHUNK)
            for h in range(RET_HEADS):
                decay, xi, zeta, gch = _ret_tables(LOG_GAMMA[h])
                cq = slice(h * RET_QK, (h + 1) * RET_QK)
                cv = slice(h * RET_V, (h + 1) * RET_V)
                q = q_ref[rc, cq]
                kc32 = k_ref[rc, cq].astype(F32) * (RET_QK ** -0.5)
                kc = kc32.astype(BF16)
                v = v_ref[rc, cv]
                st = state[h]
                stb = st.astype(BF16)
                st_ref[h, c] = stb
                s = _dot(q, kc, _NT) * decay
                o = _dot(s.astype(BF16), v) + _dot(q, stb) * xi
                state[h] = st * gch + _dot((kc32 * zeta).astype(BF16), v, _TN)
                mu = jnp.mean(o, axis=-1, keepdims=True)
                cen = o - mu
                yh = cen * lax.rsqrt(jnp.mean(cen * cen, axis=-1, keepdims=True) + EPS)
                gr = gr_ref[rc, cv].astype(F32)
                or_ref[rc, cv] = ((yh * g_ref[:, cv] + b_ref[:, cv]) * (gr * jax.nn.sigmoid(gr))).astype(BF16)
                o_ref[rc, cv] = o

    qk, vv, par, wide, st = _ret_specs(nsteps, False)
    return pl.pallas_call(
        body, name="ret_fwd", grid=(nsteps,),
        in_specs=[qk(R_Q), qk(R_K), vv(R_V), vv(R_G), par, par],
        out_specs=[wide, wide, st],
        out_shape=[_sds((S, RET_V_W), BF16), _sds((S, RET_V_W), F32), _sds((RET_HEADS, nch, RET_QK, RET_V), BF16)],
        scratch_shapes=[pltpu.VMEM((RET_HEADS, RET_QK, RET_V), F32)],
        compiler_params=_cparams(("arbitrary",)),
    )(z_rest, z_rest, z_rest, z_rest, gn_g, gn_b)


def _merge_fwd(o_a, o_r, z_rest, x, wpa, wpb, wout, g2, tm):
    S = x.shape[0]

    def body(oa_ref, or_ref, ga_ref, gb_ref, x_ref, wpa_ref, wpb_ref, wo_ref, g2_ref,
             x1_ref, y_ref, pa_ref, pb_ref, xn2_ref):
        pa = _dot(oa_ref[...].astype(BF16), wpa_ref[...])
        pb = _dot(or_ref[...], wpb_ref[...])
        y = jax.nn.sigmoid(ga_ref[...].astype(F32)) * pa + jax.nn.sigmoid(gb_ref[...].astype(F32)) * pb
        yb = y.astype(BF16)
        x1 = x_ref[...] + _dot(yb, wo_ref[...])
        x1_ref[...] = x1
        y_ref[...] = yb
        pa_ref[...] = pa.astype(BF16)
        pb_ref[...] = pb.astype(BF16)
        xn2_ref[...] = (x1 * _rms_rows(x1) * g2_ref[...]).astype(BF16)

    row = lambda w: pl.BlockSpec((tm, w), lambda i: (i, 0))
    full = lambda a: pl.BlockSpec(a.shape, lambda i: (0, 0))
    return pl.pallas_call(
        body, name="merge_fwd", grid=(S // tm,),
        in_specs=[row(ATT_OUT_W), row(RET_V_W),
                  pl.BlockSpec((tm, D_MODEL), lambda i: (i, R_GA // D_MODEL)),
                  pl.BlockSpec((tm, D_MODEL), lambda i: (i, R_GB // D_MODEL)),
                  row(D_MODEL), full(wpa), full(wpb), full(wout), full(g2)],
        out_specs=[row(D_MODEL)] * 5,
        out_shape=[_sds((S, D_MODEL), F32)] + [_sds((S, D_MODEL), BF16)] * 4,
        compiler_params=_cparams(("parallel",)),
    )(o_a, o_r, z_rest, z_rest, x, wpa, wpb, wout, g2)


def _rms_bwd(dy, xv, g):
    r = _rms_rows(xv)
    xh = xv * r
    dg = dy * g
    dx = r * (dg - xh * jnp.mean(dg * xh, axis=-1, keepdims=True))
    return dx, jnp.sum(dy * xh, axis=0, keepdims=True)


def _norm_gate_bwd(dout, o, gr, gam, bet):
    mean = lambda xs: [jnp.mean(x, axis=-1, keepdims=True) for x in xs]
    cen = [x - m for x, m in zip(o, mean(o))]
    rstd = [lax.rsqrt(v + EPS) for v in mean([c * c for c in cen])]
    yh = [c * r for c, r in zip(cen, rstd)]
    y = [a * g + b for a, g, b in zip(yh, gam, bet)]
    sg = [jax.nn.sigmoid(g) for g in gr]
    dy = [d * (g * s) for d, g, s in zip(dout, gr, sg)]
    dgr = [d * a * (s * (1.0 + g * (1.0 - s))) for d, a, s, g in zip(dout, y, sg, gr)]
    dyh = [d * g for d, g in zip(dy, gam)]
    m1, m2 = mean(dyh), mean([a * b for a, b in zip(dyh, yh)])
    do = [r * (d - a - h * b) for r, d, a, h, b in zip(rstd, dyh, m1, yh, m2)]
    dg = [jnp.sum(d * h, axis=0, keepdims=True) for d, h in zip(dy, yh)]
    db = [jnp.sum(d, axis=0, keepdims=True) for d in dy]
    return do, dgr, dg, db


def _ret_bwd(do, states, z_rest, dz_rest):
    S = z_rest.shape[0]
    nsteps = S // CHUNK // RET_PER_STEP

    def body(do_ref, st_ref, q_ref, k_ref, v_ref, dz_in, dz_ref, gst):
        del dz_in

        @pl.when(pl.program_id(0) == 0)
        def _():
            gst[...] = jnp.zeros_like(gst)

        for c in reversed(range(RET_PER_STEP)):
            rc = slice(c * CHUNK, (c + 1) * CHUNK)
            for h in range(RET_HEADS):
                decay, xi, zeta, gch = _ret_tables(LOG_GAMMA[h])
                cq = slice(h * RET_QK, (h + 1) * RET_QK)
                cv = slice(h * RET_V, (h + 1) * RET_V)
                q = q_ref[rc, cq]
                kc32 = k_ref[rc, cq].astype(F32) * (RET_QK ** -0.5)
                kc = kc32.astype(BF16)
                v = v_ref[rc, cv]
                dob = do_ref[rc, cv]
                a = (_dot(q, kc, _NT) * decay).astype(BF16)
                da = (_dot(dob, v, _NT) * decay).astype(BF16)
                dcross = (dob.astype(F32) * xi).astype(BF16)
                g_next = gst[h]
                gb = g_next.astype(BF16)
                dq = _dot(da, kc) + _dot(dcross, st_ref[h, c], _NT)
                dkc = _dot(da, q, _TN)
                dkz = _dot(v, gb, _NT)
                dv = _dot(a, dob, _TN) + _dot((kc32 * zeta).astype(BF16), gb)
                gst[h] = g_next * gch + _dot(q, dcross, _TN)
                dz_ref[rc, R_Q + h * RET_QK:R_Q + (h + 1) * RET_QK] = dq.astype(BF16)
                dz_ref[rc, R_K + h * RET_QK:R_K + (h + 1) * RET_QK] = (
                    (dkc + dkz * zeta) * (RET_QK ** -0.5)).astype(BF16)
                dz_ref[rc, R_V + h * RET_V:R_V + (h + 1) * RET_V] = dv.astype(BF16)

    qk, vv, _, wide, st = _ret_specs(nsteps, True)
    return pl.pallas_call(
        body, name="ret_bwd", grid=(nsteps,),
        in_specs=[wide, st, qk(R_Q), qk(R_K), vv(R_V), _ANY],
        out_specs=pl.BlockSpec((CHUNK * RET_PER_STEP, R_G), lambda n: (nsteps - 1 - n, 0)),
        out_shape=_sds(dz_rest.shape, BF16),
        input_output_aliases={5: 0},
        scratch_shapes=[pltpu.VMEM((RET_HEADS, RET_QK, RET_V), F32)],
        compiler_params=_cparams(("arbitrary",)),
    )(do, states, z_rest, z_rest, z_rest, dz_rest)


def _att_probs(q, k, lmix, valid, bias):
    s = _dot(q, k, _NT) * (DH ** -0.5) + bias
    return jnp.where(valid, jnp.exp(jnp.where(valid, s, NEG) - lmix), 0.0)


def _att_bwd_group(z_att, dz_att, do_a, delta, lmix, gq3, gk3, slopes3, gi, d, nb, exchange=None):
    S = z_att.shape[0]
    sb = BLK * d * nb
    nsb = S // sb
    scale = DH ** -0.5
    aliased = dz_att is not None
    n_xi = len(exchange.ins) if exchange else 0
    n_xo = len(exchange.out_shapes) if exchange else 0

    def body(q_ref, k_ref, v_ref, kp_ref, vp_ref, qn_ref, do_ref, don_ref, dl_ref, dln_ref, lm_ref, lmn_ref,
             gq_ref, gk_ref, sl_ref, *rest):
        rest = rest[1:] if aliased else rest
        x_in, (dz_ref, dgq_ref, dgk_ref), rest = rest[:n_xi], rest[n_xi:n_xi + 3], rest[n_xi + 3:]
        x_out, stage, x_sems = rest[:n_xo], rest[n_xo], rest[n_xo + 1:]
        n = pl.program_id(1)
        step = pl.program_id(0) * nsb + n
        if exchange:
            start, _, finish = exchange.split(x_in, x_out, x_sems)
            pl.when(step == 0)(start)
        slope = sl_ref[...][:, :1]
        valid0, bias = _att_mask_bias(slope, d, n == 0)
        valid_in, _ = _att_mask_bias(slope, d, None)
        qi = lax.broadcasted_iota(jnp.int32, (BLK, BLK), 0)
        kj = lax.broadcasted_iota(jnp.int32, (BLK, BLK), 1)
        dist_n = BLK + qi - kj
        valid_n_in = dist_n <= BLK
        valid_n_last = valid_n_in & (n < nsb - 1)
        bias_n = -slope * (dist_n * d).astype(F32)
        gq, gk = gq_ref[...], gk_ref[...]
        dgq = jnp.zeros((1, DH), F32)
        dgk = jnp.zeros((1, DH), F32)
        memo = {}

        def get(kind, r, b):
            if (kind, r, b) not in memo:
                inner = 0 <= b < nb
                bb = b if inner else 0
                if kind == "q":
                    val = _head_norm(_rows(q_ref if inner else qn_ref, r, bb, d), gq)
                elif kind == "k":
                    val = _head_norm(_rows(k_ref if inner else kp_ref, r, bb, d), gk)
                elif kind == "v":
                    val = _rows(v_ref if inner else vp_ref, r, bb, d).astype(BF16)
                elif kind == "do":
                    val = _rows(do_ref if inner else don_ref, r, bb, d).astype(BF16)
                elif kind == "dl":
                    val = _rows(dl_ref if inner else dln_ref, r, bb, d)[:, :1]
                else:
                    val = _rows(lm_ref if inner else lmn_ref, r, bb, d)[:, :1]
                memo[(kind, r, b)] = val
            return memo[(kind, r, b)]

        units = [(r, b) for r in range(d) for b in range(nb)]
        for c0 in range(0, len(units), ATT_TOGETHER):
            us = units[c0:c0 + ATT_TOGETHER]
            k2 = [jnp.concatenate([get("k", r, b - 1)[0], get("k", r, b)[0]], axis=0) for r, b in us]
            v2 = [jnp.concatenate([get("v", r, b - 1), get("v", r, b)], axis=0) for r, b in us]
            p = [_att_probs(get("q", r, b)[0], k2[i], get("lm", r, b), valid0 if b == 0 else valid_in, bias)
                 for i, (r, b) in enumerate(us)]
            dp = [_dot(get("do", r, b), v2[i], _NT) for i, (r, b) in enumerate(us)]
            ds = [(p[i] * (dp[i] - get("dl", r, b)) * scale).astype(BF16) for i, (r, b) in enumerate(us)]
            dq = [_dot(ds[i], k2[i]) for i in range(len(us))]
            p_n = [_att_probs(get("q", r, b + 1)[0], get("k", r, b)[0], get("lm", r, b + 1),
                              valid_n_last if b == nb - 1 else valid_n_in, bias_n) for r, b in us]
            dp_n = [_dot(get("do", r, b + 1), get("v", r, b), _NT) for r, b in us]
            ds_n = [(p_n[i] * (dp_n[i] - get("dl", r, b + 1)) * scale).astype(BF16) for i, (r, b) in enumerate(us)]
            dk = [_dot(ds[i][:, BLK:], get("q", r, b)[0], _TN) + _dot(ds_n[i], get("q", r, b + 1)[0], _TN)
                  for i, (r, b) in enumerate(us)]
            dv = [_dot(p[i][:, BLK:].astype(BF16), get("do", r, b), _TN)
                  + _dot(p_n[i].astype(BF16), get("do", r, b + 1), _TN) for i, (r, b) in enumerate(us)]
            for i, (r, b) in enumerate(us):
                _, qh, qr = get("q", r, b)
                _, kh, kr = get("k", r, b)
                dxq, dg_q = _head_norm_bwd(dq[i], qh, qr, gq)
                dxk, dg_k = _head_norm_bwd(dk[i], kh, kr, gk)
                _put_rows(stage.at[0], r, b, d, dxq)
                _put_rows(stage.at[1], r, b, d, dxk)
                _put_rows(stage.at[2], r, b, d, dv[i])
                dgq, dgk = dgq + dg_q, dgk + dg_k
        for kind in range(3):
            dz_ref[:, kind * DH:(kind + 1) * DH] = stage[kind].astype(BF16)
        _acc_rows(dgq_ref, dgq, n == 0)
        _acc_rows(dgk_ref, dgk, n == 0)
        if exchange:
            pl.when(step == HPG * nsb - 1)(finish)

    cur, prev, nxt, slot, slot_next, head = _att_specs(gi, d, nb, S)
    gain = pl.BlockSpec((None, 1, DH), lambda j, n: (j, 0, 0))
    in_specs = [cur(0), cur(1), cur(2), prev(1), prev(2), nxt(0),
                slot, slot_next, slot, slot_next, slot, slot_next, head, head, head]
    args = [z_att] * 6 + [do_a, do_a, delta, delta, lmix, lmix, gq3, gk3, slopes3]
    if aliased:
        in_specs.append(_ANY)
        args.append(dz_att)
    res = pl.pallas_call(
        body, name=f"att_bwd_g{gi}", grid=(HPG, nsb),
        in_specs=in_specs + [_ANY] * n_xi,
        out_specs=[pl.BlockSpec((sb, 3 * DH), lambda j, n: (n, gi * HPG + j)), gain, gain] + [_ANY] * n_xo,
        out_shape=[_sds(z_att.shape, BF16), _sds((HPG, 1, DH), F32), _sds((HPG, 1, DH), F32)]
        + (exchange.out_shapes if exchange else []),
        input_output_aliases={len(args) - 1: 0} if aliased else {},
        scratch_shapes=[pltpu.VMEM((3, sb, DH), F32)] + (exchange.sems if exchange else []),
        compiler_params=_cparams(("arbitrary", "arbitrary") if exchange else ("parallel", "arbitrary")),
    )(*args, *(exchange.ins if exchange else []))
    return res[:3], res[3:]


def _step(x, target, norm1_g, q_norm_g, k_norm_g, gn_g, gn_b, norm2_g, shards, core):
    S = x.shape[0]
    tm = min(512, S)
    ts = min(256, S)
    tk = min(2048, S)
    tk2 = min(4096, S)
    later_shards = shards[1:]
    gq3 = q_norm_g.reshape(ATT_HEADS, 1, DH)
    gk3 = k_norm_g.reshape(ATT_HEADS, 1, DH)
    slopes3 = jnp.asarray(np.broadcast_to(ALIBI[:, None, None], (ATT_HEADS, 1, DH)).copy())

    xn, (g_in,) = _rmsnorm_fwd(x, norm1_g, tm, _gather_exchange(shards[:1]))
    w_in = _cols(g_in)
    w_att = w_in[:, :3 * ATT_W].reshape(D_MODEL, 3, ATT_HEADS, DH).transpose(0, 2, 1, 3).reshape(D_MODEL, 3 * ATT_W)
    w_rest = w_in[:, 3 * ATT_W:]
    w_att_t, w_rest_t = w_att.T, w_rest.T
    z_att = _mm("in_proj_att", xn, w_att, "nn", tm, 3 * ATT_W, D_MODEL,
                [(_sds((S, 3 * ATT_W), F32), _tile_ij(tm, 3 * ATT_W))], _epi_store(F32))[0]
    (z_rest,), gathered = _mm("in_proj_rest", xn, w_rest, "nn", ts, REST_W, D_MODEL,
                              [(_sds((S, REST_W), BF16), _tile_ij(ts, REST_W))], _epi_store(BF16),
                              exchange=_gather_exchange(later_shards))
    w_pa, w_pb, w_out, w_up, w_down = [f(g) for f, g in zip((_cols, _rows_of, _rows_of, _cols, _rows_of), gathered)]
    w_up_t, w_down_t, w_out_t, w_pa_t, w_pb_t = w_up.T, w_down.T, w_out.T, w_pa.T, w_pb.T
    done = []
    for gi, ((_, d), nb) in enumerate(zip(ATT_GROUPS, ATT_BLOCKS_PER_STEP)):
        last = gi == len(ATT_GROUPS) - 1
        done.append(_att_fwd_group(z_att, gq3, gk3, slopes3, gi, d, nb, others=tuple(done) if last else ()))
    o_a, lmix = done[-1]
    o_r, o_pre, states = _ret_fwd(z_rest, gn_g, gn_b)
    x1, y, pa, pb, xn2 = _merge_fwd(o_a, o_r, z_rest, x, w_pa, w_pb, w_out, norm2_g, min(256, S))

    def epi_up(acc, ex, out):
        r = jnp.maximum(acc, 0.0)
        out[0][...] = (r * r).astype(BF16)
        out[1][...] = r.astype(BF16)

    h, relu_u = _mm("mlp_up", xn2, w_up, "nn", tm, D_FF, D_MODEL,
                    [(_sds((S, D_FF), BF16), _tile_ij(tm, D_FF)), (_sds((S, D_FF), BF16), _tile_ij(tm, D_FF))], epi_up)

    def epi_down(acc, ex, out):
        diff = ex[0][...] + acc - ex[1][...]
        out[0][...] = diff * (1.0 / D_MODEL)
        out[1][...] = jnp.broadcast_to(jnp.sum(diff * diff) * (1.0 / (8 * LANES)), (8, LANES))

    row_tile = _tile_ij(tm, D_MODEL)
    dx2, loss_parts = _mm(
        "mlp_down_loss", h, w_down, "nn", tm, D_MODEL, D_FF,
        [(_sds((S, D_MODEL), F32), row_tile),
         (_sds((S // tm * 8, LANES), F32), pl.BlockSpec((8, LANES), lambda i, j, k: (i, 0)))],
        epi_down, extras=[(x1, row_tile), (target, row_tile)])
    loss_local = jnp.sum(loss_parts) * (0.5 / D_MODEL)

    def epi_du(acc, ex, out):
        out[0][...] = (acc * (2.0 * ex[0][...].astype(F32))).astype(BF16)

    du = _mm("mlp_down_bwd", dx2, w_down_t, "nn", tm, D_FF, D_MODEL,
             [(_sds((S, D_FF), BF16), _tile_ij(tm, D_FF))], epi_du, extras=[(relu_u, _tile_ij(tm, D_FF))])[0]
    gw_down = _mm("gw_down", h, dx2, "tn", 1024, D_MODEL, tk,
                  [(_sds((D_FF, D_MODEL), BF16), _tile_ij(1024, D_MODEL))], _epi_store(BF16))[0]
    gw_up = _mm("gw_up", xn2, du, "tn", D_MODEL, 512, tk2,
                [(_sds((N_DEV, D_MODEL, 512), BF16), pl.BlockSpec((None, D_MODEL, 512), lambda i, j, k: (j, 0, 0)))],
                _epi_store(BF16))[0]

    vec = pl.BlockSpec((1, D_MODEL), lambda i, j, k: (0, 0))
    seq_sem = ("arbitrary", "arbitrary", "arbitrary")

    def epi_norm2(acc, ex, out):
        dx, dg = _rms_bwd(acc, ex[0][...], ex[2][...])
        out[0][...] = ex[1][...] + dx
        _acc_rows(out[1], dg, pl.program_id(0) == 0)

    dx1, g_norm2 = _mm(
        "mlp_up_bwd", du, w_up_t, "nn", tm, D_MODEL, D_FF,
        [(_sds((S, D_MODEL), F32), row_tile), (_sds((1, D_MODEL), F32), vec)],
        epi_norm2, extras=[(x1, row_tile), (dx2, row_tile), (norm2_g, vec)], sem=seq_sem)

    def epi_dy(acc, ex, out):
        sa = jax.nn.sigmoid(ex[0][...].astype(F32))
        sb = jax.nn.sigmoid(ex[1][...].astype(F32))
        out[0][...] = (acc * sa).astype(BF16)
        out[1][...] = (acc * sb).astype(BF16)
        out[2][:, :D_MODEL] = (acc * ex[2][...].astype(F32) * (sa * (1.0 - sa))).astype(BF16)
        out[2][:, D_MODEL:] = (acc * ex[3][...].astype(F32) * (sb * (1.0 - sb))).astype(BF16)

    ga_spec = pl.BlockSpec((tm, D_MODEL), lambda i, j, k: (i, R_GA // D_MODEL))
    gb_spec = pl.BlockSpec((tm, D_MODEL), lambda i, j, k: (i, R_GB // D_MODEL))
    gates_spec = pl.BlockSpec((tm, 2 * D_MODEL), lambda i, j, k: (i, R_GA // (2 * D_MODEL)))
    dpa, dpb, dz_rest = _mm(
        "out_proj_bwd", dx1, w_out_t, "nn", tm, D_MODEL, D_MODEL,
        [(_sds((S, D_MODEL), BF16), row_tile), (_sds((S, D_MODEL), BF16), row_tile), (_sds((S, REST_W), BF16), gates_spec)],
        epi_dy, extras=[(z_rest, ga_spec), (z_rest, gb_spec), (pa, row_tile), (pb, row_tile)])
    gw_out = _mm("gw_out", y, dx1, "tn", D_MODEL, D_MODEL, tk,
                 [(_sds((D_MODEL, D_MODEL), BF16), _tile_ij(D_MODEL, D_MODEL))], _epi_store(BF16))[0]
    gw_pa = _mm("gw_proj_a", o_a, dpa, "tn", ATT_OUT_W, D_MODEL, tk,
                [(_sds((ATT_OUT_W, D_MODEL), BF16), _tile_ij(ATT_OUT_W, D_MODEL))], _epi_store(BF16))[0]
    gw_pb = _mm("gw_proj_b", o_r, dpb, "tn", 1024, D_MODEL, tk2,
                [(_sds((RET_V_W, D_MODEL), BF16), _tile_ij(1024, D_MODEL))], _epi_store(BF16))[0]

    def epi_doa(acc, ex, out):
        out[0][...] = acc
        prod = acc * ex[0][...]
        out[1][...] = jnp.concatenate(
            [jnp.broadcast_to(jnp.sum(prod[:, j * DH:(j + 1) * DH], axis=-1, keepdims=True), (prod.shape[0], DH))
             for j in range(HPG)], axis=1)

    slot_tile = _tile_ij(tm, ATT_OUT_W)
    do_a, delta = _mm("proj_a_bwd", dpa, w_pa_t, "nn", tm, ATT_OUT_W, D_MODEL,
                      [(_sds((S, ATT_OUT_W), F32), slot_tile), (_sds((S, ATT_OUT_W), F32), slot_tile)],
                      epi_doa, extras=[(o_a, slot_tile)])
    owed = [_chip_core(_by_owner_cols(gw_pa)), _chip_core(gw_pb.reshape(N_DEV, -1, D_MODEL)),
            _chip_core(gw_out.reshape(N_DEV, -1, D_MODEL)), _chip_core(gw_up), _chip_core(gw_down.reshape(N_DEV, -1, D_MODEL))]
    names = ("w_proj_a", "w_proj_b", "w_out", "w_up", "w_down")

    def epi_dor(acc, ex, out):
        cvs = [slice(h * RET_V, (h + 1) * RET_V) for h in range(RET_HEADS)]
        do, dgr, dg, db = _norm_gate_bwd([acc[:, cv] for cv in cvs], [ex[0][:, cv] for cv in cvs],
                                         [ex[1][:, cv].astype(F32) for cv in cvs],
                                         [ex[2][:, cv] for cv in cvs], [ex[3][:, cv] for cv in cvs])
        for h, cv in enumerate(cvs):
            out[0][:, cv] = do[h].astype(BF16)
            out[1][:, cv] = dgr[h].astype(BF16)
        first = pl.program_id(0) == 0
        _acc_rows(out[2], jnp.concatenate(dg, axis=1), first)
        _acc_rows(out[3], jnp.concatenate(db, axis=1), first)

    wide_tile = _tile_ij(ts, RET_V_W)
    gate_tile = pl.BlockSpec((ts, RET_V_W), lambda i, j, k: (i, R_G // RET_V_W))
    wide_vec = pl.BlockSpec((1, RET_V_W), lambda i, j, k: (0, 0))
    (do_ret, dz_rest, g_gn_g, g_gn_b), got = _mm(
        "proj_b_bwd", dpb, w_pb_t, "nn", ts, RET_V_W, D_MODEL,
        [(_sds((S, RET_V_W), BF16), wide_tile), (_sds((S, REST_W), BF16), gate_tile),
         (_sds((1, RET_V_W), F32), wide_vec), (_sds((1, RET_V_W), F32), wide_vec)],
        epi_dor, extras=[(o_pre, wide_tile), (z_rest, gate_tile), (gn_g, wide_vec), (gn_b, wide_vec)],
        exchange=_pair_exchange(owed), in_place=(dz_rest, 1))
    chip_sums = [_pair_sum(f"pair_sum_{n}", g, r, core, min(256, g.shape[2])) for n, g, r in zip(names, owed, got)]

    dz_rest = _ret_bwd(do_ret, states, z_rest, dz_rest)
    dz_att, gq_parts, gk_parts, parts_late = None, [], [], None
    for gi, ((_, d), nb) in enumerate(zip(ATT_GROUPS, ATT_BLOCKS_PER_STEP)):
        last = gi == len(ATT_GROUPS) - 1
        (dz_att, gq_p, gk_p), parts = _att_bwd_group(z_att, dz_att, do_a, delta, lmix, gq3, gk3, slopes3, gi, d, nb,
                                                     exchange=_chip_exchange(chip_sums) if last else None)
        parts_late = parts if last else parts_late
        gq_parts.append(gq_p)
        gk_parts.append(gk_p)
    g_qn = jnp.concatenate(gq_parts, axis=0).reshape(1, ATT_HEADS, DH)
    g_kn = jnp.concatenate(gk_parts, axis=0).reshape(1, ATT_HEADS, DH)

    gw_att = _mm("gw_in_att", xn, dz_att, "tn", D_MODEL, ATT_W, tk,
                 [(_sds((D_MODEL, 3 * ATT_W), BF16), _tile_ij(D_MODEL, ATT_W))], _epi_store(BF16))[0]
    gw_rest = _mm("gw_in_rest", xn, dz_rest, "tn", D_MODEL, 1024, tk2,
                  [(_sds((D_MODEL, REST_W), BF16), _tile_ij(D_MODEL, 1024))], _epi_store(BF16))[0]
    gw_att = gw_att.reshape(D_MODEL, ATT_HEADS, 3, DH).transpose(0, 2, 1, 3).reshape(D_MODEL, 3 * ATT_W)
    gw_in = jnp.concatenate([gw_att, gw_rest], axis=1)
    owed_in = _chip_core(_by_owner_cols(gw_in))
    (dxn_att,), (got_in,) = _mm("in_proj_att_bwd", dz_att, w_att_t, "nn", tm, D_MODEL, 3 * ATT_W,
                                [(_sds((S, D_MODEL), F32), row_tile)], _epi_store(F32), exchange=_pair_exchange([owed_in]))
    chip_sum_in = _pair_sum("pair_sum_w_in", owed_in, got_in, core, min(256, owed_in.shape[2]))

    def epi_norm1(acc, ex, out):
        dx, dg = _rms_bwd(acc + ex[0][...], ex[1][...], ex[3][...])
        out[0][...] = ex[2][...] + dx
        _acc_rows(out[1], dg, pl.program_id(0) == 0)

    short_tile = _tile_ij(ts, D_MODEL)
    (grad_x, g_norm1), parts_in = _mm(
        "in_proj_rest_bwd", dz_rest, w_rest_t, "nn", ts, D_MODEL, REST_W,
        [(_sds((S, D_MODEL), F32), short_tile), (_sds((1, D_MODEL), F32), vec)],
        epi_norm1, extras=[(dxn_att, short_tile), (x, short_tile), (dx1, short_tile), (norm1_g, vec)],
        exchange=_chip_exchange([chip_sum_in]))

    small = (g_norm1, g_qn, g_kn, g_gn_g, g_gn_b, g_norm2)
    return loss_local, grad_x, list(parts_in) + list(parts_late), small


def _position():
    return lax.axis_index("x"), lax.axis_index("y"), lax.axis_index("c")


def _other_chips(x, y):
    return [(1 - x, y), (x, 1 - y), (1 - x, 1 - y)]


_ANY = pl.BlockSpec(memory_space=pl.ANY)


def _gather_exchange(shards):
    nw = len(shards)

    def plan(x_refs, out_refs, sems):
        send_sems, recv_sems, local_sems = sems
        x, y, c = _position()
        me, sibling = (x, y, c), (x, y, 1 - c)
        chips = _other_chips(x, y)

        def copy(w, k, block, to, own=False):
            px, py, pc = block
            rows = out_refs[w].at[4 * px + 2 * py + pc]
            return pltpu.make_async_remote_copy(
                src_ref=x_refs[w] if own else rows, dst_ref=rows,
                send_sem=send_sems.at[7 * w + k], recv_sem=recv_sems.at[7 * w + k], device_id=to, device_id_type=MESH)

        def mine(w):
            return pltpu.make_async_copy(x_refs[w], out_refs[w].at[4 * x + 2 * y + c], local_sems.at[w])

        def own_sends(w):
            return [copy(w, 0, me, sibling, own=True)] + [copy(w, 1 + j, me, (*chip, c), own=True)
                                                          for j, chip in enumerate(chips)]

        def start():
            for w in range(nw):
                mine(w).start()
                for cp in own_sends(w):
                    cp.start()

        def relay():
            for j, chip in enumerate(chips):
                for w in range(nw):
                    copy(w, 1 + j, (*chip, c), me).wait_recv()
                    copy(w, 4 + j, (*chip, c), sibling).start()

        def finish():
            for w in range(nw):
                copy(w, 0, sibling, me).wait_recv()
                for j, chip in enumerate(chips):
                    copy(w, 4 + j, (*chip, 1 - c), me).wait_recv()
            for w in range(nw):
                for cp in own_sends(w):
                    cp.wait_send()
                for j, chip in enumerate(chips):
                    copy(w, 4 + j, (*chip, c), sibling).wait_send()
                mine(w).wait()

        return start, relay, finish

    return _Exchange(shards, [_sds((N_DEV,) + s.shape, s.dtype) for s in shards],
                     [pltpu.SemaphoreType.DMA((7 * nw,)), pltpu.SemaphoreType.DMA((7 * nw,)),
                      pltpu.SemaphoreType.DMA((nw,))], plan)


def _run_exchange(name, exchange):
    n_in, n_out = len(exchange.ins), len(exchange.out_shapes)

    def body(*refs):
        start, relay, finish = exchange.split(refs[:n_in], refs[n_in:n_in + n_out], refs[n_in + n_out:])
        start()
        if relay is not None:
            relay()
        finish()

    return pl.pallas_call(
        body, name=name, out_shape=exchange.out_shapes,
        in_specs=[_ANY] * n_in, out_specs=[_ANY] * n_out, scratch_shapes=exchange.sems,
    )(*exchange.ins)


def _pair_exchange(grads):
    ng = len(grads)

    def plan(g_refs, out_refs, sems):
        send_sems, recv_sems = sems
        x, y, c = _position()

        def copies():
            return [pltpu.make_async_remote_copy(
                src_ref=g_refs[w].at[:, 1 - c], dst_ref=out_refs[w], send_sem=send_sems.at[w],
                recv_sem=recv_sems.at[w], device_id=(x, y, 1 - c), device_id_type=MESH) for w in range(ng)]

        def start():
            for cp in copies():
                cp.start()

        def finish():
            for cp in copies():
                cp.wait()

        return start, None, finish

    return _Exchange(grads, [_sds((N_CHIP,) + g.shape[2:], g.dtype) for g in grads],
                     [pltpu.SemaphoreType.DMA((ng,)), pltpu.SemaphoreType.DMA((ng,))], plan)


def _small_all_gather(small):
    def plan(in_refs, out_refs, sems):
        (s_ref,), (s_out,) = in_refs, out_refs
        send_sems, recv_sems, local_sem = sems
        x, y, c = _position()
        me_id = 4 * x + 2 * y + c
        flips = [(a, b, e) for a in (0, 1) for b in (0, 1) for e in (0, 1)][1:]
        peers = [(x ^ a, y ^ b, c ^ e) for a, b, e in flips]

        def start():
            pltpu.make_async_copy(s_ref, s_out.at[me_id], local_sem).start()
            for k, p in enumerate(peers):
                pltpu.make_async_remote_copy(
                    src_ref=s_ref, dst_ref=s_out.at[me_id], send_sem=send_sems.at[k], recv_sem=recv_sems.at[k],
                    device_id=p, device_id_type=MESH).start()

        def finish():
            for k, (px, py, pc) in enumerate(peers):
                pltpu.make_async_remote_copy(
                    src_ref=s_ref, dst_ref=s_out.at[4 * px + 2 * py + pc], send_sem=send_sems.at[k],
                    recv_sem=recv_sems.at[k], device_id=(px, py, pc), device_id_type=MESH).wait()
            pltpu.make_async_copy(s_ref, s_out.at[me_id], local_sem).wait()

        return start, None, finish

    return _run_exchange("small_grad_all_gather", _Exchange(
        [small], [_sds((N_DEV,) + small.shape, small.dtype)],
        [pltpu.SemaphoreType.DMA((7,)), pltpu.SemaphoreType.DMA((7,)), pltpu.SemaphoreType.DMA], plan))[0]


def _pair_sum(name, g, got, core, tr):
    n_chip, _, R, C = g.shape

    def body(c_ref, a_ref, b_ref, o_ref):
        del c_ref
        o_ref[...] = (a_ref[...].astype(F32) + b_ref[...].astype(F32)).astype(o_ref.dtype)

    return pl.pallas_call(
        body, name=name,
        grid_spec=pltpu.PrefetchScalarGridSpec(
            num_scalar_prefetch=1, grid=(n_chip, R // tr),
            in_specs=[pl.BlockSpec((None, None, tr, C), lambda ch, i, c_ref: (ch, c_ref[0], i, 0)),
                      pl.BlockSpec((None, tr, C), lambda ch, i, c_ref: (ch, i, 0))],
            out_specs=pl.BlockSpec((None, tr, C), lambda ch, i, c_ref: (ch, i, 0))),
        out_shape=_sds(got.shape, got.dtype),
        compiler_params=_cparams(("parallel", "parallel")),
    )(core, g, got)


def _chip_exchange(parts):
    ng = len(parts)

    def plan(p_refs, out_refs, sems):
        send_sems, recv_sems, local_sems = sems
        x, y, c = _position()
        my_chip = 2 * x + y

        def copies():
            local = [pltpu.make_async_copy(p_refs[w].at[my_chip], out_refs[w].at[my_chip], local_sems.at[w])
                     for w in range(ng)]
            remote = [pltpu.make_async_remote_copy(
                src_ref=p_refs[w].at[2 * cx + cy], dst_ref=out_refs[w].at[my_chip],
                send_sem=send_sems.at[3 * w + k], recv_sem=recv_sems.at[3 * w + k],
                device_id=(cx, cy, c), device_id_type=MESH)
                for w in range(ng) for k, (cx, cy) in enumerate(_other_chips(x, y))]
            return local + remote

        def start():
            for cp in copies():
                cp.start()

        def finish():
            for cp in copies():
                cp.wait()

        return start, None, finish

    return _Exchange(parts, [_sds(p.shape, p.dtype) for p in parts],
                     [pltpu.SemaphoreType.DMA((3 * ng,)), pltpu.SemaphoreType.DMA((3 * ng,)),
                      pltpu.SemaphoreType.DMA((ng,))], plan)


def _adamw(name, parts, w, m, v, tr):
    n_parts = parts.shape[0]
    R, C = w.shape

    def body(p_ref, w_ref, m_ref, v_ref, g_ref, d_ref, mo_ref, vo_ref):
        g = p_ref[0].astype(F32)
        for i in range(1, n_parts):
            g = g + p_ref[i].astype(F32)
        m_new = ADAM_B1 * m_ref[...] + (1.0 - ADAM_B1) * g
        v_new = ADAM_B2 * v_ref[...] + (1.0 - ADAM_B2) * (g * g)
        m_hat = m_new / (1.0 - ADAM_B1 ** ADAM_STEP)
        v_hat = v_new / (1.0 - ADAM_B2 ** ADAM_STEP)
        g_ref[...] = g
        d_ref[...] = -ADAM_LR * (m_hat / (jnp.sqrt(v_hat) + ADAM_EPS) + ADAM_WD * w_ref[...])
        mo_ref[...] = m_new
        vo_ref[...] = v_new

    tile = pl.BlockSpec((tr, C), lambda i: (i, 0))
    return pl.pallas_call(
        body, name=name, grid=(R // tr,),
        in_specs=[pl.BlockSpec((n_parts, tr, C), lambda i: (0, i, 0)), tile, tile, tile],
        out_specs=[tile] * 4,
        out_shape=[_sds((R, C), F32)] * 4,
        compiler_params=_cparams(("parallel",)),
    )(parts, w, m, v)


def _flat_small(arrs):
    return jnp.concatenate([a.reshape(-1) for a in arrs]).reshape(SMALL_ROWS, LANES)


def _cols(g):
    return g.transpose(1, 0, 2).reshape(g.shape[1], -1)


def _rows_of(g):
    return g.reshape(-1, g.shape[2])


def _by_owner_cols(g):
    rows, cols = g.shape
    return g.reshape(rows, N_DEV, cols // N_DEV).transpose(1, 0, 2)


def _chip_core(g):
    return g.reshape((N_CHIP, 2) + g.shape[1:])


def kernel(x, norm1_g, w_in, q_norm_g, k_norm_g, ret_gn_g, ret_gn_b, w_proj_a, w_proj_b, w_out, norm2_g, w_up, w_down, loss_target, m_norm1_g, m_w_in, m_q_norm_g, m_k_norm_g, m_ret_gn_g, m_ret_gn_b, m_w_proj_a, m_w_proj_b, m_w_out, m_norm2_g, m_w_up, m_w_down, v_norm1_g, v_w_in, v_q_norm_g, v_k_norm_g, v_ret_gn_g, v_ret_gn_b, v_w_proj_a, v_w_proj_b, v_w_out, v_norm2_g, v_w_up, v_w_down):
    big_w = (w_in, w_proj_a, w_proj_b, w_out, w_up, w_down)
    big_m = (m_w_in, m_w_proj_a, m_w_proj_b, m_w_out, m_w_up, m_w_down)
    big_v = (v_w_in, v_w_proj_a, v_w_proj_b, v_w_out, v_w_up, v_w_down)
    small_w = (norm1_g, q_norm_g, k_norm_g, ret_gn_g, ret_gn_b, norm2_g)
    small_m = (m_norm1_g, m_q_norm_g, m_k_norm_g, m_ret_gn_g, m_ret_gn_b, m_norm2_g)
    small_v = (v_norm1_g, v_q_norm_g, v_k_norm_g, v_ret_gn_g, v_ret_gn_b, v_norm2_g)

    shards = [w[0].astype(BF16) for w in big_w]
    core = lax.axis_index("c").astype(jnp.int32).reshape(1)
    loss_local, grad_x, parts, small_g = _step(
        x[0], loss_target[0], norm1_g, q_norm_g[0], k_norm_g[0], ret_gn_g, ret_gn_b, norm2_g, shards, core)
    small_all = _small_all_gather(_flat_small(small_g))
    names = ("w_in", "w_proj_a", "w_proj_b", "w_out", "w_up", "w_down")

    res = {}
    for n, p, w, m, v in zip(names, parts, big_w, big_m, big_v):
        outs = _adamw(f"adamw_{n}", p, w[0], m[0], v[0], min(128, w.shape[1]))
        res[n] = [o[None] for o in outs]
    s_outs = _adamw("adamw_small", small_all, _flat_small(small_w), _flat_small(small_m), _flat_small(small_v), SMALL_ROWS)
    small_names = ("norm1_g", "q_norm_g", "k_norm_g", "ret_gn_g", "ret_gn_b", "norm2_g")
    for n in small_names:
        res[n] = []
    for o in s_outs:
        flat, off = o.reshape(-1), 0
        for n, w in zip(small_names, small_w):
            res[n].append(flat[off:off + w.size].reshape(w.shape))
            off += w.size

    order = ("norm1_g", "w_in", "q_norm_g", "k_norm_g", "ret_gn_g", "ret_gn_b", "w_proj_a", "w_proj_b", "w_out",
             "norm2_g", "w_up", "w_down")
    loss = lax.psum(loss_local, MESH_AXES)
    return (loss, grad_x[None], *[res[n][0] for n in order], *[res[n][1] for n in order],
            *[res[n][2] for n in order], *[res[n][3] for n in order])
```

```python
import math

import numpy as np
import jax
import jax.numpy as jnp
from jax import lax
from jax.experimental import pallas as pl
from jax.experimental.pallas import tpu as pltpu

F32 = jnp.float32
BF16 = jnp.bfloat16

D_MODEL = 1024
ATT_GROUPS = ((128, 1), (512, 4), (2048, 16))
ATT_BLOCKS_PER_STEP = (16, 4, 1)
ATT_TOGETHER = 4
HPG = 4
ATT_HEADS = 12
DH = 128
BLK = 128
ATT_W = ATT_HEADS * DH
ATT_OUT_W = HPG * DH
RET_HEADS = 4
RET_QK = 256
RET_V = 512
RET_QK_W = RET_HEADS * RET_QK
RET_V_W = RET_HEADS * RET_V
CHUNK = 128
RET_PER_STEP = 4
D_FF = 4096
IN_W = 12800
REST_W = IN_W - 3 * ATT_W
EPS = 1e-6
ADAM_LR, ADAM_B1, ADAM_B2, ADAM_EPS, ADAM_WD, ADAM_STEP = 0.001, 0.9, 0.999, 1e-08, 0.01, 10
N_DEV = 8
N_CHIP = 4
MESH_AXES = ("x", "y", "c")
MESH = pl.DeviceIdType.MESH
VMEM_LIMIT = 56 * 1024 * 1024
LANES = 128
NEG = -1e30

_NN = (((1,), (0,)), ((), ()))
_NT = (((1,), (1,)), ((), ()))
_TN = (((0,), (0,)), ((), ()))

R_Q, R_K, R_V, R_G, R_GA, R_GB = 0, 1024, 2048, 4096, 6144, 7168

LOG_GAMMA = [float(v) for v in np.log(1.0 - 2.0 ** (-5.0 - np.arange(RET_HEADS, dtype=np.float32))).astype(np.float32)]
ALIBI = np.asarray(2.0 ** (-8.0 * np.arange(1, ATT_HEADS + 1, dtype=np.float32) / ATT_HEADS), np.float32)

SMALL_ROWS = (1024 + 1536 + 1536 + 2048 + 2048 + 1024) // LANES


def _dot(a, b, dims=_NN):
    return lax.dot_general(a, b, dims, preferred_element_type=F32)


def _cparams(sem):
    return pltpu.CompilerParams(dimension_semantics=sem, vmem_limit_bytes=VMEM_LIMIT)


def _sds(shape, dtype):
    return jax.ShapeDtypeStruct(shape, dtype)


class _Exchange:
    def __init__(self, ins, out_shapes, sems, plan):
        self.ins, self.out_shapes, self.sems, self.plan = list(ins), list(out_shapes), list(sems), plan

    def split(self, in_refs, out_refs, sem_refs):
        return self.plan(in_refs, out_refs, sem_refs)


def _mm(name, a, b, mode, tm, tn, tk, outs, epi, extras=(), b_pro=None,
        sem=("parallel", "parallel", "arbitrary"), exchange=None, in_place=None):
    if mode == "nn":
        (M, K), (_, N) = a.shape, b.shape
        a_spec = pl.BlockSpec((tm, tk), lambda i, j, k: (i, k))
        dims = _NN
    else:
        (K, M), (_, N) = a.shape, b.shape
        a_spec = pl.BlockSpec((tk, tm), lambda i, j, k: (k, i))
        dims = _TN
    assert M % tm == 0 and N % tn == 0 and K % tk == 0, (name, M, N, K, tm, tn, tk)
    nk = K // tk
    whole_b = dict(pipeline_mode=pl.Buffered(1)) if (nk == 1 and N == tn) else {}
    b_spec = pl.BlockSpec((tk, tn), lambda i, j, k: (k, j), **whole_b)
    n_ex, n_out = len(extras), len(outs)
    grid = (M // tm, N // tn, nk)
    n_xi = len(exchange.ins) if exchange else 0
    n_xo = len(exchange.out_shapes) if exchange else 0
    n_acc = 1 if nk > 1 else 0

    n_ip = 1 if in_place else 0

    def body(a_ref, b_ref, *rest):
        ex, rest = rest[:n_ex], rest[n_ex:]
        x_in, rest = rest[:n_xi], rest[n_xi + n_ip:]
        out, rest = rest[:n_out], rest[n_out:]
        x_out, rest = rest[:n_xo], rest[n_xo:]
        step =(pl.program_id(0) * grid[1] + pl.program_id(1)) * grid[2] + pl.program_id(2)
        n_steps = grid[0] * grid[1] * grid[2]
        if exchange:
            start, relay, finish = exchange.split(x_in, x_out, rest[n_acc:])
            pl.when(step == 0)(start)
        bv = b_ref[...]
        if b_pro is not None:
            bv = b_pro(bv)
        part = _dot(a_ref[...].astype(BF16), bv.astype(BF16), dims)
        if nk == 1:
            epi(part, ex, out)
        else:
            acc_ref = rest[0]
            k = pl.program_id(2)

            @pl.when(k == 0)
            def _():
                acc_ref[...] = part

            @pl.when(k > 0)
            def _():
                acc_ref[...] += part

            @pl.when(k == nk - 1)
            def _():
                epi(acc_ref[...], ex, out)
        if exchange:
            if relay is not None:
                pl.when(step == (7 * n_steps) // 8)(relay)
            pl.when(step == n_steps - 1)(finish)

    res = pl.pallas_call(
        body,
        name=name,
        grid=grid,
        in_specs=[a_spec, b_spec] + [s for _, s in extras] + [_ANY] * (n_xi + n_ip),
        out_specs=[s for _, s in outs] + [_ANY] * n_xo,
        out_shape=[o for o, _ in outs] + (exchange.out_shapes if exchange else []),
        scratch_shapes=([pltpu.VMEM((tm, tn), F32)] if nk > 1 else []) + (exchange.sems if exchange else []),
        input_output_aliases={2 + n_ex + n_xi: in_place[1]} if in_place else {},
        compiler_params=_cparams(("arbitrary",) * 3 if exchange else sem),
    )(a, b, *[e for e, _ in extras], *(exchange.ins if exchange else []), *([in_place[0]] if in_place else []))
    return (res[:n_out], res[n_out:]) if exchange else res


def _tile_ij(tm, tn):
    return pl.BlockSpec((tm, tn), lambda i, j, k: (i, j))


def _epi_store(dtype):
    def epi(acc, ex, out):
        out[0][...] = acc.astype(dtype)
    return epi


def _rms_rows(x):
    return lax.rsqrt(jnp.mean(x * x, axis=-1, keepdims=True) + EPS)


def _acc_rows(ref, part, first):
    @pl.when(first)
    def _():
        ref[...] = part

    @pl.when(jnp.logical_not(first))
    def _():
        ref[...] += part


def _rmsnorm_fwd(x, g, tm, exchange):
    S, Dm = x.shape
    n_steps = S // tm
    n_xi, n_xo = len(exchange.ins), len(exchange.out_shapes)

    def body(x_ref, g_ref, *rest):
        x_in, o_ref, x_out, sems = rest[:n_xi], rest[n_xi], rest[n_xi + 1:n_xi + 1 + n_xo], rest[n_xi + 1 + n_xo:]
        start, relay, finish = exchange.split(x_in, x_out, sems)
        step = pl.program_id(0)
        pl.when(step == 0)(start)
        xv = x_ref[...]
        o_ref[...] = (xv * _rms_rows(xv) * g_ref[...]).astype(BF16)

        @pl.when(step == n_steps - 1)
        def _():
            relay()
            finish()

    res = pl.pallas_call(
        body, name="rmsnorm1_fwd", grid=(n_steps,),
        in_specs=[pl.BlockSpec((tm, Dm), lambda i: (i, 0)), pl.BlockSpec((1, Dm), lambda i: (0, 0))] + [_ANY] * n_xi,
        out_specs=[pl.BlockSpec((tm, Dm), lambda i: (i, 0))] + [_ANY] * n_xo,
        out_shape=[_sds((S, Dm), BF16)] + exchange.out_shapes,
        scratch_shapes=exchange.sems,
        compiler_params=_cparams(("arbitrary",)),
    )(x, g, *exchange.ins)
    return res[0], res[1:]


def _rows(ref, r, b, d):
    if d == 1:
        return ref[b * BLK:(b + 1) * BLK, :]
    return ref[pl.ds(b * BLK * d + r, BLK, stride=d), :]


def _put_rows(ref, r, b, d, val):
    if d == 1:
        ref[b * BLK:(b + 1) * BLK, :] = val
    else:
        ref[pl.ds(b * BLK * d + r, BLK, stride=d), :] = val


def _head_norm(x, g):
    r = _rms_rows(x)
    xh = x * r
    return (xh * g).astype(BF16), xh, r


def _head_norm_bwd(dyn, xh, r, g):
    dxh = dyn * g
    dx = r * (dxh - xh * jnp.mean(dxh * xh, axis=-1, keepdims=True))
    return dx, jnp.sum(dyn * xh, axis=0, keepdims=True)


def _att_mask_bias(slope, d, first):
    qi = lax.broadcasted_iota(jnp.int32, (BLK, 2 * BLK), 0)
    kj = lax.broadcasted_iota(jnp.int32, (BLK, 2 * BLK), 1)
    dist = BLK + qi - kj
    valid = (dist >= 0) & (dist <= BLK)
    if first is not None:
        valid = valid & (jnp.logical_not(first) | (kj >= BLK))
    bias = -slope * (dist * d).astype(F32)
    return valid, bias


def _att_specs(gi, d, nb, S):
    span = BLK * d
    sb = span * nb
    nspan = S // span
    before = lambda n: jnp.maximum(n * nb - 1, 0)
    after = lambda n: jnp.minimum((n + 1) * nb, nspan - 1)
    zcol = lambda j, kind: 3 * (gi * HPG + j) + kind
    cur = lambda kind: pl.BlockSpec((sb, DH), lambda j, n: (n, zcol(j, kind)))
    prev = lambda kind: pl.BlockSpec((span, DH), lambda j, n: (before(n), zcol(j, kind)))
    nxt = lambda kind: pl.BlockSpec((span, DH), lambda j, n: (after(n), zcol(j, kind)))
    slot = pl.BlockSpec((sb, DH), lambda j, n: (n, j))
    slot_next = pl.BlockSpec((span, DH), lambda j, n: (after(n), j))
    head = pl.BlockSpec((None, 1, DH), lambda j, n: (gi * HPG + j, 0, 0))
    return cur, prev, nxt, slot, slot_next, head


def _att_fwd_group(z_att, gq3, gk3, slopes3, gi, d, nb, others=()):
    S = z_att.shape[0]
    nsb = S // (BLK * d * nb)
    scale = DH ** -0.5
    n_other = len(others)

    def body(q_ref, k_ref, v_ref, kp_ref, vp_ref, gq_ref, gk_ref, sl_ref, *rest):
        other_refs, (o_ref, l_ref) = rest[:2 * n_other], rest[2 * n_other:]
        slope = sl_ref[...][:, :1]
        valid0, bias = _att_mask_bias(slope, d, pl.program_id(1) == 0)
        valid_in, _ = _att_mask_bias(slope, d, None)
        gq, gk = gq_ref[...], gk_ref[...]
        memo = {}

        def get(kind, r, b):
            if (kind, r, b) not in memo:
                if kind == "k":
                    val = _head_norm(_rows(k_ref if b >= 0 else kp_ref, r, max(b, 0), d), gk)[0]
                else:
                    val = _rows(v_ref if b >= 0 else vp_ref, r, max(b, 0), d).astype(BF16)
                memo[(kind, r, b)] = val
            return memo[(kind, r, b)]

        units = [(r, b) for r in range(d) for b in range(nb)]
        for c0 in range(0, len(units), ATT_TOGETHER):
            us = units[c0:c0 + ATT_TOGETHER]
            q = [_head_norm(_rows(q_ref, r, b, d), gq)[0] for r, b in us]
            k2 = [jnp.concatenate([get("k", r, b - 1), get("k", r, b)], axis=0) for r, b in us]
            v2 = [jnp.concatenate([get("v", r, b - 1), get("v", r, b)], axis=0) for r, b in us]
            s = [jnp.where(valid0 if b == 0 else valid_in, _dot(q[i], k2[i], _NT) * scale + bias, NEG)
                 for i, (r, b) in enumerate(us)]
            m = [jnp.max(si, axis=-1, keepdims=True) for si in s]
            p = [jnp.exp(si - mi) for si, mi in zip(s, m)]
            den = [jnp.sum(pi, axis=-1, keepdims=True) for pi in p]
            o = [_dot(pi.astype(BF16), vi) / di for pi, vi, di in zip(p, v2, den)]
            for i, (r, b) in enumerate(us):
                _put_rows(o_ref, r, b, d, o[i])
                _put_rows(l_ref, r, b, d, jnp.broadcast_to(m[i] + jnp.log(den[i]), (BLK, DH)))
        if n_other:
            os_ = [ref[...] for ref in other_refs[:n_other]] + [o_ref[...]]
            ls_ = [ref[...] for ref in other_refs[n_other:]] + [l_ref[...]]
            m = ls_[0]
            for l in ls_[1:]:
                m = jnp.maximum(m, l)
            es = [jnp.exp(l - m) for l in ls_]
            tot, mix = es[0], es[0] * os_[0]
            for e, o in zip(es[1:], os_[1:]):
                tot, mix = tot + e, mix + e * o
            o_ref[...] = mix / tot
            l_ref[...] = m + jnp.log(tot)

    cur, prev, _, slot, _, head = _att_specs(gi, d, nb, S)
    return pl.pallas_call(
        body, name=f"att_fwd_g{gi}", grid=(HPG, nsb),
        in_specs=[cur(0), cur(1), cur(2), prev(1), prev(2), head, head, head] + [slot] * (2 * n_other),
        out_specs=[slot, slot],
        out_shape=[_sds((S, ATT_OUT_W), F32), _sds((S, ATT_OUT_W), F32)],
        compiler_params=_cparams(("parallel", "arbitrary")),
    )(z_att, z_att, z_att, z_att, z_att, gq3, gk3, slopes3, *[o for o, _ in others], *[l for _, l in others])


def _ret_tables(lg):
    ri = lax.broadcasted_iota(jnp.int32, (CHUNK, CHUNK), 0)
    ci = lax.broadcasted_iota(jnp.int32, (CHUNK, CHUNK), 1)
    diff = (ri - ci).astype(F32)
    decay = jnp.where(diff >= 0, jnp.exp(lg * jnp.maximum(diff, 0.0)), 0.0)
    idx = lax.broadcasted_iota(jnp.int32, (CHUNK, 1), 0).astype(F32)
    xi = jnp.exp(lg * (idx + 1.0))
    zeta = jnp.exp(lg * (CHUNK - 1.0 - idx))
    return decay, xi, zeta, math.exp(lg * CHUNK)


def _ret_specs(nsteps, rev):
    idx = (lambda n: nsteps - 1 - n) if rev else (lambda n: n)
    rows = CHUNK * RET_PER_STEP
    qk = lambda off: pl.BlockSpec((rows, RET_QK_W), lambda n: (idx(n), off // RET_QK_W))
    vv = lambda off: pl.BlockSpec((rows, RET_V_W), lambda n: (idx(n), off // RET_V_W))
    par = pl.BlockSpec((1, RET_V_W), lambda n: (0, 0))
    wide = pl.BlockSpec((rows, RET_V_W), lambda n: (idx(n), 0))
    st = pl.BlockSpec((RET_HEADS, RET_PER_STEP, RET_QK, RET_V), lambda n: (0, idx(n), 0, 0))
    return qk, vv, par, wide, st


def _ret_fwd(z_rest, gn_g, gn_b):
    S = z_rest.shape[0]
    nch = S // CHUNK
    nsteps = nch // RET_PER_STEP

    def body(q_ref, k_ref, v_ref, gr_ref, g_ref, b_ref, or_ref, o_ref, st_ref, state):
        @pl.when(pl.program_id(0) == 0)
        def _():
            state[...] = jnp.zeros_like(state)

        for c in range(RET_PER_STEP):
            rc = slice(c * CHUNK, (c + 1) * CHUNK)
            for h in range(RET_HEADS):
                decay, xi, zeta, gch = _ret_tables(LOG_GAMMA[h])
                cq = slice(h * RET_QK, (h + 1) * RET_QK)
                cv = slice(h * RET_V, (h + 1) * RET_V)
                q = q_ref[rc, cq]
                kc32 = k_ref[rc, cq].astype(F32) * (RET_QK ** -0.5)
                kc = kc32.astype(BF16)
                v = v_ref[rc, cv]
                st = state[h]
                stb = st.astype(BF16)
                st_ref[h, c] = stb
                s = _dot(q, kc, _NT) * decay
                o = _dot(s.astype(BF16), v) + _dot(q, stb) * xi
                state[h] = st * gch + _dot((kc32 * zeta).astype(BF16), v, _TN)
                mu = jnp.mean(o, axis=-1, keepdims=True)
                cen = o - mu
                yh = cen * lax.rsqrt(jnp.mean(cen * cen, axis=-1, keepdims=True) + EPS)
                gr = gr_ref[rc, cv].astype(F32)
                or_ref[rc, cv] = ((yh * g_ref[:, cv] + b_ref[:, cv]) * (gr * jax.nn.sigmoid(gr))).astype(BF16)
                o_ref[rc, cv] = o

    qk, vv, par, wide, st = _ret_specs(nsteps, False)
    return pl.pallas_call(
        body, name="ret_fwd", grid=(nsteps,),
        in_specs=[qk(R_Q), qk(R_K), vv(R_V), vv(R_G), par, par],
        out_specs=[wide, wide, st],
        out_shape=[_sds((S, RET_V_W), BF16), _sds((S, RET_V_W), F32), _sds((RET_HEADS, nch, RET_QK, RET_V), BF16)],
        scratch_shapes=[pltpu.VMEM((RET_HEADS, RET_QK, RET_V), F32)],
        compiler_params=_cparams(("arbitrary",)),
    )(z_rest, z_rest, z_rest, z_rest, gn_g, gn_b)


def _merge_fwd(o_a, o_r, z_rest, x, wpa, wpb, wout, g2, tm):
    S = x.shape[0]

    def body(oa_ref, or_ref, ga_ref, gb_ref, x_ref, wpa_ref, wpb_ref, wo_ref, g2_ref,
             x1_ref, y_ref, pa_ref, pb_ref, xn2_ref):
        pa = _dot(oa_ref[...].astype(BF16), wpa_ref[...])
        pb = _dot(or_ref[...], wpb_ref[...])
        y = jax.nn.sigmoid(ga_ref[...].astype(F32)) * pa + jax.nn.sigmoid(gb_ref[...].astype(F32)) * pb
        yb = y.astype(BF16)
        x1 = x_ref[...] + _dot(yb, wo_ref[...])
        x1_ref[...] = x1
        y_ref[...] = yb
        pa_ref[...] = pa.astype(BF16)
        pb_ref[...] = pb.astype(BF16)
        xn2_ref[...] = (x1 * _rms_rows(x1) * g2_ref[...]).astype(BF16)

    row = lambda w: pl.BlockSpec((tm, w), lambda i: (i, 0))
    full = lambda a: pl.BlockSpec(a.shape, lambda i: (0, 0))
    return pl.pallas_call(
        body, name="merge_fwd", grid=(S // tm,),
        in_specs=[row(ATT_OUT_W), row(RET_V_W),
                  pl.BlockSpec((tm, D_MODEL), lambda i: (i, R_GA // D_MODEL)),
                  pl.BlockSpec((tm, D_MODEL), lambda i: (i, R_GB // D_MODEL)),
                  row(D_MODEL), full(wpa), full(wpb), full(wout), full(g2)],
        out_specs=[row(D_MODEL)] * 5,
        out_shape=[_sds((S, D_MODEL), F32)] + [_sds((S, D_MODEL), BF16)] * 4,
        compiler_params=_cparams(("parallel",)),
    )(o_a, o_r, z_rest, z_rest, x, wpa, wpb, wout, g2)


def _rms_bwd(dy, xv, g):
    r = _rms_rows(xv)
    xh = xv * r
    dg = dy * g
    dx = r * (dg - xh * jnp.mean(dg * xh, axis=-1, keepdims=True))
    return dx, jnp.sum(dy * xh, axis=0, keepdims=True)


def _norm_gate_bwd(dout, o, gr, gam, bet):
    mean = lambda xs: [jnp.mean(x, axis=-1, keepdims=True) for x in xs]
    cen = [x - m for x, m in zip(o, mean(o))]
    rstd = [lax.rsqrt(v + EPS) for v in mean([c * c for c in cen])]
    yh = [c * r for c, r in zip(cen, rstd)]
    y = [a * g + b for a, g, b in zip(yh, gam, bet)]
    sg = [jax.nn.sigmoid(g) for g in gr]
    dy = [d * (g * s) for d, g, s in zip(dout, gr, sg)]
    dgr = [d * a * (s * (1.0 + g * (1.0 - s))) for d, a, s, g in zip(dout, y, sg, gr)]
    dyh = [d * g for d, g in zip(dy, gam)]
    m1, m2 = mean(dyh), mean([a * b for a, b in zip(dyh, yh)])
    do = [r * (d - a - h * b) for r, d, a, h, b in zip(rstd, dyh, m1, yh, m2)]
    dg = [jnp.sum(d * h, axis=0, keepdims=True) for d, h in zip(dy, yh)]
    db = [jnp.sum(d, axis=0, keepdims=True) for d in dy]
    return do, dgr, dg, db


def _ret_bwd(do, states, z_rest, dz_rest):
    S = z_rest.shape[0]
    nsteps = S // CHUNK // RET_PER_STEP

    def body(do_ref, st_ref, q_ref, k_ref, v_ref, dz_in, dz_ref, gst):
        del dz_in

        @pl.when(pl.program_id(0) == 0)
        def _():
            gst[...] = jnp.zeros_like(gst)

        for c in reversed(range(RET_PER_STEP)):
            rc = slice(c * CHUNK, (c + 1) * CHUNK)
            for h in range(RET_HEADS):
                decay, xi, zeta, gch = _ret_tables(LOG_GAMMA[h])
                cq = slice(h * RET_QK, (h + 1) * RET_QK)
                cv = slice(h * RET_V, (h + 1) * RET_V)
                q = q_ref[rc, cq]
                kc32 = k_ref[rc, cq].astype(F32) * (RET_QK ** -0.5)
                kc = kc32.astype(BF16)
                v = v_ref[rc, cv]
                dob = do_ref[rc, cv]
                a = (_dot(q, kc, _NT) * decay).astype(BF16)
                da = (_dot(dob, v, _NT) * decay).astype(BF16)
                dcross = (dob.astype(F32) * xi).astype(BF16)
                g_next = gst[h]
                gb = g_next.astype(BF16)
                dq = _dot(da, kc) + _dot(dcross, st_ref[h, c], _NT)
                dkc = _dot(da, q, _TN)
                dkz = _dot(v, gb, _NT)
                dv = _dot(a, dob, _TN) + _dot((kc32 * zeta).astype(BF16), gb)
                gst[h] = g_next * gch + _dot(q, dcross, _TN)
                dz_ref[rc, R_Q + h * RET_QK:R_Q + (h + 1) * RET_QK] = dq.astype(BF16)
                dz_ref[rc, R_K + h * RET_QK:R_K + (h + 1) * RET_QK] = (
                    (dkc + dkz * zeta) * (RET_QK ** -0.5)).astype(BF16)
                dz_ref[rc, R_V + h * RET_V:R_V + (h + 1) * RET_V] = dv.astype(BF16)

    qk, vv, _, wide, st = _ret_specs(nsteps, True)
    return pl.pallas_call(
        body, name="ret_bwd", grid=(nsteps,),
        in_specs=[wide, st, qk(R_Q), qk(R_K), vv(R_V), _ANY],
        out_specs=pl.BlockSpec((CHUNK * RET_PER_STEP, R_G), lambda n: (nsteps - 1 - n, 0)),
        out_shape=_sds(dz_rest.shape, BF16),
        input_output_aliases={5: 0},
        scratch_shapes=[pltpu.VMEM((RET_HEADS, RET_QK, RET_V), F32)],
        compiler_params=_cparams(("arbitrary",)),
    )(do, states, z_rest, z_rest, z_rest, dz_rest)


def _att_probs(q, k, lmix, valid, bias):
    s = _dot(q, k, _NT) * (DH ** -0.5) + bias
    return jnp.where(valid, jnp.exp(jnp.where(valid, s, NEG) - lmix), 0.0)


def _att_bwd_group(z_att, dz_att, do_a, delta, lmix, gq3, gk3, slopes3, gi, d, nb, exchange=None):
    S = z_att.shape[0]
    sb = BLK * d * nb
    nsb = S // sb
    scale = DH ** -0.5
    aliased = dz_att is not None
    n_xi = len(exchange.ins) if exchange else 0
    n_xo = len(exchange.out_shapes) if exchange else 0

    def body(q_ref, k_ref, v_ref, kp_ref, vp_ref, qn_ref, do_ref, don_ref, dl_ref, dln_ref, lm_ref, lmn_ref,
             gq_ref, gk_ref, sl_ref, *rest):
        rest = rest[1:] if aliased else rest
        x_in, (dz_ref, dgq_ref, dgk_ref), rest = rest[:n_xi], rest[n_xi:n_xi + 3], rest[n_xi + 3:]
        x_out, stage, x_sems = rest[:n_xo], rest[n_xo], rest[n_xo + 1:]
        n = pl.program_id(1)
        step = pl.program_id(0) * nsb + n
        if exchange:
            start, _, finish = exchange.split(x_in, x_out, x_sems)
            pl.when(step == 0)(start)
        slope = sl_ref[...][:, :1]
        valid0, bias = _att_mask_bias(slope, d, n == 0)
        valid_in, _ = _att_mask_bias(slope, d, None)
        qi = lax.broadcasted_iota(jnp.int32, (BLK, BLK), 0)
        kj = lax.broadcasted_iota(jnp.int32, (BLK, BLK), 1)
        dist_n = BLK + qi - kj
        valid_n_in = dist_n <= BLK
        valid_n_last = valid_n_in & (n < nsb - 1)
        bias_n = -slope * (dist_n * d).astype(F32)
        gq, gk = gq_ref[...], gk_ref[...]
        dgq = jnp.zeros((1, DH), F32)
        dgk = jnp.zeros((1, DH), F32)
        memo = {}

        def get(kind, r, b):
            if (kind, r, b) not in memo:
                inner = 0 <= b < nb
                bb = b if inner else 0
                if kind == "q":
                    val = _head_norm(_rows(q_ref if inner else qn_ref, r, bb, d), gq)
                elif kind == "k":
                    val = _head_norm(_rows(k_ref if inner else kp_ref, r, bb, d), gk)
                elif kind == "v":
                    val = _rows(v_ref if inner else vp_ref, r, bb, d).astype(BF16)
                elif kind == "do":
                    val = _rows(do_ref if inner else don_ref, r, bb, d).astype(BF16)
                elif kind == "dl":
                    val = _rows(dl_ref if inner else dln_ref, r, bb, d)[:, :1]
                else:
                    val = _rows(lm_ref if inner else lmn_ref, r, bb, d)[:, :1]
                memo[(kind, r, b)] = val
            return memo[(kind, r, b)]

        units = [(r, b) for r in range(d) for b in range(nb)]
        for c0 in range(0, len(units), ATT_TOGETHER):
            us = units[c0:c0 + ATT_TOGETHER]
            k2 = [jnp.concatenate([get("k", r, b - 1)[0], get("k", r, b)[0]], axis=0) for r, b in us]
            v2 = [jnp.concatenate([get("v", r, b - 1), get("v", r, b)], axis=0) for r, b in us]
            p = [_att_probs(get("q", r, b)[0], k2[i], get("lm", r, b), valid0 if b == 0 else valid_in, bias)
                 for i, (r, b) in enumerate(us)]
            dp = [_dot(get("do", r, b), v2[i], _NT) for i, (r, b) in enumerate(us)]
            ds = [(p[i] * (dp[i] - get("dl", r, b)) * scale).astype(BF16) for i, (r, b) in enumerate(us)]
            dq = [_dot(ds[i], k2[i]) for i in range(len(us))]
            p_n = [_att_probs(get("q", r, b + 1)[0], get("k", r, b)[0], get("lm", r, b + 1),
                              valid_n_last if b == nb - 1 else valid_n_in, bias_n) for r, b in us]
            dp_n = [_dot(get("do", r, b + 1), get("v", r, b), _NT) for r, b in us]
            ds_n = [(p_n[i] * (dp_n[i] - get("dl", r, b + 1)) * scale).astype(BF16) for i, (r, b) in enumerate(us)]
            dk = [_dot(ds[i][:, BLK:], get("q", r, b)[0], _TN) + _dot(ds_n[i], get("q", r, b + 1)[0], _TN)
                  for i, (r, b) in enumerate(us)]
            dv = [_dot(p[i][:, BLK:].astype(BF16), get("do", r, b), _TN)
                  + _dot(p_n[i].astype(BF16), get("do", r, b + 1), _TN) for i, (r, b) in enumerate(us)]
            for i, (r, b) in enumerate(us):
                _, qh, qr = get("q", r, b)
                _, kh, kr = get("k", r, b)
                dxq, dg_q = _head_norm_bwd(dq[i], qh, qr, gq)
                dxk, dg_k = _head_norm_bwd(dk[i], kh, kr, gk)
                _put_rows(stage.at[0], r, b, d, dxq)
                _put_rows(stage.at[1], r, b, d, dxk)
                _put_rows(stage.at[2], r, b, d, dv[i])
                dgq, dgk = dgq + dg_q, dgk + dg_k
        for kind in range(3):
            dz_ref[:, kind * DH:(kind + 1) * DH] = stage[kind].astype(BF16)
        _acc_rows(dgq_ref, dgq, n == 0)
        _acc_rows(dgk_ref, dgk, n == 0)
        if exchange:
            pl.when(step == HPG * nsb - 1)(finish)

    cur, prev, nxt, slot, slot_next, head = _att_specs(gi, d, nb, S)
    gain = pl.BlockSpec((None, 1, DH), lambda j, n: (j, 0, 0))
    in_specs = [cur(0), cur(1), cur(2), prev(1), prev(2), nxt(0),
                slot, slot_next, slot, slot_next, slot, slot_next, head, head, head]
    args = [z_att] * 6 + [do_a, do_a, delta, delta, lmix, lmix, gq3, gk3, slopes3]
    if aliased:
        in_specs.append(_ANY)
        args.append(dz_att)
    res = pl.pallas_call(
        body, name=f"att_bwd_g{gi}", grid=(HPG, nsb),
        in_specs=in_specs + [_ANY] * n_xi,
        out_specs=[pl.BlockSpec((sb, 3 * DH), lambda j, n: (n, gi * HPG + j)), gain, gain] + [_ANY] * n_xo,
        out_shape=[_sds(z_att.shape, BF16), _sds((HPG, 1, DH), F32), _sds((HPG, 1, DH), F32)]
        + (exchange.out_shapes if exchange else []),
        input_output_aliases={len(args) - 1: 0} if aliased else {},
        scratch_shapes=[pltpu.VMEM((3, sb, DH), F32)] + (exchange.sems if exchange else []),
        compiler_params=_cparams(("arbitrary", "arbitrary") if exchange else ("parallel", "arbitrary")),
    )(*args, *(exchange.ins if exchange else []))
    return res[:3], res[3:]


def _step(x, target, norm1_g, q_norm_g, k_norm_g, gn_g, gn_b, norm2_g, shards, core):
    S = x.shape[0]
    tm = min(512, S)
    ts = min(256, S)
    tk = min(2048, S)
    tk2 = min(4096, S)
    later_shards = shards[1:]
    gq3 = q_norm_g.reshape(ATT_HEADS, 1, DH)
    gk3 = k_norm_g.reshape(ATT_HEADS, 1, DH)
    slopes3 = jnp.asarray(np.broadcast_to(ALIBI[:, None, None], (ATT_HEADS, 1, DH)).copy())

    xn, (g_in,) = _rmsnorm_fwd(x, norm1_g, tm, _gather_exchange(shards[:1]))
    w_in = _cols(g_in)
    w_att = w_in[:, :3 * ATT_W].reshape(D_MODEL, 3, ATT_HEADS, DH).transpose(0, 2, 1, 3).reshape(D_MODEL, 3 * ATT_W)
    w_rest = w_in[:, 3 * ATT_W:]
    w_att_t, w_rest_t = w_att.T, w_rest.T
    z_att = _mm("in_proj_att", xn, w_att, "nn", tm, 3 * ATT_W, D_MODEL,
                [(_sds((S, 3 * ATT_W), F32), _tile_ij(tm, 3 * ATT_W))], _epi_store(F32))[0]
    (z_rest,), gathered = _mm("in_proj_rest", xn, w_rest, "nn", tm, REST_W, D_MODEL,
                              [(_sds((S, REST_W), BF16), _tile_ij(tm, REST_W))], _epi_store(BF16),
                              exchange=_gather_exchange(later_shards))
    w_pa, w_pb, w_out, w_up, w_down = [f(g) for f, g in zip((_cols, _rows_of, _rows_of, _cols, _rows_of), gathered)]
    w_up_t, w_down_t, w_out_t, w_pa_t, w_pb_t = w_up.T, w_down.T, w_out.T, w_pa.T, w_pb.T
    done = []
    for gi, ((_, d), nb) in enumerate(zip(ATT_GROUPS, ATT_BLOCKS_PER_STEP)):
        last = gi == len(ATT_GROUPS) - 1
        done.append(_att_fwd_group(z_att, gq3, gk3, slopes3, gi, d, nb, others=tuple(done) if last else ()))
    o_a, lmix = done[-1]
    o_r, o_pre, states = _ret_fwd(z_rest, gn_g, gn_b)
    x1, y, pa, pb, xn2 = _merge_fwd(o_a, o_r, z_rest, x, w_pa, w_pb, w_out, norm2_g, tm)

    def epi_up(acc, ex, out):
        r = jnp.maximum(acc, 0.0)
        out[0][...] = (r * r).astype(BF16)
        out[1][...] = r.astype(BF16)

    h, relu_u = _mm("mlp_up", xn2, w_up, "nn", tm, D_FF, D_MODEL,
                    [(_sds((S, D_FF), BF16), _tile_ij(tm, D_FF)), (_sds((S, D_FF), BF16), _tile_ij(tm, D_FF))], epi_up)

    def epi_down(acc, ex, out):
        diff = ex[0][...] + acc - ex[1][...]
        out[0][...] = diff * (1.0 / D_MODEL)
        out[1][...] = jnp.broadcast_to(jnp.sum(diff * diff) * (1.0 / (8 * LANES)), (8, LANES))

    row_tile = _tile_ij(tm, D_MODEL)
    dx2, loss_parts = _mm(
        "mlp_down_loss", h, w_down, "nn", tm, D_MODEL, D_FF,
        [(_sds((S, D_MODEL), F32), row_tile),
         (_sds((S // tm * 8, LANES), F32), pl.BlockSpec((8, LANES), lambda i, j, k: (i, 0)))],
        epi_down, extras=[(x1, row_tile), (target, row_tile)])
    loss_local = jnp.sum(loss_parts) * (0.5 / D_MODEL)

    def epi_du(acc, ex, out):
        out[0][...] = (acc * (2.0 * ex[0][...].astype(F32))).astype(BF16)

    du = _mm("mlp_down_bwd", dx2, w_down_t, "nn", tm, D_FF, D_MODEL,
             [(_sds((S, D_FF), BF16), _tile_ij(tm, D_FF))], epi_du, extras=[(relu_u, _tile_ij(tm, D_FF))])[0]
    gw_down = _mm("gw_down", h, dx2, "tn", 1024, D_MODEL, tk,
                  [(_sds((D_FF, D_MODEL), BF16), _tile_ij(1024, D_MODEL))], _epi_store(BF16))[0]
    gw_up = _mm("gw_up", xn2, du, "tn", D_MODEL, 512, tk2,
                [(_sds((N_DEV, D_MODEL, 512), BF16), pl.BlockSpec((None, D_MODEL, 512), lambda i, j, k: (j, 0, 0)))],
                _epi_store(BF16))[0]

    vec = pl.BlockSpec((1, D_MODEL), lambda i, j, k: (0, 0))
    seq_sem = ("arbitrary", "arbitrary", "arbitrary")

    def epi_norm2(acc, ex, out):
        dx, dg = _rms_bwd(acc, ex[0][...], ex[2][...])
        out[0][...] = ex[1][...] + dx
        _acc_rows(out[1], dg, pl.program_id(0) == 0)

    dx1, g_norm2 = _mm(
        "mlp_up_bwd", du, w_up_t, "nn", tm, D_MODEL, D_FF,
        [(_sds((S, D_MODEL), F32), row_tile), (_sds((1, D_MODEL), F32), vec)],
        epi_norm2, extras=[(x1, row_tile), (dx2, row_tile), (norm2_g, vec)], sem=seq_sem)

    def epi_dy(acc, ex, out):
        sa = jax.nn.sigmoid(ex[0][...].astype(F32))
        sb = jax.nn.sigmoid(ex[1][...].astype(F32))
        out[0][...] = (acc * sa).astype(BF16)
        out[1][...] = (acc * sb).astype(BF16)
        out[2][:, :D_MODEL] = (acc * ex[2][...].astype(F32) * (sa * (1.0 - sa))).astype(BF16)
        out[2][:, D_MODEL:] = (acc * ex[3][...].astype(F32) * (sb * (1.0 - sb))).astype(BF16)

    ga_spec = pl.BlockSpec((tm, D_MODEL), lambda i, j, k: (i, R_GA // D_MODEL))
    gb_spec = pl.BlockSpec((tm, D_MODEL), lambda i, j, k: (i, R_GB // D_MODEL))
    gates_spec = pl.BlockSpec((tm, 2 * D_MODEL), lambda i, j, k: (i, R_GA // (2 * D_MODEL)))
    dpa, dpb, dz_rest = _mm(
        "out_proj_bwd", dx1, w_out_t, "nn", tm, D_MODEL, D_MODEL,
        [(_sds((S, D_MODEL), BF16), row_tile), (_sds((S, D_MODEL), BF16), row_tile), (_sds((S, REST_W), BF16), gates_spec)],
        epi_dy, extras=[(z_rest, ga_spec), (z_rest, gb_spec), (pa, row_tile), (pb, row_tile)])
    gw_out = _mm("gw_out", y, dx1, "tn", D_MODEL, D_MODEL, tk,
                 [(_sds((D_MODEL, D_MODEL), BF16), _tile_ij(D_MODEL, D_MODEL))], _epi_store(BF16))[0]
    gw_pa = _mm("gw_proj_a", o_a, dpa, "tn", ATT_OUT_W, D_MODEL, tk,
                [(_sds((ATT_OUT_W, D_MODEL), BF16), _tile_ij(ATT_OUT_W, D_MODEL))], _epi_store(BF16))[0]
    gw_pb = _mm("gw_proj_b", o_r, dpb, "tn", 1024, D_MODEL, tk2,
                [(_sds((RET_V_W, D_MODEL), BF16), _tile_ij(1024, D_MODEL))], _epi_store(BF16))[0]

    def epi_doa(acc, ex, out):
        out[0][...] = acc
        prod = acc * ex[0][...]
        out[1][...] = jnp.concatenate(
            [jnp.broadcast_to(jnp.sum(prod[:, j * DH:(j + 1) * DH], axis=-1, keepdims=True), (prod.shape[0], DH))
             for j in range(HPG)], axis=1)

    slot_tile = _tile_ij(tm, ATT_OUT_W)
    do_a, delta = _mm("proj_a_bwd", dpa, w_pa_t, "nn", tm, ATT_OUT_W, D_MODEL,
                      [(_sds((S, ATT_OUT_W), F32), slot_tile), (_sds((S, ATT_OUT_W), F32), slot_tile)],
                      epi_doa, extras=[(o_a, slot_tile)])
    owed = [_chip_core(_by_owner_cols(gw_pa)), _chip_core(gw_pb.reshape(N_DEV, -1, D_MODEL)),
            _chip_core(gw_out.reshape(N_DEV, -1, D_MODEL)), _chip_core(gw_up), _chip_core(gw_down.reshape(N_DEV, -1, D_MODEL))]
    names = ("w_proj_a", "w_proj_b", "w_out", "w_up", "w_down")

    def epi_dor(acc, ex, out):
        cvs = [slice(h * RET_V, (h + 1) * RET_V) for h in range(RET_HEADS)]
        do, dgr, dg, db = _norm_gate_bwd([acc[:, cv] for cv in cvs], [ex[0][:, cv] for cv in cvs],
                                         [ex[1][:, cv].astype(F32) for cv in cvs],
                                         [ex[2][:, cv] for cv in cvs], [ex[3][:, cv] for cv in cvs])
        for h, cv in enumerate(cvs):
            out[0][:, cv] = do[h].astype(BF16)
            out[1][:, cv] = dgr[h].astype(BF16)
        first = pl.program_id(0) == 0
        _acc_rows(out[2], jnp.concatenate(dg, axis=1), first)
        _acc_rows(out[3], jnp.concatenate(db, axis=1), first)

    wide_tile = _tile_ij(ts, RET_V_W)
    gate_tile = pl.BlockSpec((ts, RET_V_W), lambda i, j, k: (i, R_G // RET_V_W))
    wide_vec = pl.BlockSpec((1, RET_V_W), lambda i, j, k: (0, 0))
    (do_ret, dz_rest, g_gn_g, g_gn_b), got = _mm(
        "proj_b_bwd", dpb, w_pb_t, "nn", ts, RET_V_W, D_MODEL,
        [(_sds((S, RET_V_W), BF16), wide_tile), (_sds((S, REST_W), BF16), gate_tile),
         (_sds((1, RET_V_W), F32), wide_vec), (_sds((1, RET_V_W), F32), wide_vec)],
        epi_dor, extras=[(o_pre, wide_tile), (z_rest, gate_tile), (gn_g, wide_vec), (gn_b, wide_vec)],
        exchange=_pair_exchange(owed), in_place=(dz_rest, 1))
    chip_sums = [_pair_sum(f"pair_sum_{n}", g, r, core, min(256, g.shape[2])) for n, g, r in zip(names, owed, got)]

    dz_rest = _ret_bwd(do_ret, states, z_rest, dz_rest)
    dz_att, gq_parts, gk_parts, parts_late = None, [], [], None
    for gi, ((_, d), nb) in enumerate(zip(ATT_GROUPS, ATT_BLOCKS_PER_STEP)):
        last = gi == len(ATT_GROUPS) - 1
        (dz_att, gq_p, gk_p), parts = _att_bwd_group(z_att, dz_att, do_a, delta, lmix, gq3, gk3, slopes3, gi, d, nb,
                                                     exchange=_chip_exchange(chip_sums) if last else None)
        parts_late = parts if last else parts_late
        gq_parts.append(gq_p)
        gk_parts.append(gk_p)
    g_qn = jnp.concatenate(gq_parts, axis=0).reshape(1, ATT_HEADS, DH)
    g_kn = jnp.concatenate(gk_parts, axis=0).reshape(1, ATT_HEADS, DH)

    gw_att = _mm("gw_in_att", xn, dz_att, "tn", D_MODEL, ATT_W, tk,
                 [(_sds((D_MODEL, 3 * ATT_W), BF16), _tile_ij(D_MODEL, ATT_W))], _epi_store(BF16))[0]
    gw_rest = _mm("gw_in_rest", xn, dz_rest, "tn", D_MODEL, 1024, tk2,
                  [(_sds((D_MODEL, REST_W), BF16), _tile_ij(D_MODEL, 1024))], _epi_store(BF16))[0]
    gw_att = gw_att.reshape(D_MODEL, ATT_HEADS, 3, DH).transpose(0, 2, 1, 3).reshape(D_MODEL, 3 * ATT_W)
    gw_in = jnp.concatenate([gw_att, gw_rest], axis=1)
    owed_in = _chip_core(_by_owner_cols(gw_in))
    (dxn_att,), (got_in,) = _mm("in_proj_att_bwd", dz_att, w_att_t, "nn", tm, D_MODEL, 3 * ATT_W,
                                [(_sds((S, D_MODEL), F32), row_tile)], _epi_store(F32), exchange=_pair_exchange([owed_in]))
    chip_sum_in = _pair_sum("pair_sum_w_in", owed_in, got_in, core, min(256, owed_in.shape[2]))

    def epi_norm1(acc, ex, out):
        dx, dg = _rms_bwd(acc + ex[0][...], ex[1][...], ex[3][...])
        out[0][...] = ex[2][...] + dx
        _acc_rows(out[1], dg, pl.program_id(0) == 0)

    short_tile = _tile_ij(ts, D_MODEL)
    (grad_x, g_norm1), parts_in = _mm(
        "in_proj_rest_bwd", dz_rest, w_rest_t, "nn", ts, D_MODEL, REST_W,
        [(_sds((S, D_MODEL), F32), short_tile), (_sds((1, D_MODEL), F32), vec)],
        epi_norm1, extras=[(dxn_att, short_tile), (x, short_tile), (dx1, short_tile), (norm1_g, vec)],
        exchange=_chip_exchange([chip_sum_in]))

    small = (g_norm1, g_qn, g_kn, g_gn_g, g_gn_b, g_norm2)
    return loss_local, grad_x, list(parts_in) + list(parts_late), small


def _position():
    return lax.axis_index("x"), lax.axis_index("y"), lax.axis_index("c")


def _other_chips(x, y):
    return [(1 - x, y), (x, 1 - y), (1 - x, 1 - y)]


_ANY = pl.BlockSpec(memory_space=pl.ANY)


def _gather_exchange(shards):
    nw = len(shards)

    def plan(x_refs, out_refs, sems):
        send_sems, recv_sems, local_sems = sems
        x, y, c = _position()
        me, sibling = (x, y, c), (x, y, 1 - c)
        chips = _other_chips(x, y)

        def copy(w, k, block, to, own=False):
            px, py, pc = block
            rows = out_refs[w].at[4 * px + 2 * py + pc]
            return pltpu.make_async_remote_copy(
                src_ref=x_refs[w] if own else rows, dst_ref=rows,
                send_sem=send_sems.at[7 * w + k], recv_sem=recv_sems.at[7 * w + k], device_id=to, device_id_type=MESH)

        def mine(w):
            return pltpu.make_async_copy(x_refs[w], out_refs[w].at[4 * x + 2 * y + c], local_sems.at[w])

        def own_sends(w):
            return [copy(w, 0, me, sibling, own=True)] + [copy(w, 1 + j, me, (*chip, c), own=True)
                                                          for j, chip in enumerate(chips)]

        def start():
            for w in range(nw):
                mine(w).start()
                for cp in own_sends(w):
                    cp.start()

        def relay():
            for j, chip in enumerate(chips):
                for w in range(nw):
                    copy(w, 1 + j, (*chip, c), me).wait_recv()
                    copy(w, 4 + j, (*chip, c), sibling).start()

        def finish():
            for w in range(nw):
                copy(w, 0, sibling, me).wait_recv()
                for j, chip in enumerate(chips):
                    copy(w, 4 + j, (*chip, 1 - c), me).wait_recv()
            for w in range(nw):
                for cp in own_sends(w):
                    cp.wait_send()
                for j, chip in enumerate(chips):
                    copy(w, 4 + j, (*chip, c), sibling).wait_send()
                mine(w).wait()

        return start, relay, finish

    return _Exchange(shards, [_sds((N_DEV,) + s.shape, s.dtype) for s in shards],
                     [pltpu.SemaphoreType.DMA((7 * nw,)), pltpu.SemaphoreType.DMA((7 * nw,)),
                      pltpu.SemaphoreType.DMA((nw,))], plan)


def _run_exchange(name, exchange):
    n_in, n_out = len(exchange.ins), len(exchange.out_shapes)

    def body(*refs):
        start, relay, finish = exchange.split(refs[:n_in], refs[n_in:n_in + n_out], refs[n_in + n_out:])
        start()
        if relay is not None:
            relay()
        finish()

    return pl.pallas_call(
        body, name=name, out_shape=exchange.out_shapes,
        in_specs=[_ANY] * n_in, out_specs=[_ANY] * n_out, scratch_shapes=exchange.sems,
    )(*exchange.ins)


def _pair_exchange(grads):
    ng = len(grads)

    def plan(g_refs, out_refs, sems):
        send_sems, recv_sems = sems
        x, y, c = _position()

        def copies():
            return [pltpu.make_async_remote_copy(
                src_ref=g_refs[w].at[:, 1 - c], dst_ref=out_refs[w], send_sem=send_sems.at[w],
                recv_sem=recv_sems.at[w], device_id=(x, y, 1 - c), device_id_type=MESH) for w in range(ng)]

        def start():
            for cp in copies():
                cp.start()

        def finish():
            for cp in copies():
                cp.wait()

        return start, None, finish

    return _Exchange(grads, [_sds((N_CHIP,) + g.shape[2:], g.dtype) for g in grads],
                     [pltpu.SemaphoreType.DMA((ng,)), pltpu.SemaphoreType.DMA((ng,))], plan)


def _small_all_gather(small):
    def plan(in_refs, out_refs, sems):
        (s_ref,), (s_out,) = in_refs, out_refs
        send_sems, recv_sems, local_sem = sems
        x, y, c = _position()
        me_id = 4 * x + 2 * y + c
        flips = [(a, b, e) for a in (0, 1) for b in (0, 1) for e in (0, 1)][1:]
        peers = [(x ^ a, y ^ b, c ^ e) for a, b, e in flips]

        def start():
            pltpu.make_async_copy(s_ref, s_out.at[me_id], local_sem).start()
            for k, p in enumerate(peers):
                pltpu.make_async_remote_copy(
                    src_ref=s_ref, dst_ref=s_out.at[me_id], send_sem=send_sems.at[k], recv_sem=recv_sems.at[k],
                    device_id=p, device_id_type=MESH).start()

        def finish():
            for k, (px, py, pc) in enumerate(peers):
                pltpu.make_async_remote_copy(
                    src_ref=s_ref, dst_ref=s_out.at[4 * px + 2 * py + pc], send_sem=send_sems.at[k],
                    recv_sem=recv_sems.at[k], device_id=(px, py, pc), device_id_type=MESH).wait()
            pltpu.make_async_copy(s_ref, s_out.at[me_id], local_sem).wait()

        return start, None, finish

    return _run_exchange("small_grad_all_gather", _Exchange(
        [small], [_sds((N_DEV,) + small.shape, small.dtype)],
        [pltpu.SemaphoreType.DMA((7,)), pltpu.SemaphoreType.DMA((7,)), pltpu.SemaphoreType.DMA], plan))[0]


def _pair_sum(name, g, got, core, tr):
    n_chip, _, R, C = g.shape

    def body(c_ref, a_ref, b_ref, o_ref):
        del c_ref
        o_ref[...] = (a_ref[...].astype(F32) + b_ref[...].astype(F32)).astype(o_ref.dtype)

    return pl.pallas_call(
        body, name=name,
        grid_spec=pltpu.PrefetchScalarGridSpec(
            num_scalar_prefetch=1, grid=(n_chip, R // tr),
            in_specs=[pl.BlockSpec((None, None, tr, C), lambda ch, i, c_ref: (ch, c_ref[0], i, 0)),
                      pl.BlockSpec((None, tr, C), lambda ch, i, c_ref: (ch, i, 0))],
            out_specs=pl.BlockSpec((None, tr, C), lambda ch, i, c_ref: (ch, i, 0))),
        out_shape=_sds(got.shape, got.dtype),
        compiler_params=_cparams(("parallel", "parallel")),
    )(core, g, got)


def _chip_exchange(parts):
    ng = len(parts)

    def plan(p_refs, out_refs, sems):
        send_sems, recv_sems, local_sems = sems
        x, y, c = _position()
        my_chip = 2 * x + y

        def copies():
            local = [pltpu.make_async_copy(p_refs[w].at[my_chip], out_refs[w].at[my_chip], local_sems.at[w])
                     for w in range(ng)]
            remote = [pltpu.make_async_remote_copy(
                src_ref=p_refs[w].at[2 * cx + cy], dst_ref=out_refs[w].at[my_chip],
                send_sem=send_sems.at[3 * w + k], recv_sem=recv_sems.at[3 * w + k],
                device_id=(cx, cy, c), device_id_type=MESH)
                for w in range(ng) for k, (cx, cy) in enumerate(_other_chips(x, y))]
            return local + remote

        def start():
            for cp in copies():
                cp.start()

        def finish():
            for cp in copies():
                cp.wait()

        return start, None, finish

    return _Exchange(parts, [_sds(p.shape, p.dtype) for p in parts],
                     [pltpu.SemaphoreType.DMA((3 * ng,)), pltpu.SemaphoreType.DMA((3 * ng,)),
                      pltpu.SemaphoreType.DMA((ng,))], plan)


def _adamw(name, parts, w, m, v, tr):
    n_parts = parts.shape[0]
    R, C = w.shape

    def body(p_ref, w_ref, m_ref, v_ref, g_ref, d_ref, mo_ref, vo_ref):
        g = p_ref[0].astype(F32)
        for i in range(1, n_parts):
            g = g + p_ref[i].astype(F32)
        m_new = ADAM_B1 * m_ref[...] + (1.0 - ADAM_B1) * g
        v_new = ADAM_B2 * v_ref[...] + (1.0 - ADAM_B2) * (g * g)
        m_hat = m_new / (1.0 - ADAM_B1 ** ADAM_STEP)
        v_hat = v_new / (1.0 - ADAM_B2 ** ADAM_STEP)
        g_ref[...] = g
        d_ref[...] = -ADAM_LR * (m_hat / (jnp.sqrt(v_hat) + ADAM_EPS) + ADAM_WD * w_ref[...])
        mo_ref[...] = m_new
        vo_ref[...] = v_new

    tile = pl.BlockSpec((tr, C), lambda i: (i, 0))
    return pl.pallas_call(
        body, name=name, grid=(R // tr,),
        in_specs=[pl.BlockSpec((n_parts, tr, C), lambda i: (0, i, 0)), tile, tile, tile],
        out_specs=[tile] * 4,
        out_shape=[_sds((R, C), F32)] * 4,
        compiler_params=_cparams(("parallel",)),
    )(parts, w, m, v)


def _flat_small(arrs):
    return jnp.concatenate([a.reshape(-1) for a in arrs]).reshape(SMALL_ROWS, LANES)


def _cols(g):
    return g.transpose(1, 0, 2).reshape(g.shape[1], -1)


def _rows_of(g):
    return g.reshape(-1, g.shape[2])


def _by_owner_cols(g):
    rows, cols = g.shape
    return g.reshape(rows, N_DEV, cols // N_DEV).transpose(1, 0, 2)


def _chip_core(g):
    return g.reshape((N_CHIP, 2) + g.shape[1:])


def kernel(x, norm1_g, w_in, q_norm_g, k_norm_g, ret_gn_g, ret_gn_b, w_proj_a, w_proj_b, w_out, norm2_g, w_up, w_down, loss_target, m_norm1_g, m_w_in, m_q_norm_g, m_k_norm_g, m_ret_gn_g, m_ret_gn_b, m_w_proj_a, m_w_proj_b, m_w_out, m_norm2_g, m_w_up, m_w_down, v_norm1_g, v_w_in, v_q_norm_g, v_k_norm_g, v_ret_gn_g, v_ret_gn_b, v_w_proj_a, v_w_proj_b, v_w_out, v_norm2_g, v_w_up, v_w_down):
    big_w = (w_in, w_proj_a, w_proj_b, w_out, w_up, w_down)
    big_m = (m_w_in, m_w_proj_a, m_w_proj_b, m_w_out, m_w_up, m_w_down)
    big_v = (v_w_in, v_w_proj_a, v_w_proj_b, v_w_out, v_w_up, v_w_down)
    small_w = (norm1_g, q_norm_g, k_norm_g, ret_gn_g, ret_gn_b, norm2_g)
    small_m = (m_norm1_g, m_q_norm_g, m_k_norm_g, m_ret_gn_g, m_ret_gn_b, m_norm2_g)
    small_v = (v_norm1_g, v_q_norm_g, v_k_norm_g, v_ret_gn_g, v_ret_gn_b, v_norm2_g)

    shards = [w[0].astype(BF16) for w in big_w]
    core = lax.axis_index("c").astype(jnp.int32).reshape(1)
    loss_local, grad_x, parts, small_g = _step(
        x[0], loss_target[0], norm1_g, q_norm_g[0], k_norm_g[0], ret_gn_g, ret_gn_b, norm2_g, shards, core)
    small_all = _small_all_gather(_flat_small(small_g))
    names = ("w_in", "w_proj_a", "w_proj_b", "w_out", "w_up", "w_down")

    res = {}
    for n, p, w, m, v in zip(names, parts, big_w, big_m, big_v):
        outs = _adamw(f"adamw_{n}", p, w[0], m[0], v[0], min(128, w.shape[1]))
        res[n] = [o[None] for o in outs]
    s_outs = _adamw("adamw_small", small_all, _flat_small(small_w), _flat_small(small_m), _flat_small(small_v), SMALL_ROWS)
    small_names = ("norm1_g", "q_norm_g", "k_norm_g", "ret_gn_g", "ret_gn_b", "norm2_g")
    for n in small_names:
        res[n] = []
    for o in s_outs:
        flat, off = o.reshape(-1), 0
        for n, w in zip(small_names, small_w):
            res[n].append(flat[off:off + w.size].reshape(w.shape))
            off += w.size

    order = ("norm1_g", "w_in", "q_norm_g", "k_norm_g", "ret_gn_g", "ret_gn_b", "w_proj_a", "w_proj_b", "w_out",
             "norm2_g", "w_up", "w_down")
    loss = lax.psum(loss_local, MESH_AXES)
    return (loss, grad_x[None], *[res[n][0] for n in order], *[res[n][1] for n in order],
            *[res[n][2] for n in order], *[res[n][3] for n in order])
```

```python
import math

import numpy as np
import jax
import jax.numpy as jnp
from jax import lax
from jax.experimental import pallas as pl
from jax.experimental.pallas import tpu as pltpu

F32 = jnp.float32
BF16 = jnp.bfloat16

D_MODEL = 1024
ATT_GROUPS = ((128, 1), (512, 4), (2048, 16))
ATT_BLOCKS_PER_STEP = (16, 4, 1)
ATT_TOGETHER = 4
HPG = 4
ATT_HEADS = 12
DH = 128
BLK = 128
ATT_W = ATT_HEADS * DH
ATT_OUT_W = HPG * DH
RET_HEADS = 4
RET_QK = 256
RET_V = 512
RET_QK_W = RET_HEADS * RET_QK
RET_V_W = RET_HEADS * RET_V
CHUNK = 128
RET_PER_STEP = 4
D_FF = 4096
IN_W = 12800
REST_W = IN_W - 3 * ATT_W
EPS = 1e-6
ADAM_LR, ADAM_B1, ADAM_B2, ADAM_EPS, ADAM_WD, ADAM_STEP = 0.001, 0.9, 0.999, 1e-08, 0.01, 10
N_DEV = 8
N_CHIP = 4
MESH_AXES = ("x", "y", "c")
MESH = pl.DeviceIdType.MESH
VMEM_LIMIT = 56 * 1024 * 1024
LANES = 128
NEG = -1e30

_NN = (((1,), (0,)), ((), ()))
_NT = (((1,), (1,)), ((), ()))
_TN = (((0,), (0,)), ((), ()))

R_Q, R_K, R_V, R_G, R_GA, R_GB = 0, 1024, 2048, 4096, 6144, 7168

LOG_GAMMA = [float(v) for v in np.log(1.0 - 2.0 ** (-5.0 - np.arange(RET_HEADS, dtype=np.float32))).astype(np.float32)]
ALIBI = np.asarray(2.0 ** (-8.0 * np.arange(1, ATT_HEADS + 1, dtype=np.float32) / ATT_HEADS), np.float32)

SMALL_ROWS = (1024 + 1536 + 1536 + 2048 + 2048 + 1024) // LANES


def _dot(a, b, dims=_NN):
    return lax.dot_general(a, b, dims, preferred_element_type=F32)


def _cparams(sem):
    return pltpu.CompilerParams(dimension_semantics=sem, vmem_limit_bytes=VMEM_LIMIT)


def _sds(shape, dtype):
    return jax.ShapeDtypeStruct(shape, dtype)


class _Exchange:
    def __init__(self, ins, out_shapes, sems, plan):
        self.ins, self.out_shapes, self.sems, self.plan = list(ins), list(out_shapes), list(sems), plan

    def split(self, in_refs, out_refs, sem_refs):
        return self.plan(in_refs, out_refs, sem_refs)


def _mm(name, a, b, mode, tm, tn, tk, outs, epi, extras=(), b_pro=None,
        sem=("parallel", "parallel", "arbitrary"), exchange=None, in_place=None):
    if mode == "nn":
        (M, K), (_, N) = a.shape, b.shape
        a_spec = pl.BlockSpec((tm, tk), lambda i, j, k: (i, k))
        dims = _NN
    else:
        (K, M), (_, N) = a.shape, b.shape
        a_spec = pl.BlockSpec((tk, tm), lambda i, j, k: (k, i))
        dims = _TN
    assert M % tm == 0 and N % tn == 0 and K % tk == 0, (name, M, N, K, tm, tn, tk)
    nk = K // tk
    whole_b = dict(pipeline_mode=pl.Buffered(1)) if (nk == 1 and N == tn) else {}
    b_spec = pl.BlockSpec((tk, tn), lambda i, j, k: (k, j), **whole_b)
    n_ex, n_out = len(extras), len(outs)
    grid = (M // tm, N // tn, nk)
    n_xi = len(exchange.ins) if exchange else 0
    n_xo = len(exchange.out_shapes) if exchange else 0
    n_acc = 1 if nk > 1 else 0

    n_ip = 1 if in_place else 0

    def body(a_ref, b_ref, *rest):
        ex, rest = rest[:n_ex], rest[n_ex:]
        x_in, rest = rest[:n_xi], rest[n_xi + n_ip:]
        out, rest = rest[:n_out], rest[n_out:]
        x_out, rest = rest[:n_xo], rest[n_xo:]
        step =(pl.program_id(0) * grid[1] + pl.program_id(1)) * grid[2] + pl.program_id(2)
        n_steps = grid[0] * grid[1] * grid[2]
        if exchange:
            start, relay, finish = exchange.split(x_in, x_out, rest[n_acc:])
            pl.when(step == 0)(start)
        bv = b_ref[...]
        if b_pro is not None:
            bv = b_pro(bv)
        part = _dot(a_ref[...].astype(BF16), bv.astype(BF16), dims)
        if nk == 1:
            epi(part, ex, out)
        else:
            acc_ref = rest[0]
            k = pl.program_id(2)

            @pl.when(k == 0)
            def _():
                acc_ref[...] = part

            @pl.when(k > 0)
            def _():
                acc_ref[...] += part

            @pl.when(k == nk - 1)
            def _():
                epi(acc_ref[...], ex, out)
        if exchange:
            if relay is not None:
                pl.when(step == (7 * n_steps) // 8)(relay)
            pl.when(step == n_steps - 1)(finish)

    res = pl.pallas_call(
        body,
        name=name,
        grid=grid,
        in_specs=[a_spec, b_spec] + [s for _, s in extras] + [_ANY] * (n_xi + n_ip),
        out_specs=[s for _, s in outs] + [_ANY] * n_xo,
        out_shape=[o for o, _ in outs] + (exchange.out_shapes if exchange else []),
        scratch_shapes=([pltpu.VMEM((tm, tn), F32)] if nk > 1 else []) + (exchange.sems if exchange else []),
        input_output_aliases={2 + n_ex + n_xi: in_place[1]} if in_place else {},
        compiler_params=_cparams(("arbitrary",) * 3 if exchange else sem),
    )(a, b, *[e for e, _ in extras], *(exchange.ins if exchange else []), *([in_place[0]] if in_place else []))
    return (res[:n_out], res[n_out:]) if exchange else res


def _tile_ij(tm, tn):
    return pl.BlockSpec((tm, tn), lambda i, j, k: (i, j))


def _epi_store(dtype):
    def epi(acc, ex, out):
        out[0][...] = acc.astype(dtype)
    return epi


def _rms_rows(x):
    return lax.rsqrt(jnp.mean(x * x, axis=-1, keepdims=True) + EPS)


def _acc_rows(ref, part, first):
    @pl.when(first)
    def _():
        ref[...] = part

    @pl.when(jnp.logical_not(first))
    def _():
        ref[...] += part


def _rmsnorm_fwd(x, g, tm, exchange):
    S, Dm = x.shape
    n_steps = S // tm
    n_xi, n_xo = len(exchange.ins), len(exchange.out_shapes)

    def body(x_ref, g_ref, *rest):
        x_in, o_ref, x_out, sems = rest[:n_xi], rest[n_xi], rest[n_xi + 1:n_xi + 1 + n_xo], rest[n_xi + 1 + n_xo:]
        start, relay, finish = exchange.split(x_in, x_out, sems)
        step = pl.program_id(0)
        pl.when(step == 0)(start)
        xv = x_ref[...]
        o_ref[...] = (xv * _rms_rows(xv) * g_ref[...]).astype(BF16)

        @pl.when(step == n_steps - 1)
        def _():
            relay()
            finish()

    res = pl.pallas_call(
        body, name="rmsnorm1_fwd", grid=(n_steps,),
        in_specs=[pl.BlockSpec((tm, Dm), lambda i: (i, 0)), pl.BlockSpec((1, Dm), lambda i: (0, 0))] + [_ANY] * n_xi,
        out_specs=[pl.BlockSpec((tm, Dm), lambda i: (i, 0))] + [_ANY] * n_xo,
        out_shape=[_sds((S, Dm), BF16)] + exchange.out_shapes,
        scratch_shapes=exchange.sems,
        compiler_params=_cparams(("arbitrary",)),
    )(x, g, *exchange.ins)
    return res[0], res[1:]


def _rows(ref, r, b, d):
    if d == 1:
        return ref[b * BLK:(b + 1) * BLK, :]
    return ref[pl.ds(b * BLK * d + r, BLK, stride=d), :]


def _put_rows(ref, r, b, d, val):
    if d == 1:
        ref[b * BLK:(b + 1) * BLK, :] = val
    else:
        ref[pl.ds(b * BLK * d + r, BLK, stride=d), :] = val


def _head_norm(x, g):
    r = _rms_rows(x)
    xh = x * r
    return (xh * g).astype(BF16), xh, r


def _head_norm_bwd(dyn, xh, r, g):
    dxh = dyn * g
    dx = r * (dxh - xh * jnp.mean(dxh * xh, axis=-1, keepdims=True))
    return dx, jnp.sum(dyn * xh, axis=0, keepdims=True)


def _att_mask_bias(slope, d, first):
    qi = lax.broadcasted_iota(jnp.int32, (BLK, 2 * BLK), 0)
    kj = lax.broadcasted_iota(jnp.int32, (BLK, 2 * BLK), 1)
    dist = BLK + qi - kj
    valid = (dist >= 0) & (dist <= BLK)
    if first is not None:
        valid = valid & (jnp.logical_not(first) | (kj >= BLK))
    bias = -slope * (dist * d).astype(F32)
    return valid, bias


def _att_specs(gi, d, nb, S):
    span = BLK * d
    sb = span * nb
    nspan = S // span
    before = lambda n: jnp.maximum(n * nb - 1, 0)
    after = lambda n: jnp.minimum((n + 1) * nb, nspan - 1)
    zcol = lambda j, kind: 3 * (gi * HPG + j) + kind
    cur = lambda kind: pl.BlockSpec((sb, DH), lambda j, n: (n, zcol(j, kind)))
    prev = lambda kind: pl.BlockSpec((span, DH), lambda j, n: (before(n), zcol(j, kind)))
    nxt = lambda kind: pl.BlockSpec((span, DH), lambda j, n: (after(n), zcol(j, kind)))
    slot = pl.BlockSpec((sb, DH), lambda j, n: (n, j))
    slot_next = pl.BlockSpec((span, DH), lambda j, n: (after(n), j))
    head = pl.BlockSpec((None, 1, DH), lambda j, n: (gi * HPG + j, 0, 0))
    return cur, prev, nxt, slot, slot_next, head


def _att_fwd_group(z_att, gq3, gk3, slopes3, gi, d, nb, others=()):
    S = z_att.shape[0]
    nsb = S // (BLK * d * nb)
    scale = DH ** -0.5
    n_other = len(others)

    def body(q_ref, k_ref, v_ref, kp_ref, vp_ref, gq_ref, gk_ref, sl_ref, *rest):
        other_refs, (o_ref, l_ref) = rest[:2 * n_other], rest[2 * n_other:]
        slope = sl_ref[...][:, :1]
        valid0, bias = _att_mask_bias(slope, d, pl.program_id(1) == 0)
        valid_in, _ = _att_mask_bias(slope, d, None)
        gq, gk = gq_ref[...], gk_ref[...]
        memo = {}

        def get(kind, r, b):
            if (kind, r, b) not in memo:
                if kind == "k":
                    val = _head_norm(_rows(k_ref if b >= 0 else kp_ref, r, max(b, 0), d), gk)[0]
                else:
                    val = _rows(v_ref if b >= 0 else vp_ref, r, max(b, 0), d).astype(BF16)
                memo[(kind, r, b)] = val
            return memo[(kind, r, b)]

        units = [(r, b) for r in range(d) for b in range(nb)]
        for c0 in range(0, len(units), ATT_TOGETHER):
            us = units[c0:c0 + ATT_TOGETHER]
            q = [_head_norm(_rows(q_ref, r, b, d), gq)[0] for r, b in us]
            k2 = [jnp.concatenate([get("k", r, b - 1), get("k", r, b)], axis=0) for r, b in us]
            v2 = [jnp.concatenate([get("v", r, b - 1), get("v", r, b)], axis=0) for r, b in us]
            s = [jnp.where(valid0 if b == 0 else valid_in, _dot(q[i], k2[i], _NT) * scale + bias, NEG)
                 for i, (r, b) in enumerate(us)]
            m = [jnp.max(si, axis=-1, keepdims=True) for si in s]
            p = [jnp.exp(si - mi) for si, mi in zip(s, m)]
            den = [jnp.sum(pi, axis=-1, keepdims=True) for pi in p]
            o = [_dot(pi.astype(BF16), vi) / di for pi, vi, di in zip(p, v2, den)]
            for i, (r, b) in enumerate(us):
                _put_rows(o_ref, r, b, d, o[i])
                _put_rows(l_ref, r, b, d, jnp.broadcast_to(m[i] + jnp.log(den[i]), (BLK, DH)))
        if n_other:
            os_ = [ref[...] for ref in other_refs[:n_other]] + [o_ref[...]]
            ls_ = [ref[...] for ref in other_refs[n_other:]] + [l_ref[...]]
            m = ls_[0]
            for l in ls_[1:]:
                m = jnp.maximum(m, l)
            es = [jnp.exp(l - m) for l in ls_]
            tot, mix = es[0], es[0] * os_[0]
            for e, o in zip(es[1:], os_[1:]):
                tot, mix = tot + e, mix + e * o
            o_ref[...] = mix / tot
            l_ref[...] = m + jnp.log(tot)

    cur, prev, _, slot, _, head = _att_specs(gi, d, nb, S)
    return pl.pallas_call(
        body, name=f"att_fwd_g{gi}", grid=(HPG, nsb),
        in_specs=[cur(0), cur(1), cur(2), prev(1), prev(2), head, head, head] + [slot] * (2 * n_other),
        out_specs=[slot, slot],
        out_shape=[_sds((S, ATT_OUT_W), F32), _sds((S, ATT_OUT_W), F32)],
        compiler_params=_cparams(("parallel", "arbitrary")),
    )(z_att, z_att, z_att, z_att, z_att, gq3, gk3, slopes3, *[o for o, _ in others], *[l for _, l in others])


def _ret_tables(lg):
    ri = lax.broadcasted_iota(jnp.int32, (CHUNK, CHUNK), 0)
    ci = lax.broadcasted_iota(jnp.int32, (CHUNK, CHUNK), 1)
    diff = (ri - ci).astype(F32)
    decay = jnp.where(diff >= 0, jnp.exp(lg * jnp.maximum(diff, 0.0)), 0.0)
    idx = lax.broadcasted_iota(jnp.int32, (CHUNK, 1), 0).astype(F32)
    xi = jnp.exp(lg * (idx + 1.0))
    zeta = jnp.exp(lg * (CHUNK - 1.0 - idx))
    return decay, xi, zeta, math.exp(lg * CHUNK)


def _ret_specs(nsteps, rev):
    idx = (lambda n: nsteps - 1 - n) if rev else (lambda n: n)
    rows = CHUNK * RET_PER_STEP
    qk = lambda off: pl.BlockSpec((rows, RET_QK_W), lambda n: (idx(n), off // RET_QK_W))
    vv = lambda off: pl.BlockSpec((rows, RET_V_W), lambda n: (idx(n), off // RET_V_W))
    par = pl.BlockSpec((1, RET_V_W), lambda n: (0, 0))
    wide = pl.BlockSpec((rows, RET_V_W), lambda n: (idx(n), 0))
    st = pl.BlockSpec((RET_HEADS, RET_PER_STEP, RET_QK, RET_V), lambda n: (0, idx(n), 0, 0))
    return qk, vv, par, wide, st


def _ret_fwd(z_rest, gn_g, gn_b):
    S = z_rest.shape[0]
    nch = S // CHUNK
    nsteps = nch // RET_PER_STEP

    def body(q_ref, k_ref, v_ref, gr_ref, g_ref, b_ref, or_ref, o_ref, st_ref, state):
        @pl.when(pl.program_id(0) == 0)
        def _():
            state[...] = jnp.zeros_like(state)

        for c in range(RET_PER_STEP):
            rc = slice(c * CHUNK, (c + 1) * CHUNK)
            for h in range(RET_HEADS):
                decay, xi, zeta, gch = _ret_tables(LOG_GAMMA[h])
                cq = slice(h * RET_QK, (h + 1) * RET_QK)
                cv = slice(h * RET_V, (h + 1) * RET_V)
                q = q_ref[rc, cq]
                kc32 = k_ref[rc, cq].astype(F32) * (RET_QK ** -0.5)
                kc = kc32.astype(BF16)
                v = v_ref[rc, cv]
                st = state[h]
                stb = st.astype(BF16)
                st_ref[h, c] = stb
                s = _dot(q, kc, _NT) * decay
                o = _dot(s.astype(BF16), v) + _dot(q, stb) * xi
                state[h] = st * gch + _dot((kc32 * zeta).astype(BF16), v, _TN)
                mu = jnp.mean(o, axis=-1, keepdims=True)
                cen = o - mu
                yh = cen * lax.rsqrt(jnp.mean(cen * cen, axis=-1, keepdims=True) + EPS)
                gr = gr_ref[rc, cv].astype(F32)
                or_ref[rc, cv] = ((yh * g_ref[:, cv] + b_ref[:, cv]) * (gr * jax.nn.sigmoid(gr))).astype(BF16)
                o_ref[rc, cv] = o

    qk, vv, par, wide, st = _ret_specs(nsteps, False)
    return pl.pallas_call(
        body, name="ret_fwd", grid=(nsteps,),
        in_specs=[qk(R_Q), qk(R_K), vv(R_V), vv(R_G), par, par],
        out_specs=[wide, wide, st],
        out_shape=[_sds((S, RET_V_W), BF16), _sds((S, RET_V_W), F32), _sds((RET_HEADS, nch, RET_QK, RET_V), BF16)],
        scratch_shapes=[pltpu.VMEM((RET_HEADS, RET_QK, RET_V), F32)],
        compiler_params=_cparams(("arbitrary",)),
    )(z_rest, z_rest, z_rest, z_rest, gn_g, gn_b)


def _merge_fwd(o_a, o_r, z_rest, x, wpa, wpb, wout, g2, tm):
    S = x.shape[0]

    def body(oa_ref, or_ref, ga_ref, gb_ref, x_ref, wpa_ref, wpb_ref, wo_ref, g2_ref,
             x1_ref, y_ref, pa_ref, pb_ref, xn2_ref):
        pa = _dot(oa_ref[...].astype(BF16), wpa_ref[...])
        pb = _dot(or_ref[...], wpb_ref[...])
        y = jax.nn.sigmoid(ga_ref[...].astype(F32)) * pa + jax.nn.sigmoid(gb_ref[...].astype(F32)) * pb
        yb = y.astype(BF16)
        x1 = x_ref[...] + _dot(yb, wo_ref[...])
        x1_ref[...] = x1
        y_ref[...] = yb
        pa_ref[...] = pa.astype(BF16)
        pb_ref[...] = pb.astype(BF16)
        xn2_ref[...] = (x1 * _rms_rows(x1) * g2_ref[...]).astype(BF16)

    row = lambda w: pl.BlockSpec((tm, w), lambda i: (i, 0))
    full = lambda a: pl.BlockSpec(a.shape, lambda i: (0, 0))
    return pl.pallas_call(
        body, name="merge_fwd", grid=(S // tm,),
        in_specs=[row(ATT_OUT_W), row(RET_V_W),
                  pl.BlockSpec((tm, D_MODEL), lambda i: (i, R_GA // D_MODEL)),
                  pl.BlockSpec((tm, D_MODEL), lambda i: (i, R_GB // D_MODEL)),
                  row(D_MODEL), full(wpa), full(wpb), full(wout), full(g2)],
        out_specs=[row(D_MODEL)] * 5,
        out_shape=[_sds((S, D_MODEL), F32)] + [_sds((S, D_MODEL), BF16)] * 4,
        compiler_params=_cparams(("parallel",)),
    )(o_a, o_r, z_rest, z_rest, x, wpa, wpb, wout, g2)


def _rms_bwd(dy, xv, g):
    r = _rms_rows(xv)
    xh = xv * r
    dg = dy * g
    dx = r * (dg - xh * jnp.mean(dg * xh, axis=-1, keepdims=True))
    return dx, jnp.sum(dy * xh, axis=0, keepdims=True)


def _norm_gate_bwd(dout, o, gr, gam, bet):
    mean = lambda xs: [jnp.mean(x, axis=-1, keepdims=True) for x in xs]
    cen = [x - m for x, m in zip(o, mean(o))]
    rstd = [lax.rsqrt(v + EPS) for v in mean([c * c for c in cen])]
    yh = [c * r for c, r in zip(cen, rstd)]
    y = [a * g + b for a, g, b in zip(yh, gam, bet)]
    sg = [jax.nn.sigmoid(g) for g in gr]
    dy = [d * (g * s) for d, g, s in zip(dout, gr, sg)]
    dgr = [d * a * (s * (1.0 + g * (1.0 - s))) for d, a, s, g in zip(dout, y, sg, gr)]
    dyh = [d * g for d, g in zip(dy, gam)]
    m1, m2 = mean(dyh), mean([a * b for a, b in zip(dyh, yh)])
    do = [r * (d - a - h * b) for r, d, a, h, b in zip(rstd, dyh, m1, yh, m2)]
    dg = [jnp.sum(d * h, axis=0, keepdims=True) for d, h in zip(dy, yh)]
    db = [jnp.sum(d, axis=0, keepdims=True) for d in dy]
    return do, dgr, dg, db


def _ret_bwd(do, states, z_rest, dz_rest):
    S = z_rest.shape[0]
    nsteps = S // CHUNK // RET_PER_STEP

    def body(do_ref, st_ref, q_ref, k_ref, v_ref, dz_in, dz_ref, gst):
        del dz_in

        @pl.when(pl.program_id(0) == 0)
        def _():
            gst[...] = jnp.zeros_like(gst)

        for c in reversed(range(RET_PER_STEP)):
            rc = slice(c * CHUNK, (c + 1) * CHUNK)
            for h in range(RET_HEADS):
                decay, xi, zeta, gch = _ret_tables(LOG_GAMMA[h])
                cq = slice(h * RET_QK, (h + 1) * RET_QK)
                cv = slice(h * RET_V, (h + 1) * RET_V)
                q = q_ref[rc, cq]
                kc32 = k_ref[rc, cq].astype(F32) * (RET_QK ** -0.5)
                kc = kc32.astype(BF16)
                v = v_ref[rc, cv]
                dob = do_ref[rc, cv]
                a = (_dot(q, kc, _NT) * decay).astype(BF16)
                da = (_dot(dob, v, _NT) * decay).astype(BF16)
                dcross = (dob.astype(F32) * xi).astype(BF16)
                g_next = gst[h]
                gb = g_next.astype(BF16)
                dq = _dot(da, kc) + _dot(dcross, st_ref[h, c], _NT)
                dkc = _dot(da, q, _TN)
                dkz = _dot(v, gb, _NT)
                dv = _dot(a, dob, _TN) + _dot((kc32 * zeta).astype(BF16), gb)
                gst[h] = g_next * gch + _dot(q, dcross, _TN)
                dz_ref[rc, R_Q + h * RET_QK:R_Q + (h + 1) * RET_QK] = dq.astype(BF16)
                dz_ref[rc, R_K + h * RET_QK:R_K + (h + 1) * RET_QK] = (
                    (dkc + dkz * zeta) * (RET_QK ** -0.5)).astype(BF16)
                dz_ref[rc, R_V + h * RET_V:R_V + (h + 1) * RET_V] = dv.astype(BF16)

    qk, vv, _, wide, st = _ret_specs(nsteps, True)
    return pl.pallas_call(
        body, name="ret_bwd", grid=(nsteps,),
        in_specs=[wide, st, qk(R_Q), qk(R_K), vv(R_V), _ANY],
        out_specs=pl.BlockSpec((CHUNK * RET_PER_STEP, R_G), lambda n: (nsteps - 1 - n, 0)),
        out_shape=_sds(dz_rest.shape, BF16),
        input_output_aliases={5: 0},
        scratch_shapes=[pltpu.VMEM((RET_HEADS, RET_QK, RET_V), F32)],
        compiler_params=_cparams(("arbitrary",)),
    )(do, states, z_rest, z_rest, z_rest, dz_rest)


def _att_probs(q, k, lmix, valid, bias):
    s = _dot(q, k, _NT) * (DH ** -0.5) + bias
    return jnp.where(valid, jnp.exp(jnp.where(valid, s, NEG) - lmix), 0.0)


def _att_bwd_group(z_att, dz_att, do_a, delta_lmix, gq3, gk3, slopes3, gi, d, nb, exchange=None):
    S = z_att.shape[0]
    sb = BLK * d * nb
    nsb = S // sb
    scale = DH ** -0.5
    aliased = dz_att is not None
    n_xi = len(exchange.ins) if exchange else 0
    n_xo = len(exchange.out_shapes) if exchange else 0

    def body(q_ref, k_ref, v_ref, kp_ref, vp_ref, qn_ref, do_ref, don_ref, dl_ref, dln_ref,
             gq_ref, gk_ref, sl_ref, *rest):
        rest = rest[1:] if aliased else rest
        x_in, (dz_ref, dgq_ref, dgk_ref), rest = rest[:n_xi], rest[n_xi:n_xi + 3], rest[n_xi + 3:]
        x_out, stage, x_sems = rest[:n_xo], rest[n_xo], rest[n_xo + 1:]
        n = pl.program_id(1)
        step = pl.program_id(0) * nsb + n
        if exchange:
            start, _, finish = exchange.split(x_in, x_out, x_sems)
            pl.when(step == 0)(start)
        slope = sl_ref[...][:, :1]
        valid0, bias = _att_mask_bias(slope, d, n == 0)
        valid_in, _ = _att_mask_bias(slope, d, None)
        qi = lax.broadcasted_iota(jnp.int32, (BLK, BLK), 0)
        kj = lax.broadcasted_iota(jnp.int32, (BLK, BLK), 1)
        dist_n = BLK + qi - kj
        valid_n_in = dist_n <= BLK
        valid_n_last = valid_n_in & (n < nsb - 1)
        bias_n = -slope * (dist_n * d).astype(F32)
        gq, gk = gq_ref[...], gk_ref[...]
        dgq = jnp.zeros((1, DH), F32)
        dgk = jnp.zeros((1, DH), F32)
        memo = {}

        def get(kind, r, b):
            if (kind, r, b) not in memo:
                inner = 0 <= b < nb
                bb = b if inner else 0
                if kind == "q":
                    val = _head_norm(_rows(q_ref if inner else qn_ref, r, bb, d), gq)
                elif kind == "k":
                    val = _head_norm(_rows(k_ref if inner else kp_ref, r, bb, d), gk)
                elif kind == "v":
                    val = _rows(v_ref if inner else vp_ref, r, bb, d).astype(BF16)
                elif kind == "do":
                    val = _rows(do_ref if inner else don_ref, r, bb, d).astype(BF16)
                elif kind == "dl+lm":
                    val = _rows(dl_ref if inner else dln_ref, r, bb, d)
                elif kind == "dl":
                    val = get("dl+lm", r, b)[:, :1]
                else:
                    val = get("dl+lm", r, b)[:, DH // 2:DH // 2 + 1]
                memo[(kind, r, b)] = val
            return memo[(kind, r, b)]

        units = [(r, b) for r in range(d) for b in range(nb)]
        for c0 in range(0, len(units), ATT_TOGETHER):
            us = units[c0:c0 + ATT_TOGETHER]
            k2 = [jnp.concatenate([get("k", r, b - 1)[0], get("k", r, b)[0]], axis=0) for r, b in us]
            v2 = [jnp.concatenate([get("v", r, b - 1), get("v", r, b)], axis=0) for r, b in us]
            p = [_att_probs(get("q", r, b)[0], k2[i], get("lm", r, b), valid0 if b == 0 else valid_in, bias)
                 for i, (r, b) in enumerate(us)]
            dp = [_dot(get("do", r, b), v2[i], _NT) for i, (r, b) in enumerate(us)]
            ds = [(p[i] * (dp[i] - get("dl", r, b)) * scale).astype(BF16) for i, (r, b) in enumerate(us)]
            dq = [_dot(ds[i], k2[i]) for i in range(len(us))]
            p_n = [_att_probs(get("q", r, b + 1)[0], get("k", r, b)[0], get("lm", r, b + 1),
                              valid_n_last if b == nb - 1 else valid_n_in, bias_n) for r, b in us]
            dp_n = [_dot(get("do", r, b + 1), get("v", r, b), _NT) for r, b in us]
            ds_n = [(p_n[i] * (dp_n[i] - get("dl", r, b + 1)) * scale).astype(BF16) for i, (r, b) in enumerate(us)]
            dk = [_dot(ds[i][:, BLK:], get("q", r, b)[0], _TN) + _dot(ds_n[i], get("q", r, b + 1)[0], _TN)
                  for i, (r, b) in enumerate(us)]
            dv = [_dot(p[i][:, BLK:].astype(BF16), get("do", r, b), _TN)
                  + _dot(p_n[i].astype(BF16), get("do", r, b + 1), _TN) for i, (r, b) in enumerate(us)]
            for i, (r, b) in enumerate(us):
                _, qh, qr = get("q", r, b)
                _, kh, kr = get("k", r, b)
                dxq, dg_q = _head_norm_bwd(dq[i], qh, qr, gq)
                dxk, dg_k = _head_norm_bwd(dk[i], kh, kr, gk)
                _put_rows(stage.at[0], r, b, d, dxq)
                _put_rows(stage.at[1], r, b, d, dxk)
                _put_rows(stage.at[2], r, b, d, dv[i])
                dgq, dgk = dgq + dg_q, dgk + dg_k
        for kind in range(3):
            dz_ref[:, kind * DH:(kind + 1) * DH] = stage[kind].astype(BF16)
        _acc_rows(dgq_ref, dgq, n == 0)
        _acc_rows(dgk_ref, dgk, n == 0)
        if exchange:
            pl.when(step == HPG * nsb - 1)(finish)

    cur, prev, nxt, slot, slot_next, head = _att_specs(gi, d, nb, S)
    gain = pl.BlockSpec((None, 1, DH), lambda j, n: (j, 0, 0))
    in_specs = [cur(0), cur(1), cur(2), prev(1), prev(2), nxt(0),
                slot, slot_next, slot, slot_next, head, head, head]
    args = [z_att] * 6 + [do_a, do_a, delta_lmix, delta_lmix, gq3, gk3, slopes3]
    if aliased:
        in_specs.append(_ANY)
        args.append(dz_att)
    res = pl.pallas_call(
        body, name=f"att_bwd_g{gi}", grid=(HPG, nsb),
        in_specs=in_specs + [_ANY] * n_xi,
        out_specs=[pl.BlockSpec((sb, 3 * DH), lambda j, n: (n, gi * HPG + j)), gain, gain] + [_ANY] * n_xo,
        out_shape=[_sds(z_att.shape, BF16), _sds((HPG, 1, DH), F32), _sds((HPG, 1, DH), F32)]
        + (exchange.out_shapes if exchange else []),
        input_output_aliases={len(args) - 1: 0} if aliased else {},
        scratch_shapes=[pltpu.VMEM((3, sb, DH), F32)] + (exchange.sems if exchange else []),
        compiler_params=_cparams(("arbitrary", "arbitrary") if exchange else ("parallel", "arbitrary")),
    )(*args, *(exchange.ins if exchange else []))
    return res[:3], res[3:]


def _step(x, target, norm1_g, q_norm_g, k_norm_g, gn_g, gn_b, norm2_g, shards, core):
    S = x.shape[0]
    tm = min(512, S)
    ts = min(256, S)
    tk = min(2048, S)
    tk2 = min(4096, S)
    later_shards = shards[1:]
    gq3 = q_norm_g.reshape(ATT_HEADS, 1, DH)
    gk3 = k_norm_g.reshape(ATT_HEADS, 1, DH)
    slopes3 = jnp.asarray(np.broadcast_to(ALIBI[:, None, None], (ATT_HEADS, 1, DH)).copy())

    xn, (g_in,) = _rmsnorm_fwd(x, norm1_g, tm, _gather_exchange(shards[:1]))
    w_in = _cols(g_in)
    w_att = w_in[:, :3 * ATT_W].reshape(D_MODEL, 3, ATT_HEADS, DH).transpose(0, 2, 1, 3).reshape(D_MODEL, 3 * ATT_W)
    w_rest = w_in[:, 3 * ATT_W:]
    w_att_t, w_rest_t = w_att.T, w_rest.T
    z_att = _mm("in_proj_att", xn, w_att, "nn", tm, 3 * ATT_W, D_MODEL,
                [(_sds((S, 3 * ATT_W), F32), _tile_ij(tm, 3 * ATT_W))], _epi_store(F32))[0]
    (z_rest,), gathered = _mm("in_proj_rest", xn, w_rest, "nn", tm, REST_W, D_MODEL,
                              [(_sds((S, REST_W), BF16), _tile_ij(tm, REST_W))], _epi_store(BF16),
                              exchange=_gather_exchange(later_shards))
    w_pa, w_pb, w_out, w_up, w_down = [f(g) for f, g in zip((_cols, _rows_of, _rows_of, _cols, _rows_of), gathered)]
    w_up_t, w_down_t, w_out_t, w_pa_t, w_pb_t = w_up.T, w_down.T, w_out.T, w_pa.T, w_pb.T
    done = []
    for gi, ((_, d), nb) in enumerate(zip(ATT_GROUPS, ATT_BLOCKS_PER_STEP)):
        last = gi == len(ATT_GROUPS) - 1
        done.append(_att_fwd_group(z_att, gq3, gk3, slopes3, gi, d, nb, others=tuple(done) if last else ()))
    o_a, lmix = done[-1]
    o_r, o_pre, states = _ret_fwd(z_rest, gn_g, gn_b)
    x1, y, pa, pb, xn2 = _merge_fwd(o_a, o_r, z_rest, x, w_pa, w_pb, w_out, norm2_g, tm)

    def epi_up(acc, ex, out):
        r = jnp.maximum(acc, 0.0)
        out[0][...] = (r * r).astype(BF16)
        out[1][...] = r.astype(BF16)

    h, relu_u = _mm("mlp_up", xn2, w_up, "nn", tm, D_FF, D_MODEL,
                    [(_sds((S, D_FF), BF16), _tile_ij(tm, D_FF)), (_sds((S, D_FF), BF16), _tile_ij(tm, D_FF))], epi_up)

    def epi_down(acc, ex, out):
        diff = ex[0][...] + acc - ex[1][...]
        out[0][...] = diff * (1.0 / D_MODEL)
        out[1][...] = jnp.broadcast_to(jnp.sum(diff * diff) * (1.0 / (8 * LANES)), (8, LANES))

    row_tile = _tile_ij(tm, D_MODEL)
    dx2, loss_parts = _mm(
        "mlp_down_loss", h, w_down, "nn", tm, D_MODEL, D_FF,
        [(_sds((S, D_MODEL), F32), row_tile),
         (_sds((S // tm * 8, LANES), F32), pl.BlockSpec((8, LANES), lambda i, j, k: (i, 0)))],
        epi_down, extras=[(x1, row_tile), (target, row_tile)])
    loss_local = jnp.sum(loss_parts) * (0.5 / D_MODEL)

    def epi_du(acc, ex, out):
        out[0][...] = (acc * (2.0 * ex[0][...].astype(F32))).astype(BF16)

    du = _mm("mlp_down_bwd", dx2, w_down_t, "nn", tm, D_FF, D_MODEL,
             [(_sds((S, D_FF), BF16), _tile_ij(tm, D_FF))], epi_du, extras=[(relu_u, _tile_ij(tm, D_FF))])[0]
    gw_down = _mm("gw_down", h, dx2, "tn", 1024, D_MODEL, tk,
                  [(_sds((D_FF, D_MODEL), BF16), _tile_ij(1024, D_MODEL))], _epi_store(BF16))[0]
    gw_up = _mm("gw_up", xn2, du, "tn", D_MODEL, 512, tk2,
                [(_sds((N_DEV, D_MODEL, 512), BF16), pl.BlockSpec((None, D_MODEL, 512), lambda i, j, k: (j, 0, 0)))],
                _epi_store(BF16))[0]

    vec = pl.BlockSpec((1, D_MODEL), lambda i, j, k: (0, 0))
    seq_sem = ("arbitrary", "arbitrary", "arbitrary")

    def epi_norm2(acc, ex, out):
        dx, dg = _rms_bwd(acc, ex[0][...], ex[2][...])
        out[0][...] = ex[1][...] + dx
        _acc_rows(out[1], dg, pl.program_id(0) == 0)

    dx1, g_norm2 = _mm(
        "mlp_up_bwd", du, w_up_t, "nn", tm, D_MODEL, D_FF,
        [(_sds((S, D_MODEL), F32), row_tile), (_sds((1, D_MODEL), F32), vec)],
        epi_norm2, extras=[(x1, row_tile), (dx2, row_tile), (norm2_g, vec)], sem=seq_sem)

    def epi_dy(acc, ex, out):
        sa = jax.nn.sigmoid(ex[0][...].astype(F32))
        sb = jax.nn.sigmoid(ex[1][...].astype(F32))
        out[0][...] = (acc * sa).astype(BF16)
        out[1][...] = (acc * sb).astype(BF16)
        out[2][:, :D_MODEL] = (acc * ex[2][...].astype(F32) * (sa * (1.0 - sa))).astype(BF16)
        out[2][:, D_MODEL:] = (acc * ex[3][...].astype(F32) * (sb * (1.0 - sb))).astype(BF16)

    ga_spec = pl.BlockSpec((tm, D_MODEL), lambda i, j, k: (i, R_GA // D_MODEL))
    gb_spec = pl.BlockSpec((tm, D_MODEL), lambda i, j, k: (i, R_GB // D_MODEL))
    gates_spec = pl.BlockSpec((tm, 2 * D_MODEL), lambda i, j, k: (i, R_GA // (2 * D_MODEL)))
    dpa, dpb, dz_rest = _mm(
        "out_proj_bwd", dx1, w_out_t, "nn", tm, D_MODEL, D_MODEL,
        [(_sds((S, D_MODEL), BF16), row_tile), (_sds((S, D_MODEL), BF16), row_tile), (_sds((S, REST_W), BF16), gates_spec)],
        epi_dy, extras=[(z_rest, ga_spec), (z_rest, gb_spec), (pa, row_tile), (pb, row_tile)])
    gw_out = _mm("gw_out", y, dx1, "tn", D_MODEL, D_MODEL, tk,
                 [(_sds((D_MODEL, D_MODEL), BF16), _tile_ij(D_MODEL, D_MODEL))], _epi_store(BF16))[0]
    gw_pa = _mm("gw_proj_a", o_a, dpa, "tn", ATT_OUT_W, D_MODEL, tk,
                [(_sds((ATT_OUT_W, D_MODEL), BF16), _tile_ij(ATT_OUT_W, D_MODEL))], _epi_store(BF16))[0]
    gw_pb = _mm("gw_proj_b", o_r, dpb, "tn", 1024, D_MODEL, tk2,
                [(_sds((RET_V_W, D_MODEL), BF16), _tile_ij(1024, D_MODEL))], _epi_store(BF16))[0]

    def epi_doa(acc, ex, out):
        out[0][...] = acc
        prod = acc * ex[0][...]
        delta = jnp.concatenate(
            [jnp.broadcast_to(jnp.sum(prod[:, j * DH:(j + 1) * DH], axis=-1, keepdims=True), (prod.shape[0], DH))
             for j in range(HPG)], axis=1)
        lane = lax.broadcasted_iota(jnp.int32, delta.shape, 1)
        out[1][...] = jnp.where(lane % DH < DH // 2, delta, ex[1][...])

    slot_tile = _tile_ij(tm, ATT_OUT_W)
    do_a, delta_lmix = _mm("proj_a_bwd", dpa, w_pa_t, "nn", tm, ATT_OUT_W, D_MODEL,
                           [(_sds((S, ATT_OUT_W), F32), slot_tile), (_sds((S, ATT_OUT_W), F32), slot_tile)],
                           epi_doa, extras=[(o_a, slot_tile), (lmix, slot_tile)])
    owed = [_chip_core(_by_owner_cols(gw_pa)), _chip_core(gw_pb.reshape(N_DEV, -1, D_MODEL)),
            _chip_core(gw_out.reshape(N_DEV, -1, D_MODEL)), _chip_core(gw_up), _chip_core(gw_down.reshape(N_DEV, -1, D_MODEL))]
    names = ("w_proj_a", "w_proj_b", "w_out", "w_up", "w_down")

    def epi_dor(acc, ex, out):
        cvs = [slice(h * RET_V, (h + 1) * RET_V) for h in range(RET_HEADS)]
        do, dgr, dg, db = _norm_gate_bwd([acc[:, cv] for cv in cvs], [ex[0][:, cv] for cv in cvs],
                                         [ex[1][:, cv].astype(F32) for cv in cvs],
                                         [ex[2][:, cv] for cv in cvs], [ex[3][:, cv] for cv in cvs])
        for h, cv in enumerate(cvs):
            out[0][:, cv] = do[h].astype(BF16)
            out[1][:, cv] = dgr[h].astype(BF16)
        first = pl.program_id(0) == 0
        _acc_rows(out[2], jnp.concatenate(dg, axis=1), first)
        _acc_rows(out[3], jnp.concatenate(db, axis=1), first)

    wide_tile = _tile_ij(tm, RET_V_W)
    gate_tile = pl.BlockSpec((tm, RET_V_W), lambda i, j, k: (i, R_G // RET_V_W))
    wide_vec = pl.BlockSpec((1, RET_V_W), lambda i, j, k: (0, 0))
    (do_ret, dz_rest, g_gn_g, g_gn_b), got = _mm(
        "proj_b_bwd", dpb, w_pb_t, "nn", tm, RET_V_W, D_MODEL,
        [(_sds((S, RET_V_W), BF16), wide_tile), (_sds((S, REST_W), BF16), gate_tile),
         (_sds((1, RET_V_W), F32), wide_vec), (_sds((1, RET_V_W), F32), wide_vec)],
        epi_dor, extras=[(o_pre, wide_tile), (z_rest, gate_tile), (gn_g, wide_vec), (gn_b, wide_vec)],
        exchange=_pair_exchange(owed), in_place=(dz_rest, 1))
    chip_sums = [_pair_sum(f"pair_sum_{n}", g, r, core, min(256, g.shape[2])) for n, g, r in zip(names, owed, got)]

    dz_rest = _ret_bwd(do_ret, states, z_rest, dz_rest)
    dz_att, gq_parts, gk_parts, parts_late = None, [], [], None
    for gi, ((_, d), nb) in enumerate(zip(ATT_GROUPS, ATT_BLOCKS_PER_STEP)):
        last = gi == len(ATT_GROUPS) - 1
        (dz_att, gq_p, gk_p), parts = _att_bwd_group(z_att, dz_att, do_a, delta_lmix, gq3, gk3, slopes3, gi, d, nb,
                                                     exchange=_chip_exchange(chip_sums) if last else None)
        parts_late = parts if last else parts_late
        gq_parts.append(gq_p)
        gk_parts.append(gk_p)
    g_qn = jnp.concatenate(gq_parts, axis=0).reshape(1, ATT_HEADS, DH)
    g_kn = jnp.concatenate(gk_parts, axis=0).reshape(1, ATT_HEADS, DH)

    gw_att = _mm("gw_in_att", xn, dz_att, "tn", D_MODEL, ATT_W, tk,
                 [(_sds((D_MODEL, 3 * ATT_W), BF16), _tile_ij(D_MODEL, ATT_W))], _epi_store(BF16))[0]
    gw_rest = _mm("gw_in_rest", xn, dz_rest, "tn", D_MODEL, 1024, tk2,
                  [(_sds((D_MODEL, REST_W), BF16), _tile_ij(D_MODEL, 1024))], _epi_store(BF16))[0]
    gw_att = gw_att.reshape(D_MODEL, ATT_HEADS, 3, DH).transpose(0, 2, 1, 3).reshape(D_MODEL, 3 * ATT_W)
    gw_in = jnp.concatenate([gw_att, gw_rest], axis=1)
    owed_in = _chip_core(_by_owner_cols(gw_in))
    (dxn_att,), (got_in,) = _mm("in_proj_att_bwd", dz_att, w_att_t, "nn", tm, D_MODEL, 3 * ATT_W,
                                [(_sds((S, D_MODEL), F32), row_tile)], _epi_store(F32), exchange=_pair_exchange([owed_in]))
    chip_sum_in = _pair_sum("pair_sum_w_in", owed_in, got_in, core, min(256, owed_in.shape[2]))

    def epi_norm1(acc, ex, out):
        dx, dg = _rms_bwd(acc + ex[0][...], ex[1][...], ex[3][...])
        out[0][...] = ex[2][...] + dx
        _acc_rows(out[1], dg, pl.program_id(0) == 0)

    short_tile = _tile_ij(ts, D_MODEL)
    (grad_x, g_norm1), parts_in = _mm(
        "in_proj_rest_bwd", dz_rest, w_rest_t, "nn", ts, D_MODEL, REST_W,
        [(_sds((S, D_MODEL), F32), short_tile), (_sds((1, D_MODEL), F32), vec)],
        epi_norm1, extras=[(dxn_att, short_tile), (x, short_tile), (dx1, short_tile), (norm1_g, vec)],
        exchange=_chip_exchange([chip_sum_in]))

    small = (g_norm1, g_qn, g_kn, g_gn_g, g_gn_b, g_norm2)
    return loss_local, grad_x, list(parts_in) + list(parts_late), small


def _position():
    return lax.axis_index("x"), lax.axis_index("y"), lax.axis_index("c")


def _other_chips(x, y):
    return [(1 - x, y), (x, 1 - y), (1 - x, 1 - y)]


_ANY = pl.BlockSpec(memory_space=pl.ANY)


def _gather_exchange(shards):
    nw = len(shards)

    def plan(x_refs, out_refs, sems):
        send_sems, recv_sems, local_sems = sems
        x, y, c = _position()
        me, sibling = (x, y, c), (x, y, 1 - c)
        chips = _other_chips(x, y)

        def copy(w, k, block, to, own=False):
            px, py, pc = block
            rows = out_refs[w].at[4 * px + 2 * py + pc]
            return pltpu.make_async_remote_copy(
                src_ref=x_refs[w] if own else rows, dst_ref=rows,
                send_sem=send_sems.at[7 * w + k], recv_sem=recv_sems.at[7 * w + k], device_id=to, device_id_type=MESH)

        def mine(w):
            return pltpu.make_async_copy(x_refs[w], out_refs[w].at[4 * x + 2 * y + c], local_sems.at[w])

        def own_sends(w):
            return [copy(w, 0, me, sibling, own=True)] + [copy(w, 1 + j, me, (*chip, c), own=True)
                                                          for j, chip in enumerate(chips)]

        def start():
            for w in range(nw):
                mine(w).start()
                for cp in own_sends(w):
                    cp.start()

        def relay():
            for j, chip in enumerate(chips):
                for w in range(nw):
                    copy(w, 1 + j, (*chip, c), me).wait_recv()
                    copy(w, 4 + j, (*chip, c), sibling).start()

        def finish():
            for w in range(nw):
                copy(w, 0, sibling, me).wait_recv()
                for j, chip in enumerate(chips):
                    copy(w, 4 + j, (*chip, 1 - c), me).wait_recv()
            for w in range(nw):
                for cp in own_sends(w):
                    cp.wait_send()
                for j, chip in enumerate(chips):
                    copy(w, 4 + j, (*chip, c), sibling).wait_send()
                mine(w).wait()

        return start, relay, finish

    return _Exchange(shards, [_sds((N_DEV,) + s.shape, s.dtype) for s in shards],
                     [pltpu.SemaphoreType.DMA((7 * nw,)), pltpu.SemaphoreType.DMA((7 * nw,)),
                      pltpu.SemaphoreType.DMA((nw,))], plan)


def _run_exchange(name, exchange):
    n_in, n_out = len(exchange.ins), len(exchange.out_shapes)

    def body(*refs):
        start, relay, finish = exchange.split(refs[:n_in], refs[n_in:n_in + n_out], refs[n_in + n_out:])
        start()
        if relay is not None:
            relay()
        finish()

    return pl.pallas_call(
        body, name=name, out_shape=exchange.out_shapes,
        in_specs=[_ANY] * n_in, out_specs=[_ANY] * n_out, scratch_shapes=exchange.sems,
    )(*exchange.ins)


def _pair_exchange(grads):
    ng = len(grads)

    def plan(g_refs, out_refs, sems):
        send_sems, recv_sems = sems
        x, y, c = _position()

        def copies():
            return [pltpu.make_async_remote_copy(
                src_ref=g_refs[w].at[:, 1 - c], dst_ref=out_refs[w], send_sem=send_sems.at[w],
                recv_sem=recv_sems.at[w], device_id=(x, y, 1 - c), device_id_type=MESH) for w in range(ng)]

        def start():
            for cp in copies():
                cp.start()

        def finish():
            for cp in copies():
                cp.wait()

        return start, None, finish

    return _Exchange(grads, [_sds((N_CHIP,) + g.shape[2:], g.dtype) for g in grads],
                     [pltpu.SemaphoreType.DMA((ng,)), pltpu.SemaphoreType.DMA((ng,))], plan)


def _small_all_gather(small):
    def plan(in_refs, out_refs, sems):
        (s_ref,), (s_out,) = in_refs, out_refs
        send_sems, recv_sems, local_sem = sems
        x, y, c = _position()
        me_id = 4 * x + 2 * y + c
        flips = [(a, b, e) for a in (0, 1) for b in (0, 1) for e in (0, 1)][1:]
        peers = [(x ^ a, y ^ b, c ^ e) for a, b, e in flips]

        def start():
            pltpu.make_async_copy(s_ref, s_out.at[me_id], local_sem).start()
            for k, p in enumerate(peers):
                pltpu.make_async_remote_copy(
                    src_ref=s_ref, dst_ref=s_out.at[me_id], send_sem=send_sems.at[k], recv_sem=recv_sems.at[k],
                    device_id=p, device_id_type=MESH).start()

        def finish():
            for k, (px, py, pc) in enumerate(peers):
                pltpu.make_async_remote_copy(
                    src_ref=s_ref, dst_ref=s_out.at[4 * px + 2 * py + pc], send_sem=send_sems.at[k],
                    recv_sem=recv_sems.at[k], device_id=(px, py, pc), device_id_type=MESH).wait()
            pltpu.make_async_copy(s_ref, s_out.at[me_id], local_sem).wait()

        return start, None, finish

    return _run_exchange("small_grad_all_gather", _Exchange(
        [small], [_sds((N_DEV,) + small.shape, small.dtype)],
        [pltpu.SemaphoreType.DMA((7,)), pltpu.SemaphoreType.DMA((7,)), pltpu.SemaphoreType.DMA], plan))[0]


def _pair_sum(name, g, got, core, tr):
    n_chip, _, R, C = g.shape

    def body(c_ref, a_ref, b_ref, o_ref):
        del c_ref
        o_ref[...] = (a_ref[...].astype(F32) + b_ref[...].astype(F32)).astype(o_ref.dtype)

    return pl.pallas_call(
        body, name=name,
        grid_spec=pltpu.PrefetchScalarGridSpec(
            num_scalar_prefetch=1, grid=(n_chip, R // tr),
            in_specs=[pl.BlockSpec((None, None, tr, C), lambda ch, i, c_ref: (ch, c_ref[0], i, 0)),
                      pl.BlockSpec((None, tr, C), lambda ch, i, c_ref: (ch, i, 0))],
            out_specs=pl.BlockSpec((None, tr, C), lambda ch, i, c_ref: (ch, i, 0))),
        out_shape=_sds(got.shape, got.dtype),
        compiler_params=_cparams(("parallel", "parallel")),
    )(core, g, got)


def _chip_exchange(parts):
    ng = len(parts)

    def plan(p_refs, out_refs, sems):
        send_sems, recv_sems, local_sems = sems
        x, y, c = _position()
        my_chip = 2 * x + y

        def copies():
            local = [pltpu.make_async_copy(p_refs[w].at[my_chip], out_refs[w].at[my_chip], local_sems.at[w])
                     for w in range(ng)]
            remote = [pltpu.make_async_remote_copy(
                src_ref=p_refs[w].at[2 * cx + cy], dst_ref=out_refs[w].at[my_chip],
                send_sem=send_sems.at[3 * w + k], recv_sem=recv_sems.at[3 * w + k],
                device_id=(cx, cy, c), device_id_type=MESH)
                for w in range(ng) for k, (cx, cy) in enumerate(_other_chips(x, y))]
            return local + remote

        def start():
            for cp in copies():
                cp.start()

        def finish():
            for cp in copies():
                cp.wait()

        return start, None, finish

    return _Exchange(parts, [_sds(p.shape, p.dtype) for p in parts],
                     [pltpu.SemaphoreType.DMA((3 * ng,)), pltpu.SemaphoreType.DMA((3 * ng,)),
                      pltpu.SemaphoreType.DMA((ng,))], plan)


def _adamw(name, parts, w, m, v, tr):
    n_parts = parts.shape[0]
    R, C = w.shape

    def body(p_ref, w_ref, m_ref, v_ref, g_ref, d_ref, mo_ref, vo_ref):
        g = p_ref[0].astype(F32)
        for i in range(1, n_parts):
            g = g + p_ref[i].astype(F32)
        m_new = ADAM_B1 * m_ref[...] + (1.0 - ADAM_B1) * g
        v_new = ADAM_B2 * v_ref[...] + (1.0 - ADAM_B2) * (g * g)
        m_hat = m_new / (1.0 - ADAM_B1 ** ADAM_STEP)
        v_hat = v_new / (1.0 - ADAM_B2 ** ADAM_STEP)
        g_ref[...] = g
        d_ref[...] = -ADAM_LR * (m_hat / (jnp.sqrt(v_hat) + ADAM_EPS) + ADAM_WD * w_ref[...])
        mo_ref[...] = m_new
        vo_ref[...] = v_new

    tile = pl.BlockSpec((tr, C), lambda i: (i, 0))
    return pl.pallas_call(
        body, name=name, grid=(R // tr,),
        in_specs=[pl.BlockSpec((n_parts, tr, C), lambda i: (0, i, 0)), tile, tile, tile],
        out_specs=[tile] * 4,
        out_shape=[_sds((R, C), F32)] * 4,
        compiler_params=_cparams(("parallel",)),
    )(parts, w, m, v)


def _flat_small(arrs):
    return jnp.concatenate([a.reshape(-1) for a in arrs]).reshape(SMALL_ROWS, LANES)


def _cols(g):
    return g.transpose(1, 0, 2).reshape(g.shape[1], -1)


def _rows_of(g):
    return g.reshape(-1, g.shape[2])


def _by_owner_cols(g):
    rows, cols = g.shape
    return g.reshape(rows, N_DEV, cols // N_DEV).transpose(1, 0, 2)


def _chip_core(g):
    return g.reshape((N_CHIP, 2) + g.shape[1:])


def kernel(x, norm1_g, w_in, q_norm_g, k_norm_g, ret_gn_g, ret_gn_b, w_proj_a, w_proj_b, w_out, norm2_g, w_up, w_down, loss_target, m_norm1_g, m_w_in, m_q_norm_g, m_k_norm_g, m_ret_gn_g, m_ret_gn_b, m_w_proj_a, m_w_proj_b, m_w_out, m_norm2_g, m_w_up, m_w_down, v_norm1_g, v_w_in, v_q_norm_g, v_k_norm_g, v_ret_gn_g, v_ret_gn_b, v_w_proj_a, v_w_proj_b, v_w_out, v_norm2_g, v_w_up, v_w_down):
    big_w = (w_in, w_proj_a, w_proj_b, w_out, w_up, w_down)
    big_m = (m_w_in, m_w_proj_a, m_w_proj_b, m_w_out, m_w_up, m_w_down)
    big_v = (v_w_in, v_w_proj_a, v_w_proj_b, v_w_out, v_w_up, v_w_down)
    small_w = (norm1_g, q_norm_g, k_norm_g, ret_gn_g, ret_gn_b, norm2_g)
    small_m = (m_norm1_g, m_q_norm_g, m_k_norm_g, m_ret_gn_g, m_ret_gn_b, m_norm2_g)
    small_v = (v_norm1_g, v_q_norm_g, v_k_norm_g, v_ret_gn_g, v_ret_gn_b, v_norm2_g)

    shards = [w[0].astype(BF16) for w in big_w]
    core = lax.axis_index("c").astype(jnp.int32).reshape(1)
    loss_local, grad_x, parts, small_g = _step(
        x[0], loss_target[0], norm1_g, q_norm_g[0], k_norm_g[0], ret_gn_g, ret_gn_b, norm2_g, shards, core)
    small_all = _small_all_gather(_flat_small(small_g))
    names = ("w_in", "w_proj_a", "w_proj_b", "w_out", "w_up", "w_down")

    res = {}
    for n, p, w, m, v in zip(names, parts, big_w, big_m, big_v):
        outs = _adamw(f"adamw_{n}", p, w[0], m[0], v[0], min(128, w.shape[1]))
        res[n] = [o[None] for o in outs]
    s_outs = _adamw("adamw_small", small_all, _flat_small(small_w), _flat_small(small_m), _flat_small(small_v), SMALL_ROWS)
    small_names = ("norm1_g", "q_norm_g", "k_norm_g", "ret_gn_g", "ret_gn_b", "norm2_g")
    for n in small_names:
        res[n] = []
    for o in s_outs:
        flat, off = o.reshape(-1), 0
        for n, w in zip(small_names, small_w):
            res[n].append(flat[off:off + w.size].reshape(w.shape))
            off += w.size

    order = ("norm1_g", "w_in", "q_norm_g", "k_norm_g", "ret_gn_g", "ret_gn_b", "w_proj_a", "w_proj_b", "w_out",
             "norm2_g", "w_up", "w_down")
    loss = lax.psum(loss_local, MESH_AXES)
    return (loss, grad_x[None], *[res[n][0] for n in order], *[res[n][1] for n in order],
            *[res[n][2] for n in order], *[res[n][3] for n in order])
```

```python
import math

import numpy as np
import jax
import jax.numpy as jnp
from jax import lax
from jax.experimental import pallas as pl
from jax.experimental.pallas import tpu as pltpu

F32 = jnp.float32
BF16 = jnp.bfloat16

D_MODEL = 1024
ATT_GROUPS = ((128, 1), (512, 4), (2048, 16))
ATT_BLOCKS_PER_STEP = (16, 4, 2)
ATT_TOGETHER = 4
HPG = 4
ATT_HEADS = 12
DH = 128
BLK = 128
ATT_W = ATT_HEADS * DH
ATT_OUT_W = HPG * DH
RET_HEADS = 4
RET_QK = 256
RET_V = 512
RET_QK_W = RET_HEADS * RET_QK
RET_V_W = RET_HEADS * RET_V
CHUNK = 128
RET_PER_STEP = 4
D_FF = 4096
IN_W = 12800
REST_W = IN_W - 3 * ATT_W
EPS = 1e-6
ADAM_LR, ADAM_B1, ADAM_B2, ADAM_EPS, ADAM_WD, ADAM_STEP = 0.001, 0.9, 0.999, 1e-08, 0.01, 10
N_DEV = 8
N_CHIP = 4
MESH_AXES = ("x", "y", "c")
MESH = pl.DeviceIdType.MESH
VMEM_LIMIT = 56 * 1024 * 1024
LANES = 128
NEG = -1e30

_NN = (((1,), (0,)), ((), ()))
_NT = (((1,), (1,)), ((), ()))
_TN = (((0,), (0,)), ((), ()))

R_Q, R_K, R_V, R_G, R_GA, R_GB = 0, 1024, 2048, 4096, 6144, 7168

LOG_GAMMA = [float(v) for v in np.log(1.0 - 2.0 ** (-5.0 - np.arange(RET_HEADS, dtype=np.float32))).astype(np.float32)]
ALIBI = np.asarray(2.0 ** (-8.0 * np.arange(1, ATT_HEADS + 1, dtype=np.float32) / ATT_HEADS), np.float32)

SMALL_ROWS = (1024 + 1536 + 1536 + 2048 + 2048 + 1024) // LANES


def _dot(a, b, dims=_NN):
    return lax.dot_general(a, b, dims, preferred_element_type=F32)


def _cparams(sem):
    return pltpu.CompilerParams(dimension_semantics=sem, vmem_limit_bytes=VMEM_LIMIT)


def _sds(shape, dtype):
    return jax.ShapeDtypeStruct(shape, dtype)


class _Exchange:
    def __init__(self, ins, out_shapes, sems, plan):
        self.ins, self.out_shapes, self.sems, self.plan = list(ins), list(out_shapes), list(sems), plan

    def split(self, in_refs, out_refs, sem_refs):
        return self.plan(in_refs, out_refs, sem_refs)


def _mm(name, a, b, mode, tm, tn, tk, outs, epi, extras=(), b_pro=None,
        sem=("parallel", "parallel", "arbitrary"), exchange=None, in_place=None):
    if mode == "nn":
        (M, K), (_, N) = a.shape, b.shape
        a_spec = pl.BlockSpec((tm, tk), lambda i, j, k: (i, k))
        dims = _NN
    else:
        (K, M), (_, N) = a.shape, b.shape
        a_spec = pl.BlockSpec((tk, tm), lambda i, j, k: (k, i))
        dims = _TN
    assert M % tm == 0 and N % tn == 0 and K % tk == 0, (name, M, N, K, tm, tn, tk)
    nk = K // tk
    whole_b = dict(pipeline_mode=pl.Buffered(1)) if (nk == 1 and N == tn) else {}
    b_spec = pl.BlockSpec((tk, tn), lambda i, j, k: (k, j), **whole_b)
    n_ex, n_out = len(extras), len(outs)
    grid = (M // tm, N // tn, nk)
    n_xi = len(exchange.ins) if exchange else 0
    n_xo = len(exchange.out_shapes) if exchange else 0
    n_acc = 1 if nk > 1 else 0

    n_ip = 1 if in_place else 0

    def body(a_ref, b_ref, *rest):
        ex, rest = rest[:n_ex], rest[n_ex:]
        x_in, rest = rest[:n_xi], rest[n_xi + n_ip:]
        out, rest = rest[:n_out], rest[n_out:]
        x_out, rest = rest[:n_xo], rest[n_xo:]
        step =(pl.program_id(0) * grid[1] + pl.program_id(1)) * grid[2] + pl.program_id(2)
        n_steps = grid[0] * grid[1] * grid[2]
        if exchange:
            start, relay, finish = exchange.split(x_in, x_out, rest[n_acc:])
            pl.when(step == 0)(start)
        bv = b_ref[...]
        if b_pro is not None:
            bv = b_pro(bv)
        part = _dot(a_ref[...].astype(BF16), bv.astype(BF16), dims)
        if nk == 1:
            epi(part, ex, out)
        else:
            acc_ref = rest[0]
            k = pl.program_id(2)

            @pl.when(k == 0)
            def _():
                acc_ref[...] = part

            @pl.when(k > 0)
            def _():
                acc_ref[...] += part

            @pl.when(k == nk - 1)
            def _():
                epi(acc_ref[...], ex, out)
        if exchange:
            if relay is not None:
                pl.when(step == (7 * n_steps) // 8)(relay)
            pl.when(step == n_steps - 1)(finish)

    res = pl.pallas_call(
        body,
        name=name,
        grid=grid,
        in_specs=[a_spec, b_spec] + [s for _, s in extras] + [_ANY] * (n_xi + n_ip),
        out_specs=[s for _, s in outs] + [_ANY] * n_xo,
        out_shape=[o for o, _ in outs] + (exchange.out_shapes if exchange else []),
        scratch_shapes=([pltpu.VMEM((tm, tn), F32)] if nk > 1 else []) + (exchange.sems if exchange else []),
        input_output_aliases={2 + n_ex + n_xi: in_place[1]} if in_place else {},
        compiler_params=_cparams(("arbitrary",) * 3 if exchange else sem),
    )(a, b, *[e for e, _ in extras], *(exchange.ins if exchange else []), *([in_place[0]] if in_place else []))
    return (res[:n_out], res[n_out:]) if exchange else res


def _tile_ij(tm, tn):
    return pl.BlockSpec((tm, tn), lambda i, j, k: (i, j))


def _epi_store(dtype):
    def epi(acc, ex, out):
        out[0][...] = acc.astype(dtype)
    return epi


def _rms_rows(x):
    return lax.rsqrt(jnp.mean(x * x, axis=-1, keepdims=True) + EPS)


def _acc_rows(ref, part, first):
    @pl.when(first)
    def _():
        ref[...] = part

    @pl.when(jnp.logical_not(first))
    def _():
        ref[...] += part


def _rmsnorm_fwd(x, g, tm, exchange):
    S, Dm = x.shape
    n_steps = S // tm
    n_xi, n_xo = len(exchange.ins), len(exchange.out_shapes)

    def body(x_ref, g_ref, *rest):
        x_in, o_ref, x_out, sems = rest[:n_xi], rest[n_xi], rest[n_xi + 1:n_xi + 1 + n_xo], rest[n_xi + 1 + n_xo:]
        start, relay, finish = exchange.split(x_in, x_out, sems)
        step = pl.program_id(0)
        pl.when(step == 0)(start)
        xv = x_ref[...]
        o_ref[...] = (xv * _rms_rows(xv) * g_ref[...]).astype(BF16)

        @pl.when(step == n_steps - 1)
        def _():
            relay()
            finish()

    res = pl.pallas_call(
        body, name="rmsnorm1_fwd", grid=(n_steps,),
        in_specs=[pl.BlockSpec((tm, Dm), lambda i: (i, 0)), pl.BlockSpec((1, Dm), lambda i: (0, 0))] + [_ANY] * n_xi,
        out_specs=[pl.BlockSpec((tm, Dm), lambda i: (i, 0))] + [_ANY] * n_xo,
        out_shape=[_sds((S, Dm), BF16)] + exchange.out_shapes,
        scratch_shapes=exchange.sems,
        compiler_params=_cparams(("arbitrary",)),
    )(x, g, *exchange.ins)
    return res[0], res[1:]


def _rows(ref, r, b, d):
    if d == 1:
        return ref[b * BLK:(b + 1) * BLK, :]
    return ref[pl.ds(b * BLK * d + r, BLK, stride=d), :]


def _put_rows(ref, r, b, d, val):
    if d == 1:
        ref[b * BLK:(b + 1) * BLK, :] = val
    else:
        ref[pl.ds(b * BLK * d + r, BLK, stride=d), :] = val


def _head_norm(x, g):
    r = _rms_rows(x)
    xh = x * r
    return (xh * g).astype(BF16), xh, r


def _head_norm_bwd(dyn, xh, r, g):
    dxh = dyn * g
    dx = r * (dxh - xh * jnp.mean(dxh * xh, axis=-1, keepdims=True))
    return dx, jnp.sum(dyn * xh, axis=0, keepdims=True)


def _att_mask_bias(slope, d, first):
    qi = lax.broadcasted_iota(jnp.int32, (BLK, 2 * BLK), 0)
    kj = lax.broadcasted_iota(jnp.int32, (BLK, 2 * BLK), 1)
    dist = BLK + qi - kj
    valid = (dist >= 0) & (dist <= BLK)
    if first is not None:
        valid = valid & (jnp.logical_not(first) | (kj >= BLK))
    bias = -slope * (dist * d).astype(F32)
    return valid, bias


def _att_specs(gi, d, nb, S):
    span = BLK * d
    sb = span * nb
    nspan = S // span
    before = lambda n: jnp.maximum(n * nb - 1, 0)
    after = lambda n: jnp.minimum((n + 1) * nb, nspan - 1)
    zcol = lambda j, kind: 3 * (gi * HPG + j) + kind
    cur = lambda kind: pl.BlockSpec((sb, DH), lambda j, n: (n, zcol(j, kind)))
    prev = lambda kind: pl.BlockSpec((span, DH), lambda j, n: (before(n), zcol(j, kind)))
    nxt = lambda kind: pl.BlockSpec((span, DH), lambda j, n: (after(n), zcol(j, kind)))
    slot = pl.BlockSpec((sb, DH), lambda j, n: (n, j))
    slot_next = pl.BlockSpec((span, DH), lambda j, n: (after(n), j))
    head = pl.BlockSpec((None, 1, DH), lambda j, n: (gi * HPG + j, 0, 0))
    return cur, prev, nxt, slot, slot_next, head


def _att_fwd_group(z_att, gq3, gk3, slopes3, gi, d, nb, others=()):
    S = z_att.shape[0]
    nsb = S // (BLK * d * nb)
    scale = DH ** -0.5
    n_other = len(others)

    def body(q_ref, k_ref, v_ref, kp_ref, vp_ref, gq_ref, gk_ref, sl_ref, *rest):
        other_refs, (o_ref, l_ref) = rest[:2 * n_other], rest[2 * n_other:]
        slope = sl_ref[...][:, :1]
        valid0, bias = _att_mask_bias(slope, d, pl.program_id(1) == 0)
        valid_in, _ = _att_mask_bias(slope, d, None)
        gq, gk = gq_ref[...], gk_ref[...]
        memo = {}

        def get(kind, r, b):
            if (kind, r, b) not in memo:
                if kind == "k":
                    val = _head_norm(_rows(k_ref if b >= 0 else kp_ref, r, max(b, 0), d), gk)[0]
                else:
                    val = _rows(v_ref if b >= 0 else vp_ref, r, max(b, 0), d).astype(BF16)
                memo[(kind, r, b)] = val
            return memo[(kind, r, b)]

        units = [(r, b) for r in range(d) for b in range(nb)]
        for c0 in range(0, len(units), ATT_TOGETHER):
            us = units[c0:c0 + ATT_TOGETHER]
            q = [_head_norm(_rows(q_ref, r, b, d), gq)[0] for r, b in us]
            k2 = [jnp.concatenate([get("k", r, b - 1), get("k", r, b)], axis=0) for r, b in us]
            v2 = [jnp.concatenate([get("v", r, b - 1), get("v", r, b)], axis=0) for r, b in us]
            s = [jnp.where(valid0 if b == 0 else valid_in, _dot(q[i], k2[i], _NT) * scale + bias, NEG)
                 for i, (r, b) in enumerate(us)]
            m = [jnp.max(si, axis=-1, keepdims=True) for si in s]
            p = [jnp.exp(si - mi) for si, mi in zip(s, m)]
            den = [jnp.sum(pi, axis=-1, keepdims=True) for pi in p]
            o = [_dot(pi.astype(BF16), vi) / di for pi, vi, di in zip(p, v2, den)]
            for i, (r, b) in enumerate(us):
                _put_rows(o_ref, r, b, d, o[i])
                _put_rows(l_ref, r, b, d, jnp.broadcast_to(m[i] + jnp.log(den[i]), (BLK, DH)))
        if n_other:
            os_ = [ref[...] for ref in other_refs[:n_other]] + [o_ref[...]]
            ls_ = [ref[...] for ref in other_refs[n_other:]] + [l_ref[...]]
            m = ls_[0]
            for l in ls_[1:]:
                m = jnp.maximum(m, l)
            es = [jnp.exp(l - m) for l in ls_]
            tot, mix = es[0], es[0] * os_[0]
            for e, o in zip(es[1:], os_[1:]):
                tot, mix = tot + e, mix + e * o
            o_ref[...] = mix / tot
            l_ref[...] = m + jnp.log(tot)

    cur, prev, _, slot, _, head = _att_specs(gi, d, nb, S)
    return pl.pallas_call(
        body, name=f"att_fwd_g{gi}", grid=(HPG, nsb),
        in_specs=[cur(0), cur(1), cur(2), prev(1), prev(2), head, head, head] + [slot] * (2 * n_other),
        out_specs=[slot, slot],
        out_shape=[_sds((S, ATT_OUT_W), F32), _sds((S, ATT_OUT_W), F32)],
        compiler_params=_cparams(("parallel", "arbitrary")),
    )(z_att, z_att, z_att, z_att, z_att, gq3, gk3, slopes3, *[o for o, _ in others], *[l for _, l in others])


def _ret_tables(lg):
    ri = lax.broadcasted_iota(jnp.int32, (CHUNK, CHUNK), 0)
    ci = lax.broadcasted_iota(jnp.int32, (CHUNK, CHUNK), 1)
    diff = (ri - ci).astype(F32)
    decay = jnp.where(diff >= 0, jnp.exp(lg * jnp.maximum(diff, 0.0)), 0.0)
    idx = lax.broadcasted_iota(jnp.int32, (CHUNK, 1), 0).astype(F32)
    xi = jnp.exp(lg * (idx + 1.0))
    zeta = jnp.exp(lg * (CHUNK - 1.0 - idx))
    return decay, xi, zeta, math.exp(lg * CHUNK)


def _ret_specs(nsteps, rev):
    idx = (lambda n: nsteps - 1 - n) if rev else (lambda n: n)
    rows = CHUNK * RET_PER_STEP
    qk = lambda off: pl.BlockSpec((rows, RET_QK_W), lambda n: (idx(n), off // RET_QK_W))
    vv = lambda off: pl.BlockSpec((rows, RET_V_W), lambda n: (idx(n), off // RET_V_W))
    par = pl.BlockSpec((1, RET_V_W), lambda n: (0, 0))
    wide = pl.BlockSpec((rows, RET_V_W), lambda n: (idx(n), 0))
    st = pl.BlockSpec((RET_HEADS, RET_PER_STEP, RET_QK, RET_V), lambda n: (0, idx(n), 0, 0))
    return qk, vv, par, wide, st


def _ret_fwd(z_rest, gn_g, gn_b):
    S = z_rest.shape[0]
    nch = S // CHUNK
    nsteps = nch // RET_PER_STEP

    def body(q_ref, k_ref, v_ref, gr_ref, g_ref, b_ref, or_ref, o_ref, st_ref, state):
        @pl.when(pl.program_id(0) == 0)
        def _():
            state[...] = jnp.zeros_like(state)

        for c in range(RET_PER_STEP):
            rc = slice(c * CHUNK, (c + 1) * CHUNK)
            for h in range(RET_HEADS):
                decay, xi, zeta, gch = _ret_tables(LOG_GAMMA[h])
                cq = slice(h * RET_QK, (h + 1) * RET_QK)
                cv = slice(h * RET_V, (h + 1) * RET_V)
                q = q_ref[rc, cq]
                kc32 = k_ref[rc, cq].astype(F32) * (RET_QK ** -0.5)
                kc = kc32.astype(BF16)
                v = v_ref[rc, cv]
                st = state[h]
                stb = st.astype(BF16)
                st_ref[h, c] = stb
                s = _dot(q, kc, _NT) * decay
                o = _dot(s.astype(BF16), v) + _dot(q, stb) * xi
                state[h] = st * gch + _dot((kc32 * zeta).astype(BF16), v, _TN)
                mu = jnp.mean(o, axis=-1, keepdims=True)
                cen = o - mu
                yh = cen * lax.rsqrt(jnp.mean(cen * cen, axis=-1, keepdims=True) + EPS)
                gr = gr_ref[rc, cv].astype(F32)
                or_ref[rc, cv] = ((yh * g_ref[:, cv] + b_ref[:, cv]) * (gr * jax.nn.sigmoid(gr))).astype(BF16)
                o_ref[rc, cv] = o

    qk, vv, par, wide, st = _ret_specs(nsteps, False)
    return pl.pallas_call(
        body, name="ret_fwd", grid=(nsteps,),
        in_specs=[qk(R_Q), qk(R_K), vv(R_V), vv(R_G), par, par],
        out_specs=[wide, wide, st],
        out_shape=[_sds((S, RET_V_W), BF16), _sds((S, RET_V_W), F32), _sds((RET_HEADS, nch, RET_QK, RET_V), BF16)],
        scratch_shapes=[pltpu.VMEM((RET_HEADS, RET_QK, RET_V), F32)],
        compiler_params=_cparams(("arbitrary",)),
    )(z_rest, z_rest, z_rest, z_rest, gn_g, gn_b)


def _merge_fwd(o_a, o_r, z_rest, x, wpa, wpb, wout, g2, tm):
    S = x.shape[0]

    def body(oa_ref, or_ref, ga_ref, gb_ref, x_ref, wpa_ref, wpb_ref, wo_ref, g2_ref,
             x1_ref, y_ref, pa_ref, pb_ref, xn2_ref):
        pa = _dot(oa_ref[...].astype(BF16), wpa_ref[...])
        pb = _dot(or_ref[...], wpb_ref[...])
        y = jax.nn.sigmoid(ga_ref[...].astype(F32)) * pa + jax.nn.sigmoid(gb_ref[...].astype(F32)) * pb
        yb = y.astype(BF16)
        x1 = x_ref[...] + _dot(yb, wo_ref[...])
        x1_ref[...] = x1
        y_ref[...] = yb
        pa_ref[...] = pa.astype(BF16)
        pb_ref[...] = pb.astype(BF16)
        xn2_ref[...] = (x1 * _rms_rows(x1) * g2_ref[...]).astype(BF16)

    row = lambda w: pl.BlockSpec((tm, w), lambda i: (i, 0))
    full = lambda a: pl.BlockSpec(a.shape, lambda i: (0, 0))
    return pl.pallas_call(
        body, name="merge_fwd", grid=(S // tm,),
        in_specs=[row(ATT_OUT_W), row(RET_V_W),
                  pl.BlockSpec((tm, D_MODEL), lambda i: (i, R_GA // D_MODEL)),
                  pl.BlockSpec((tm, D_MODEL), lambda i: (i, R_GB // D_MODEL)),
                  row(D_MODEL), full(wpa), full(wpb), full(wout), full(g2)],
        out_specs=[row(D_MODEL)] * 5,
        out_shape=[_sds((S, D_MODEL), F32)] + [_sds((S, D_MODEL), BF16)] * 4,
        compiler_params=_cparams(("parallel",)),
    )(o_a, o_r, z_rest, z_rest, x, wpa, wpb, wout, g2)


def _rms_bwd(dy, xv, g):
    r = _rms_rows(xv)
    xh = xv * r
    dg = dy * g
    dx = r * (dg - xh * jnp.mean(dg * xh, axis=-1, keepdims=True))
    return dx, jnp.sum(dy * xh, axis=0, keepdims=True)


def _norm_gate_bwd(dout, o, gr, gam, bet):
    mean = lambda xs: [jnp.mean(x, axis=-1, keepdims=True) for x in xs]
    cen = [x - m for x, m in zip(o, mean(o))]
    rstd = [lax.rsqrt(v + EPS) for v in mean([c * c for c in cen])]
    yh = [c * r for c, r in zip(cen, rstd)]
    y = [a * g + b for a, g, b in zip(yh, gam, bet)]
    sg = [jax.nn.sigmoid(g) for g in gr]
    dy = [d * (g * s) for d, g, s in zip(dout, gr, sg)]
    dgr = [d * a * (s * (1.0 + g * (1.0 - s))) for d, a, s, g in zip(dout, y, sg, gr)]
    dyh = [d * g for d, g in zip(dy, gam)]
    m1, m2 = mean(dyh), mean([a * b for a, b in zip(dyh, yh)])
    do = [r * (d - a - h * b) for r, d, a, h, b in zip(rstd, dyh, m1, yh, m2)]
    dg = [jnp.sum(d * h, axis=0, keepdims=True) for d, h in zip(dy, yh)]
    db = [jnp.sum(d, axis=0, keepdims=True) for d in dy]
    return do, dgr, dg, db


def _ret_bwd(do, states, z_rest, dz_rest):
    S = z_rest.shape[0]
    nsteps = S // CHUNK // RET_PER_STEP

    def body(do_ref, st_ref, q_ref, k_ref, v_ref, dz_in, dz_ref, gst):
        del dz_in

        @pl.when(pl.program_id(0) == 0)
        def _():
            gst[...] = jnp.zeros_like(gst)

        for c in reversed(range(RET_PER_STEP)):
            rc = slice(c * CHUNK, (c + 1) * CHUNK)
            for h in range(RET_HEADS):
                decay, xi, zeta, gch = _ret_tables(LOG_GAMMA[h])
                cq = slice(h * RET_QK, (h + 1) * RET_QK)
                cv = slice(h * RET_V, (h + 1) * RET_V)
                q = q_ref[rc, cq]
                kc32 = k_ref[rc, cq].astype(F32) * (RET_QK ** -0.5)
                kc = kc32.astype(BF16)
                v = v_ref[rc, cv]
                dob = do_ref[rc, cv]
                a = (_dot(q, kc, _NT) * decay).astype(BF16)
                da = (_dot(dob, v, _NT) * decay).astype(BF16)
                dcross = (dob.astype(F32) * xi).astype(BF16)
                g_next = gst[h]
                gb = g_next.astype(BF16)
                dq = _dot(da, kc) + _dot(dcross, st_ref[h, c], _NT)
                dkc = _dot(da, q, _TN)
                dkz = _dot(v, gb, _NT)
                dv = _dot(a, dob, _TN) + _dot((kc32 * zeta).astype(BF16), gb)
                gst[h] = g_next * gch + _dot(q, dcross, _TN)
                dz_ref[rc, R_Q + h * RET_QK:R_Q + (h + 1) * RET_QK] = dq.astype(BF16)
                dz_ref[rc, R_K + h * RET_QK:R_K + (h + 1) * RET_QK] = (
                    (dkc + dkz * zeta) * (RET_QK ** -0.5)).astype(BF16)
                dz_ref[rc, R_V + h * RET_V:R_V + (h + 1) * RET_V] = dv.astype(BF16)

    qk, vv, _, wide, st = _ret_specs(nsteps, True)
    return pl.pallas_call(
        body, name="ret_bwd", grid=(nsteps,),
        in_specs=[wide, st, qk(R_Q), qk(R_K), vv(R_V), _ANY],
        out_specs=pl.BlockSpec((CHUNK * RET_PER_STEP, R_G), lambda n: (nsteps - 1 - n, 0)),
        out_shape=_sds(dz_rest.shape, BF16),
        input_output_aliases={5: 0},
        scratch_shapes=[pltpu.VMEM((RET_HEADS, RET_QK, RET_V), F32)],
        compiler_params=_cparams(("arbitrary",)),
    )(do, states, z_rest, z_rest, z_rest, dz_rest)


def _att_probs(q, k, lmix, valid, bias):
    s = _dot(q, k, _NT) * (DH ** -0.5) + bias
    return jnp.where(valid, jnp.exp(jnp.where(valid, s, NEG) - lmix), 0.0)


def _att_bwd_group(z_att, dz_att, do_a, delta_lmix, gq3, gk3, slopes3, gi, d, nb, exchange=None):
    S = z_att.shape[0]
    sb = BLK * d * nb
    nsb = S // sb
    scale = DH ** -0.5
    aliased = dz_att is not None
    n_xi = len(exchange.ins) if exchange else 0
    n_xo = len(exchange.out_shapes) if exchange else 0

    def body(q_ref, k_ref, v_ref, kp_ref, vp_ref, qn_ref, do_ref, don_ref, dl_ref, dln_ref,
             gq_ref, gk_ref, sl_ref, *rest):
        rest = rest[1:] if aliased else rest
        x_in, (dz_ref, dgq_ref, dgk_ref), rest = rest[:n_xi], rest[n_xi:n_xi + 3], rest[n_xi + 3:]
        x_out, stage, x_sems = rest[:n_xo], rest[n_xo], rest[n_xo + 1:]
        n = pl.program_id(1)
        step = pl.program_id(0) * nsb + n
        if exchange:
            start, _, finish = exchange.split(x_in, x_out, x_sems)
            pl.when(step == 0)(start)
        slope = sl_ref[...][:, :1]
        valid0, bias = _att_mask_bias(slope, d, n == 0)
        valid_in, _ = _att_mask_bias(slope, d, None)
        qi = lax.broadcasted_iota(jnp.int32, (BLK, BLK), 0)
        kj = lax.broadcasted_iota(jnp.int32, (BLK, BLK), 1)
        dist_n = BLK + qi - kj
        valid_n_in = dist_n <= BLK
        valid_n_last = valid_n_in & (n < nsb - 1)
        bias_n = -slope * (dist_n * d).astype(F32)
        gq, gk = gq_ref[...], gk_ref[...]
        dgq = jnp.zeros((1, DH), F32)
        dgk = jnp.zeros((1, DH), F32)
        memo = {}

        def get(kind, r, b):
            if (kind, r, b) not in memo:
                inner = 0 <= b < nb
                bb = b if inner else 0
                if kind == "q":
                    val = _head_norm(_rows(q_ref if inner else qn_ref, r, bb, d), gq)
                elif kind == "k":
                    val = _head_norm(_rows(k_ref if inner else kp_ref, r, bb, d), gk)
                elif kind == "v":
                    val = _rows(v_ref if inner else vp_ref, r, bb, d).astype(BF16)
                elif kind == "do":
                    val = _rows(do_ref if inner else don_ref, r, bb, d).astype(BF16)
                elif kind == "dl+lm":
                    val = _rows(dl_ref if inner else dln_ref, r, bb, d)
                elif kind == "dl":
                    val = get("dl+lm", r, b)[:, :1]
                else:
                    val = get("dl+lm", r, b)[:, DH // 2:DH // 2 + 1]
                memo[(kind, r, b)] = val
            return memo[(kind, r, b)]

        units = [(r, b) for r in range(d) for b in range(nb)]
        for c0 in range(0, len(units), ATT_TOGETHER):
            us = units[c0:c0 + ATT_TOGETHER]
            k2 = [jnp.concatenate([get("k", r, b - 1)[0], get("k", r, b)[0]], axis=0) for r, b in us]
            v2 = [jnp.concatenate([get("v", r, b - 1), get("v", r, b)], axis=0) for r, b in us]
            p = [_att_probs(get("q", r, b)[0], k2[i], get("lm", r, b), valid0 if b == 0 else valid_in, bias)
                 for i, (r, b) in enumerate(us)]
            dp = [_dot(get("do", r, b), v2[i], _NT) for i, (r, b) in enumerate(us)]
            ds = [(p[i] * (dp[i] - get("dl", r, b)) * scale).astype(BF16) for i, (r, b) in enumerate(us)]
            dq = [_dot(ds[i], k2[i]) for i in range(len(us))]
            p_n = [_att_probs(get("q", r, b + 1)[0], get("k", r, b)[0], get("lm", r, b + 1),
                              valid_n_last if b == nb - 1 else valid_n_in, bias_n) for r, b in us]
            dp_n = [_dot(get("do", r, b + 1), get("v", r, b), _NT) for r, b in us]
            ds_n = [(p_n[i] * (dp_n[i] - get("dl", r, b + 1)) * scale).astype(BF16) for i, (r, b) in enumerate(us)]
            dk = [_dot(ds[i][:, BLK:], get("q", r, b)[0], _TN) + _dot(ds_n[i], get("q", r, b + 1)[0], _TN)
                  for i, (r, b) in enumerate(us)]
            dv = [_dot(p[i][:, BLK:].astype(BF16), get("do", r, b), _TN)
                  + _dot(p_n[i].astype(BF16), get("do", r, b + 1), _TN) for i, (r, b) in enumerate(us)]
            for i, (r, b) in enumerate(us):
                _, qh, qr = get("q", r, b)
                _, kh, kr = get("k", r, b)
                dxq, dg_q = _head_norm_bwd(dq[i], qh, qr, gq)
                dxk, dg_k = _head_norm_bwd(dk[i], kh, kr, gk)
                _put_rows(stage.at[0], r, b, d, dxq)
                _put_rows(stage.at[1], r, b, d, dxk)
                _put_rows(stage.at[2], r, b, d, dv[i])
                dgq, dgk = dgq + dg_q, dgk + dg_k
        for kind in range(3):
            dz_ref[:, kind * DH:(kind + 1) * DH] = stage[kind].astype(BF16)
        _acc_rows(dgq_ref, dgq, n == 0)
        _acc_rows(dgk_ref, dgk, n == 0)
        if exchange:
            pl.when(step == HPG * nsb - 1)(finish)

    cur, prev, nxt, slot, slot_next, head = _att_specs(gi, d, nb, S)
    gain = pl.BlockSpec((None, 1, DH), lambda j, n: (j, 0, 0))
    in_specs = [cur(0), cur(1), cur(2), prev(1), prev(2), nxt(0),
                slot, slot_next, slot, slot_next, head, head, head]
    args = [z_att] * 6 + [do_a, do_a, delta_lmix, delta_lmix, gq3, gk3, slopes3]
    if aliased:
        in_specs.append(_ANY)
        args.append(dz_att)
    res = pl.pallas_call(
        body, name=f"att_bwd_g{gi}", grid=(HPG, nsb),
        in_specs=in_specs + [_ANY] * n_xi,
        out_specs=[pl.BlockSpec((sb, 3 * DH), lambda j, n: (n, gi * HPG + j)), gain, gain] + [_ANY] * n_xo,
        out_shape=[_sds(z_att.shape, BF16), _sds((HPG, 1, DH), F32), _sds((HPG, 1, DH), F32)]
        + (exchange.out_shapes if exchange else []),
        input_output_aliases={len(args) - 1: 0} if aliased else {},
        scratch_shapes=[pltpu.VMEM((3, sb, DH), F32)] + (exchange.sems if exchange else []),
        compiler_params=_cparams(("arbitrary", "arbitrary") if exchange else ("parallel", "arbitrary")),
    )(*args, *(exchange.ins if exchange else []))
    return res[:3], res[3:]


def _step(x, target, norm1_g, q_norm_g, k_norm_g, gn_g, gn_b, norm2_g, shards, core):
    S = x.shape[0]
    tm = min(512, S)
    ts = min(256, S)
    tk = min(2048, S)
    tk2 = min(4096, S)
    later_shards = shards[1:]
    gq3 = q_norm_g.reshape(ATT_HEADS, 1, DH)
    gk3 = k_norm_g.reshape(ATT_HEADS, 1, DH)
    slopes3 = jnp.asarray(np.broadcast_to(ALIBI[:, None, None], (ATT_HEADS, 1, DH)).copy())

    xn, (g_in,) = _rmsnorm_fwd(x, norm1_g, tm, _gather_exchange(shards[:1]))
    w_in = _cols(g_in)
    w_att = w_in[:, :3 * ATT_W].reshape(D_MODEL, 3, ATT_HEADS, DH).transpose(0, 2, 1, 3).reshape(D_MODEL, 3 * ATT_W)
    w_rest = w_in[:, 3 * ATT_W:]
    w_att_t, w_rest_t = w_att.T, w_rest.T
    z_att = _mm("in_proj_att", xn, w_att, "nn", tm, 3 * ATT_W, D_MODEL,
                [(_sds((S, 3 * ATT_W), F32), _tile_ij(tm, 3 * ATT_W))], _epi_store(F32))[0]
    (z_rest,), gathered = _mm("in_proj_rest", xn, w_rest, "nn", tm, REST_W, D_MODEL,
                              [(_sds((S, REST_W), BF16), _tile_ij(tm, REST_W))], _epi_store(BF16),
                              exchange=_gather_exchange(later_shards))
    w_pa, w_pb, w_out, w_up, w_down = [f(g) for f, g in zip((_cols, _rows_of, _rows_of, _cols, _rows_of), gathered)]
    w_up_t, w_down_t, w_out_t, w_pa_t, w_pb_t = w_up.T, w_down.T, w_out.T, w_pa.T, w_pb.T
    done = []
    for gi, ((_, d), nb) in enumerate(zip(ATT_GROUPS, ATT_BLOCKS_PER_STEP)):
        last = gi == len(ATT_GROUPS) - 1
        done.append(_att_fwd_group(z_att, gq3, gk3, slopes3, gi, d, nb, others=tuple(done) if last else ()))
    o_a, lmix = done[-1]
    o_r, o_pre, states = _ret_fwd(z_rest, gn_g, gn_b)
    x1, y, pa, pb, xn2 = _merge_fwd(o_a, o_r, z_rest, x, w_pa, w_pb, w_out, norm2_g, tm)

    def epi_up(acc, ex, out):
        r = jnp.maximum(acc, 0.0)
        out[0][...] = (r * r).astype(BF16)
        out[1][...] = r.astype(BF16)

    h, relu_u = _mm("mlp_up", xn2, w_up, "nn", tm, D_FF, D_MODEL,
                    [(_sds((S, D_FF), BF16), _tile_ij(tm, D_FF)), (_sds((S, D_FF), BF16), _tile_ij(tm, D_FF))], epi_up)

    def epi_down(acc, ex, out):
        diff = ex[0][...] + acc - ex[1][...]
        out[0][...] = diff * (1.0 / D_MODEL)
        out[1][...] = jnp.broadcast_to(jnp.sum(diff * diff) * (1.0 / (8 * LANES)), (8, LANES))

    row_tile = _tile_ij(tm, D_MODEL)
    dx2, loss_parts = _mm(
        "mlp_down_loss", h, w_down, "nn", tm, D_MODEL, D_FF,
        [(_sds((S, D_MODEL), F32), row_tile),
         (_sds((S // tm * 8, LANES), F32), pl.BlockSpec((8, LANES), lambda i, j, k: (i, 0)))],
        epi_down, extras=[(x1, row_tile), (target, row_tile)])
    loss_local = jnp.sum(loss_parts) * (0.5 / D_MODEL)

    def epi_du(acc, ex, out):
        out[0][...] = (acc * (2.0 * ex[0][...].astype(F32))).astype(BF16)

    du = _mm("mlp_down_bwd", dx2, w_down_t, "nn", tm, D_FF, D_MODEL,
             [(_sds((S, D_FF), BF16), _tile_ij(tm, D_FF))], epi_du, extras=[(relu_u, _tile_ij(tm, D_FF))])[0]
    gw_down = _mm("gw_down", h, dx2, "tn", 1024, D_MODEL, tk,
                  [(_sds((D_FF, D_MODEL), BF16), _tile_ij(1024, D_MODEL))], _epi_store(BF16))[0]
    gw_up = _mm("gw_up", xn2, du, "tn", D_MODEL, 512, tk2,
                [(_sds((N_DEV, D_MODEL, 512), BF16), pl.BlockSpec((None, D_MODEL, 512), lambda i, j, k: (j, 0, 0)))],
                _epi_store(BF16))[0]

    vec = pl.BlockSpec((1, D_MODEL), lambda i, j, k: (0, 0))
    seq_sem = ("arbitrary", "arbitrary", "arbitrary")

    def epi_norm2(acc, ex, out):
        dx, dg = _rms_bwd(acc, ex[0][...], ex[2][...])
        out[0][...] = ex[1][...] + dx
        _acc_rows(out[1], dg, pl.program_id(0) == 0)

    dx1, g_norm2 = _mm(
        "mlp_up_bwd", du, w_up_t, "nn", tm, D_MODEL, D_FF,
        [(_sds((S, D_MODEL), F32), row_tile), (_sds((1, D_MODEL), F32), vec)],
        epi_norm2, extras=[(x1, row_tile), (dx2, row_tile), (norm2_g, vec)], sem=seq_sem)

    def epi_dy(acc, ex, out):
        sa = jax.nn.sigmoid(ex[0][...].astype(F32))
        sb = jax.nn.sigmoid(ex[1][...].astype(F32))
        out[0][...] = (acc * sa).astype(BF16)
        out[1][...] = (acc * sb).astype(BF16)
        out[2][:, :D_MODEL] = (acc * ex[2][...].astype(F32) * (sa * (1.0 - sa))).astype(BF16)
        out[2][:, D_MODEL:] = (acc * ex[3][...].astype(F32) * (sb * (1.0 - sb))).astype(BF16)

    ga_spec = pl.BlockSpec((tm, D_MODEL), lambda i, j, k: (i, R_GA // D_MODEL))
    gb_spec = pl.BlockSpec((tm, D_MODEL), lambda i, j, k: (i, R_GB // D_MODEL))
    gates_spec = pl.BlockSpec((tm, 2 * D_MODEL), lambda i, j, k: (i, R_GA // (2 * D_MODEL)))
    dpa, dpb, dz_rest = _mm(
        "out_proj_bwd", dx1, w_out_t, "nn", tm, D_MODEL, D_MODEL,
        [(_sds((S, D_MODEL), BF16), row_tile), (_sds((S, D_MODEL), BF16), row_tile), (_sds((S, REST_W), BF16), gates_spec)],
        epi_dy, extras=[(z_rest, ga_spec), (z_rest, gb_spec), (pa, row_tile), (pb, row_tile)])
    gw_out = _mm("gw_out", y, dx1, "tn", D_MODEL, D_MODEL, tk,
                 [(_sds((D_MODEL, D_MODEL), BF16), _tile_ij(D_MODEL, D_MODEL))], _epi_store(BF16))[0]
    gw_pa = _mm("gw_proj_a", o_a, dpa, "tn", ATT_OUT_W, D_MODEL, tk,
                [(_sds((ATT_OUT_W, D_MODEL), BF16), _tile_ij(ATT_OUT_W, D_MODEL))], _epi_store(BF16))[0]
    gw_pb = _mm("gw_proj_b", o_r, dpb, "tn", 1024, D_MODEL, tk2,
                [(_sds((RET_V_W, D_MODEL), BF16), _tile_ij(1024, D_MODEL))], _epi_store(BF16))[0]

    def epi_doa(acc, ex, out):
        out[0][...] = acc
        prod = acc * ex[0][...]
        delta = jnp.concatenate(
            [jnp.broadcast_to(jnp.sum(prod[:, j * DH:(j + 1) * DH], axis=-1, keepdims=True), (prod.shape[0], DH))
             for j in range(HPG)], axis=1)
        lane = lax.broadcasted_iota(jnp.int32, delta.shape, 1)
        out[1][...] = jnp.where(lane % DH < DH // 2, delta, ex[1][...])

    slot_tile = _tile_ij(tm, ATT_OUT_W)
    do_a, delta_lmix = _mm("proj_a_bwd", dpa, w_pa_t, "nn", tm, ATT_OUT_W, D_MODEL,
                           [(_sds((S, ATT_OUT_W), F32), slot_tile), (_sds((S, ATT_OUT_W), F32), slot_tile)],
                           epi_doa, extras=[(o_a, slot_tile), (lmix, slot_tile)])
    owed = [_chip_core(_by_owner_cols(gw_pa)), _chip_core(gw_pb.reshape(N_DEV, -1, D_MODEL)),
            _chip_core(gw_out.reshape(N_DEV, -1, D_MODEL)), _chip_core(gw_up), _chip_core(gw_down.reshape(N_DEV, -1, D_MODEL))]
    names = ("w_proj_a", "w_proj_b", "w_out", "w_up", "w_down")

    def epi_dor(acc, ex, out):
        cvs = [slice(h * RET_V, (h + 1) * RET_V) for h in range(RET_HEADS)]
        do, dgr, dg, db = _norm_gate_bwd([acc[:, cv] for cv in cvs], [ex[0][:, cv] for cv in cvs],
                                         [ex[1][:, cv].astype(F32) for cv in cvs],
                                         [ex[2][:, cv] for cv in cvs], [ex[3][:, cv] for cv in cvs])
        for h, cv in enumerate(cvs):
            out[0][:, cv] = do[h].astype(BF16)
            out[1][:, cv] = dgr[h].astype(BF16)
        first = pl.program_id(0) == 0
        _acc_rows(out[2], jnp.concatenate(dg, axis=1), first)
        _acc_rows(out[3], jnp.concatenate(db, axis=1), first)

    wide_tile = _tile_ij(tm, RET_V_W)
    gate_tile = pl.BlockSpec((tm, RET_V_W), lambda i, j, k: (i, R_G // RET_V_W))
    wide_vec = pl.BlockSpec((1, RET_V_W), lambda i, j, k: (0, 0))
    (do_ret, dz_rest, g_gn_g, g_gn_b), got = _mm(
        "proj_b_bwd", dpb, w_pb_t, "nn", tm, RET_V_W, D_MODEL,
        [(_sds((S, RET_V_W), BF16), wide_tile), (_sds((S, REST_W), BF16), gate_tile),
         (_sds((1, RET_V_W), F32), wide_vec), (_sds((1, RET_V_W), F32), wide_vec)],
        epi_dor, extras=[(o_pre, wide_tile), (z_rest, gate_tile), (gn_g, wide_vec), (gn_b, wide_vec)],
        exchange=_pair_exchange(owed), in_place=(dz_rest, 1))
    chip_sums = [_pair_sum(f"pair_sum_{n}", g, r, core, min(256, g.shape[2])) for n, g, r in zip(names, owed, got)]

    dz_rest = _ret_bwd(do_ret, states, z_rest, dz_rest)
    dz_att, gq_parts, gk_parts, parts_late = None, [], [], None
    for gi, ((_, d), nb) in enumerate(zip(ATT_GROUPS, ATT_BLOCKS_PER_STEP)):
        last = gi == len(ATT_GROUPS) - 1
        (dz_att, gq_p, gk_p), parts = _att_bwd_group(z_att, dz_att, do_a, delta_lmix, gq3, gk3, slopes3, gi, d, nb,
                                                     exchange=_chip_exchange(chip_sums) if last else None)
        parts_late = parts if last else parts_late
        gq_parts.append(gq_p)
        gk_parts.append(gk_p)
    g_qn = jnp.concatenate(gq_parts, axis=0).reshape(1, ATT_HEADS, DH)
    g_kn = jnp.concatenate(gk_parts, axis=0).reshape(1, ATT_HEADS, DH)

    gw_att = _mm("gw_in_att", xn, dz_att, "tn", D_MODEL, ATT_W, tk,
                 [(_sds((D_MODEL, 3 * ATT_W), BF16), _tile_ij(D_MODEL, ATT_W))], _epi_store(BF16))[0]
    gw_rest = _mm("gw_in_rest", xn, dz_rest, "tn", D_MODEL, 1024, tk2,
                  [(_sds((D_MODEL, REST_W), BF16), _tile_ij(D_MODEL, 1024))], _epi_store(BF16))[0]
    gw_att = gw_att.reshape(D_MODEL, ATT_HEADS, 3, DH).transpose(0, 2, 1, 3).reshape(D_MODEL, 3 * ATT_W)
    gw_in = jnp.concatenate([gw_att, gw_rest], axis=1)
    owed_in = _chip_core(_by_owner_cols(gw_in))
    (dxn_att,), (got_in,) = _mm("in_proj_att_bwd", dz_att, w_att_t, "nn", tm, D_MODEL, 3 * ATT_W,
                                [(_sds((S, D_MODEL), F32), row_tile)], _epi_store(F32), exchange=_pair_exchange([owed_in]))
    chip_sum_in = _pair_sum("pair_sum_w_in", owed_in, got_in, core, min(256, owed_in.shape[2]))

    def epi_norm1(acc, ex, out):
        dx, dg = _rms_bwd(acc + ex[0][...], ex[1][...], ex[3][...])
        out[0][...] = ex[2][...] + dx
        _acc_rows(out[1], dg, pl.program_id(0) == 0)

    short_tile = _tile_ij(ts, D_MODEL)
    (grad_x, g_norm1), parts_in = _mm(
        "in_proj_rest_bwd", dz_rest, w_rest_t, "nn", ts, D_MODEL, REST_W,
        [(_sds((S, D_MODEL), F32), short_tile), (_sds((1, D_MODEL), F32), vec)],
        epi_norm1, extras=[(dxn_att, short_tile), (x, short_tile), (dx1, short_tile), (norm1_g, vec)],
        exchange=_chip_exchange([chip_sum_in]))

    small = (g_norm1, g_qn, g_kn, g_gn_g, g_gn_b, g_norm2)
    return loss_local, grad_x, list(parts_in) + list(parts_late), small


def _position():
    return lax.axis_index("x"), lax.axis_index("y"), lax.axis_index("c")


def _other_chips(x, y):
    return [(1 - x, y), (x, 1 - y), (1 - x, 1 - y)]


_ANY = pl.BlockSpec(memory_space=pl.ANY)


def _gather_exchange(shards):
    nw = len(shards)

    def plan(x_refs, out_refs, sems):
        send_sems, recv_sems, local_sems = sems
        x, y, c = _position()
        me, sibling = (x, y, c), (x, y, 1 - c)
        chips = _other_chips(x, y)

        def copy(w, k, block, to, own=False):
            px, py, pc = block
            rows = out_refs[w].at[4 * px + 2 * py + pc]
            return pltpu.make_async_remote_copy(
                src_ref=x_refs[w] if own else rows, dst_ref=rows,
                send_sem=send_sems.at[7 * w + k], recv_sem=recv_sems.at[7 * w + k], device_id=to, device_id_type=MESH)

        def mine(w):
            return pltpu.make_async_copy(x_refs[w], out_refs[w].at[4 * x + 2 * y + c], local_sems.at[w])

        def own_sends(w):
            return [copy(w, 0, me, sibling, own=True)] + [copy(w, 1 + j, me, (*chip, c), own=True)
                                                          for j, chip in enumerate(chips)]

        def start():
            for w in range(nw):
                mine(w).start()
                for cp in own_sends(w):
                    cp.start()

        def relay():
            for j, chip in enumerate(chips):
                for w in range(nw):
                    copy(w, 1 + j, (*chip, c), me).wait_recv()
                    copy(w, 4 + j, (*chip, c), sibling).start()

        def finish():
            for w in range(nw):
                copy(w, 0, sibling, me).wait_recv()
                for j, chip in enumerate(chips):
                    copy(w, 4 + j, (*chip, 1 - c), me).wait_recv()
            for w in range(nw):
                for cp in own_sends(w):
                    cp.wait_send()
                for j, chip in enumerate(chips):
                    copy(w, 4 + j, (*chip, c), sibling).wait_send()
                mine(w).wait()

        return start, relay, finish

    return _Exchange(shards, [_sds((N_DEV,) + s.shape, s.dtype) for s in shards],
                     [pltpu.SemaphoreType.DMA((7 * nw,)), pltpu.SemaphoreType.DMA((7 * nw,)),
                      pltpu.SemaphoreType.DMA((nw,))], plan)


def _run_exchange(name, exchange):
    n_in, n_out = len(exchange.ins), len(exchange.out_shapes)

    def body(*refs):
        start, relay, finish = exchange.split(refs[:n_in], refs[n_in:n_in + n_out], refs[n_in + n_out:])
        start()
        if relay is not None:
            relay()
        finish()

    return pl.pallas_call(
        body, name=name, out_shape=exchange.out_shapes,
        in_specs=[_ANY] * n_in, out_specs=[_ANY] * n_out, scratch_shapes=exchange.sems,
    )(*exchange.ins)


def _pair_exchange(grads):
    ng = len(grads)

    def plan(g_refs, out_refs, sems):
        send_sems, recv_sems = sems
        x, y, c = _position()

        def copies():
            return [pltpu.make_async_remote_copy(
                src_ref=g_refs[w].at[:, 1 - c], dst_ref=out_refs[w], send_sem=send_sems.at[w],
                recv_sem=recv_sems.at[w], device_id=(x, y, 1 - c), device_id_type=MESH) for w in range(ng)]

        def start():
            for cp in copies():
                cp.start()

        def finish():
            for cp in copies():
                cp.wait()

        return start, None, finish

    return _Exchange(grads, [_sds((N_CHIP,) + g.shape[2:], g.dtype) for g in grads],
                     [pltpu.SemaphoreType.DMA((ng,)), pltpu.SemaphoreType.DMA((ng,))], plan)


def _small_all_gather(small):
    def plan(in_refs, out_refs, sems):
        (s_ref,), (s_out,) = in_refs, out_refs
        send_sems, recv_sems, local_sem = sems
        x, y, c = _position()
        me_id = 4 * x + 2 * y + c
        flips = [(a, b, e) for a in (0, 1) for b in (0, 1) for e in (0, 1)][1:]
        peers = [(x ^ a, y ^ b, c ^ e) for a, b, e in flips]

        def start():
            pltpu.make_async_copy(s_ref, s_out.at[me_id], local_sem).start()
            for k, p in enumerate(peers):
                pltpu.make_async_remote_copy(
                    src_ref=s_ref, dst_ref=s_out.at[me_id], send_sem=send_sems.at[k], recv_sem=recv_sems.at[k],
                    device_id=p, device_id_type=MESH).start()

        def finish():
            for k, (px, py, pc) in enumerate(peers):
                pltpu.make_async_remote_copy(
                    src_ref=s_ref, dst_ref=s_out.at[4 * px + 2 * py + pc], send_sem=send_sems.at[k],
                    recv_sem=recv_sems.at[k], device_id=(px, py, pc), device_id_type=MESH).wait()
            pltpu.make_async_copy(s_ref, s_out.at[me_id], local_sem).wait()

        return start, None, finish

    return _run_exchange("small_grad_all_gather", _Exchange(
        [small], [_sds((N_DEV,) + small.shape, small.dtype)],
        [pltpu.SemaphoreType.DMA((7,)), pltpu.SemaphoreType.DMA((7,)), pltpu.SemaphoreType.DMA], plan))[0]


def _pair_sum(name, g, got, core, tr):
    n_chip, _, R, C = g.shape

    def body(c_ref, a_ref, b_ref, o_ref):
        del c_ref
        o_ref[...] = (a_ref[...].astype(F32) + b_ref[...].astype(F32)).astype(o_ref.dtype)

    return pl.pallas_call(
        body, name=name,
        grid_spec=pltpu.PrefetchScalarGridSpec(
            num_scalar_prefetch=1, grid=(n_chip, R // tr),
            in_specs=[pl.BlockSpec((None, None, tr, C), lambda ch, i, c_ref: (ch, c_ref[0], i, 0)),
                      pl.BlockSpec((None, tr, C), lambda ch, i, c_ref: (ch, i, 0))],
            out_specs=pl.BlockSpec((None, tr, C), lambda ch, i, c_ref: (ch, i, 0))),
        out_shape=_sds(got.shape, got.dtype),
        compiler_params=_cparams(("parallel", "parallel")),
    )(core, g, got)


def _chip_exchange(parts):
    ng = len(parts)

    def plan(p_refs, out_refs, sems):
        send_sems, recv_sems, local_sems = sems
        x, y, c = _position()
        my_chip = 2 * x + y

        def copies():
            local = [pltpu.make_async_copy(p_refs[w].at[my_chip], out_refs[w].at[my_chip], local_sems.at[w])
                     for w in range(ng)]
            remote = [pltpu.make_async_remote_copy(
                src_ref=p_refs[w].at[2 * cx + cy], dst_ref=out_refs[w].at[my_chip],
                send_sem=send_sems.at[3 * w + k], recv_sem=recv_sems.at[3 * w + k],
                device_id=(cx, cy, c), device_id_type=MESH)
                for w in range(ng) for k, (cx, cy) in enumerate(_other_chips(x, y))]
            return local + remote

        def start():
            for cp in copies():
                cp.start()

        def finish():
            for cp in copies():
                cp.wait()

        return start, None, finish

    return _Exchange(parts, [_sds(p.shape, p.dtype) for p in parts],
                     [pltpu.SemaphoreType.DMA((3 * ng,)), pltpu.SemaphoreType.DMA((3 * ng,)),
                      pltpu.SemaphoreType.DMA((ng,))], plan)


def _adamw(name, parts, w, m, v, tr):
    n_parts = parts.shape[0]
    R, C = w.shape

    def body(p_ref, w_ref, m_ref, v_ref, g_ref, d_ref, mo_ref, vo_ref):
        g = p_ref[0].astype(F32)
        for i in range(1, n_parts):
            g = g + p_ref[i].astype(F32)
        m_new = ADAM_B1 * m_ref[...] + (1.0 - ADAM_B1) * g
        v_new = ADAM_B2 * v_ref[...] + (1.0 - ADAM_B2) * (g * g)
        m_hat = m_new / (1.0 - ADAM_B1 ** ADAM_STEP)
        v_hat = v_new / (1.0 - ADAM_B2 ** ADAM_STEP)
        g_ref[...] = g
        d_ref[...] = -ADAM_LR * (m_hat / (jnp.sqrt(v_hat) + ADAM_EPS) + ADAM_WD * w_ref[...])
        mo_ref[...] = m_new
        vo_ref[...] = v_new

    tile = pl.BlockSpec((tr, C), lambda i: (i, 0))
    return pl.pallas_call(
        body, name=name, grid=(R // tr,),
        in_specs=[pl.BlockSpec((n_parts, tr, C), lambda i: (0, i, 0)), tile, tile, tile],
        out_specs=[tile] * 4,
        out_shape=[_sds((R, C), F32)] * 4,
        compiler_params=_cparams(("parallel",)),
    )(parts, w, m, v)


def _flat_small(arrs):
    return jnp.concatenate([a.reshape(-1) for a in arrs]).reshape(SMALL_ROWS, LANES)


def _cols(g):
    return g.transpose(1, 0, 2).reshape(g.shape[1], -1)


def _rows_of(g):
    return g.reshape(-1, g.shape[2])


def _by_owner_cols(g):
    rows, cols = g.shape
    return g.reshape(rows, N_DEV, cols // N_DEV).transpose(1, 0, 2)


def _chip_core(g):
    return g.reshape((N_CHIP, 2) + g.shape[1:])


def kernel(x, norm1_g, w_in, q_norm_g, k_norm_g, ret_gn_g, ret_gn_b, w_proj_a, w_proj_b, w_out, norm2_g, w_up, w_down, loss_target, m_norm1_g, m_w_in, m_q_norm_g, m_k_norm_g, m_ret_gn_g, m_ret_gn_b, m_w_proj_a, m_w_proj_b, m_w_out, m_norm2_g, m_w_up, m_w_down, v_norm1_g, v_w_in, v_q_norm_g, v_k_norm_g, v_ret_gn_g, v_ret_gn_b, v_w_proj_a, v_w_proj_b, v_w_out, v_norm2_g, v_w_up, v_w_down):
    big_w = (w_in, w_proj_a, w_proj_b, w_out, w_up, w_down)
    big_m = (m_w_in, m_w_proj_a, m_w_proj_b, m_w_out, m_w_up, m_w_down)
    big_v = (v_w_in, v_w_proj_a, v_w_proj_b, v_w_out, v_w_up, v_w_down)
    small_w = (norm1_g, q_norm_g, k_norm_g, ret_gn_g, ret_gn_b, norm2_g)
    small_m = (m_norm1_g, m_q_norm_g, m_k_norm_g, m_ret_gn_g, m_ret_gn_b, m_norm2_g)
    small_v = (v_norm1_g, v_q_norm_g, v_k_norm_g, v_ret_gn_g, v_ret_gn_b, v_norm2_g)

    shards = [w[0].astype(BF16) for w in big_w]
    core = lax.axis_index("c").astype(jnp.int32).reshape(1)
    loss_local, grad_x, parts, small_g = _step(
        x[0], loss_target[0], norm1_g, q_norm_g[0], k_norm_g[0], ret_gn_g, ret_gn_b, norm2_g, shards, core)
    small_all = _small_all_gather(_flat_small(small_g))
    names = ("w_in", "w_proj_a", "w_proj_b", "w_out", "w_up", "w_down")

    res = {}
    for n, p, w, m, v in zip(names, parts, big_w, big_m, big_v):
        outs = _adamw(f"adamw_{n}", p, w[0], m[0], v[0], min(128, w.shape[1]))
        res[n] = [o[None] for o in outs]
    s_outs = _adamw("adamw_small", small_all, _flat_small(small_w), _flat_small(small_m), _flat_small(small_v), SMALL_ROWS)
    small_names = ("norm1_g", "q_norm_g", "k_norm_g", "ret_gn_g", "ret_gn_b", "norm2_g")
    for n in small_names:
        res[n] = []
    for o in s_outs:
        flat, off = o.reshape(-1), 0
        for n, w in zip(small_names, small_w):
            res[n].append(flat[off:off + w.size].reshape(w.shape))
            off += w.size

    order = ("norm1_g", "w_in", "q_norm_g", "k_norm_g", "ret_gn_g", "ret_gn_b", "w_proj_a", "w_proj_b", "w_out",
             "norm2_g", "w_up", "w_down")
    loss = lax.psum(loss_local, MESH_AXES)
    return (loss, grad_x[None], *[res[n][0] for n in order], *[res[n][1] for n in order],
            *[res[n][2] for n in order], *[res[n][3] for n in order])
```

```python
import math

import numpy as np
import jax
import jax.numpy as jnp
from jax import lax
from jax.experimental import pallas as pl
from jax.experimental.pallas import tpu as pltpu

F32 = jnp.float32
BF16 = jnp.bfloat16

D_MODEL = 1024
ATT_GROUPS = ((128, 1), (512, 4), (2048, 16))
ATT_BLOCKS_PER_STEP = (32, 8, 2)
ATT_TOGETHER = 4
HPG = 4
ATT_HEADS = 12
DH = 128
BLK = 128
ATT_W = ATT_HEADS * DH
ATT_OUT_W = HPG * DH
RET_HEADS = 4
RET_QK = 256
RET_V = 512
RET_QK_W = RET_HEADS * RET_QK
RET_V_W = RET_HEADS * RET_V
CHUNK = 128
RET_PER_STEP = 4
D_FF = 4096
IN_W = 12800
REST_W = IN_W - 3 * ATT_W
EPS = 1e-6
ADAM_LR, ADAM_B1, ADAM_B2, ADAM_EPS, ADAM_WD, ADAM_STEP = 0.001, 0.9, 0.999, 1e-08, 0.01, 10
N_DEV = 8
N_CHIP = 4
MESH_AXES = ("x", "y", "c")
MESH = pl.DeviceIdType.MESH
VMEM_LIMIT = 56 * 1024 * 1024
LANES = 128
NEG = -1e30

_NN = (((1,), (0,)), ((), ()))
_NT = (((1,), (1,)), ((), ()))
_TN = (((0,), (0,)), ((), ()))

R_Q, R_K, R_V, R_G, R_GA, R_GB = 0, 1024, 2048, 4096, 6144, 7168

LOG_GAMMA = [float(v) for v in np.log(1.0 - 2.0 ** (-5.0 - np.arange(RET_HEADS, dtype=np.float32))).astype(np.float32)]
ALIBI = np.asarray(2.0 ** (-8.0 * np.arange(1, ATT_HEADS + 1, dtype=np.float32) / ATT_HEADS), np.float32)

SMALL_ROWS = (1024 + 1536 + 1536 + 2048 + 2048 + 1024) // LANES


def _dot(a, b, dims=_NN):
    return lax.dot_general(a, b, dims, preferred_element_type=F32)


def _cparams(sem):
    return pltpu.CompilerParams(dimension_semantics=sem, vmem_limit_bytes=VMEM_LIMIT)


def _sds(shape, dtype):
    return jax.ShapeDtypeStruct(shape, dtype)


class _Exchange:
    def __init__(self, ins, out_shapes, sems, plan):
        self.ins, self.out_shapes, self.sems, self.plan = list(ins), list(out_shapes), list(sems), plan

    def split(self, in_refs, out_refs, sem_refs):
        return self.plan(in_refs, out_refs, sem_refs)


def _mm(name, a, b, mode, tm, tn, tk, outs, epi, extras=(), b_pro=None,
        sem=("parallel", "parallel", "arbitrary"), exchange=None, in_place=None):
    if mode == "nn":
        (M, K), (_, N) = a.shape, b.shape
        a_spec = pl.BlockSpec((tm, tk), lambda i, j, k: (i, k))
        dims = _NN
    else:
        (K, M), (_, N) = a.shape, b.shape
        a_spec = pl.BlockSpec((tk, tm), lambda i, j, k: (k, i))
        dims = _TN
    assert M % tm == 0 and N % tn == 0 and K % tk == 0, (name, M, N, K, tm, tn, tk)
    nk = K // tk
    whole_b = dict(pipeline_mode=pl.Buffered(1)) if (nk == 1 and N == tn) else {}
    b_spec = pl.BlockSpec((tk, tn), lambda i, j, k: (k, j), **whole_b)
    n_ex, n_out = len(extras), len(outs)
    grid = (M // tm, N // tn, nk)
    n_xi = len(exchange.ins) if exchange else 0
    n_xo = len(exchange.out_shapes) if exchange else 0
    n_acc = 1 if nk > 1 else 0

    n_ip = 1 if in_place else 0

    def body(a_ref, b_ref, *rest):
        ex, rest = rest[:n_ex], rest[n_ex:]
        x_in, rest = rest[:n_xi], rest[n_xi + n_ip:]
        out, rest = rest[:n_out], rest[n_out:]
        x_out, rest = rest[:n_xo], rest[n_xo:]
        step =(pl.program_id(0) * grid[1] + pl.program_id(1)) * grid[2] + pl.program_id(2)
        n_steps = grid[0] * grid[1] * grid[2]
        if exchange:
            start, relay, finish = exchange.split(x_in, x_out, rest[n_acc:])
            pl.when(step == 0)(start)
        bv = b_ref[...]
        if b_pro is not None:
            bv = b_pro(bv)
        part = _dot(a_ref[...].astype(BF16), bv.astype(BF16), dims)
        if nk == 1:
            epi(part, ex, out)
        else:
            acc_ref = rest[0]
            k = pl.program_id(2)

            @pl.when(k == 0)
            def _():
                acc_ref[...] = part

            @pl.when(k > 0)
            def _():
                acc_ref[...] += part

            @pl.when(k == nk - 1)
            def _():
                epi(acc_ref[...], ex, out)
        if exchange:
            if relay is not None:
                pl.when(step == (7 * n_steps) // 8)(relay)
            pl.when(step == n_steps - 1)(finish)

    res = pl.pallas_call(
        body,
        name=name,
        grid=grid,
        in_specs=[a_spec, b_spec] + [s for _, s in extras] + [_ANY] * (n_xi + n_ip),
        out_specs=[s for _, s in outs] + [_ANY] * n_xo,
        out_shape=[o for o, _ in outs] + (exchange.out_shapes if exchange else []),
        scratch_shapes=([pltpu.VMEM((tm, tn), F32)] if nk > 1 else []) + (exchange.sems if exchange else []),
        input_output_aliases={2 + n_ex + n_xi: in_place[1]} if in_place else {},
        compiler_params=_cparams(("arbitrary",) * 3 if exchange else sem),
    )(a, b, *[e for e, _ in extras], *(exchange.ins if exchange else []), *([in_place[0]] if in_place else []))
    return (res[:n_out], res[n_out:]) if exchange else res


def _tile_ij(tm, tn):
    return pl.BlockSpec((tm, tn), lambda i, j, k: (i, j))


def _epi_store(dtype):
    def epi(acc, ex, out):
        out[0][...] = acc.astype(dtype)
    return epi


def _rms_rows(x):
    return lax.rsqrt(jnp.mean(x * x, axis=-1, keepdims=True) + EPS)


def _acc_rows(ref, part, first):
    @pl.when(first)
    def _():
        ref[...] = part

    @pl.when(jnp.logical_not(first))
    def _():
        ref[...] += part


def _rmsnorm_fwd(x, g, tm, exchange):
    S, Dm = x.shape
    n_steps = S // tm
    n_xi, n_xo = len(exchange.ins), len(exchange.out_shapes)

    def body(x_ref, g_ref, *rest):
        x_in, o_ref, x_out, sems = rest[:n_xi], rest[n_xi], rest[n_xi + 1:n_xi + 1 + n_xo], rest[n_xi + 1 + n_xo:]
        start, relay, finish = exchange.split(x_in, x_out, sems)
        step = pl.program_id(0)
        pl.when(step == 0)(start)
        xv = x_ref[...]
        o_ref[...] = (xv * _rms_rows(xv) * g_ref[...]).astype(BF16)

        @pl.when(step == n_steps - 1)
        def _():
            relay()
            finish()

    res = pl.pallas_call(
        body, name="rmsnorm1_fwd", grid=(n_steps,),
        in_specs=[pl.BlockSpec((tm, Dm), lambda i: (i, 0)), pl.BlockSpec((1, Dm), lambda i: (0, 0))] + [_ANY] * n_xi,
        out_specs=[pl.BlockSpec((tm, Dm), lambda i: (i, 0))] + [_ANY] * n_xo,
        out_shape=[_sds((S, Dm), BF16)] + exchange.out_shapes,
        scratch_shapes=exchange.sems,
        compiler_params=_cparams(("arbitrary",)),
    )(x, g, *exchange.ins)
    return res[0], res[1:]


def _rows(ref, r, b, d):
    if d == 1:
        return ref[b * BLK:(b + 1) * BLK, :]
    return ref[pl.ds(b * BLK * d + r, BLK, stride=d), :]


def _put_rows(ref, r, b, d, val):
    if d == 1:
        ref[b * BLK:(b + 1) * BLK, :] = val
    else:
        ref[pl.ds(b * BLK * d + r, BLK, stride=d), :] = val


def _head_norm(x, g):
    r = _rms_rows(x)
    xh = x * r
    return (xh * g).astype(BF16), xh, r


def _head_norm_bwd(dyn, xh, r, g):
    dxh = dyn * g
    dx = r * (dxh - xh * jnp.mean(dxh * xh, axis=-1, keepdims=True))
    return dx, jnp.sum(dyn * xh, axis=0, keepdims=True)


def _att_mask_bias(slope, d, first):
    qi = lax.broadcasted_iota(jnp.int32, (BLK, 2 * BLK), 0)
    kj = lax.broadcasted_iota(jnp.int32, (BLK, 2 * BLK), 1)
    dist = BLK + qi - kj
    valid = (dist >= 0) & (dist <= BLK)
    if first is not None:
        valid = valid & (jnp.logical_not(first) | (kj >= BLK))
    bias = -slope * (dist * d).astype(F32)
    return valid, bias


def _att_specs(gi, d, nb, S):
    span = BLK * d
    sb = span * nb
    nspan = S // span
    before = lambda n: jnp.maximum(n * nb - 1, 0)
    after = lambda n: jnp.minimum((n + 1) * nb, nspan - 1)
    zcol = lambda j, kind: 3 * (gi * HPG + j) + kind
    cur = lambda kind: pl.BlockSpec((sb, DH), lambda j, n: (n, zcol(j, kind)))
    prev = lambda kind: pl.BlockSpec((span, DH), lambda j, n: (before(n), zcol(j, kind)))
    nxt = lambda kind: pl.BlockSpec((span, DH), lambda j, n: (after(n), zcol(j, kind)))
    slot = pl.BlockSpec((sb, DH), lambda j, n: (n, j))
    slot_next = pl.BlockSpec((span, DH), lambda j, n: (after(n), j))
    head = pl.BlockSpec((None, 1, DH), lambda j, n: (gi * HPG + j, 0, 0))
    return cur, prev, nxt, slot, slot_next, head


def _att_fwd_group(z_att, gq3, gk3, slopes3, gi, d, nb, others=()):
    S = z_att.shape[0]
    nsb = S // (BLK * d * nb)
    scale = DH ** -0.5
    n_other = len(others)

    def body(q_ref, k_ref, v_ref, kp_ref, vp_ref, gq_ref, gk_ref, sl_ref, *rest):
        other_refs, (o_ref, l_ref) = rest[:2 * n_other], rest[2 * n_other:]
        slope = sl_ref[...][:, :1]
        valid0, bias = _att_mask_bias(slope, d, pl.program_id(1) == 0)
        valid_in, _ = _att_mask_bias(slope, d, None)
        gq, gk = gq_ref[...], gk_ref[...]
        memo = {}

        def get(kind, r, b):
            if (kind, r, b) not in memo:
                if kind == "k":
                    val = _head_norm(_rows(k_ref if b >= 0 else kp_ref, r, max(b, 0), d), gk)[0]
                else:
                    val = _rows(v_ref if b >= 0 else vp_ref, r, max(b, 0), d).astype(BF16)
                memo[(kind, r, b)] = val
            return memo[(kind, r, b)]

        units = [(r, b) for r in range(d) for b in range(nb)]
        for c0 in range(0, len(units), ATT_TOGETHER):
            us = units[c0:c0 + ATT_TOGETHER]
            q = [_head_norm(_rows(q_ref, r, b, d), gq)[0] for r, b in us]
            k2 = [jnp.concatenate([get("k", r, b - 1), get("k", r, b)], axis=0) for r, b in us]
            v2 = [jnp.concatenate([get("v", r, b - 1), get("v", r, b)], axis=0) for r, b in us]
            s = [jnp.where(valid0 if b == 0 else valid_in, _dot(q[i], k2[i], _NT) * scale + bias, NEG)
                 for i, (r, b) in enumerate(us)]
            m = [jnp.max(si, axis=-1, keepdims=True) for si in s]
            p = [jnp.exp(si - mi) for si, mi in zip(s, m)]
            den = [jnp.sum(pi, axis=-1, keepdims=True) for pi in p]
            o = [_dot(pi.astype(BF16), vi) / di for pi, vi, di in zip(p, v2, den)]
            for i, (r, b) in enumerate(us):
                _put_rows(o_ref, r, b, d, o[i])
                _put_rows(l_ref, r, b, d, jnp.broadcast_to(m[i] + jnp.log(den[i]), (BLK, DH)))
        if n_other:
            os_ = [ref[...] for ref in other_refs[:n_other]] + [o_ref[...]]
            ls_ = [ref[...] for ref in other_refs[n_other:]] + [l_ref[...]]
            m = ls_[0]
            for l in ls_[1:]:
                m = jnp.maximum(m, l)
            es = [jnp.exp(l - m) for l in ls_]
            tot, mix = es[0], es[0] * os_[0]
            for e, o in zip(es[1:], os_[1:]):
                tot, mix = tot + e, mix + e * o
            o_ref[...] = mix / tot
            l_ref[...] = m + jnp.log(tot)

    cur, prev, _, slot, _, head = _att_specs(gi, d, nb, S)
    return pl.pallas_call(
        body, name=f"att_fwd_g{gi}", grid=(HPG, nsb),
        in_specs=[cur(0), cur(1), cur(2), prev(1), prev(2), head, head, head] + [slot] * (2 * n_other),
        out_specs=[slot, slot],
        out_shape=[_sds((S, ATT_OUT_W), F32), _sds((S, ATT_OUT_W), F32)],
        compiler_params=_cparams(("parallel", "arbitrary")),
    )(z_att, z_att, z_att, z_att, z_att, gq3, gk3, slopes3, *[o for o, _ in others], *[l for _, l in others])


def _ret_tables(lg):
    ri = lax.broadcasted_iota(jnp.int32, (CHUNK, CHUNK), 0)
    ci = lax.broadcasted_iota(jnp.int32, (CHUNK, CHUNK), 1)
    diff = (ri - ci).astype(F32)
    decay = jnp.where(diff >= 0, jnp.exp(lg * jnp.maximum(diff, 0.0)), 0.0)
    idx = lax.broadcasted_iota(jnp.int32, (CHUNK, 1), 0).astype(F32)
    xi = jnp.exp(lg * (idx + 1.0))
    zeta = jnp.exp(lg * (CHUNK - 1.0 - idx))
    return decay, xi, zeta, math.exp(lg * CHUNK)


def _ret_specs(nsteps, rev):
    idx = (lambda n: nsteps - 1 - n) if rev else (lambda n: n)
    rows = CHUNK * RET_PER_STEP
    qk = lambda off: pl.BlockSpec((rows, RET_QK_W), lambda n: (idx(n), off // RET_QK_W))
    vv = lambda off: pl.BlockSpec((rows, RET_V_W), lambda n: (idx(n), off // RET_V_W))
    par = pl.BlockSpec((1, RET_V_W), lambda n: (0, 0))
    wide = pl.BlockSpec((rows, RET_V_W), lambda n: (idx(n), 0))
    st = pl.BlockSpec((RET_HEADS, RET_PER_STEP, RET_QK, RET_V), lambda n: (0, idx(n), 0, 0))
    return qk, vv, par, wide, st


def _ret_fwd(z_rest, gn_g, gn_b):
    S = z_rest.shape[0]
    nch = S // CHUNK
    nsteps = nch // RET_PER_STEP

    def body(q_ref, k_ref, v_ref, gr_ref, g_ref, b_ref, or_ref, o_ref, st_ref, state):
        @pl.when(pl.program_id(0) == 0)
        def _():
            state[...] = jnp.zeros_like(state)

        for c in range(RET_PER_STEP):
            rc = slice(c * CHUNK, (c + 1) * CHUNK)
            for h in range(RET_HEADS):
                decay, xi, zeta, gch = _ret_tables(LOG_GAMMA[h])
                cq = slice(h * RET_QK, (h + 1) * RET_QK)
                cv = slice(h * RET_V, (h + 1) * RET_V)
                q = q_ref[rc, cq]
                kc32 = k_ref[rc, cq].astype(F32) * (RET_QK ** -0.5)
                kc = kc32.astype(BF16)
                v = v_ref[rc, cv]
                st = state[h]
                stb = st.astype(BF16)
                st_ref[h, c] = stb
                s = _dot(q, kc, _NT) * decay
                o = _dot(s.astype(BF16), v) + _dot(q, stb) * xi
                state[h] = st * gch + _dot((kc32 * zeta).astype(BF16), v, _TN)
                mu = jnp.mean(o, axis=-1, keepdims=True)
                cen = o - mu
                yh = cen * lax.rsqrt(jnp.mean(cen * cen, axis=-1, keepdims=True) + EPS)
                gr = gr_ref[rc, cv].astype(F32)
                or_ref[rc, cv] = ((yh * g_ref[:, cv] + b_ref[:, cv]) * (gr * jax.nn.sigmoid(gr))).astype(BF16)
                o_ref[rc, cv] = o

    qk, vv, par, wide, st = _ret_specs(nsteps, False)
    return pl.pallas_call(
        body, name="ret_fwd", grid=(nsteps,),
        in_specs=[qk(R_Q), qk(R_K), vv(R_V), vv(R_G), par, par],
        out_specs=[wide, wide, st],
        out_shape=[_sds((S, RET_V_W), BF16), _sds((S, RET_V_W), F32), _sds((RET_HEADS, nch, RET_QK, RET_V), BF16)],
        scratch_shapes=[pltpu.VMEM((RET_HEADS, RET_QK, RET_V), F32)],
        compiler_params=_cparams(("arbitrary",)),
    )(z_rest, z_rest, z_rest, z_rest, gn_g, gn_b)


def _merge_fwd(o_a, o_r, z_rest, x, wpa, wpb, wout, g2, tm):
    S = x.shape[0]

    def body(oa_ref, or_ref, ga_ref, gb_ref, x_ref, wpa_ref, wpb_ref, wo_ref, g2_ref,
             x1_ref, y_ref, pa_ref, pb_ref, xn2_ref):
        pa = _dot(oa_ref[...].astype(BF16), wpa_ref[...])
        pb = _dot(or_ref[...], wpb_ref[...])
        y = jax.nn.sigmoid(ga_ref[...].astype(F32)) * pa + jax.nn.sigmoid(gb_ref[...].astype(F32)) * pb
        yb = y.astype(BF16)
        x1 = x_ref[...] + _dot(yb, wo_ref[...])
        x1_ref[...] = x1
        y_ref[...] = yb
        pa_ref[...] = pa.astype(BF16)
        pb_ref[...] = pb.astype(BF16)
        xn2_ref[...] = (x1 * _rms_rows(x1) * g2_ref[...]).astype(BF16)

    row = lambda w: pl.BlockSpec((tm, w), lambda i: (i, 0))
    full = lambda a: pl.BlockSpec(a.shape, lambda i: (0, 0))
    return pl.pallas_call(
        body, name="merge_fwd", grid=(S // tm,),
        in_specs=[row(ATT_OUT_W), row(RET_V_W),
                  pl.BlockSpec((tm, D_MODEL), lambda i: (i, R_GA // D_MODEL)),
                  pl.BlockSpec((tm, D_MODEL), lambda i: (i, R_GB // D_MODEL)),
                  row(D_MODEL), full(wpa), full(wpb), full(wout), full(g2)],
        out_specs=[row(D_MODEL)] * 5,
        out_shape=[_sds((S, D_MODEL), F32)] + [_sds((S, D_MODEL), BF16)] * 4,
        compiler_params=_cparams(("parallel",)),
    )(o_a, o_r, z_rest, z_rest, x, wpa, wpb, wout, g2)


def _rms_bwd(dy, xv, g):
    r = _rms_rows(xv)
    xh = xv * r
    dg = dy * g
    dx = r * (dg - xh * jnp.mean(dg * xh, axis=-1, keepdims=True))
    return dx, jnp.sum(dy * xh, axis=0, keepdims=True)


def _norm_gate_bwd(dout, o, gr, gam, bet):
    mean = lambda xs: [jnp.mean(x, axis=-1, keepdims=True) for x in xs]
    cen = [x - m for x, m in zip(o, mean(o))]
    rstd = [lax.rsqrt(v + EPS) for v in mean([c * c for c in cen])]
    yh = [c * r for c, r in zip(cen, rstd)]
    y = [a * g + b for a, g, b in zip(yh, gam, bet)]
    sg = [jax.nn.sigmoid(g) for g in gr]
    dy = [d * (g * s) for d, g, s in zip(dout, gr, sg)]
    dgr = [d * a * (s * (1.0 + g * (1.0 - s))) for d, a, s, g in zip(dout, y, sg, gr)]
    dyh = [d * g for d, g in zip(dy, gam)]
    m1, m2 = mean(dyh), mean([a * b for a, b in zip(dyh, yh)])
    do = [r * (d - a - h * b) for r, d, a, h, b in zip(rstd, dyh, m1, yh, m2)]
    dg = [jnp.sum(d * h, axis=0, keepdims=True) for d, h in zip(dy, yh)]
    db = [jnp.sum(d, axis=0, keepdims=True) for d in dy]
    return do, dgr, dg, db


def _ret_bwd(do, states, z_rest, dz_rest):
    S = z_rest.shape[0]
    nsteps = S // CHUNK // RET_PER_STEP

    def body(do_ref, st_ref, q_ref, k_ref, v_ref, dz_in, dz_ref, gst):
        del dz_in

        @pl.when(pl.program_id(0) == 0)
        def _():
            gst[...] = jnp.zeros_like(gst)

        for c in reversed(range(RET_PER_STEP)):
            rc = slice(c * CHUNK, (c + 1) * CHUNK)
            for h in range(RET_HEADS):
                decay, xi, zeta, gch = _ret_tables(LOG_GAMMA[h])
                cq = slice(h * RET_QK, (h + 1) * RET_QK)
                cv = slice(h * RET_V, (h + 1) * RET_V)
                q = q_ref[rc, cq]
                kc32 = k_ref[rc, cq].astype(F32) * (RET_QK ** -0.5)
                kc = kc32.astype(BF16)
                v = v_ref[rc, cv]
                dob = do_ref[rc, cv]
                a = (_dot(q, kc, _NT) * decay).astype(BF16)
                da = (_dot(dob, v, _NT) * decay).astype(BF16)
                dcross = (dob.astype(F32) * xi).astype(BF16)
                g_next = gst[h]
                gb = g_next.astype(BF16)
                dq = _dot(da, kc) + _dot(dcross, st_ref[h, c], _NT)
                dkc = _dot(da, q, _TN)
                dkz = _dot(v, gb, _NT)
                dv = _dot(a, dob, _TN) + _dot((kc32 * zeta).astype(BF16), gb)
                gst[h] = g_next * gch + _dot(q, dcross, _TN)
                dz_ref[rc, R_Q + h * RET_QK:R_Q + (h + 1) * RET_QK] = dq.astype(BF16)
                dz_ref[rc, R_K + h * RET_QK:R_K + (h + 1) * RET_QK] = (
                    (dkc + dkz * zeta) * (RET_QK ** -0.5)).astype(BF16)
                dz_ref[rc, R_V + h * RET_V:R_V + (h + 1) * RET_V] = dv.astype(BF16)

    qk, vv, _, wide, st = _ret_specs(nsteps, True)
    return pl.pallas_call(
        body, name="ret_bwd", grid=(nsteps,),
        in_specs=[wide, st, qk(R_Q), qk(R_K), vv(R_V), _ANY],
        out_specs=pl.BlockSpec((CHUNK * RET_PER_STEP, R_G), lambda n: (nsteps - 1 - n, 0)),
        out_shape=_sds(dz_rest.shape, BF16),
        input_output_aliases={5: 0},
        scratch_shapes=[pltpu.VMEM((RET_HEADS, RET_QK, RET_V), F32)],
        compiler_params=_cparams(("arbitrary",)),
    )(do, states, z_rest, z_rest, z_rest, dz_rest)


def _att_probs(q, k, lmix, valid, bias):
    s = _dot(q, k, _NT) * (DH ** -0.5) + bias
    return jnp.where(valid, jnp.exp(jnp.where(valid, s, NEG) - lmix), 0.0)


def _att_bwd_group(z_att, dz_att, do_a, delta_lmix, gq3, gk3, slopes3, gi, d, nb, exchange=None):
    S = z_att.shape[0]
    sb = BLK * d * nb
    nsb = S // sb
    scale = DH ** -0.5
    aliased = dz_att is not None
    n_xi = len(exchange.ins) if exchange else 0
    n_xo = len(exchange.out_shapes) if exchange else 0

    def body(q_ref, k_ref, v_ref, kp_ref, vp_ref, qn_ref, do_ref, don_ref, dl_ref, dln_ref,
             gq_ref, gk_ref, sl_ref, *rest):
        rest = rest[1:] if aliased else rest
        x_in, (dz_ref, dgq_ref, dgk_ref), rest = rest[:n_xi], rest[n_xi:n_xi + 3], rest[n_xi + 3:]
        x_out, stage, x_sems = rest[:n_xo], rest[n_xo], rest[n_xo + 1:]
        n = pl.program_id(1)
        step = pl.program_id(0) * nsb + n
        if exchange:
            start, _, finish = exchange.split(x_in, x_out, x_sems)
            pl.when(step == 0)(start)
        slope = sl_ref[...][:, :1]
        valid0, bias = _att_mask_bias(slope, d, n == 0)
        valid_in, _ = _att_mask_bias(slope, d, None)
        qi = lax.broadcasted_iota(jnp.int32, (BLK, BLK), 0)
        kj = lax.broadcasted_iota(jnp.int32, (BLK, BLK), 1)
        dist_n = BLK + qi - kj
        valid_n_in = dist_n <= BLK
        valid_n_last = valid_n_in & (n < nsb - 1)
        bias_n = -slope * (dist_n * d).astype(F32)
        gq, gk = gq_ref[...], gk_ref[...]
        dgq = jnp.zeros((1, DH), F32)
        dgk = jnp.zeros((1, DH), F32)
        memo = {}

        def get(kind, r, b):
            if (kind, r, b) not in memo:
                inner = 0 <= b < nb
                bb = b if inner else 0
                if kind == "q":
                    val = _head_norm(_rows(q_ref if inner else qn_ref, r, bb, d), gq)
                elif kind == "k":
                    val = _head_norm(_rows(k_ref if inner else kp_ref, r, bb, d), gk)
                elif kind == "v":
                    val = _rows(v_ref if inner else vp_ref, r, bb, d).astype(BF16)
                elif kind == "do":
                    val = _rows(do_ref if inner else don_ref, r, bb, d).astype(BF16)
                elif kind == "dl+lm":
                    val = _rows(dl_ref if inner else dln_ref, r, bb, d)
                elif kind == "dl":
                    val = get("dl+lm", r, b)[:, :1]
                else:
                    val = get("dl+lm", r, b)[:, DH // 2:DH // 2 + 1]
                memo[(kind, r, b)] = val
            return memo[(kind, r, b)]

        units = [(r, b) for r in range(d) for b in range(nb)]
        for c0 in range(0, len(units), ATT_TOGETHER):
            us = units[c0:c0 + ATT_TOGETHER]
            k2 = [jnp.concatenate([get("k", r, b - 1)[0], get("k", r, b)[0]], axis=0) for r, b in us]
            v2 = [jnp.concatenate([get("v", r, b - 1), get("v", r, b)], axis=0) for r, b in us]
            p = [_att_probs(get("q", r, b)[0], k2[i], get("lm", r, b), valid0 if b == 0 else valid_in, bias)
                 for i, (r, b) in enumerate(us)]
            dp = [_dot(get("do", r, b), v2[i], _NT) for i, (r, b) in enumerate(us)]
            ds = [(p[i] * (dp[i] - get("dl", r, b)) * scale).astype(BF16) for i, (r, b) in enumerate(us)]
            dq = [_dot(ds[i], k2[i]) for i in range(len(us))]
            p_n = [_att_probs(get("q", r, b + 1)[0], get("k", r, b)[0], get("lm", r, b + 1),
                              valid_n_last if b == nb - 1 else valid_n_in, bias_n) for r, b in us]
            dp_n = [_dot(get("do", r, b + 1), get("v", r, b), _NT) for r, b in us]
            ds_n = [(p_n[i] * (dp_n[i] - get("dl", r, b + 1)) * scale).astype(BF16) for i, (r, b) in enumerate(us)]
            dk = [_dot(ds[i][:, BLK:], get("q", r, b)[0], _TN) + _dot(ds_n[i], get("q", r, b + 1)[0], _TN)
                  for i, (r, b) in enumerate(us)]
            dv = [_dot(p[i][:, BLK:].astype(BF16), get("do", r, b), _TN)
                  + _dot(p_n[i].astype(BF16), get("do", r, b + 1), _TN) for i, (r, b) in enumerate(us)]
            for i, (r, b) in enumerate(us):
                _, qh, qr = get("q", r, b)
                _, kh, kr = get("k", r, b)
                dxq, dg_q = _head_norm_bwd(dq[i], qh, qr, gq)
                dxk, dg_k = _head_norm_bwd(dk[i], kh, kr, gk)
                _put_rows(stage.at[0], r, b, d, dxq)
                _put_rows(stage.at[1], r, b, d, dxk)
                _put_rows(stage.at[2], r, b, d, dv[i])
                dgq, dgk = dgq + dg_q, dgk + dg_k
        for kind in range(3):
            dz_ref[:, kind * DH:(kind + 1) * DH] = stage[kind].astype(BF16)
        _acc_rows(dgq_ref, dgq, n == 0)
        _acc_rows(dgk_ref, dgk, n == 0)
        if exchange:
            pl.when(step == HPG * nsb - 1)(finish)

    cur, prev, nxt, slot, slot_next, head = _att_specs(gi, d, nb, S)
    gain = pl.BlockSpec((None, 1, DH), lambda j, n: (j, 0, 0))
    in_specs = [cur(0), cur(1), cur(2), prev(1), prev(2), nxt(0),
                slot, slot_next, slot, slot_next, head, head, head]
    args = [z_att] * 6 + [do_a, do_a, delta_lmix, delta_lmix, gq3, gk3, slopes3]
    if aliased:
        in_specs.append(_ANY)
        args.append(dz_att)
    res = pl.pallas_call(
        body, name=f"att_bwd_g{gi}", grid=(HPG, nsb),
        in_specs=in_specs + [_ANY] * n_xi,
        out_specs=[pl.BlockSpec((sb, 3 * DH), lambda j, n: (n, gi * HPG + j)), gain, gain] + [_ANY] * n_xo,
        out_shape=[_sds(z_att.shape, BF16), _sds((HPG, 1, DH), F32), _sds((HPG, 1, DH), F32)]
        + (exchange.out_shapes if exchange else []),
        input_output_aliases={len(args) - 1: 0} if aliased else {},
        scratch_shapes=[pltpu.VMEM((3, sb, DH), F32)] + (exchange.sems if exchange else []),
        compiler_params=_cparams(("arbitrary", "arbitrary") if exchange else ("parallel", "arbitrary")),
    )(*args, *(exchange.ins if exchange else []))
    return res[:3], res[3:]


def _step(x, target, norm1_g, q_norm_g, k_norm_g, gn_g, gn_b, norm2_g, shards, core):
    S = x.shape[0]
    tm = min(512, S)
    ts = min(256, S)
    tk = min(2048, S)
    tk2 = min(4096, S)
    later_shards = shards[1:]
    gq3 = q_norm_g.reshape(ATT_HEADS, 1, DH)
    gk3 = k_norm_g.reshape(ATT_HEADS, 1, DH)
    slopes3 = jnp.asarray(np.broadcast_to(ALIBI[:, None, None], (ATT_HEADS, 1, DH)).copy())

    xn, (g_in,) = _rmsnorm_fwd(x, norm1_g, tm, _gather_exchange(shards[:1]))
    w_in = _cols(g_in)
    w_att = w_in[:, :3 * ATT_W].reshape(D_MODEL, 3, ATT_HEADS, DH).transpose(0, 2, 1, 3).reshape(D_MODEL, 3 * ATT_W)
    w_rest = w_in[:, 3 * ATT_W:]
    w_att_t, w_rest_t = w_att.T, w_rest.T
    z_att = _mm("in_proj_att", xn, w_att, "nn", tm, 3 * ATT_W, D_MODEL,
                [(_sds((S, 3 * ATT_W), F32), _tile_ij(tm, 3 * ATT_W))], _epi_store(F32))[0]
    (z_rest,), gathered = _mm("in_proj_rest", xn, w_rest, "nn", tm, REST_W, D_MODEL,
                              [(_sds((S, REST_W), BF16), _tile_ij(tm, REST_W))], _epi_store(BF16),
                              exchange=_gather_exchange(later_shards))
    w_pa, w_pb, w_out, w_up, w_down = [f(g) for f, g in zip((_cols, _rows_of, _rows_of, _cols, _rows_of), gathered)]
    w_up_t, w_down_t, w_out_t, w_pa_t, w_pb_t = w_up.T, w_down.T, w_out.T, w_pa.T, w_pb.T
    done = []
    for gi, ((_, d), nb) in enumerate(zip(ATT_GROUPS, ATT_BLOCKS_PER_STEP)):
        last = gi == len(ATT_GROUPS) - 1
        done.append(_att_fwd_group(z_att, gq3, gk3, slopes3, gi, d, nb, others=tuple(done) if last else ()))
    o_a, lmix = done[-1]
    o_r, o_pre, states = _ret_fwd(z_rest, gn_g, gn_b)
    x1, y, pa, pb, xn2 = _merge_fwd(o_a, o_r, z_rest, x, w_pa, w_pb, w_out, norm2_g, tm)

    def epi_up(acc, ex, out):
        r = jnp.maximum(acc, 0.0)
        out[0][...] = (r * r).astype(BF16)
        out[1][...] = r.astype(BF16)

    h, relu_u = _mm("mlp_up", xn2, w_up, "nn", tm, D_FF, D_MODEL,
                    [(_sds((S, D_FF), BF16), _tile_ij(tm, D_FF)), (_sds((S, D_FF), BF16), _tile_ij(tm, D_FF))], epi_up)

    def epi_down(acc, ex, out):
        diff = ex[0][...] + acc - ex[1][...]
        out[0][...] = diff * (1.0 / D_MODEL)
        out[1][...] = jnp.broadcast_to(jnp.sum(diff * diff) * (1.0 / (8 * LANES)), (8, LANES))

    row_tile = _tile_ij(tm, D_MODEL)
    dx2, loss_parts = _mm(
        "mlp_down_loss", h, w_down, "nn", tm, D_MODEL, D_FF,
        [(_sds((S, D_MODEL), F32), row_tile),
         (_sds((S // tm * 8, LANES), F32), pl.BlockSpec((8, LANES), lambda i, j, k: (i, 0)))],
        epi_down, extras=[(x1, row_tile), (target, row_tile)])
    loss_local = jnp.sum(loss_parts) * (0.5 / D_MODEL)

    def epi_du(acc, ex, out):
        out[0][...] = (acc * (2.0 * ex[0][...].astype(F32))).astype(BF16)

    du = _mm("mlp_down_bwd", dx2, w_down_t, "nn", tm, D_FF, D_MODEL,
             [(_sds((S, D_FF), BF16), _tile_ij(tm, D_FF))], epi_du, extras=[(relu_u, _tile_ij(tm, D_FF))])[0]
    gw_down = _mm("gw_down", h, dx2, "tn", 1024, D_MODEL, tk,
                  [(_sds((D_FF, D_MODEL), BF16), _tile_ij(1024, D_MODEL))], _epi_store(BF16))[0]
    gw_up = _mm("gw_up", xn2, du, "tn", D_MODEL, 512, tk2,
                [(_sds((N_DEV, D_MODEL, 512), BF16), pl.BlockSpec((None, D_MODEL, 512), lambda i, j, k: (j, 0, 0)))],
                _epi_store(BF16))[0]

    vec = pl.BlockSpec((1, D_MODEL), lambda i, j, k: (0, 0))
    seq_sem = ("arbitrary", "arbitrary", "arbitrary")

    def epi_norm2(acc, ex, out):
        dx, dg = _rms_bwd(acc, ex[0][...], ex[2][...])
        out[0][...] = ex[1][...] + dx
        _acc_rows(out[1], dg, pl.program_id(0) == 0)

    dx1, g_norm2 = _mm(
        "mlp_up_bwd", du, w_up_t, "nn", tm, D_MODEL, D_FF,
        [(_sds((S, D_MODEL), F32), row_tile), (_sds((1, D_MODEL), F32), vec)],
        epi_norm2, extras=[(x1, row_tile), (dx2, row_tile), (norm2_g, vec)], sem=seq_sem)

    def epi_dy(acc, ex, out):
        sa = jax.nn.sigmoid(ex[0][...].astype(F32))
        sb = jax.nn.sigmoid(ex[1][...].astype(F32))
        out[0][...] = (acc * sa).astype(BF16)
        out[1][...] = (acc * sb).astype(BF16)
        out[2][:, :D_MODEL] = (acc * ex[2][...].astype(F32) * (sa * (1.0 - sa))).astype(BF16)
        out[2][:, D_MODEL:] = (acc * ex[3][...].astype(F32) * (sb * (1.0 - sb))).astype(BF16)

    ga_spec = pl.BlockSpec((tm, D_MODEL), lambda i, j, k: (i, R_GA // D_MODEL))
    gb_spec = pl.BlockSpec((tm, D_MODEL), lambda i, j, k: (i, R_GB // D_MODEL))
    gates_spec = pl.BlockSpec((tm, 2 * D_MODEL), lambda i, j, k: (i, R_GA // (2 * D_MODEL)))
    dpa, dpb, dz_rest = _mm(
        "out_proj_bwd", dx1, w_out_t, "nn", tm, D_MODEL, D_MODEL,
        [(_sds((S, D_MODEL), BF16), row_tile), (_sds((S, D_MODEL), BF16), row_tile), (_sds((S, REST_W), BF16), gates_spec)],
        epi_dy, extras=[(z_rest, ga_spec), (z_rest, gb_spec), (pa, row_tile), (pb, row_tile)])
    gw_out = _mm("gw_out", y, dx1, "tn", D_MODEL, D_MODEL, tk,
                 [(_sds((D_MODEL, D_MODEL), BF16), _tile_ij(D_MODEL, D_MODEL))], _epi_store(BF16))[0]
    gw_pa = _mm("gw_proj_a", o_a, dpa, "tn", ATT_OUT_W, D_MODEL, tk,
                [(_sds((ATT_OUT_W, D_MODEL), BF16), _tile_ij(ATT_OUT_W, D_MODEL))], _epi_store(BF16))[0]
    gw_pb = _mm("gw_proj_b", o_r, dpb, "tn", 1024, D_MODEL, tk2,
                [(_sds((RET_V_W, D_MODEL), BF16), _tile_ij(1024, D_MODEL))], _epi_store(BF16))[0]

    def epi_doa(acc, ex, out):
        out[0][...] = acc
        prod = acc * ex[0][...]
        delta = jnp.concatenate(
            [jnp.broadcast_to(jnp.sum(prod[:, j * DH:(j + 1) * DH], axis=-1, keepdims=True), (prod.shape[0], DH))
             for j in range(HPG)], axis=1)
        lane = lax.broadcasted_iota(jnp.int32, delta.shape, 1)
        out[1][...] = jnp.where(lane % DH < DH // 2, delta, ex[1][...])

    slot_tile = _tile_ij(tm, ATT_OUT_W)
    do_a, delta_lmix = _mm("proj_a_bwd", dpa, w_pa_t, "nn", tm, ATT_OUT_W, D_MODEL,
                           [(_sds((S, ATT_OUT_W), F32), slot_tile), (_sds((S, ATT_OUT_W), F32), slot_tile)],
                           epi_doa, extras=[(o_a, slot_tile), (lmix, slot_tile)])
    owed = [_chip_core(_by_owner_cols(gw_pa)), _chip_core(gw_pb.reshape(N_DEV, -1, D_MODEL)),
            _chip_core(gw_out.reshape(N_DEV, -1, D_MODEL)), _chip_core(gw_up), _chip_core(gw_down.reshape(N_DEV, -1, D_MODEL))]
    names = ("w_proj_a", "w_proj_b", "w_out", "w_up", "w_down")

    def epi_dor(acc, ex, out):
        cvs = [slice(h * RET_V, (h + 1) * RET_V) for h in range(RET_HEADS)]
        do, dgr, dg, db = _norm_gate_bwd([acc[:, cv] for cv in cvs], [ex[0][:, cv] for cv in cvs],
                                         [ex[1][:, cv].astype(F32) for cv in cvs],
                                         [ex[2][:, cv] for cv in cvs], [ex[3][:, cv] for cv in cvs])
        for h, cv in enumerate(cvs):
            out[0][:, cv] = do[h].astype(BF16)
            out[1][:, cv] = dgr[h].astype(BF16)
        first = pl.program_id(0) == 0
        _acc_rows(out[2], jnp.concatenate(dg, axis=1), first)
        _acc_rows(out[3], jnp.concatenate(db, axis=1), first)

    wide_tile = _tile_ij(tm, RET_V_W)
    gate_tile = pl.BlockSpec((tm, RET_V_W), lambda i, j, k: (i, R_G // RET_V_W))
    wide_vec = pl.BlockSpec((1, RET_V_W), lambda i, j, k: (0, 0))
    (do_ret, dz_rest, g_gn_g, g_gn_b), got = _mm(
        "proj_b_bwd", dpb, w_pb_t, "nn", tm, RET_V_W, D_MODEL,
        [(_sds((S, RET_V_W), BF16), wide_tile), (_sds((S, REST_W), BF16), gate_tile),
         (_sds((1, RET_V_W), F32), wide_vec), (_sds((1, RET_V_W), F32), wide_vec)],
        epi_dor, extras=[(o_pre, wide_tile), (z_rest, gate_tile), (gn_g, wide_vec), (gn_b, wide_vec)],
        exchange=_pair_exchange(owed), in_place=(dz_rest, 1))
    chip_sums = [_pair_sum(f"pair_sum_{n}", g, r, core, min(256, g.shape[2])) for n, g, r in zip(names, owed, got)]

    dz_rest = _ret_bwd(do_ret, states, z_rest, dz_rest)
    dz_att, gq_parts, gk_parts, parts_late = None, [], [], None
    for gi, ((_, d), nb) in enumerate(zip(ATT_GROUPS, ATT_BLOCKS_PER_STEP)):
        last = gi == len(ATT_GROUPS) - 1
        (dz_att, gq_p, gk_p), parts = _att_bwd_group(z_att, dz_att, do_a, delta_lmix, gq3, gk3, slopes3, gi, d, nb,
                                                     exchange=_chip_exchange(chip_sums) if last else None)
        parts_late = parts if last else parts_late
        gq_parts.append(gq_p)
        gk_parts.append(gk_p)
    g_qn = jnp.concatenate(gq_parts, axis=0).reshape(1, ATT_HEADS, DH)
    g_kn = jnp.concatenate(gk_parts, axis=0).reshape(1, ATT_HEADS, DH)

    gw_att = _mm("gw_in_att", xn, dz_att, "tn", D_MODEL, ATT_W, tk,
                 [(_sds((D_MODEL, 3 * ATT_W), BF16), _tile_ij(D_MODEL, ATT_W))], _epi_store(BF16))[0]
    gw_rest = _mm("gw_in_rest", xn, dz_rest, "tn", D_MODEL, 1024, tk2,
                  [(_sds((D_MODEL, REST_W), BF16), _tile_ij(D_MODEL, 1024))], _epi_store(BF16))[0]
    gw_att = gw_att.reshape(D_MODEL, ATT_HEADS, 3, DH).transpose(0, 2, 1, 3).reshape(D_MODEL, 3 * ATT_W)
    gw_in = jnp.concatenate([gw_att, gw_rest], axis=1)
    owed_in = _chip_core(_by_owner_cols(gw_in))
    (dxn_att,), (got_in,) = _mm("in_proj_att_bwd", dz_att, w_att_t, "nn", tm, D_MODEL, 3 * ATT_W,
                                [(_sds((S, D_MODEL), F32), row_tile)], _epi_store(F32), exchange=_pair_exchange([owed_in]))
    chip_sum_in = _pair_sum("pair_sum_w_in", owed_in, got_in, core, min(256, owed_in.shape[2]))

    def epi_norm1(acc, ex, out):
        dx, dg = _rms_bwd(acc + ex[0][...], ex[1][...], ex[3][...])
        out[0][...] = ex[2][...] + dx
        _acc_rows(out[1], dg, pl.program_id(0) == 0)

    short_tile = _tile_ij(ts, D_MODEL)
    (grad_x, g_norm1), parts_in = _mm(
        "in_proj_rest_bwd", dz_rest, w_rest_t, "nn", ts, D_MODEL, REST_W,
        [(_sds((S, D_MODEL), F32), short_tile), (_sds((1, D_MODEL), F32), vec)],
        epi_norm1, extras=[(dxn_att, short_tile), (x, short_tile), (dx1, short_tile), (norm1_g, vec)],
        exchange=_chip_exchange([chip_sum_in]))

    small = (g_norm1, g_qn, g_kn, g_gn_g, g_gn_b, g_norm2)
    return loss_local, grad_x, list(parts_in) + list(parts_late), small


def _position():
    return lax.axis_index("x"), lax.axis_index("y"), lax.axis_index("c")


def _other_chips(x, y):
    return [(1 - x, y), (x, 1 - y), (1 - x, 1 - y)]


_ANY = pl.BlockSpec(memory_space=pl.ANY)


def _gather_exchange(shards):
    nw = len(shards)

    def plan(x_refs, out_refs, sems):
        send_sems, recv_sems, local_sems = sems
        x, y, c = _position()
        me, sibling = (x, y, c), (x, y, 1 - c)
        chips = _other_chips(x, y)

        def copy(w, k, block, to, own=False):
            px, py, pc = block
            rows = out_refs[w].at[4 * px + 2 * py + pc]
            return pltpu.make_async_remote_copy(
                src_ref=x_refs[w] if own else rows, dst_ref=rows,
                send_sem=send_sems.at[7 * w + k], recv_sem=recv_sems.at[7 * w + k], device_id=to, device_id_type=MESH)

        def mine(w):
            return pltpu.make_async_copy(x_refs[w], out_refs[w].at[4 * x + 2 * y + c], local_sems.at[w])

        def own_sends(w):
            return [copy(w, 0, me, sibling, own=True)] + [copy(w, 1 + j, me, (*chip, c), own=True)
                                                          for j, chip in enumerate(chips)]

        def start():
            for w in range(nw):
                mine(w).start()
                for cp in own_sends(w):
                    cp.start()

        def relay():
            for j, chip in enumerate(chips):
                for w in range(nw):
                    copy(w, 1 + j, (*chip, c), me).wait_recv()
                    copy(w, 4 + j, (*chip, c), sibling).start()

        def finish():
            for w in range(nw):
                copy(w, 0, sibling, me).wait_recv()
                for j, chip in enumerate(chips):
                    copy(w, 4 + j, (*chip, 1 - c), me).wait_recv()
            for w in range(nw):
                for cp in own_sends(w):
                    cp.wait_send()
                for j, chip in enumerate(chips):
                    copy(w, 4 + j, (*chip, c), sibling).wait_send()
                mine(w).wait()

        return start, relay, finish

    return _Exchange(shards, [_sds((N_DEV,) + s.shape, s.dtype) for s in shards],
                     [pltpu.SemaphoreType.DMA((7 * nw,)), pltpu.SemaphoreType.DMA((7 * nw,)),
                      pltpu.SemaphoreType.DMA((nw,))], plan)


def _run_exchange(name, exchange):
    n_in, n_out = len(exchange.ins), len(exchange.out_shapes)

    def body(*refs):
        start, relay, finish = exchange.split(refs[:n_in], refs[n_in:n_in + n_out], refs[n_in + n_out:])
        start()
        if relay is not None:
            relay()
        finish()

    return pl.pallas_call(
        body, name=name, out_shape=exchange.out_shapes,
        in_specs=[_ANY] * n_in, out_specs=[_ANY] * n_out, scratch_shapes=exchange.sems,
    )(*exchange.ins)


def _pair_exchange(grads):
    ng = len(grads)

    def plan(g_refs, out_refs, sems):
        send_sems, recv_sems = sems
        x, y, c = _position()

        def copies():
            return [pltpu.make_async_remote_copy(
                src_ref=g_refs[w].at[:, 1 - c], dst_ref=out_refs[w], send_sem=send_sems.at[w],
                recv_sem=recv_sems.at[w], device_id=(x, y, 1 - c), device_id_type=MESH) for w in range(ng)]

        def start():
            for cp in copies():
                cp.start()

        def finish():
            for cp in copies():
                cp.wait()

        return start, None, finish

    return _Exchange(grads, [_sds((N_CHIP,) + g.shape[2:], g.dtype) for g in grads],
                     [pltpu.SemaphoreType.DMA((ng,)), pltpu.SemaphoreType.DMA((ng,))], plan)


def _small_all_gather(small):
    def plan(in_refs, out_refs, sems):
        (s_ref,), (s_out,) = in_refs, out_refs
        send_sems, recv_sems, local_sem = sems
        x, y, c = _position()
        me_id = 4 * x + 2 * y + c
        flips = [(a, b, e) for a in (0, 1) for b in (0, 1) for e in (0, 1)][1:]
        peers = [(x ^ a, y ^ b, c ^ e) for a, b, e in flips]

        def start():
            pltpu.make_async_copy(s_ref, s_out.at[me_id], local_sem).start()
            for k, p in enumerate(peers):
                pltpu.make_async_remote_copy(
                    src_ref=s_ref, dst_ref=s_out.at[me_id], send_sem=send_sems.at[k], recv_sem=recv_sems.at[k],
                    device_id=p, device_id_type=MESH).start()

        def finish():
            for k, (px, py, pc) in enumerate(peers):
                pltpu.make_async_remote_copy(
                    src_ref=s_ref, dst_ref=s_out.at[4 * px + 2 * py + pc], send_sem=send_sems.at[k],
                    recv_sem=recv_sems.at[k], device_id=(px, py, pc), device_id_type=MESH).wait()
            pltpu.make_async_copy(s_ref, s_out.at[me_id], local_sem).wait()

        return start, None, finish

    return _run_exchange("small_grad_all_gather", _Exchange(
        [small], [_sds((N_DEV,) + small.shape, small.dtype)],
        [pltpu.SemaphoreType.DMA((7,)), pltpu.SemaphoreType.DMA((7,)), pltpu.SemaphoreType.DMA], plan))[0]


def _pair_sum(name, g, got, core, tr):
    n_chip, _, R, C = g.shape

    def body(c_ref, a_ref, b_ref, o_ref):
        del c_ref
        o_ref[...] = (a_ref[...].astype(F32) + b_ref[...].astype(F32)).astype(o_ref.dtype)

    return pl.pallas_call(
        body, name=name,
        grid_spec=pltpu.PrefetchScalarGridSpec(
            num_scalar_prefetch=1, grid=(n_chip, R // tr),
            in_specs=[pl.BlockSpec((None, None, tr, C), lambda ch, i, c_ref: (ch, c_ref[0], i, 0)),
                      pl.BlockSpec((None, tr, C), lambda ch, i, c_ref: (ch, i, 0))],
            out_specs=pl.BlockSpec((None, tr, C), lambda ch, i, c_ref: (ch, i, 0))),
        out_shape=_sds(got.shape, got.dtype),
        compiler_params=_cparams(("parallel", "parallel")),
    )(core, g, got)


def _chip_exchange(parts):
    ng = len(parts)

    def plan(p_refs, out_refs, sems):
        send_sems, recv_sems, local_sems = sems
        x, y, c = _position()
        my_chip = 2 * x + y

        def copies():
            local = [pltpu.make_async_copy(p_refs[w].at[my_chip], out_refs[w].at[my_chip], local_sems.at[w])
                     for w in range(ng)]
            remote = [pltpu.make_async_remote_copy(
                src_ref=p_refs[w].at[2 * cx + cy], dst_ref=out_refs[w].at[my_chip],
                send_sem=send_sems.at[3 * w + k], recv_sem=recv_sems.at[3 * w + k],
                device_id=(cx, cy, c), device_id_type=MESH)
                for w in range(ng) for k, (cx, cy) in enumerate(_other_chips(x, y))]
            return local + remote

        def start():
            for cp in copies():
                cp.start()

        def finish():
            for cp in copies():
                cp.wait()

        return start, None, finish

    return _Exchange(parts, [_sds(p.shape, p.dtype) for p in parts],
                     [pltpu.SemaphoreType.DMA((3 * ng,)), pltpu.SemaphoreType.DMA((3 * ng,)),
                      pltpu.SemaphoreType.DMA((ng,))], plan)


def _adamw(name, parts, w, m, v, tr):
    n_parts = parts.shape[0]
    R, C = w.shape

    def body(p_ref, w_ref, m_ref, v_ref, g_ref, d_ref, mo_ref, vo_ref):
        g = p_ref[0].astype(F32)
        for i in range(1, n_parts):
            g = g + p_ref[i].astype(F32)
        m_new = ADAM_B1 * m_ref[...] + (1.0 - ADAM_B1) * g
        v_new = ADAM_B2 * v_ref[...] + (1.0 - ADAM_B2) * (g * g)
        m_hat = m_new / (1.0 - ADAM_B1 ** ADAM_STEP)
        v_hat = v_new / (1.0 - ADAM_B2 ** ADAM_STEP)
        g_ref[...] = g
        d_ref[...] = -ADAM_LR * (m_hat / (jnp.sqrt(v_hat) + ADAM_EPS) + ADAM_WD * w_ref[...])
        mo_ref[...] = m_new
        vo_ref[...] = v_new

    tile = pl.BlockSpec((tr, C), lambda i: (i, 0))
    return pl.pallas_call(
        body, name=name, grid=(R // tr,),
        in_specs=[pl.BlockSpec((n_parts, tr, C), lambda i: (0, i, 0)), tile, tile, tile],
        out_specs=[tile] * 4,
        out_shape=[_sds((R, C), F32)] * 4,
        compiler_params=_cparams(("parallel",)),
    )(parts, w, m, v)


def _flat_small(arrs):
    return jnp.concatenate([a.reshape(-1) for a in arrs]).reshape(SMALL_ROWS, LANES)


def _cols(g):
    return g.transpose(1, 0, 2).reshape(g.shape[1], -1)


def _rows_of(g):
    return g.reshape(-1, g.shape[2])


def _by_owner_cols(g):
    rows, cols = g.shape
    return g.reshape(rows, N_DEV, cols // N_DEV).transpose(1, 0, 2)


def _chip_core(g):
    return g.reshape((N_CHIP, 2) + g.shape[1:])


def kernel(x, norm1_g, w_in, q_norm_g, k_norm_g, ret_gn_g, ret_gn_b, w_proj_a, w_proj_b, w_out, norm2_g, w_up, w_down, loss_target, m_norm1_g, m_w_in, m_q_norm_g, m_k_norm_g, m_ret_gn_g, m_ret_gn_b, m_w_proj_a, m_w_proj_b, m_w_out, m_norm2_g, m_w_up, m_w_down, v_norm1_g, v_w_in, v_q_norm_g, v_k_norm_g, v_ret_gn_g, v_ret_gn_b, v_w_proj_a, v_w_proj_b, v_w_out, v_norm2_g, v_w_up, v_w_down):
    big_w = (w_in, w_proj_a, w_proj_b, w_out, w_up, w_down)
    big_m = (m_w_in, m_w_proj_a, m_w_proj_b, m_w_out, m_w_up, m_w_down)
    big_v = (v_w_in, v_w_proj_a, v_w_proj_b, v_w_out, v_w_up, v_w_down)
    small_w = (norm1_g, q_norm_g, k_norm_g, ret_gn_g, ret_gn_b, norm2_g)
    small_m = (m_norm1_g, m_q_norm_g, m_k_norm_g, m_ret_gn_g, m_ret_gn_b, m_norm2_g)
    small_v = (v_norm1_g, v_q_norm_g, v_k_norm_g, v_ret_gn_g, v_ret_gn_b, v_norm2_g)

    shards = [w[0].astype(BF16) for w in big_w]
    core = lax.axis_index("c").astype(jnp.int32).reshape(1)
    loss_local, grad_x, parts, small_g = _step(
        x[0], loss_target[0], norm1_g, q_norm_g[0], k_norm_g[0], ret_gn_g, ret_gn_b, norm2_g, shards, core)
    small_all = _small_all_gather(_flat_small(small_g))
    names = ("w_in", "w_proj_a", "w_proj_b", "w_out", "w_up", "w_down")

    res = {}
    for n, p, w, m, v in zip(names, parts, big_w, big_m, big_v):
        outs = _adamw(f"adamw_{n}", p, w[0], m[0], v[0], min(128, w.shape[1]))
        res[n] = [o[None] for o in outs]
    s_outs = _adamw("adamw_small", small_all, _flat_small(small_w), _flat_small(small_m), _flat_small(small_v), SMALL_ROWS)
    small_names = ("norm1_g", "q_norm_g", "k_norm_g", "ret_gn_g", "ret_gn_b", "norm2_g")
    for n in small_names:
        res[n] = []
    for o in s_outs:
        flat, off = o.reshape(-1), 0
        for n, w in zip(small_names, small_w):
            res[n].append(flat[off:off + w.size].reshape(w.shape))
            off += w.size

    order = ("norm1_g", "w_in", "q_norm_g", "k_norm_g", "ret_gn_g", "ret_gn_b", "w_proj_a", "w_proj_b", "w_out",
             "norm2_g", "w_up", "w_down")
    loss = lax.psum(loss_local, MESH_AXES)
    return (loss, grad_x[None], *[res[n][0] for n in order], *[res[n][1] for n in order],
            *[res[n][2] for n in order], *[res[n][3] for n in order])
```

```python
import math

import numpy as np
import jax
import jax.numpy as jnp
from jax import lax
from jax.experimental import pallas as pl
from jax.experimental.pallas import tpu as pltpu

F32 = jnp.float32
BF16 = jnp.bfloat16

D_MODEL = 1024
ATT_GROUPS = ((128, 1), (512, 4), (2048, 16))
ATT_BLOCKS_PER_STEP = (32, 8, 2)
ATT_TOGETHER = 4
HPG = 4
ATT_HEADS = 12
DH = 128
BLK = 128
ATT_W = ATT_HEADS * DH
ATT_OUT_W = HPG * DH
RET_HEADS = 4
RET_QK = 256
RET_V = 512
RET_QK_W = RET_HEADS * RET_QK
RET_V_W = RET_HEADS * RET_V
CHUNK = 128
RET_PER_STEP = 4
D_FF = 4096
IN_W = 12800
REST_W = IN_W - 3 * ATT_W
EPS = 1e-6
ADAM_LR, ADAM_B1, ADAM_B2, ADAM_EPS, ADAM_WD, ADAM_STEP = 0.001, 0.9, 0.999, 1e-08, 0.01, 10
N_DEV = 8
N_CHIP = 4
MESH_AXES = ("x", "y", "c")
MESH = pl.DeviceIdType.MESH
VMEM_LIMIT = 56 * 1024 * 1024
LANES = 128
NEG = -1e30

_NN = (((1,), (0,)), ((), ()))
_NT = (((1,), (1,)), ((), ()))
_TN = (((0,), (0,)), ((), ()))

R_Q, R_K, R_V, R_G, R_GA, R_GB = 0, 1024, 2048, 4096, 6144, 7168

LOG_GAMMA = [float(v) for v in np.log(1.0 - 2.0 ** (-5.0 - np.arange(RET_HEADS, dtype=np.float32))).astype(np.float32)]
ALIBI = np.asarray(2.0 ** (-8.0 * np.arange(1, ATT_HEADS + 1, dtype=np.float32) / ATT_HEADS), np.float32)

SMALL_ROWS = (1024 + 1536 + 1536 + 2048 + 2048 + 1024) // LANES


def _dot(a, b, dims=_NN):
    return lax.dot_general(a, b, dims, preferred_element_type=F32)


def _cparams(sem):
    return pltpu.CompilerParams(dimension_semantics=sem, vmem_limit_bytes=VMEM_LIMIT)


def _sds(shape, dtype):
    return jax.ShapeDtypeStruct(shape, dtype)


class _Exchange:
    def __init__(self, ins, out_shapes, sems, plan):
        self.ins, self.out_shapes, self.sems, self.plan = list(ins), list(out_shapes), list(sems), plan

    def split(self, in_refs, out_refs, sem_refs):
        return self.plan(in_refs, out_refs, sem_refs)


def _mm(name, a, b, mode, tm, tn, tk, outs, epi, extras=(), b_pro=None,
        sem=("parallel", "parallel", "arbitrary"), exchange=None, in_place=None):
    if mode == "nn":
        (M, K), (_, N) = a.shape, b.shape
        a_spec = pl.BlockSpec((tm, tk), lambda i, j, k: (i, k))
        dims = _NN
    else:
        (K, M), (_, N) = a.shape, b.shape
        a_spec = pl.BlockSpec((tk, tm), lambda i, j, k: (k, i))
        dims = _TN
    assert M % tm == 0 and N % tn == 0 and K % tk == 0, (name, M, N, K, tm, tn, tk)
    nk = K // tk
    whole_b = dict(pipeline_mode=pl.Buffered(1)) if (nk == 1 and N == tn) else {}
    b_spec = pl.BlockSpec((tk, tn), lambda i, j, k: (k, j), **whole_b)
    n_ex, n_out = len(extras), len(outs)
    grid = (M // tm, N // tn, nk)
    n_xi = len(exchange.ins) if exchange else 0
    n_xo = len(exchange.out_shapes) if exchange else 0
    n_acc = 1 if nk > 1 else 0

    n_ip = 1 if in_place else 0

    def body(a_ref, b_ref, *rest):
        ex, rest = rest[:n_ex], rest[n_ex:]
        x_in, rest = rest[:n_xi], rest[n_xi + n_ip:]
        out, rest = rest[:n_out], rest[n_out:]
        x_out, rest = rest[:n_xo], rest[n_xo:]
        step =(pl.program_id(0) * grid[1] + pl.program_id(1)) * grid[2] + pl.program_id(2)
        n_steps = grid[0] * grid[1] * grid[2]
        if exchange:
            start, relay, finish = exchange.split(x_in, x_out, rest[n_acc:])
            pl.when(step == 0)(start)
        bv = b_ref[...]
        if b_pro is not None:
            bv = b_pro(bv)
        part = _dot(a_ref[...].astype(BF16), bv.astype(BF16), dims)
        if nk == 1:
            epi(part, ex, out)
        else:
            acc_ref = rest[0]
            k = pl.program_id(2)

            @pl.when(k == 0)
            def _():
                acc_ref[...] = part

            @pl.when(k > 0)
            def _():
                acc_ref[...] += part

            @pl.when(k == nk - 1)
            def _():
                epi(acc_ref[...], ex, out)
        if exchange:
            if relay is not None:
                pl.when(step == (7 * n_steps) // 8)(relay)
            pl.when(step == n_steps - 1)(finish)

    res = pl.pallas_call(
        body,
        name=name,
        grid=grid,
        in_specs=[a_spec, b_spec] + [s for _, s in extras] + [_ANY] * (n_xi + n_ip),
        out_specs=[s for _, s in outs] + [_ANY] * n_xo,
        out_shape=[o for o, _ in outs] + (exchange.out_shapes if exchange else []),
        scratch_shapes=([pltpu.VMEM((tm, tn), F32)] if nk > 1 else []) + (exchange.sems if exchange else []),
        input_output_aliases={2 + n_ex + n_xi: in_place[1]} if in_place else {},
        compiler_params=_cparams(("arbitrary",) * 3 if exchange else sem),
    )(a, b, *[e for e, _ in extras], *(exchange.ins if exchange else []), *([in_place[0]] if in_place else []))
    return (res[:n_out], res[n_out:]) if exchange else res


def _tile_ij(tm, tn):
    return pl.BlockSpec((tm, tn), lambda i, j, k: (i, j))


def _epi_store(dtype):
    def epi(acc, ex, out):
        out[0][...] = acc.astype(dtype)
    return epi


def _rms_rows(x):
    return lax.rsqrt(jnp.mean(x * x, axis=-1, keepdims=True) + EPS)


def _acc_rows(ref, part, first):
    @pl.when(first)
    def _():
        ref[...] = part

    @pl.when(jnp.logical_not(first))
    def _():
        ref[...] += part


def _rmsnorm_fwd(x, g, tm, exchange):
    S, Dm = x.shape
    n_steps = S // tm
    n_xi, n_xo = len(exchange.ins), len(exchange.out_shapes)

    def body(x_ref, g_ref, *rest):
        x_in, o_ref, x_out, sems = rest[:n_xi], rest[n_xi], rest[n_xi + 1:n_xi + 1 + n_xo], rest[n_xi + 1 + n_xo:]
        start, relay, finish = exchange.split(x_in, x_out, sems)
        step = pl.program_id(0)
        pl.when(step == 0)(start)
        xv = x_ref[...]
        o_ref[...] = (xv * _rms_rows(xv) * g_ref[...]).astype(BF16)

        @pl.when(step == n_steps - 1)
        def _():
            relay()
            finish()

    res = pl.pallas_call(
        body, name="rmsnorm1_fwd", grid=(n_steps,),
        in_specs=[pl.BlockSpec((tm, Dm), lambda i: (i, 0)), pl.BlockSpec((1, Dm), lambda i: (0, 0))] + [_ANY] * n_xi,
        out_specs=[pl.BlockSpec((tm, Dm), lambda i: (i, 0))] + [_ANY] * n_xo,
        out_shape=[_sds((S, Dm), BF16)] + exchange.out_shapes,
        scratch_shapes=exchange.sems,
        compiler_params=_cparams(("arbitrary",)),
    )(x, g, *exchange.ins)
    return res[0], res[1:]


def _rows(ref, r, b, d):
    if d == 1:
        return ref[b * BLK:(b + 1) * BLK, :]
    return ref[pl.ds(b * BLK * d + r, BLK, stride=d), :]


def _put_rows(ref, r, b, d, val):
    if d == 1:
        ref[b * BLK:(b + 1) * BLK, :] = val
    else:
        ref[pl.ds(b * BLK * d + r, BLK, stride=d), :] = val


def _head_norm(x, g):
    r = _rms_rows(x)
    xh = x * r
    return (xh * g).astype(BF16), xh, r


def _head_norm_bwd(dyn, xh, r, g):
    dxh = dyn * g
    dx = r * (dxh - xh * jnp.mean(dxh * xh, axis=-1, keepdims=True))
    return dx, jnp.sum(dyn * xh, axis=0, keepdims=True)


def _att_mask_bias(slope, d, first):
    qi = lax.broadcasted_iota(jnp.int32, (BLK, 2 * BLK), 0)
    kj = lax.broadcasted_iota(jnp.int32, (BLK, 2 * BLK), 1)
    dist = BLK + qi - kj
    valid = (dist >= 0) & (dist <= BLK)
    if first is not None:
        valid = valid & (jnp.logical_not(first) | (kj >= BLK))
    bias = -slope * (dist * d).astype(F32)
    return valid, bias


def _att_specs(gi, d, nb, S):
    span = BLK * d
    sb = span * nb
    nspan = S // span
    before = lambda n: jnp.maximum(n * nb - 1, 0)
    after = lambda n: jnp.minimum((n + 1) * nb, nspan - 1)
    zcol = lambda j, kind: 3 * (gi * HPG + j) + kind
    cur = lambda kind: pl.BlockSpec((sb, DH), lambda j, n: (n, zcol(j, kind)))
    prev = lambda kind: pl.BlockSpec((span, DH), lambda j, n: (before(n), zcol(j, kind)))
    nxt = lambda kind: pl.BlockSpec((span, DH), lambda j, n: (after(n), zcol(j, kind)))
    slot = pl.BlockSpec((sb, DH), lambda j, n: (n, j))
    slot_next = pl.BlockSpec((span, DH), lambda j, n: (after(n), j))
    head = pl.BlockSpec((None, 1, DH), lambda j, n: (gi * HPG + j, 0, 0))
    return cur, prev, nxt, slot, slot_next, head


def _att_fwd_group(z_att, gq3, gk3, slopes3, gi, d, nb, others=()):
    S = z_att.shape[0]
    nsb = S // (BLK * d * nb)
    scale = DH ** -0.5
    n_other = len(others)

    def body(q_ref, k_ref, v_ref, kp_ref, vp_ref, gq_ref, gk_ref, sl_ref, *rest):
        other_refs, (o_ref, l_ref) = rest[:2 * n_other], rest[2 * n_other:]
        slope = sl_ref[...][:, :1]
        valid0, bias = _att_mask_bias(slope, d, pl.program_id(1) == 0)
        valid_in, _ = _att_mask_bias(slope, d, None)
        gq, gk = gq_ref[...], gk_ref[...]
        memo = {}

        def get(kind, r, b):
            if (kind, r, b) not in memo:
                if kind == "k":
                    val = _head_norm(_rows(k_ref if b >= 0 else kp_ref, r, max(b, 0), d), gk)[0]
                else:
                    val = _rows(v_ref if b >= 0 else vp_ref, r, max(b, 0), d).astype(BF16)
                memo[(kind, r, b)] = val
            return memo[(kind, r, b)]

        units = [(r, b) for r in range(d) for b in range(nb)]
        for c0 in range(0, len(units), ATT_TOGETHER):
            us = units[c0:c0 + ATT_TOGETHER]
            q = [_head_norm(_rows(q_ref, r, b, d), gq)[0] for r, b in us]
            k2 = [jnp.concatenate([get("k", r, b - 1), get("k", r, b)], axis=0) for r, b in us]
            v2 = [jnp.concatenate([get("v", r, b - 1), get("v", r, b)], axis=0) for r, b in us]
            s = [jnp.where(valid0 if b == 0 else valid_in, _dot(q[i], k2[i], _NT) * scale + bias, NEG)
                 for i, (r, b) in enumerate(us)]
            m = [jnp.max(si, axis=-1, keepdims=True) for si in s]
            p = [jnp.exp(si - mi) for si, mi in zip(s, m)]
            den = [jnp.sum(pi, axis=-1, keepdims=True) for pi in p]
            o = [_dot(pi.astype(BF16), vi) / di for pi, vi, di in zip(p, v2, den)]
            for i, (r, b) in enumerate(us):
                _put_rows(o_ref, r, b, d, o[i])
                _put_rows(l_ref, r, b, d, jnp.broadcast_to(m[i] + jnp.log(den[i]), (BLK, DH)))
        if n_other:
            os_ = [ref[...] for ref in other_refs[:n_other]] + [o_ref[...]]
            ls_ = [ref[...] for ref in other_refs[n_other:]] + [l_ref[...]]
            m = ls_[0]
            for l in ls_[1:]:
                m = jnp.maximum(m, l)
            es = [jnp.exp(l - m) for l in ls_]
            tot, mix = es[0], es[0] * os_[0]
            for e, o in zip(es[1:], os_[1:]):
                tot, mix = tot + e, mix + e * o
            o_ref[...] = mix / tot
            l_ref[...] = m + jnp.log(tot)

    cur, prev, _, slot, _, head = _att_specs(gi, d, nb, S)
    return pl.pallas_call(
        body, name=f"att_fwd_g{gi}", grid=(HPG, nsb),
        in_specs=[cur(0), cur(1), cur(2), prev(1), prev(2), head, head, head] + [slot] * (2 * n_other),
        out_specs=[slot, slot],
        out_shape=[_sds((S, ATT_OUT_W), F32), _sds((S, ATT_OUT_W), F32)],
        compiler_params=_cparams(("parallel", "arbitrary")),
    )(z_att, z_att, z_att, z_att, z_att, gq3, gk3, slopes3, *[o for o, _ in others], *[l for _, l in others])


def _ret_tables(lg):
    ri = lax.broadcasted_iota(jnp.int32, (CHUNK, CHUNK), 0)
    ci = lax.broadcasted_iota(jnp.int32, (CHUNK, CHUNK), 1)
    diff = (ri - ci).astype(F32)
    decay = jnp.where(diff >= 0, jnp.exp(lg * jnp.maximum(diff, 0.0)), 0.0)
    idx = lax.broadcasted_iota(jnp.int32, (CHUNK, 1), 0).astype(F32)
    xi = jnp.exp(lg * (idx + 1.0))
    zeta = jnp.exp(lg * (CHUNK - 1.0 - idx))
    return decay, xi, zeta, math.exp(lg * CHUNK)


def _ret_specs(nsteps, rev):
    idx = (lambda n: nsteps - 1 - n) if rev else (lambda n: n)
    rows = CHUNK * RET_PER_STEP
    qk = lambda off: pl.BlockSpec((rows, RET_QK_W), lambda n: (idx(n), off // RET_QK_W))
    vv = lambda off: pl.BlockSpec((rows, RET_V_W), lambda n: (idx(n), off // RET_V_W))
    par = pl.BlockSpec((1, RET_V_W), lambda n: (0, 0))
    wide = pl.BlockSpec((rows, RET_V_W), lambda n: (idx(n), 0))
    st = pl.BlockSpec((RET_HEADS, RET_PER_STEP, RET_QK, RET_V), lambda n: (0, idx(n), 0, 0))
    return qk, vv, par, wide, st


def _ret_fwd(z_rest, gn_g, gn_b):
    S = z_rest.shape[0]
    nch = S // CHUNK
    nsteps = nch // RET_PER_STEP

    def body(q_ref, k_ref, v_ref, gr_ref, g_ref, b_ref, or_ref, o_ref, st_ref, state):
        @pl.when(pl.program_id(0) == 0)
        def _():
            state[...] = jnp.zeros_like(state)

        for c in range(RET_PER_STEP):
            rc = slice(c * CHUNK, (c + 1) * CHUNK)
            for h in range(RET_HEADS):
                decay, xi, zeta, gch = _ret_tables(LOG_GAMMA[h])
                cq = slice(h * RET_QK, (h + 1) * RET_QK)
                cv = slice(h * RET_V, (h + 1) * RET_V)
                q = q_ref[rc, cq]
                kc32 = k_ref[rc, cq].astype(F32) * (RET_QK ** -0.5)
                kc = kc32.astype(BF16)
                v = v_ref[rc, cv]
                st = state[h]
                stb = st.astype(BF16)
                st_ref[h, c] = stb
                s = _dot(q, kc, _NT) * decay
                o = _dot(s.astype(BF16), v) + _dot(q, stb) * xi
                state[h] = st * gch + _dot((kc32 * zeta).astype(BF16), v, _TN)
                mu = jnp.mean(o, axis=-1, keepdims=True)
                cen = o - mu
                yh = cen * lax.rsqrt(jnp.mean(cen * cen, axis=-1, keepdims=True) + EPS)
                gr = gr_ref[rc, cv].astype(F32)
                or_ref[rc, cv] = ((yh * g_ref[:, cv] + b_ref[:, cv]) * (gr * jax.nn.sigmoid(gr))).astype(BF16)
                o_ref[rc, cv] = o

    qk, vv, par, wide, st = _ret_specs(nsteps, False)
    return pl.pallas_call(
        body, name="ret_fwd", grid=(nsteps,),
        in_specs=[qk(R_Q), qk(R_K), vv(R_V), vv(R_G), par, par],
        out_specs=[wide, wide, st],
        out_shape=[_sds((S, RET_V_W), BF16), _sds((S, RET_V_W), F32), _sds((RET_HEADS, nch, RET_QK, RET_V), BF16)],
        scratch_shapes=[pltpu.VMEM((RET_HEADS, RET_QK, RET_V), F32)],
        compiler_params=_cparams(("arbitrary",)),
    )(z_rest, z_rest, z_rest, z_rest, gn_g, gn_b)


def _merge_fwd(o_a, o_r, z_rest, x, wpa, wpb, wout, g2, tm):
    S = x.shape[0]

    def body(oa_ref, or_ref, ga_ref, gb_ref, x_ref, wpa_ref, wpb_ref, wo_ref, g2_ref,
             x1_ref, y_ref, pa_ref, pb_ref, xn2_ref):
        pa = _dot(oa_ref[...].astype(BF16), wpa_ref[...])
        pb = _dot(or_ref[...], wpb_ref[...])
        y = jax.nn.sigmoid(ga_ref[...].astype(F32)) * pa + jax.nn.sigmoid(gb_ref[...].astype(F32)) * pb
        yb = y.astype(BF16)
        x1 = x_ref[...] + _dot(yb, wo_ref[...])
        x1_ref[...] = x1
        y_ref[...] = yb
        pa_ref[...] = pa.astype(BF16)
        pb_ref[...] = pb.astype(BF16)
        xn2_ref[...] = (x1 * _rms_rows(x1) * g2_ref[...]).astype(BF16)

    row = lambda w: pl.BlockSpec((tm, w), lambda i: (i, 0))
    full = lambda a: pl.BlockSpec(a.shape, lambda i: (0, 0))
    return pl.pallas_call(
        body, name="merge_fwd", grid=(S // tm,),
        in_specs=[row(ATT_OUT_W), row(RET_V_W),
                  pl.BlockSpec((tm, D_MODEL), lambda i: (i, R_GA // D_MODEL)),
                  pl.BlockSpec((tm, D_MODEL), lambda i: (i, R_GB // D_MODEL)),
                  row(D_MODEL), full(wpa), full(wpb), full(wout), full(g2)],
        out_specs=[row(D_MODEL)] * 5,
        out_shape=[_sds((S, D_MODEL), F32)] + [_sds((S, D_MODEL), BF16)] * 4,
        compiler_params=_cparams(("parallel",)),
    )(o_a, o_r, z_rest, z_rest, x, wpa, wpb, wout, g2)


def _rms_bwd(dy, xv, g):
    r = _rms_rows(xv)
    xh = xv * r
    dg = dy * g
    dx = r * (dg - xh * jnp.mean(dg * xh, axis=-1, keepdims=True))
    return dx, jnp.sum(dy * xh, axis=0, keepdims=True)


def _norm_gate_bwd(dout, o, gr, gam, bet):
    mean = lambda xs: [jnp.mean(x, axis=-1, keepdims=True) for x in xs]
    cen = [x - m for x, m in zip(o, mean(o))]
    rstd = [lax.rsqrt(v + EPS) for v in mean([c * c for c in cen])]
    yh = [c * r for c, r in zip(cen, rstd)]
    y = [a * g + b for a, g, b in zip(yh, gam, bet)]
    sg = [jax.nn.sigmoid(g) for g in gr]
    dy = [d * (g * s) for d, g, s in zip(dout, gr, sg)]
    dgr = [d * a * (s * (1.0 + g * (1.0 - s))) for d, a, s, g in zip(dout, y, sg, gr)]
    dyh = [d * g for d, g in zip(dy, gam)]
    m1, m2 = mean(dyh), mean([a * b for a, b in zip(dyh, yh)])
    do = [r * (d - a - h * b) for r, d, a, h, b in zip(rstd, dyh, m1, yh, m2)]
    dg = [jnp.sum(d * h, axis=0, keepdims=True) for d, h in zip(dy, yh)]
    db = [jnp.sum(d, axis=0, keepdims=True) for d in dy]
    return do, dgr, dg, db


def _ret_bwd(do, states, z_rest, dz_rest):
    S = z_rest.shape[0]
    nsteps = S // CHUNK // RET_PER_STEP

    def body(do_ref, st_ref, q_ref, k_ref, v_ref, dz_in, dz_ref, gst):
        del dz_in

        @pl.when(pl.program_id(0) == 0)
        def _():
            gst[...] = jnp.zeros_like(gst)

        for c in reversed(range(RET_PER_STEP)):
            rc = slice(c * CHUNK, (c + 1) * CHUNK)
            for h in range(RET_HEADS):
                decay, xi, zeta, gch = _ret_tables(LOG_GAMMA[h])
                cq = slice(h * RET_QK, (h + 1) * RET_QK)
                cv = slice(h * RET_V, (h + 1) * RET_V)
                q = q_ref[rc, cq]
                kc32 = k_ref[rc, cq].astype(F32) * (RET_QK ** -0.5)
                kc = kc32.astype(BF16)
                v = v_ref[rc, cv]
                dob = do_ref[rc, cv]
                a = (_dot(q, kc, _NT) * decay).astype(BF16)
                da = (_dot(dob, v, _NT) * decay).astype(BF16)
                dcross = (dob.astype(F32) * xi).astype(BF16)
                g_next = gst[h]
                gb = g_next.astype(BF16)
                dq = _dot(da, kc) + _dot(dcross, st_ref[h, c], _NT)
                dkc = _dot(da, q, _TN)
                dkz = _dot(v, gb, _NT)
                dv = _dot(a, dob, _TN) + _dot((kc32 * zeta).astype(BF16), gb)
                gst[h] = g_next * gch + _dot(q, dcross, _TN)
                dz_ref[rc, R_Q + h * RET_QK:R_Q + (h + 1) * RET_QK] = dq.astype(BF16)
                dz_ref[rc, R_K + h * RET_QK:R_K + (h + 1) * RET_QK] = (
                    (dkc + dkz * zeta) * (RET_QK ** -0.5)).astype(BF16)
                dz_ref[rc, R_V + h * RET_V:R_V + (h + 1) * RET_V] = dv.astype(BF16)

    qk, vv, _, wide, st = _ret_specs(nsteps, True)
    return pl.pallas_call(
        body, name="ret_bwd", grid=(nsteps,),
        in_specs=[wide, st, qk(R_Q), qk(R_K), vv(R_V), _ANY],
        out_specs=pl.BlockSpec((CHUNK * RET_PER_STEP, R_G), lambda n: (nsteps - 1 - n, 0)),
        out_shape=_sds(dz_rest.shape, BF16),
        input_output_aliases={5: 0},
        scratch_shapes=[pltpu.VMEM((RET_HEADS, RET_QK, RET_V), F32)],
        compiler_params=_cparams(("arbitrary",)),
    )(do, states, z_rest, z_rest, z_rest, dz_rest)


def _att_probs(q, k, lmix, valid, bias):
    s = _dot(q, k, _NT) * (DH ** -0.5) + bias
    return jnp.where(valid, jnp.exp(jnp.where(valid, s, NEG) - lmix), 0.0)


def _att_bwd_group(z_att, dz_att, do_a, delta_lmix, gq3, gk3, slopes3, gi, d, nb, exchange=None):
    S = z_att.shape[0]
    sb = BLK * d * nb
    nsb = S // sb
    scale = DH ** -0.5
    aliased = dz_att is not None
    n_xi = len(exchange.ins) if exchange else 0
    n_xo = len(exchange.out_shapes) if exchange else 0

    def body(q_ref, k_ref, v_ref, kp_ref, vp_ref, qn_ref, do_ref, don_ref, dl_ref, dln_ref,
             gq_ref, gk_ref, sl_ref, *rest):
        rest = rest[1:] if aliased else rest
        x_in, (dz_ref, dgq_ref, dgk_ref), rest = rest[:n_xi], rest[n_xi:n_xi + 3], rest[n_xi + 3:]
        x_out, stage, x_sems = rest[:n_xo], rest[n_xo], rest[n_xo + 1:]
        n = pl.program_id(1)
        step = pl.program_id(0) * nsb + n
        if exchange:
            start, _, finish = exchange.split(x_in, x_out, x_sems)
            pl.when(step == 0)(start)
        slope = sl_ref[...][:, :1]
        valid0, bias = _att_mask_bias(slope, d, n == 0)
        valid_in, _ = _att_mask_bias(slope, d, None)
        qi = lax.broadcasted_iota(jnp.int32, (BLK, BLK), 0)
        kj = lax.broadcasted_iota(jnp.int32, (BLK, BLK), 1)
        dist_n = BLK + qi - kj
        valid_n_in = dist_n <= BLK
        valid_n_last = valid_n_in & (n < nsb - 1)
        bias_n = -slope * (dist_n * d).astype(F32)
        gq, gk = gq_ref[...], gk_ref[...]
        dgq = jnp.zeros((1, DH), F32)
        dgk = jnp.zeros((1, DH), F32)
        memo = {}

        def get(kind, r, b):
            if (kind, r, b) not in memo:
                inner = 0 <= b < nb
                bb = b if inner else 0
                if kind == "q":
                    val = _head_norm(_rows(q_ref if inner else qn_ref, r, bb, d), gq)
                elif kind == "k":
                    val = _head_norm(_rows(k_ref if inner else kp_ref, r, bb, d), gk)
                elif kind == "v":
                    val = _rows(v_ref if inner else vp_ref, r, bb, d).astype(BF16)
                elif kind == "do":
                    val = _rows(do_ref if inner else don_ref, r, bb, d).astype(BF16)
                elif kind == "dl+lm":
                    val = _rows(dl_ref if inner else dln_ref, r, bb, d)
                elif kind == "dl":
                    val = get("dl+lm", r, b)[:, :1]
                else:
                    val = get("dl+lm", r, b)[:, DH // 2:DH // 2 + 1]
                memo[(kind, r, b)] = val
            return memo[(kind, r, b)]

        units = [(r, b) for r in range(d) for b in range(nb)]
        for c0 in range(0, len(units), ATT_TOGETHER):
            us = units[c0:c0 + ATT_TOGETHER]
            k2 = [jnp.concatenate([get("k", r, b - 1)[0], get("k", r, b)[0]], axis=0) for r, b in us]
            v2 = [jnp.concatenate([get("v", r, b - 1), get("v", r, b)], axis=0) for r, b in us]
            p = [_att_probs(get("q", r, b)[0], k2[i], get("lm", r, b), valid0 if b == 0 else valid_in, bias)
                 for i, (r, b) in enumerate(us)]
            dp = [_dot(get("do", r, b), v2[i], _NT) for i, (r, b) in enumerate(us)]
            ds = [(p[i] * (dp[i] - get("dl", r, b)) * scale).astype(BF16) for i, (r, b) in enumerate(us)]
            dq = [_dot(ds[i], k2[i]) for i in range(len(us))]
            p_n = [_att_probs(get("q", r, b + 1)[0], get("k", r, b)[0], get("lm", r, b + 1),
                              valid_n_last if b == nb - 1 else valid_n_in, bias_n) for r, b in us]
            dp_n = [_dot(get("do", r, b + 1), get("v", r, b), _NT) for r, b in us]
            ds_n = [(p_n[i] * (dp_n[i] - get("dl", r, b + 1)) * scale).astype(BF16) for i, (r, b) in enumerate(us)]
            dk = [_dot(ds[i][:, BLK:], get("q", r, b)[0], _TN) + _dot(ds_n[i], get("q", r, b + 1)[0], _TN)
                  for i, (r, b) in enumerate(us)]
            dv = [_dot(p[i][:, BLK:].astype(BF16), get("do", r, b), _TN)
                  + _dot(p_n[i].astype(BF16), get("do", r, b + 1), _TN) for i, (r, b) in enumerate(us)]
            for i, (r, b) in enumerate(us):
                _, qh, qr = get("q", r, b)
                _, kh, kr = get("k", r, b)
                dxq, dg_q = _head_norm_bwd(dq[i], qh, qr, gq)
                dxk, dg_k = _head_norm_bwd(dk[i], kh, kr, gk)
                _put_rows(stage.at[0], r, b, d, dxq)
                _put_rows(stage.at[1], r, b, d, dxk)
                _put_rows(stage.at[2], r, b, d, dv[i])
                dgq, dgk = dgq + dg_q, dgk + dg_k
        for kind in range(3):
            dz_ref[:, kind * DH:(kind + 1) * DH] = stage[kind].astype(BF16)
        _acc_rows(dgq_ref, dgq, n == 0)
        _acc_rows(dgk_ref, dgk, n == 0)
        if exchange:
            pl.when(step == HPG * nsb - 1)(finish)

    cur, prev, nxt, slot, slot_next, head = _att_specs(gi, d, nb, S)
    gain = pl.BlockSpec((None, 1, DH), lambda j, n: (j, 0, 0))
    in_specs = [cur(0), cur(1), cur(2), prev(1), prev(2), nxt(0),
                slot, slot_next, slot, slot_next, head, head, head]
    args = [z_att] * 6 + [do_a, do_a, delta_lmix, delta_lmix, gq3, gk3, slopes3]
    if aliased:
        in_specs.append(_ANY)
        args.append(dz_att)
    res = pl.pallas_call(
        body, name=f"att_bwd_g{gi}", grid=(HPG, nsb),
        in_specs=in_specs + [_ANY] * n_xi,
        out_specs=[pl.BlockSpec((sb, 3 * DH), lambda j, n: (n, gi * HPG + j)), gain, gain] + [_ANY] * n_xo,
        out_shape=[_sds(z_att.shape, BF16), _sds((HPG, 1, DH), F32), _sds((HPG, 1, DH), F32)]
        + (exchange.out_shapes if exchange else []),
        input_output_aliases={len(args) - 1: 0} if aliased else {},
        scratch_shapes=[pltpu.VMEM((3, sb, DH), F32)] + (exchange.sems if exchange else []),
        compiler_params=_cparams(("arbitrary", "arbitrary") if exchange else ("parallel", "arbitrary")),
    )(*args, *(exchange.ins if exchange else []))
    return res[:3], res[3:]


def _step(x, target, norm1_g, q_norm_g, k_norm_g, gn_g, gn_b, norm2_g, shards, core):
    S = x.shape[0]
    tm = min(512, S)
    ts = min(256, S)
    tk = min(2048, S)
    tk2 = min(4096, S)
    later_shards = shards[1:]
    gq3 = q_norm_g.reshape(ATT_HEADS, 1, DH)
    gk3 = k_norm_g.reshape(ATT_HEADS, 1, DH)
    slopes3 = jnp.asarray(np.broadcast_to(ALIBI[:, None, None], (ATT_HEADS, 1, DH)).copy())

    xn, (g_in,) = _rmsnorm_fwd(x, norm1_g, tm, _gather_exchange(shards[:1]))
    w_in = _cols(g_in)
    w_att = w_in[:, :3 * ATT_W].reshape(D_MODEL, 3, ATT_HEADS, DH).transpose(0, 2, 1, 3).reshape(D_MODEL, 3 * ATT_W)
    w_rest = w_in[:, 3 * ATT_W:]
    w_att_t, w_rest_t = w_att.T, w_rest.T
    z_att = _mm("in_proj_att", xn, w_att, "nn", tm, 3 * ATT_W, D_MODEL,
                [(_sds((S, 3 * ATT_W), F32), _tile_ij(tm, 3 * ATT_W))], _epi_store(F32))[0]
    (z_rest,), gathered = _mm("in_proj_rest", xn, w_rest, "nn", tm, REST_W, D_MODEL,
                              [(_sds((S, REST_W), BF16), _tile_ij(tm, REST_W))], _epi_store(BF16),
                              exchange=_gather_exchange(later_shards))
    w_pa, w_pb, w_out, w_up, w_down = [f(g) for f, g in zip((_cols, _rows_of, _rows_of, _cols, _rows_of), gathered)]
    w_up_t, w_down_t, w_out_t, w_pa_t, w_pb_t = w_up.T, w_down.T, w_out.T, w_pa.T, w_pb.T
    done = []
    for gi, ((_, d), nb) in enumerate(zip(ATT_GROUPS, ATT_BLOCKS_PER_STEP)):
        last = gi == len(ATT_GROUPS) - 1
        done.append(_att_fwd_group(z_att, gq3, gk3, slopes3, gi, d, nb, others=tuple(done) if last else ()))
    o_a, lmix = done[-1]
    o_r, o_pre, states = _ret_fwd(z_rest, gn_g, gn_b)
    x1, y, pa, pb, xn2 = _merge_fwd(o_a, o_r, z_rest, x, w_pa, w_pb, w_out, norm2_g, tm)

    def epi_up(acc, ex, out):
        r = jnp.maximum(acc, 0.0)
        out[0][...] = (r * r).astype(BF16)
        out[1][...] = r.astype(BF16)

    h, relu_u = _mm("mlp_up", xn2, w_up, "nn", tm, D_FF, D_MODEL,
                    [(_sds((S, D_FF), BF16), _tile_ij(tm, D_FF)), (_sds((S, D_FF), BF16), _tile_ij(tm, D_FF))], epi_up)

    def epi_down(acc, ex, out):
        diff = ex[0][...] + acc - ex[1][...]
        dx = diff * (1.0 / D_MODEL)
        out[0][...] = dx
        out[1][...] = jnp.broadcast_to(jnp.sum(diff * diff) * (1.0 / (8 * LANES)), (8, LANES))
        out[2][...] = dx.astype(BF16)

    row_tile = _tile_ij(tm, D_MODEL)
    dx2, loss_parts, dx2_b = _mm(
        "mlp_down_loss", h, w_down, "nn", tm, D_MODEL, D_FF,
        [(_sds((S, D_MODEL), F32), row_tile),
         (_sds((S // tm * 8, LANES), F32), pl.BlockSpec((8, LANES), lambda i, j, k: (i, 0))),
         (_sds((S, D_MODEL), BF16), row_tile)],
        epi_down, extras=[(x1, row_tile), (target, row_tile)])
    loss_local = jnp.sum(loss_parts) * (0.5 / D_MODEL)

    def epi_du(acc, ex, out):
        out[0][...] = (acc * (2.0 * ex[0][...].astype(F32))).astype(BF16)

    du = _mm("mlp_down_bwd", dx2_b, w_down_t, "nn", tm, D_FF, D_MODEL,
             [(_sds((S, D_FF), BF16), _tile_ij(tm, D_FF))], epi_du, extras=[(relu_u, _tile_ij(tm, D_FF))])[0]
    gw_down = _mm("gw_down", h, dx2_b, "tn", 1024, D_MODEL, tk2,
                  [(_sds((D_FF, D_MODEL), BF16), _tile_ij(1024, D_MODEL))], _epi_store(BF16))[0]
    gw_up = _mm("gw_up", xn2, du, "tn", D_MODEL, 512, tk2,
                [(_sds((N_DEV, D_MODEL, 512), BF16), pl.BlockSpec((None, D_MODEL, 512), lambda i, j, k: (j, 0, 0)))],
                _epi_store(BF16))[0]

    vec = pl.BlockSpec((1, D_MODEL), lambda i, j, k: (0, 0))
    seq_sem = ("arbitrary", "arbitrary", "arbitrary")

    def epi_norm2(acc, ex, out):
        dx, dg = _rms_bwd(acc, ex[0][...], ex[2][...])
        dx1_tile = ex[1][...] + dx
        out[0][...] = dx1_tile
        _acc_rows(out[1], dg, pl.program_id(0) == 0)
        out[2][...] = dx1_tile.astype(BF16)

    dx1, g_norm2, dx1_b = _mm(
        "mlp_up_bwd", du, w_up_t, "nn", tm, D_MODEL, D_FF,
        [(_sds((S, D_MODEL), F32), row_tile), (_sds((1, D_MODEL), F32), vec), (_sds((S, D_MODEL), BF16), row_tile)],
        epi_norm2, extras=[(x1, row_tile), (dx2, row_tile), (norm2_g, vec)], sem=seq_sem)

    def epi_dy(acc, ex, out):
        sa = jax.nn.sigmoid(ex[0][...].astype(F32))
        sb = jax.nn.sigmoid(ex[1][...].astype(F32))
        out[0][...] = (acc * sa).astype(BF16)
        out[1][...] = (acc * sb).astype(BF16)
        out[2][:, :D_MODEL] = (acc * ex[2][...].astype(F32) * (sa * (1.0 - sa))).astype(BF16)
        out[2][:, D_MODEL:] = (acc * ex[3][...].astype(F32) * (sb * (1.0 - sb))).astype(BF16)

    ga_spec = pl.BlockSpec((tm, D_MODEL), lambda i, j, k: (i, R_GA // D_MODEL))
    gb_spec = pl.BlockSpec((tm, D_MODEL), lambda i, j, k: (i, R_GB // D_MODEL))
    gates_spec = pl.BlockSpec((tm, 2 * D_MODEL), lambda i, j, k: (i, R_GA // (2 * D_MODEL)))
    dpa, dpb, dz_rest = _mm(
        "out_proj_bwd", dx1_b, w_out_t, "nn", tm, D_MODEL, D_MODEL,
        [(_sds((S, D_MODEL), BF16), row_tile), (_sds((S, D_MODEL), BF16), row_tile), (_sds((S, REST_W), BF16), gates_spec)],
        epi_dy, extras=[(z_rest, ga_spec), (z_rest, gb_spec), (pa, row_tile), (pb, row_tile)])
    gw_out = _mm("gw_out", y, dx1_b, "tn", D_MODEL, D_MODEL, tk2,
                 [(_sds((D_MODEL, D_MODEL), BF16), _tile_ij(D_MODEL, D_MODEL))], _epi_store(BF16))[0]
    gw_pa = _mm("gw_proj_a", o_a, dpa, "tn", ATT_OUT_W, D_MODEL, tk,
                [(_sds((ATT_OUT_W, D_MODEL), BF16), _tile_ij(ATT_OUT_W, D_MODEL))], _epi_store(BF16))[0]
    gw_pb = _mm("gw_proj_b", o_r, dpb, "tn", 1024, D_MODEL, tk2,
                [(_sds((RET_V_W, D_MODEL), BF16), _tile_ij(1024, D_MODEL))], _epi_store(BF16))[0]

    def epi_doa(acc, ex, out):
        out[0][...] = acc
        prod = acc * ex[0][...]
        delta = jnp.concatenate(
            [jnp.broadcast_to(jnp.sum(prod[:, j * DH:(j + 1) * DH], axis=-1, keepdims=True), (prod.shape[0], DH))
             for j in range(HPG)], axis=1)
        lane = lax.broadcasted_iota(jnp.int32, delta.shape, 1)
        out[1][...] = jnp.where(lane % DH < DH // 2, delta, ex[1][...])

    slot_tile = _tile_ij(tm, ATT_OUT_W)
    do_a, delta_lmix = _mm("proj_a_bwd", dpa, w_pa_t, "nn", tm, ATT_OUT_W, D_MODEL,
                           [(_sds((S, ATT_OUT_W), F32), slot_tile), (_sds((S, ATT_OUT_W), F32), slot_tile)],
                           epi_doa, extras=[(o_a, slot_tile), (lmix, slot_tile)])
    owed = [_chip_core(_by_owner_cols(gw_pa)), _chip_core(gw_pb.reshape(N_DEV, -1, D_MODEL)),
            _chip_core(gw_out.reshape(N_DEV, -1, D_MODEL)), _chip_core(gw_up), _chip_core(gw_down.reshape(N_DEV, -1, D_MODEL))]
    names = ("w_proj_a", "w_proj_b", "w_out", "w_up", "w_down")

    def epi_dor(acc, ex, out):
        cvs = [slice(h * RET_V, (h + 1) * RET_V) for h in range(RET_HEADS)]
        do, dgr, dg, db = _norm_gate_bwd([acc[:, cv] for cv in cvs], [ex[0][:, cv] for cv in cvs],
                                         [ex[1][:, cv].astype(F32) for cv in cvs],
                                         [ex[2][:, cv] for cv in cvs], [ex[3][:, cv] for cv in cvs])
        for h, cv in enumerate(cvs):
            out[0][:, cv] = do[h].astype(BF16)
            out[1][:, cv] = dgr[h].astype(BF16)
        first = pl.program_id(0) == 0
        _acc_rows(out[2], jnp.concatenate(dg, axis=1), first)
        _acc_rows(out[3], jnp.concatenate(db, axis=1), first)

    wide_tile = _tile_ij(tm, RET_V_W)
    gate_tile = pl.BlockSpec((tm, RET_V_W), lambda i, j, k: (i, R_G // RET_V_W))
    wide_vec = pl.BlockSpec((1, RET_V_W), lambda i, j, k: (0, 0))
    (do_ret, dz_rest, g_gn_g, g_gn_b), got = _mm(
        "proj_b_bwd", dpb, w_pb_t, "nn", tm, RET_V_W, D_MODEL,
        [(_sds((S, RET_V_W), BF16), wide_tile), (_sds((S, REST_W), BF16), gate_tile),
         (_sds((1, RET_V_W), F32), wide_vec), (_sds((1, RET_V_W), F32), wide_vec)],
        epi_dor, extras=[(o_pre, wide_tile), (z_rest, gate_tile), (gn_g, wide_vec), (gn_b, wide_vec)],
        exchange=_pair_exchange(owed), in_place=(dz_rest, 1))
    chip_sums = [_pair_sum(f"pair_sum_{n}", g, r, core, min(256, g.shape[2])) for n, g, r in zip(names, owed, got)]

    dz_rest = _ret_bwd(do_ret, states, z_rest, dz_rest)
    dz_att, gq_parts, gk_parts, parts_late = None, [], [], None
    for gi, ((_, d), nb) in enumerate(zip(ATT_GROUPS, ATT_BLOCKS_PER_STEP)):
        last = gi == len(ATT_GROUPS) - 1
        (dz_att, gq_p, gk_p), parts = _att_bwd_group(z_att, dz_att, do_a, delta_lmix, gq3, gk3, slopes3, gi, d, nb,
                                                     exchange=_chip_exchange(chip_sums) if last else None)
        parts_late = parts if last else parts_late
        gq_parts.append(gq_p)
        gk_parts.append(gk_p)
    g_qn = jnp.concatenate(gq_parts, axis=0).reshape(1, ATT_HEADS, DH)
    g_kn = jnp.concatenate(gk_parts, axis=0).reshape(1, ATT_HEADS, DH)

    gw_att = _mm("gw_in_att", xn, dz_att, "tn", D_MODEL, ATT_W, tk,
                 [(_sds((D_MODEL, 3 * ATT_W), BF16), _tile_ij(D_MODEL, ATT_W))], _epi_store(BF16))[0]
    gw_rest = _mm("gw_in_rest", xn, dz_rest, "tn", D_MODEL, 1024, tk2,
                  [(_sds((D_MODEL, REST_W), BF16), _tile_ij(D_MODEL, 1024))], _epi_store(BF16))[0]
    gw_att = gw_att.reshape(D_MODEL, ATT_HEADS, 3, DH).transpose(0, 2, 1, 3).reshape(D_MODEL, 3 * ATT_W)
    gw_in = jnp.concatenate([gw_att, gw_rest], axis=1)
    owed_in = _chip_core(_by_owner_cols(gw_in))
    (dxn_att,), (got_in,) = _mm("in_proj_att_bwd", dz_att, w_att_t, "nn", tm, D_MODEL, 3 * ATT_W,
                                [(_sds((S, D_MODEL), F32), row_tile)], _epi_store(F32), exchange=_pair_exchange([owed_in]))
    chip_sum_in = _pair_sum("pair_sum_w_in", owed_in, got_in, core, min(256, owed_in.shape[2]))

    def epi_norm1(acc, ex, out):
        dx, dg = _rms_bwd(acc + ex[0][...], ex[1][...], ex[3][...])
        out[0][...] = ex[2][...] + dx
        _acc_rows(out[1], dg, pl.program_id(0) == 0)

    short_tile = _tile_ij(ts, D_MODEL)
    (grad_x, g_norm1), parts_in = _mm(
        "in_proj_rest_bwd", dz_rest, w_rest_t, "nn", ts, D_MODEL, REST_W,
        [(_sds((S, D_MODEL), F32), short_tile), (_sds((1, D_MODEL), F32), vec)],
        epi_norm1, extras=[(dxn_att, short_tile), (x, short_tile), (dx1, short_tile), (norm1_g, vec)],
        exchange=_chip_exchange([chip_sum_in]))

    small = (g_norm1, g_qn, g_kn, g_gn_g, g_gn_b, g_norm2)
    return loss_local, grad_x, list(parts_in) + list(parts_late), small


def _position():
    return lax.axis_index("x"), lax.axis_index("y"), lax.axis_index("c")


def _other_chips(x, y):
    return [(1 - x, y), (x, 1 - y), (1 - x, 1 - y)]


_ANY = pl.BlockSpec(memory_space=pl.ANY)


def _gather_exchange(shards):
    nw = len(shards)

    def plan(x_refs, out_refs, sems):
        send_sems, recv_sems, local_sems = sems
        x, y, c = _position()
        me, sibling = (x, y, c), (x, y, 1 - c)
        chips = _other_chips(x, y)

        def copy(w, k, block, to, own=False):
            px, py, pc = block
            rows = out_refs[w].at[4 * px + 2 * py + pc]
            return pltpu.make_async_remote_copy(
                src_ref=x_refs[w] if own else rows, dst_ref=rows,
                send_sem=send_sems.at[7 * w + k], recv_sem=recv_sems.at[7 * w + k], device_id=to, device_id_type=MESH)

        def mine(w):
            return pltpu.make_async_copy(x_refs[w], out_refs[w].at[4 * x + 2 * y + c], local_sems.at[w])

        def own_sends(w):
            return [copy(w, 0, me, sibling, own=True)] + [copy(w, 1 + j, me, (*chip, c), own=True)
                                                          for j, chip in enumerate(chips)]

        def start():
            for w in range(nw):
                mine(w).start()
                for cp in own_sends(w):
                    cp.start()

        def relay():
            for j, chip in enumerate(chips):
                for w in range(nw):
                    copy(w, 1 + j, (*chip, c), me).wait_recv()
                    copy(w, 4 + j, (*chip, c), sibling).start()

        def finish():
            for w in range(nw):
                copy(w, 0, sibling, me).wait_recv()
                for j, chip in enumerate(chips):
                    copy(w, 4 + j, (*chip, 1 - c), me).wait_recv()
            for w in range(nw):
                for cp in own_sends(w):
                    cp.wait_send()
                for j, chip in enumerate(chips):
                    copy(w, 4 + j, (*chip, c), sibling).wait_send()
                mine(w).wait()

        return start, relay, finish

    return _Exchange(shards, [_sds((N_DEV,) + s.shape, s.dtype) for s in shards],
                     [pltpu.SemaphoreType.DMA((7 * nw,)), pltpu.SemaphoreType.DMA((7 * nw,)),
                      pltpu.SemaphoreType.DMA((nw,))], plan)


def _run_exchange(name, exchange):
    n_in, n_out = len(exchange.ins), len(exchange.out_shapes)

    def body(*refs):
        start, relay, finish = exchange.split(refs[:n_in], refs[n_in:n_in + n_out], refs[n_in + n_out:])
        start()
        if relay is not None:
            relay()
        finish()

    return pl.pallas_call(
        body, name=name, out_shape=exchange.out_shapes,
        in_specs=[_ANY] * n_in, out_specs=[_ANY] * n_out, scratch_shapes=exchange.sems,
    )(*exchange.ins)


def _pair_exchange(grads):
    ng = len(grads)

    def plan(g_refs, out_refs, sems):
        send_sems, recv_sems = sems
        x, y, c = _position()

        def copies():
            return [pltpu.make_async_remote_copy(
                src_ref=g_refs[w].at[:, 1 - c], dst_ref=out_refs[w], send_sem=send_sems.at[w],
                recv_sem=recv_sems.at[w], device_id=(x, y, 1 - c), device_id_type=MESH) for w in range(ng)]

        def start():
            for cp in copies():
                cp.start()

        def finish():
            for cp in copies():
                cp.wait()

        return start, None, finish

    return _Exchange(grads, [_sds((N_CHIP,) + g.shape[2:], g.dtype) for g in grads],
                     [pltpu.SemaphoreType.DMA((ng,)), pltpu.SemaphoreType.DMA((ng,))], plan)


def _small_all_gather(small):
    def plan(in_refs, out_refs, sems):
        (s_ref,), (s_out,) = in_refs, out_refs
        send_sems, recv_sems, local_sem = sems
        x, y, c = _position()
        me_id = 4 * x + 2 * y + c
        flips = [(a, b, e) for a in (0, 1) for b in (0, 1) for e in (0, 1)][1:]
        peers = [(x ^ a, y ^ b, c ^ e) for a, b, e in flips]

        def start():
            pltpu.make_async_copy(s_ref, s_out.at[me_id], local_sem).start()
            for k, p in enumerate(peers):
                pltpu.make_async_remote_copy(
                    src_ref=s_ref, dst_ref=s_out.at[me_id], send_sem=send_sems.at[k], recv_sem=recv_sems.at[k],
                    device_id=p, device_id_type=MESH).start()

        def finish():
            for k, (px, py, pc) in enumerate(peers):
                pltpu.make_async_remote_copy(
                    src_ref=s_ref, dst_ref=s_out.at[4 * px + 2 * py + pc], send_sem=send_sems.at[k],
                    recv_sem=recv_sems.at[k], device_id=(px, py, pc), device_id_type=MESH).wait()
            pltpu.make_async_copy(s_ref, s_out.at[me_id], local_sem).wait()

        return start, None, finish

    return _run_exchange("small_grad_all_gather", _Exchange(
        [small], [_sds((N_DEV,) + small.shape, small.dtype)],
        [pltpu.SemaphoreType.DMA((7,)), pltpu.SemaphoreType.DMA((7,)), pltpu.SemaphoreType.DMA], plan))[0]


def _pair_sum(name, g, got, core, tr):
    n_chip, _, R, C = g.shape

    def body(c_ref, a_ref, b_ref, o_ref):
        del c_ref
        o_ref[...] = (a_ref[...].astype(F32) + b_ref[...].astype(F32)).astype(o_ref.dtype)

    return pl.pallas_call(
        body, name=name,
        grid_spec=pltpu.PrefetchScalarGridSpec(
            num_scalar_prefetch=1, grid=(n_chip, R // tr),
            in_specs=[pl.BlockSpec((None, None, tr, C), lambda ch, i, c_ref: (ch, c_ref[0], i, 0)),
                      pl.BlockSpec((None, tr, C), lambda ch, i, c_ref: (ch, i, 0))],
            out_specs=pl.BlockSpec((None, tr, C), lambda ch, i, c_ref: (ch, i, 0))),
        out_shape=_sds(got.shape, got.dtype),
        compiler_params=_cparams(("parallel", "parallel")),
    )(core, g, got)


def _chip_exchange(parts):
    ng = len(parts)

    def plan(p_refs, out_refs, sems):
        send_sems, recv_sems, local_sems = sems
        x, y, c = _position()
        my_chip = 2 * x + y

        def copies():
            local = [pltpu.make_async_copy(p_refs[w].at[my_chip], out_refs[w].at[my_chip], local_sems.at[w])
                     for w in range(ng)]
            remote = [pltpu.make_async_remote_copy(
                src_ref=p_refs[w].at[2 * cx + cy], dst_ref=out_refs[w].at[my_chip],
                send_sem=send_sems.at[3 * w + k], recv_sem=recv_sems.at[3 * w + k],
                device_id=(cx, cy, c), device_id_type=MESH)
                for w in range(ng) for k, (cx, cy) in enumerate(_other_chips(x, y))]
            return local + remote

        def start():
            for cp in copies():
                cp.start()

        def finish():
            for cp in copies():
                cp.wait()

        return start, None, finish

    return _Exchange(parts, [_sds(p.shape, p.dtype) for p in parts],
                     [pltpu.SemaphoreType.DMA((3 * ng,)), pltpu.SemaphoreType.DMA((3 * ng,)),
                      pltpu.SemaphoreType.DMA((ng,))], plan)


def _adamw(name, parts, w, m, v, tr):
    n_parts = parts.shape[0]
    R, C = w.shape

    def body(p_ref, w_ref, m_ref, v_ref, g_ref, d_ref, mo_ref, vo_ref):
        g = p_ref[0].astype(F32)
        for i in range(1, n_parts):
            g = g + p_ref[i].astype(F32)
        m_new = ADAM_B1 * m_ref[...] + (1.0 - ADAM_B1) * g
        v_new = ADAM_B2 * v_ref[...] + (1.0 - ADAM_B2) * (g * g)
        m_hat = m_new / (1.0 - ADAM_B1 ** ADAM_STEP)
        v_hat = v_new / (1.0 - ADAM_B2 ** ADAM_STEP)
        g_ref[...] = g
        d_ref[...] = -ADAM_LR * (m_hat / (jnp.sqrt(v_hat) + ADAM_EPS) + ADAM_WD * w_ref[...])
        mo_ref[...] = m_new
        vo_ref[...] = v_new

    tile = pl.BlockSpec((tr, C), lambda i: (i, 0))
    return pl.pallas_call(
        body, name=name, grid=(R // tr,),
        in_specs=[pl.BlockSpec((n_parts, tr, C), lambda i: (0, i, 0)), tile, tile, tile],
        out_specs=[tile] * 4,
        out_shape=[_sds((R, C), F32)] * 4,
        compiler_params=_cparams(("parallel",)),
    )(parts, w, m, v)


def _flat_small(arrs):
    return jnp.concatenate([a.reshape(-1) for a in arrs]).reshape(SMALL_ROWS, LANES)


def _cols(g):
    return g.transpose(1, 0, 2).reshape(g.shape[1], -1)


def _rows_of(g):
    return g.reshape(-1, g.shape[2])


def _by_owner_cols(g):
    rows, cols = g.shape
    return g.reshape(rows, N_DEV, cols // N_DEV).transpose(1, 0, 2)


def _chip_core(g):
    return g.reshape((N_CHIP, 2) + g.shape[1:])


def kernel(x, norm1_g, w_in, q_norm_g, k_norm_g, ret_gn_g, ret_gn_b, w_proj_a, w_proj_b, w_out, norm2_g, w_up, w_down, loss_target, m_norm1_g, m_w_in, m_q_norm_g, m_k_norm_g, m_ret_gn_g, m_ret_gn_b, m_w_proj_a, m_w_proj_b, m_w_out, m_norm2_g, m_w_up, m_w_down, v_norm1_g, v_w_in, v_q_norm_g, v_k_norm_g, v_ret_gn_g, v_ret_gn_b, v_w_proj_a, v_w_proj_b, v_w_out, v_norm2_g, v_w_up, v_w_down):
    big_w = (w_in, w_proj_a, w_proj_b, w_out, w_up, w_down)
    big_m = (m_w_in, m_w_proj_a, m_w_proj_b, m_w_out, m_w_up, m_w_down)
    big_v = (v_w_in, v_w_proj_a, v_w_proj_b, v_w_out, v_w_up, v_w_down)
    small_w = (norm1_g, q_norm_g, k_norm_g, ret_gn_g, ret_gn_b, norm2_g)
    small_m = (m_norm1_g, m_q_norm_g, m_k_norm_g, m_ret_gn_g, m_ret_gn_b, m_norm2_g)
    small_v = (v_norm1_g, v_q_norm_g, v_k_norm_g, v_ret_gn_g, v_ret_gn_b, v_norm2_g)

    shards = [w[0].astype(BF16) for w in big_w]
    core = lax.axis_index("c").astype(jnp.int32).reshape(1)
    loss_local, grad_x, parts, small_g = _step(
        x[0], loss_target[0], norm1_g, q_norm_g[0], k_norm_g[0], ret_gn_g, ret_gn_b, norm2_g, shards, core)
    small_all = _small_all_gather(_flat_small(small_g))
    names = ("w_in", "w_proj_a", "w_proj_b", "w_out", "w_up", "w_down")

    res = {}
    for n, p, w, m, v in zip(names, parts, big_w, big_m, big_v):
        outs = _adamw(f"adamw_{n}", p, w[0], m[0], v[0], min(128, w.shape[1]))
        res[n] = [o[None] for o in outs]
    s_outs = _adamw("adamw_small", small_all, _flat_small(small_w), _flat_small(small_m), _flat_small(small_v), SMALL_ROWS)
    small_names = ("norm1_g", "q_norm_g", "k_norm_g", "ret_gn_g", "ret_gn_b", "norm2_g")
    for n in small_names:
        res[n] = []
    for o in s_outs:
        flat, off = o.reshape(-1), 0
        for n, w in zip(small_names, small_w):
            res[n].append(flat[off:off + w.size].reshape(w.shape))
            off += w.size

    order = ("norm1_g", "w_in", "q_norm_g", "k_norm_g", "ret_gn_g", "ret_gn_b", "w_proj_a", "w_proj_b", "w_out",
             "norm2_g", "w_up", "w_down")
    loss = lax.psum(loss_local, MESH_AXES)
    return (loss, grad_x[None], *[res[n][0] for n in order], *[res[n][1] for n in order],
            *[res[n][2] for n in order], *[res[n][3] for n in order])
```

```python
import math

import numpy as np
import jax
import jax.numpy as jnp
from jax import lax
from jax.experimental import pallas as pl
from jax.experimental.pallas import tpu as pltpu

F32 = jnp.float32
BF16 = jnp.bfloat16

D_MODEL = 1024
ATT_GROUPS = ((128, 1), (512, 4), (2048, 16))
ATT_BLOCKS_PER_STEP = (32, 8, 2)
ATT_TOGETHER = 4
HPG = 4
ATT_HEADS = 12
DH = 128
BLK = 128
ATT_W = ATT_HEADS * DH
ATT_OUT_W = HPG * DH
RET_HEADS = 4
RET_QK = 256
RET_V = 512
RET_QK_W = RET_HEADS * RET_QK
RET_V_W = RET_HEADS * RET_V
CHUNK = 128
RET_PER_STEP = 4
D_FF = 4096
IN_W = 12800
REST_W = IN_W - 3 * ATT_W
EPS = 1e-6
ADAM_LR, ADAM_B1, ADAM_B2, ADAM_EPS, ADAM_WD, ADAM_STEP = 0.001, 0.9, 0.999, 1e-08, 0.01, 10
N_DEV = 8
N_CHIP = 4
GATHER_PIECES = 2
MESH_AXES = ("x", "y", "c")
MESH = pl.DeviceIdType.MESH
VMEM_LIMIT = 56 * 1024 * 1024
LANES = 128
NEG = -1e30

_NN = (((1,), (0,)), ((), ()))
_NT = (((1,), (1,)), ((), ()))
_TN = (((0,), (0,)), ((), ()))

R_Q, R_K, R_V, R_G, R_GA, R_GB = 0, 1024, 2048, 4096, 6144, 7168

LOG_GAMMA = [float(v) for v in np.log(1.0 - 2.0 ** (-5.0 - np.arange(RET_HEADS, dtype=np.float32))).astype(np.float32)]
ALIBI = np.asarray(2.0 ** (-8.0 * np.arange(1, ATT_HEADS + 1, dtype=np.float32) / ATT_HEADS), np.float32)

SMALL_ROWS = (1024 + 1536 + 1536 + 2048 + 2048 + 1024) // LANES


def _dot(a, b, dims=_NN):
    return lax.dot_general(a, b, dims, preferred_element_type=F32)


def _cparams(sem):
    return pltpu.CompilerParams(dimension_semantics=sem, vmem_limit_bytes=VMEM_LIMIT)


def _sds(shape, dtype):
    return jax.ShapeDtypeStruct(shape, dtype)


class _Exchange:
    def __init__(self, ins, out_shapes, sems, plan):
        self.ins, self.out_shapes, self.sems, self.plan = list(ins), list(out_shapes), list(sems), plan

    def split(self, in_refs, out_refs, sem_refs):
        return self.plan(in_refs, out_refs, sem_refs)


def _mm(name, a, b, mode, tm, tn, tk, outs, epi, extras=(), b_pro=None,
        sem=("parallel", "parallel", "arbitrary"), exchange=None, in_place=None):
    if mode == "nn":
        (M, K), (_, N) = a.shape, b.shape
        a_spec = pl.BlockSpec((tm, tk), lambda i, j, k: (i, k))
        dims = _NN
    else:
        (K, M), (_, N) = a.shape, b.shape
        a_spec = pl.BlockSpec((tk, tm), lambda i, j, k: (k, i))
        dims = _TN
    assert M % tm == 0 and N % tn == 0 and K % tk == 0, (name, M, N, K, tm, tn, tk)
    nk = K // tk
    whole_b = dict(pipeline_mode=pl.Buffered(1)) if (nk == 1 and N == tn) else {}
    b_spec = pl.BlockSpec((tk, tn), lambda i, j, k: (k, j), **whole_b)
    n_ex, n_out = len(extras), len(outs)
    grid = (M // tm, N // tn, nk)
    n_xi = len(exchange.ins) if exchange else 0
    n_xo = len(exchange.out_shapes) if exchange else 0
    n_acc = 1 if nk > 1 else 0

    n_ip = 1 if in_place else 0

    def body(a_ref, b_ref, *rest):
        ex, rest = rest[:n_ex], rest[n_ex:]
        x_in, rest = rest[:n_xi], rest[n_xi + n_ip:]
        out, rest = rest[:n_out], rest[n_out:]
        x_out, rest = rest[:n_xo], rest[n_xo:]
        step =(pl.program_id(0) * grid[1] + pl.program_id(1)) * grid[2] + pl.program_id(2)
        n_steps = grid[0] * grid[1] * grid[2]
        if exchange:
            start, relay, finish = exchange.split(x_in, x_out, rest[n_acc:])
            pl.when(step == 0)(start)
        bv = b_ref[...]
        if b_pro is not None:
            bv = b_pro(bv)
        part = _dot(a_ref[...].astype(BF16), bv.astype(BF16), dims)
        if nk == 1:
            epi(part, ex, out)
        else:
            acc_ref = rest[0]
            k = pl.program_id(2)

            @pl.when(k == 0)
            def _():
                acc_ref[...] = part

            @pl.when(k > 0)
            def _():
                acc_ref[...] += part

            @pl.when(k == nk - 1)
            def _():
                epi(acc_ref[...], ex, out)
        if exchange:
            if relay is not None:
                pl.when(step == (7 * n_steps) // 8)(relay)
            pl.when(step == n_steps - 1)(finish)

    res = pl.pallas_call(
        body,
        name=name,
        grid=grid,
        in_specs=[a_spec, b_spec] + [s for _, s in extras] + [_ANY] * (n_xi + n_ip),
        out_specs=[s for _, s in outs] + [_ANY] * n_xo,
        out_shape=[o for o, _ in outs] + (exchange.out_shapes if exchange else []),
        scratch_shapes=([pltpu.VMEM((tm, tn), F32)] if nk > 1 else []) + (exchange.sems if exchange else []),
        input_output_aliases={2 + n_ex + n_xi: in_place[1]} if in_place else {},
        compiler_params=_cparams(("arbitrary",) * 3 if exchange else sem),
    )(a, b, *[e for e, _ in extras], *(exchange.ins if exchange else []), *([in_place[0]] if in_place else []))
    return (res[:n_out], res[n_out:]) if exchange else res


def _tile_ij(tm, tn):
    return pl.BlockSpec((tm, tn), lambda i, j, k: (i, j))


def _epi_store(dtype):
    def epi(acc, ex, out):
        out[0][...] = acc.astype(dtype)
    return epi


def _rms_rows(x):
    return lax.rsqrt(jnp.mean(x * x, axis=-1, keepdims=True) + EPS)


def _acc_rows(ref, part, first):
    @pl.when(first)
    def _():
        ref[...] = part

    @pl.when(jnp.logical_not(first))
    def _():
        ref[...] += part


def _rmsnorm_fwd(x, g, tm, exchange):
    S, Dm = x.shape
    n_steps = S // tm
    n_xi, n_xo = len(exchange.ins), len(exchange.out_shapes)

    def body(x_ref, g_ref, *rest):
        x_in, o_ref, x_out, sems = rest[:n_xi], rest[n_xi], rest[n_xi + 1:n_xi + 1 + n_xo], rest[n_xi + 1 + n_xo:]
        start, relay, finish = exchange.split(x_in, x_out, sems)
        step = pl.program_id(0)
        pl.when(step == 0)(start)
        xv = x_ref[...]
        o_ref[...] = (xv * _rms_rows(xv) * g_ref[...]).astype(BF16)

        @pl.when(step == n_steps - 1)
        def _():
            relay()
            finish()

    res = pl.pallas_call(
        body, name="rmsnorm1_fwd", grid=(n_steps,),
        in_specs=[pl.BlockSpec((tm, Dm), lambda i: (i, 0)), pl.BlockSpec((1, Dm), lambda i: (0, 0))] + [_ANY] * n_xi,
        out_specs=[pl.BlockSpec((tm, Dm), lambda i: (i, 0))] + [_ANY] * n_xo,
        out_shape=[_sds((S, Dm), BF16)] + exchange.out_shapes,
        scratch_shapes=exchange.sems,
        compiler_params=_cparams(("arbitrary",)),
    )(x, g, *exchange.ins)
    return res[0], res[1:]


def _rows(ref, r, b, d):
    if d == 1:
        return ref[b * BLK:(b + 1) * BLK, :]
    return ref[pl.ds(b * BLK * d + r, BLK, stride=d), :]


def _put_rows(ref, r, b, d, val):
    if d == 1:
        ref[b * BLK:(b + 1) * BLK, :] = val
    else:
        ref[pl.ds(b * BLK * d + r, BLK, stride=d), :] = val


def _head_norm(x, g):
    r = _rms_rows(x)
    xh = x * r
    return (xh * g).astype(BF16), xh, r


def _head_norm_bwd(dyn, xh, r, g):
    dxh = dyn * g
    dx = r * (dxh - xh * jnp.mean(dxh * xh, axis=-1, keepdims=True))
    return dx, jnp.sum(dyn * xh, axis=0, keepdims=True)


def _att_mask_bias(slope, d, first):
    qi = lax.broadcasted_iota(jnp.int32, (BLK, 2 * BLK), 0)
    kj = lax.broadcasted_iota(jnp.int32, (BLK, 2 * BLK), 1)
    dist = BLK + qi - kj
    valid = (dist >= 0) & (dist <= BLK)
    if first is not None:
        valid = valid & (jnp.logical_not(first) | (kj >= BLK))
    bias = -slope * (dist * d).astype(F32)
    return valid, bias


def _att_specs(gi, d, nb, S):
    span = BLK * d
    sb = span * nb
    nspan = S // span
    before = lambda n: jnp.maximum(n * nb - 1, 0)
    after = lambda n: jnp.minimum((n + 1) * nb, nspan - 1)
    zcol = lambda j, kind: 3 * (gi * HPG + j) + kind
    cur = lambda kind: pl.BlockSpec((sb, DH), lambda j, n: (n, zcol(j, kind)))
    prev = lambda kind: pl.BlockSpec((span, DH), lambda j, n: (before(n), zcol(j, kind)))
    nxt = lambda kind: pl.BlockSpec((span, DH), lambda j, n: (after(n), zcol(j, kind)))
    slot = pl.BlockSpec((sb, DH), lambda j, n: (n, j))
    slot_next = pl.BlockSpec((span, DH), lambda j, n: (after(n), j))
    head = pl.BlockSpec((None, 1, DH), lambda j, n: (gi * HPG + j, 0, 0))
    return cur, prev, nxt, slot, slot_next, head


def _att_fwd_group(z_att, gq3, gk3, slopes3, gi, d, nb, others=()):
    S = z_att.shape[0]
    nsb = S // (BLK * d * nb)
    scale = DH ** -0.5
    n_other = len(others)

    def body(q_ref, k_ref, v_ref, kp_ref, vp_ref, gq_ref, gk_ref, sl_ref, *rest):
        other_refs, (o_ref, l_ref) = rest[:2 * n_other], rest[2 * n_other:]
        slope = sl_ref[...][:, :1]
        valid0, bias = _att_mask_bias(slope, d, pl.program_id(1) == 0)
        valid_in, _ = _att_mask_bias(slope, d, None)
        gq, gk = gq_ref[...], gk_ref[...]
        memo = {}

        def get(kind, r, b):
            if (kind, r, b) not in memo:
                if kind == "k":
                    val = _head_norm(_rows(k_ref if b >= 0 else kp_ref, r, max(b, 0), d), gk)[0]
                else:
                    val = _rows(v_ref if b >= 0 else vp_ref, r, max(b, 0), d).astype(BF16)
                memo[(kind, r, b)] = val
            return memo[(kind, r, b)]

        units = [(r, b) for r in range(d) for b in range(nb)]
        for c0 in range(0, len(units), ATT_TOGETHER):
            us = units[c0:c0 + ATT_TOGETHER]
            q = [_head_norm(_rows(q_ref, r, b, d), gq)[0] for r, b in us]
            k2 = [jnp.concatenate([get("k", r, b - 1), get("k", r, b)], axis=0) for r, b in us]
            v2 = [jnp.concatenate([get("v", r, b - 1), get("v", r, b)], axis=0) for r, b in us]
            s = [jnp.where(valid0 if b == 0 else valid_in, _dot(q[i], k2[i], _NT) * scale + bias, NEG)
                 for i, (r, b) in enumerate(us)]
            m = [jnp.max(si, axis=-1, keepdims=True) for si in s]
            p = [jnp.exp(si - mi) for si, mi in zip(s, m)]
            den = [jnp.sum(pi, axis=-1, keepdims=True) for pi in p]
            o = [_dot(pi.astype(BF16), vi) / di for pi, vi, di in zip(p, v2, den)]
            for i, (r, b) in enumerate(us):
                _put_rows(o_ref, r, b, d, o[i])
                _put_rows(l_ref, r, b, d, jnp.broadcast_to(m[i] + jnp.log(den[i]), (BLK, DH)))
        if n_other:
            os_ = [ref[...] for ref in other_refs[:n_other]] + [o_ref[...]]
            ls_ = [ref[...] for ref in other_refs[n_other:]] + [l_ref[...]]
            m = ls_[0]
            for l in ls_[1:]:
                m = jnp.maximum(m, l)
            es = [jnp.exp(l - m) for l in ls_]
            tot, mix = es[0], es[0] * os_[0]
            for e, o in zip(es[1:], os_[1:]):
                tot, mix = tot + e, mix + e * o
            o_ref[...] = mix / tot
            l_ref[...] = m + jnp.log(tot)

    cur, prev, _, slot, _, head = _att_specs(gi, d, nb, S)
    return pl.pallas_call(
        body, name=f"att_fwd_g{gi}", grid=(HPG, nsb),
        in_specs=[cur(0), cur(1), cur(2), prev(1), prev(2), head, head, head] + [slot] * (2 * n_other),
        out_specs=[slot, slot],
        out_shape=[_sds((S, ATT_OUT_W), F32), _sds((S, ATT_OUT_W), F32)],
        compiler_params=_cparams(("parallel", "arbitrary")),
    )(z_att, z_att, z_att, z_att, z_att, gq3, gk3, slopes3, *[o for o, _ in others], *[l for _, l in others])


def _ret_tables(lg):
    ri = lax.broadcasted_iota(jnp.int32, (CHUNK, CHUNK), 0)
    ci = lax.broadcasted_iota(jnp.int32, (CHUNK, CHUNK), 1)
    diff = (ri - ci).astype(F32)
    decay = jnp.where(diff >= 0, jnp.exp(lg * jnp.maximum(diff, 0.0)), 0.0)
    idx = lax.broadcasted_iota(jnp.int32, (CHUNK, 1), 0).astype(F32)
    xi = jnp.exp(lg * (idx + 1.0))
    zeta = jnp.exp(lg * (CHUNK - 1.0 - idx))
    return decay, xi, zeta, math.exp(lg * CHUNK)


def _ret_specs(nsteps, rev):
    idx = (lambda n: nsteps - 1 - n) if rev else (lambda n: n)
    rows = CHUNK * RET_PER_STEP
    qk = lambda off: pl.BlockSpec((rows, RET_QK_W), lambda n: (idx(n), off // RET_QK_W))
    vv = lambda off: pl.BlockSpec((rows, RET_V_W), lambda n: (idx(n), off // RET_V_W))
    par = pl.BlockSpec((1, RET_V_W), lambda n: (0, 0))
    wide = pl.BlockSpec((rows, RET_V_W), lambda n: (idx(n), 0))
    st = pl.BlockSpec((RET_HEADS, RET_PER_STEP, RET_QK, RET_V), lambda n: (0, idx(n), 0, 0))
    return qk, vv, par, wide, st


def _ret_fwd(z_rest, gn_g, gn_b):
    S = z_rest.shape[0]
    nch = S // CHUNK
    nsteps = nch // RET_PER_STEP

    def body(q_ref, k_ref, v_ref, gr_ref, g_ref, b_ref, or_ref, o_ref, st_ref, state):
        @pl.when(pl.program_id(0) == 0)
        def _():
            state[...] = jnp.zeros_like(state)

        for c in range(RET_PER_STEP):
            rc = slice(c * CHUNK, (c + 1) * CHUNK)
            for h in range(RET_HEADS):
                decay, xi, zeta, gch = _ret_tables(LOG_GAMMA[h])
                cq = slice(h * RET_QK, (h + 1) * RET_QK)
                cv = slice(h * RET_V, (h + 1) * RET_V)
                q = q_ref[rc, cq]
                kc32 = k_ref[rc, cq].astype(F32) * (RET_QK ** -0.5)
                kc = kc32.astype(BF16)
                v = v_ref[rc, cv]
                st = state[h]
                stb = st.astype(BF16)
                st_ref[h, c] = stb
                s = _dot(q, kc, _NT) * decay
                o = _dot(s.astype(BF16), v) + _dot(q, stb) * xi
                state[h] = st * gch + _dot((kc32 * zeta).astype(BF16), v, _TN)
                mu = jnp.mean(o, axis=-1, keepdims=True)
                cen = o - mu
                yh = cen * lax.rsqrt(jnp.mean(cen * cen, axis=-1, keepdims=True) + EPS)
                gr = gr_ref[rc, cv].astype(F32)
                or_ref[rc, cv] = ((yh * g_ref[:, cv] + b_ref[:, cv]) * (gr * jax.nn.sigmoid(gr))).astype(BF16)
                o_ref[rc, cv] = o

    qk, vv, par, wide, st = _ret_specs(nsteps, False)
    return pl.pallas_call(
        body, name="ret_fwd", grid=(nsteps,),
        in_specs=[qk(R_Q), qk(R_K), vv(R_V), vv(R_G), par, par],
        out_specs=[wide, wide, st],
        out_shape=[_sds((S, RET_V_W), BF16), _sds((S, RET_V_W), F32), _sds((RET_HEADS, nch, RET_QK, RET_V), BF16)],
        scratch_shapes=[pltpu.VMEM((RET_HEADS, RET_QK, RET_V), F32)],
        compiler_params=_cparams(("arbitrary",)),
    )(z_rest, z_rest, z_rest, z_rest, gn_g, gn_b)


def _merge_fwd(o_a, o_r, z_rest, x, wpa, wpb, wout, g2, tm):
    S = x.shape[0]

    def body(oa_ref, or_ref, ga_ref, gb_ref, x_ref, wpa_ref, wpb_ref, wo_ref, g2_ref,
             x1_ref, y_ref, pa_ref, pb_ref, xn2_ref):
        pa = _dot(oa_ref[...].astype(BF16), wpa_ref[...])
        pb = _dot(or_ref[...], wpb_ref[...])
        y = jax.nn.sigmoid(ga_ref[...].astype(F32)) * pa + jax.nn.sigmoid(gb_ref[...].astype(F32)) * pb
        yb = y.astype(BF16)
        x1 = x_ref[...] + _dot(yb, wo_ref[...])
        x1_ref[...] = x1
        y_ref[...] = yb
        pa_ref[...] = pa.astype(BF16)
        pb_ref[...] = pb.astype(BF16)
        xn2_ref[...] = (x1 * _rms_rows(x1) * g2_ref[...]).astype(BF16)

    row = lambda w: pl.BlockSpec((tm, w), lambda i: (i, 0))
    full = lambda a: pl.BlockSpec(a.shape, lambda i: (0, 0))
    return pl.pallas_call(
        body, name="merge_fwd", grid=(S // tm,),
        in_specs=[row(ATT_OUT_W), row(RET_V_W),
                  pl.BlockSpec((tm, D_MODEL), lambda i: (i, R_GA // D_MODEL)),
                  pl.BlockSpec((tm, D_MODEL), lambda i: (i, R_GB // D_MODEL)),
                  row(D_MODEL), full(wpa), full(wpb), full(wout), full(g2)],
        out_specs=[row(D_MODEL)] * 5,
        out_shape=[_sds((S, D_MODEL), F32)] + [_sds((S, D_MODEL), BF16)] * 4,
        compiler_params=_cparams(("parallel",)),
    )(o_a, o_r, z_rest, z_rest, x, wpa, wpb, wout, g2)


def _rms_bwd(dy, xv, g):
    r = _rms_rows(xv)
    xh = xv * r
    dg = dy * g
    dx = r * (dg - xh * jnp.mean(dg * xh, axis=-1, keepdims=True))
    return dx, jnp.sum(dy * xh, axis=0, keepdims=True)


def _norm_gate_bwd(dout, o, gr, gam, bet):
    mean = lambda xs: [jnp.mean(x, axis=-1, keepdims=True) for x in xs]
    cen = [x - m for x, m in zip(o, mean(o))]
    rstd = [lax.rsqrt(v + EPS) for v in mean([c * c for c in cen])]
    yh = [c * r for c, r in zip(cen, rstd)]
    y = [a * g + b for a, g, b in zip(yh, gam, bet)]
    sg = [jax.nn.sigmoid(g) for g in gr]
    dy = [d * (g * s) for d, g, s in zip(dout, gr, sg)]
    dgr = [d * a * (s * (1.0 + g * (1.0 - s))) for d, a, s, g in zip(dout, y, sg, gr)]
    dyh = [d * g for d, g in zip(dy, gam)]
    m1, m2 = mean(dyh), mean([a * b for a, b in zip(dyh, yh)])
    do = [r * (d - a - h * b) for r, d, a, h, b in zip(rstd, dyh, m1, yh, m2)]
    dg = [jnp.sum(d * h, axis=0, keepdims=True) for d, h in zip(dy, yh)]
    db = [jnp.sum(d, axis=0, keepdims=True) for d in dy]
    return do, dgr, dg, db


def _ret_bwd(do, states, z_rest, dz_rest):
    S = z_rest.shape[0]
    nsteps = S // CHUNK // RET_PER_STEP

    def body(do_ref, st_ref, q_ref, k_ref, v_ref, dz_in, dz_ref, gst):
        del dz_in

        @pl.when(pl.program_id(0) == 0)
        def _():
            gst[...] = jnp.zeros_like(gst)

        for c in reversed(range(RET_PER_STEP)):
            rc = slice(c * CHUNK, (c + 1) * CHUNK)
            for h in range(RET_HEADS):
                decay, xi, zeta, gch = _ret_tables(LOG_GAMMA[h])
                cq = slice(h * RET_QK, (h + 1) * RET_QK)
                cv = slice(h * RET_V, (h + 1) * RET_V)
                q = q_ref[rc, cq]
                kc32 = k_ref[rc, cq].astype(F32) * (RET_QK ** -0.5)
                kc = kc32.astype(BF16)
                v = v_ref[rc, cv]
                dob = do_ref[rc, cv]
                a = (_dot(q, kc, _NT) * decay).astype(BF16)
                da = (_dot(dob, v, _NT) * decay).astype(BF16)
                dcross = (dob.astype(F32) * xi).astype(BF16)
                g_next = gst[h]
                gb = g_next.astype(BF16)
                dq = _dot(da, kc) + _dot(dcross, st_ref[h, c], _NT)
                dkc = _dot(da, q, _TN)
                dkz = _dot(v, gb, _NT)
                dv = _dot(a, dob, _TN) + _dot((kc32 * zeta).astype(BF16), gb)
                gst[h] = g_next * gch + _dot(q, dcross, _TN)
                dz_ref[rc, R_Q + h * RET_QK:R_Q + (h + 1) * RET_QK] = dq.astype(BF16)
                dz_ref[rc, R_K + h * RET_QK:R_K + (h + 1) * RET_QK] = (
                    (dkc + dkz * zeta) * (RET_QK ** -0.5)).astype(BF16)
                dz_ref[rc, R_V + h * RET_V:R_V + (h + 1) * RET_V] = dv.astype(BF16)

    qk, vv, _, wide, st = _ret_specs(nsteps, True)
    return pl.pallas_call(
        body, name="ret_bwd", grid=(nsteps,),
        in_specs=[wide, st, qk(R_Q), qk(R_K), vv(R_V), _ANY],
        out_specs=pl.BlockSpec((CHUNK * RET_PER_STEP, R_G), lambda n: (nsteps - 1 - n, 0)),
        out_shape=_sds(dz_rest.shape, BF16),
        input_output_aliases={5: 0},
        scratch_shapes=[pltpu.VMEM((RET_HEADS, RET_QK, RET_V), F32)],
        compiler_params=_cparams(("arbitrary",)),
    )(do, states, z_rest, z_rest, z_rest, dz_rest)


def _att_probs(q, k, lmix, valid, bias):
    s = _dot(q, k, _NT) * (DH ** -0.5) + bias
    return jnp.where(valid, jnp.exp(jnp.where(valid, s, NEG) - lmix), 0.0)


def _att_bwd_group(z_att, dz_att, do_a, delta_lmix, gq3, gk3, slopes3, gi, d, nb, exchange=None):
    S = z_att.shape[0]
    sb = BLK * d * nb
    nsb = S // sb
    scale = DH ** -0.5
    aliased = dz_att is not None
    n_xi = len(exchange.ins) if exchange else 0
    n_xo = len(exchange.out_shapes) if exchange else 0

    def body(q_ref, k_ref, v_ref, kp_ref, vp_ref, qn_ref, do_ref, don_ref, dl_ref, dln_ref,
             gq_ref, gk_ref, sl_ref, *rest):
        rest = rest[1:] if aliased else rest
        x_in, (dz_ref, dgq_ref, dgk_ref), rest = rest[:n_xi], rest[n_xi:n_xi + 3], rest[n_xi + 3:]
        x_out, stage, x_sems = rest[:n_xo], rest[n_xo], rest[n_xo + 1:]
        n = pl.program_id(1)
        step = pl.program_id(0) * nsb + n
        if exchange:
            start, _, finish = exchange.split(x_in, x_out, x_sems)
            pl.when(step == 0)(start)
        slope = sl_ref[...][:, :1]
        valid0, bias = _att_mask_bias(slope, d, n == 0)
        valid_in, _ = _att_mask_bias(slope, d, None)
        qi = lax.broadcasted_iota(jnp.int32, (BLK, BLK), 0)
        kj = lax.broadcasted_iota(jnp.int32, (BLK, BLK), 1)
        dist_n = BLK + qi - kj
        valid_n_in = dist_n <= BLK
        valid_n_last = valid_n_in & (n < nsb - 1)
        bias_n = -slope * (dist_n * d).astype(F32)
        gq, gk = gq_ref[...], gk_ref[...]
        dgq = jnp.zeros((1, DH), F32)
        dgk = jnp.zeros((1, DH), F32)
        memo = {}

        def get(kind, r, b):
            if (kind, r, b) not in memo:
                inner = 0 <= b < nb
                bb = b if inner else 0
                if kind == "q":
                    val = _head_norm(_rows(q_ref if inner else qn_ref, r, bb, d), gq)
                elif kind == "k":
                    val = _head_norm(_rows(k_ref if inner else kp_ref, r, bb, d), gk)
                elif kind == "v":
                    val = _rows(v_ref if inner else vp_ref, r, bb, d).astype(BF16)
                elif kind == "do":
                    val = _rows(do_ref if inner else don_ref, r, bb, d).astype(BF16)
                elif kind == "dl+lm":
                    val = _rows(dl_ref if inner else dln_ref, r, bb, d)
                elif kind == "dl":
                    val = get("dl+lm", r, b)[:, :1]
                else:
                    val = get("dl+lm", r, b)[:, DH // 2:DH // 2 + 1]
                memo[(kind, r, b)] = val
            return memo[(kind, r, b)]

        units = [(r, b) for r in range(d) for b in range(nb)]
        for c0 in range(0, len(units), ATT_TOGETHER):
            us = units[c0:c0 + ATT_TOGETHER]
            k2 = [jnp.concatenate([get("k", r, b - 1)[0], get("k", r, b)[0]], axis=0) for r, b in us]
            v2 = [jnp.concatenate([get("v", r, b - 1), get("v", r, b)], axis=0) for r, b in us]
            p = [_att_probs(get("q", r, b)[0], k2[i], get("lm", r, b), valid0 if b == 0 else valid_in, bias)
                 for i, (r, b) in enumerate(us)]
            dp = [_dot(get("do", r, b), v2[i], _NT) for i, (r, b) in enumerate(us)]
            ds = [(p[i] * (dp[i] - get("dl", r, b)) * scale).astype(BF16) for i, (r, b) in enumerate(us)]
            dq = [_dot(ds[i], k2[i]) for i in range(len(us))]
            p_n = [_att_probs(get("q", r, b + 1)[0], get("k", r, b)[0], get("lm", r, b + 1),
                              valid_n_last if b == nb - 1 else valid_n_in, bias_n) for r, b in us]
            dp_n = [_dot(get("do", r, b + 1), get("v", r, b), _NT) for r, b in us]
            ds_n = [(p_n[i] * (dp_n[i] - get("dl", r, b + 1)) * scale).astype(BF16) for i, (r, b) in enumerate(us)]
            dk = [_dot(ds[i][:, BLK:], get("q", r, b)[0], _TN) + _dot(ds_n[i], get("q", r, b + 1)[0], _TN)
                  for i, (r, b) in enumerate(us)]
            dv = [_dot(p[i][:, BLK:].astype(BF16), get("do", r, b), _TN)
                  + _dot(p_n[i].astype(BF16), get("do", r, b + 1), _TN) for i, (r, b) in enumerate(us)]
            for i, (r, b) in enumerate(us):
                _, qh, qr = get("q", r, b)
                _, kh, kr = get("k", r, b)
                dxq, dg_q = _head_norm_bwd(dq[i], qh, qr, gq)
                dxk, dg_k = _head_norm_bwd(dk[i], kh, kr, gk)
                _put_rows(stage.at[0], r, b, d, dxq)
                _put_rows(stage.at[1], r, b, d, dxk)
                _put_rows(stage.at[2], r, b, d, dv[i])
                dgq, dgk = dgq + dg_q, dgk + dg_k
        for kind in range(3):
            dz_ref[:, kind * DH:(kind + 1) * DH] = stage[kind].astype(BF16)
        _acc_rows(dgq_ref, dgq, n == 0)
        _acc_rows(dgk_ref, dgk, n == 0)
        if exchange:
            pl.when(step == HPG * nsb - 1)(finish)

    cur, prev, nxt, slot, slot_next, head = _att_specs(gi, d, nb, S)
    gain = pl.BlockSpec((None, 1, DH), lambda j, n: (j, 0, 0))
    in_specs = [cur(0), cur(1), cur(2), prev(1), prev(2), nxt(0),
                slot, slot_next, slot, slot_next, head, head, head]
    args = [z_att] * 6 + [do_a, do_a, delta_lmix, delta_lmix, gq3, gk3, slopes3]
    if aliased:
        in_specs.append(_ANY)
        args.append(dz_att)
    res = pl.pallas_call(
        body, name=f"att_bwd_g{gi}", grid=(HPG, nsb),
        in_specs=in_specs + [_ANY] * n_xi,
        out_specs=[pl.BlockSpec((sb, 3 * DH), lambda j, n: (n, gi * HPG + j)), gain, gain] + [_ANY] * n_xo,
        out_shape=[_sds(z_att.shape, BF16), _sds((HPG, 1, DH), F32), _sds((HPG, 1, DH), F32)]
        + (exchange.out_shapes if exchange else []),
        input_output_aliases={len(args) - 1: 0} if aliased else {},
        scratch_shapes=[pltpu.VMEM((3, sb, DH), F32)] + (exchange.sems if exchange else []),
        compiler_params=_cparams(("arbitrary", "arbitrary") if exchange else ("parallel", "arbitrary")),
    )(*args, *(exchange.ins if exchange else []))
    return res[:3], res[3:]


def _step(x, target, norm1_g, q_norm_g, k_norm_g, gn_g, gn_b, norm2_g, shards, core):
    S = x.shape[0]
    tm = min(512, S)
    ts = min(256, S)
    tk = min(2048, S)
    tk2 = min(4096, S)
    later_shards = shards[1:]
    gq3 = q_norm_g.reshape(ATT_HEADS, 1, DH)
    gk3 = k_norm_g.reshape(ATT_HEADS, 1, DH)
    slopes3 = jnp.asarray(np.broadcast_to(ALIBI[:, None, None], (ATT_HEADS, 1, DH)).copy())

    xn, (g_in,) = _rmsnorm_fwd(x, norm1_g, tm, _gather_exchange(shards[:1]))
    w_in = _cols(g_in)
    w_att = w_in[:, :3 * ATT_W].reshape(D_MODEL, 3, ATT_HEADS, DH).transpose(0, 2, 1, 3).reshape(D_MODEL, 3 * ATT_W)
    w_rest = w_in[:, 3 * ATT_W:]
    w_att_t, w_rest_t = w_att.T, w_rest.T
    z_att = _mm("in_proj_att", xn, w_att, "nn", tm, 3 * ATT_W, D_MODEL,
                [(_sds((S, 3 * ATT_W), F32), _tile_ij(tm, 3 * ATT_W))], _epi_store(F32))[0]
    (z_rest,), gathered = _mm("in_proj_rest", xn, w_rest, "nn", tm, REST_W, D_MODEL,
                              [(_sds((S, REST_W), BF16), _tile_ij(tm, REST_W))], _epi_store(BF16),
                              exchange=_gather_exchange(later_shards))
    w_pa, w_pb, w_out, w_up, w_down = [f(g) for f, g in zip((_cols, _rows_of, _rows_of, _cols, _rows_of), gathered)]
    w_up_t, w_down_t, w_out_t, w_pa_t, w_pb_t = w_up.T, w_down.T, w_out.T, w_pa.T, w_pb.T
    done = []
    for gi, ((_, d), nb) in enumerate(zip(ATT_GROUPS, ATT_BLOCKS_PER_STEP)):
        last = gi == len(ATT_GROUPS) - 1
        done.append(_att_fwd_group(z_att, gq3, gk3, slopes3, gi, d, nb, others=tuple(done) if last else ()))
    o_a, lmix = done[-1]
    o_r, o_pre, states = _ret_fwd(z_rest, gn_g, gn_b)
    x1, y, pa, pb, xn2 = _merge_fwd(o_a, o_r, z_rest, x, w_pa, w_pb, w_out, norm2_g, tm)

    def epi_up(acc, ex, out):
        r = jnp.maximum(acc, 0.0)
        out[0][...] = (r * r).astype(BF16)
        out[1][...] = r.astype(BF16)

    h, relu_u = _mm("mlp_up", xn2, w_up, "nn", tm, D_FF, D_MODEL,
                    [(_sds((S, D_FF), BF16), _tile_ij(tm, D_FF)), (_sds((S, D_FF), BF16), _tile_ij(tm, D_FF))], epi_up)

    def epi_down(acc, ex, out):
        diff = ex[0][...] + acc - ex[1][...]
        dx = diff * (1.0 / D_MODEL)
        out[0][...] = dx
        out[1][...] = jnp.broadcast_to(jnp.sum(diff * diff) * (1.0 / (8 * LANES)), (8, LANES))
        out[2][...] = dx.astype(BF16)

    row_tile = _tile_ij(tm, D_MODEL)
    dx2, loss_parts, dx2_b = _mm(
        "mlp_down_loss", h, w_down, "nn", tm, D_MODEL, D_FF,
        [(_sds((S, D_MODEL), F32), row_tile),
         (_sds((S // tm * 8, LANES), F32), pl.BlockSpec((8, LANES), lambda i, j, k: (i, 0))),
         (_sds((S, D_MODEL), BF16), row_tile)],
        epi_down, extras=[(x1, row_tile), (target, row_tile)])
    loss_local = jnp.sum(loss_parts) * (0.5 / D_MODEL)

    def epi_du(acc, ex, out):
        out[0][...] = (acc * (2.0 * ex[0][...].astype(F32))).astype(BF16)

    du = _mm("mlp_down_bwd", dx2_b, w_down_t, "nn", tm, D_FF, D_MODEL,
             [(_sds((S, D_FF), BF16), _tile_ij(tm, D_FF))], epi_du, extras=[(relu_u, _tile_ij(tm, D_FF))])[0]
    gw_down = _mm("gw_down", h, dx2_b, "tn", 1024, D_MODEL, tk2,
                  [(_sds((D_FF, D_MODEL), BF16), _tile_ij(1024, D_MODEL))], _epi_store(BF16))[0]
    gw_up = _mm("gw_up", xn2, du, "tn", D_MODEL, 512, tk2,
                [(_sds((N_DEV, D_MODEL, 512), BF16), pl.BlockSpec((None, D_MODEL, 512), lambda i, j, k: (j, 0, 0)))],
                _epi_store(BF16))[0]

    vec = pl.BlockSpec((1, D_MODEL), lambda i, j, k: (0, 0))
    seq_sem = ("arbitrary", "arbitrary", "arbitrary")

    def epi_norm2(acc, ex, out):
        dx, dg = _rms_bwd(acc, ex[0][...], ex[2][...])
        dx1_tile = ex[1][...] + dx
        out[0][...] = dx1_tile
        _acc_rows(out[1], dg, pl.program_id(0) == 0)
        out[2][...] = dx1_tile.astype(BF16)

    dx1, g_norm2, dx1_b = _mm(
        "mlp_up_bwd", du, w_up_t, "nn", tm, D_MODEL, D_FF,
        [(_sds((S, D_MODEL), F32), row_tile), (_sds((1, D_MODEL), F32), vec), (_sds((S, D_MODEL), BF16), row_tile)],
        epi_norm2, extras=[(x1, row_tile), (dx2, row_tile), (norm2_g, vec)], sem=seq_sem)

    def epi_dy(acc, ex, out):
        sa = jax.nn.sigmoid(ex[0][...].astype(F32))
        sb = jax.nn.sigmoid(ex[1][...].astype(F32))
        out[0][...] = (acc * sa).astype(BF16)
        out[1][...] = (acc * sb).astype(BF16)
        out[2][:, :D_MODEL] = (acc * ex[2][...].astype(F32) * (sa * (1.0 - sa))).astype(BF16)
        out[2][:, D_MODEL:] = (acc * ex[3][...].astype(F32) * (sb * (1.0 - sb))).astype(BF16)

    ga_spec = pl.BlockSpec((tm, D_MODEL), lambda i, j, k: (i, R_GA // D_MODEL))
    gb_spec = pl.BlockSpec((tm, D_MODEL), lambda i, j, k: (i, R_GB // D_MODEL))
    gates_spec = pl.BlockSpec((tm, 2 * D_MODEL), lambda i, j, k: (i, R_GA // (2 * D_MODEL)))
    dpa, dpb, dz_rest = _mm(
        "out_proj_bwd", dx1_b, w_out_t, "nn", tm, D_MODEL, D_MODEL,
        [(_sds((S, D_MODEL), BF16), row_tile), (_sds((S, D_MODEL), BF16), row_tile), (_sds((S, REST_W), BF16), gates_spec)],
        epi_dy, extras=[(z_rest, ga_spec), (z_rest, gb_spec), (pa, row_tile), (pb, row_tile)])
    gw_out = _mm("gw_out", y, dx1_b, "tn", D_MODEL, D_MODEL, tk2,
                 [(_sds((D_MODEL, D_MODEL), BF16), _tile_ij(D_MODEL, D_MODEL))], _epi_store(BF16))[0]
    gw_pa = _mm("gw_proj_a", o_a, dpa, "tn", ATT_OUT_W, D_MODEL, tk,
                [(_sds((ATT_OUT_W, D_MODEL), BF16), _tile_ij(ATT_OUT_W, D_MODEL))], _epi_store(BF16))[0]
    gw_pb = _mm("gw_proj_b", o_r, dpb, "tn", 1024, D_MODEL, tk2,
                [(_sds((RET_V_W, D_MODEL), BF16), _tile_ij(1024, D_MODEL))], _epi_store(BF16))[0]

    def epi_doa(acc, ex, out):
        out[0][...] = acc
        prod = acc * ex[0][...]
        delta = jnp.concatenate(
            [jnp.broadcast_to(jnp.sum(prod[:, j * DH:(j + 1) * DH], axis=-1, keepdims=True), (prod.shape[0], DH))
             for j in range(HPG)], axis=1)
        lane = lax.broadcasted_iota(jnp.int32, delta.shape, 1)
        out[1][...] = jnp.where(lane % DH < DH // 2, delta, ex[1][...])

    slot_tile = _tile_ij(tm, ATT_OUT_W)
    do_a, delta_lmix = _mm("proj_a_bwd", dpa, w_pa_t, "nn", tm, ATT_OUT_W, D_MODEL,
                           [(_sds((S, ATT_OUT_W), F32), slot_tile), (_sds((S, ATT_OUT_W), F32), slot_tile)],
                           epi_doa, extras=[(o_a, slot_tile), (lmix, slot_tile)])
    owed = [_chip_core(_by_owner_cols(gw_pa)), _chip_core(gw_pb.reshape(N_DEV, -1, D_MODEL)),
            _chip_core(gw_out.reshape(N_DEV, -1, D_MODEL)), _chip_core(gw_up), _chip_core(gw_down.reshape(N_DEV, -1, D_MODEL))]
    names = ("w_proj_a", "w_proj_b", "w_out", "w_up", "w_down")

    def epi_dor(acc, ex, out):
        cvs = [slice(h * RET_V, (h + 1) * RET_V) for h in range(RET_HEADS)]
        do, dgr, dg, db = _norm_gate_bwd([acc[:, cv] for cv in cvs], [ex[0][:, cv] for cv in cvs],
                                         [ex[1][:, cv].astype(F32) for cv in cvs],
                                         [ex[2][:, cv] for cv in cvs], [ex[3][:, cv] for cv in cvs])
        for h, cv in enumerate(cvs):
            out[0][:, cv] = do[h].astype(BF16)
            out[1][:, cv] = dgr[h].astype(BF16)
        first = pl.program_id(0) == 0
        _acc_rows(out[2], jnp.concatenate(dg, axis=1), first)
        _acc_rows(out[3], jnp.concatenate(db, axis=1), first)

    wide_tile = _tile_ij(tm, RET_V_W)
    gate_tile = pl.BlockSpec((tm, RET_V_W), lambda i, j, k: (i, R_G // RET_V_W))
    wide_vec = pl.BlockSpec((1, RET_V_W), lambda i, j, k: (0, 0))
    (do_ret, dz_rest, g_gn_g, g_gn_b), got = _mm(
        "proj_b_bwd", dpb, w_pb_t, "nn", tm, RET_V_W, D_MODEL,
        [(_sds((S, RET_V_W), BF16), wide_tile), (_sds((S, REST_W), BF16), gate_tile),
         (_sds((1, RET_V_W), F32), wide_vec), (_sds((1, RET_V_W), F32), wide_vec)],
        epi_dor, extras=[(o_pre, wide_tile), (z_rest, gate_tile), (gn_g, wide_vec), (gn_b, wide_vec)],
        exchange=_pair_exchange(owed), in_place=(dz_rest, 1))
    chip_sums = [_pair_sum(f"pair_sum_{n}", g, r, core, min(256, g.shape[2])) for n, g, r in zip(names, owed, got)]

    dz_rest = _ret_bwd(do_ret, states, z_rest, dz_rest)
    dz_att, gq_parts, gk_parts, parts_late = None, [], [], None
    for gi, ((_, d), nb) in enumerate(zip(ATT_GROUPS, ATT_BLOCKS_PER_STEP)):
        last = gi == len(ATT_GROUPS) - 1
        (dz_att, gq_p, gk_p), parts = _att_bwd_group(z_att, dz_att, do_a, delta_lmix, gq3, gk3, slopes3, gi, d, nb,
                                                     exchange=_chip_exchange(chip_sums) if last else None)
        parts_late = parts if last else parts_late
        gq_parts.append(gq_p)
        gk_parts.append(gk_p)
    g_qn = jnp.concatenate(gq_parts, axis=0).reshape(1, ATT_HEADS, DH)
    g_kn = jnp.concatenate(gk_parts, axis=0).reshape(1, ATT_HEADS, DH)

    gw_att = _mm("gw_in_att", xn, dz_att, "tn", D_MODEL, ATT_W, tk,
                 [(_sds((D_MODEL, 3 * ATT_W), BF16), _tile_ij(D_MODEL, ATT_W))], _epi_store(BF16))[0]
    gw_rest = _mm("gw_in_rest", xn, dz_rest, "tn", D_MODEL, 1024, tk2,
                  [(_sds((D_MODEL, REST_W), BF16), _tile_ij(D_MODEL, 1024))], _epi_store(BF16))[0]
    gw_att = gw_att.reshape(D_MODEL, ATT_HEADS, 3, DH).transpose(0, 2, 1, 3).reshape(D_MODEL, 3 * ATT_W)
    gw_in = jnp.concatenate([gw_att, gw_rest], axis=1)
    owed_in = _chip_core(_by_owner_cols(gw_in))
    (dxn_att,), (got_in,) = _mm("in_proj_att_bwd", dz_att, w_att_t, "nn", tm, D_MODEL, 3 * ATT_W,
                                [(_sds((S, D_MODEL), F32), row_tile)], _epi_store(F32), exchange=_pair_exchange([owed_in]))
    chip_sum_in = _pair_sum("pair_sum_w_in", owed_in, got_in, core, min(256, owed_in.shape[2]))

    def epi_norm1(acc, ex, out):
        dx, dg = _rms_bwd(acc + ex[0][...], ex[1][...], ex[3][...])
        out[0][...] = ex[2][...] + dx
        _acc_rows(out[1], dg, pl.program_id(0) == 0)

    short_tile = _tile_ij(ts, D_MODEL)
    (grad_x, g_norm1), parts_in = _mm(
        "in_proj_rest_bwd", dz_rest, w_rest_t, "nn", ts, D_MODEL, REST_W,
        [(_sds((S, D_MODEL), F32), short_tile), (_sds((1, D_MODEL), F32), vec)],
        epi_norm1, extras=[(dxn_att, short_tile), (x, short_tile), (dx1, short_tile), (norm1_g, vec)],
        exchange=_chip_exchange([chip_sum_in]))

    small = (g_norm1, g_qn, g_kn, g_gn_g, g_gn_b, g_norm2)
    return loss_local, grad_x, list(parts_in) + list(parts_late), small


def _position():
    return lax.axis_index("x"), lax.axis_index("y"), lax.axis_index("c")


def _other_chips(x, y):
    return [(1 - x, y), (x, 1 - y), (1 - x, 1 - y)]


_ANY = pl.BlockSpec(memory_space=pl.ANY)


def _gather_exchange(shards):
    nw = len(shards)
    pieces = range(GATHER_PIECES)

    def plan(x_refs, out_refs, sems):
        send_sems, recv_sems, local_sems = sems
        x, y, c = _position()
        me, sibling = (x, y, c), (x, y, 1 - c)
        chips = _other_chips(x, y)

        def copy(w, k, p, block, to, own=False):
            px, py, pc = block
            n_rows = x_refs[w].shape[0] // GATHER_PIECES
            part = pl.ds(p * n_rows, n_rows)
            rows = out_refs[w].at[4 * px + 2 * py + pc, part]
            sem = (7 * w + k) * GATHER_PIECES + p
            return pltpu.make_async_remote_copy(
                src_ref=x_refs[w].at[part] if own else rows, dst_ref=rows,
                send_sem=send_sems.at[sem], recv_sem=recv_sems.at[sem], device_id=to, device_id_type=MESH)

        def mine(w):
            return pltpu.make_async_copy(x_refs[w], out_refs[w].at[4 * x + 2 * y + c], local_sems.at[w])

        def own_sends(w, p):
            return [copy(w, 0, p, me, sibling, own=True)] + [copy(w, 1 + j, p, me, (*chip, c), own=True)
                                                             for j, chip in enumerate(chips)]

        def start():
            for w in range(nw):
                mine(w).start()
            for p in pieces:
                for w in range(nw):
                    for cp in own_sends(w, p):
                        cp.start()

        def relay():
            for p in pieces:
                for j, chip in enumerate(chips):
                    for w in range(nw):
                        copy(w, 1 + j, p, (*chip, c), me).wait_recv()
                        copy(w, 4 + j, p, (*chip, c), sibling).start()

        def finish():
            for p in pieces:
                for w in range(nw):
                    copy(w, 0, p, sibling, me).wait_recv()
                    for j, chip in enumerate(chips):
                        copy(w, 4 + j, p, (*chip, 1 - c), me).wait_recv()
            for p in pieces:
                for w in range(nw):
                    for cp in own_sends(w, p):
                        cp.wait_send()
                    for j, chip in enumerate(chips):
                        copy(w, 4 + j, p, (*chip, c), sibling).wait_send()
            for w in range(nw):
                mine(w).wait()

        return start, relay, finish

    n_sems = 7 * nw * GATHER_PIECES
    return _Exchange(shards, [_sds((N_DEV,) + s.shape, s.dtype) for s in shards],
                     [pltpu.SemaphoreType.DMA((n_sems,)), pltpu.SemaphoreType.DMA((n_sems,)),
                      pltpu.SemaphoreType.DMA((nw,))], plan)


def _run_exchange(name, exchange):
    n_in, n_out = len(exchange.ins), len(exchange.out_shapes)

    def body(*refs):
        start, relay, finish = exchange.split(refs[:n_in], refs[n_in:n_in + n_out], refs[n_in + n_out:])
        start()
        if relay is not None:
            relay()
        finish()

    return pl.pallas_call(
        body, name=name, out_shape=exchange.out_shapes,
        in_specs=[_ANY] * n_in, out_specs=[_ANY] * n_out, scratch_shapes=exchange.sems,
    )(*exchange.ins)


def _pair_exchange(grads):
    ng = len(grads)

    def plan(g_refs, out_refs, sems):
        send_sems, recv_sems = sems
        x, y, c = _position()

        def copies():
            return [pltpu.make_async_remote_copy(
                src_ref=g_refs[w].at[:, 1 - c], dst_ref=out_refs[w], send_sem=send_sems.at[w],
                recv_sem=recv_sems.at[w], device_id=(x, y, 1 - c), device_id_type=MESH) for w in range(ng)]

        def start():
            for cp in copies():
                cp.start()

        def finish():
            for cp in copies():
                cp.wait()

        return start, None, finish

    return _Exchange(grads, [_sds((N_CHIP,) + g.shape[2:], g.dtype) for g in grads],
                     [pltpu.SemaphoreType.DMA((ng,)), pltpu.SemaphoreType.DMA((ng,))], plan)


def _small_all_gather(small):
    def plan(in_refs, out_refs, sems):
        (s_ref,), (s_out,) = in_refs, out_refs
        send_sems, recv_sems, local_sem = sems
        x, y, c = _position()
        me_id = 4 * x + 2 * y + c
        flips = [(a, b, e) for a in (0, 1) for b in (0, 1) for e in (0, 1)][1:]
        peers = [(x ^ a, y ^ b, c ^ e) for a, b, e in flips]

        def start():
            pltpu.make_async_copy(s_ref, s_out.at[me_id], local_sem).start()
            for k, p in enumerate(peers):
                pltpu.make_async_remote_copy(
                    src_ref=s_ref, dst_ref=s_out.at[me_id], send_sem=send_sems.at[k], recv_sem=recv_sems.at[k],
                    device_id=p, device_id_type=MESH).start()

        def finish():
            for k, (px, py, pc) in enumerate(peers):
                pltpu.make_async_remote_copy(
                    src_ref=s_ref, dst_ref=s_out.at[4 * px + 2 * py + pc], send_sem=send_sems.at[k],
                    recv_sem=recv_sems.at[k], device_id=(px, py, pc), device_id_type=MESH).wait()
            pltpu.make_async_copy(s_ref, s_out.at[me_id], local_sem).wait()

        return start, None, finish

    return _run_exchange("small_grad_all_gather", _Exchange(
        [small], [_sds((N_DEV,) + small.shape, small.dtype)],
        [pltpu.SemaphoreType.DMA((7,)), pltpu.SemaphoreType.DMA((7,)), pltpu.SemaphoreType.DMA], plan))[0]


def _pair_sum(name, g, got, core, tr):
    n_chip, _, R, C = g.shape

    def body(c_ref, a_ref, b_ref, o_ref):
        del c_ref
        o_ref[...] = (a_ref[...].astype(F32) + b_ref[...].astype(F32)).astype(o_ref.dtype)

    return pl.pallas_call(
        body, name=name,
        grid_spec=pltpu.PrefetchScalarGridSpec(
            num_scalar_prefetch=1, grid=(n_chip, R // tr),
            in_specs=[pl.BlockSpec((None, None, tr, C), lambda ch, i, c_ref: (ch, c_ref[0], i, 0)),
                      pl.BlockSpec((None, tr, C), lambda ch, i, c_ref: (ch, i, 0))],
            out_specs=pl.BlockSpec((None, tr, C), lambda ch, i, c_ref: (ch, i, 0))),
        out_shape=_sds(got.shape, got.dtype),
        compiler_params=_cparams(("parallel", "parallel")),
    )(core, g, got)


def _chip_exchange(parts):
    ng = len(parts)

    def plan(p_refs, out_refs, sems):
        send_sems, recv_sems, local_sems = sems
        x, y, c = _position()
        my_chip = 2 * x + y

        def copies():
            local = [pltpu.make_async_copy(p_refs[w].at[my_chip], out_refs[w].at[my_chip], local_sems.at[w])
                     for w in range(ng)]
            remote = [pltpu.make_async_remote_copy(
                src_ref=p_refs[w].at[2 * cx + cy], dst_ref=out_refs[w].at[my_chip],
                send_sem=send_sems.at[3 * w + k], recv_sem=recv_sems.at[3 * w + k],
                device_id=(cx, cy, c), device_id_type=MESH)
                for w in range(ng) for k, (cx, cy) in enumerate(_other_chips(x, y))]
            return local + remote

        def start():
            for cp in copies():
                cp.start()

        def finish():
            for cp in copies():
                cp.wait()

        return start, None, finish

    return _Exchange(parts, [_sds(p.shape, p.dtype) for p in parts],
                     [pltpu.SemaphoreType.DMA((3 * ng,)), pltpu.SemaphoreType.DMA((3 * ng,)),
                      pltpu.SemaphoreType.DMA((ng,))], plan)


def _adamw(name, parts, w, m, v, tr):
    n_parts = parts.shape[0]
    R, C = w.shape

    def body(p_ref, w_ref, m_ref, v_ref, g_ref, d_ref, mo_ref, vo_ref):
        g = p_ref[0].astype(F32)
        for i in range(1, n_parts):
            g = g + p_ref[i].astype(F32)
        m_new = ADAM_B1 * m_ref[...] + (1.0 - ADAM_B1) * g
        v_new = ADAM_B2 * v_ref[...] + (1.0 - ADAM_B2) * (g * g)
        m_hat = m_new / (1.0 - ADAM_B1 ** ADAM_STEP)
        v_hat = v_new / (1.0 - ADAM_B2 ** ADAM_STEP)
        g_ref[...] = g
        d_ref[...] = -ADAM_LR * (m_hat / (jnp.sqrt(v_hat) + ADAM_EPS) + ADAM_WD * w_ref[...])
        mo_ref[...] = m_new
        vo_ref[...] = v_new

    tile = pl.BlockSpec((tr, C), lambda i: (i, 0))
    return pl.pallas_call(
        body, name=name, grid=(R // tr,),
        in_specs=[pl.BlockSpec((n_parts, tr, C), lambda i: (0, i, 0)), tile, tile, tile],
        out_specs=[tile] * 4,
        out_shape=[_sds((R, C), F32)] * 4,
        compiler_params=_cparams(("parallel",)),
    )(parts, w, m, v)


def _flat_small(arrs):
    return jnp.concatenate([a.reshape(-1) for a in arrs]).reshape(SMALL_ROWS, LANES)


def _cols(g):
    return g.transpose(1, 0, 2).reshape(g.shape[1], -1)


def _rows_of(g):
    return g.reshape(-1, g.shape[2])


def _by_owner_cols(g):
    rows, cols = g.shape
    return g.reshape(rows, N_DEV, cols // N_DEV).transpose(1, 0, 2)


def _chip_core(g):
    return g.reshape((N_CHIP, 2) + g.shape[1:])


def kernel(x, norm1_g, w_in, q_norm_g, k_norm_g, ret_gn_g, ret_gn_b, w_proj_a, w_proj_b, w_out, norm2_g, w_up, w_down, loss_target, m_norm1_g, m_w_in, m_q_norm_g, m_k_norm_g, m_ret_gn_g, m_ret_gn_b, m_w_proj_a, m_w_proj_b, m_w_out, m_norm2_g, m_w_up, m_w_down, v_norm1_g, v_w_in, v_q_norm_g, v_k_norm_g, v_ret_gn_g, v_ret_gn_b, v_w_proj_a, v_w_proj_b, v_w_out, v_norm2_g, v_w_up, v_w_down):
    big_w = (w_in, w_proj_a, w_proj_b, w_out, w_up, w_down)
    big_m = (m_w_in, m_w_proj_a, m_w_proj_b, m_w_out, m_w_up, m_w_down)
    big_v = (v_w_in, v_w_proj_a, v_w_proj_b, v_w_out, v_w_up, v_w_down)
    small_w = (norm1_g, q_norm_g, k_norm_g, ret_gn_g, ret_gn_b, norm2_g)
    small_m = (m_norm1_g, m_q_norm_g, m_k_norm_g, m_ret_gn_g, m_ret_gn_b, m_norm2_g)
    small_v = (v_norm1_g, v_q_norm_g, v_k_norm_g, v_ret_gn_g, v_ret_gn_b, v_norm2_g)

    shards = [w[0].astype(BF16) for w in big_w]
    core = lax.axis_index("c").astype(jnp.int32).reshape(1)
    loss_local, grad_x, parts, small_g = _step(
        x[0], loss_target[0], norm1_g, q_norm_g[0], k_norm_g[0], ret_gn_g, ret_gn_b, norm2_g, shards, core)
    small_all = _small_all_gather(_flat_small(small_g))
    names = ("w_in", "w_proj_a", "w_proj_b", "w_out", "w_up", "w_down")

    res = {}
    for n, p, w, m, v in zip(names, parts, big_w, big_m, big_v):
        outs = _adamw(f"adamw_{n}", p, w[0], m[0], v[0], min(128, w.shape[1]))
        res[n] = [o[None] for o in outs]
    s_outs = _adamw("adamw_small", small_all, _flat_small(small_w), _flat_small(small_m), _flat_small(small_v), SMALL_ROWS)
    small_names = ("norm1_g", "q_norm_g", "k_norm_g", "ret_gn_g", "ret_gn_b", "norm2_g")
    for n in small_names:
        res[n] = []
    for o in s_outs:
        flat, off = o.reshape(-1), 0
        for n, w in zip(small_names, small_w):
            res[n].append(flat[off:off + w.size].reshape(w.shape))
            off += w.size

    order = ("norm1_g", "w_in", "q_norm_g", "k_norm_g", "ret_gn_g", "ret_gn_b", "w_proj_a", "w_proj_b", "w_out",
             "norm2_g", "w_up", "w_down")
    loss = lax.psum(loss_local, MESH_AXES)
    return (loss, grad_x[None], *[res[n][0] for n in order], *[res[n][1] for n in order],
            *[res[n][2] for n in order], *[res[n][3] for n in order])
```
